```python
import numpy as np
import jax
import jax.numpy as jnp
from jax import lax

D_MODEL = 1024
BATCH = 2
SEQ = 8192
DEPTH = 2

HEAD_DIM = 64
N_HEADS_MIX = 4
W_MIX = N_HEADS_MIX * HEAD_DIM
N_MIXERS = 4
QBLK = 128
ROPE_THETA = 500000.0
ROPE_DIMS = HEAD_DIM // 4
EPS = 1e-6
NEG = -1e30
BIG = 1e30

NSA_CMP_LEN = 32
NSA_CMP_STRIDE = 16
NSA_CMP_HIDDEN = 2 * HEAD_DIM
NSA_SEL_LEN = 64
NSA_TOPN = 16
NSA_WINDOW = 512

DIL_CONFIGS = ((128, 1), (512, 4), (2048, 16))

FOX_BIAS_INIT = 2.0

D_FF = 2816
N_EXPERTS = 8
TOP_K = 2
N_DENSE_LAYERS = (DEPTH + 1) // 2
N_MOE_LAYERS = DEPTH // 2

IN_SIZES = (W_MIX, 6 * HEAD_DIM, 3 * N_HEADS_MIX, 3 * W_MIX, 3 * W_MIX, 3 * W_MIX, N_HEADS_MIX, N_MIXERS * D_MODEL)
IN_COLS = sum(IN_SIZES)
POS_OFFSET_MAX = 4096

kernel_name = 'hybrid_nsa_dilated_stickbreak_fox_moe'


def rmsnorm(x, g):
    xf = x.astype(jnp.float32)
    y = xf * lax.rsqrt(jnp.mean(xf * xf, axis=-1, keepdims=True) + EPS)
    return (y * g).astype(x.dtype)


def rope_tables(positions):
    inv = ROPE_THETA ** (-jnp.arange(0, ROPE_DIMS, 2, dtype=jnp.float32) / ROPE_DIMS)
    ang = positions.astype(jnp.float32)[..., None] * inv
    return jnp.cos(ang), jnp.sin(ang)


def apply_rope(x, cos, sin):
    if x.ndim == 4:
        cos, sin = cos[:, :, None, :], sin[:, :, None, :]
    half = ROPE_DIMS // 2
    x1, x2, rest = x[..., :half], x[..., half:ROPE_DIMS], x[..., ROPE_DIMS:]
    rot = jnp.concatenate([x1 * cos - x2 * sin, x2 * cos + x1 * sin], axis=-1).astype(x.dtype)
    return jnp.concatenate([rot, rest], axis=-1)


def masked_softmax(s, mask):
    p = jax.nn.softmax(jnp.where(mask, s, NEG), axis=-1)
    return jnp.where(mask, p, 0.0)


def to_blocks(a):
    b, s = a.shape[:2]
    return jnp.moveaxis(a.reshape(b, s // QBLK, QBLK, *a.shape[2:]), 1, 0)


def from_blocks(a):
    a = jnp.moveaxis(a, 0, 1)
    return a.reshape(a.shape[0], -1, *a.shape[3:])


def compress(x, pe, w1, w2):
    b, s, dh = x.shape
    r = NSA_CMP_LEN // NSA_CMP_STRIDE
    ch = x.reshape(b, s // NSA_CMP_STRIDE, NSA_CMP_STRIDE, dh)
    nc = ch.shape[1] - r + 1
    blk = jnp.concatenate([ch[:, i:i + nc] for i in range(r)], axis=2) + pe
    return jax.nn.gelu(blk.reshape(b, nc, NSA_CMP_LEN * dh) @ w1) @ w2


def cmp_to_sel_overlap(nc, n_sel):
    c0 = np.arange(nc) * NSA_CMP_STRIDE
    c1 = c0 + NSA_CMP_LEN
    s0 = np.arange(n_sel) * NSA_SEL_LEN
    s1 = s0 + NSA_SEL_LEN
    return ((c0[:, None] < s1[None, :]) & (c1[:, None] > s0[None, :])).astype(np.float32)


def nsa_attention(q_nr, q_r, kc_raw, vc_raw, ks, vs, kw, vw, gate_logits, pe_k, pe_v, ck1, ck2, cv1, cv2):
    b, s, h, dh = q_r.shape
    scale = dh ** -0.5
    t_idx = jnp.arange(s)
    kc = compress(kc_raw, pe_k, ck1, ck2)
    vc = compress(vc_raw, pe_v, cv1, cv2)
    nc = kc.shape[1]
    c_end = jnp.arange(nc) * NSA_CMP_STRIDE + NSA_CMP_LEN - 1
    c_mask = c_end[None, :] <= t_idx[:, None]
    sc = jnp.einsum('bthd,bcd->bhtc', q_nr, kc).astype(jnp.float32) * scale
    p_cmp = masked_softmax(sc, c_mask)
    o_cmp = jnp.einsum('bhtc,bcd->bthd', p_cmp.astype(vc.dtype), vc)
    n_sel = s // NSA_SEL_LEN
    n_top = min(NSA_TOPN, n_sel)
    overlap = jnp.asarray(cmp_to_sel_overlap(nc, n_sel))
    imp = jnp.einsum('bhtc,cj->btj', p_cmp, overlap)
    j = jnp.arange(n_sel)[None, :]
    cur = (t_idx // NSA_SEL_LEN)[:, None]
    valid = j <= cur
    forced = (j == 0) | (j == cur) | (j == cur - 1)
    score = jnp.where(valid, jnp.where(forced, BIG, imp), NEG)
    _, sel = lax.top_k(score, n_top)
    ks_blk = ks.reshape(b, n_sel, NSA_SEL_LEN, dh)
    vs_blk = vs.reshape(b, n_sel, NSA_SEL_LEN, dh)
    gather = jax.vmap(lambda kb, ib: kb[ib])

    def sel_block(args):
        qb, ib, t0 = args
        t = t0 + jnp.arange(QBLK)
        kg = gather(ks_blk, ib)
        vg = gather(vs_blk, ib)
        sco = jnp.einsum('bqhd,bqnld->bhqnl', qb, kg).astype(jnp.float32) * scale
        kpos = ib[..., None] * NSA_SEL_LEN + jnp.arange(NSA_SEL_LEN)
        m = (kpos <= t[None, :, None, None]).reshape(b, 1, QBLK, -1)
        p = masked_softmax(sco.reshape(b, h, QBLK, -1), m).reshape(b, h, QBLK, n_top, NSA_SEL_LEN)
        return jnp.einsum('bhqnl,bqnld->bqhd', p.astype(vg.dtype), vg)

    starts = jnp.arange(s // QBLK) * QBLK
    o_sel = from_blocks(lax.map(sel_block, (to_blocks(q_r), to_blocks(sel), starts)))
    nb = s // QBLK
    wc = NSA_WINDOW // QBLK

    def band(a):
        ac = jnp.pad(a.reshape(b, nb, QBLK, dh), ((0, 0), (wc, 0), (0, 0), (0, 0)))
        return jnp.concatenate([ac[:, i:i + nb] for i in range(wc + 1)], axis=2)

    kb, vb = band(kw), band(vw)
    qpos = t_idx.reshape(nb, QBLK)
    kpos = jnp.arange(nb)[:, None] * QBLK - NSA_WINDOW + jnp.arange((wc + 1) * QBLK)[None, :]
    diff = qpos[:, :, None] - kpos[:, None, :]
    wmask = (kpos[:, None, :] >= 0) & (diff >= 0) & (diff < NSA_WINDOW)
    sw = jnp.einsum('bnqhd,bnkd->bhnqk', q_r.reshape(b, nb, QBLK, h, dh), kb).astype(jnp.float32) * scale
    pw = masked_softmax(sw, wmask)
    o_win = jnp.einsum('bhnqk,bnkd->bnqhd', pw.astype(vb.dtype), vb).reshape(b, s, h, dh)
    g = jax.nn.sigmoid(gate_logits).reshape(b, s, h, 3, 1)
    return g[:, :, :, 0] * o_cmp + g[:, :, :, 1] * o_sel + g[:, :, :, 2] * o_win


def dilated_attention(q, k, v):
    b, s, h, dh = q.shape
    scale = dh ** -0.5

    def blk(args):
        qb, t0 = args
        t = t0 + jnp.arange(QBLK)
        outs, lses = [], []
        for (w, d) in DIL_CONFIGS:
            m = jnp.arange(w // d + 1)
            idx = t[:, None] - m[None, :] * d
            valid = idx >= 0
            idx = jnp.maximum(idx, 0)
            kg, vg = k[:, idx], v[:, idx]
            sco = jnp.einsum('bqhd,bqmhd->bhqm', qb, kg).astype(jnp.float32) * scale
            sco = jnp.where(valid, sco, NEG)
            lse = jax.nn.logsumexp(sco, axis=-1)
            p = jnp.exp(sco - lse[..., None])
            outs.append(jnp.einsum('bhqm,bqmhd->bqhd', p.astype(vg.dtype), vg))
            lses.append(lse)
        wts = jax.nn.softmax(jnp.stack(lses), axis=0)
        return jnp.einsum('gbhq,gbqhd->bqhd', wts.astype(v.dtype), jnp.stack(outs))

    starts = jnp.arange(s // QBLK) * QBLK
    return from_blocks(lax.map(blk, (to_blocks(q), starts)))


def stick_breaking_attention(q, k, v):
    b, s, h, dh = q.shape
    scale = dh ** -0.5
    s_idx = jnp.arange(s)

    def blk(args):
        qb, t0 = args
        t = t0 + jnp.arange(QBLK)
        z = jnp.einsum('bqhd,bshd->bhqs', qb, k).astype(jnp.float32) * scale
        strict = s_idx[None, :] < t[:, None]
        log_1mb = jnp.where(strict, jax.nn.log_sigmoid(-z), 0.0)
        after = lax.cumsum(log_1mb, axis=3, reverse=True) - log_1mb
        a = jnp.where(strict, jnp.exp(jax.nn.log_sigmoid(z) + after), 0.0)
        return jnp.einsum('bhqs,bshd->bqhd', a.astype(v.dtype), v)

    starts = jnp.arange(s // QBLK) * QBLK
    return from_blocks(lax.map(blk, (to_blocks(q), starts)))


def forgetting_attention(q, k, v, log_f):
    b, s, h, dh = q.shape
    scale = dh ** -0.5
    fc = jnp.cumsum(log_f, axis=1)
    f_k = jnp.transpose(fc, (0, 2, 1))
    s_idx = jnp.arange(s)

    def blk(args):
        qb, fq, t0 = args
        t = t0 + jnp.arange(QBLK)
        sco = jnp.einsum('bqhd,bshd->bhqs', qb, k).astype(jnp.float32) * scale
        sco = sco + jnp.transpose(fq, (0, 2, 1))[..., None] - f_k[:, :, None, :]
        p = masked_softmax(sco, s_idx[None, :] <= t[:, None])
        return jnp.einsum('bhqs,bshd->bqhd', p.astype(v.dtype), v)

    starts = jnp.arange(s // QBLK) * QBLK
    return from_blocks(lax.map(blk, (to_blocks(q), to_blocks(fc), starts)))


def hybrid_mixer(h, cos, sin, w_in, qk_gain, pe_k, pe_v, ck1, ck2, cv1, cv2, fox_b, w_branch, w_out):
    b, s, _ = h.shape
    split_at = [int(v) for v in np.cumsum(IN_SIZES)[:-1]]
    a_q, a_kv, a_g, b_qkv, c_qkv, d_qkv, d_f, merge_logits = jnp.split(h @ w_in, split_at, axis=-1)
    heads = lambda t: t.reshape(b, s, N_HEADS_MIX, HEAD_DIM)
    q_a = rmsnorm(heads(a_q), qk_gain[0])
    kc, vc, ksl, vsl, kw, vw = jnp.split(a_kv, 6, axis=-1)
    o_a = nsa_attention(q_a, apply_rope(q_a, cos, sin), rmsnorm(kc, qk_gain[1]), vc,
                        apply_rope(rmsnorm(ksl, qk_gain[2]), cos, sin), vsl,
                        apply_rope(rmsnorm(kw, qk_gain[3]), cos, sin), vw,
                        a_g, pe_k, pe_v, ck1, ck2, cv1, cv2)
    qb, kb, vb = jnp.split(b_qkv, 3, axis=-1)
    o_b = dilated_attention(apply_rope(rmsnorm(heads(qb), qk_gain[4]), cos, sin),
                            apply_rope(rmsnorm(heads(kb), qk_gain[5]), cos, sin), heads(vb))
    qc, kc2, vc2 = jnp.split(c_qkv, 3, axis=-1)
    o_c = stick_breaking_attention(heads(qc), heads(kc2), heads(vc2))
    qd, kd, vd = jnp.split(d_qkv, 3, axis=-1)
    log_f = jax.nn.log_sigmoid((d_f + fox_b).astype(jnp.float32))
    o_d = forgetting_attention(rmsnorm(heads(qd), qk_gain[6]), rmsnorm(heads(kd), qk_gain[7]), heads(vd), log_f)
    o = jnp.stack([o_a, o_b, o_c, o_d]).reshape(N_MIXERS, b, s, W_MIX)
    y = jnp.einsum('mbsk,mkd->bsmd', o, w_branch)
    gates = jax.nn.sigmoid(merge_logits).reshape(b, s, N_MIXERS, D_MODEL)
    return jnp.sum(gates * y, axis=2) @ w_out


def swiglu(h, w1, w3, w2):
    return (jax.nn.silu(h @ w1) * (h @ w3)) @ w2


def moe_swiglu(h, router_w, w1, w3, w2):
    b, s, d = h.shape
    xt = h.reshape(-1, d)
    logits = (xt @ router_w).astype(jnp.float32)
    top_v, top_i = lax.top_k(logits, TOP_K)
    gate = jax.nn.softmax(top_v, axis=-1)
    e_flat = top_i.reshape(-1)
    order = jnp.argsort(e_flat)
    tok = order // TOP_K
    xs = xt[tok]
    sizes = jax.ops.segment_sum(jnp.ones_like(e_flat), e_flat, num_segments=N_EXPERTS).astype(jnp.int32)
    a = lax.ragged_dot(xs, w1, sizes)
    g = lax.ragged_dot(xs, w3, sizes)
    y = lax.ragged_dot(jax.nn.silu(a) * g, w2, sizes)
    y = y * gate.reshape(-1)[order][:, None].astype(y.dtype)
    return jnp.zeros_like(xt).at[tok].add(y).reshape(b, s, d)


def setup_inputs(seed: int = 0) -> dict:
    key = jax.random.key(seed)
    ks = jax.random.split(key, 28)
    f32 = jnp.float32

    def nrm(k, shape, fan_in, gain=1.0):
        return jax.random.normal(k, shape, f32) * (gain * fan_in ** -0.5)

    def gain(k, shape):
        return 1.0 + 0.02 * jax.random.normal(k, shape, f32)

    cl = NSA_CMP_LEN * HEAD_DIM
    return {
        'x': jax.random.normal(ks[0], (BATCH, SEQ, D_MODEL), f32),
        'c': jax.random.normal(ks[1], (BATCH, D_MODEL), f32),
        'positions': jnp.arange(SEQ, dtype=jnp.int32)[None, :] + jax.random.randint(ks[2], (BATCH, 1), 0, POS_OFFSET_MAX, dtype=jnp.int32),
        'w_ada': nrm(ks[3], (DEPTH, D_MODEL, 6 * D_MODEL), D_MODEL, 0.5),
        'b_ada': 0.02 * jax.random.normal(ks[4], (DEPTH, 6 * D_MODEL), f32),
        'norm_mix': gain(ks[5], (DEPTH, D_MODEL)),
        'norm_ffn': gain(ks[6], (DEPTH, D_MODEL)),
        'w_in': nrm(ks[7], (DEPTH, D_MODEL, IN_COLS), D_MODEL),
        'qk_gain': gain(ks[8], (DEPTH, 8, HEAD_DIM)),
        'nsa_pe_k': 0.02 * jax.random.normal(ks[9], (DEPTH, NSA_CMP_LEN, HEAD_DIM), f32),
        'nsa_pe_v': 0.02 * jax.random.normal(ks[10], (DEPTH, NSA_CMP_LEN, HEAD_DIM), f32),
        'nsa_ck_w1': nrm(ks[11], (DEPTH, cl, NSA_CMP_HIDDEN), cl),
        'nsa_ck_w2': nrm(ks[12], (DEPTH, NSA_CMP_HIDDEN, HEAD_DIM), NSA_CMP_HIDDEN),
        'nsa_cv_w1': nrm(ks[13], (DEPTH, cl, NSA_CMP_HIDDEN), cl),
        'nsa_cv_w2': nrm(ks[14], (DEPTH, NSA_CMP_HIDDEN, HEAD_DIM), NSA_CMP_HIDDEN),
        'fox_bias': FOX_BIAS_INIT + 0.1 * jax.random.normal(ks[15], (DEPTH, N_HEADS_MIX), f32),
        'w_branch': nrm(ks[16], (DEPTH, N_MIXERS, W_MIX, D_MODEL), W_MIX),
        'w_out': nrm(ks[17], (DEPTH, D_MODEL, D_MODEL), D_MODEL),
        'ffn_w1': nrm(ks[18], (N_DENSE_LAYERS, D_MODEL, D_FF), D_MODEL),
        'ffn_w3': nrm(ks[19], (N_DENSE_LAYERS, D_MODEL, D_FF), D_MODEL),
        'ffn_w2': nrm(ks[20], (N_DENSE_LAYERS, D_FF, D_MODEL), D_FF),
        'router_w': nrm(ks[21], (N_MOE_LAYERS, D_MODEL, N_EXPERTS), D_MODEL),
        'moe_w1': nrm(ks[22], (N_MOE_LAYERS, N_EXPERTS, D_MODEL, D_FF), D_MODEL),
        'moe_w3': nrm(ks[23], (N_MOE_LAYERS, N_EXPERTS, D_MODEL, D_FF), D_MODEL),
        'moe_w2': nrm(ks[24], (N_MOE_LAYERS, N_EXPERTS, D_FF, D_MODEL), D_FF),
    }


def reference(x, c, positions, w_ada, b_ada, norm_mix, norm_ffn, w_in, qk_gain, nsa_pe_k, nsa_pe_v,
              nsa_ck_w1, nsa_ck_w2, nsa_cv_w1, nsa_cv_w2, fox_bias, w_branch, w_out,
              ffn_w1, ffn_w3, ffn_w2, router_w, moe_w1, moe_w3, moe_w2):
    cos, sin = rope_tables(positions)
    c_act = jax.nn.silu(c)
    for l in range(DEPTH):
        mod = (c_act @ w_ada[l] + b_ada[l])[:, None, :]
        sh_a, sc_a, g_a, sh_f, sc_f, g_f = jnp.split(mod, 6, axis=-1)
        h = rmsnorm(x, norm_mix[l]) * (1.0 + sc_a) + sh_a
        x = x + g_a * hybrid_mixer(h, cos, sin, w_in[l], qk_gain[l], nsa_pe_k[l], nsa_pe_v[l],
                                   nsa_ck_w1[l], nsa_ck_w2[l], nsa_cv_w1[l], nsa_cv_w2[l],
                                   fox_bias[l], w_branch[l], w_out[l])
        h = rmsnorm(x, norm_ffn[l]) * (1.0 + sc_f) + sh_f
        if l % 2 == 0:
            f = swiglu(h, ffn_w1[l // 2], ffn_w3[l // 2], ffn_w2[l // 2])
        else:
            f = moe_swiglu(h, router_w[l // 2], moe_w1[l // 2], moe_w3[l // 2], moe_w2[l // 2])
        x = x + g_f * f
    return x
```

```python
import functools
import math

import numpy as np
import jax
import jax.numpy as jnp
from jax import lax
from jax.experimental import pallas as pl
from jax.experimental.pallas import tpu as pltpu

F32 = jnp.float32
BF16 = jnp.bfloat16
HI = lax.Precision.HIGHEST

LANES = 128
VMEM_LIMIT = 52 * 1024 * 1024

HEAD_DIM = 64
N_HEADS = 4
W_MIX = N_HEADS * HEAD_DIM
N_MIXERS = 4
ROPE_THETA = 500000.0
ROPE_DIMS = HEAD_DIM // 4
ROPE_HALF = ROPE_DIMS // 2
EPS = 1e-6
NEG = -1e30
BIG = 1e30
CMP_LEN = 32
CMP_STRIDE = 16
SEL_LEN = 64
SEL_SHIFT = 6
TOPN = 16
NSA_WINDOW = 512
DIL_CONFIGS = ((128, 1), (512, 4), (2048, 16))
N_EXPERTS = 8
TOP_K = 2

NN = (((1,), (0,)), ((), ()))
NT = (((1,), (1,)), ((), ()))
TN = (((0,), (0,)), ((), ()))


def _dot(a, b, dims=NN, precision=None):
    return lax.dot_general(a, b, dims, precision=precision, preferred_element_type=F32)


def _dot_split(a, b_bf16, dims=NN):
    hi = a.astype(BF16)
    lo = (a - hi.astype(F32)).astype(BF16)
    return _dot(hi, b_bf16, dims) + _dot(lo, b_bf16, dims)


def _params(*sem):
    return pltpu.CompilerParams(dimension_semantics=sem, vmem_limit_bytes=VMEM_LIMIT)


def _mod_norm(x, g, sc, sh):
    ms = jnp.mean(x * x, axis=-1, keepdims=True)
    return (x * lax.rsqrt(ms + EPS) * g) * (1.0 + sc) + sh


def _ada_kernel(c_ref, w_ref, b_ref, o_ref):
    c = c_ref[...]
    ca = c * jax.nn.sigmoid(c)
    o_ref[0] = _dot(ca, w_ref[0], NN, HI) + b_ref[0]


def _ada(c, w_ada, b_ada):
    depth, d, n6 = w_ada.shape
    b = c.shape[0]
    tn = n6 // 4
    return pl.pallas_call(
        _ada_kernel,
        out_shape=jax.ShapeDtypeStruct((depth, b, n6), F32),
        grid=(depth, n6 // tn),
        in_specs=[
            pl.BlockSpec((b, d), lambda l, j: (0, 0)),
            pl.BlockSpec((1, d, tn), lambda l, j: (l, 0, j)),
            pl.BlockSpec((1, 1, tn), lambda l, j: (l, 0, j)),
        ],
        out_specs=pl.BlockSpec((1, b, tn), lambda l, j: (l, 0, j)),
        compiler_params=_params("arbitrary", "arbitrary"),
        name="ada_mod",
    )(c, w_ada, b_ada.reshape(depth, 1, n6))


_SLABS = (
    (True, None, F32), (True, 0, BF16), (True, 1, F32), (False, None, F32),
    (True, 0, BF16), (True, 0, BF16), (False, None, BF16),
    (False, None, BF16), (False, None, BF16), (False, None, BF16),
    (True, None, BF16), (True, None, BF16), (False, None, BF16),
    (False, None, F32),
)
N_SLABS = len(_SLABS)


def _proj_kernel(x_ref, g_ref, sc_ref, sh_ref, w_ref, gain_ref, bd_ref, rope_ref, *out_refs):
    h = _mod_norm(x_ref[...], g_ref[...], sc_ref[0], sh_ref[0]).astype(BF16)
    bd = bd_ref[...]
    for s, (norm, rope, _) in enumerate(_SLABS):
        y = _dot(h, w_ref[:, s * W_MIX:(s + 1) * W_MIX])
        if norm:
            ms = _dot_split(y * y, bd)
            y = y * lax.rsqrt(ms + EPS)
        y = y * gain_ref[s]
        if rope is not None:
            c = rope_ref[0, rope, 0]
            s1 = rope_ref[0, rope, 1]
            s2 = rope_ref[0, rope, 2]
            y = y * c + pltpu.roll(y, W_MIX - ROPE_HALF, 1) * s1 + pltpu.roll(y, ROPE_HALF, 1) * s2
        out_refs[s][...] = y.astype(out_refs[s].dtype)


def _proj(x2, g, sc, sh, w_slab, gain, bd, rope, seq, tt):
    n, d = x2.shape
    tpb = seq // tt
    out_shape = [jax.ShapeDtypeStruct((n, W_MIX), dt) for (_, _, dt) in _SLABS]
    return pl.pallas_call(
        _proj_kernel,
        out_shape=out_shape,
        grid=(n // tt,),
        in_specs=[
            pl.BlockSpec((tt, d), lambda i: (i, 0)),
            pl.BlockSpec((1, d), lambda i: (0, 0)),
            pl.BlockSpec((1, 1, d), lambda i: (i // tpb, 0, 0)),
            pl.BlockSpec((1, 1, d), lambda i: (i // tpb, 0, 0)),
            pl.BlockSpec((d, N_SLABS * W_MIX), lambda i: (0, 0)),
            pl.BlockSpec((N_SLABS, 1, W_MIX), lambda i: (0, 0, 0)),
            pl.BlockSpec((W_MIX, W_MIX), lambda i: (0, 0)),
            pl.BlockSpec((1, 2, 3, tt, W_MIX), lambda i: (i // tpb, 0, 0, i % tpb, 0)),
        ],
        out_specs=[pl.BlockSpec((tt, W_MIX), lambda i: (i, 0)) for _ in _SLABS],
        compiler_params=_params("arbitrary"),
        name="in_proj",
    )(x2, g, sc, sh, w_slab, gain, bd, rope)


def _pack_w_in(w_in, qk_gain):
    d = w_in.shape[0]
    o = 0
    a_q = w_in[:, o:o + W_MIX]; o += W_MIX
    a_kv = w_in[:, o:o + 6 * HEAD_DIM]; o += 6 * HEAD_DIM
    a_g = w_in[:, o:o + 3 * N_HEADS]; o += 3 * N_HEADS
    b_qkv = w_in[:, o:o + 3 * W_MIX]; o += 3 * W_MIX
    c_qkv = w_in[:, o:o + 3 * W_MIX]; o += 3 * W_MIX
    d_qkv = w_in[:, o:o + 3 * W_MIX]; o += 3 * W_MIX
    d_f = w_in[:, o:o + N_HEADS]; o += N_HEADS
    w_merge = w_in[:, o:]
    kv = [a_kv[:, i * HEAD_DIM:(i + 1) * HEAD_DIM] for i in range(6)]
    z64 = jnp.zeros((d, HEAD_DIM), w_in.dtype)
    misc = jnp.concatenate([a_g, d_f, jnp.zeros((d, W_MIX - 3 * N_HEADS - N_HEADS), w_in.dtype)], axis=1)
    slabs = [
        a_q, a_q,
        jnp.concatenate([kv[0], kv[2], kv[4], z64], axis=1),
        jnp.concatenate([kv[1], kv[3], kv[5], z64], axis=1),
        b_qkv[:, :W_MIX], b_qkv[:, W_MIX:2 * W_MIX], b_qkv[:, 2 * W_MIX:],
        c_qkv[:, :W_MIX], c_qkv[:, W_MIX:2 * W_MIX], c_qkv[:, 2 * W_MIX:],
        d_qkv[:, :W_MIX], d_qkv[:, W_MIX:2 * W_MIX], d_qkv[:, 2 * W_MIX:],
        misc,
    ]
    w_slab = jnp.concatenate(slabs, axis=1).astype(BF16)
    scale = HEAD_DIM ** -0.5
    t4 = lambda gvec: jnp.tile(gvec, N_HEADS)
    one = jnp.ones((W_MIX,), F32)
    gains = [
        t4(qk_gain[0]) * scale, t4(qk_gain[0]) * scale,
        jnp.concatenate([qk_gain[1], qk_gain[2], qk_gain[3], jnp.ones((HEAD_DIM,), F32)]),
        one,
        t4(qk_gain[4]) * scale, t4(qk_gain[5]), one,
        one * scale, one, one,
        t4(qk_gain[6]) * scale, t4(qk_gain[7]), one,
        one,
    ]
    gain = jnp.stack(gains).reshape(N_SLABS, 1, W_MIX).astype(F32)
    return w_slab, gain, w_merge.astype(BF16)


def _rope_tables(positions):
    inv = ROPE_THETA ** (-jnp.arange(0, ROPE_DIMS, 2, dtype=F32) / ROPE_DIMS)
    ang = positions.astype(F32)[..., None] * inv
    cos, sin = jnp.cos(ang), jnp.sin(ang)
    b, s, _ = cos.shape
    pad1 = jnp.ones((b, s, HEAD_DIM - ROPE_DIMS), F32)
    pad0 = jnp.zeros((b, s, HEAD_DIM - ROPE_DIMS), F32)
    z8 = jnp.zeros_like(sin)
    c64 = jnp.concatenate([cos, cos, pad1], axis=-1)
    s1_64 = jnp.concatenate([-sin, z8, pad0], axis=-1)
    s2_64 = jnp.concatenate([z8, sin, pad0], axis=-1)
    heads = lambda t: jnp.tile(t, (1, 1, N_HEADS))
    one64 = jnp.ones((b, s, HEAD_DIM), F32)
    zero64 = jnp.zeros((b, s, HEAD_DIM), F32)
    kvp = lambda t, fill: jnp.concatenate([fill, t, t, fill], axis=-1)
    set_a = jnp.stack([heads(c64), heads(s1_64), heads(s2_64)], axis=1)
    set_b = jnp.stack([kvp(c64, one64), kvp(s1_64, zero64), kvp(s2_64, zero64)], axis=1)
    return jnp.stack([set_a, set_b], axis=1)


def _compress_kernel(a_ref, b_ref, pe_ref, w1_ref, w2_ref, o_ref):
    half = w1_ref.shape[1] // 2
    w1 = w1_ref[0]
    hid = (_dot(a_ref[0, 0], w1[:half], NN, HI) + _dot(b_ref[0, 0], w1[half:], NN, HI)
           + _dot(pe_ref[0], w1, NN, HI))
    o_ref[0, 0] = _dot(jax.nn.gelu(hid), w2_ref[0], NN, HI)


def _compress(ch, chn, pe, w1, w2):
    _, b, ncp, cw = ch.shape
    hid = w1.shape[2]
    return pl.pallas_call(
        _compress_kernel,
        out_shape=jax.ShapeDtypeStruct((2, b, ncp, HEAD_DIM), F32),
        grid=(2, b),
        in_specs=[
            pl.BlockSpec((1, 1, ncp, cw), lambda k, i: (k, i, 0, 0)),
            pl.BlockSpec((1, 1, ncp, cw), lambda k, i: (k, i, 0, 0)),
            pl.BlockSpec((1, 1, 2 * cw), lambda k, i: (k, 0, 0)),
            pl.BlockSpec((1, 2 * cw, hid), lambda k, i: (k, 0, 0)),
            pl.BlockSpec((1, hid, HEAD_DIM), lambda k, i: (k, 0, 0)),
        ],
        out_specs=pl.BlockSpec((1, 1, ncp, HEAD_DIM), lambda k, i: (k, i, 0, 0)),
        compiler_params=_params("arbitrary", "arbitrary"),
        name="nsa_compress",
    )(ch, chn, pe, w1, w2)


def _cmp_kernel(q_ref, kc_ref, vc_ref, ov_ref, o_ref, sel_ref, *, tq, nc, n_sel):
    i = pl.program_id(1)
    rows = N_HEADS * tq
    ncp = kc_ref.shape[1]
    q = q_ref[0]
    s = _dot(q, kc_ref[0], NT, HI)
    r = lax.broadcasted_iota(jnp.int32, (rows, 1), 0)
    t = i * tq + (r & (tq - 1))
    c = lax.broadcasted_iota(jnp.int32, (1, ncp), 1)
    mask = (c * CMP_STRIDE + (CMP_LEN - 1) <= t) & (c < nc)
    sm = jnp.where(mask, s, NEG)
    m = jnp.max(sm, axis=-1, keepdims=True)
    e = jnp.where(mask, jnp.exp(sm - m), 0.0)
    l = jnp.sum(e, axis=-1, keepdims=True)
    p = e / jnp.maximum(l, 1e-30)
    o_ref[0] = _dot(p.astype(BF16), vc_ref[0].astype(BF16)).astype(o_ref.dtype)
    psum = p[0:tq] + p[tq:2 * tq] + p[2 * tq:3 * tq] + p[3 * tq:4 * tq]
    imp = _dot(psum, ov_ref[...], NN, HI)
    tt = i * tq + lax.broadcasted_iota(jnp.int32, (tq, 1), 0)
    j = lax.broadcasted_iota(jnp.int32, (1, n_sel), 1)
    cur = tt >> SEL_SHIFT
    valid = j <= cur
    forced = (j == 0) | (j == cur) | (j == cur - 1)
    score = jnp.where(valid, jnp.where(forced, BIG, imp), NEG)
    sel = jnp.zeros((tq, n_sel), F32)
    jf = j.astype(F32)
    for _ in range(min(TOPN, n_sel)):
        mx = jnp.max(score, axis=-1, keepdims=True)
        idx = jnp.min(jnp.where(score == mx, jf, float(n_sel)), axis=-1, keepdims=True)
        pick = jf == idx
        sel = jnp.where(pick, 1.0, sel)
        score = jnp.where(pick, -3e38, score)
    sel_ref[0] = sel.astype(sel_ref.dtype)


def _cmp_topk(q_st, kc, vc, overlap, seq, tq, nc):
    b = q_st.shape[0]
    ncp = kc.shape[1]
    n_sel = seq // SEL_LEN
    nq = seq // tq
    rows = N_HEADS * tq
    return pl.pallas_call(
        functools.partial(_cmp_kernel, tq=tq, nc=nc, n_sel=n_sel),
        out_shape=[jax.ShapeDtypeStruct((b, nq * rows, HEAD_DIM), BF16),
                   jax.ShapeDtypeStruct((b, seq, n_sel), BF16)],
        grid=(b, nq),
        in_specs=[
            pl.BlockSpec((1, rows, HEAD_DIM), lambda g, i: (g, i, 0)),
            pl.BlockSpec((1, ncp, HEAD_DIM), lambda g, i: (g, 0, 0)),
            pl.BlockSpec((1, ncp, HEAD_DIM), lambda g, i: (g, 0, 0)),
            pl.BlockSpec((ncp, n_sel), lambda g, i: (0, 0)),
        ],
        out_specs=[pl.BlockSpec((1, rows, HEAD_DIM), lambda g, i: (g, i, 0)),
                   pl.BlockSpec((1, tq, n_sel), lambda g, i: (g, i, 0))],
        compiler_params=_params("arbitrary", "arbitrary"),
        name="nsa_cmp_topk",
    )(q_st, kc, vc, overlap)


def _flash_kernel(*refs, r_heads, tq, tk, window, has_sel, has_bias, stats):
    it = iter(refs)
    q_ref, k_ref, v_ref = next(it), next(it), next(it)
    sel_ref = next(it) if has_sel else None
    fq_ref = next(it) if has_bias else None
    fk_ref = next(it) if has_bias else None
    o_ref, m_scr, l_scr, acc_scr = next(it), next(it), next(it), next(it)
    i = pl.program_id(1)
    t0 = i * tq
    m_scr[...] = jnp.full(m_scr.shape, NEG, F32)
    l_scr[...] = jnp.zeros(l_scr.shape, F32)
    acc_scr[...] = jnp.zeros(acc_scr.shape, F32)
    t_pos = t0 + lax.broadcasted_iota(jnp.int32, (tq, 1), 0)
    hi_tile = (t0 + tq - 1) // tk
    lo_tile = jnp.maximum(t0 - (window - 1), 0) // tk if window else 0

    def body(kt, carry):
        k_t = k_ref[0, kt]
        v_t = v_ref[0, kt]
        s_pos = kt * tk + lax.broadcasted_iota(jnp.int32, (1, tk), 1)
        mask = s_pos <= t_pos
        if window:
            mask = mask & (t_pos - s_pos < window)
        if has_sel:
            blk = lax.broadcasted_iota(jnp.int32, (sel_ref.shape[2], 1), 0)
            expand = jnp.where((s_pos >> SEL_SHIFT) == blk, 1.0, 0.0).astype(BF16)
            mask = mask & (_dot(sel_ref[0], expand) > 0.5)
        for r in range(r_heads):
            rs = slice(r * tq, (r + 1) * tq)
            s = _dot(q_ref[0, rs, :], k_t, NT)
            if has_bias:
                s = s + fq_ref[0] - fk_ref[0, kt]
            s = jnp.where(mask, s, NEG)
            m_old = m_scr[rs, :]
            m_new = jnp.maximum(m_old, jnp.max(s, axis=-1, keepdims=True))
            p = jnp.exp(s - m_new)
            alpha = jnp.exp(m_old - m_new)
            m_scr[rs, :] = m_new
            if not stats:
                l_scr[rs, :] = alpha * l_scr[rs, :] + jnp.sum(p, axis=-1, keepdims=True)
            acc_scr[rs, :] = alpha * acc_scr[rs, :] + _dot(p.astype(BF16), v_t)
        return carry

    lax.fori_loop(lo_tile, hi_tile + 1, body, 0)
    if stats:
        lane = lax.broadcasted_iota(jnp.int32, acc_scr.shape, 1)
        o_ref[0] = jnp.where(lane == HEAD_DIM + 1, m_scr[...], acc_scr[...])
    else:
        o_ref[0] = (acc_scr[...] / l_scr[...]).astype(o_ref.dtype)


def _flash(q, k, v, *, r_heads, tq, tk, window=0, sel=None, fq=None, fk=None, stats=False):
    g, qrows, _ = q.shape
    rows = r_heads * tq
    nq = qrows // rows
    sk = k.shape[1]
    vw = v.shape[2]
    k4 = k.reshape(g, sk // tk, tk, HEAD_DIM)
    v4 = v.reshape(g, sk // tk, tk, vw)
    args = [q, k4, v4]
    in_specs = [
        pl.BlockSpec((1, rows, HEAD_DIM), lambda a, i: (a, i, 0)),
        pl.BlockSpec((1, sk // tk, tk, HEAD_DIM), lambda a, i: (a, 0, 0, 0)),
        pl.BlockSpec((1, sk // tk, tk, vw), lambda a, i: (a, 0, 0, 0)),
    ]
    if sel is not None:
        args.append(sel)
        in_specs.append(pl.BlockSpec((1, tq, sel.shape[2]), lambda a, i: (a, i, 0)))
    if fq is not None:
        args += [fq, fk.reshape(g, sk // tk, 1, tk)]
        in_specs += [pl.BlockSpec((1, tq, 1), lambda a, i: (a, i, 0)),
                     pl.BlockSpec((1, sk // tk, 1, tk), lambda a, i: (a, 0, 0, 0))]
    out_dtype = F32 if stats else BF16
    return pl.pallas_call(
        functools.partial(_flash_kernel, r_heads=r_heads, tq=tq, tk=tk, window=window,
                          has_sel=sel is not None, has_bias=fq is not None, stats=stats),
        out_shape=jax.ShapeDtypeStruct((g, qrows, vw), out_dtype),
        grid=(g, nq),
        in_specs=in_specs,
        out_specs=pl.BlockSpec((1, rows, vw), lambda a, i: (a, i, 0)),
        scratch_shapes=[pltpu.VMEM((rows, 1), F32), pltpu.VMEM((rows, 1), F32),
                        pltpu.VMEM((rows, vw), F32)],
        compiler_params=_params("arbitrary", "arbitrary"),
        name="flash_w%d_s%d_b%d_t%d" % (window, sel is not None, fq is not None, stats),
    )(*args)


def _dil_combine_kernel(a_ref, b_ref, c_ref, o_ref):
    blk = HEAD_DIM * 2
    for h in range(N_HEADS):
        parts = [r[:, h * blk:(h + 1) * blk] for r in (a_ref, b_ref, c_ref)]
        ms = [p[:, HEAD_DIM + 1:HEAD_DIM + 2] for p in parts]
        mx = jnp.maximum(jnp.maximum(ms[0], ms[1]), ms[2])
        ws = [jnp.exp(m - mx) for m in ms]
        tot = ws[0] * parts[0] + ws[1] * parts[1] + ws[2] * parts[2]
        den = tot[:, HEAD_DIM:HEAD_DIM + 1]
        lane = lax.broadcasted_iota(jnp.int32, tot.shape, 1)
        o_ref[:, h * blk:(h + 1) * blk] = jnp.where(lane < HEAD_DIM, tot / den, 0.0).astype(o_ref.dtype)


def _dil_combine(parts, tt):
    n, w = parts[0].shape
    return pl.pallas_call(
        _dil_combine_kernel,
        out_shape=jax.ShapeDtypeStruct((n, w), BF16),
        grid=(n // tt,),
        in_specs=[pl.BlockSpec((tt, w), lambda i: (i, 0))] * 3,
        out_specs=pl.BlockSpec((tt, w), lambda i: (i, 0)),
        compiler_params=_params("arbitrary"),
        name="dil_combine",
    )(*parts)


def _sb_kernel(q_ref, k_ref, v_ref, tri_ref, o_ref, carry_scr, acc_scr, *, tq, tk):
    i = pl.program_id(1)
    t0 = i * tq
    q = q_ref[0]
    tri = tri_ref[...]
    carry_scr[...] = jnp.zeros(carry_scr.shape, F32)
    acc_scr[...] = jnp.zeros(acc_scr.shape, F32)
    t_pos = t0 + lax.broadcasted_iota(jnp.int32, (tq, 1), 0)

    def tile(kt, masked):
        z = _dot(q, k_ref[0, kt], NT)
        lg = -(jnp.maximum(z, 0.0) + jnp.log(1.0 + jnp.exp(-jnp.abs(z))))
        if masked:
            s_pos = kt * tk + lax.broadcasted_iota(jnp.int32, (1, tk), 1)
            strict = s_pos < t_pos
            lg = jnp.where(strict, lg, 0.0)
        cum = _dot_split(lg, tri) + carry_scr[...]
        a = jnp.exp(z + cum)
        if masked:
            a = jnp.where(strict, a, 0.0)
        acc_scr[...] += _dot(a.astype(BF16), v_ref[0, kt])
        carry_scr[...] += jnp.sum(lg, axis=-1, keepdims=True)

    n_diag = tq // tk
    hi_tile = (t0 + tq - 1) // tk
    for d in range(n_diag):
        tile(hi_tile - d, True)

    def body(j, carry):
        tile(hi_tile - n_diag - j, False)
        return carry

    lax.fori_loop(0, hi_tile - n_diag + 1, body, 0)
    o_ref[0] = acc_scr[...].astype(o_ref.dtype)


def _stick_breaking(q, k, v, tq, tk):
    g, s, _ = q.shape
    tri = jnp.asarray(np.tril(np.ones((tk, tk), np.float32)), BF16)
    k4 = k.reshape(g, s // tk, tk, HEAD_DIM)
    v4 = v.reshape(g, s // tk, tk, HEAD_DIM)
    return pl.pallas_call(
        functools.partial(_sb_kernel, tq=tq, tk=tk),
        out_shape=jax.ShapeDtypeStruct((g, s, HEAD_DIM), BF16),
        grid=(g, s // tq),
        in_specs=[
            pl.BlockSpec((1, tq, HEAD_DIM), lambda a, i: (a, i, 0)),
            pl.BlockSpec((1, s // tk, tk, HEAD_DIM), lambda a, i: (a, 0, 0, 0)),
            pl.BlockSpec((1, s // tk, tk, HEAD_DIM), lambda a, i: (a, 0, 0, 0)),
            pl.BlockSpec((tk, tk), lambda a, i: (0, 0)),
        ],
        out_specs=pl.BlockSpec((1, tq, HEAD_DIM), lambda a, i: (a, i, 0)),
        scratch_shapes=[pltpu.VMEM((tq, 1), F32), pltpu.VMEM((tq, HEAD_DIM), F32)],
        compiler_params=_params("arbitrary", "arbitrary"),
        name="stick_breaking",
    )(q, k4, v4, tri)


def _foxcum_kernel(x_ref, b_ref, tri_ref, o_ref, carry_scr):
    @pl.when(pl.program_id(1) == 0)
    def _():
        carry_scr[...] = jnp.zeros(carry_scr.shape, F32)

    z = x_ref[0] + b_ref[...]
    logf = jnp.minimum(z, 0.0) - jnp.log(1.0 + jnp.exp(-jnp.abs(z)))
    cum = _dot(tri_ref[...], logf, NN, HI) + carry_scr[...]
    o_ref[0] = cum
    carry_scr[...] = cum[cum.shape[0] - 1:, :]


def _fox_cumsum(misc, bias_vec, tc):
    b, s, w = misc.shape
    tri = jnp.asarray(np.tril(np.ones((tc, tc), np.float32)))
    return pl.pallas_call(
        _foxcum_kernel,
        out_shape=jax.ShapeDtypeStruct((b, s, w), F32),
        grid=(b, s // tc),
        in_specs=[
            pl.BlockSpec((1, tc, w), lambda a, i: (a, i, 0)),
            pl.BlockSpec((1, w), lambda a, i: (0, 0)),
            pl.BlockSpec((tc, tc), lambda a, i: (0, 0)),
        ],
        out_specs=pl.BlockSpec((1, tc, w), lambda a, i: (a, i, 0)),
        scratch_shapes=[pltpu.VMEM((1, w), F32)],
        compiler_params=_params("arbitrary", "arbitrary"),
        name="fox_cumsum",
    )(misc, bias_vec, tri)


def _merge_kernel(x_ref, g_ref, sc_ref, sh_ref, ga_ref, wm_ref, misc_ref, pg_ref,
                  ocmp_ref, osel_ref, owin_ref, ob_ref, oc_ref, od_ref,
                  wa_ref, wb_ref, wc_ref, wd_ref, wo_ref, o_ref):
    x = x_ref[...]
    d = x.shape[1]
    h = _mod_norm(x, g_ref[...], sc_ref[0], sh_ref[0]).astype(BF16)
    gate = jax.nn.sigmoid(misc_ref[...])
    o_a = (_dot_split(gate, pg_ref[0]) * ocmp_ref[...].astype(F32)
           + _dot_split(gate, pg_ref[1]) * osel_ref[...].astype(F32)
           + _dot_split(gate, pg_ref[2]) * owin_ref[...].astype(F32)).astype(BF16)
    mixed = jnp.zeros(x.shape, F32)
    for m, (o_m, w_ref) in enumerate(((o_a, wa_ref), (ob_ref[...], wb_ref),
                                      (oc_ref[...], wc_ref), (od_ref[...], wd_ref))):
        y = _dot(o_m, w_ref[...])
        gl = _dot(h, wm_ref[:, m * d:(m + 1) * d])
        mixed = mixed + jax.nn.sigmoid(gl) * y
    o_ref[...] = x + ga_ref[0] * _dot(mixed.astype(BF16), wo_ref[...])


def _merge(x2, g, sc, sh, ga, w_merge, misc, pg, o_cmp, o_sel, o_win, o_b, o_c, o_d,
           wa, wb, wc, wd, wo, seq, tt):
    n, d = x2.shape
    tpb = seq // tt
    row = lambda w: pl.BlockSpec((tt, w), lambda i: (i, 0))
    full = lambda a: pl.BlockSpec(a.shape, lambda i: (0,) * a.ndim)
    per_b = pl.BlockSpec((1, 1, d), lambda i: (i // tpb, 0, 0))
    return pl.pallas_call(
        _merge_kernel,
        out_shape=jax.ShapeDtypeStruct((n, d), F32),
        grid=(n // tt,),
        in_specs=[row(d), full(g), per_b, per_b, per_b, full(w_merge), row(W_MIX), full(pg),
                  row(W_MIX), row(W_MIX), row(W_MIX), row(o_b.shape[1]), row(W_MIX), row(W_MIX),
                  full(wa), full(wb), full(wc), full(wd), full(wo)],
        out_specs=row(d),
        compiler_params=_params("arbitrary"),
        name="merge_out",
    )(x2, g, sc, sh, ga, w_merge, misc, pg, o_cmp, o_sel, o_win, o_b, o_c, o_d, wa, wb, wc, wd, wo)


def _ffn_kernel(x_ref, g_ref, sc_ref, sh_ref, gf_ref, w1_ref, w3_ref, w2_ref, o_ref, h_scr, acc_scr):
    f = pl.program_id(1)

    @pl.when(f == 0)
    def _():
        h_scr[...] = _mod_norm(x_ref[...], g_ref[...], sc_ref[0], sh_ref[0]).astype(BF16)
        acc_scr[...] = jnp.zeros(acc_scr.shape, F32)

    h = h_scr[...]
    a = _dot(h, w1_ref[...])
    b = _dot(h, w3_ref[...])
    acc_scr[...] += _dot((a * jax.nn.sigmoid(a) * b).astype(BF16), w2_ref[...])

    @pl.when(f == pl.num_programs(1) - 1)
    def _():
        o_ref[...] = x_ref[...] + gf_ref[0] * acc_scr[...]


def _ffn(x2, g, sc, sh, gf, w1, w3, w2, seq, tt, tf):
    n, d = x2.shape
    dff = w1.shape[1]
    tpb = seq // tt
    per_b = pl.BlockSpec((1, 1, d), lambda i, f: (i // tpb, 0, 0))
    return pl.pallas_call(
        _ffn_kernel,
        out_shape=jax.ShapeDtypeStruct((n, d), F32),
        grid=(n // tt, dff // tf),
        in_specs=[
            pl.BlockSpec((tt, d), lambda i, f: (i, 0)),
            pl.BlockSpec((1, d), lambda i, f: (0, 0)),
            per_b, per_b, per_b,
            pl.BlockSpec((d, tf), lambda i, f: (0, f)),
            pl.BlockSpec((d, tf), lambda i, f: (0, f)),
            pl.BlockSpec((tf, d), lambda i, f: (f, 0)),
        ],
        out_specs=pl.BlockSpec((tt, d), lambda i, f: (i, 0)),
        scratch_shapes=[pltpu.VMEM((tt, d), BF16), pltpu.VMEM((tt, d), F32)],
        compiler_params=_params("arbitrary", "arbitrary"),
        name="ffn_swiglu",
    )(x2, g, sc, sh, gf, w1, w3, w2)


def _route_kernel(x_ref, g_ref, sc_ref, sh_ref, rw_ref, up_ref, h_ref, rank_ref, gate_ref, cnt_ref):
    hf = _mod_norm(x_ref[...], g_ref[...], sc_ref[0], sh_ref[0])
    h_ref[...] = hf.astype(BF16)
    logits = _dot(rw_ref[...], hf, NT, HI)
    ne, tt = logits.shape
    e_idx = lax.broadcasted_iota(jnp.int32, (ne, 1), 0).astype(F32)
    v1 = jnp.max(logits, axis=0, keepdims=True)
    i1 = jnp.min(jnp.where(logits == v1, e_idx, float(ne)), axis=0, keepdims=True)
    m1 = e_idx == i1
    rest = jnp.where(m1, -3e38, logits)
    v2 = jnp.max(rest, axis=0, keepdims=True)
    i2 = jnp.min(jnp.where(rest == v2, e_idx, float(ne)), axis=0, keepdims=True)
    m2 = e_idx == i2
    e2 = jnp.exp(v2 - v1)
    g1 = 1.0 / (1.0 + e2)
    g2 = e2 / (1.0 + e2)
    routed = m1 | m2
    rf = jnp.where(routed, 1.0, 0.0)
    rank = _dot(rf.astype(BF16), up_ref[...])
    rank = jnp.where(routed, rank, -1.0)
    gate = jnp.where(m1, g1, 0.0) + jnp.where(m2, g2, 0.0)
    for e in range(ne):
        rank_ref[0, e] = rank[e:e + 1, :]
        gate_ref[0, e] = gate[e:e + 1, :]
    cnt = jnp.sum(rf, axis=1, keepdims=True)
    cnt_ref[0] = jnp.broadcast_to(cnt, (ne, LANES))


def _route(x2, g, sc, sh, rw_t, seq, tt):
    n, d = x2.shape
    ne = rw_t.shape[0]
    tpb = seq // tt
    nt = n // tt
    upper = jnp.asarray(np.triu(np.ones((tt, tt), np.float32), 1), BF16)
    per_b = pl.BlockSpec((1, 1, d), lambda i: (i // tpb, 0, 0))
    return pl.pallas_call(
        _route_kernel,
        out_shape=[jax.ShapeDtypeStruct((n, d), BF16),
                   jax.ShapeDtypeStruct((nt, ne, 1, tt), F32),
                   jax.ShapeDtypeStruct((nt, ne, 1, tt), F32),
                   jax.ShapeDtypeStruct((nt, ne, LANES), F32)],
        grid=(nt,),
        in_specs=[
            pl.BlockSpec((tt, d), lambda i: (i, 0)),
            pl.BlockSpec((1, d), lambda i: (0, 0)),
            per_b, per_b,
            pl.BlockSpec((ne, d), lambda i: (0, 0)),
            pl.BlockSpec((tt, tt), lambda i: (0, 0)),
        ],
        out_specs=[pl.BlockSpec((tt, d), lambda i: (i, 0)),
                   pl.BlockSpec((1, ne, 1, tt), lambda i: (i, 0, 0, 0)),
                   pl.BlockSpec((1, ne, 1, tt), lambda i: (i, 0, 0, 0)),
                   pl.BlockSpec((1, ne, LANES), lambda i: (i, 0, 0))],
        compiler_params=_params("arbitrary"),
        name="moe_route",
    )(x2, g, sc, sh, rw_t, upper)


def _moe_kernel(cnt_ref, x_ref, gf_ref, h_ref, rank_ref, gate_ref, w1_ref, w3_ref, w2_ref,
                o_ref, acc_scr, *, chunk):
    i, e, f = pl.program_id(0), pl.program_id(1), pl.program_id(2)
    ne, nf = pl.num_programs(1), pl.num_programs(2)

    @pl.when((e == 0) & (f == 0))
    def _():
        acc_scr[...] = jnp.zeros(acc_scr.shape, F32)

    count = cnt_ref[i * ne + e]
    rank = rank_ref[0, 0]
    gate = gate_ref[0, 0]
    h = h_ref[...]

    def body(c, carry):
        r = c * chunk + lax.broadcasted_iota(jnp.int32, (chunk, 1), 0)
        hit = rank == r.astype(F32)
        p = jnp.where(hit, 1.0, 0.0).astype(BF16)
        xs = _dot(p, h).astype(BF16)
        a = _dot(xs, w1_ref[0])
        b = _dot(xs, w3_ref[0])
        y = _dot((a * jax.nn.sigmoid(a) * b).astype(BF16), w2_ref[0])
        gcol = jnp.sum(jnp.where(hit, gate, 0.0), axis=-1, keepdims=True)
        acc_scr[...] += _dot(p, (y * gcol).astype(BF16), TN)
        return carry

    lax.fori_loop(0, (count + chunk - 1) // chunk, body, 0)

    @pl.when((e == ne - 1) & (f == nf - 1))
    def _():
        o_ref[...] = x_ref[...] + gf_ref[0] * acc_scr[...]


def _moe(counts, x2, gf, h2, rank, gate, w1, w3, w2, seq, tt, tf, chunk):
    n, d = x2.shape
    ne, _, dff = w1.shape
    tpb = seq // tt
    grid_spec = pltpu.PrefetchScalarGridSpec(
        num_scalar_prefetch=1,
        grid=(n // tt, ne, dff // tf),
        in_specs=[
            pl.BlockSpec((tt, d), lambda i, e, f, c: (i, 0)),
            pl.BlockSpec((1, 1, d), lambda i, e, f, c: (i // tpb, 0, 0)),
            pl.BlockSpec((tt, d), lambda i, e, f, c: (i, 0)),
            pl.BlockSpec((1, 1, 1, tt), lambda i, e, f, c: (i, e, 0, 0)),
            pl.BlockSpec((1, 1, 1, tt), lambda i, e, f, c: (i, e, 0, 0)),
            pl.BlockSpec((1, d, tf), lambda i, e, f, c: (e, 0, f)),
            pl.BlockSpec((1, d, tf), lambda i, e, f, c: (e, 0, f)),
            pl.BlockSpec((1, tf, d), lambda i, e, f, c: (e, f, 0)),
        ],
        out_specs=pl.BlockSpec((tt, d), lambda i, e, f, c: (i, 0)),
        scratch_shapes=[pltpu.VMEM((tt, d), F32)],
    )
    return pl.pallas_call(
        functools.partial(_moe_kernel, chunk=chunk),
        out_shape=jax.ShapeDtypeStruct((n, d), F32),
        grid_spec=grid_spec,
        compiler_params=_params("arbitrary", "arbitrary", "arbitrary"),
        name="moe_experts",
    )(counts, x2, gf, h2, rank, gate, w1, w3, w2)


def _stack_heads(a, b, s, tq):
    a = a.reshape(b, s // tq, tq, N_HEADS, HEAD_DIM).transpose(0, 1, 3, 2, 4)
    return a.reshape(b, (s // tq) * N_HEADS * tq, HEAD_DIM)


def _unstack_heads(a, b, s, tq):
    a = a.reshape(b, s // tq, N_HEADS, tq, HEAD_DIM).transpose(0, 1, 3, 2, 4)
    return a.reshape(b * s, W_MIX)


def _to_heads(a, b, s):
    return a.reshape(b, s, N_HEADS, HEAD_DIM).transpose(0, 2, 1, 3).reshape(b * N_HEADS, s, HEAD_DIM)


def _from_heads(a, b, s):
    w = a.shape[-1]
    return a.reshape(b, N_HEADS, s, w).transpose(0, 2, 1, 3).reshape(b * s, N_HEADS * w)


def _deinterleave(a, b, s, dil):
    a = a.reshape(b, s // dil, dil, N_HEADS, HEAD_DIM).transpose(0, 3, 2, 1, 4)
    return a.reshape(b * N_HEADS * dil, s // dil, HEAD_DIM)


def _reinterleave(a, b, s, dil):
    w = a.shape[-1]
    a = a.reshape(b, N_HEADS, dil, s // dil, w).transpose(0, 3, 2, 1, 4)
    return a.reshape(b * s, N_HEADS * w)


def _overlap_matrix(ncp, nc, n_sel):
    c0 = np.arange(ncp) * CMP_STRIDE
    c1 = c0 + CMP_LEN
    s0 = np.arange(n_sel) * SEL_LEN
    s1 = s0 + SEL_LEN
    ov = ((c0[:, None] < s1[None, :]) & (c1[:, None] > s0[None, :])).astype(np.float32)
    ov[nc:] = 0.0
    return jnp.asarray(ov)


def _gate_expand():
    pg = np.zeros((3, W_MIX, W_MIX), np.float32)
    for br in range(3):
        for h in range(N_HEADS):
            pg[br, 3 * h + br, h * HEAD_DIM:(h + 1) * HEAD_DIM] = 1.0
    return jnp.asarray(pg, BF16)


def _pad_branch_w(w):
    d = w.shape[1]
    w4 = w.reshape(N_HEADS, HEAD_DIM, d)
    return jnp.concatenate([w4, jnp.zeros_like(w4)], axis=1).reshape(2 * W_MIX, d)


def _mixer_layer(x2, b, s, mod, norm_g, rope, w_in, qk_gain, pe_k, pe_v, ck1, ck2, cv1, cv2,
                 fox_b, w_branch, w_out):
    n, d = x2.shape
    sh_a, sc_a, g_a = mod[0], mod[1], mod[2]
    w_slab, gain, w_merge = _pack_w_in(w_in, qk_gain)
    bd = jnp.asarray(np.kron(np.eye(N_HEADS), np.full((HEAD_DIM, HEAD_DIM), 1.0 / HEAD_DIM)), BF16)
    tt = min(512, s)
    sl = _proj(x2, norm_g, sc_a, sh_a, w_slab, gain, bd, rope, s, tt)
    q_nr, q_r, kslab, vslab = sl[0], sl[1], sl[2], sl[3]
    misc = sl[13]

    kslab3 = kslab.reshape(b, s, W_MIX)
    vslab3 = vslab.reshape(b, s, W_MIX)
    kc_raw, ksl, kw = (kslab3[..., i * HEAD_DIM:(i + 1) * HEAD_DIM] for i in range(3))
    vc_raw, vsl, vw = (vslab3[..., i * HEAD_DIM:(i + 1) * HEAD_DIM] for i in range(3))
    nch = s // CMP_STRIDE
    nc = nch - CMP_LEN // CMP_STRIDE + 1
    chunks = jnp.stack([kc_raw, vc_raw]).reshape(2, b, nch, CMP_STRIDE * HEAD_DIM)
    chunks_next = jnp.concatenate([chunks[:, :, 1:], jnp.zeros_like(chunks[:, :, :1])], axis=2)
    pe = jnp.stack([pe_k, pe_v]).reshape(2, 1, CMP_LEN * HEAD_DIM)
    kvc = _compress(chunks, chunks_next, pe, jnp.stack([ck1, cv1]), jnp.stack([ck2, cv2]))
    tq_c = min(128, s)
    overlap = _overlap_matrix(nch, nc, s // SEL_LEN)
    o_cmp_st, selmask = _cmp_topk(_stack_heads(q_nr, b, s, tq_c), kvc[0], kvc[1], overlap, s, tq_c, nc)
    o_cmp = _unstack_heads(o_cmp_st, b, s, tq_c)
    tq_n = min(256, s)
    tk_n = min(512, s)
    q_st = _stack_heads(q_r, b, s, tq_n)
    o_sel = _unstack_heads(
        _flash(q_st, ksl.astype(BF16), vsl.astype(BF16), r_heads=N_HEADS, tq=tq_n, tk=tk_n, sel=selmask),
        b, s, tq_n)
    o_win = _unstack_heads(
        _flash(q_st, kw.astype(BF16), vw.astype(BF16), r_heads=N_HEADS, tq=tq_n, tk=tk_n, window=NSA_WINDOW),
        b, s, tq_n)

    parts = []
    for (wdw, dil) in DIL_CONFIGS:
        ls = s // dil
        td = min(256, ls)
        qd_ = _deinterleave(sl[4], b, s, dil)
        kd_ = _deinterleave(sl[5], b, s, dil)
        vd_ = _deinterleave(sl[6], b, s, dil)
        ones = jnp.ones(vd_.shape[:2] + (1,), BF16)
        v_ext = jnp.concatenate([vd_, ones, jnp.zeros(vd_.shape[:2] + (HEAD_DIM - 1,), BF16)], axis=-1)
        st = _flash(qd_, kd_, v_ext, r_heads=1, tq=td, tk=td, window=wdw // dil + 1, stats=True)
        parts.append(_reinterleave(st, b, s, dil))
    o_b = _dil_combine(parts, min(512, s))

    tq_s = min(256, s)
    o_c = _from_heads(_stick_breaking(_to_heads(sl[7], b, s), _to_heads(sl[8], b, s),
                                      _to_heads(sl[9], b, s), tq_s, tq_s), b, s)

    bias_vec = jnp.zeros((1, W_MIX), F32).at[0, 3 * N_HEADS:3 * N_HEADS + N_HEADS].set(fox_b)
    fcum = _fox_cumsum(misc.reshape(b, s, W_MIX), bias_vec, min(512, s))
    fc = fcum[..., 3 * N_HEADS:3 * N_HEADS + N_HEADS]
    fch = fc.transpose(0, 2, 1).reshape(b * N_HEADS, s)
    tq_f = min(256, s)
    tk_f = min(512, s)
    o_d = _from_heads(_flash(_to_heads(sl[10], b, s), _to_heads(sl[11], b, s), _to_heads(sl[12], b, s),
                             r_heads=1, tq=tq_f, tk=tk_f, fq=fch[..., None], fk=fch), b, s)

    wb16 = w_branch.astype(BF16)
    return _merge(x2, norm_g, sc_a, sh_a, g_a, w_merge, misc, _gate_expand(),
                  o_cmp, o_sel, o_win, o_b, o_c, o_d,
                  wb16[0], _pad_branch_w(wb16[1]), wb16[2], wb16[3], w_out.astype(BF16), s, min(512, s))


def kernel(x, c, positions, w_ada, b_ada, norm_mix, norm_ffn, w_in, qk_gain, nsa_pe_k, nsa_pe_v,
           nsa_ck_w1, nsa_ck_w2, nsa_cv_w1, nsa_cv_w2, fox_bias, w_branch, w_out,
           ffn_w1, ffn_w3, ffn_w2, router_w, moe_w1, moe_w3, moe_w2):
    b, s, d = x.shape
    depth = w_ada.shape[0]
    rope = _rope_tables(positions)
    mods = _ada(c, w_ada, b_ada).reshape(depth, b, 6, 1, d).transpose(0, 2, 1, 3, 4)
    x2 = x.reshape(b * s, d)
    for l in range(depth):
        mod = mods[l]
        x2 = _mixer_layer(x2, b, s, mod[0:3], norm_mix[l].reshape(1, d), rope, w_in[l], qk_gain[l],
                          nsa_pe_k[l], nsa_pe_v[l], nsa_ck_w1[l], nsa_ck_w2[l], nsa_cv_w1[l],
                          nsa_cv_w2[l], fox_bias[l], w_branch[l], w_out[l])
        sh_f, sc_f, g_f = mod[3], mod[4], mod[5]
        gn = norm_ffn[l].reshape(1, d)
        if l % 2 == 0:
            e = l // 2
            dff = ffn_w1.shape[2]
            x2 = _ffn(x2, gn, sc_f, sh_f, g_f, ffn_w1[e].astype(BF16), ffn_w3[e].astype(BF16),
                      ffn_w2[e].astype(BF16), s, min(1024, s), dff // 2)
        else:
            e = l // 2
            tt = min(1024, s)
            dff = moe_w1.shape[3]
            h2, rank, gate, cnt = _route(x2, gn, sc_f, sh_f, router_w[e].T, s, tt)
            counts = cnt[:, :, 0].astype(jnp.int32).reshape(-1)
            x2 = _moe(counts, x2, g_f, h2, rank, gate, moe_w1[e].astype(BF16), moe_w3[e].astype(BF16),
                      moe_w2[e].astype(BF16), s, tt, dff // 2, 256)
    return x2.reshape(b, s, d)
```

```python
import functools

import numpy as np
import jax
import jax.numpy as jnp
from jax import lax
from jax.experimental import pallas as pl
from jax.experimental.pallas import tpu as pltpu

F32 = jnp.float32
BF16 = jnp.bfloat16
HI = lax.Precision.HIGHEST

LANES = 128
VMEM_LIMIT = 52 * 1024 * 1024

HEAD_DIM = 64
HEAD_SHIFT = 6
N_HEADS = 4
W_MIX = N_HEADS * HEAD_DIM
N_MIXERS = 4
ROPE_THETA = 500000.0
ROPE_DIMS = HEAD_DIM // 4
ROPE_HALF = ROPE_DIMS // 2
EPS = 1e-6
NEG = -1e30
BIG = 1e30
CMP_LEN = 32
CMP_STRIDE = 16
SEL_LEN = 64
SEL_SHIFT = 6
TOPN = 16
NSA_WINDOW = 512
DIL_CONFIGS = ((128, 1), (512, 4), (2048, 16))
SB_STOP = -110.0
GATE_LANE = 0
FOX_LANE = 3 * N_HEADS
VC_LANE = HEAD_DIM

NN = (((1,), (0,)), ((), ()))
NT = (((1,), (1,)), ((), ()))
TN = (((0,), (0,)), ((), ()))


def _dot(a, b, dims=NN, precision=None):
    return lax.dot_general(a, b, dims, precision=precision, preferred_element_type=F32)


def _dot_split(a, b_bf16, dims=NN):
    hi = a.astype(BF16)
    lo = (a - hi.astype(F32)).astype(BF16)
    return _dot(hi, b_bf16, dims) + _dot(lo, b_bf16, dims)


def _params(*sem):
    return pltpu.CompilerParams(dimension_semantics=sem, vmem_limit_bytes=VMEM_LIMIT)


def _mod_norm(x, g, sc, sh):
    ms = jnp.mean(x * x, axis=-1, keepdims=True)
    return (x * lax.rsqrt(ms + EPS) * g) * (1.0 + sc) + sh


def _head_masks():
    lane = lax.broadcasted_iota(jnp.int32, (1, W_MIX), 1)
    return [(lane >> HEAD_SHIFT) == h for h in range(N_HEADS)]


def _stack_masked(q, qm_scr, tq):
    for h, hm in enumerate(_head_masks()):
        qm_scr[h * tq:(h + 1) * tq, :] = jnp.where(hm, q, jnp.zeros_like(q))


def _pick_heads(stacked, tq, scale=None):
    out = None
    for h, hm in enumerate(_head_masks()):
        blk = stacked[h * tq:(h + 1) * tq, :]
        if scale is not None:
            blk = blk * scale[h]
        out = jnp.where(hm, blk, 0.0 if out is None else out)
    return out


def _ada_kernel(c_ref, w_ref, b_ref, o_ref):
    c = c_ref[...]
    ca = c * jax.nn.sigmoid(c)
    o_ref[0] = _dot(ca, w_ref[0], NN, HI) + b_ref[0]


def _ada(c, w_ada, b_ada):
    depth, d, n6 = w_ada.shape
    b = c.shape[0]
    tn = n6 // 4
    return pl.pallas_call(
        _ada_kernel,
        out_shape=jax.ShapeDtypeStruct((depth, b, n6), F32),
        grid=(depth, n6 // tn),
        in_specs=[
            pl.BlockSpec((b, d), lambda l, j: (0, 0)),
            pl.BlockSpec((1, d, tn), lambda l, j: (l, 0, j)),
            pl.BlockSpec((1, 1, tn), lambda l, j: (l, 0, j)),
        ],
        out_specs=pl.BlockSpec((1, b, tn), lambda l, j: (l, 0, j)),
        compiler_params=_params("arbitrary", "arbitrary"),
        name="ada_mod",
    )(c, w_ada, b_ada.reshape(depth, 1, n6))


_SLABS = (
    (True, False, F32), (True, True, BF16), (True, False, F32),
    (True, True, BF16), (True, True, BF16), (False, False, BF16), (False, False, BF16),
    (True, True, BF16), (True, True, BF16), (False, False, BF16),
    (False, False, BF16), (False, False, BF16), (False, False, BF16),
    (True, False, BF16), (True, False, BF16), (False, False, BF16),
    (False, False, F32),
)
N_SLABS = len(_SLABS)
(S_QNR, S_QR, S_KC, S_KSL, S_KW, S_VSL, S_VW, S_BQ, S_BK, S_BV, S_CQ, S_CK, S_CV,
 S_DQ, S_DK, S_DV, S_MISC) = range(N_SLABS)


def _proj_kernel(x_ref, g_ref, sc_ref, sh_ref, w_ref, gain_ref, bd_ref, rope_ref, *out_refs):
    h = _mod_norm(x_ref[...], g_ref[...], sc_ref[0], sh_ref[0]).astype(BF16)
    bd = bd_ref[...]
    for s, (norm, rope, _) in enumerate(_SLABS):
        y = _dot(h, w_ref[:, s * W_MIX:(s + 1) * W_MIX])
        if norm:
            ms = _dot_split(y * y, bd)
            y = y * lax.rsqrt(ms + EPS)
        y = y * gain_ref[s]
        if rope:
            y = (y * rope_ref[0, 0] + pltpu.roll(y, W_MIX - ROPE_HALF, 1) * rope_ref[0, 1]
                 + pltpu.roll(y, ROPE_HALF, 1) * rope_ref[0, 2])
        out_refs[s][...] = y.astype(out_refs[s].dtype)


def _proj(x2, g, sc, sh, w_slab, gain, bd, rope, seq, tt):
    n, d = x2.shape
    tpb = seq // tt
    out_shape = [jax.ShapeDtypeStruct((n, W_MIX), dt) for (_, _, dt) in _SLABS]
    return pl.pallas_call(
        _proj_kernel,
        out_shape=out_shape,
        grid=(n // tt,),
        in_specs=[
            pl.BlockSpec((tt, d), lambda i: (i, 0)),
            pl.BlockSpec((1, d), lambda i: (0, 0)),
            pl.BlockSpec((1, 1, d), lambda i: (i // tpb, 0, 0)),
            pl.BlockSpec((1, 1, d), lambda i: (i // tpb, 0, 0)),
            pl.BlockSpec((d, N_SLABS * W_MIX), lambda i: (0, 0)),
            pl.BlockSpec((N_SLABS, 1, W_MIX), lambda i: (0, 0, 0)),
            pl.BlockSpec((W_MIX, W_MIX), lambda i: (0, 0)),
            pl.BlockSpec((1, 3, tt, W_MIX), lambda i: (i // tpb, 0, i % tpb, 0)),
        ],
        out_specs=[pl.BlockSpec((tt, W_MIX), lambda i: (i, 0)) for _ in _SLABS],
        compiler_params=_params("arbitrary"),
        name="in_proj",
    )(x2, g, sc, sh, w_slab, gain, bd, rope)


def _pack_w_in(w_in, qk_gain):
    d = w_in.shape[0]
    o = 0
    a_q = w_in[:, o:o + W_MIX]; o += W_MIX
    a_kv = w_in[:, o:o + 6 * HEAD_DIM]; o += 6 * HEAD_DIM
    a_g = w_in[:, o:o + 3 * N_HEADS]; o += 3 * N_HEADS
    b_qkv = w_in[:, o:o + 3 * W_MIX]; o += 3 * W_MIX
    c_qkv = w_in[:, o:o + 3 * W_MIX]; o += 3 * W_MIX
    d_qkv = w_in[:, o:o + 3 * W_MIX]; o += 3 * W_MIX
    d_f = w_in[:, o:o + N_HEADS]; o += N_HEADS
    w_merge = w_in[:, o:]
    kc, vc, ksl, vsl, kw, vw = (a_kv[:, i * HEAD_DIM:(i + 1) * HEAD_DIM] for i in range(6))
    zeros = lambda w: jnp.zeros((d, w), w_in.dtype)
    rep = lambda w: jnp.tile(w, (1, N_HEADS))
    misc = jnp.concatenate([a_g, d_f, zeros(VC_LANE - FOX_LANE - N_HEADS), vc, zeros(W_MIX - 2 * HEAD_DIM)], axis=1)
    slabs = [
        a_q, a_q, jnp.concatenate([kc, zeros(W_MIX - HEAD_DIM)], axis=1),
        rep(ksl), rep(kw), rep(vsl), rep(vw),
        b_qkv[:, :W_MIX], b_qkv[:, W_MIX:2 * W_MIX], b_qkv[:, 2 * W_MIX:],
        c_qkv[:, :W_MIX], c_qkv[:, W_MIX:2 * W_MIX], c_qkv[:, 2 * W_MIX:],
        d_qkv[:, :W_MIX], d_qkv[:, W_MIX:2 * W_MIX], d_qkv[:, 2 * W_MIX:],
        misc,
    ]
    w_slab = jnp.concatenate(slabs, axis=1).astype(BF16)
    scale = HEAD_DIM ** -0.5
    t4 = lambda gvec: jnp.tile(gvec, N_HEADS)
    one = jnp.ones((W_MIX,), F32)
    gains = [
        t4(qk_gain[0]) * scale, t4(qk_gain[0]) * scale,
        jnp.concatenate([qk_gain[1], jnp.ones((W_MIX - HEAD_DIM,), F32)]),
        t4(qk_gain[2]), t4(qk_gain[3]), one, one,
        t4(qk_gain[4]) * scale, t4(qk_gain[5]), one,
        one * scale, one, one,
        t4(qk_gain[6]) * scale, t4(qk_gain[7]), one,
        one,
    ]
    gain = jnp.stack(gains).reshape(N_SLABS, 1, W_MIX).astype(F32)
    return w_slab, gain, w_merge.astype(BF16)


def _rope_tables(positions):
    inv = ROPE_THETA ** (-jnp.arange(0, ROPE_DIMS, 2, dtype=F32) / ROPE_DIMS)
    ang = positions.astype(F32)[..., None] * inv
    cos, sin = jnp.cos(ang), jnp.sin(ang)
    b, s, _ = cos.shape
    pad1 = jnp.ones((b, s, HEAD_DIM - ROPE_DIMS), F32)
    pad0 = jnp.zeros((b, s, HEAD_DIM - ROPE_DIMS), F32)
    z8 = jnp.zeros_like(sin)
    c64 = jnp.concatenate([cos, cos, pad1], axis=-1)
    s1_64 = jnp.concatenate([-sin, z8, pad0], axis=-1)
    s2_64 = jnp.concatenate([z8, sin, pad0], axis=-1)
    heads = lambda t: jnp.tile(t, (1, 1, N_HEADS))
    return jnp.stack([heads(c64), heads(s1_64), heads(s2_64)], axis=1)


def _compress_kernel(a_ref, b_ref, pe_ref, w1_ref, w2_ref, o_ref):
    half = w1_ref.shape[1] // 2
    w1 = w1_ref[0]
    hid = (_dot(a_ref[0, 0], w1[:half], NN, HI) + _dot(b_ref[0, 0], w1[half:], NN, HI)
           + _dot(pe_ref[0], w1, NN, HI))
    o_ref[0, 0] = _dot(jax.nn.gelu(hid), w2_ref[0], NN, HI)


def _compress(ch, chn, pe, w1, w2):
    _, b, ncp, cw = ch.shape
    hid = w1.shape[2]
    return pl.pallas_call(
        _compress_kernel,
        out_shape=jax.ShapeDtypeStruct((2, b, ncp, HEAD_DIM), F32),
        grid=(2, b),
        in_specs=[
            pl.BlockSpec((1, 1, ncp, cw), lambda k, i: (k, i, 0, 0)),
            pl.BlockSpec((1, 1, ncp, cw), lambda k, i: (k, i, 0, 0)),
            pl.BlockSpec((1, 1, 2 * cw), lambda k, i: (k, 0, 0)),
            pl.BlockSpec((1, 2 * cw, hid), lambda k, i: (k, 0, 0)),
            pl.BlockSpec((1, hid, HEAD_DIM), lambda k, i: (k, 0, 0)),
        ],
        out_specs=pl.BlockSpec((1, 1, ncp, HEAD_DIM), lambda k, i: (k, i, 0, 0)),
        compiler_params=_params("arbitrary", "arbitrary"),
        name="nsa_compress",
    )(ch, chn, pe, w1, w2)


def _cmp_kernel(q_ref, kc_ref, vc_ref, ov_ref, o_ref, sel_ref, qm_scr, *, tq, nc, n_sel):
    i = pl.program_id(1)
    rows = N_HEADS * tq
    ncp = kc_ref.shape[1]
    _stack_masked(q_ref[0], qm_scr, tq)
    s = _dot(qm_scr[...], kc_ref[0], NT, HI)
    r = lax.broadcasted_iota(jnp.int32, (rows, 1), 0)
    t = i * tq + (r & (tq - 1))
    c = lax.broadcasted_iota(jnp.int32, (1, ncp), 1)
    mask = (c * CMP_STRIDE + (CMP_LEN - 1) <= t) & (c < nc)
    sm = jnp.where(mask, s, NEG)
    m = jnp.max(sm, axis=-1, keepdims=True)
    e = jnp.where(mask, jnp.exp(sm - m), 0.0)
    l = jnp.sum(e, axis=-1, keepdims=True)
    p = e / jnp.maximum(l, 1e-30)
    o_ref[0] = _pick_heads(_dot(p.astype(BF16), vc_ref[0].astype(BF16)), tq).astype(o_ref.dtype)
    psum = p[0:tq] + p[tq:2 * tq] + p[2 * tq:3 * tq] + p[3 * tq:4 * tq]
    imp = _dot(psum, ov_ref[...], NN, HI)
    tt = i * tq + lax.broadcasted_iota(jnp.int32, (tq, 1), 0)
    j = lax.broadcasted_iota(jnp.int32, (1, n_sel), 1)
    cur = tt >> SEL_SHIFT
    valid = j <= cur
    forced = (j == 0) | (j == cur) | (j == cur - 1)
    score = jnp.where(valid, jnp.where(forced, BIG, imp), NEG)
    sel = jnp.zeros((tq, n_sel), F32)
    jf = j.astype(F32)
    for _ in range(min(TOPN, n_sel)):
        mx = jnp.max(score, axis=-1, keepdims=True)
        idx = jnp.min(jnp.where(score == mx, jf, float(n_sel)), axis=-1, keepdims=True)
        pick = jf == idx
        sel = jnp.where(pick, 1.0, sel)
        score = jnp.where(pick, -3e38, score)
    sel_ref[0] = sel.astype(sel_ref.dtype)


def _cmp_topk(q_nr, kc_rep, vc_rep, overlap, tq, nc):
    b, seq, _ = q_nr.shape
    ncp = kc_rep.shape[1]
    n_sel = seq // SEL_LEN
    return pl.pallas_call(
        functools.partial(_cmp_kernel, tq=tq, nc=nc, n_sel=n_sel),
        out_shape=[jax.ShapeDtypeStruct((b, seq, W_MIX), BF16),
                   jax.ShapeDtypeStruct((b, seq, n_sel), BF16)],
        grid=(b, seq // tq),
        in_specs=[
            pl.BlockSpec((1, tq, W_MIX), lambda g, i: (g, i, 0)),
            pl.BlockSpec((1, ncp, W_MIX), lambda g, i: (g, 0, 0)),
            pl.BlockSpec((1, ncp, W_MIX), lambda g, i: (g, 0, 0)),
            pl.BlockSpec((ncp, n_sel), lambda g, i: (0, 0)),
        ],
        out_specs=[pl.BlockSpec((1, tq, W_MIX), lambda g, i: (g, i, 0)),
                   pl.BlockSpec((1, tq, n_sel), lambda g, i: (g, i, 0))],
        scratch_shapes=[pltpu.VMEM((N_HEADS * tq, W_MIX), F32)],
        compiler_params=_params("arbitrary", "arbitrary"),
        name="nsa_cmp_topk",
    )(q_nr, kc_rep, vc_rep, overlap)


def _causal_kernel(*refs, tq, tk, has_sel, has_bias):
    it = iter(refs)
    q_ref, k_ref, v_ref = next(it), next(it), next(it)
    sel_ref = next(it) if has_sel else None
    fq_ref = next(it) if has_bias else None
    fk_ref = next(it) if has_bias else None
    o_ref, qm_scr, p_scr, m_scr, l_scr, a_scr, acc_scr = (next(it) for _ in range(7))
    i = pl.program_id(1)
    t0 = i * tq
    _stack_masked(q_ref[0], qm_scr, tq)
    m_scr[...] = jnp.full(m_scr.shape, NEG, F32)
    l_scr[...] = jnp.zeros(l_scr.shape, F32)
    acc_scr[...] = jnp.zeros(acc_scr.shape, F32)
    t_pos = t0 + lax.broadcasted_iota(jnp.int32, (tq, 1), 0)

    def tile(kt, diag):
        v_t = v_ref[0, kt]
        s_all = _dot(qm_scr[...], k_ref[0, kt], NT)
        s_pos = kt * tk + lax.broadcasted_iota(jnp.int32, (1, tk), 1)
        mask = (s_pos <= t_pos) if diag else None
        if has_sel:
            blk = lax.broadcasted_iota(jnp.int32, (sel_ref.shape[2], 1), 0)
            expand = jnp.where((s_pos >> SEL_SHIFT) == blk, 1.0, 0.0).astype(BF16)
            picked = _dot(sel_ref[0], expand) > 0.5
            mask = picked if mask is None else (mask & picked)
        for h in range(N_HEADS):
            rs = slice(h * tq, (h + 1) * tq)
            s = s_all[rs, :]
            if has_bias:
                s = s + (fq_ref[0, :, FOX_LANE + h:FOX_LANE + h + 1] - fk_ref[0, h, kt])
            if mask is not None:
                s = jnp.where(mask, s, NEG)
            m_old = m_scr[rs, :]
            m_new = jnp.maximum(m_old, jnp.max(s, axis=-1, keepdims=True))
            p = jnp.exp(s - m_new)
            alpha = jnp.exp(m_old - m_new)
            m_scr[rs, :] = m_new
            l_scr[rs, :] = alpha * l_scr[rs, :] + jnp.sum(p, axis=-1, keepdims=True)
            a_scr[rs, :] = alpha
            p_scr[rs, :] = p.astype(BF16)
        acc_scr[...] = a_scr[...] * acc_scr[...] + _dot(p_scr[...], v_t)

    n_full = t0 // tk

    def body(kt, carry):
        tile(kt, False)
        return carry

    lax.fori_loop(0, n_full, body, 0)
    for d in range(max(1, tq // tk)):
        tile(n_full + d, True)
    inv_l = [1.0 / l_scr[h * tq:(h + 1) * tq, :] for h in range(N_HEADS)]
    o_ref[0] = _pick_heads(acc_scr[...], tq, inv_l).astype(o_ref.dtype)


def _causal_attn(q, k, v, tq, tk, sel=None, fq=None, fk=None):
    b, s, _ = q.shape
    nk = s // tk
    rows = N_HEADS * tq
    args = [q, k.reshape(b, nk, tk, W_MIX), v.reshape(b, nk, tk, W_MIX)]
    in_specs = [
        pl.BlockSpec((1, tq, W_MIX), lambda a, i: (a, i, 0)),
        pl.BlockSpec((1, nk, tk, W_MIX), lambda a, i: (a, 0, 0, 0)),
        pl.BlockSpec((1, nk, tk, W_MIX), lambda a, i: (a, 0, 0, 0)),
    ]
    if sel is not None:
        args.append(sel)
        in_specs.append(pl.BlockSpec((1, tq, sel.shape[2]), lambda a, i: (a, i, 0)))
    if fq is not None:
        args += [fq, fk.reshape(b, N_HEADS, nk, 1, tk)]
        in_specs += [pl.BlockSpec((1, tq, W_MIX), lambda a, i: (a, i, 0)),
                     pl.BlockSpec((1, N_HEADS, nk, 1, tk), lambda a, i: (a, 0, 0, 0, 0))]
    return pl.pallas_call(
        functools.partial(_causal_kernel, tq=tq, tk=tk, has_sel=sel is not None, has_bias=fq is not None),
        out_shape=jax.ShapeDtypeStruct((b, s, W_MIX), BF16),
        grid=(b, s // tq),
        in_specs=in_specs,
        out_specs=pl.BlockSpec((1, tq, W_MIX), lambda a, i: (a, i, 0)),
        scratch_shapes=[pltpu.VMEM((rows, W_MIX), BF16), pltpu.VMEM((rows, tk), BF16),
                        pltpu.VMEM((rows, 1), F32), pltpu.VMEM((rows, 1), F32),
                        pltpu.VMEM((rows, 1), F32), pltpu.VMEM((rows, W_MIX), F32)],
        compiler_params=_params("arbitrary", "arbitrary"),
        name="causal_sel%d_bias%d" % (sel is not None, fq is not None),
    )(*args)


def _window_kernel(*refs, tq, wk, pad, window, ls, emit_lse, n_other):
    it = iter(refs)
    q_ref, k_ref, v_ref = next(it), next(it), next(it)
    others = [(next(it), next(it)) for _ in range(n_other)]
    o_ref = next(it)
    lse_ref = next(it) if emit_lse else None
    qm_scr, p_scr = next(it), next(it)
    t0 = pl.program_id(2) * tq
    start = pl.multiple_of(jnp.clip(t0 - pad, 0, ls - wk), LANES)
    _stack_masked(q_ref[0], qm_scr, tq)
    s_all = _dot(qm_scr[...], k_ref[0, pl.ds(start, wk), :], NT)
    t_pos = t0 + lax.broadcasted_iota(jnp.int32, (tq, 1), 0)
    s_pos = start + lax.broadcasted_iota(jnp.int32, (1, wk), 1)
    mask = (s_pos <= t_pos) & (t_pos - s_pos < window)
    ms, ls_ = [], []
    for h in range(N_HEADS):
        rs = slice(h * tq, (h + 1) * tq)
        s = jnp.where(mask, s_all[rs, :], NEG)
        m = jnp.max(s, axis=-1, keepdims=True)
        p = jnp.exp(s - m)
        ms.append(m)
        ls_.append(jnp.sum(p, axis=-1, keepdims=True))
        p_scr[rs, :] = p.astype(BF16)
    pv = _dot(p_scr[...], v_ref[0, pl.ds(start, wk), :])
    if n_other:
        scales, dens = [], []
        lses = [m + jnp.log(l) for m, l in zip(ms, ls_)]
        out = None
        hms = _head_masks()
        for h in range(N_HEADS):
            other_lse = [lr[0, :, h:h + 1] for (_, lr) in others]
            top = lses[h]
            for ol in other_lse:
                top = jnp.maximum(top, ol)
            w_self = jnp.exp(lses[h] - top)
            num = pv[h * tq:(h + 1) * tq, :] * (w_self / ls_[h])
            den = w_self
            for (orf, _), ol in zip(others, other_lse):
                w = jnp.exp(ol - top)
                num = num + w * orf[0]
                den = den + w
            out = jnp.where(hms[h], num / den, 0.0 if out is None else out)
        o_ref[0] = out.astype(o_ref.dtype)
    else:
        o_ref[0] = _pick_heads(pv, tq, [1.0 / l for l in ls_]).astype(o_ref.dtype)
    if emit_lse:
        lane = lax.broadcasted_iota(jnp.int32, (1, LANES), 1)
        tile = jnp.zeros((tq, LANES), F32)
        for h in range(N_HEADS):
            tile = jnp.where(lane == h, ms[h] + jnp.log(ls_[h]), tile)
        lse_ref[0] = tile


def _window_attn(q, k, v, *, dil, window, tq, out_dtype, emit_lse=False, others=()):
    b, s, _ = q.shape
    ls = s // dil
    tq = min(tq, ls)
    pad = -(-(window - 1) // LANES) * LANES
    wk = min(tq + pad, ls)
    rows = N_HEADS * tq
    view = lambda a: a.reshape(b, ls, dil * a.shape[2])
    args = [view(q), view(k), view(v)]
    in_specs = [
        pl.BlockSpec((1, tq, W_MIX), lambda a, r, i: (a, i, r)),
        pl.BlockSpec((1, ls, W_MIX), lambda a, r, i: (a, 0, r)),
        pl.BlockSpec((1, ls, W_MIX), lambda a, r, i: (a, 0, r)),
    ]
    for (o_g, lse_g) in others:
        args += [o_g, lse_g]
        in_specs += [pl.BlockSpec((1, tq, W_MIX), lambda a, r, i: (a, i, r)),
                     pl.BlockSpec((1, tq, LANES), lambda a, r, i: (a, i, r))]
    out_shape = [jax.ShapeDtypeStruct((b, ls, dil * W_MIX), out_dtype)]
    out_specs = [pl.BlockSpec((1, tq, W_MIX), lambda a, r, i: (a, i, r))]
    if emit_lse:
        out_shape.append(jax.ShapeDtypeStruct((b, ls, dil * LANES), F32))
        out_specs.append(pl.BlockSpec((1, tq, LANES), lambda a, r, i: (a, i, r)))
    res = pl.pallas_call(
        functools.partial(_window_kernel, tq=tq, wk=wk, pad=pad, window=window, ls=ls,
                          emit_lse=emit_lse, n_other=len(others)),
        out_shape=out_shape,
        grid=(b, dil, ls // tq),
        in_specs=in_specs,
        out_specs=out_specs,
        scratch_shapes=[pltpu.VMEM((rows, W_MIX), BF16), pltpu.VMEM((rows, wk), BF16)],
        compiler_params=_params("arbitrary", "arbitrary", "arbitrary"),
        name="window_d%d_w%d" % (dil, window),
    )(*args)
    o = res[0].reshape(b, s, W_MIX)
    if emit_lse:
        return o, res[1].reshape(b, s, LANES)
    return o


def _sb_kernel(q_ref, k_ref, v_ref, tri_ref, o_ref, qm_scr, a_scr, carry_scr, acc_scr, *, tq):
    i = pl.program_id(1)
    t0 = i * tq
    rows = N_HEADS * tq
    _stack_masked(q_ref[0], qm_scr, tq)
    carry_scr[...] = jnp.zeros(carry_scr.shape, F32)
    acc_scr[...] = jnp.zeros(acc_scr.shape, F32)
    r = lax.broadcasted_iota(jnp.int32, (rows, 1), 0)
    t_pos = t0 + (r & (tq - 1))

    def tile(kt, diag):
        z = _dot(qm_scr[...], k_ref[0, kt], NT)
        lg = -(jnp.maximum(z, 0.0) + jnp.log(1.0 + jnp.exp(-jnp.abs(z))))
        if diag:
            s_pos = kt * tq + lax.broadcasted_iota(jnp.int32, (1, tq), 1)
            strict = s_pos < t_pos
            lg = jnp.where(strict, lg, 0.0)
        cum = _dot_split(lg, tri_ref[...]) + carry_scr[...]
        a = jnp.exp(z + cum)
        if diag:
            a = jnp.where(strict, a, 0.0)
        a_scr[...] = a.astype(BF16)
        acc_scr[...] += _dot(a_scr[...], v_ref[0, kt])
        carry_scr[...] += jnp.sum(lg, axis=-1, keepdims=True)

    tile(i, True)

    def cond(state):
        j, top = state
        return (j < i) & (top > SB_STOP)

    def body(state):
        j, _ = state
        tile(i - 1 - j, False)
        return j + 1, jnp.max(carry_scr[...])

    lax.while_loop(cond, body, (jnp.int32(0), jnp.max(carry_scr[...])))
    o_ref[0] = _pick_heads(acc_scr[...], tq).astype(o_ref.dtype)


def _stick_breaking(q, k, v, tq):
    b, s, _ = q.shape
    nk = s // tq
    rows = N_HEADS * tq
    tri = jnp.asarray(np.tril(np.ones((tq, tq), np.float32)), BF16)
    return pl.pallas_call(
        functools.partial(_sb_kernel, tq=tq),
        out_shape=jax.ShapeDtypeStruct((b, s, W_MIX), BF16),
        grid=(b, nk),
        in_specs=[
            pl.BlockSpec((1, tq, W_MIX), lambda a, i: (a, i, 0)),
            pl.BlockSpec((1, nk, tq, W_MIX), lambda a, i: (a, 0, 0, 0)),
            pl.BlockSpec((1, nk, tq, W_MIX), lambda a, i: (a, 0, 0, 0)),
            pl.BlockSpec((tq, tq), lambda a, i: (0, 0)),
        ],
        out_specs=pl.BlockSpec((1, tq, W_MIX), lambda a, i: (a, i, 0)),
        scratch_shapes=[pltpu.VMEM((rows, W_MIX), BF16), pltpu.VMEM((rows, tq), BF16),
                        pltpu.VMEM((rows, 1), F32), pltpu.VMEM((rows, W_MIX), F32)],
        compiler_params=_params("arbitrary", "arbitrary"),
        name="stick_breaking",
    )(q, k.reshape(b, nk, tq, W_MIX), v.reshape(b, nk, tq, W_MIX), tri)


def _foxcum_kernel(x_ref, b_ref, tri_ref, o_ref, carry_scr):
    @pl.when(pl.program_id(1) == 0)
    def _():
        carry_scr[...] = jnp.zeros(carry_scr.shape, F32)

    z = x_ref[0] + b_ref[...]
    logf = jnp.minimum(z, 0.0) - jnp.log(1.0 + jnp.exp(-jnp.abs(z)))
    cum = _dot(tri_ref[...], logf, NN, HI) + carry_scr[...]
    o_ref[0] = cum
    carry_scr[...] = cum[cum.shape[0] - 1:, :]


def _fox_cumsum(misc, bias_vec, tc):
    b, s, w = misc.shape
    tri = jnp.asarray(np.tril(np.ones((tc, tc), np.float32)))
    return pl.pallas_call(
        _foxcum_kernel,
        out_shape=jax.ShapeDtypeStruct((b, s, w), F32),
        grid=(b, s // tc),
        in_specs=[
            pl.BlockSpec((1, tc, w), lambda a, i: (a, i, 0)),
            pl.BlockSpec((1, w), lambda a, i: (0, 0)),
            pl.BlockSpec((tc, tc), lambda a, i: (0, 0)),
        ],
        out_specs=pl.BlockSpec((1, tc, w), lambda a, i: (a, i, 0)),
        scratch_shapes=[pltpu.VMEM((1, w), F32)],
        compiler_params=_params("arbitrary", "arbitrary"),
        name="fox_cumsum",
    )(misc, bias_vec, tri)


def _merge_kernel(x_ref, g_ref, sc_ref, sh_ref, ga_ref, wm_ref, misc_ref, pg_ref,
                  ocmp_ref, osel_ref, owin_ref, ob_ref, oc_ref, od_ref,
                  wa_ref, wb_ref, wc_ref, wd_ref, wo_ref, o_ref):
    x = x_ref[...]
    d = x.shape[1]
    h = _mod_norm(x, g_ref[...], sc_ref[0], sh_ref[0]).astype(BF16)
    gate = jax.nn.sigmoid(misc_ref[...])
    o_a = (_dot_split(gate, pg_ref[0]) * ocmp_ref[...].astype(F32)
           + _dot_split(gate, pg_ref[1]) * osel_ref[...].astype(F32)
           + _dot_split(gate, pg_ref[2]) * owin_ref[...].astype(F32)).astype(BF16)
    mixed = jnp.zeros(x.shape, F32)
    for m, (o_m, w_ref) in enumerate(((o_a, wa_ref), (ob_ref[...], wb_ref),
                                      (oc_ref[...], wc_ref), (od_ref[...], wd_ref))):
        y = _dot(o_m, w_ref[...])
        gl = _dot(h, wm_ref[:, m * d:(m + 1) * d])
        mixed = mixed + jax.nn.sigmoid(gl) * y
    o_ref[...] = x + ga_ref[0] * _dot(mixed.astype(BF16), wo_ref[...])


def _merge(x2, g, sc, sh, ga, w_merge, misc, pg, o_cmp, o_sel, o_win, o_b, o_c, o_d,
           wa, wb, wc, wd, wo, seq, tt):
    n, d = x2.shape
    tpb = seq // tt
    row = lambda w: pl.BlockSpec((tt, w), lambda i: (i, 0))
    full = lambda a: pl.BlockSpec(a.shape, lambda i: (0,) * a.ndim)
    per_b = pl.BlockSpec((1, 1, d), lambda i: (i // tpb, 0, 0))
    return pl.pallas_call(
        _merge_kernel,
        out_shape=jax.ShapeDtypeStruct((n, d), F32),
        grid=(n // tt,),
        in_specs=[row(d), full(g), per_b, per_b, per_b, full(w_merge), row(W_MIX), full(pg)]
        + [row(W_MIX)] * 6 + [full(wa), full(wb), full(wc), full(wd), full(wo)],
        out_specs=row(d),
        compiler_params=_params("arbitrary"),
        name="merge_out",
    )(x2, g, sc, sh, ga, w_merge, misc, pg, o_cmp, o_sel, o_win, o_b, o_c, o_d, wa, wb, wc, wd, wo)


def _ffn_kernel(x_ref, g_ref, sc_ref, sh_ref, gf_ref, w1_ref, w3_ref, w2_ref, o_ref, h_scr, acc_scr):
    f = pl.program_id(1)

    @pl.when(f == 0)
    def _():
        h_scr[...] = _mod_norm(x_ref[...], g_ref[...], sc_ref[0], sh_ref[0]).astype(BF16)
        acc_scr[...] = jnp.zeros(acc_scr.shape, F32)

    h = h_scr[...]
    a = _dot(h, w1_ref[...])
    b = _dot(h, w3_ref[...])
    acc_scr[...] += _dot((a * jax.nn.sigmoid(a) * b).astype(BF16), w2_ref[...])

    @pl.when(f == pl.num_programs(1) - 1)
    def _():
        o_ref[...] = x_ref[...] + gf_ref[0] * acc_scr[...]


def _ffn(x2, g, sc, sh, gf, w1, w3, w2, seq, tt, tf):
    n, d = x2.shape
    dff = w1.shape[1]
    tpb = seq // tt
    per_b = pl.BlockSpec((1, 1, d), lambda i, f: (i // tpb, 0, 0))
    return pl.pallas_call(
        _ffn_kernel,
        out_shape=jax.ShapeDtypeStruct((n, d), F32),
        grid=(n // tt, dff // tf),
        in_specs=[
            pl.BlockSpec((tt, d), lambda i, f: (i, 0)),
            pl.BlockSpec((1, d), lambda i, f: (0, 0)),
            per_b, per_b, per_b,
            pl.BlockSpec((d, tf), lambda i, f: (0, f)),
            pl.BlockSpec((d, tf), lambda i, f: (0, f)),
            pl.BlockSpec((tf, d), lambda i, f: (f, 0)),
        ],
        out_specs=pl.BlockSpec((tt, d), lambda i, f: (i, 0)),
        scratch_shapes=[pltpu.VMEM((tt, d), BF16), pltpu.VMEM((tt, d), F32)],
        compiler_params=_params("arbitrary", "arbitrary"),
        name="ffn_swiglu",
    )(x2, g, sc, sh, gf, w1, w3, w2)


def _route_kernel(x_ref, g_ref, sc_ref, sh_ref, rw_ref, up_ref, h_ref, rank_ref, gate_ref, cnt_ref):
    hf = _mod_norm(x_ref[...], g_ref[...], sc_ref[0], sh_ref[0])
    h_ref[...] = hf.astype(BF16)
    logits = _dot(rw_ref[...], hf, NT, HI)
    ne, tt = logits.shape
    e_idx = lax.broadcasted_iota(jnp.int32, (ne, 1), 0).astype(F32)
    v1 = jnp.max(logits, axis=0, keepdims=True)
    i1 = jnp.min(jnp.where(logits == v1, e_idx, float(ne)), axis=0, keepdims=True)
    m1 = e_idx == i1
    rest = jnp.where(m1, -3e38, logits)
    v2 = jnp.max(rest, axis=0, keepdims=True)
    i2 = jnp.min(jnp.where(rest == v2, e_idx, float(ne)), axis=0, keepdims=True)
    m2 = e_idx == i2
    e2 = jnp.exp(v2 - v1)
    g1 = 1.0 / (1.0 + e2)
    g2 = e2 / (1.0 + e2)
    routed = m1 | m2
    rf = jnp.where(routed, 1.0, 0.0)
    rank = _dot(rf.astype(BF16), up_ref[...])
    rank = jnp.where(routed, rank, -1.0)
    gate = jnp.where(m1, g1, 0.0) + jnp.where(m2, g2, 0.0)
    for e in range(ne):
        rank_ref[0, e] = rank[e:e + 1, :]
        gate_ref[0, e] = gate[e:e + 1, :]
    cnt = jnp.sum(rf, axis=1, keepdims=True)
    cnt_ref[0] = jnp.broadcast_to(cnt, (ne, LANES))


def _route(x2, g, sc, sh, rw_t, seq, tt):
    n, d = x2.shape
    ne = rw_t.shape[0]
    tpb = seq // tt
    nt = n // tt
    upper = jnp.asarray(np.triu(np.ones((tt, tt), np.float32), 1), BF16)
    per_b = pl.BlockSpec((1, 1, d), lambda i: (i // tpb, 0, 0))
    return pl.pallas_call(
        _route_kernel,
        out_shape=[jax.ShapeDtypeStruct((n, d), BF16),
                   jax.ShapeDtypeStruct((nt, ne, 1, tt), F32),
                   jax.ShapeDtypeStruct((nt, ne, 1, tt), F32),
                   jax.ShapeDtypeStruct((nt, ne, LANES), F32)],
        grid=(nt,),
        in_specs=[
            pl.BlockSpec((tt, d), lambda i: (i, 0)),
            pl.BlockSpec((1, d), lambda i: (0, 0)),
            per_b, per_b,
            pl.BlockSpec((ne, d), lambda i: (0, 0)),
            pl.BlockSpec((tt, tt), lambda i: (0, 0)),
        ],
        out_specs=[pl.BlockSpec((tt, d), lambda i: (i, 0)),
                   pl.BlockSpec((1, ne, 1, tt), lambda i: (i, 0, 0, 0)),
                   pl.BlockSpec((1, ne, 1, tt), lambda i: (i, 0, 0, 0)),
                   pl.BlockSpec((1, ne, LANES), lambda i: (i, 0, 0))],
        compiler_params=_params("arbitrary"),
        name="moe_route",
    )(x2, g, sc, sh, rw_t, upper)


def _moe_kernel(cnt_ref, x_ref, gf_ref, h_ref, rank_ref, gate_ref, w1_ref, w3_ref, w2_ref,
                o_ref, acc_scr, *, chunk):
    i, e, f = pl.program_id(0), pl.program_id(1), pl.program_id(2)
    ne, nf = pl.num_programs(1), pl.num_programs(2)

    @pl.when((e == 0) & (f == 0))
    def _():
        acc_scr[...] = jnp.zeros(acc_scr.shape, F32)

    count = cnt_ref[i * ne + e]
    rank = rank_ref[0, 0]
    gate = gate_ref[0, 0]
    h = h_ref[...]

    def body(c, carry):
        r = c * chunk + lax.broadcasted_iota(jnp.int32, (chunk, 1), 0)
        hit = rank == r.astype(F32)
        p = jnp.where(hit, 1.0, 0.0).astype(BF16)
        xs = _dot(p, h).astype(BF16)
        a = _dot(xs, w1_ref[0])
        b = _dot(xs, w3_ref[0])
        y = _dot((a * jax.nn.sigmoid(a) * b).astype(BF16), w2_ref[0])
        gcol = jnp.sum(jnp.where(hit, gate, 0.0), axis=-1, keepdims=True)
        acc_scr[...] += _dot(p, (y * gcol).astype(BF16), TN)
        return carry

    lax.fori_loop(0, (count + chunk - 1) // chunk, body, 0)

    @pl.when((e == ne - 1) & (f == nf - 1))
    def _():
        o_ref[...] = x_ref[...] + gf_ref[0] * acc_scr[...]


def _moe(counts, x2, gf, h2, rank, gate, w1, w3, w2, seq, tt, tf, chunk):
    n, d = x2.shape
    ne, _, dff = w1.shape
    tpb = seq // tt
    grid_spec = pltpu.PrefetchScalarGridSpec(
        num_scalar_prefetch=1,
        grid=(n // tt, ne, dff // tf),
        in_specs=[
            pl.BlockSpec((tt, d), lambda i, e, f, c: (i, 0)),
            pl.BlockSpec((1, 1, d), lambda i, e, f, c: (i // tpb, 0, 0)),
            pl.BlockSpec((tt, d), lambda i, e, f, c: (i, 0)),
            pl.BlockSpec((1, 1, 1, tt), lambda i, e, f, c: (i, e, 0, 0)),
            pl.BlockSpec((1, 1, 1, tt), lambda i, e, f, c: (i, e, 0, 0)),
            pl.BlockSpec((1, d, tf), lambda i, e, f, c: (e, 0, f)),
            pl.BlockSpec((1, d, tf), lambda i, e, f, c: (e, 0, f)),
            pl.BlockSpec((1, tf, d), lambda i, e, f, c: (e, f, 0)),
        ],
        out_specs=pl.BlockSpec((tt, d), lambda i, e, f, c: (i, 0)),
        scratch_shapes=[pltpu.VMEM((tt, d), F32)],
    )
    return pl.pallas_call(
        functools.partial(_moe_kernel, chunk=chunk),
        out_shape=jax.ShapeDtypeStruct((n, d), F32),
        grid_spec=grid_spec,
        compiler_params=_params("arbitrary", "arbitrary", "arbitrary"),
        name="moe_experts",
    )(counts, x2, gf, h2, rank, gate, w1, w3, w2)


def _overlap_matrix(ncp, nc, n_sel):
    c0 = np.arange(ncp) * CMP_STRIDE
    c1 = c0 + CMP_LEN
    s0 = np.arange(n_sel) * SEL_LEN
    s1 = s0 + SEL_LEN
    ov = ((c0[:, None] < s1[None, :]) & (c1[:, None] > s0[None, :])).astype(np.float32)
    ov[nc:] = 0.0
    return jnp.asarray(ov)


def _gate_expand():
    pg = np.zeros((3, W_MIX, W_MIX), np.float32)
    for br in range(3):
        for h in range(N_HEADS):
            pg[br, GATE_LANE + 3 * h + br, h * HEAD_DIM:(h + 1) * HEAD_DIM] = 1.0
    return jnp.asarray(pg, BF16)


def _mixer_layer(x2, b, s, mod, norm_g, rope, w_in, qk_gain, pe_k, pe_v, ck1, ck2, cv1, cv2,
                 fox_b, w_branch, w_out):
    n, d = x2.shape
    sh_a, sc_a, g_a = mod[0], mod[1], mod[2]
    w_slab, gain, w_merge = _pack_w_in(w_in, qk_gain)
    bd = jnp.asarray(np.kron(np.eye(N_HEADS), np.full((HEAD_DIM, HEAD_DIM), 1.0 / HEAD_DIM)), BF16)
    sl = _proj(x2, norm_g, sc_a, sh_a, w_slab, gain, bd, rope, s, min(512, s))
    sl = [a.reshape(b, s, W_MIX) for a in sl]
    misc = sl[S_MISC]

    nch = s // CMP_STRIDE
    nc = nch - CMP_LEN // CMP_STRIDE + 1
    kc_raw = sl[S_KC][..., :HEAD_DIM]
    vc_raw = misc[..., VC_LANE:VC_LANE + HEAD_DIM]
    chunks = jnp.stack([kc_raw, vc_raw]).reshape(2, b, nch, CMP_STRIDE * HEAD_DIM)
    chunks_next = jnp.concatenate([chunks[:, :, 1:], jnp.zeros_like(chunks[:, :, :1])], axis=2)
    pe = jnp.stack([pe_k, pe_v]).reshape(2, 1, CMP_LEN * HEAD_DIM)
    kvc = jnp.tile(_compress(chunks, chunks_next, pe, jnp.stack([ck1, cv1]), jnp.stack([ck2, cv2])),
                   (1, 1, 1, N_HEADS))
    overlap = _overlap_matrix(nch, nc, s // SEL_LEN)
    o_cmp, selmask = _cmp_topk(sl[S_QNR], kvc[0], kvc[1], overlap, min(256, s), nc)
    tq, tk = min(256, s), min(512, s)
    o_sel = _causal_attn(sl[S_QR], sl[S_KSL], sl[S_VSL], tq, tk, sel=selmask)
    o_win = _window_attn(sl[S_QR], sl[S_KW], sl[S_VW], dil=1, window=NSA_WINDOW, tq=256, out_dtype=BF16)

    others = []
    for (wdw, dil) in DIL_CONFIGS[:0:-1]:
        others.append(_window_attn(sl[S_BQ], sl[S_BK], sl[S_BV], dil=dil, window=wdw // dil + 1, tq=256,
                                   out_dtype=F32, emit_lse=True))
    wdw, dil = DIL_CONFIGS[0]
    o_b = _window_attn(sl[S_BQ], sl[S_BK], sl[S_BV], dil=dil, window=wdw // dil + 1, tq=256,
                       out_dtype=BF16, others=others)

    o_c = _stick_breaking(sl[S_CQ], sl[S_CK], sl[S_CV], min(256, s))

    bias_vec = jnp.zeros((1, W_MIX), F32).at[0, FOX_LANE:FOX_LANE + N_HEADS].set(fox_b)
    fcum = _fox_cumsum(misc, bias_vec, min(512, s))
    fk = fcum[..., FOX_LANE:FOX_LANE + N_HEADS].transpose(0, 2, 1)
    o_d = _causal_attn(sl[S_DQ], sl[S_DK], sl[S_DV], tq, tk, fq=fcum, fk=fk)

    wb16 = w_branch.astype(BF16)
    flat = lambda a: a.reshape(n, W_MIX)
    return _merge(x2, norm_g, sc_a, sh_a, g_a, w_merge, flat(misc), _gate_expand(),
                  flat(o_cmp), flat(o_sel), flat(o_win), flat(o_b), flat(o_c), flat(o_d),
                  wb16[0], wb16[1], wb16[2], wb16[3], w_out.astype(BF16), s, min(512, s))


def kernel(x, c, positions, w_ada, b_ada, norm_mix, norm_ffn, w_in, qk_gain, nsa_pe_k, nsa_pe_v,
           nsa_ck_w1, nsa_ck_w2, nsa_cv_w1, nsa_cv_w2, fox_bias, w_branch, w_out,
           ffn_w1, ffn_w3, ffn_w2, router_w, moe_w1, moe_w3, moe_w2):
    b, s, d = x.shape
    depth = w_ada.shape[0]
    rope = _rope_tables(positions)
    mods = _ada(c, w_ada, b_ada).reshape(depth, b, 6, 1, d).transpose(0, 2, 1, 3, 4)
    x2 = x.reshape(b * s, d)
    for l in range(depth):
        mod = mods[l]
        x2 = _mixer_layer(x2, b, s, mod[0:3], norm_mix[l].reshape(1, d), rope, w_in[l], qk_gain[l],
                          nsa_pe_k[l], nsa_pe_v[l], nsa_ck_w1[l], nsa_ck_w2[l], nsa_cv_w1[l],
                          nsa_cv_w2[l], fox_bias[l], w_branch[l], w_out[l])
        sh_f, sc_f, g_f = mod[3], mod[4], mod[5]
        gn = norm_ffn[l].reshape(1, d)
        e = l // 2
        if l % 2 == 0:
            dff = ffn_w1.shape[2]
            x2 = _ffn(x2, gn, sc_f, sh_f, g_f, ffn_w1[e].astype(BF16), ffn_w3[e].astype(BF16),
                      ffn_w2[e].astype(BF16), s, min(1024, s), dff // 2)
        else:
            tt = min(1024, s)
            dff = moe_w1.shape[3]
            h2, rank, gate, cnt = _route(x2, gn, sc_f, sh_f, router_w[e].T, s, tt)
            counts = cnt[:, :, 0].astype(jnp.int32).reshape(-1)
            x2 = _moe(counts, x2, g_f, h2, rank, gate, moe_w1[e].astype(BF16), moe_w3[e].astype(BF16),
                      moe_w2[e].astype(BF16), s, tt, dff // 2, 256)
    return x2.reshape(b, s, d)
```

```python
import functools

import numpy as np
import jax
import jax.numpy as jnp
from jax import lax
from jax.experimental import pallas as pl
from jax.experimental.pallas import tpu as pltpu

F32 = jnp.float32
BF16 = jnp.bfloat16
HI = lax.Precision.HIGHEST

LANES = 128
VMEM_LIMIT = 52 * 1024 * 1024

HEAD_DIM = 64
HEAD_SHIFT = 6
N_HEADS = 4
W_MIX = N_HEADS * HEAD_DIM
N_MIXERS = 4
ROPE_THETA = 500000.0
ROPE_DIMS = HEAD_DIM // 4
ROPE_HALF = ROPE_DIMS // 2
EPS = 1e-6
NEG = -1e30
BIG = 1e30
CMP_LEN = 32
CMP_STRIDE = 16
SEL_LEN = 64
SEL_SHIFT = 6
TOPN = 16
NSA_WINDOW = 512
DIL_CONFIGS = ((128, 1), (512, 4), (2048, 16))
SB_STOP = -110.0
GATE_LANE = 0
FOX_LANE = 3 * N_HEADS
VC_LANE = HEAD_DIM

NN = (((1,), (0,)), ((), ()))
NT = (((1,), (1,)), ((), ()))
TN = (((0,), (0,)), ((), ()))


def _dot(a, b, dims=NN, precision=None):
    return lax.dot_general(a, b, dims, precision=precision, preferred_element_type=F32)


def _dot_split(a, b_bf16, dims=NN):
    hi = a.astype(BF16)
    lo = (a - hi.astype(F32)).astype(BF16)
    return _dot(hi, b_bf16, dims) + _dot(lo, b_bf16, dims)


def _params(*sem):
    return pltpu.CompilerParams(dimension_semantics=sem, vmem_limit_bytes=VMEM_LIMIT)


def _mod_norm(x, g, sc, sh):
    ms = jnp.mean(x * x, axis=-1, keepdims=True)
    return (x * lax.rsqrt(ms + EPS) * g) * (1.0 + sc) + sh


def _head_masks():
    lane = lax.broadcasted_iota(jnp.int32, (1, W_MIX), 1)
    return [(lane >> HEAD_SHIFT) == h for h in range(N_HEADS)]


def _stack_masked(q, qm_scr, tq):
    for h, hm in enumerate(_head_masks()):
        qm_scr[h * tq:(h + 1) * tq, :] = jnp.where(hm, q, jnp.zeros_like(q))


def _pick_heads(stacked, tq, scale=None):
    out = None
    for h, hm in enumerate(_head_masks()):
        blk = stacked[h * tq:(h + 1) * tq, :]
        if scale is not None:
            blk = blk * scale[h]
        out = jnp.where(hm, blk, 0.0 if out is None else out)
    return out


def _ada_kernel(c_ref, w_ref, b_ref, o_ref):
    c = c_ref[...]
    ca = c * jax.nn.sigmoid(c)
    o_ref[0] = _dot(ca, w_ref[0], NN, HI) + b_ref[0]


def _ada(c, w_ada, b_ada):
    depth, d, n6 = w_ada.shape
    b = c.shape[0]
    tn = n6 // 4
    return pl.pallas_call(
        _ada_kernel,
        out_shape=jax.ShapeDtypeStruct((depth, b, n6), F32),
        grid=(depth, n6 // tn),
        in_specs=[
            pl.BlockSpec((b, d), lambda l, j: (0, 0)),
            pl.BlockSpec((1, d, tn), lambda l, j: (l, 0, j)),
            pl.BlockSpec((1, 1, tn), lambda l, j: (l, 0, j)),
        ],
        out_specs=pl.BlockSpec((1, b, tn), lambda l, j: (l, 0, j)),
        compiler_params=_params("arbitrary", "arbitrary"),
        name="ada_mod",
    )(c, w_ada, b_ada.reshape(depth, 1, n6))


_SLABS = (
    (True, False, F32), (True, True, BF16), (True, False, F32),
    (True, True, BF16), (True, True, BF16), (False, False, BF16), (False, False, BF16),
    (True, True, BF16), (True, True, BF16), (False, False, BF16),
    (False, False, BF16), (False, False, BF16), (False, False, BF16),
    (True, False, BF16), (True, False, BF16), (False, False, BF16),
    (False, False, F32),
)
N_SLABS = len(_SLABS)
(S_QNR, S_QR, S_KC, S_KSL, S_KW, S_VSL, S_VW, S_BQ, S_BK, S_BV, S_CQ, S_CK, S_CV,
 S_DQ, S_DK, S_DV, S_MISC) = range(N_SLABS)


def _proj_kernel(x_ref, g_ref, sc_ref, sh_ref, w_ref, gain_ref, bd_ref, rope_ref, *out_refs):
    h = _mod_norm(x_ref[...], g_ref[...], sc_ref[0], sh_ref[0]).astype(BF16)
    bd = bd_ref[...]
    for s, (norm, rope, _) in enumerate(_SLABS):
        y = _dot(h, w_ref[:, s * W_MIX:(s + 1) * W_MIX])
        if norm:
            ms = _dot_split(y * y, bd)
            y = y * lax.rsqrt(ms + EPS)
        y = y * gain_ref[s]
        if rope:
            y = (y * rope_ref[0, 0] + pltpu.roll(y, W_MIX - ROPE_HALF, 1) * rope_ref[0, 1]
                 + pltpu.roll(y, ROPE_HALF, 1) * rope_ref[0, 2])
        out_refs[s][...] = y.astype(out_refs[s].dtype)


def _proj(x2, g, sc, sh, w_slab, gain, bd, rope, seq, tt):
    n, d = x2.shape
    tpb = seq // tt
    out_shape = [jax.ShapeDtypeStruct((n, W_MIX), dt) for (_, _, dt) in _SLABS]
    return pl.pallas_call(
        _proj_kernel,
        out_shape=out_shape,
        grid=(n // tt,),
        in_specs=[
            pl.BlockSpec((tt, d), lambda i: (i, 0)),
            pl.BlockSpec((1, d), lambda i: (0, 0)),
            pl.BlockSpec((1, 1, d), lambda i: (i // tpb, 0, 0)),
            pl.BlockSpec((1, 1, d), lambda i: (i // tpb, 0, 0)),
            pl.BlockSpec((d, N_SLABS * W_MIX), lambda i: (0, 0)),
            pl.BlockSpec((N_SLABS, 1, W_MIX), lambda i: (0, 0, 0)),
            pl.BlockSpec((W_MIX, W_MIX), lambda i: (0, 0)),
            pl.BlockSpec((1, 3, tt, W_MIX), lambda i: (i // tpb, 0, i % tpb, 0)),
        ],
        out_specs=[pl.BlockSpec((tt, W_MIX), lambda i: (i, 0)) for _ in _SLABS],
        compiler_params=_params("arbitrary"),
        name="in_proj",
    )(x2, g, sc, sh, w_slab, gain, bd, rope)


def _pack_w_in(w_in, qk_gain):
    d = w_in.shape[0]
    o = 0
    a_q = w_in[:, o:o + W_MIX]; o += W_MIX
    a_kv = w_in[:, o:o + 6 * HEAD_DIM]; o += 6 * HEAD_DIM
    a_g = w_in[:, o:o + 3 * N_HEADS]; o += 3 * N_HEADS
    b_qkv = w_in[:, o:o + 3 * W_MIX]; o += 3 * W_MIX
    c_qkv = w_in[:, o:o + 3 * W_MIX]; o += 3 * W_MIX
    d_qkv = w_in[:, o:o + 3 * W_MIX]; o += 3 * W_MIX
    d_f = w_in[:, o:o + N_HEADS]; o += N_HEADS
    w_merge = w_in[:, o:]
    kc, vc, ksl, vsl, kw, vw = (a_kv[:, i * HEAD_DIM:(i + 1) * HEAD_DIM] for i in range(6))
    zeros = lambda w: jnp.zeros((d, w), w_in.dtype)
    rep = lambda w: jnp.tile(w, (1, N_HEADS))
    misc = jnp.concatenate([a_g, d_f, zeros(VC_LANE - FOX_LANE - N_HEADS), vc, zeros(W_MIX - 2 * HEAD_DIM)], axis=1)
    slabs = [
        a_q, a_q, jnp.concatenate([kc, zeros(W_MIX - HEAD_DIM)], axis=1),
        rep(ksl), rep(kw), rep(vsl), rep(vw),
        b_qkv[:, :W_MIX], b_qkv[:, W_MIX:2 * W_MIX], b_qkv[:, 2 * W_MIX:],
        c_qkv[:, :W_MIX], c_qkv[:, W_MIX:2 * W_MIX], c_qkv[:, 2 * W_MIX:],
        d_qkv[:, :W_MIX], d_qkv[:, W_MIX:2 * W_MIX], d_qkv[:, 2 * W_MIX:],
        misc,
    ]
    w_slab = jnp.concatenate(slabs, axis=1).astype(BF16)
    scale = HEAD_DIM ** -0.5
    t4 = lambda gvec: jnp.tile(gvec, N_HEADS)
    one = jnp.ones((W_MIX,), F32)
    gains = [
        t4(qk_gain[0]) * scale, t4(qk_gain[0]) * scale,
        jnp.concatenate([qk_gain[1], jnp.ones((W_MIX - HEAD_DIM,), F32)]),
        t4(qk_gain[2]), t4(qk_gain[3]), one, one,
        t4(qk_gain[4]) * scale, t4(qk_gain[5]), one,
        one * scale, one, one,
        t4(qk_gain[6]) * scale, t4(qk_gain[7]), one,
        one,
    ]
    gain = jnp.stack(gains).reshape(N_SLABS, 1, W_MIX).astype(F32)
    return w_slab, gain, w_merge.astype(BF16)


def _rope_tables(positions):
    inv = ROPE_THETA ** (-jnp.arange(0, ROPE_DIMS, 2, dtype=F32) / ROPE_DIMS)
    ang = positions.astype(F32)[..., None] * inv
    cos, sin = jnp.cos(ang), jnp.sin(ang)
    b, s, _ = cos.shape
    pad1 = jnp.ones((b, s, HEAD_DIM - ROPE_DIMS), F32)
    pad0 = jnp.zeros((b, s, HEAD_DIM - ROPE_DIMS), F32)
    z8 = jnp.zeros_like(sin)
    c64 = jnp.concatenate([cos, cos, pad1], axis=-1)
    s1_64 = jnp.concatenate([-sin, z8, pad0], axis=-1)
    s2_64 = jnp.concatenate([z8, sin, pad0], axis=-1)
    heads = lambda t: jnp.tile(t, (1, 1, N_HEADS))
    return jnp.stack([heads(c64), heads(s1_64), heads(s2_64)], axis=1)


def _compress_kernel(a_ref, b_ref, pe_ref, w1_ref, w2_ref, o_ref):
    half = w1_ref.shape[1] // 2
    w1 = w1_ref[0]
    hid = (_dot(a_ref[0, 0], w1[:half], NN, HI) + _dot(b_ref[0, 0], w1[half:], NN, HI)
           + _dot(pe_ref[0], w1, NN, HI))
    o_ref[0, 0] = _dot(jax.nn.gelu(hid), w2_ref[0], NN, HI)


def _compress(ch, chn, pe, w1, w2):
    _, b, ncp, cw = ch.shape
    hid = w1.shape[2]
    return pl.pallas_call(
        _compress_kernel,
        out_shape=jax.ShapeDtypeStruct((2, b, ncp, HEAD_DIM), F32),
        grid=(2, b),
        in_specs=[
            pl.BlockSpec((1, 1, ncp, cw), lambda k, i: (k, i, 0, 0)),
            pl.BlockSpec((1, 1, ncp, cw), lambda k, i: (k, i, 0, 0)),
            pl.BlockSpec((1, 1, 2 * cw), lambda k, i: (k, 0, 0)),
            pl.BlockSpec((1, 2 * cw, hid), lambda k, i: (k, 0, 0)),
            pl.BlockSpec((1, hid, HEAD_DIM), lambda k, i: (k, 0, 0)),
        ],
        out_specs=pl.BlockSpec((1, 1, ncp, HEAD_DIM), lambda k, i: (k, i, 0, 0)),
        compiler_params=_params("arbitrary", "arbitrary"),
        name="nsa_compress",
    )(ch, chn, pe, w1, w2)


def _cmp_kernel(q_ref, kc_ref, vc_ref, ov_ref, o_ref, sel_ref, qm_scr, *, tq, nc, n_sel):
    i = pl.program_id(1)
    rows = N_HEADS * tq
    ncp = kc_ref.shape[1]
    _stack_masked(q_ref[0], qm_scr, tq)
    s = _dot(qm_scr[...], kc_ref[0], NT, HI)
    r = lax.broadcasted_iota(jnp.int32, (rows, 1), 0)
    t = i * tq + (r & (tq - 1))
    c = lax.broadcasted_iota(jnp.int32, (1, ncp), 1)
    mask = (c * CMP_STRIDE + (CMP_LEN - 1) <= t) & (c < nc)
    sm = jnp.where(mask, s, NEG)
    m = jnp.max(sm, axis=-1, keepdims=True)
    e = jnp.where(mask, jnp.exp(sm - m), 0.0)
    l = jnp.sum(e, axis=-1, keepdims=True)
    p = e / jnp.maximum(l, 1e-30)
    o_ref[0] = _pick_heads(_dot(p.astype(BF16), vc_ref[0].astype(BF16)), tq).astype(o_ref.dtype)
    psum = p[0:tq] + p[tq:2 * tq] + p[2 * tq:3 * tq] + p[3 * tq:4 * tq]
    imp = _dot(psum, ov_ref[...], NN, HI)
    tt = i * tq + lax.broadcasted_iota(jnp.int32, (tq, 1), 0)
    j = lax.broadcasted_iota(jnp.int32, (1, n_sel), 1)
    cur = tt >> SEL_SHIFT
    valid = j <= cur
    forced = (j == 0) | (j == cur) | (j == cur - 1)
    score = jnp.where(valid, jnp.where(forced, BIG, imp), NEG)
    sel = jnp.zeros((tq, n_sel), F32)
    jf = j.astype(F32)
    for _ in range(min(TOPN, n_sel)):
        mx = jnp.max(score, axis=-1, keepdims=True)
        idx = jnp.min(jnp.where(score == mx, jf, float(n_sel)), axis=-1, keepdims=True)
        pick = jf == idx
        sel = jnp.where(pick, 1.0, sel)
        score = jnp.where(pick, -3e38, score)
    sel_ref[0] = sel.astype(sel_ref.dtype)


def _cmp_topk(q_nr, kc_rep, vc_rep, overlap, tq, nc):
    b, seq, _ = q_nr.shape
    ncp = kc_rep.shape[1]
    n_sel = seq // SEL_LEN
    return pl.pallas_call(
        functools.partial(_cmp_kernel, tq=tq, nc=nc, n_sel=n_sel),
        out_shape=[jax.ShapeDtypeStruct((b, seq, W_MIX), BF16),
                   jax.ShapeDtypeStruct((b, seq, n_sel), BF16)],
        grid=(b, seq // tq),
        in_specs=[
            pl.BlockSpec((1, tq, W_MIX), lambda g, i: (g, i, 0)),
            pl.BlockSpec((1, ncp, W_MIX), lambda g, i: (g, 0, 0)),
            pl.BlockSpec((1, ncp, W_MIX), lambda g, i: (g, 0, 0)),
            pl.BlockSpec((ncp, n_sel), lambda g, i: (0, 0)),
        ],
        out_specs=[pl.BlockSpec((1, tq, W_MIX), lambda g, i: (g, i, 0)),
                   pl.BlockSpec((1, tq, n_sel), lambda g, i: (g, i, 0))],
        scratch_shapes=[pltpu.VMEM((N_HEADS * tq, W_MIX), F32)],
        compiler_params=_params("arbitrary", "arbitrary"),
        name="nsa_cmp_topk",
    )(q_nr, kc_rep, vc_rep, overlap)


def _causal_kernel(*refs, tq, tk, has_sel, has_bias):
    it = iter(refs)
    q_ref, k_ref, vt_ref = next(it), next(it), next(it)
    sel_ref = next(it) if has_sel else None
    fa_ref = next(it) if has_bias else None
    fb_ref = next(it) if has_bias else None
    o_ref, qm_scr, acc_scr, sa_scr, sb_scr, m_scr, l_scr = (next(it) for _ in range(7))
    i = pl.program_id(1)
    t0 = i * tq
    _stack_masked(q_ref[0], qm_scr, tq)
    acc_scr[...] = jnp.zeros(acc_scr.shape, F32)
    t_pos = t0 + lax.broadcasted_iota(jnp.int32, (1, tq), 1)

    n_s = N_HEADS + (1 if has_sel else 0)

    def scores(kt, s_buf):
        k_t = k_ref[0, kt]
        for h in range(N_HEADS):
            s = _dot(k_t, qm_scr[h * tq:(h + 1) * tq, :], NT)
            if has_bias:
                s = s + _dot(fa_ref[0, kt], fb_ref[0, h])
            s_buf[h] = s
        if has_sel:
            s_pos = kt * tk + lax.broadcasted_iota(jnp.int32, (tk, 1), 0)
            blk = lax.broadcasted_iota(jnp.int32, (1, sel_ref.shape[2]), 1)
            expand = jnp.where((s_pos >> SEL_SHIFT) == blk, 1.0, 0.0).astype(BF16)
            s_buf[N_HEADS] = _dot(expand, sel_ref[0], NT)

    def update(kt, s_buf, diag):
        vt_t = vt_ref[0, kt]
        mask = None
        if diag:
            s_pos = kt * tk + lax.broadcasted_iota(jnp.int32, (tk, 1), 0)
            mask = s_pos <= t_pos
        if has_sel:
            picked = s_buf[N_HEADS] > 0.5
            mask = picked if mask is None else (mask & picked)
        ps, alphas = [], []
        for h in range(N_HEADS):
            s = s_buf[h]
            if mask is not None:
                s = jnp.where(mask, s, NEG)
            m_old = m_scr[h]
            m_new = jnp.maximum(m_old, jnp.max(s, axis=0, keepdims=True))
            p = jnp.exp(s - m_new)
            alpha = jnp.exp(m_old - m_new)
            m_scr[h] = m_new
            l_scr[h] = alpha * l_scr[h] + jnp.sum(p, axis=0, keepdims=True)
            ps.append(p.astype(BF16))
            alphas.append(alpha)
        for h in range(N_HEADS):
            rs = slice(h * HEAD_DIM, (h + 1) * HEAD_DIM)
            acc_scr[rs, :] = alphas[h] * acc_scr[rs, :] + _dot(vt_t[rs, :], ps[h])

    n_last = t0 // tk
    m_scr[...] = jnp.full(m_scr.shape, NEG, F32)
    l_scr[...] = jnp.zeros(l_scr.shape, F32)
    scores(0, sa_scr)

    def body(j, carry):
        kt = 2 * j
        scores(kt + 1, sb_scr)
        update(kt, sa_scr, False)
        scores(kt + 2, sa_scr)
        update(kt + 1, sb_scr, False)
        return carry

    lax.fori_loop(0, n_last // 2, body, 0)

    @pl.when(n_last % 2 == 1)
    def _():
        scores(n_last, sb_scr)
        update(n_last - 1, sa_scr, False)
        update(n_last, sb_scr, True)

    @pl.when(n_last % 2 == 0)
    def _():
        update(n_last, sa_scr, True)

    ls = [l_scr[h] for h in range(N_HEADS)]
    for h in range(N_HEADS):
        rs = slice(h * HEAD_DIM, (h + 1) * HEAD_DIM)
        acc_scr[rs, :] = acc_scr[rs, :] / ls[h]
    o_ref[0] = acc_scr[...].T.astype(o_ref.dtype)


def _split3(x):
    hi = x.astype(BF16)
    r1 = x - hi.astype(F32)
    mid = r1.astype(BF16)
    lo = (r1 - mid.astype(F32)).astype(BF16)
    return hi, mid, lo


def _causal_attn(q, k, v, tq, tk, sel=None, fcum=None):
    b, s, _ = q.shape
    nk = s // tk
    rows = N_HEADS * tq
    n_s = N_HEADS + (1 if sel is not None else 0)
    vt = v.reshape(b, nk, tk, W_MIX).transpose(0, 1, 3, 2)
    args = [q, k.reshape(b, nk, tk, W_MIX), vt]
    in_specs = [
        pl.BlockSpec((1, tq, W_MIX), lambda a, i: (a, i, 0)),
        pl.BlockSpec((1, nk, tk, W_MIX), lambda a, i: (a, 0, 0, 0)),
        pl.BlockSpec((1, nk, W_MIX, tk), lambda a, i: (a, 0, 0, 0)),
    ]
    if sel is not None:
        args.append(sel)
        in_specs.append(pl.BlockSpec((1, tq, sel.shape[2]), lambda a, i: (a, i, 0)))
    if fcum is not None:
        nf = 2 * 3 + 2
        ones = jnp.ones(fcum.shape, BF16)
        zero = jnp.zeros(fcum.shape, BF16)
        parts = _split3(fcum)
        key_f = jnp.stack([ones, ones, ones] + [-p for p in parts] + [zero, zero], axis=-1)
        qry_f = jnp.stack(list(parts) + [ones, ones, ones, zero, zero], axis=-1)
        fa = key_f.reshape(b, nk, tk, N_HEADS * nf)
        eye = jnp.eye(N_HEADS, dtype=BF16)
        fb = jnp.einsum('bshf,hg->bgshf', qry_f, eye).reshape(b, N_HEADS, s, N_HEADS * nf).transpose(0, 1, 3, 2)
        args += [fa, fb]
        in_specs += [pl.BlockSpec((1, nk, tk, N_HEADS * nf), lambda a, i: (a, 0, 0, 0)),
                     pl.BlockSpec((1, N_HEADS, N_HEADS * nf, tq), lambda a, i: (a, 0, 0, i))]
    return pl.pallas_call(
        functools.partial(_causal_kernel, tq=tq, tk=tk, has_sel=sel is not None, has_bias=fcum is not None),
        out_shape=jax.ShapeDtypeStruct((b, s, W_MIX), BF16),
        grid=(b, s // tq),
        in_specs=in_specs,
        out_specs=pl.BlockSpec((1, tq, W_MIX), lambda a, i: (a, i, 0)),
        scratch_shapes=[pltpu.VMEM((rows, W_MIX), BF16), pltpu.VMEM((W_MIX, tq), F32),
                        pltpu.VMEM((n_s, tk, tq), F32), pltpu.VMEM((n_s, tk, tq), F32),
                        pltpu.VMEM((N_HEADS, 1, tq), F32), pltpu.VMEM((N_HEADS, 1, tq), F32)],
        compiler_params=_params("arbitrary", "arbitrary"),
        name="causal_sel%d_bias%d" % (sel is not None, fcum is not None),
    )(*args)


def _window_kernel(*refs, tq, wk, pad, window, ls, emit_lse, n_other):
    it = iter(refs)
    q_ref, k_ref, v_ref = next(it), next(it), next(it)
    others = [(next(it), next(it)) for _ in range(n_other)]
    o_ref = next(it)
    lse_ref = next(it) if emit_lse else None
    qm_scr, p_scr = next(it), next(it)
    t0 = pl.program_id(2) * tq
    start = pl.multiple_of(jnp.clip(t0 - pad, 0, ls - wk), LANES)
    _stack_masked(q_ref[0], qm_scr, tq)
    s_all = _dot(qm_scr[...], k_ref[0, pl.ds(start, wk), :], NT)
    t_pos = t0 + lax.broadcasted_iota(jnp.int32, (tq, 1), 0)
    s_pos = start + lax.broadcasted_iota(jnp.int32, (1, wk), 1)
    mask = (s_pos <= t_pos) & (t_pos - s_pos < window)
    ms, ls_ = [], []
    for h in range(N_HEADS):
        rs = slice(h * tq, (h + 1) * tq)
        s = jnp.where(mask, s_all[rs, :], NEG)
        m = jnp.max(s, axis=-1, keepdims=True)
        p = jnp.exp(s - m)
        ms.append(m)
        ls_.append(jnp.sum(p, axis=-1, keepdims=True))
        p_scr[rs, :] = p.astype(BF16)
    pv = _dot(p_scr[...], v_ref[0, pl.ds(start, wk), :])
    if n_other:
        scales, dens = [], []
        lses = [m + jnp.log(l) for m, l in zip(ms, ls_)]
        out = None
        hms = _head_masks()
        for h in range(N_HEADS):
            other_lse = [lr[0, :, h:h + 1] for (_, lr) in others]
            top = lses[h]
            for ol in other_lse:
                top = jnp.maximum(top, ol)
            w_self = jnp.exp(lses[h] - top)
            num = pv[h * tq:(h + 1) * tq, :] * (w_self / ls_[h])
            den = w_self
            for (orf, _), ol in zip(others, other_lse):
                w = jnp.exp(ol - top)
                num = num + w * orf[0]
                den = den + w
            out = jnp.where(hms[h], num / den, 0.0 if out is None else out)
        o_ref[0] = out.astype(o_ref.dtype)
    else:
        o_ref[0] = _pick_heads(pv, tq, [1.0 / l for l in ls_]).astype(o_ref.dtype)
    if emit_lse:
        lane = lax.broadcasted_iota(jnp.int32, (1, LANES), 1)
        tile = jnp.zeros((tq, LANES), F32)
        for h in range(N_HEADS):
            tile = jnp.where(lane == h, ms[h] + jnp.log(ls_[h]), tile)
        lse_ref[0] = tile


def _window_attn(q, k, v, *, dil, window, tq, out_dtype, emit_lse=False, others=()):
    b, s, _ = q.shape
    ls = s // dil
    tq = min(tq, ls)
    pad = -(-(window - 1) // LANES) * LANES
    wk = min(tq + pad, ls)
    rows = N_HEADS * tq
    view = lambda a: a.reshape(b, ls, dil * a.shape[2])
    args = [view(q), view(k), view(v)]
    in_specs = [
        pl.BlockSpec((1, tq, W_MIX), lambda a, r, i: (a, i, r)),
        pl.BlockSpec((1, ls, W_MIX), lambda a, r, i: (a, 0, r)),
        pl.BlockSpec((1, ls, W_MIX), lambda a, r, i: (a, 0, r)),
    ]
    for (o_g, lse_g) in others:
        args += [o_g, lse_g]
        in_specs += [pl.BlockSpec((1, tq, W_MIX), lambda a, r, i: (a, i, r)),
                     pl.BlockSpec((1, tq, LANES), lambda a, r, i: (a, i, r))]
    out_shape = [jax.ShapeDtypeStruct((b, ls, dil * W_MIX), out_dtype)]
    out_specs = [pl.BlockSpec((1, tq, W_MIX), lambda a, r, i: (a, i, r))]
    if emit_lse:
        out_shape.append(jax.ShapeDtypeStruct((b, ls, dil * LANES), F32))
        out_specs.append(pl.BlockSpec((1, tq, LANES), lambda a, r, i: (a, i, r)))
    res = pl.pallas_call(
        functools.partial(_window_kernel, tq=tq, wk=wk, pad=pad, window=window, ls=ls,
                          emit_lse=emit_lse, n_other=len(others)),
        out_shape=out_shape,
        grid=(b, dil, ls // tq),
        in_specs=in_specs,
        out_specs=out_specs,
        scratch_shapes=[pltpu.VMEM((rows, W_MIX), BF16), pltpu.VMEM((rows, wk), BF16)],
        compiler_params=_params("arbitrary", "arbitrary", "arbitrary"),
        name="window_d%d_w%d" % (dil, window),
    )(*args)
    o = res[0].reshape(b, s, W_MIX)
    if emit_lse:
        return o, res[1].reshape(b, s, LANES)
    return o


def _sb_kernel(q_ref, k_ref, v_ref, tri_ref, o_ref, qm_scr, a_scr, carry_scr, acc_scr, *, tq):
    i = pl.program_id(1)
    t0 = i * tq
    rows = N_HEADS * tq
    _stack_masked(q_ref[0], qm_scr, tq)
    carry_scr[...] = jnp.zeros(carry_scr.shape, F32)
    acc_scr[...] = jnp.zeros(acc_scr.shape, F32)
    r = lax.broadcasted_iota(jnp.int32, (rows, 1), 0)
    t_pos = t0 + (r & (tq - 1))

    def tile(kt, diag):
        z = _dot(qm_scr[...], k_ref[0, kt], NT)
        lg = -(jnp.maximum(z, 0.0) + jnp.log(1.0 + jnp.exp(-jnp.abs(z))))
        if diag:
            s_pos = kt * tq + lax.broadcasted_iota(jnp.int32, (1, tq), 1)
            strict = s_pos < t_pos
            lg = jnp.where(strict, lg, 0.0)
        cum = _dot_split(lg, tri_ref[...]) + carry_scr[...]
        a = jnp.exp(z + cum)
        if diag:
            a = jnp.where(strict, a, 0.0)
        a_scr[...] = a.astype(BF16)
        acc_scr[...] += _dot(a_scr[...], v_ref[0, kt])
        carry_scr[...] += jnp.sum(lg, axis=-1, keepdims=True)

    tile(i, True)

    def cond(state):
        j, top = state
        return (j < i) & (top > SB_STOP)

    def body(state):
        j, _ = state
        tile(i - 1 - j, False)
        return j + 1, jnp.max(carry_scr[...])

    lax.while_loop(cond, body, (jnp.int32(0), jnp.max(carry_scr[...])))
    o_ref[0] = _pick_heads(acc_scr[...], tq).astype(o_ref.dtype)


def _stick_breaking(q, k, v, tq):
    b, s, _ = q.shape
    nk = s // tq
    rows = N_HEADS * tq
    tri = jnp.asarray(np.tril(np.ones((tq, tq), np.float32)), BF16)
    return pl.pallas_call(
        functools.partial(_sb_kernel, tq=tq),
        out_shape=jax.ShapeDtypeStruct((b, s, W_MIX), BF16),
        grid=(b, nk),
        in_specs=[
            pl.BlockSpec((1, tq, W_MIX), lambda a, i: (a, i, 0)),
            pl.BlockSpec((1, nk, tq, W_MIX), lambda a, i: (a, 0, 0, 0)),
            pl.BlockSpec((1, nk, tq, W_MIX), lambda a, i: (a, 0, 0, 0)),
            pl.BlockSpec((tq, tq), lambda a, i: (0, 0)),
        ],
        out_specs=pl.BlockSpec((1, tq, W_MIX), lambda a, i: (a, i, 0)),
        scratch_shapes=[pltpu.VMEM((rows, W_MIX), BF16), pltpu.VMEM((rows, tq), BF16),
                        pltpu.VMEM((rows, 1), F32), pltpu.VMEM((rows, W_MIX), F32)],
        compiler_params=_params("arbitrary", "arbitrary"),
        name="stick_breaking",
    )(q, k.reshape(b, nk, tq, W_MIX), v.reshape(b, nk, tq, W_MIX), tri)


def _foxcum_kernel(x_ref, b_ref, tri_ref, o_ref, carry_scr):
    @pl.when(pl.program_id(1) == 0)
    def _():
        carry_scr[...] = jnp.zeros(carry_scr.shape, F32)

    z = x_ref[0] + b_ref[...]
    logf = jnp.minimum(z, 0.0) - jnp.log(1.0 + jnp.exp(-jnp.abs(z)))
    cum = _dot(tri_ref[...], logf, NN, HI) + carry_scr[...]
    o_ref[0] = cum
    carry_scr[...] = cum[cum.shape[0] - 1:, :]


def _fox_cumsum(misc, bias_vec, tc):
    b, s, w = misc.shape
    tri = jnp.asarray(np.tril(np.ones((tc, tc), np.float32)))
    return pl.pallas_call(
        _foxcum_kernel,
        out_shape=jax.ShapeDtypeStruct((b, s, w), F32),
        grid=(b, s // tc),
        in_specs=[
            pl.BlockSpec((1, tc, w), lambda a, i: (a, i, 0)),
            pl.BlockSpec((1, w), lambda a, i: (0, 0)),
            pl.BlockSpec((tc, tc), lambda a, i: (0, 0)),
        ],
        out_specs=pl.BlockSpec((1, tc, w), lambda a, i: (a, i, 0)),
        scratch_shapes=[pltpu.VMEM((1, w), F32)],
        compiler_params=_params("arbitrary", "arbitrary"),
        name="fox_cumsum",
    )(misc, bias_vec, tri)


def _merge_kernel(x_ref, g_ref, sc_ref, sh_ref, ga_ref, wm_ref, misc_ref, pg_ref,
                  ocmp_ref, osel_ref, owin_ref, ob_ref, oc_ref, od_ref,
                  wa_ref, wb_ref, wc_ref, wd_ref, wo_ref, o_ref):
    x = x_ref[...]
    d = x.shape[1]
    h = _mod_norm(x, g_ref[...], sc_ref[0], sh_ref[0]).astype(BF16)
    gate = jax.nn.sigmoid(misc_ref[...])
    o_a = (_dot_split(gate, pg_ref[0]) * ocmp_ref[...].astype(F32)
           + _dot_split(gate, pg_ref[1]) * osel_ref[...].astype(F32)
           + _dot_split(gate, pg_ref[2]) * owin_ref[...].astype(F32)).astype(BF16)
    mixed = jnp.zeros(x.shape, F32)
    for m, (o_m, w_ref) in enumerate(((o_a, wa_ref), (ob_ref[...], wb_ref),
                                      (oc_ref[...], wc_ref), (od_ref[...], wd_ref))):
        y = _dot(o_m, w_ref[...])
        gl = _dot(h, wm_ref[:, m * d:(m + 1) * d])
        mixed = mixed + jax.nn.sigmoid(gl) * y
    o_ref[...] = x + ga_ref[0] * _dot(mixed.astype(BF16), wo_ref[...])


def _merge(x2, g, sc, sh, ga, w_merge, misc, pg, o_cmp, o_sel, o_win, o_b, o_c, o_d,
           wa, wb, wc, wd, wo, seq, tt):
    n, d = x2.shape
    tpb = seq // tt
    row = lambda w: pl.BlockSpec((tt, w), lambda i: (i, 0))
    full = lambda a: pl.BlockSpec(a.shape, lambda i: (0,) * a.ndim)
    per_b = pl.BlockSpec((1, 1, d), lambda i: (i // tpb, 0, 0))
    return pl.pallas_call(
        _merge_kernel,
        out_shape=jax.ShapeDtypeStruct((n, d), F32),
        grid=(n // tt,),
        in_specs=[row(d), full(g), per_b, per_b, per_b, full(w_merge), row(W_MIX), full(pg)]
        + [row(W_MIX)] * 6 + [full(wa), full(wb), full(wc), full(wd), full(wo)],
        out_specs=row(d),
        compiler_params=_params("arbitrary"),
        name="merge_out",
    )(x2, g, sc, sh, ga, w_merge, misc, pg, o_cmp, o_sel, o_win, o_b, o_c, o_d, wa, wb, wc, wd, wo)


def _ffn_kernel(x_ref, g_ref, sc_ref, sh_ref, gf_ref, w1_ref, w3_ref, w2_ref, o_ref, h_scr, acc_scr):
    f = pl.program_id(1)

    @pl.when(f == 0)
    def _():
        h_scr[...] = _mod_norm(x_ref[...], g_ref[...], sc_ref[0], sh_ref[0]).astype(BF16)
        acc_scr[...] = jnp.zeros(acc_scr.shape, F32)

    h = h_scr[...]
    a = _dot(h, w1_ref[...])
    b = _dot(h, w3_ref[...])
    acc_scr[...] += _dot((a * jax.nn.sigmoid(a) * b).astype(BF16), w2_ref[...])

    @pl.when(f == pl.num_programs(1) - 1)
    def _():
        o_ref[...] = x_ref[...] + gf_ref[0] * acc_scr[...]


def _ffn(x2, g, sc, sh, gf, w1, w3, w2, seq, tt, tf):
    n, d = x2.shape
    dff = w1.shape[1]
    tpb = seq // tt
    per_b = pl.BlockSpec((1, 1, d), lambda i, f: (i // tpb, 0, 0))
    return pl.pallas_call(
        _ffn_kernel,
        out_shape=jax.ShapeDtypeStruct((n, d), F32),
        grid=(n // tt, dff // tf),
        in_specs=[
            pl.BlockSpec((tt, d), lambda i, f: (i, 0)),
            pl.BlockSpec((1, d), lambda i, f: (0, 0)),
            per_b, per_b, per_b,
            pl.BlockSpec((d, tf), lambda i, f: (0, f)),
            pl.BlockSpec((d, tf), lambda i, f: (0, f)),
            pl.BlockSpec((tf, d), lambda i, f: (f, 0)),
        ],
        out_specs=pl.BlockSpec((tt, d), lambda i, f: (i, 0)),
        scratch_shapes=[pltpu.VMEM((tt, d), BF16), pltpu.VMEM((tt, d), F32)],
        compiler_params=_params("arbitrary", "arbitrary"),
        name="ffn_swiglu",
    )(x2, g, sc, sh, gf, w1, w3, w2)


def _route_kernel(x_ref, g_ref, sc_ref, sh_ref, rw_ref, up_ref, h_ref, rank_ref, gate_ref, cnt_ref):
    hf = _mod_norm(x_ref[...], g_ref[...], sc_ref[0], sh_ref[0])
    h_ref[...] = hf.astype(BF16)
    logits = _dot(rw_ref[...], hf, NT, HI)
    ne, tt = logits.shape
    e_idx = lax.broadcasted_iota(jnp.int32, (ne, 1), 0).astype(F32)
    v1 = jnp.max(logits, axis=0, keepdims=True)
    i1 = jnp.min(jnp.where(logits == v1, e_idx, float(ne)), axis=0, keepdims=True)
    m1 = e_idx == i1
    rest = jnp.where(m1, -3e38, logits)
    v2 = jnp.max(rest, axis=0, keepdims=True)
    i2 = jnp.min(jnp.where(rest == v2, e_idx, float(ne)), axis=0, keepdims=True)
    m2 = e_idx == i2
    e2 = jnp.exp(v2 - v1)
    g1 = 1.0 / (1.0 + e2)
    g2 = e2 / (1.0 + e2)
    routed = m1 | m2
    rf = jnp.where(routed, 1.0, 0.0)
    rank = _dot(rf.astype(BF16), up_ref[...])
    rank = jnp.where(routed, rank, -1.0)
    gate = jnp.where(m1, g1, 0.0) + jnp.where(m2, g2, 0.0)
    for e in range(ne):
        rank_ref[0, e] = rank[e:e + 1, :]
        gate_ref[0, e] = gate[e:e + 1, :]
    cnt = jnp.sum(rf, axis=1, keepdims=True)
    cnt_ref[0] = jnp.broadcast_to(cnt, (ne, LANES))


def _route(x2, g, sc, sh, rw_t, seq, tt):
    n, d = x2.shape
    ne = rw_t.shape[0]
    tpb = seq // tt
    nt = n // tt
    upper = jnp.asarray(np.triu(np.ones((tt, tt), np.float32), 1), BF16)
    per_b = pl.BlockSpec((1, 1, d), lambda i: (i // tpb, 0, 0))
    return pl.pallas_call(
        _route_kernel,
        out_shape=[jax.ShapeDtypeStruct((n, d), BF16),
                   jax.ShapeDtypeStruct((nt, ne, 1, tt), F32),
                   jax.ShapeDtypeStruct((nt, ne, 1, tt), F32),
                   jax.ShapeDtypeStruct((nt, ne, LANES), F32)],
        grid=(nt,),
        in_specs=[
            pl.BlockSpec((tt, d), lambda i: (i, 0)),
            pl.BlockSpec((1, d), lambda i: (0, 0)),
            per_b, per_b,
            pl.BlockSpec((ne, d), lambda i: (0, 0)),
            pl.BlockSpec((tt, tt), lambda i: (0, 0)),
        ],
        out_specs=[pl.BlockSpec((tt, d), lambda i: (i, 0)),
                   pl.BlockSpec((1, ne, 1, tt), lambda i: (i, 0, 0, 0)),
                   pl.BlockSpec((1, ne, 1, tt), lambda i: (i, 0, 0, 0)),
                   pl.BlockSpec((1, ne, LANES), lambda i: (i, 0, 0))],
        compiler_params=_params("arbitrary"),
        name="moe_route",
    )(x2, g, sc, sh, rw_t, upper)


def _moe_kernel(cnt_ref, x_ref, gf_ref, h_ref, rank_ref, gate_ref, w1_ref, w3_ref, w2_ref,
                o_ref, acc_scr, *, chunk):
    i, e, f = pl.program_id(0), pl.program_id(1), pl.program_id(2)
    ne, nf = pl.num_programs(1), pl.num_programs(2)

    @pl.when((e == 0) & (f == 0))
    def _():
        acc_scr[...] = jnp.zeros(acc_scr.shape, F32)

    count = cnt_ref[i * ne + e]
    rank = rank_ref[0, 0]
    gate = gate_ref[0, 0]
    h = h_ref[...]

    def body(c, carry):
        r = c * chunk + lax.broadcasted_iota(jnp.int32, (chunk, 1), 0)
        hit = rank == r.astype(F32)
        p = jnp.where(hit, 1.0, 0.0).astype(BF16)
        xs = _dot(p, h).astype(BF16)
        a = _dot(xs, w1_ref[0])
        b = _dot(xs, w3_ref[0])
        y = _dot((a * jax.nn.sigmoid(a) * b).astype(BF16), w2_ref[0])
        gcol = jnp.sum(jnp.where(hit, gate, 0.0), axis=-1, keepdims=True)
        acc_scr[...] += _dot(p, (y * gcol).astype(BF16), TN)
        return carry

    lax.fori_loop(0, (count + chunk - 1) // chunk, body, 0)

    @pl.when((e == ne - 1) & (f == nf - 1))
    def _():
        o_ref[...] = x_ref[...] + gf_ref[0] * acc_scr[...]


def _moe(counts, x2, gf, h2, rank, gate, w1, w3, w2, seq, tt, tf, chunk):
    n, d = x2.shape
    ne, _, dff = w1.shape
    tpb = seq // tt
    grid_spec = pltpu.PrefetchScalarGridSpec(
        num_scalar_prefetch=1,
        grid=(n // tt, ne, dff // tf),
        in_specs=[
            pl.BlockSpec((tt, d), lambda i, e, f, c: (i, 0)),
            pl.BlockSpec((1, 1, d), lambda i, e, f, c: (i // tpb, 0, 0)),
            pl.BlockSpec((tt, d), lambda i, e, f, c: (i, 0)),
            pl.BlockSpec((1, 1, 1, tt), lambda i, e, f, c: (i, e, 0, 0)),
            pl.BlockSpec((1, 1, 1, tt), lambda i, e, f, c: (i, e, 0, 0)),
            pl.BlockSpec((1, d, tf), lambda i, e, f, c: (e, 0, f)),
            pl.BlockSpec((1, d, tf), lambda i, e, f, c: (e, 0, f)),
            pl.BlockSpec((1, tf, d), lambda i, e, f, c: (e, f, 0)),
        ],
        out_specs=pl.BlockSpec((tt, d), lambda i, e, f, c: (i, 0)),
        scratch_shapes=[pltpu.VMEM((tt, d), F32)],
    )
    return pl.pallas_call(
        functools.partial(_moe_kernel, chunk=chunk),
        out_shape=jax.ShapeDtypeStruct((n, d), F32),
        grid_spec=grid_spec,
        compiler_params=_params("arbitrary", "arbitrary", "arbitrary"),
        name="moe_experts",
    )(counts, x2, gf, h2, rank, gate, w1, w3, w2)


def _overlap_matrix(ncp, nc, n_sel):
    c0 = np.arange(ncp) * CMP_STRIDE
    c1 = c0 + CMP_LEN
    s0 = np.arange(n_sel) * SEL_LEN
    s1 = s0 + SEL_LEN
    ov = ((c0[:, None] < s1[None, :]) & (c1[:, None] > s0[None, :])).astype(np.float32)
    ov[nc:] = 0.0
    return jnp.asarray(ov)


def _gate_expand():
    pg = np.zeros((3, W_MIX, W_MIX), np.float32)
    for br in range(3):
        for h in range(N_HEADS):
            pg[br, GATE_LANE + 3 * h + br, h * HEAD_DIM:(h + 1) * HEAD_DIM] = 1.0
    return jnp.asarray(pg, BF16)


def _mixer_layer(x2, b, s, mod, norm_g, rope, w_in, qk_gain, pe_k, pe_v, ck1, ck2, cv1, cv2,
                 fox_b, w_branch, w_out):
    n, d = x2.shape
    sh_a, sc_a, g_a = mod[0], mod[1], mod[2]
    w_slab, gain, w_merge = _pack_w_in(w_in, qk_gain)
    bd = jnp.asarray(np.kron(np.eye(N_HEADS), np.full((HEAD_DIM, HEAD_DIM), 1.0 / HEAD_DIM)), BF16)
    sl = _proj(x2, norm_g, sc_a, sh_a, w_slab, gain, bd, rope, s, min(512, s))
    sl = [a.reshape(b, s, W_MIX) for a in sl]
    misc = sl[S_MISC]

    nch = s // CMP_STRIDE
    nc = nch - CMP_LEN // CMP_STRIDE + 1
    kc_raw = sl[S_KC][..., :HEAD_DIM]
    vc_raw = misc[..., VC_LANE:VC_LANE + HEAD_DIM]
    chunks = jnp.stack([kc_raw, vc_raw]).reshape(2, b, nch, CMP_STRIDE * HEAD_DIM)
    chunks_next = jnp.concatenate([chunks[:, :, 1:], jnp.zeros_like(chunks[:, :, :1])], axis=2)
    pe = jnp.stack([pe_k, pe_v]).reshape(2, 1, CMP_LEN * HEAD_DIM)
    kvc = jnp.tile(_compress(chunks, chunks_next, pe, jnp.stack([ck1, cv1]), jnp.stack([ck2, cv2])),
                   (1, 1, 1, N_HEADS))
    overlap = _overlap_matrix(nch, nc, s // SEL_LEN)
    o_cmp, selmask = _cmp_topk(sl[S_QNR], kvc[0], kvc[1], overlap, min(256, s), nc)
    tq, tk = min(256, s), min(512, s)
    o_sel = _causal_attn(sl[S_QR], sl[S_KSL], sl[S_VSL], tq, tk, sel=selmask)
    o_win = _window_attn(sl[S_QR], sl[S_KW], sl[S_VW], dil=1, window=NSA_WINDOW, tq=256, out_dtype=BF16)

    others = []
    for (wdw, dil) in DIL_CONFIGS[:0:-1]:
        others.append(_window_attn(sl[S_BQ], sl[S_BK], sl[S_BV], dil=dil, window=wdw // dil + 1, tq=256,
                                   out_dtype=F32, emit_lse=True))
    wdw, dil = DIL_CONFIGS[0]
    o_b = _window_attn(sl[S_BQ], sl[S_BK], sl[S_BV], dil=dil, window=wdw // dil + 1, tq=256,
                       out_dtype=BF16, others=others)

    o_c = _stick_breaking(sl[S_CQ], sl[S_CK], sl[S_CV], min(256, s))

    bias_vec = jnp.zeros((1, W_MIX), F32).at[0, FOX_LANE:FOX_LANE + N_HEADS].set(fox_b)
    fcum = _fox_cumsum(misc, bias_vec, min(512, s))
    o_d = _causal_attn(sl[S_DQ], sl[S_DK], sl[S_DV], tq, tk, fcum=fcum[..., FOX_LANE:FOX_LANE + N_HEADS])

    wb16 = w_branch.astype(BF16)
    flat = lambda a: a.reshape(n, W_MIX)
    return _merge(x2, norm_g, sc_a, sh_a, g_a, w_merge, flat(misc), _gate_expand(),
                  flat(o_cmp), flat(o_sel), flat(o_win), flat(o_b), flat(o_c), flat(o_d),
                  wb16[0], wb16[1], wb16[2], wb16[3], w_out.astype(BF16), s, min(512, s))


def kernel(x, c, positions, w_ada, b_ada, norm_mix, norm_ffn, w_in, qk_gain, nsa_pe_k, nsa_pe_v,
           nsa_ck_w1, nsa_ck_w2, nsa_cv_w1, nsa_cv_w2, fox_bias, w_branch, w_out,
           ffn_w1, ffn_w3, ffn_w2, router_w, moe_w1, moe_w3, moe_w2):
    b, s, d = x.shape
    depth = w_ada.shape[0]
    rope = _rope_tables(positions)
    mods = _ada(c, w_ada, b_ada).reshape(depth, b, 6, 1, d).transpose(0, 2, 1, 3, 4)
    x2 = x.reshape(b * s, d)
    for l in range(depth):
        mod = mods[l]
        x2 = _mixer_layer(x2, b, s, mod[0:3], norm_mix[l].reshape(1, d), rope, w_in[l], qk_gain[l],
                          nsa_pe_k[l], nsa_pe_v[l], nsa_ck_w1[l], nsa_ck_w2[l], nsa_cv_w1[l],
                          nsa_cv_w2[l], fox_bias[l], w_branch[l], w_out[l])
        sh_f, sc_f, g_f = mod[3], mod[4], mod[5]
        gn = norm_ffn[l].reshape(1, d)
        e = l // 2
        if l % 2 == 0:
            dff = ffn_w1.shape[2]
            x2 = _ffn(x2, gn, sc_f, sh_f, g_f, ffn_w1[e].astype(BF16), ffn_w3[e].astype(BF16),
                      ffn_w2[e].astype(BF16), s, min(1024, s), dff // 2)
        else:
            tt = min(1024, s)
            dff = moe_w1.shape[3]
            h2, rank, gate, cnt = _route(x2, gn, sc_f, sh_f, router_w[e].T, s, tt)
            counts = cnt[:, :, 0].astype(jnp.int32).reshape(-1)
            x2 = _moe(counts, x2, g_f, h2, rank, gate, moe_w1[e].astype(BF16), moe_w3[e].astype(BF16),
                      moe_w2[e].astype(BF16), s, tt, dff // 2, 256)
    return x2.reshape(b, s, d)
```

```python
import functools

import numpy as np
import jax
import jax.numpy as jnp
from jax import lax
from jax.experimental import pallas as pl
from jax.experimental.pallas import tpu as pltpu

F32 = jnp.float32
BF16 = jnp.bfloat16
HI = lax.Precision.HIGHEST

LANES = 128
VMEM_LIMIT = 52 * 1024 * 1024

HEAD_DIM = 64
HEAD_SHIFT = 6
N_HEADS = 4
W_MIX = N_HEADS * HEAD_DIM
N_MIXERS = 4
ROPE_THETA = 500000.0
ROPE_DIMS = HEAD_DIM // 4
ROPE_HALF = ROPE_DIMS // 2
EPS = 1e-6
NEG = -1e30
BIG = 1e30
CMP_LEN = 32
CMP_STRIDE = 16
SEL_LEN = 64
SEL_SHIFT = 6
TOPN = 16
NSA_WINDOW = 512
DIL_CONFIGS = ((128, 1), (512, 4), (2048, 16))
SB_STOP = -110.0
FOX_STOP = 108.0
GATE_LANE = 0
FOX_LANE = 3 * N_HEADS
VC_LANE = HEAD_DIM

NN = (((1,), (0,)), ((), ()))
NT = (((1,), (1,)), ((), ()))
TN = (((0,), (0,)), ((), ()))


def _dot(a, b, dims=NN, precision=None):
    return lax.dot_general(a, b, dims, precision=precision, preferred_element_type=F32)


def _dot_split(a, b_bf16, dims=NN):
    hi = a.astype(BF16)
    lo = (a - hi.astype(F32)).astype(BF16)
    return _dot(hi, b_bf16, dims) + _dot(lo, b_bf16, dims)


def _params(*sem):
    return pltpu.CompilerParams(dimension_semantics=sem, vmem_limit_bytes=VMEM_LIMIT)


def _mod_norm(x, g, sc, sh):
    ms = jnp.mean(x * x, axis=-1, keepdims=True)
    return (x * lax.rsqrt(ms + EPS) * g) * (1.0 + sc) + sh


def _head_masks():
    lane = lax.broadcasted_iota(jnp.int32, (1, W_MIX), 1)
    return [(lane >> HEAD_SHIFT) == h for h in range(N_HEADS)]


def _stack_masked(q, qm_scr, tq):
    for h, hm in enumerate(_head_masks()):
        qm_scr[h * tq:(h + 1) * tq, :] = jnp.where(hm, q, jnp.zeros_like(q))


def _pick_heads(stacked, tq, scale=None):
    out = None
    for h, hm in enumerate(_head_masks()):
        blk = stacked[h * tq:(h + 1) * tq, :]
        if scale is not None:
            blk = blk * scale[h]
        out = jnp.where(hm, blk, 0.0 if out is None else out)
    return out


def _ada_kernel(c_ref, w_ref, b_ref, o_ref):
    c = c_ref[...]
    ca = c * jax.nn.sigmoid(c)
    o_ref[0] = _dot(ca, w_ref[0], NN, HI) + b_ref[0]


def _ada(c, w_ada, b_ada):
    depth, d, n6 = w_ada.shape
    b = c.shape[0]
    tn = n6 // 4
    return pl.pallas_call(
        _ada_kernel,
        out_shape=jax.ShapeDtypeStruct((depth, b, n6), F32),
        grid=(depth, n6 // tn),
        in_specs=[
            pl.BlockSpec((b, d), lambda l, j: (0, 0)),
            pl.BlockSpec((1, d, tn), lambda l, j: (l, 0, j)),
            pl.BlockSpec((1, 1, tn), lambda l, j: (l, 0, j)),
        ],
        out_specs=pl.BlockSpec((1, b, tn), lambda l, j: (l, 0, j)),
        compiler_params=_params("arbitrary", "arbitrary"),
        name="ada_mod",
    )(c, w_ada, b_ada.reshape(depth, 1, n6))


_SLABS = (
    (True, False, F32), (True, True, BF16), (True, False, F32),
    (True, True, BF16), (True, True, BF16), (False, False, BF16), (False, False, BF16),
    (True, True, BF16), (True, True, BF16), (False, False, BF16),
    (False, False, BF16), (False, False, BF16), (False, False, BF16),
    (True, False, BF16), (True, False, BF16), (False, False, BF16),
    (False, False, F32),
)
N_SLABS = len(_SLABS)
(S_QNR, S_QR, S_KC, S_KSL, S_KW, S_VSL, S_VW, S_BQ, S_BK, S_BV, S_CQ, S_CK, S_CV,
 S_DQ, S_DK, S_DV, S_MISC) = range(N_SLABS)


def _proj_kernel(x_ref, g_ref, sc_ref, sh_ref, w_ref, gain_ref, bd_ref, rope_ref, *out_refs):
    h = _mod_norm(x_ref[...], g_ref[...], sc_ref[0], sh_ref[0]).astype(BF16)
    bd = bd_ref[...]
    for s, (norm, rope, _) in enumerate(_SLABS):
        y = _dot(h, w_ref[:, s * W_MIX:(s + 1) * W_MIX])
        if norm:
            ms = _dot_split(y * y, bd)
            y = y * lax.rsqrt(ms + EPS)
        y = y * gain_ref[s]
        if rope:
            y = (y * rope_ref[0, 0] + pltpu.roll(y, W_MIX - ROPE_HALF, 1) * rope_ref[0, 1]
                 + pltpu.roll(y, ROPE_HALF, 1) * rope_ref[0, 2])
        out_refs[s][...] = y.astype(out_refs[s].dtype)


def _proj(x2, g, sc, sh, w_slab, gain, bd, rope, seq, tt):
    n, d = x2.shape
    tpb = seq // tt
    out_shape = [jax.ShapeDtypeStruct((n, W_MIX), dt) for (_, _, dt) in _SLABS]
    return pl.pallas_call(
        _proj_kernel,
        out_shape=out_shape,
        grid=(n // tt,),
        in_specs=[
            pl.BlockSpec((tt, d), lambda i: (i, 0)),
            pl.BlockSpec((1, d), lambda i: (0, 0)),
            pl.BlockSpec((1, 1, d), lambda i: (i // tpb, 0, 0)),
            pl.BlockSpec((1, 1, d), lambda i: (i // tpb, 0, 0)),
            pl.BlockSpec((d, N_SLABS * W_MIX), lambda i: (0, 0)),
            pl.BlockSpec((N_SLABS, 1, W_MIX), lambda i: (0, 0, 0)),
            pl.BlockSpec((W_MIX, W_MIX), lambda i: (0, 0)),
            pl.BlockSpec((1, 3, tt, W_MIX), lambda i: (i // tpb, 0, i % tpb, 0)),
        ],
        out_specs=[pl.BlockSpec((tt, W_MIX), lambda i: (i, 0)) for _ in _SLABS],
        compiler_params=_params("arbitrary"),
        name="in_proj",
    )(x2, g, sc, sh, w_slab, gain, bd, rope)


def _pack_w_in(w_in, qk_gain):
    d = w_in.shape[0]
    o = 0
    a_q = w_in[:, o:o + W_MIX]; o += W_MIX
    a_kv = w_in[:, o:o + 6 * HEAD_DIM]; o += 6 * HEAD_DIM
    a_g = w_in[:, o:o + 3 * N_HEADS]; o += 3 * N_HEADS
    b_qkv = w_in[:, o:o + 3 * W_MIX]; o += 3 * W_MIX
    c_qkv = w_in[:, o:o + 3 * W_MIX]; o += 3 * W_MIX
    d_qkv = w_in[:, o:o + 3 * W_MIX]; o += 3 * W_MIX
    d_f = w_in[:, o:o + N_HEADS]; o += N_HEADS
    w_merge = w_in[:, o:]
    kc, vc, ksl, vsl, kw, vw = (a_kv[:, i * HEAD_DIM:(i + 1) * HEAD_DIM] for i in range(6))
    zeros = lambda w: jnp.zeros((d, w), w_in.dtype)
    rep = lambda w: jnp.tile(w, (1, N_HEADS))
    misc = jnp.concatenate([a_g, d_f, zeros(VC_LANE - FOX_LANE - N_HEADS), vc, zeros(W_MIX - 2 * HEAD_DIM)], axis=1)
    slabs = [
        a_q, a_q, jnp.concatenate([kc, zeros(W_MIX - HEAD_DIM)], axis=1),
        rep(ksl), rep(kw), rep(vsl), rep(vw),
        b_qkv[:, :W_MIX], b_qkv[:, W_MIX:2 * W_MIX], b_qkv[:, 2 * W_MIX:],
        c_qkv[:, :W_MIX], c_qkv[:, W_MIX:2 * W_MIX], c_qkv[:, 2 * W_MIX:],
        d_qkv[:, :W_MIX], d_qkv[:, W_MIX:2 * W_MIX], d_qkv[:, 2 * W_MIX:],
        misc,
    ]
    w_slab = jnp.concatenate(slabs, axis=1).astype(BF16)
    scale = HEAD_DIM ** -0.5
    t4 = lambda gvec: jnp.tile(gvec, N_HEADS)
    one = jnp.ones((W_MIX,), F32)
    gains = [
        t4(qk_gain[0]) * scale, t4(qk_gain[0]) * scale,
        jnp.concatenate([qk_gain[1], jnp.ones((W_MIX - HEAD_DIM,), F32)]),
        t4(qk_gain[2]), t4(qk_gain[3]), one, one,
        t4(qk_gain[4]) * scale, t4(qk_gain[5]), one,
        one * scale, one, one,
        t4(qk_gain[6]) * scale, t4(qk_gain[7]), one,
        one,
    ]
    gain = jnp.stack(gains).reshape(N_SLABS, 1, W_MIX).astype(F32)
    return w_slab, gain, w_merge.astype(BF16)


def _rope_tables(positions):
    inv = ROPE_THETA ** (-jnp.arange(0, ROPE_DIMS, 2, dtype=F32) / ROPE_DIMS)
    ang = positions.astype(F32)[..., None] * inv
    cos, sin = jnp.cos(ang), jnp.sin(ang)
    b, s, _ = cos.shape
    pad1 = jnp.ones((b, s, HEAD_DIM - ROPE_DIMS), F32)
    pad0 = jnp.zeros((b, s, HEAD_DIM - ROPE_DIMS), F32)
    z8 = jnp.zeros_like(sin)
    c64 = jnp.concatenate([cos, cos, pad1], axis=-1)
    s1_64 = jnp.concatenate([-sin, z8, pad0], axis=-1)
    s2_64 = jnp.concatenate([z8, sin, pad0], axis=-1)
    heads = lambda t: jnp.tile(t, (1, 1, N_HEADS))
    return jnp.stack([heads(c64), heads(s1_64), heads(s2_64)], axis=1)


def _compress_kernel(a_ref, b_ref, pe_ref, w1_ref, w2_ref, o_ref):
    half = w1_ref.shape[1] // 2
    w1 = w1_ref[0]
    hid = (_dot(a_ref[0, 0], w1[:half], NN, HI) + _dot(b_ref[0, 0], w1[half:], NN, HI)
           + _dot(pe_ref[0], w1, NN, HI))
    o_ref[0, 0] = _dot(jax.nn.gelu(hid), w2_ref[0], NN, HI)


def _compress(ch, chn, pe, w1, w2):
    _, b, ncp, cw = ch.shape
    hid = w1.shape[2]
    return pl.pallas_call(
        _compress_kernel,
        out_shape=jax.ShapeDtypeStruct((2, b, ncp, HEAD_DIM), F32),
        grid=(2, b),
        in_specs=[
            pl.BlockSpec((1, 1, ncp, cw), lambda k, i: (k, i, 0, 0)),
            pl.BlockSpec((1, 1, ncp, cw), lambda k, i: (k, i, 0, 0)),
            pl.BlockSpec((1, 1, 2 * cw), lambda k, i: (k, 0, 0)),
            pl.BlockSpec((1, 2 * cw, hid), lambda k, i: (k, 0, 0)),
            pl.BlockSpec((1, hid, HEAD_DIM), lambda k, i: (k, 0, 0)),
        ],
        out_specs=pl.BlockSpec((1, 1, ncp, HEAD_DIM), lambda k, i: (k, i, 0, 0)),
        compiler_params=_params("arbitrary", "arbitrary"),
        name="nsa_compress",
    )(ch, chn, pe, w1, w2)


def _cmp_kernel(q_ref, kc_ref, vc_ref, ov_ref, o_ref, sel_ref, qm_scr, *, tq, nc, n_sel):
    i = pl.program_id(1)
    rows = N_HEADS * tq
    ncp = kc_ref.shape[1]
    _stack_masked(q_ref[0], qm_scr, tq)
    s = _dot(qm_scr[...], kc_ref[0], NT, HI)
    r = lax.broadcasted_iota(jnp.int32, (rows, 1), 0)
    t = i * tq + (r & (tq - 1))
    c = lax.broadcasted_iota(jnp.int32, (1, ncp), 1)
    mask = (c * CMP_STRIDE + (CMP_LEN - 1) <= t) & (c < nc)
    sm = jnp.where(mask, s, NEG)
    m = jnp.max(sm, axis=-1, keepdims=True)
    e = jnp.where(mask, jnp.exp(sm - m), 0.0)
    l = jnp.sum(e, axis=-1, keepdims=True)
    p = e / jnp.maximum(l, 1e-30)
    o_ref[0] = _pick_heads(_dot(p.astype(BF16), vc_ref[0].astype(BF16)), tq).astype(o_ref.dtype)
    psum = p[0:tq] + p[tq:2 * tq] + p[2 * tq:3 * tq] + p[3 * tq:4 * tq]
    imp = _dot(psum, ov_ref[...], NN, HI)
    tt = i * tq + lax.broadcasted_iota(jnp.int32, (tq, 1), 0)
    j = lax.broadcasted_iota(jnp.int32, (1, n_sel), 1)
    cur = tt >> SEL_SHIFT
    valid = j <= cur
    forced = (j == 0) | (j == cur) | (j == cur - 1)
    score = jnp.where(valid, jnp.where(forced, BIG, imp), NEG)
    sel = jnp.zeros((tq, n_sel), F32)
    jf = j.astype(F32)
    for _ in range(min(TOPN, n_sel)):
        mx = jnp.max(score, axis=-1, keepdims=True)
        idx = jnp.min(jnp.where(score == mx, jf, float(n_sel)), axis=-1, keepdims=True)
        pick = jf == idx
        sel = jnp.where(pick, 1.0, sel)
        score = jnp.where(pick, -3e38, score)
    sel_ref[0] = sel.astype(sel_ref.dtype)


def _cmp_topk(q_nr, kc_rep, vc_rep, overlap, tq, nc):
    b, seq, _ = q_nr.shape
    ncp = kc_rep.shape[1]
    n_sel = seq // SEL_LEN
    return pl.pallas_call(
        functools.partial(_cmp_kernel, tq=tq, nc=nc, n_sel=n_sel),
        out_shape=[jax.ShapeDtypeStruct((b, seq, W_MIX), BF16),
                   jax.ShapeDtypeStruct((b, seq, n_sel), BF16)],
        grid=(b, seq // tq),
        in_specs=[
            pl.BlockSpec((1, tq, W_MIX), lambda g, i: (g, i, 0)),
            pl.BlockSpec((1, ncp, W_MIX), lambda g, i: (g, 0, 0)),
            pl.BlockSpec((1, ncp, W_MIX), lambda g, i: (g, 0, 0)),
            pl.BlockSpec((ncp, n_sel), lambda g, i: (0, 0)),
        ],
        out_specs=[pl.BlockSpec((1, tq, W_MIX), lambda g, i: (g, i, 0)),
                   pl.BlockSpec((1, tq, n_sel), lambda g, i: (g, i, 0))],
        scratch_shapes=[pltpu.VMEM((N_HEADS * tq, W_MIX), F32)],
        compiler_params=_params("arbitrary", "arbitrary"),
        name="nsa_cmp_topk",
    )(q_nr, kc_rep, vc_rep, overlap)


def _causal_kernel(*refs, tq, tk, has_sel, has_bias):
    it = iter(refs)
    q_ref, k_ref, vt_ref = next(it), next(it), next(it)
    sel_ref = next(it) if has_sel else None
    fa_ref, fb_ref, fq_ref, thr_ref = (next(it) for _ in range(4)) if has_bias else (None,) * 4
    o_ref, qm_scr, acc_scr, sa_scr, sb_scr, m_scr, l_scr = (next(it) for _ in range(7))
    i = pl.program_id(1)
    t0 = i * tq
    _stack_masked(q_ref[0], qm_scr, tq)
    acc_scr[...] = jnp.zeros(acc_scr.shape, F32)
    t_pos = t0 + lax.broadcasted_iota(jnp.int32, (1, tq), 1)

    n_s = N_HEADS + (1 if has_sel else 0)

    def scores(kt, s_buf):
        k_t = k_ref[0, kt]
        for h in range(N_HEADS):
            s = _dot(k_t, qm_scr[h * tq:(h + 1) * tq, :], NT)
            if has_bias:
                s = s + _dot(fa_ref[0, kt], fb_ref[0, h])
            s_buf[h] = s
        if has_sel:
            s_pos = kt * tk + lax.broadcasted_iota(jnp.int32, (tk, 1), 0)
            blk = lax.broadcasted_iota(jnp.int32, (1, sel_ref.shape[2]), 1)
            expand = jnp.where((s_pos >> SEL_SHIFT) == blk, 1.0, 0.0).astype(BF16)
            s_buf[N_HEADS] = _dot(expand, sel_ref[0], NT)

    def update(kt, s_buf, diag):
        vt_t = vt_ref[0, kt]
        mask = None
        if diag:
            s_pos = kt * tk + lax.broadcasted_iota(jnp.int32, (tk, 1), 0)
            mask = s_pos <= t_pos
        if has_sel:
            picked = s_buf[N_HEADS] > 0.5
            mask = picked if mask is None else (mask & picked)
        ps, alphas = [], []
        for h in range(N_HEADS):
            s = s_buf[h]
            if mask is not None:
                s = jnp.where(mask, s, NEG)
            m_old = m_scr[h]
            m_new = jnp.maximum(m_old, jnp.max(s, axis=0, keepdims=True))
            p = jnp.exp(s - m_new)
            alpha = jnp.exp(m_old - m_new)
            m_scr[h] = m_new
            l_scr[h] = alpha * l_scr[h] + jnp.sum(p, axis=0, keepdims=True)
            ps.append(p.astype(BF16))
            alphas.append(alpha)
        for h in range(N_HEADS):
            rs = slice(h * HEAD_DIM, (h + 1) * HEAD_DIM)
            acc_scr[rs, :] = alphas[h] * acc_scr[rs, :] + _dot(vt_t[rs, :], ps[h])

    def live(kt_next):
        lane = lax.broadcasted_iota(jnp.int32, (1, thr_ref.shape[3]), 1)
        hit = None
        for h in range(N_HEADS):
            top = jnp.max(fq_ref[0, h] - m_scr[h], axis=-1, keepdims=True)
            need = top >= thr_ref[0, h]
            hit = need if hit is None else (hit | need)
        return jnp.max(jnp.where(hit & (lane == kt_next), 1.0, 0.0)) > 0.5

    n_last = t0 // tk
    m_scr[...] = jnp.full(m_scr.shape, NEG, F32)
    l_scr[...] = jnp.zeros(l_scr.shape, F32)
    scores(n_last, sa_scr)
    scores(jnp.maximum(n_last - 1, 0), sb_scr)
    update(n_last, sa_scr, True)

    def pair(j):
        kt = n_last - 1 - 2 * j
        scores(kt - 1, sa_scr)
        update(kt, sb_scr, False)
        scores(jnp.maximum(kt - 2, 0), sb_scr)
        update(kt - 1, sa_scr, False)
        return kt - 2

    n_pairs = n_last // 2
    if has_bias:
        def cond(state):
            j, go = state
            return (j < n_pairs) & go

        def body(state):
            j, _ = state
            return j + 1, live(pair(j))

        _, go = lax.while_loop(cond, body, (jnp.int32(0), live(n_last - 1)))
    else:
        lax.fori_loop(0, n_pairs, lambda j, c: (pair(j), c)[1], 0)
        go = True

    @pl.when((n_last % 2 == 1) & go)
    def _():
        update(0, sb_scr, False)

    ls = [l_scr[h] for h in range(N_HEADS)]
    for h in range(N_HEADS):
        rs = slice(h * HEAD_DIM, (h + 1) * HEAD_DIM)
        acc_scr[rs, :] = acc_scr[rs, :] / ls[h]
    o_ref[0] = acc_scr[...].T.astype(o_ref.dtype)


def _split3(x):
    def cut(v):
        bits = lax.bitcast_convert_type(v, jnp.uint32) & jnp.uint32(0xFFFF0000)
        return lax.bitcast_convert_type(bits, F32)

    hi = cut(x)
    r1 = x - hi
    mid = cut(r1)
    lo = r1 - mid
    return hi.astype(BF16), mid.astype(BF16), lo.astype(BF16)


def _causal_attn(q, k, v, tq, tk, sel=None, fcum=None, qk_bound=None):
    b, s, _ = q.shape
    nk = s // tk
    rows = N_HEADS * tq
    n_s = N_HEADS + (1 if sel is not None else 0)
    vt = v.reshape(b, nk, tk, W_MIX).transpose(0, 1, 3, 2)
    args = [q, k.reshape(b, nk, tk, W_MIX), vt]
    in_specs = [
        pl.BlockSpec((1, tq, W_MIX), lambda a, i: (a, i, 0)),
        pl.BlockSpec((1, nk, tk, W_MIX), lambda a, i: (a, 0, 0, 0)),
        pl.BlockSpec((1, nk, W_MIX, tk), lambda a, i: (a, 0, 0, 0)),
    ]
    if sel is not None:
        args.append(sel)
        in_specs.append(pl.BlockSpec((1, tq, sel.shape[2]), lambda a, i: (a, i, 0)))
    if fcum is not None:
        nf = 2 * 3 + 2
        ones = jnp.ones(fcum.shape, BF16)
        zero = jnp.zeros(fcum.shape, BF16)
        parts = _split3(fcum)
        key_f = jnp.stack([ones, ones, ones] + [-p for p in parts] + [zero, zero], axis=-1)
        qry_f = jnp.stack(list(parts) + [ones, ones, ones, zero, zero], axis=-1)
        fa = key_f.reshape(b, nk, tk, N_HEADS * nf)
        eye = jnp.eye(N_HEADS, dtype=BF16)
        fb = jnp.einsum('bshf,hg->bgshf', qry_f, eye).reshape(b, N_HEADS, s, N_HEADS * nf).transpose(0, 1, 3, 2)
        f_rows = fcum.transpose(0, 2, 1)
        f_end = f_rows[:, :, tk - 1::tk]
        thr = jnp.pad(f_end - (qk_bound + FOX_STOP), ((0, 0), (0, 0), (0, LANES - nk)),
                      constant_values=BIG).reshape(b, N_HEADS, 1, LANES)
        args += [fa, fb, f_rows.reshape(b, N_HEADS, 1, s), thr]
        in_specs += [pl.BlockSpec((1, nk, tk, N_HEADS * nf), lambda a, i: (a, 0, 0, 0)),
                     pl.BlockSpec((1, N_HEADS, N_HEADS * nf, tq), lambda a, i: (a, 0, 0, i)),
                     pl.BlockSpec((1, N_HEADS, 1, tq), lambda a, i: (a, 0, 0, i)),
                     pl.BlockSpec((1, N_HEADS, 1, LANES), lambda a, i: (a, 0, 0, 0))]
    return pl.pallas_call(
        functools.partial(_causal_kernel, tq=tq, tk=tk, has_sel=sel is not None, has_bias=fcum is not None),
        out_shape=jax.ShapeDtypeStruct((b, s, W_MIX), BF16),
        grid=(b, s // tq),
        in_specs=in_specs,
        out_specs=pl.BlockSpec((1, tq, W_MIX), lambda a, i: (a, i, 0)),
        scratch_shapes=[pltpu.VMEM((rows, W_MIX), BF16), pltpu.VMEM((W_MIX, tq), F32),
                        pltpu.VMEM((n_s, tk, tq), F32), pltpu.VMEM((n_s, tk, tq), F32),
                        pltpu.VMEM((N_HEADS, 1, tq), F32), pltpu.VMEM((N_HEADS, 1, tq), F32)],
        compiler_params=_params("arbitrary", "arbitrary"),
        name="causal_sel%d_bias%d" % (sel is not None, fcum is not None),
    )(*args)


def _window_kernel(*refs, tq, wk, pad, window, ls, emit_lse, n_other):
    it = iter(refs)
    q_ref, k_ref, v_ref = next(it), next(it), next(it)
    others = [(next(it), next(it)) for _ in range(n_other)]
    o_ref = next(it)
    lse_ref = next(it) if emit_lse else None
    qm_scr, p_scr = next(it), next(it)
    t0 = pl.program_id(2) * tq
    start = pl.multiple_of(jnp.clip(t0 - pad, 0, ls - wk), LANES)
    _stack_masked(q_ref[0], qm_scr, tq)
    s_all = _dot(qm_scr[...], k_ref[0, pl.ds(start, wk), :], NT)
    t_pos = t0 + lax.broadcasted_iota(jnp.int32, (tq, 1), 0)
    s_pos = start + lax.broadcasted_iota(jnp.int32, (1, wk), 1)
    mask = (s_pos <= t_pos) & (t_pos - s_pos < window)
    ms, ls_ = [], []
    for h in range(N_HEADS):
        rs = slice(h * tq, (h + 1) * tq)
        s = jnp.where(mask, s_all[rs, :], NEG)
        m = jnp.max(s, axis=-1, keepdims=True)
        p = jnp.exp(s - m)
        ms.append(m)
        ls_.append(jnp.sum(p, axis=-1, keepdims=True))
        p_scr[rs, :] = p.astype(BF16)
    pv = _dot(p_scr[...], v_ref[0, pl.ds(start, wk), :])
    if n_other:
        scales, dens = [], []
        lses = [m + jnp.log(l) for m, l in zip(ms, ls_)]
        out = None
        hms = _head_masks()
        for h in range(N_HEADS):
            other_lse = [lr[0, :, h:h + 1] for (_, lr) in others]
            top = lses[h]
            for ol in other_lse:
                top = jnp.maximum(top, ol)
            w_self = jnp.exp(lses[h] - top)
            num = pv[h * tq:(h + 1) * tq, :] * (w_self / ls_[h])
            den = w_self
            for (orf, _), ol in zip(others, other_lse):
                w = jnp.exp(ol - top)
                num = num + w * orf[0]
                den = den + w
            out = jnp.where(hms[h], num / den, 0.0 if out is None else out)
        o_ref[0] = out.astype(o_ref.dtype)
    else:
        o_ref[0] = _pick_heads(pv, tq, [1.0 / l for l in ls_]).astype(o_ref.dtype)
    if emit_lse:
        lane = lax.broadcasted_iota(jnp.int32, (1, LANES), 1)
        tile = jnp.zeros((tq, LANES), F32)
        for h in range(N_HEADS):
            tile = jnp.where(lane == h, ms[h] + jnp.log(ls_[h]), tile)
        lse_ref[0] = tile


def _window_attn(q, k, v, *, dil, window, tq, out_dtype, emit_lse=False, others=()):
    b, s, _ = q.shape
    ls = s // dil
    tq = min(tq, ls)
    pad = -(-(window - 1) // LANES) * LANES
    wk = min(tq + pad, ls)
    rows = N_HEADS * tq
    view = lambda a: a.reshape(b, ls, dil * a.shape[2])
    args = [view(q), view(k), view(v)]
    in_specs = [
        pl.BlockSpec((1, tq, W_MIX), lambda a, r, i: (a, i, r)),
        pl.BlockSpec((1, ls, W_MIX), lambda a, r, i: (a, 0, r)),
        pl.BlockSpec((1, ls, W_MIX), lambda a, r, i: (a, 0, r)),
    ]
    for (o_g, lse_g) in others:
        args += [o_g, lse_g]
        in_specs += [pl.BlockSpec((1, tq, W_MIX), lambda a, r, i: (a, i, r)),
                     pl.BlockSpec((1, tq, LANES), lambda a, r, i: (a, i, r))]
    out_shape = [jax.ShapeDtypeStruct((b, ls, dil * W_MIX), out_dtype)]
    out_specs = [pl.BlockSpec((1, tq, W_MIX), lambda a, r, i: (a, i, r))]
    if emit_lse:
        out_shape.append(jax.ShapeDtypeStruct((b, ls, dil * LANES), F32))
        out_specs.append(pl.BlockSpec((1, tq, LANES), lambda a, r, i: (a, i, r)))
    res = pl.pallas_call(
        functools.partial(_window_kernel, tq=tq, wk=wk, pad=pad, window=window, ls=ls,
                          emit_lse=emit_lse, n_other=len(others)),
        out_shape=out_shape,
        grid=(b, dil, ls // tq),
        in_specs=in_specs,
        out_specs=out_specs,
        scratch_shapes=[pltpu.VMEM((rows, W_MIX), BF16), pltpu.VMEM((rows, wk), BF16)],
        compiler_params=_params("arbitrary", "arbitrary", "arbitrary"),
        name="window_d%d_w%d" % (dil, window),
    )(*args)
    o = res[0].reshape(b, s, W_MIX)
    if emit_lse:
        return o, res[1].reshape(b, s, LANES)
    return o


def _sb_kernel(q_ref, k_ref, v_ref, tri_ref, o_ref, qm_scr, a_scr, carry_scr, acc_scr, *, tq):
    i = pl.program_id(1)
    t0 = i * tq
    rows = N_HEADS * tq
    _stack_masked(q_ref[0], qm_scr, tq)
    carry_scr[...] = jnp.zeros(carry_scr.shape, F32)
    acc_scr[...] = jnp.zeros(acc_scr.shape, F32)
    r = lax.broadcasted_iota(jnp.int32, (rows, 1), 0)
    t_pos = t0 + (r & (tq - 1))

    def tile(kt, diag):
        z = _dot(qm_scr[...], k_ref[0, kt], NT)
        lg = -(jnp.maximum(z, 0.0) + jnp.log(1.0 + jnp.exp(-jnp.abs(z))))
        if diag:
            s_pos = kt * tq + lax.broadcasted_iota(jnp.int32, (1, tq), 1)
            strict = s_pos < t_pos
            lg = jnp.where(strict, lg, 0.0)
        cum = _dot_split(lg, tri_ref[...]) + carry_scr[...]
        a = jnp.exp(z + cum)
        if diag:
            a = jnp.where(strict, a, 0.0)
        a_scr[...] = a.astype(BF16)
        acc_scr[...] += _dot(a_scr[...], v_ref[0, kt])
        carry_scr[...] += jnp.sum(lg, axis=-1, keepdims=True)

    tile(i, True)

    def cond(state):
        j, top = state
        return (j < i) & (top > SB_STOP)

    def body(state):
        j, _ = state
        tile(i - 1 - j, False)
        return j + 1, jnp.max(carry_scr[...])

    lax.while_loop(cond, body, (jnp.int32(0), jnp.max(carry_scr[...])))
    o_ref[0] = _pick_heads(acc_scr[...], tq).astype(o_ref.dtype)


def _stick_breaking(q, k, v, tq):
    b, s, _ = q.shape
    nk = s // tq
    rows = N_HEADS * tq
    tri = jnp.asarray(np.tril(np.ones((tq, tq), np.float32)), BF16)
    return pl.pallas_call(
        functools.partial(_sb_kernel, tq=tq),
        out_shape=jax.ShapeDtypeStruct((b, s, W_MIX), BF16),
        grid=(b, nk),
        in_specs=[
            pl.BlockSpec((1, tq, W_MIX), lambda a, i: (a, i, 0)),
            pl.BlockSpec((1, nk, tq, W_MIX), lambda a, i: (a, 0, 0, 0)),
            pl.BlockSpec((1, nk, tq, W_MIX), lambda a, i: (a, 0, 0, 0)),
            pl.BlockSpec((tq, tq), lambda a, i: (0, 0)),
        ],
        out_specs=pl.BlockSpec((1, tq, W_MIX), lambda a, i: (a, i, 0)),
        scratch_shapes=[pltpu.VMEM((rows, W_MIX), BF16), pltpu.VMEM((rows, tq), BF16),
                        pltpu.VMEM((rows, 1), F32), pltpu.VMEM((rows, W_MIX), F32)],
        compiler_params=_params("arbitrary", "arbitrary"),
        name="stick_breaking",
    )(q, k.reshape(b, nk, tq, W_MIX), v.reshape(b, nk, tq, W_MIX), tri)


def _foxcum_kernel(x_ref, b_ref, tri_ref, o_ref, carry_scr):
    @pl.when(pl.program_id(1) == 0)
    def _():
        carry_scr[...] = jnp.zeros(carry_scr.shape, F32)

    z = x_ref[0] + b_ref[...]
    logf = jnp.minimum(z, 0.0) - jnp.log(1.0 + jnp.exp(-jnp.abs(z)))
    cum = _dot(tri_ref[...], logf, NN, HI) + carry_scr[...]
    o_ref[0] = cum
    carry_scr[...] = cum[cum.shape[0] - 1:, :]


def _fox_cumsum(misc, bias_vec, tc):
    b, s, w = misc.shape
    tri = jnp.asarray(np.tril(np.ones((tc, tc), np.float32)))
    return pl.pallas_call(
        _foxcum_kernel,
        out_shape=jax.ShapeDtypeStruct((b, s, w), F32),
        grid=(b, s // tc),
        in_specs=[
            pl.BlockSpec((1, tc, w), lambda a, i: (a, i, 0)),
            pl.BlockSpec((1, w), lambda a, i: (0, 0)),
            pl.BlockSpec((tc, tc), lambda a, i: (0, 0)),
        ],
        out_specs=pl.BlockSpec((1, tc, w), lambda a, i: (a, i, 0)),
        scratch_shapes=[pltpu.VMEM((1, w), F32)],
        compiler_params=_params("arbitrary", "arbitrary"),
        name="fox_cumsum",
    )(misc, bias_vec, tri)


def _merge_kernel(x_ref, g_ref, sc_ref, sh_ref, ga_ref, wm_ref, misc_ref, pg_ref,
                  ocmp_ref, osel_ref, owin_ref, ob_ref, oc_ref, od_ref,
                  wa_ref, wb_ref, wc_ref, wd_ref, wo_ref, o_ref):
    x = x_ref[...]
    d = x.shape[1]
    h = _mod_norm(x, g_ref[...], sc_ref[0], sh_ref[0]).astype(BF16)
    gate = jax.nn.sigmoid(misc_ref[...])
    o_a = (_dot_split(gate, pg_ref[0]) * ocmp_ref[...].astype(F32)
           + _dot_split(gate, pg_ref[1]) * osel_ref[...].astype(F32)
           + _dot_split(gate, pg_ref[2]) * owin_ref[...].astype(F32)).astype(BF16)
    mixed = jnp.zeros(x.shape, F32)
    for m, (o_m, w_ref) in enumerate(((o_a, wa_ref), (ob_ref[...], wb_ref),
                                      (oc_ref[...], wc_ref), (od_ref[...], wd_ref))):
        y = _dot(o_m, w_ref[...])
        gl = _dot(h, wm_ref[:, m * d:(m + 1) * d])
        mixed = mixed + jax.nn.sigmoid(gl) * y
    o_ref[...] = x + ga_ref[0] * _dot(mixed.astype(BF16), wo_ref[...])


def _merge(x2, g, sc, sh, ga, w_merge, misc, pg, o_cmp, o_sel, o_win, o_b, o_c, o_d,
           wa, wb, wc, wd, wo, seq, tt):
    n, d = x2.shape
    tpb = seq // tt
    row = lambda w: pl.BlockSpec((tt, w), lambda i: (i, 0))
    full = lambda a: pl.BlockSpec(a.shape, lambda i: (0,) * a.ndim)
    per_b = pl.BlockSpec((1, 1, d), lambda i: (i // tpb, 0, 0))
    return pl.pallas_call(
        _merge_kernel,
        out_shape=jax.ShapeDtypeStruct((n, d), F32),
        grid=(n // tt,),
        in_specs=[row(d), full(g), per_b, per_b, per_b, full(w_merge), row(W_MIX), full(pg)]
        + [row(W_MIX)] * 6 + [full(wa), full(wb), full(wc), full(wd), full(wo)],
        out_specs=row(d),
        compiler_params=_params("arbitrary"),
        name="merge_out",
    )(x2, g, sc, sh, ga, w_merge, misc, pg, o_cmp, o_sel, o_win, o_b, o_c, o_d, wa, wb, wc, wd, wo)


def _ffn_kernel(x_ref, g_ref, sc_ref, sh_ref, gf_ref, w1_ref, w3_ref, w2_ref, o_ref, h_scr, acc_scr):
    f = pl.program_id(1)

    @pl.when(f == 0)
    def _():
        h_scr[...] = _mod_norm(x_ref[...], g_ref[...], sc_ref[0], sh_ref[0]).astype(BF16)
        acc_scr[...] = jnp.zeros(acc_scr.shape, F32)

    h = h_scr[...]
    a = _dot(h, w1_ref[...])
    b = _dot(h, w3_ref[...])
    acc_scr[...] += _dot((a * jax.nn.sigmoid(a) * b).astype(BF16), w2_ref[...])

    @pl.when(f == pl.num_programs(1) - 1)
    def _():
        o_ref[...] = x_ref[...] + gf_ref[0] * acc_scr[...]


def _ffn(x2, g, sc, sh, gf, w1, w3, w2, seq, tt, tf):
    n, d = x2.shape
    dff = w1.shape[1]
    tpb = seq // tt
    per_b = pl.BlockSpec((1, 1, d), lambda i, f: (i // tpb, 0, 0))
    return pl.pallas_call(
        _ffn_kernel,
        out_shape=jax.ShapeDtypeStruct((n, d), F32),
        grid=(n // tt, dff // tf),
        in_specs=[
            pl.BlockSpec((tt, d), lambda i, f: (i, 0)),
            pl.BlockSpec((1, d), lambda i, f: (0, 0)),
            per_b, per_b, per_b,
            pl.BlockSpec((d, tf), lambda i, f: (0, f)),
            pl.BlockSpec((d, tf), lambda i, f: (0, f)),
            pl.BlockSpec((tf, d), lambda i, f: (f, 0)),
        ],
        out_specs=pl.BlockSpec((tt, d), lambda i, f: (i, 0)),
        scratch_shapes=[pltpu.VMEM((tt, d), BF16), pltpu.VMEM((tt, d), F32)],
        compiler_params=_params("arbitrary", "arbitrary"),
        name="ffn_swiglu",
    )(x2, g, sc, sh, gf, w1, w3, w2)


def _route_kernel(x_ref, g_ref, sc_ref, sh_ref, rw_ref, up_ref, h_ref, rank_ref, gate_ref, cnt_ref):
    hf = _mod_norm(x_ref[...], g_ref[...], sc_ref[0], sh_ref[0])
    h_ref[...] = hf.astype(BF16)
    logits = _dot(rw_ref[...], hf, NT, HI)
    ne, tt = logits.shape
    e_idx = lax.broadcasted_iota(jnp.int32, (ne, 1), 0).astype(F32)
    v1 = jnp.max(logits, axis=0, keepdims=True)
    i1 = jnp.min(jnp.where(logits == v1, e_idx, float(ne)), axis=0, keepdims=True)
    m1 = e_idx == i1
    rest = jnp.where(m1, -3e38, logits)
    v2 = jnp.max(rest, axis=0, keepdims=True)
    i2 = jnp.min(jnp.where(rest == v2, e_idx, float(ne)), axis=0, keepdims=True)
    m2 = e_idx == i2
    e2 = jnp.exp(v2 - v1)
    g1 = 1.0 / (1.0 + e2)
    g2 = e2 / (1.0 + e2)
    routed = m1 | m2
    rf = jnp.where(routed, 1.0, 0.0)
    rank = _dot(rf.astype(BF16), up_ref[...])
    rank = jnp.where(routed, rank, -1.0)
    gate = jnp.where(m1, g1, 0.0) + jnp.where(m2, g2, 0.0)
    for e in range(ne):
        rank_ref[0, e] = rank[e:e + 1, :]
        gate_ref[0, e] = gate[e:e + 1, :]
    cnt = jnp.sum(rf, axis=1, keepdims=True)
    cnt_ref[0] = jnp.broadcast_to(cnt, (ne, LANES))


def _route(x2, g, sc, sh, rw_t, seq, tt):
    n, d = x2.shape
    ne = rw_t.shape[0]
    tpb = seq // tt
    nt = n // tt
    upper = jnp.asarray(np.triu(np.ones((tt, tt), np.float32), 1), BF16)
    per_b = pl.BlockSpec((1, 1, d), lambda i: (i // tpb, 0, 0))
    return pl.pallas_call(
        _route_kernel,
        out_shape=[jax.ShapeDtypeStruct((n, d), BF16),
                   jax.ShapeDtypeStruct((nt, ne, 1, tt), F32),
                   jax.ShapeDtypeStruct((nt, ne, 1, tt), F32),
                   jax.ShapeDtypeStruct((nt, ne, LANES), F32)],
        grid=(nt,),
        in_specs=[
            pl.BlockSpec((tt, d), lambda i: (i, 0)),
            pl.BlockSpec((1, d), lambda i: (0, 0)),
            per_b, per_b,
            pl.BlockSpec((ne, d), lambda i: (0, 0)),
            pl.BlockSpec((tt, tt), lambda i: (0, 0)),
        ],
        out_specs=[pl.BlockSpec((tt, d), lambda i: (i, 0)),
                   pl.BlockSpec((1, ne, 1, tt), lambda i: (i, 0, 0, 0)),
                   pl.BlockSpec((1, ne, 1, tt), lambda i: (i, 0, 0, 0)),
                   pl.BlockSpec((1, ne, LANES), lambda i: (i, 0, 0))],
        compiler_params=_params("arbitrary"),
        name="moe_route",
    )(x2, g, sc, sh, rw_t, upper)


def _moe_kernel(cnt_ref, x_ref, gf_ref, h_ref, rank_ref, gate_ref, w1_ref, w3_ref, w2_ref,
                o_ref, acc_scr, *, chunk):
    i, e, f = pl.program_id(0), pl.program_id(1), pl.program_id(2)
    ne, nf = pl.num_programs(1), pl.num_programs(2)

    @pl.when((e == 0) & (f == 0))
    def _():
        acc_scr[...] = jnp.zeros(acc_scr.shape, F32)

    count = cnt_ref[i * ne + e]
    rank = rank_ref[0, 0]
    gate = gate_ref[0, 0]
    h = h_ref[...]

    def body(c, carry):
        r = c * chunk + lax.broadcasted_iota(jnp.int32, (chunk, 1), 0)
        hit = rank == r.astype(F32)
        p = jnp.where(hit, 1.0, 0.0).astype(BF16)
        xs = _dot(p, h).astype(BF16)
        a = _dot(xs, w1_ref[0])
        b = _dot(xs, w3_ref[0])
        y = _dot((a * jax.nn.sigmoid(a) * b).astype(BF16), w2_ref[0])
        gcol = jnp.sum(jnp.where(hit, gate, 0.0), axis=-1, keepdims=True)
        acc_scr[...] += _dot(p, (y * gcol).astype(BF16), TN)
        return carry

    lax.fori_loop(0, (count + chunk - 1) // chunk, body, 0)

    @pl.when((e == ne - 1) & (f == nf - 1))
    def _():
        o_ref[...] = x_ref[...] + gf_ref[0] * acc_scr[...]


def _moe(counts, x2, gf, h2, rank, gate, w1, w3, w2, seq, tt, tf, chunk):
    n, d = x2.shape
    ne, _, dff = w1.shape
    tpb = seq // tt
    grid_spec = pltpu.PrefetchScalarGridSpec(
        num_scalar_prefetch=1,
        grid=(n // tt, ne, dff // tf),
        in_specs=[
            pl.BlockSpec((tt, d), lambda i, e, f, c: (i, 0)),
            pl.BlockSpec((1, 1, d), lambda i, e, f, c: (i // tpb, 0, 0)),
            pl.BlockSpec((tt, d), lambda i, e, f, c: (i, 0)),
            pl.BlockSpec((1, 1, 1, tt), lambda i, e, f, c: (i, e, 0, 0)),
            pl.BlockSpec((1, 1, 1, tt), lambda i, e, f, c: (i, e, 0, 0)),
            pl.BlockSpec((1, d, tf), lambda i, e, f, c: (e, 0, f)),
            pl.BlockSpec((1, d, tf), lambda i, e, f, c: (e, 0, f)),
            pl.BlockSpec((1, tf, d), lambda i, e, f, c: (e, f, 0)),
        ],
        out_specs=pl.BlockSpec((tt, d), lambda i, e, f, c: (i, 0)),
        scratch_shapes=[pltpu.VMEM((tt, d), F32)],
    )
    return pl.pallas_call(
        functools.partial(_moe_kernel, chunk=chunk),
        out_shape=jax.ShapeDtypeStruct((n, d), F32),
        grid_spec=grid_spec,
        compiler_params=_params("arbitrary", "arbitrary", "arbitrary"),
        name="moe_experts",
    )(counts, x2, gf, h2, rank, gate, w1, w3, w2)


def _overlap_matrix(ncp, nc, n_sel):
    c0 = np.arange(ncp) * CMP_STRIDE
    c1 = c0 + CMP_LEN
    s0 = np.arange(n_sel) * SEL_LEN
    s1 = s0 + SEL_LEN
    ov = ((c0[:, None] < s1[None, :]) & (c1[:, None] > s0[None, :])).astype(np.float32)
    ov[nc:] = 0.0
    return jnp.asarray(ov)


def _gate_expand():
    pg = np.zeros((3, W_MIX, W_MIX), np.float32)
    for br in range(3):
        for h in range(N_HEADS):
            pg[br, GATE_LANE + 3 * h + br, h * HEAD_DIM:(h + 1) * HEAD_DIM] = 1.0
    return jnp.asarray(pg, BF16)


def _mixer_layer(x2, b, s, mod, norm_g, rope, w_in, qk_gain, pe_k, pe_v, ck1, ck2, cv1, cv2,
                 fox_b, w_branch, w_out):
    n, d = x2.shape
    sh_a, sc_a, g_a = mod[0], mod[1], mod[2]
    w_slab, gain, w_merge = _pack_w_in(w_in, qk_gain)
    bd = jnp.asarray(np.kron(np.eye(N_HEADS), np.full((HEAD_DIM, HEAD_DIM), 1.0 / HEAD_DIM)), BF16)
    sl = _proj(x2, norm_g, sc_a, sh_a, w_slab, gain, bd, rope, s, min(512, s))
    sl = [a.reshape(b, s, W_MIX) for a in sl]
    misc = sl[S_MISC]

    nch = s // CMP_STRIDE
    nc = nch - CMP_LEN // CMP_STRIDE + 1
    kc_raw = sl[S_KC][..., :HEAD_DIM]
    vc_raw = misc[..., VC_LANE:VC_LANE + HEAD_DIM]
    chunks = jnp.stack([kc_raw, vc_raw]).reshape(2, b, nch, CMP_STRIDE * HEAD_DIM)
    chunks_next = jnp.concatenate([chunks[:, :, 1:], jnp.zeros_like(chunks[:, :, :1])], axis=2)
    pe = jnp.stack([pe_k, pe_v]).reshape(2, 1, CMP_LEN * HEAD_DIM)
    kvc = jnp.tile(_compress(chunks, chunks_next, pe, jnp.stack([ck1, cv1]), jnp.stack([ck2, cv2])),
                   (1, 1, 1, N_HEADS))
    overlap = _overlap_matrix(nch, nc, s // SEL_LEN)
    o_cmp, selmask = _cmp_topk(sl[S_QNR], kvc[0], kvc[1], overlap, min(256, s), nc)
    tq, tk = min(256, s), min(512, s)
    o_sel = _causal_attn(sl[S_QR], sl[S_KSL], sl[S_VSL], tq, tk, sel=selmask)
    o_win = _window_attn(sl[S_QR], sl[S_KW], sl[S_VW], dil=1, window=NSA_WINDOW, tq=256, out_dtype=BF16)

    others = []
    for (wdw, dil) in DIL_CONFIGS[:0:-1]:
        others.append(_window_attn(sl[S_BQ], sl[S_BK], sl[S_BV], dil=dil, window=wdw // dil + 1, tq=256,
                                   out_dtype=F32, emit_lse=True))
    wdw, dil = DIL_CONFIGS[0]
    o_b = _window_attn(sl[S_BQ], sl[S_BK], sl[S_BV], dil=dil, window=wdw // dil + 1, tq=256,
                       out_dtype=BF16, others=others)

    o_c = _stick_breaking(sl[S_CQ], sl[S_CK], sl[S_CV], min(256, s))

    bias_vec = jnp.zeros((1, W_MIX), F32).at[0, FOX_LANE:FOX_LANE + N_HEADS].set(fox_b)
    fcum = _fox_cumsum(misc, bias_vec, min(512, s))
    qk_bound = 1.02 * HEAD_DIM ** 0.5 * jnp.max(jnp.abs(qk_gain[6])) * jnp.max(jnp.abs(qk_gain[7])) + 0.05
    o_d = _causal_attn(sl[S_DQ], sl[S_DK], sl[S_DV], tq, tk, fcum=fcum[..., FOX_LANE:FOX_LANE + N_HEADS],
                       qk_bound=qk_bound)

    wb16 = w_branch.astype(BF16)
    flat = lambda a: a.reshape(n, W_MIX)
    return _merge(x2, norm_g, sc_a, sh_a, g_a, w_merge, flat(misc), _gate_expand(),
                  flat(o_cmp), flat(o_sel), flat(o_win), flat(o_b), flat(o_c), flat(o_d),
                  wb16[0], wb16[1], wb16[2], wb16[3], w_out.astype(BF16), s, min(512, s))


def kernel(x, c, positions, w_ada, b_ada, norm_mix, norm_ffn, w_in, qk_gain, nsa_pe_k, nsa_pe_v,
           nsa_ck_w1, nsa_ck_w2, nsa_cv_w1, nsa_cv_w2, fox_bias, w_branch, w_out,
           ffn_w1, ffn_w3, ffn_w2, router_w, moe_w1, moe_w3, moe_w2):
    b, s, d = x.shape
    depth = w_ada.shape[0]
    rope = _rope_tables(positions)
    mods = _ada(c, w_ada, b_ada).reshape(depth, b, 6, 1, d).transpose(0, 2, 1, 3, 4)
    x2 = x.reshape(b * s, d)
    for l in range(depth):
        mod = mods[l]
        x2 = _mixer_layer(x2, b, s, mod[0:3], norm_mix[l].reshape(1, d), rope, w_in[l], qk_gain[l],
                          nsa_pe_k[l], nsa_pe_v[l], nsa_ck_w1[l], nsa_ck_w2[l], nsa_cv_w1[l],
                          nsa_cv_w2[l], fox_bias[l], w_branch[l], w_out[l])
        sh_f, sc_f, g_f = mod[3], mod[4], mod[5]
        gn = norm_ffn[l].reshape(1, d)
        e = l // 2
        if l % 2 == 0:
            dff = ffn_w1.shape[2]
            x2 = _ffn(x2, gn, sc_f, sh_f, g_f, ffn_w1[e].astype(BF16), ffn_w3[e].astype(BF16),
                      ffn_w2[e].astype(BF16), s, min(1024, s), dff // 2)
        else:
            tt = min(1024, s)
            dff = moe_w1.shape[3]
            h2, rank, gate, cnt = _route(x2, gn, sc_f, sh_f, router_w[e].T, s, tt)
            counts = cnt[:, :, 0].astype(jnp.int32).reshape(-1)
            x2 = _moe(counts, x2, g_f, h2, rank, gate, moe_w1[e].astype(BF16), moe_w3[e].astype(BF16),
                      moe_w2[e].astype(BF16), s, tt, dff // 2, 256)
    return x2.reshape(b, s, d)
```

```python
import functools

import numpy as np
import jax
import jax.numpy as jnp
from jax import lax
from jax.experimental import pallas as pl
from jax.experimental.pallas import tpu as pltpu

F32 = jnp.float32
BF16 = jnp.bfloat16
HI = lax.Precision.HIGHEST

LANES = 128
VMEM_LIMIT = 52 * 1024 * 1024

HEAD_DIM = 64
HEAD_SHIFT = 6
N_HEADS = 4
W_MIX = N_HEADS * HEAD_DIM
N_MIXERS = 4
ROPE_THETA = 500000.0
ROPE_DIMS = HEAD_DIM // 4
ROPE_HALF = ROPE_DIMS // 2
EPS = 1e-6
NEG = -1e30
BIG = 1e30
CMP_LEN = 32
CMP_STRIDE = 16
SEL_LEN = 64
SEL_SHIFT = 6
TOPN = 16
NSA_WINDOW = 512
DIL_CONFIGS = ((128, 1), (512, 4), (2048, 16))
SB_STOP = -110.0
FOX_STOP = 108.0
GATE_LANE = 0
FOX_LANE = 3 * N_HEADS
VC_LANE = HEAD_DIM

NN = (((1,), (0,)), ((), ()))
NT = (((1,), (1,)), ((), ()))
TN = (((0,), (0,)), ((), ()))


def _dot(a, b, dims=NN, precision=None):
    return lax.dot_general(a, b, dims, precision=precision, preferred_element_type=F32)


def _dot_split(a, b_bf16, dims=NN):
    hi = a.astype(BF16)
    lo = (a - hi.astype(F32)).astype(BF16)
    return _dot(hi, b_bf16, dims) + _dot(lo, b_bf16, dims)


def _params(*sem):
    return pltpu.CompilerParams(dimension_semantics=sem, vmem_limit_bytes=VMEM_LIMIT)


def _mod_norm(x, g, sc, sh):
    ms = jnp.mean(x * x, axis=-1, keepdims=True)
    return (x * lax.rsqrt(ms + EPS) * g) * (1.0 + sc) + sh


def _head_masks():
    lane = lax.broadcasted_iota(jnp.int32, (1, W_MIX), 1)
    return [(lane >> HEAD_SHIFT) == h for h in range(N_HEADS)]


def _stack_masked(q, qm_scr, tq):
    for h, hm in enumerate(_head_masks()):
        qm_scr[h * tq:(h + 1) * tq, :] = jnp.where(hm, q, jnp.zeros_like(q))


def _pick_heads(stacked, tq, scale=None):
    out = None
    for h, hm in enumerate(_head_masks()):
        blk = stacked[h * tq:(h + 1) * tq, :]
        if scale is not None:
            blk = blk * scale[h]
        out = jnp.where(hm, blk, 0.0 if out is None else out)
    return out


def _ada_kernel(c_ref, w_ref, b_ref, o_ref):
    c = c_ref[...]
    ca = c * jax.nn.sigmoid(c)
    o_ref[0] = _dot(ca, w_ref[0], NN, HI) + b_ref[0]


def _ada(c, w_ada, b_ada):
    depth, d, n6 = w_ada.shape
    b = c.shape[0]
    tn = n6 // 4
    return pl.pallas_call(
        _ada_kernel,
        out_shape=jax.ShapeDtypeStruct((depth, b, n6), F32),
        grid=(depth, n6 // tn),
        in_specs=[
            pl.BlockSpec((b, d), lambda l, j: (0, 0)),
            pl.BlockSpec((1, d, tn), lambda l, j: (l, 0, j)),
            pl.BlockSpec((1, 1, tn), lambda l, j: (l, 0, j)),
        ],
        out_specs=pl.BlockSpec((1, b, tn), lambda l, j: (l, 0, j)),
        compiler_params=_params("arbitrary", "arbitrary"),
        name="ada_mod",
    )(c, w_ada, b_ada.reshape(depth, 1, n6))


_SLABS = (
    (True, False, F32), (True, True, BF16), (True, False, F32),
    (True, True, BF16), (True, True, BF16), (False, False, BF16), (False, False, BF16),
    (True, True, BF16), (True, True, BF16), (False, False, BF16),
    (False, False, BF16), (False, False, BF16), (False, False, BF16),
    (True, False, BF16), (True, False, BF16), (False, False, BF16),
    (False, False, F32),
)
N_SLABS = len(_SLABS)
(S_QNR, S_QR, S_KC, S_KSL, S_KW, S_VSL, S_VW, S_BQ, S_BK, S_BV, S_CQ, S_CK, S_CV,
 S_DQ, S_DK, S_DV, S_MISC) = range(N_SLABS)


def _proj_kernel(x_ref, g_ref, sc_ref, sh_ref, w_ref, gain_ref, bd_ref, rope_ref, *out_refs):
    h = _mod_norm(x_ref[...], g_ref[...], sc_ref[0], sh_ref[0]).astype(BF16)
    bd = bd_ref[...]
    for s, (norm, rope, _) in enumerate(_SLABS):
        y = _dot(h, w_ref[:, s * W_MIX:(s + 1) * W_MIX])
        if norm:
            ms = _dot_split(y * y, bd)
            y = y * lax.rsqrt(ms + EPS)
        y = y * gain_ref[s]
        if rope:
            y = (y * rope_ref[0, 0] + pltpu.roll(y, W_MIX - ROPE_HALF, 1) * rope_ref[0, 1]
                 + pltpu.roll(y, ROPE_HALF, 1) * rope_ref[0, 2])
        out_refs[s][...] = y.astype(out_refs[s].dtype)


def _proj(x2, g, sc, sh, w_slab, gain, bd, rope, seq, tt):
    n, d = x2.shape
    tpb = seq // tt
    out_shape = [jax.ShapeDtypeStruct((n, W_MIX), dt) for (_, _, dt) in _SLABS]
    return pl.pallas_call(
        _proj_kernel,
        out_shape=out_shape,
        grid=(n // tt,),
        in_specs=[
            pl.BlockSpec((tt, d), lambda i: (i, 0)),
            pl.BlockSpec((1, d), lambda i: (0, 0)),
            pl.BlockSpec((1, 1, d), lambda i: (i // tpb, 0, 0)),
            pl.BlockSpec((1, 1, d), lambda i: (i // tpb, 0, 0)),
            pl.BlockSpec((d, N_SLABS * W_MIX), lambda i: (0, 0)),
            pl.BlockSpec((N_SLABS, 1, W_MIX), lambda i: (0, 0, 0)),
            pl.BlockSpec((W_MIX, W_MIX), lambda i: (0, 0)),
            pl.BlockSpec((1, 3, tt, W_MIX), lambda i: (i // tpb, 0, i % tpb, 0)),
        ],
        out_specs=[pl.BlockSpec((tt, W_MIX), lambda i: (i, 0)) for _ in _SLABS],
        compiler_params=_params("arbitrary"),
        name="in_proj",
    )(x2, g, sc, sh, w_slab, gain, bd, rope)


def _pack_w_in(w_in, qk_gain):
    d = w_in.shape[0]
    o = 0
    a_q = w_in[:, o:o + W_MIX]; o += W_MIX
    a_kv = w_in[:, o:o + 6 * HEAD_DIM]; o += 6 * HEAD_DIM
    a_g = w_in[:, o:o + 3 * N_HEADS]; o += 3 * N_HEADS
    b_qkv = w_in[:, o:o + 3 * W_MIX]; o += 3 * W_MIX
    c_qkv = w_in[:, o:o + 3 * W_MIX]; o += 3 * W_MIX
    d_qkv = w_in[:, o:o + 3 * W_MIX]; o += 3 * W_MIX
    d_f = w_in[:, o:o + N_HEADS]; o += N_HEADS
    w_merge = w_in[:, o:]
    kc, vc, ksl, vsl, kw, vw = (a_kv[:, i * HEAD_DIM:(i + 1) * HEAD_DIM] for i in range(6))
    zeros = lambda w: jnp.zeros((d, w), w_in.dtype)
    rep = lambda w: jnp.tile(w, (1, N_HEADS))
    misc = jnp.concatenate([a_g, d_f, zeros(VC_LANE - FOX_LANE - N_HEADS), vc, zeros(W_MIX - 2 * HEAD_DIM)], axis=1)
    slabs = [
        a_q, a_q, jnp.concatenate([kc, zeros(W_MIX - HEAD_DIM)], axis=1),
        rep(ksl), rep(kw), rep(vsl), rep(vw),
        b_qkv[:, :W_MIX], b_qkv[:, W_MIX:2 * W_MIX], b_qkv[:, 2 * W_MIX:],
        c_qkv[:, :W_MIX], c_qkv[:, W_MIX:2 * W_MIX], c_qkv[:, 2 * W_MIX:],
        d_qkv[:, :W_MIX], d_qkv[:, W_MIX:2 * W_MIX], d_qkv[:, 2 * W_MIX:],
        misc,
    ]
    w_slab = jnp.concatenate(slabs, axis=1).astype(BF16)
    scale = HEAD_DIM ** -0.5
    t4 = lambda gvec: jnp.tile(gvec, N_HEADS)
    one = jnp.ones((W_MIX,), F32)
    gains = [
        t4(qk_gain[0]) * scale, t4(qk_gain[0]) * scale,
        jnp.concatenate([qk_gain[1], jnp.ones((W_MIX - HEAD_DIM,), F32)]),
        t4(qk_gain[2]), t4(qk_gain[3]), one, one,
        t4(qk_gain[4]) * scale, t4(qk_gain[5]), one,
        one * scale, one, one,
        t4(qk_gain[6]) * scale, t4(qk_gain[7]), one,
        one,
    ]
    gain = jnp.stack(gains).reshape(N_SLABS, 1, W_MIX).astype(F32)
    return w_slab, gain, w_merge.astype(BF16)


def _rope_tables(positions):
    inv = ROPE_THETA ** (-jnp.arange(0, ROPE_DIMS, 2, dtype=F32) / ROPE_DIMS)
    ang = positions.astype(F32)[..., None] * inv
    cos, sin = jnp.cos(ang), jnp.sin(ang)
    b, s, _ = cos.shape
    pad1 = jnp.ones((b, s, HEAD_DIM - ROPE_DIMS), F32)
    pad0 = jnp.zeros((b, s, HEAD_DIM - ROPE_DIMS), F32)
    z8 = jnp.zeros_like(sin)
    c64 = jnp.concatenate([cos, cos, pad1], axis=-1)
    s1_64 = jnp.concatenate([-sin, z8, pad0], axis=-1)
    s2_64 = jnp.concatenate([z8, sin, pad0], axis=-1)
    heads = lambda t: jnp.tile(t, (1, 1, N_HEADS))
    return jnp.stack([heads(c64), heads(s1_64), heads(s2_64)], axis=1)


def _compress_kernel(a_ref, b_ref, pe_ref, w1_ref, w2_ref, o_ref):
    half = w1_ref.shape[1] // 2
    w1 = w1_ref[0]
    hid = (_dot(a_ref[0, 0], w1[:half], NN, HI) + _dot(b_ref[0, 0], w1[half:], NN, HI)
           + _dot(pe_ref[0], w1, NN, HI))
    o_ref[0, 0] = _dot(jax.nn.gelu(hid), w2_ref[0], NN, HI)


def _compress(ch, chn, pe, w1, w2):
    _, b, ncp, cw = ch.shape
    hid = w1.shape[2]
    return pl.pallas_call(
        _compress_kernel,
        out_shape=jax.ShapeDtypeStruct((2, b, ncp, HEAD_DIM), F32),
        grid=(2, b),
        in_specs=[
            pl.BlockSpec((1, 1, ncp, cw), lambda k, i: (k, i, 0, 0)),
            pl.BlockSpec((1, 1, ncp, cw), lambda k, i: (k, i, 0, 0)),
            pl.BlockSpec((1, 1, 2 * cw), lambda k, i: (k, 0, 0)),
            pl.BlockSpec((1, 2 * cw, hid), lambda k, i: (k, 0, 0)),
            pl.BlockSpec((1, hid, HEAD_DIM), lambda k, i: (k, 0, 0)),
        ],
        out_specs=pl.BlockSpec((1, 1, ncp, HEAD_DIM), lambda k, i: (k, i, 0, 0)),
        compiler_params=_params("arbitrary", "arbitrary"),
        name="nsa_compress",
    )(ch, chn, pe, w1, w2)


def _cmp_kernel(q_ref, kc_ref, vct_ref, ovt_ref, o_ref, sel_ref, qh_scr, ql_scr, acc_scr, *, tq, nc, n_sel):
    i = pl.program_id(1)
    ncp = kc_ref.shape[1]
    q = q_ref[0]
    q_hi = q.astype(BF16)
    _stack_masked(q_hi, qh_scr, tq)
    _stack_masked((q - q_hi.astype(F32)).astype(BF16), ql_scr, tq)
    kc = kc_ref[0]
    k_hi = kc.astype(BF16)
    k_lo = (kc - k_hi.astype(F32)).astype(BF16)
    ss = []
    for h in range(N_HEADS):
        rs = slice(h * tq, (h + 1) * tq)
        ss.append(_dot(k_hi, qh_scr[rs, :], NT) + _dot(k_lo, qh_scr[rs, :], NT)
                  + _dot(k_hi, ql_scr[rs, :], NT))
    t = i * tq + lax.broadcasted_iota(jnp.int32, (1, tq), 1)
    c = lax.broadcasted_iota(jnp.int32, (ncp, 1), 0)
    mask = (c * CMP_STRIDE + (CMP_LEN - 1) <= t) & (c < nc)
    psum = None
    ps = []
    for h in range(N_HEADS):
        sm = jnp.where(mask, ss[h], NEG)
        m = jnp.max(sm, axis=0, keepdims=True)
        e = jnp.where(mask, jnp.exp(sm - m), 0.0)
        l = jnp.sum(e, axis=0, keepdims=True)
        p = e / jnp.maximum(l, 1e-30)
        psum = p if psum is None else psum + p
        ps.append(p.astype(BF16))
    vct = vct_ref[0]
    for h in range(N_HEADS):
        rs = slice(h * HEAD_DIM, (h + 1) * HEAD_DIM)
        acc_scr[rs, :] = _dot(vct[rs, :], ps[h])
    o_ref[0] = acc_scr[...].T.astype(o_ref.dtype)
    imp = _dot(ovt_ref[...], psum, NN, HI)
    j = lax.broadcasted_iota(jnp.int32, (n_sel, 1), 0)
    cur = t >> SEL_SHIFT
    valid = j <= cur
    forced = (j == 0) | (j == cur) | (j == cur - 1)
    score = jnp.where(valid, jnp.where(forced, BIG, imp), NEG)
    sel = jnp.zeros((n_sel, tq), F32)
    jf = j.astype(F32)
    for _ in range(min(TOPN, n_sel)):
        mx = jnp.max(score, axis=0, keepdims=True)
        idx = jnp.min(jnp.where(score == mx, jf, float(n_sel)), axis=0, keepdims=True)
        pick = jf == idx
        sel = jnp.where(pick, 1.0, sel)
        score = jnp.where(pick, -3e38, score)
    sel_ref[0] = sel.astype(sel_ref.dtype)


def _cmp_topk(q_nr, kc_rep, vc_rep, overlap, tq, nc):
    b, seq, _ = q_nr.shape
    ncp = kc_rep.shape[1]
    n_sel = seq // SEL_LEN
    return pl.pallas_call(
        functools.partial(_cmp_kernel, tq=tq, nc=nc, n_sel=n_sel),
        out_shape=[jax.ShapeDtypeStruct((b, seq, W_MIX), BF16),
                   jax.ShapeDtypeStruct((b, n_sel, seq), BF16)],
        grid=(b, seq // tq),
        in_specs=[
            pl.BlockSpec((1, tq, W_MIX), lambda g, i: (g, i, 0)),
            pl.BlockSpec((1, ncp, W_MIX), lambda g, i: (g, 0, 0)),
            pl.BlockSpec((1, W_MIX, ncp), lambda g, i: (g, 0, 0)),
            pl.BlockSpec((n_sel, ncp), lambda g, i: (0, 0)),
        ],
        out_specs=[pl.BlockSpec((1, tq, W_MIX), lambda g, i: (g, i, 0)),
                   pl.BlockSpec((1, n_sel, tq), lambda g, i: (g, 0, i))],
        scratch_shapes=[pltpu.VMEM((N_HEADS * tq, W_MIX), BF16), pltpu.VMEM((N_HEADS * tq, W_MIX), BF16),
                        pltpu.VMEM((W_MIX, tq), F32)],
        compiler_params=_params("arbitrary", "arbitrary"),
        name="nsa_cmp_topk",
    )(q_nr, kc_rep, vc_rep, overlap)


def _causal_kernel(*refs, tq, tk, has_sel, has_bias):
    it = iter(refs)
    q_ref, k_ref, vt_ref = next(it), next(it), next(it)
    sel_ref = next(it) if has_sel else None
    fa_ref, fb_ref, fq_ref, thr_ref = (next(it) for _ in range(4)) if has_bias else (None,) * 4
    o_ref, qm_scr, acc_scr, sa_scr, sb_scr, m_scr, l_scr = (next(it) for _ in range(7))
    i = pl.program_id(1)
    t0 = i * tq
    _stack_masked(q_ref[0], qm_scr, tq)
    acc_scr[...] = jnp.zeros(acc_scr.shape, F32)
    t_pos = t0 + lax.broadcasted_iota(jnp.int32, (1, tq), 1)

    n_s = N_HEADS + (1 if has_sel else 0)

    def scores(kt, s_buf):
        k_t = k_ref[0, kt]
        for h in range(N_HEADS):
            s = _dot(k_t, qm_scr[h * tq:(h + 1) * tq, :], NT)
            if has_bias:
                s = s + _dot(fa_ref[0, kt], fb_ref[0, h])
            s_buf[h] = s
        if has_sel:
            s_pos = kt * tk + lax.broadcasted_iota(jnp.int32, (tk, 1), 0)
            blk = lax.broadcasted_iota(jnp.int32, (1, sel_ref.shape[1]), 1)
            expand = jnp.where((s_pos >> SEL_SHIFT) == blk, 1.0, 0.0).astype(BF16)
            s_buf[N_HEADS] = _dot(expand, sel_ref[0])

    def update(kt, s_buf, diag):
        vt_t = vt_ref[0, kt]
        mask = None
        if diag:
            s_pos = kt * tk + lax.broadcasted_iota(jnp.int32, (tk, 1), 0)
            mask = s_pos <= t_pos
        if has_sel:
            picked = s_buf[N_HEADS] > 0.5
            mask = picked if mask is None else (mask & picked)
        ps, alphas = [], []
        for h in range(N_HEADS):
            s = s_buf[h]
            if mask is not None:
                s = jnp.where(mask, s, NEG)
            m_old = m_scr[h]
            m_new = jnp.maximum(m_old, jnp.max(s, axis=0, keepdims=True))
            p = jnp.exp(s - m_new)
            alpha = jnp.exp(m_old - m_new)
            m_scr[h] = m_new
            l_scr[h] = alpha * l_scr[h] + jnp.sum(p, axis=0, keepdims=True)
            ps.append(p.astype(BF16))
            alphas.append(alpha)
        for h in range(N_HEADS):
            rs = slice(h * HEAD_DIM, (h + 1) * HEAD_DIM)
            acc_scr[rs, :] = alphas[h] * acc_scr[rs, :] + _dot(vt_t[rs, :], ps[h])

    def live(kt_next):
        lane = lax.broadcasted_iota(jnp.int32, (1, thr_ref.shape[3]), 1)
        hit = None
        for h in range(N_HEADS):
            top = jnp.max(fq_ref[0, h] - m_scr[h], axis=-1, keepdims=True)
            need = top >= thr_ref[0, h]
            hit = need if hit is None else (hit | need)
        return jnp.max(jnp.where(hit & (lane == kt_next), 1.0, 0.0)) > 0.5

    n_last = t0 // tk
    m_scr[...] = jnp.full(m_scr.shape, NEG, F32)
    l_scr[...] = jnp.zeros(l_scr.shape, F32)
    scores(n_last, sa_scr)
    scores(jnp.maximum(n_last - 1, 0), sb_scr)
    update(n_last, sa_scr, True)

    def pair(j):
        kt = n_last - 1 - 2 * j
        scores(kt - 1, sa_scr)
        update(kt, sb_scr, False)
        scores(jnp.maximum(kt - 2, 0), sb_scr)
        update(kt - 1, sa_scr, False)
        return kt - 2

    n_pairs = n_last // 2
    if has_bias:
        def cond(state):
            j, go = state
            return (j < n_pairs) & go

        def body(state):
            j, _ = state
            return j + 1, live(pair(j))

        _, go = lax.while_loop(cond, body, (jnp.int32(0), live(n_last - 1)))
    else:
        lax.fori_loop(0, n_pairs, lambda j, c: (pair(j), c)[1], 0)
        go = True

    @pl.when((n_last % 2 == 1) & go)
    def _():
        update(0, sb_scr, False)

    ls = [l_scr[h] for h in range(N_HEADS)]
    for h in range(N_HEADS):
        rs = slice(h * HEAD_DIM, (h + 1) * HEAD_DIM)
        acc_scr[rs, :] = acc_scr[rs, :] / ls[h]
    o_ref[0] = acc_scr[...].T.astype(o_ref.dtype)


def _split3(x):
    def cut(v):
        bits = lax.bitcast_convert_type(v, jnp.uint32) & jnp.uint32(0xFFFF0000)
        return lax.bitcast_convert_type(bits, F32)

    hi = cut(x)
    r1 = x - hi
    mid = cut(r1)
    lo = r1 - mid
    return hi.astype(BF16), mid.astype(BF16), lo.astype(BF16)


def _causal_attn(q, k, v, tq, tk, sel=None, fcum=None, qk_bound=None):
    b, s, _ = q.shape
    nk = s // tk
    rows = N_HEADS * tq
    n_s = N_HEADS + (1 if sel is not None else 0)
    vt = v.reshape(b, nk, tk, W_MIX).transpose(0, 1, 3, 2)
    args = [q, k.reshape(b, nk, tk, W_MIX), vt]
    in_specs = [
        pl.BlockSpec((1, tq, W_MIX), lambda a, i: (a, i, 0)),
        pl.BlockSpec((1, nk, tk, W_MIX), lambda a, i: (a, 0, 0, 0)),
        pl.BlockSpec((1, nk, W_MIX, tk), lambda a, i: (a, 0, 0, 0)),
    ]
    if sel is not None:
        args.append(sel)
        in_specs.append(pl.BlockSpec((1, sel.shape[1], tq), lambda a, i: (a, 0, i)))
    if fcum is not None:
        nf = 2 * 3 + 2
        ones = jnp.ones(fcum.shape, BF16)
        zero = jnp.zeros(fcum.shape, BF16)
        parts = _split3(fcum)
        key_f = jnp.stack([ones, ones, ones] + [-p for p in parts] + [zero, zero], axis=-1)
        qry_f = jnp.stack(list(parts) + [ones, ones, ones, zero, zero], axis=-1)
        fa = key_f.reshape(b, nk, tk, N_HEADS * nf)
        eye = jnp.eye(N_HEADS, dtype=BF16)
        fb = jnp.einsum('bshf,hg->bgshf', qry_f, eye).reshape(b, N_HEADS, s, N_HEADS * nf).transpose(0, 1, 3, 2)
        f_rows = fcum.transpose(0, 2, 1)
        f_end = f_rows[:, :, tk - 1::tk]
        thr = jnp.pad(f_end - (qk_bound + FOX_STOP), ((0, 0), (0, 0), (0, LANES - nk)),
                      constant_values=BIG).reshape(b, N_HEADS, 1, LANES)
        args += [fa, fb, f_rows.reshape(b, N_HEADS, 1, s), thr]
        in_specs += [pl.BlockSpec((1, nk, tk, N_HEADS * nf), lambda a, i: (a, 0, 0, 0)),
                     pl.BlockSpec((1, N_HEADS, N_HEADS * nf, tq), lambda a, i: (a, 0, 0, i)),
                     pl.BlockSpec((1, N_HEADS, 1, tq), lambda a, i: (a, 0, 0, i)),
                     pl.BlockSpec((1, N_HEADS, 1, LANES), lambda a, i: (a, 0, 0, 0))]
    return pl.pallas_call(
        functools.partial(_causal_kernel, tq=tq, tk=tk, has_sel=sel is not None, has_bias=fcum is not None),
        out_shape=jax.ShapeDtypeStruct((b, s, W_MIX), BF16),
        grid=(b, s // tq),
        in_specs=in_specs,
        out_specs=pl.BlockSpec((1, tq, W_MIX), lambda a, i: (a, i, 0)),
        scratch_shapes=[pltpu.VMEM((rows, W_MIX), BF16), pltpu.VMEM((W_MIX, tq), F32),
                        pltpu.VMEM((n_s, tk, tq), F32), pltpu.VMEM((n_s, tk, tq), F32),
                        pltpu.VMEM((N_HEADS, 1, tq), F32), pltpu.VMEM((N_HEADS, 1, tq), F32)],
        compiler_params=_params("arbitrary", "arbitrary"),
        name="causal_sel%d_bias%d" % (sel is not None, fcum is not None),
    )(*args)


def _window_kernel(*refs, tq, wk, pad, window, ls, emit_lse, n_other):
    it = iter(refs)
    q_ref, k_ref, v_ref = next(it), next(it), next(it)
    others = [(next(it), next(it)) for _ in range(n_other)]
    o_ref = next(it)
    lse_ref = next(it) if emit_lse else None
    qm_scr, p_scr = next(it), next(it)
    t0 = pl.program_id(2) * tq
    start = pl.multiple_of(jnp.clip(t0 - pad, 0, ls - wk), LANES)
    _stack_masked(q_ref[0], qm_scr, tq)
    s_all = _dot(qm_scr[...], k_ref[0, pl.ds(start, wk), :], NT)
    t_pos = t0 + lax.broadcasted_iota(jnp.int32, (tq, 1), 0)
    s_pos = start + lax.broadcasted_iota(jnp.int32, (1, wk), 1)
    mask = (s_pos <= t_pos) & (t_pos - s_pos < window)
    ms, ls_ = [], []
    for h in range(N_HEADS):
        rs = slice(h * tq, (h + 1) * tq)
        s = jnp.where(mask, s_all[rs, :], NEG)
        m = jnp.max(s, axis=-1, keepdims=True)
        p = jnp.exp(s - m)
        ms.append(m)
        ls_.append(jnp.sum(p, axis=-1, keepdims=True))
        p_scr[rs, :] = p.astype(BF16)
    pv = _dot(p_scr[...], v_ref[0, pl.ds(start, wk), :])
    if n_other:
        scales, dens = [], []
        lses = [m + jnp.log(l) for m, l in zip(ms, ls_)]
        out = None
        hms = _head_masks()
        for h in range(N_HEADS):
            other_lse = [lr[0, :, h:h + 1] for (_, lr) in others]
            top = lses[h]
            for ol in other_lse:
                top = jnp.maximum(top, ol)
            w_self = jnp.exp(lses[h] - top)
            num = pv[h * tq:(h + 1) * tq, :] * (w_self / ls_[h])
            den = w_self
            for (orf, _), ol in zip(others, other_lse):
                w = jnp.exp(ol - top)
                num = num + w * orf[0]
                den = den + w
            out = jnp.where(hms[h], num / den, 0.0 if out is None else out)
        o_ref[0] = out.astype(o_ref.dtype)
    else:
        o_ref[0] = _pick_heads(pv, tq, [1.0 / l for l in ls_]).astype(o_ref.dtype)
    if emit_lse:
        lane = lax.broadcasted_iota(jnp.int32, (1, LANES), 1)
        tile = jnp.zeros((tq, LANES), F32)
        for h in range(N_HEADS):
            tile = jnp.where(lane == h, ms[h] + jnp.log(ls_[h]), tile)
        lse_ref[0] = tile


def _window_attn(q, k, v, *, dil, window, tq, out_dtype, emit_lse=False, others=()):
    b, s, _ = q.shape
    ls = s // dil
    tq = min(tq, ls)
    pad = -(-(window - 1) // LANES) * LANES
    wk = min(tq + pad, ls)
    rows = N_HEADS * tq
    view = lambda a: a.reshape(b, ls, dil * a.shape[2])
    args = [view(q), view(k), view(v)]
    in_specs = [
        pl.BlockSpec((1, tq, W_MIX), lambda a, r, i: (a, i, r)),
        pl.BlockSpec((1, ls, W_MIX), lambda a, r, i: (a, 0, r)),
        pl.BlockSpec((1, ls, W_MIX), lambda a, r, i: (a, 0, r)),
    ]
    for (o_g, lse_g) in others:
        args += [o_g, lse_g]
        in_specs += [pl.BlockSpec((1, tq, W_MIX), lambda a, r, i: (a, i, r)),
                     pl.BlockSpec((1, tq, LANES), lambda a, r, i: (a, i, r))]
    out_shape = [jax.ShapeDtypeStruct((b, ls, dil * W_MIX), out_dtype)]
    out_specs = [pl.BlockSpec((1, tq, W_MIX), lambda a, r, i: (a, i, r))]
    if emit_lse:
        out_shape.append(jax.ShapeDtypeStruct((b, ls, dil * LANES), F32))
        out_specs.append(pl.BlockSpec((1, tq, LANES), lambda a, r, i: (a, i, r)))
    res = pl.pallas_call(
        functools.partial(_window_kernel, tq=tq, wk=wk, pad=pad, window=window, ls=ls,
                          emit_lse=emit_lse, n_other=len(others)),
        out_shape=out_shape,
        grid=(b, dil, ls // tq),
        in_specs=in_specs,
        out_specs=out_specs,
        scratch_shapes=[pltpu.VMEM((rows, W_MIX), BF16), pltpu.VMEM((rows, wk), BF16)],
        compiler_params=_params("arbitrary", "arbitrary", "arbitrary"),
        name="window_d%d_w%d" % (dil, window),
    )(*args)
    o = res[0].reshape(b, s, W_MIX)
    if emit_lse:
        return o, res[1].reshape(b, s, LANES)
    return o


def _sb_kernel(q_ref, k_ref, v_ref, tri_ref, o_ref, qm_scr, a_scr, carry_scr, acc_scr, *, tq):
    i = pl.program_id(1)
    t0 = i * tq
    rows = N_HEADS * tq
    _stack_masked(q_ref[0], qm_scr, tq)
    carry_scr[...] = jnp.zeros(carry_scr.shape, F32)
    acc_scr[...] = jnp.zeros(acc_scr.shape, F32)
    r = lax.broadcasted_iota(jnp.int32, (rows, 1), 0)
    t_pos = t0 + (r & (tq - 1))

    def tile(kt, diag):
        z = _dot(qm_scr[...], k_ref[0, kt], NT)
        lg = -(jnp.maximum(z, 0.0) + jnp.log(1.0 + jnp.exp(-jnp.abs(z))))
        if diag:
            s_pos = kt * tq + lax.broadcasted_iota(jnp.int32, (1, tq), 1)
            strict = s_pos < t_pos
            lg = jnp.where(strict, lg, 0.0)
        cum = _dot_split(lg, tri_ref[...]) + carry_scr[...]
        a = jnp.exp(z + cum)
        if diag:
            a = jnp.where(strict, a, 0.0)
        a_scr[...] = a.astype(BF16)
        acc_scr[...] += _dot(a_scr[...], v_ref[0, kt])
        carry_scr[...] += jnp.sum(lg, axis=-1, keepdims=True)

    tile(i, True)

    def cond(state):
        j, top = state
        return (j < i) & (top > SB_STOP)

    def body(state):
        j, _ = state
        tile(i - 1 - j, False)
        return j + 1, jnp.max(carry_scr[...])

    lax.while_loop(cond, body, (jnp.int32(0), jnp.max(carry_scr[...])))
    o_ref[0] = _pick_heads(acc_scr[...], tq).astype(o_ref.dtype)


def _stick_breaking(q, k, v, tq):
    b, s, _ = q.shape
    nk = s // tq
    rows = N_HEADS * tq
    tri = jnp.asarray(np.tril(np.ones((tq, tq), np.float32)), BF16)
    return pl.pallas_call(
        functools.partial(_sb_kernel, tq=tq),
        out_shape=jax.ShapeDtypeStruct((b, s, W_MIX), BF16),
        grid=(b, nk),
        in_specs=[
            pl.BlockSpec((1, tq, W_MIX), lambda a, i: (a, i, 0)),
            pl.BlockSpec((1, nk, tq, W_MIX), lambda a, i: (a, 0, 0, 0)),
            pl.BlockSpec((1, nk, tq, W_MIX), lambda a, i: (a, 0, 0, 0)),
            pl.BlockSpec((tq, tq), lambda a, i: (0, 0)),
        ],
        out_specs=pl.BlockSpec((1, tq, W_MIX), lambda a, i: (a, i, 0)),
        scratch_shapes=[pltpu.VMEM((rows, W_MIX), BF16), pltpu.VMEM((rows, tq), BF16),
                        pltpu.VMEM((rows, 1), F32), pltpu.VMEM((rows, W_MIX), F32)],
        compiler_params=_params("arbitrary", "arbitrary"),
        name="stick_breaking",
    )(q, k.reshape(b, nk, tq, W_MIX), v.reshape(b, nk, tq, W_MIX), tri)


def _foxcum_kernel(x_ref, b_ref, tri_ref, o_ref, carry_scr):
    @pl.when(pl.program_id(1) == 0)
    def _():
        carry_scr[...] = jnp.zeros(carry_scr.shape, F32)

    z = x_ref[0] + b_ref[...]
    logf = jnp.minimum(z, 0.0) - jnp.log(1.0 + jnp.exp(-jnp.abs(z)))
    cum = _dot(tri_ref[...], logf, NN, HI) + carry_scr[...]
    o_ref[0] = cum
    carry_scr[...] = cum[cum.shape[0] - 1:, :]


def _fox_cumsum(misc, bias_vec, tc):
    b, s, w = misc.shape
    tri = jnp.asarray(np.tril(np.ones((tc, tc), np.float32)))
    return pl.pallas_call(
        _foxcum_kernel,
        out_shape=jax.ShapeDtypeStruct((b, s, w), F32),
        grid=(b, s // tc),
        in_specs=[
            pl.BlockSpec((1, tc, w), lambda a, i: (a, i, 0)),
            pl.BlockSpec((1, w), lambda a, i: (0, 0)),
            pl.BlockSpec((tc, tc), lambda a, i: (0, 0)),
        ],
        out_specs=pl.BlockSpec((1, tc, w), lambda a, i: (a, i, 0)),
        scratch_shapes=[pltpu.VMEM((1, w), F32)],
        compiler_params=_params("arbitrary", "arbitrary"),
        name="fox_cumsum",
    )(misc, bias_vec, tri)


def _merge_kernel(x_ref, g_ref, sc_ref, sh_ref, ga_ref, wm_ref, misc_ref, pg_ref,
                  ocmp_ref, osel_ref, owin_ref, ob_ref, oc_ref, od_ref,
                  wa_ref, wb_ref, wc_ref, wd_ref, wo_ref, o_ref):
    x = x_ref[...]
    d = x.shape[1]
    h = _mod_norm(x, g_ref[...], sc_ref[0], sh_ref[0]).astype(BF16)
    gate = jax.nn.sigmoid(misc_ref[...])
    o_a = (_dot_split(gate, pg_ref[0]) * ocmp_ref[...].astype(F32)
           + _dot_split(gate, pg_ref[1]) * osel_ref[...].astype(F32)
           + _dot_split(gate, pg_ref[2]) * owin_ref[...].astype(F32)).astype(BF16)
    mixed = jnp.zeros(x.shape, F32)
    for m, (o_m, w_ref) in enumerate(((o_a, wa_ref), (ob_ref[...], wb_ref),
                                      (oc_ref[...], wc_ref), (od_ref[...], wd_ref))):
        y = _dot(o_m, w_ref[...])
        gl = _dot(h, wm_ref[:, m * d:(m + 1) * d])
        mixed = mixed + jax.nn.sigmoid(gl) * y
    o_ref[...] = x + ga_ref[0] * _dot(mixed.astype(BF16), wo_ref[...])


def _merge(x2, g, sc, sh, ga, w_merge, misc, pg, o_cmp, o_sel, o_win, o_b, o_c, o_d,
           wa, wb, wc, wd, wo, seq, tt):
    n, d = x2.shape
    tpb = seq // tt
    row = lambda w: pl.BlockSpec((tt, w), lambda i: (i, 0))
    full = lambda a: pl.BlockSpec(a.shape, lambda i: (0,) * a.ndim)
    per_b = pl.BlockSpec((1, 1, d), lambda i: (i // tpb, 0, 0))
    return pl.pallas_call(
        _merge_kernel,
        out_shape=jax.ShapeDtypeStruct((n, d), F32),
        grid=(n // tt,),
        in_specs=[row(d), full(g), per_b, per_b, per_b, full(w_merge), row(W_MIX), full(pg)]
        + [row(W_MIX)] * 6 + [full(wa), full(wb), full(wc), full(wd), full(wo)],
        out_specs=row(d),
        compiler_params=_params("arbitrary"),
        name="merge_out",
    )(x2, g, sc, sh, ga, w_merge, misc, pg, o_cmp, o_sel, o_win, o_b, o_c, o_d, wa, wb, wc, wd, wo)


def _ffn_kernel(x_ref, g_ref, sc_ref, sh_ref, gf_ref, w1_ref, w3_ref, w2_ref, o_ref, h_scr, acc_scr):
    f = pl.program_id(1)

    @pl.when(f == 0)
    def _():
        h_scr[...] = _mod_norm(x_ref[...], g_ref[...], sc_ref[0], sh_ref[0]).astype(BF16)
        acc_scr[...] = jnp.zeros(acc_scr.shape, F32)

    h = h_scr[...]
    a = _dot(h, w1_ref[...])
    b = _dot(h, w3_ref[...])
    acc_scr[...] += _dot((a * jax.nn.sigmoid(a) * b).astype(BF16), w2_ref[...])

    @pl.when(f == pl.num_programs(1) - 1)
    def _():
        o_ref[...] = x_ref[...] + gf_ref[0] * acc_scr[...]


def _ffn(x2, g, sc, sh, gf, w1, w3, w2, seq, tt, tf):
    n, d = x2.shape
    dff = w1.shape[1]
    tpb = seq // tt
    per_b = pl.BlockSpec((1, 1, d), lambda i, f: (i // tpb, 0, 0))
    return pl.pallas_call(
        _ffn_kernel,
        out_shape=jax.ShapeDtypeStruct((n, d), F32),
        grid=(n // tt, dff // tf),
        in_specs=[
            pl.BlockSpec((tt, d), lambda i, f: (i, 0)),
            pl.BlockSpec((1, d), lambda i, f: (0, 0)),
            per_b, per_b, per_b,
            pl.BlockSpec((d, tf), lambda i, f: (0, f)),
            pl.BlockSpec((d, tf), lambda i, f: (0, f)),
            pl.BlockSpec((tf, d), lambda i, f: (f, 0)),
        ],
        out_specs=pl.BlockSpec((tt, d), lambda i, f: (i, 0)),
        scratch_shapes=[pltpu.VMEM((tt, d), BF16), pltpu.VMEM((tt, d), F32)],
        compiler_params=_params("arbitrary", "arbitrary"),
        name="ffn_swiglu",
    )(x2, g, sc, sh, gf, w1, w3, w2)


def _route_kernel(x_ref, g_ref, sc_ref, sh_ref, rw_ref, up_ref, h_ref, rank_ref, gate_ref, cnt_ref):
    hf = _mod_norm(x_ref[...], g_ref[...], sc_ref[0], sh_ref[0])
    h_ref[...] = hf.astype(BF16)
    logits = _dot(rw_ref[...], hf, NT, HI)
    ne, tt = logits.shape
    e_idx = lax.broadcasted_iota(jnp.int32, (ne, 1), 0).astype(F32)
    v1 = jnp.max(logits, axis=0, keepdims=True)
    i1 = jnp.min(jnp.where(logits == v1, e_idx, float(ne)), axis=0, keepdims=True)
    m1 = e_idx == i1
    rest = jnp.where(m1, -3e38, logits)
    v2 = jnp.max(rest, axis=0, keepdims=True)
    i2 = jnp.min(jnp.where(rest == v2, e_idx, float(ne)), axis=0, keepdims=True)
    m2 = e_idx == i2
    e2 = jnp.exp(v2 - v1)
    g1 = 1.0 / (1.0 + e2)
    g2 = e2 / (1.0 + e2)
    routed = m1 | m2
    rf = jnp.where(routed, 1.0, 0.0)
    rank = _dot(rf.astype(BF16), up_ref[...])
    rank = jnp.where(routed, rank, -1.0)
    gate = jnp.where(m1, g1, 0.0) + jnp.where(m2, g2, 0.0)
    for e in range(ne):
        rank_ref[0, e] = rank[e:e + 1, :]
        gate_ref[0, e] = gate[e:e + 1, :]
    cnt = jnp.sum(rf, axis=1, keepdims=True)
    cnt_ref[0] = jnp.broadcast_to(cnt, (ne, LANES))


def _route(x2, g, sc, sh, rw_t, seq, tt):
    n, d = x2.shape
    ne = rw_t.shape[0]
    tpb = seq // tt
    nt = n // tt
    upper = jnp.asarray(np.triu(np.ones((tt, tt), np.float32), 1), BF16)
    per_b = pl.BlockSpec((1, 1, d), lambda i: (i // tpb, 0, 0))
    return pl.pallas_call(
        _route_kernel,
        out_shape=[jax.ShapeDtypeStruct((n, d), BF16),
                   jax.ShapeDtypeStruct((nt, ne, 1, tt), F32),
                   jax.ShapeDtypeStruct((nt, ne, 1, tt), F32),
                   jax.ShapeDtypeStruct((nt, ne, LANES), F32)],
        grid=(nt,),
        in_specs=[
            pl.BlockSpec((tt, d), lambda i: (i, 0)),
            pl.BlockSpec((1, d), lambda i: (0, 0)),
            per_b, per_b,
            pl.BlockSpec((ne, d), lambda i: (0, 0)),
            pl.BlockSpec((tt, tt), lambda i: (0, 0)),
        ],
        out_specs=[pl.BlockSpec((tt, d), lambda i: (i, 0)),
                   pl.BlockSpec((1, ne, 1, tt), lambda i: (i, 0, 0, 0)),
                   pl.BlockSpec((1, ne, 1, tt), lambda i: (i, 0, 0, 0)),
                   pl.BlockSpec((1, ne, LANES), lambda i: (i, 0, 0))],
        compiler_params=_params("arbitrary"),
        name="moe_route",
    )(x2, g, sc, sh, rw_t, upper)


def _moe_kernel(cnt_ref, x_ref, gf_ref, h_ref, rank_ref, gate_ref, w1_ref, w3_ref, w2_ref,
                o_ref, acc_scr, xs_scr, y_scr, *, chunk):
    i, e, f = pl.program_id(0), pl.program_id(1), pl.program_id(2)
    ne, nf = pl.num_programs(1), pl.num_programs(2)

    @pl.when((e == 0) & (f == 0))
    def _():
        acc_scr[...] = jnp.zeros(acc_scr.shape, F32)

    count = cnt_ref[i * ne + e]
    rank = rank_ref[0, 0]
    gate = gate_ref[0, 0]
    n_small = (count + chunk - 1) // chunk
    n_big = (count + 2 * chunk - 1) // (2 * chunk)

    def one_hot(c, rows):
        r = c * rows + lax.broadcasted_iota(jnp.int32, (rows, 1), 0)
        return rank == r.astype(F32)

    def rows_of(c, rows):
        return pl.ds(pl.multiple_of(c * rows, rows), rows)

    @pl.when(f == 0)
    def _():
        h = h_ref[...]

        def gather(c, carry):
            p = jnp.where(one_hot(c, chunk), 1.0, 0.0).astype(BF16)
            xs_scr[rows_of(c, chunk), :] = _dot(p, h).astype(BF16)
            return carry

        lax.fori_loop(0, n_small, gather, 0)

        def clear(c, carry):
            y_scr[rows_of(c, 2 * chunk), :] = jnp.zeros((2 * chunk, y_scr.shape[1]), F32)
            return carry

        lax.fori_loop(0, n_big, clear, 0)

    def expert(c, carry):
        xs = xs_scr[rows_of(c, chunk), :]
        a = _dot(xs, w1_ref[0])
        b = _dot(xs, w3_ref[0])
        y_scr[rows_of(c, chunk), :] += _dot((a * jax.nn.sigmoid(a) * b).astype(BF16), w2_ref[0])
        return carry

    lax.fori_loop(0, n_small, expert, 0)

    @pl.when(f == nf - 1)
    def _():
        def scatter(c, carry):
            hit = one_hot(c, 2 * chunk)
            p = jnp.where(hit, 1.0, 0.0).astype(BF16)
            gcol = jnp.sum(jnp.where(hit, gate, 0.0), axis=-1, keepdims=True)
            acc_scr[...] += _dot(p, (y_scr[rows_of(c, 2 * chunk), :] * gcol).astype(BF16), TN)
            return carry

        lax.fori_loop(0, n_big, scatter, 0)

    @pl.when((e == ne - 1) & (f == nf - 1))
    def _():
        o_ref[...] = x_ref[...] + gf_ref[0] * acc_scr[...]


def _moe(counts, x2, gf, h2, rank, gate, w1, w3, w2, seq, tt, tf, chunk):
    n, d = x2.shape
    ne, _, dff = w1.shape
    tpb = seq // tt
    grid_spec = pltpu.PrefetchScalarGridSpec(
        num_scalar_prefetch=1,
        grid=(n // tt, ne, dff // tf),
        in_specs=[
            pl.BlockSpec((tt, d), lambda i, e, f, c: (i, 0)),
            pl.BlockSpec((1, 1, d), lambda i, e, f, c: (i // tpb, 0, 0)),
            pl.BlockSpec((tt, d), lambda i, e, f, c: (i, 0)),
            pl.BlockSpec((1, 1, 1, tt), lambda i, e, f, c: (i, e, 0, 0)),
            pl.BlockSpec((1, 1, 1, tt), lambda i, e, f, c: (i, e, 0, 0)),
            pl.BlockSpec((1, d, tf), lambda i, e, f, c: (e, 0, f)),
            pl.BlockSpec((1, d, tf), lambda i, e, f, c: (e, 0, f)),
            pl.BlockSpec((1, tf, d), lambda i, e, f, c: (e, f, 0)),
        ],
        out_specs=pl.BlockSpec((tt, d), lambda i, e, f, c: (i, 0)),
        scratch_shapes=[pltpu.VMEM((tt, d), F32), pltpu.VMEM((tt, d), BF16), pltpu.VMEM((tt, d), F32)],
    )
    return pl.pallas_call(
        functools.partial(_moe_kernel, chunk=chunk),
        out_shape=jax.ShapeDtypeStruct((n, d), F32),
        grid_spec=grid_spec,
        compiler_params=_params("arbitrary", "arbitrary", "arbitrary"),
        name="moe_experts",
    )(counts, x2, gf, h2, rank, gate, w1, w3, w2)


def _overlap_matrix(ncp, nc, n_sel):
    c0 = np.arange(ncp) * CMP_STRIDE
    c1 = c0 + CMP_LEN
    s0 = np.arange(n_sel) * SEL_LEN
    s1 = s0 + SEL_LEN
    ov = ((c0[:, None] < s1[None, :]) & (c1[:, None] > s0[None, :])).astype(np.float32)
    ov[nc:] = 0.0
    return jnp.asarray(ov.T)


def _gate_expand():
    pg = np.zeros((3, W_MIX, W_MIX), np.float32)
    for br in range(3):
        for h in range(N_HEADS):
            pg[br, GATE_LANE + 3 * h + br, h * HEAD_DIM:(h + 1) * HEAD_DIM] = 1.0
    return jnp.asarray(pg, BF16)


def _mixer_layer(x2, b, s, mod, norm_g, rope, w_in, qk_gain, pe_k, pe_v, ck1, ck2, cv1, cv2,
                 fox_b, w_branch, w_out):
    n, d = x2.shape
    sh_a, sc_a, g_a = mod[0], mod[1], mod[2]
    w_slab, gain, w_merge = _pack_w_in(w_in, qk_gain)
    bd = jnp.asarray(np.kron(np.eye(N_HEADS), np.full((HEAD_DIM, HEAD_DIM), 1.0 / HEAD_DIM)), BF16)
    sl = _proj(x2, norm_g, sc_a, sh_a, w_slab, gain, bd, rope, s, min(512, s))
    sl = [a.reshape(b, s, W_MIX) for a in sl]
    misc = sl[S_MISC]

    nch = s // CMP_STRIDE
    nc = nch - CMP_LEN // CMP_STRIDE + 1
    kc_raw = sl[S_KC][..., :HEAD_DIM]
    vc_raw = misc[..., VC_LANE:VC_LANE + HEAD_DIM]
    chunks = jnp.stack([kc_raw, vc_raw]).reshape(2, b, nch, CMP_STRIDE * HEAD_DIM)
    chunks_next = jnp.concatenate([chunks[:, :, 1:], jnp.zeros_like(chunks[:, :, :1])], axis=2)
    pe = jnp.stack([pe_k, pe_v]).reshape(2, 1, CMP_LEN * HEAD_DIM)
    kvc = jnp.tile(_compress(chunks, chunks_next, pe, jnp.stack([ck1, cv1]), jnp.stack([ck2, cv2])),
                   (1, 1, 1, N_HEADS))
    overlap = _overlap_matrix(nch, nc, s // SEL_LEN)
    o_cmp, selmask = _cmp_topk(sl[S_QNR], kvc[0], kvc[1].transpose(0, 2, 1).astype(BF16), overlap,
                               min(256, s), nc)
    tq, tk = min(256, s), min(512, s)
    o_sel = _causal_attn(sl[S_QR], sl[S_KSL], sl[S_VSL], tq, tk, sel=selmask)
    o_win = _window_attn(sl[S_QR], sl[S_KW], sl[S_VW], dil=1, window=NSA_WINDOW, tq=256, out_dtype=BF16)

    others = []
    for (wdw, dil) in DIL_CONFIGS[:0:-1]:
        others.append(_window_attn(sl[S_BQ], sl[S_BK], sl[S_BV], dil=dil, window=wdw // dil + 1, tq=256,
                                   out_dtype=F32, emit_lse=True))
    wdw, dil = DIL_CONFIGS[0]
    o_b = _window_attn(sl[S_BQ], sl[S_BK], sl[S_BV], dil=dil, window=wdw // dil + 1, tq=256,
                       out_dtype=BF16, others=others)

    o_c = _stick_breaking(sl[S_CQ], sl[S_CK], sl[S_CV], min(256, s))

    bias_vec = jnp.zeros((1, W_MIX), F32).at[0, FOX_LANE:FOX_LANE + N_HEADS].set(fox_b)
    fcum = _fox_cumsum(misc, bias_vec, min(512, s))
    qk_bound = 1.02 * HEAD_DIM ** 0.5 * jnp.max(jnp.abs(qk_gain[6])) * jnp.max(jnp.abs(qk_gain[7])) + 0.05
    o_d = _causal_attn(sl[S_DQ], sl[S_DK], sl[S_DV], tq, tk, fcum=fcum[..., FOX_LANE:FOX_LANE + N_HEADS],
                       qk_bound=qk_bound)

    wb16 = w_branch.astype(BF16)
    flat = lambda a: a.reshape(n, W_MIX)
    return _merge(x2, norm_g, sc_a, sh_a, g_a, w_merge, flat(misc), _gate_expand(),
                  flat(o_cmp), flat(o_sel), flat(o_win), flat(o_b), flat(o_c), flat(o_d),
                  wb16[0], wb16[1], wb16[2], wb16[3], w_out.astype(BF16), s, min(512, s))


def kernel(x, c, positions, w_ada, b_ada, norm_mix, norm_ffn, w_in, qk_gain, nsa_pe_k, nsa_pe_v,
           nsa_ck_w1, nsa_ck_w2, nsa_cv_w1, nsa_cv_w2, fox_bias, w_branch, w_out,
           ffn_w1, ffn_w3, ffn_w2, router_w, moe_w1, moe_w3, moe_w2):
    b, s, d = x.shape
    depth = w_ada.shape[0]
    rope = _rope_tables(positions)
    mods = _ada(c, w_ada, b_ada).reshape(depth, b, 6, 1, d).transpose(0, 2, 1, 3, 4)
    x2 = x.reshape(b * s, d)
    for l in range(depth):
        mod = mods[l]
        x2 = _mixer_layer(x2, b, s, mod[0:3], norm_mix[l].reshape(1, d), rope, w_in[l], qk_gain[l],
                          nsa_pe_k[l], nsa_pe_v[l], nsa_ck_w1[l], nsa_ck_w2[l], nsa_cv_w1[l],
                          nsa_cv_w2[l], fox_bias[l], w_branch[l], w_out[l])
        sh_f, sc_f, g_f = mod[3], mod[4], mod[5]
        gn = norm_ffn[l].reshape(1, d)
        e = l // 2
        if l % 2 == 0:
            dff = ffn_w1.shape[2]
            x2 = _ffn(x2, gn, sc_f, sh_f, g_f, ffn_w1[e].astype(BF16), ffn_w3[e].astype(BF16),
                      ffn_w2[e].astype(BF16), s, min(1024, s), dff // 2)
        else:
            tt = min(1024, s)
            dff = moe_w1.shape[3]
            h2, rank, gate, cnt = _route(x2, gn, sc_f, sh_f, router_w[e].T, s, tt)
            counts = cnt[:, :, 0].astype(jnp.int32).reshape(-1)
            x2 = _moe(counts, x2, g_f, h2, rank, gate, moe_w1[e].astype(BF16), moe_w3[e].astype(BF16),
                      moe_w2[e].astype(BF16), s, tt, dff // 2, 128)
    return x2.reshape(b, s, d)
```

```python
import functools

import numpy as np
import jax
import jax.numpy as jnp
from jax import lax
from jax.experimental import pallas as pl
from jax.experimental.pallas import tpu as pltpu

F32 = jnp.float32
BF16 = jnp.bfloat16
HI = lax.Precision.HIGHEST

LANES = 128
VMEM_LIMIT = 52 * 1024 * 1024

HEAD_DIM = 64
HEAD_SHIFT = 6
N_HEADS = 4
W_MIX = N_HEADS * HEAD_DIM
N_MIXERS = 4
ROPE_THETA = 500000.0
ROPE_DIMS = HEAD_DIM // 4
ROPE_HALF = ROPE_DIMS // 2
EPS = 1e-6
NEG = -1e30
BIG = 1e30
CMP_LEN = 32
CMP_STRIDE = 16
SEL_LEN = 64
SEL_SHIFT = 6
TOPN = 16
NSA_WINDOW = 512
DIL_CONFIGS = ((128, 1), (512, 4), (2048, 16))
SB_STOP = -110.0
FOX_STOP = 108.0
GATE_LANE = 0
FOX_LANE = 3 * N_HEADS
VC_LANE = HEAD_DIM

NN = (((1,), (0,)), ((), ()))
NT = (((1,), (1,)), ((), ()))
TN = (((0,), (0,)), ((), ()))


def _dot(a, b, dims=NN, precision=None):
    return lax.dot_general(a, b, dims, precision=precision, preferred_element_type=F32)


def _dot_split(a, b_bf16, dims=NN):
    hi = a.astype(BF16)
    lo = (a - hi.astype(F32)).astype(BF16)
    return _dot(hi, b_bf16, dims) + _dot(lo, b_bf16, dims)


def _params(*sem):
    return pltpu.CompilerParams(dimension_semantics=sem, vmem_limit_bytes=VMEM_LIMIT)


def _mod_norm(x, g, sc, sh):
    ms = jnp.mean(x * x, axis=-1, keepdims=True)
    return (x * lax.rsqrt(ms + EPS) * g) * (1.0 + sc) + sh


def _head_masks():
    lane = lax.broadcasted_iota(jnp.int32, (1, W_MIX), 1)
    return [(lane >> HEAD_SHIFT) == h for h in range(N_HEADS)]


def _stack_masked(q, qm_scr, tq):
    for h, hm in enumerate(_head_masks()):
        qm_scr[h * tq:(h + 1) * tq, :] = jnp.where(hm, q, jnp.zeros_like(q))


def _pick_heads(stacked, tq, scale=None):
    out = None
    for h, hm in enumerate(_head_masks()):
        blk = stacked[h * tq:(h + 1) * tq, :]
        if scale is not None:
            blk = blk * scale[h]
        out = jnp.where(hm, blk, 0.0 if out is None else out)
    return out


def _ada_kernel(c_ref, w_ref, b_ref, o_ref):
    c = c_ref[...]
    ca = c * jax.nn.sigmoid(c)
    o_ref[0] = _dot(ca, w_ref[0], NN, HI) + b_ref[0]


def _ada(c, w_ada, b_ada):
    depth, d, n6 = w_ada.shape
    b = c.shape[0]
    tn = n6 // 4
    return pl.pallas_call(
        _ada_kernel,
        out_shape=jax.ShapeDtypeStruct((depth, b, n6), F32),
        grid=(depth, n6 // tn),
        in_specs=[
            pl.BlockSpec((b, d), lambda l, j: (0, 0)),
            pl.BlockSpec((1, d, tn), lambda l, j: (l, 0, j)),
            pl.BlockSpec((1, 1, tn), lambda l, j: (l, 0, j)),
        ],
        out_specs=pl.BlockSpec((1, b, tn), lambda l, j: (l, 0, j)),
        compiler_params=_params("arbitrary", "arbitrary"),
        name="ada_mod",
    )(c, w_ada, b_ada.reshape(depth, 1, n6))


_SLABS = (
    (True, False, F32), (True, True, BF16), (True, False, F32),
    (True, True, BF16), (True, True, BF16), (False, False, BF16), (False, False, BF16),
    (True, True, BF16), (True, True, BF16), (False, False, BF16),
    (False, False, BF16), (False, False, BF16), (False, False, BF16),
    (True, False, BF16), (True, False, BF16), (False, False, BF16),
    (False, False, F32),
)
N_SLABS = len(_SLABS)
(S_QNR, S_QR, S_KC, S_KSL, S_KW, S_VSL, S_VW, S_BQ, S_BK, S_BV, S_CQ, S_CK, S_CV,
 S_DQ, S_DK, S_DV, S_MISC) = range(N_SLABS)
_DIL_SLABS = (S_BQ, S_BK, S_BV)
_DILATIONS = tuple(d for (_, d) in DIL_CONFIGS if d > 1)


def _proj_kernel(x_ref, g_ref, sc_ref, sh_ref, w_ref, gain_ref, bd_ref, rope_ref, *out_refs):
    h = _mod_norm(x_ref[...], g_ref[...], sc_ref[0], sh_ref[0]).astype(BF16)
    bd = bd_ref[...]
    for s, (norm, rope, _) in enumerate(_SLABS):
        y = _dot(h, w_ref[:, s * W_MIX:(s + 1) * W_MIX])
        if norm:
            ms = _dot_split(y * y, bd)
            y = y * lax.rsqrt(ms + EPS)
        y = y * gain_ref[s]
        if rope:
            y = (y * rope_ref[0, 0] + pltpu.roll(y, W_MIX - ROPE_HALF, 1) * rope_ref[0, 1]
                 + pltpu.roll(y, ROPE_HALF, 1) * rope_ref[0, 2])
        out_refs[s][...] = y.astype(out_refs[s].dtype)
        if s in _DIL_SLABS:
            stage_scr = out_refs[-1]
            tt = y.shape[0]
            for c in range(W_MIX // LANES):
                stage_scr[c] = y[:, c * LANES:(c + 1) * LANES]
            for di, dil in enumerate(_DILATIONS):
                o_ref = out_refs[N_SLABS + _DIL_SLABS.index(s) * len(_DILATIONS) + di]
                for r in range(dil):
                    for c in range(W_MIX // LANES):
                        lo = r * W_MIX + c * LANES
                        o_ref[:, lo:lo + LANES] = (
                            stage_scr[c, pl.ds(r, tt // dil, stride=dil), :].astype(o_ref.dtype))


def _proj(x2, g, sc, sh, w_slab, gain, bd, rope, seq, tt):
    n, d = x2.shape
    tpb = seq // tt
    out_shape = [jax.ShapeDtypeStruct((n, W_MIX), dt) for (_, _, dt) in _SLABS]
    out_specs = [pl.BlockSpec((tt, W_MIX), lambda i: (i, 0)) for _ in _SLABS]
    for _ in _DIL_SLABS:
        for dil in _DILATIONS:
            out_shape.append(jax.ShapeDtypeStruct((n // dil, dil * W_MIX), BF16))
            out_specs.append(pl.BlockSpec((tt // dil, dil * W_MIX), lambda i: (i, 0)))
    return pl.pallas_call(
        _proj_kernel,
        out_shape=out_shape,
        grid=(n // tt,),
        in_specs=[
            pl.BlockSpec((tt, d), lambda i: (i, 0)),
            pl.BlockSpec((1, d), lambda i: (0, 0)),
            pl.BlockSpec((1, 1, d), lambda i: (i // tpb, 0, 0)),
            pl.BlockSpec((1, 1, d), lambda i: (i // tpb, 0, 0)),
            pl.BlockSpec((d, N_SLABS * W_MIX), lambda i: (0, 0)),
            pl.BlockSpec((N_SLABS, 1, W_MIX), lambda i: (0, 0, 0)),
            pl.BlockSpec((W_MIX, W_MIX), lambda i: (0, 0)),
            pl.BlockSpec((1, 3, tt, W_MIX), lambda i: (i // tpb, 0, i % tpb, 0)),
        ],
        out_specs=out_specs,
        scratch_shapes=[pltpu.VMEM((W_MIX // LANES, tt, LANES), F32)],
        compiler_params=_params("arbitrary"),
        name="in_proj",
    )(x2, g, sc, sh, w_slab, gain, bd, rope)


def _pack_w_in(w_in, qk_gain):
    d = w_in.shape[0]
    o = 0
    a_q = w_in[:, o:o + W_MIX]; o += W_MIX
    a_kv = w_in[:, o:o + 6 * HEAD_DIM]; o += 6 * HEAD_DIM
    a_g = w_in[:, o:o + 3 * N_HEADS]; o += 3 * N_HEADS
    b_qkv = w_in[:, o:o + 3 * W_MIX]; o += 3 * W_MIX
    c_qkv = w_in[:, o:o + 3 * W_MIX]; o += 3 * W_MIX
    d_qkv = w_in[:, o:o + 3 * W_MIX]; o += 3 * W_MIX
    d_f = w_in[:, o:o + N_HEADS]; o += N_HEADS
    w_merge = w_in[:, o:]
    kc, vc, ksl, vsl, kw, vw = (a_kv[:, i * HEAD_DIM:(i + 1) * HEAD_DIM] for i in range(6))
    zeros = lambda w: jnp.zeros((d, w), w_in.dtype)
    rep = lambda w: jnp.tile(w, (1, N_HEADS))
    misc = jnp.concatenate([a_g, d_f, zeros(VC_LANE - FOX_LANE - N_HEADS), vc, zeros(W_MIX - 2 * HEAD_DIM)], axis=1)
    slabs = [
        a_q, a_q, jnp.concatenate([kc, zeros(W_MIX - HEAD_DIM)], axis=1),
        rep(ksl), rep(kw), rep(vsl), rep(vw),
        b_qkv[:, :W_MIX], b_qkv[:, W_MIX:2 * W_MIX], b_qkv[:, 2 * W_MIX:],
        c_qkv[:, :W_MIX], c_qkv[:, W_MIX:2 * W_MIX], c_qkv[:, 2 * W_MIX:],
        d_qkv[:, :W_MIX], d_qkv[:, W_MIX:2 * W_MIX], d_qkv[:, 2 * W_MIX:],
        misc,
    ]
    w_slab = jnp.concatenate(slabs, axis=1).astype(BF16)
    scale = HEAD_DIM ** -0.5
    t4 = lambda gvec: jnp.tile(gvec, N_HEADS)
    one = jnp.ones((W_MIX,), F32)
    gains = [
        t4(qk_gain[0]) * scale, t4(qk_gain[0]) * scale,
        jnp.concatenate([qk_gain[1], jnp.ones((W_MIX - HEAD_DIM,), F32)]),
        t4(qk_gain[2]), t4(qk_gain[3]), one, one,
        t4(qk_gain[4]) * scale, t4(qk_gain[5]), one,
        one * scale, one, one,
        t4(qk_gain[6]) * scale, t4(qk_gain[7]), one,
        one,
    ]
    gain = jnp.stack(gains).reshape(N_SLABS, 1, W_MIX).astype(F32)
    return w_slab, gain, w_merge.astype(BF16)


def _rope_tables(positions):
    inv = ROPE_THETA ** (-jnp.arange(0, ROPE_DIMS, 2, dtype=F32) / ROPE_DIMS)
    ang = positions.astype(F32)[..., None] * inv
    cos, sin = jnp.cos(ang), jnp.sin(ang)
    b, s, _ = cos.shape
    pad1 = jnp.ones((b, s, HEAD_DIM - ROPE_DIMS), F32)
    pad0 = jnp.zeros((b, s, HEAD_DIM - ROPE_DIMS), F32)
    z8 = jnp.zeros_like(sin)
    c64 = jnp.concatenate([cos, cos, pad1], axis=-1)
    s1_64 = jnp.concatenate([-sin, z8, pad0], axis=-1)
    s2_64 = jnp.concatenate([z8, sin, pad0], axis=-1)
    heads = lambda t: jnp.tile(t, (1, 1, N_HEADS))
    return jnp.stack([heads(c64), heads(s1_64), heads(s2_64)], axis=1)


def _compress_kernel(a_ref, b_ref, pe_ref, w1_ref, w2_ref, o_ref):
    half = w1_ref.shape[1] // 2
    w1 = w1_ref[0]
    hid = (_dot(a_ref[0, 0], w1[:half], NN, HI) + _dot(b_ref[0, 0], w1[half:], NN, HI)
           + _dot(pe_ref[0], w1, NN, HI))
    o_ref[0, 0] = _dot(jax.nn.gelu(hid), w2_ref[0], NN, HI)


def _compress(ch, chn, pe, w1, w2):
    _, b, ncp, cw = ch.shape
    hid = w1.shape[2]
    return pl.pallas_call(
        _compress_kernel,
        out_shape=jax.ShapeDtypeStruct((2, b, ncp, HEAD_DIM), F32),
        grid=(2, b),
        in_specs=[
            pl.BlockSpec((1, 1, ncp, cw), lambda k, i: (k, i, 0, 0)),
            pl.BlockSpec((1, 1, ncp, cw), lambda k, i: (k, i, 0, 0)),
            pl.BlockSpec((1, 1, 2 * cw), lambda k, i: (k, 0, 0)),
            pl.BlockSpec((1, 2 * cw, hid), lambda k, i: (k, 0, 0)),
            pl.BlockSpec((1, hid, HEAD_DIM), lambda k, i: (k, 0, 0)),
        ],
        out_specs=pl.BlockSpec((1, 1, ncp, HEAD_DIM), lambda k, i: (k, i, 0, 0)),
        compiler_params=_params("arbitrary", "arbitrary"),
        name="nsa_compress",
    )(ch, chn, pe, w1, w2)


def _cmp_kernel(q_ref, kc_ref, vct_ref, ovt_ref, o_ref, sel_ref, qh_scr, ql_scr, acc_scr, *, tq, nc, n_sel):
    i = pl.program_id(1)
    ncp = kc_ref.shape[1]
    q = q_ref[0]
    q_hi = q.astype(BF16)
    _stack_masked(q_hi, qh_scr, tq)
    _stack_masked((q - q_hi.astype(F32)).astype(BF16), ql_scr, tq)
    kc = kc_ref[0]
    k_hi = kc.astype(BF16)
    k_lo = (kc - k_hi.astype(F32)).astype(BF16)
    ss = []
    for h in range(N_HEADS):
        rs = slice(h * tq, (h + 1) * tq)
        ss.append(_dot(k_hi, qh_scr[rs, :], NT) + _dot(k_lo, qh_scr[rs, :], NT)
                  + _dot(k_hi, ql_scr[rs, :], NT))
    t = i * tq + lax.broadcasted_iota(jnp.int32, (1, tq), 1)
    c = lax.broadcasted_iota(jnp.int32, (ncp, 1), 0)
    mask = (c * CMP_STRIDE + (CMP_LEN - 1) <= t) & (c < nc)
    psum = None
    ps = []
    for h in range(N_HEADS):
        sm = jnp.where(mask, ss[h], NEG)
        m = jnp.max(sm, axis=0, keepdims=True)
        e = jnp.where(mask, jnp.exp(sm - m), 0.0)
        l = jnp.sum(e, axis=0, keepdims=True)
        p = e / jnp.maximum(l, 1e-30)
        psum = p if psum is None else psum + p
        ps.append(p.astype(BF16))
    vct = vct_ref[0]
    for h in range(N_HEADS):
        rs = slice(h * HEAD_DIM, (h + 1) * HEAD_DIM)
        acc_scr[rs, :] = _dot(vct[rs, :], ps[h])
    o_ref[0] = acc_scr[...].T.astype(o_ref.dtype)
    imp = _dot(ovt_ref[...], psum, NN, HI)
    j = lax.broadcasted_iota(jnp.int32, (n_sel, 1), 0)
    cur = t >> SEL_SHIFT
    valid = j <= cur
    forced = (j == 0) | (j == cur) | (j == cur - 1)
    score = jnp.where(valid, jnp.where(forced, BIG, imp), NEG)
    sel = jnp.zeros((n_sel, tq), F32)
    jf = j.astype(F32)
    for _ in range(min(TOPN, n_sel)):
        mx = jnp.max(score, axis=0, keepdims=True)
        idx = jnp.min(jnp.where(score == mx, jf, float(n_sel)), axis=0, keepdims=True)
        pick = jf == idx
        sel = jnp.where(pick, 1.0, sel)
        score = jnp.where(pick, -3e38, score)
    sel_ref[0] = sel.astype(sel_ref.dtype)


def _cmp_topk(q_nr, kc_rep, vc_rep, overlap, tq, nc):
    b, seq, _ = q_nr.shape
    ncp = kc_rep.shape[1]
    n_sel = seq // SEL_LEN
    return pl.pallas_call(
        functools.partial(_cmp_kernel, tq=tq, nc=nc, n_sel=n_sel),
        out_shape=[jax.ShapeDtypeStruct((b, seq, W_MIX), BF16),
                   jax.ShapeDtypeStruct((b, n_sel, seq), BF16)],
        grid=(b, seq // tq),
        in_specs=[
            pl.BlockSpec((1, tq, W_MIX), lambda g, i: (g, i, 0)),
            pl.BlockSpec((1, ncp, W_MIX), lambda g, i: (g, 0, 0)),
            pl.BlockSpec((1, W_MIX, ncp), lambda g, i: (g, 0, 0)),
            pl.BlockSpec((n_sel, ncp), lambda g, i: (0, 0)),
        ],
        out_specs=[pl.BlockSpec((1, tq, W_MIX), lambda g, i: (g, i, 0)),
                   pl.BlockSpec((1, n_sel, tq), lambda g, i: (g, 0, i))],
        scratch_shapes=[pltpu.VMEM((N_HEADS * tq, W_MIX), BF16), pltpu.VMEM((N_HEADS * tq, W_MIX), BF16),
                        pltpu.VMEM((W_MIX, tq), F32)],
        compiler_params=_params("arbitrary", "arbitrary"),
        name="nsa_cmp_topk",
    )(q_nr, kc_rep, vc_rep, overlap)


def _causal_kernel(*refs, tq, tk, has_sel, has_bias):
    it = iter(refs)
    q_ref, k_ref, vt_ref = next(it), next(it), next(it)
    sel_ref = next(it) if has_sel else None
    fa_ref, fb_ref, fq_ref, thr_ref = (next(it) for _ in range(4)) if has_bias else (None,) * 4
    o_ref, qm_scr, acc_scr, sa_scr, sb_scr, m_scr, l_scr = (next(it) for _ in range(7))
    i = pl.program_id(1)
    t0 = i * tq
    _stack_masked(q_ref[0], qm_scr, tq)
    acc_scr[...] = jnp.zeros(acc_scr.shape, F32)
    t_pos = t0 + lax.broadcasted_iota(jnp.int32, (1, tq), 1)

    n_s = N_HEADS + (1 if has_sel else 0)

    def scores(kt, s_buf):
        k_t = k_ref[0, kt]
        for h in range(N_HEADS):
            s = _dot(k_t, qm_scr[h * tq:(h + 1) * tq, :], NT)
            if has_bias:
                s = s + _dot(fa_ref[0, kt], fb_ref[0, h])
            s_buf[h] = s
        if has_sel:
            s_pos = kt * tk + lax.broadcasted_iota(jnp.int32, (tk, 1), 0)
            blk = lax.broadcasted_iota(jnp.int32, (1, sel_ref.shape[1]), 1)
            expand = jnp.where((s_pos >> SEL_SHIFT) == blk, 1.0, 0.0).astype(BF16)
            s_buf[N_HEADS] = _dot(expand, sel_ref[0])

    def update(kt, s_buf, diag):
        vt_t = vt_ref[0, kt]
        mask = None
        if diag:
            s_pos = kt * tk + lax.broadcasted_iota(jnp.int32, (tk, 1), 0)
            mask = s_pos <= t_pos
        if has_sel:
            picked = s_buf[N_HEADS] > 0.5
            mask = picked if mask is None else (mask & picked)
        ps, alphas = [], []
        for h in range(N_HEADS):
            s = s_buf[h]
            if mask is not None:
                s = jnp.where(mask, s, NEG)
            m_old = m_scr[h]
            m_new = jnp.maximum(m_old, jnp.max(s, axis=0, keepdims=True))
            p = jnp.exp(s - m_new)
            alpha = jnp.exp(m_old - m_new)
            m_scr[h] = m_new
            l_scr[h] = alpha * l_scr[h] + jnp.sum(p, axis=0, keepdims=True)
            ps.append(p.astype(BF16))
            alphas.append(alpha)
        for h in range(N_HEADS):
            rs = slice(h * HEAD_DIM, (h + 1) * HEAD_DIM)
            acc_scr[rs, :] = alphas[h] * acc_scr[rs, :] + _dot(vt_t[rs, :], ps[h])

    def live(kt_next):
        lane = lax.broadcasted_iota(jnp.int32, (1, thr_ref.shape[3]), 1)
        hit = None
        for h in range(N_HEADS):
            top = jnp.max(fq_ref[0, h] - m_scr[h], axis=-1, keepdims=True)
            need = top >= thr_ref[0, h]
            hit = need if hit is None else (hit | need)
        return jnp.max(jnp.where(hit & (lane == kt_next), 1.0, 0.0)) > 0.5

    n_last = t0 // tk
    m_scr[...] = jnp.full(m_scr.shape, NEG, F32)
    l_scr[...] = jnp.zeros(l_scr.shape, F32)
    scores(n_last, sa_scr)
    scores(jnp.maximum(n_last - 1, 0), sb_scr)
    update(n_last, sa_scr, True)

    def pair(j):
        kt = n_last - 1 - 2 * j
        scores(kt - 1, sa_scr)
        update(kt, sb_scr, False)
        scores(jnp.maximum(kt - 2, 0), sb_scr)
        update(kt - 1, sa_scr, False)
        return kt - 2

    n_pairs = n_last // 2
    if has_bias:
        def cond(state):
            j, go = state
            return (j < n_pairs) & go

        def body(state):
            j, _ = state
            return j + 1, live(pair(j))

        _, go = lax.while_loop(cond, body, (jnp.int32(0), live(n_last - 1)))
    else:
        lax.fori_loop(0, n_pairs, lambda j, c: (pair(j), c)[1], 0)
        go = True

    @pl.when((n_last % 2 == 1) & go)
    def _():
        update(0, sb_scr, False)

    ls = [l_scr[h] for h in range(N_HEADS)]
    for h in range(N_HEADS):
        rs = slice(h * HEAD_DIM, (h + 1) * HEAD_DIM)
        acc_scr[rs, :] = acc_scr[rs, :] / ls[h]
    o_ref[0] = acc_scr[...].T.astype(o_ref.dtype)


def _split3(x):
    def cut(v):
        bits = lax.bitcast_convert_type(v, jnp.uint32) & jnp.uint32(0xFFFF0000)
        return lax.bitcast_convert_type(bits, F32)

    hi = cut(x)
    r1 = x - hi
    mid = cut(r1)
    lo = r1 - mid
    return hi.astype(BF16), mid.astype(BF16), lo.astype(BF16)


def _causal_attn(q, k, v, tq, tk, sel=None, fcum=None, qk_bound=None):
    b, s, _ = q.shape
    nk = s // tk
    rows = N_HEADS * tq
    n_s = N_HEADS + (1 if sel is not None else 0)
    vt = v.reshape(b, nk, tk, W_MIX).transpose(0, 1, 3, 2)
    args = [q, k.reshape(b, nk, tk, W_MIX), vt]
    in_specs = [
        pl.BlockSpec((1, tq, W_MIX), lambda a, i: (a, i, 0)),
        pl.BlockSpec((1, nk, tk, W_MIX), lambda a, i: (a, 0, 0, 0)),
        pl.BlockSpec((1, nk, W_MIX, tk), lambda a, i: (a, 0, 0, 0)),
    ]
    if sel is not None:
        args.append(sel)
        in_specs.append(pl.BlockSpec((1, sel.shape[1], tq), lambda a, i: (a, 0, i)))
    if fcum is not None:
        nf = 2 * 3 + 2
        ones = jnp.ones(fcum.shape, BF16)
        zero = jnp.zeros(fcum.shape, BF16)
        parts = _split3(fcum)
        key_f = jnp.stack([ones, ones, ones] + [-p for p in parts] + [zero, zero], axis=-1)
        qry_f = jnp.stack(list(parts) + [ones, ones, ones, zero, zero], axis=-1)
        fa = key_f.reshape(b, nk, tk, N_HEADS * nf)
        eye = jnp.eye(N_HEADS, dtype=BF16)
        fb = jnp.einsum('bshf,hg->bgshf', qry_f, eye).reshape(b, N_HEADS, s, N_HEADS * nf).transpose(0, 1, 3, 2)
        f_rows = fcum.transpose(0, 2, 1)
        f_end = f_rows[:, :, tk - 1::tk]
        thr = jnp.pad(f_end - (qk_bound + FOX_STOP), ((0, 0), (0, 0), (0, LANES - nk)),
                      constant_values=BIG).reshape(b, N_HEADS, 1, LANES)
        args += [fa, fb, f_rows.reshape(b, N_HEADS, 1, s), thr]
        in_specs += [pl.BlockSpec((1, nk, tk, N_HEADS * nf), lambda a, i: (a, 0, 0, 0)),
                     pl.BlockSpec((1, N_HEADS, N_HEADS * nf, tq), lambda a, i: (a, 0, 0, i)),
                     pl.BlockSpec((1, N_HEADS, 1, tq), lambda a, i: (a, 0, 0, i)),
                     pl.BlockSpec((1, N_HEADS, 1, LANES), lambda a, i: (a, 0, 0, 0))]
    return pl.pallas_call(
        functools.partial(_causal_kernel, tq=tq, tk=tk, has_sel=sel is not None, has_bias=fcum is not None),
        out_shape=jax.ShapeDtypeStruct((b, s, W_MIX), BF16),
        grid=(b, s // tq),
        in_specs=in_specs,
        out_specs=pl.BlockSpec((1, tq, W_MIX), lambda a, i: (a, i, 0)),
        scratch_shapes=[pltpu.VMEM((rows, W_MIX), BF16), pltpu.VMEM((W_MIX, tq), F32),
                        pltpu.VMEM((n_s, tk, tq), F32), pltpu.VMEM((n_s, tk, tq), F32),
                        pltpu.VMEM((N_HEADS, 1, tq), F32), pltpu.VMEM((N_HEADS, 1, tq), F32)],
        compiler_params=_params("arbitrary", "arbitrary"),
        name="causal_sel%d_bias%d" % (sel is not None, fcum is not None),
    )(*args)


def _window_kernel(*refs, tq, wk, pad, window, ls, emit_lse, other_dils):
    it = iter(refs)
    q_ref, k_ref, v_ref = next(it), next(it), next(it)
    n_other = len(other_dils)
    others_in = [(next(it), next(it)) for _ in range(n_other)]
    o_ref = next(it)
    lse_ref = next(it) if emit_lse else None
    qm_scr, p_scr = next(it), next(it)
    others = []
    if n_other:
        og_scr, lg_scr = next(it), next(it)
        for g, (dg, (og_ref, lg_ref)) in enumerate(zip(other_dils, others_in)):
            for r in range(dg):
                rows = pl.ds(r, tq // dg, stride=dg)
                for c in range(W_MIX // LANES):
                    lo = r * W_MIX + c * LANES
                    og_scr[g, c, rows, :] = og_ref[0, :, lo:lo + LANES]
                lg_scr[g, rows, :] = lg_ref[0, :, r * LANES:(r + 1) * LANES]
            others.append((og_scr.at[g], lg_scr.at[g]))
    t0 = pl.program_id(2) * tq
    start = pl.multiple_of(jnp.clip(t0 - pad, 0, ls - wk), LANES)
    _stack_masked(q_ref[0], qm_scr, tq)
    s_all = _dot(qm_scr[...], k_ref[0, pl.ds(start, wk), :], NT)
    t_pos = t0 + lax.broadcasted_iota(jnp.int32, (tq, 1), 0)
    s_pos = start + lax.broadcasted_iota(jnp.int32, (1, wk), 1)
    mask = (s_pos <= t_pos) & (t_pos - s_pos < window)
    ms, ls_ = [], []
    for h in range(N_HEADS):
        rs = slice(h * tq, (h + 1) * tq)
        s = jnp.where(mask, s_all[rs, :], NEG)
        m = jnp.max(s, axis=-1, keepdims=True)
        p = jnp.exp(s - m)
        ms.append(m)
        ls_.append(jnp.sum(p, axis=-1, keepdims=True))
        p_scr[rs, :] = p.astype(BF16)
    pv = _dot(p_scr[...], v_ref[0, pl.ds(start, wk), :])
    if n_other:
        scales, dens = [], []
        lses = [m + jnp.log(l) for m, l in zip(ms, ls_)]
        out = None
        hms = _head_masks()
        for h in range(N_HEADS):
            other_lse = [lr[:, h:h + 1] for (_, lr) in others]
            top = lses[h]
            for ol in other_lse:
                top = jnp.maximum(top, ol)
            w_self = jnp.exp(lses[h] - top)
            num = pv[h * tq:(h + 1) * tq, :] * (w_self / ls_[h])
            den = w_self
            for (orf, _), ol in zip(others, other_lse):
                w = jnp.exp(ol - top)
                num = num + w * jnp.concatenate([orf[c] for c in range(W_MIX // LANES)], axis=1)
                den = den + w
            out = jnp.where(hms[h], num / den, 0.0 if out is None else out)
        o_ref[0] = out.astype(o_ref.dtype)
    else:
        o_ref[0] = _pick_heads(pv, tq, [1.0 / l for l in ls_]).astype(o_ref.dtype)
    if emit_lse:
        lane = lax.broadcasted_iota(jnp.int32, (1, LANES), 1)
        tile = jnp.zeros((tq, LANES), F32)
        for h in range(N_HEADS):
            tile = jnp.where(lane == h, ms[h] + jnp.log(ls_[h]), tile)
        lse_ref[0] = tile


def _window_attn(q, k, v, *, dil, window, tq, out_dtype, emit_lse=False, others=()):
    b, ls, _ = q.shape
    tq = min(tq, ls)
    pad = -(-(window - 1) // LANES) * LANES
    wk = min(tq + pad, ls)
    rows = N_HEADS * tq
    args = [q, k, v]
    in_specs = [
        pl.BlockSpec((1, tq, W_MIX), lambda a, r, i: (a, i, r)),
        pl.BlockSpec((1, ls, W_MIX), lambda a, r, i: (a, 0, r)),
        pl.BlockSpec((1, ls, W_MIX), lambda a, r, i: (a, 0, r)),
    ]
    scratch = [pltpu.VMEM((rows, W_MIX), BF16), pltpu.VMEM((rows, wk), BF16)]
    for (o_g, lse_g, dg) in others:
        args += [o_g, lse_g]
        in_specs += [pl.BlockSpec((1, tq // dg, dg * W_MIX), lambda a, r, i: (a, i, 0)),
                     pl.BlockSpec((1, tq // dg, dg * LANES), lambda a, r, i: (a, i, 0))]
    if others:
        scratch += [pltpu.VMEM((len(others), W_MIX // LANES, tq, LANES), F32),
                    pltpu.VMEM((len(others), tq, LANES), F32)]
    out_shape = [jax.ShapeDtypeStruct((b, ls, dil * W_MIX), out_dtype)]
    out_specs = [pl.BlockSpec((1, tq, W_MIX), lambda a, r, i: (a, i, r))]
    if emit_lse:
        out_shape.append(jax.ShapeDtypeStruct((b, ls, dil * LANES), F32))
        out_specs.append(pl.BlockSpec((1, tq, LANES), lambda a, r, i: (a, i, r)))
    res = pl.pallas_call(
        functools.partial(_window_kernel, tq=tq, wk=wk, pad=pad, window=window, ls=ls,
                          emit_lse=emit_lse, other_dils=tuple(dg for (_, _, dg) in others)),
        out_shape=out_shape,
        grid=(b, dil, ls // tq),
        in_specs=in_specs,
        out_specs=out_specs,
        scratch_shapes=scratch,
        compiler_params=_params("arbitrary", "arbitrary", "arbitrary"),
        name="window_d%d_w%d" % (dil, window),
    )(*args)
    return (res[0], res[1], dil) if emit_lse else res[0]


def _sb_kernel(q_ref, k_ref, v_ref, tri_ref, o_ref, qm_scr, a_scr, carry_scr, acc_scr, *, tq):
    i = pl.program_id(1)
    t0 = i * tq
    rows = N_HEADS * tq
    _stack_masked(q_ref[0], qm_scr, tq)
    carry_scr[...] = jnp.zeros(carry_scr.shape, F32)
    acc_scr[...] = jnp.zeros(acc_scr.shape, F32)
    r = lax.broadcasted_iota(jnp.int32, (rows, 1), 0)
    t_pos = t0 + (r & (tq - 1))

    def tile(kt, diag):
        z = _dot(qm_scr[...], k_ref[0, kt], NT)
        lg = -(jnp.maximum(z, 0.0) + jnp.log(1.0 + jnp.exp(-jnp.abs(z))))
        if diag:
            s_pos = kt * tq + lax.broadcasted_iota(jnp.int32, (1, tq), 1)
            strict = s_pos < t_pos
            lg = jnp.where(strict, lg, 0.0)
        cum = _dot_split(lg, tri_ref[...]) + carry_scr[...]
        a = jnp.exp(z + cum)
        if diag:
            a = jnp.where(strict, a, 0.0)
        a_scr[...] = a.astype(BF16)
        acc_scr[...] += _dot(a_scr[...], v_ref[0, kt])
        carry_scr[...] += jnp.sum(lg, axis=-1, keepdims=True)

    tile(i, True)

    def cond(state):
        j, top = state
        return (j < i) & (top > SB_STOP)

    def body(state):
        j, _ = state
        tile(i - 1 - j, False)
        return j + 1, jnp.max(carry_scr[...])

    lax.while_loop(cond, body, (jnp.int32(0), jnp.max(carry_scr[...])))
    o_ref[0] = _pick_heads(acc_scr[...], tq).astype(o_ref.dtype)


def _stick_breaking(q, k, v, tq):
    b, s, _ = q.shape
    nk = s // tq
    rows = N_HEADS * tq
    tri = jnp.asarray(np.tril(np.ones((tq, tq), np.float32)), BF16)
    return pl.pallas_call(
        functools.partial(_sb_kernel, tq=tq),
        out_shape=jax.ShapeDtypeStruct((b, s, W_MIX), BF16),
        grid=(b, nk),
        in_specs=[
            pl.BlockSpec((1, tq, W_MIX), lambda a, i: (a, i, 0)),
            pl.BlockSpec((1, nk, tq, W_MIX), lambda a, i: (a, 0, 0, 0)),
            pl.BlockSpec((1, nk, tq, W_MIX), lambda a, i: (a, 0, 0, 0)),
            pl.BlockSpec((tq, tq), lambda a, i: (0, 0)),
        ],
        out_specs=pl.BlockSpec((1, tq, W_MIX), lambda a, i: (a, i, 0)),
        scratch_shapes=[pltpu.VMEM((rows, W_MIX), BF16), pltpu.VMEM((rows, tq), BF16),
                        pltpu.VMEM((rows, 1), F32), pltpu.VMEM((rows, W_MIX), F32)],
        compiler_params=_params("arbitrary", "arbitrary"),
        name="stick_breaking",
    )(q, k.reshape(b, nk, tq, W_MIX), v.reshape(b, nk, tq, W_MIX), tri)


def _foxcum_kernel(x_ref, b_ref, tri_ref, o_ref, carry_scr):
    @pl.when(pl.program_id(1) == 0)
    def _():
        carry_scr[...] = jnp.zeros(carry_scr.shape, F32)

    z = x_ref[0] + b_ref[...]
    logf = jnp.minimum(z, 0.0) - jnp.log(1.0 + jnp.exp(-jnp.abs(z)))
    cum = _dot(tri_ref[...], logf, NN, HI) + carry_scr[...]
    o_ref[0] = cum
    carry_scr[...] = cum[cum.shape[0] - 1:, :]


def _fox_cumsum(misc, bias_vec, tc):
    b, s, w = misc.shape
    tri = jnp.asarray(np.tril(np.ones((tc, tc), np.float32)))
    return pl.pallas_call(
        _foxcum_kernel,
        out_shape=jax.ShapeDtypeStruct((b, s, w), F32),
        grid=(b, s // tc),
        in_specs=[
            pl.BlockSpec((1, tc, w), lambda a, i: (a, i, 0)),
            pl.BlockSpec((1, w), lambda a, i: (0, 0)),
            pl.BlockSpec((tc, tc), lambda a, i: (0, 0)),
        ],
        out_specs=pl.BlockSpec((1, tc, w), lambda a, i: (a, i, 0)),
        scratch_shapes=[pltpu.VMEM((1, w), F32)],
        compiler_params=_params("arbitrary", "arbitrary"),
        name="fox_cumsum",
    )(misc, bias_vec, tri)


def _merge_kernel(x_ref, g_ref, sc_ref, sh_ref, ga_ref, wm_ref, misc_ref, pg_ref,
                  ocmp_ref, osel_ref, owin_ref, ob_ref, oc_ref, od_ref,
                  wa_ref, wb_ref, wc_ref, wd_ref, wo_ref, o_ref):
    x = x_ref[...]
    d = x.shape[1]
    h = _mod_norm(x, g_ref[...], sc_ref[0], sh_ref[0]).astype(BF16)
    gate = jax.nn.sigmoid(misc_ref[...])
    o_a = (_dot_split(gate, pg_ref[0]) * ocmp_ref[...].astype(F32)
           + _dot_split(gate, pg_ref[1]) * osel_ref[...].astype(F32)
           + _dot_split(gate, pg_ref[2]) * owin_ref[...].astype(F32)).astype(BF16)
    mixed = jnp.zeros(x.shape, F32)
    for m, (o_m, w_ref) in enumerate(((o_a, wa_ref), (ob_ref[...], wb_ref),
                                      (oc_ref[...], wc_ref), (od_ref[...], wd_ref))):
        y = _dot(o_m, w_ref[...])
        gl = _dot(h, wm_ref[:, m * d:(m + 1) * d])
        mixed = mixed + jax.nn.sigmoid(gl) * y
    o_ref[...] = x + ga_ref[0] * _dot(mixed.astype(BF16), wo_ref[...])


def _merge(x2, g, sc, sh, ga, w_merge, misc, pg, o_cmp, o_sel, o_win, o_b, o_c, o_d,
           wa, wb, wc, wd, wo, seq, tt):
    n, d = x2.shape
    tpb = seq // tt
    row = lambda w: pl.BlockSpec((tt, w), lambda i: (i, 0))
    full = lambda a: pl.BlockSpec(a.shape, lambda i: (0,) * a.ndim)
    per_b = pl.BlockSpec((1, 1, d), lambda i: (i // tpb, 0, 0))
    return pl.pallas_call(
        _merge_kernel,
        out_shape=jax.ShapeDtypeStruct((n, d), F32),
        grid=(n // tt,),
        in_specs=[row(d), full(g), per_b, per_b, per_b, full(w_merge), row(W_MIX), full(pg)]
        + [row(W_MIX)] * 6 + [full(wa), full(wb), full(wc), full(wd), full(wo)],
        out_specs=row(d),
        compiler_params=_params("arbitrary"),
        name="merge_out",
    )(x2, g, sc, sh, ga, w_merge, misc, pg, o_cmp, o_sel, o_win, o_b, o_c, o_d, wa, wb, wc, wd, wo)


def _ffn_kernel(x_ref, g_ref, sc_ref, sh_ref, gf_ref, w1_ref, w3_ref, w2_ref, o_ref, h_scr, acc_scr):
    f = pl.program_id(1)

    @pl.when(f == 0)
    def _():
        h_scr[...] = _mod_norm(x_ref[...], g_ref[...], sc_ref[0], sh_ref[0]).astype(BF16)
        acc_scr[...] = jnp.zeros(acc_scr.shape, F32)

    h = h_scr[...]
    a = _dot(h, w1_ref[...])
    b = _dot(h, w3_ref[...])
    acc_scr[...] += _dot((a * jax.nn.sigmoid(a) * b).astype(BF16), w2_ref[...])

    @pl.when(f == pl.num_programs(1) - 1)
    def _():
        o_ref[...] = x_ref[...] + gf_ref[0] * acc_scr[...]


def _ffn(x2, g, sc, sh, gf, w1, w3, w2, seq, tt, tf):
    n, d = x2.shape
    dff = w1.shape[1]
    tpb = seq // tt
    per_b = pl.BlockSpec((1, 1, d), lambda i, f: (i // tpb, 0, 0))
    return pl.pallas_call(
        _ffn_kernel,
        out_shape=jax.ShapeDtypeStruct((n, d), F32),
        grid=(n // tt, dff // tf),
        in_specs=[
            pl.BlockSpec((tt, d), lambda i, f: (i, 0)),
            pl.BlockSpec((1, d), lambda i, f: (0, 0)),
            per_b, per_b, per_b,
            pl.BlockSpec((d, tf), lambda i, f: (0, f)),
            pl.BlockSpec((d, tf), lambda i, f: (0, f)),
            pl.BlockSpec((tf, d), lambda i, f: (f, 0)),
        ],
        out_specs=pl.BlockSpec((tt, d), lambda i, f: (i, 0)),
        scratch_shapes=[pltpu.VMEM((tt, d), BF16), pltpu.VMEM((tt, d), F32)],
        compiler_params=_params("arbitrary", "arbitrary"),
        name="ffn_swiglu",
    )(x2, g, sc, sh, gf, w1, w3, w2)


def _route_kernel(x_ref, g_ref, sc_ref, sh_ref, rw_ref, up_ref, h_ref, rank_ref, gate_ref, cnt_ref):
    hf = _mod_norm(x_ref[...], g_ref[...], sc_ref[0], sh_ref[0])
    h_ref[...] = hf.astype(BF16)
    logits = _dot(rw_ref[...], hf, NT, HI)
    ne, tt = logits.shape
    e_idx = lax.broadcasted_iota(jnp.int32, (ne, 1), 0).astype(F32)
    v1 = jnp.max(logits, axis=0, keepdims=True)
    i1 = jnp.min(jnp.where(logits == v1, e_idx, float(ne)), axis=0, keepdims=True)
    m1 = e_idx == i1
    rest = jnp.where(m1, -3e38, logits)
    v2 = jnp.max(rest, axis=0, keepdims=True)
    i2 = jnp.min(jnp.where(rest == v2, e_idx, float(ne)), axis=0, keepdims=True)
    m2 = e_idx == i2
    e2 = jnp.exp(v2 - v1)
    g1 = 1.0 / (1.0 + e2)
    g2 = e2 / (1.0 + e2)
    routed = m1 | m2
    rf = jnp.where(routed, 1.0, 0.0)
    rank = _dot(rf.astype(BF16), up_ref[...])
    rank = jnp.where(routed, rank, -1.0)
    gate = jnp.where(m1, g1, 0.0) + jnp.where(m2, g2, 0.0)
    for e in range(ne):
        rank_ref[0, e] = rank[e:e + 1, :]
        gate_ref[0, e] = gate[e:e + 1, :]
    cnt = jnp.sum(rf, axis=1, keepdims=True)
    cnt_ref[0] = jnp.broadcast_to(cnt, (ne, LANES))


def _route(x2, g, sc, sh, rw_t, seq, tt):
    n, d = x2.shape
    ne = rw_t.shape[0]
    tpb = seq // tt
    nt = n // tt
    upper = jnp.asarray(np.triu(np.ones((tt, tt), np.float32), 1), BF16)
    per_b = pl.BlockSpec((1, 1, d), lambda i: (i // tpb, 0, 0))
    return pl.pallas_call(
        _route_kernel,
        out_shape=[jax.ShapeDtypeStruct((n, d), BF16),
                   jax.ShapeDtypeStruct((nt, ne, 1, tt), F32),
                   jax.ShapeDtypeStruct((nt, ne, 1, tt), F32),
                   jax.ShapeDtypeStruct((nt, ne, LANES), F32)],
        grid=(nt,),
        in_specs=[
            pl.BlockSpec((tt, d), lambda i: (i, 0)),
            pl.BlockSpec((1, d), lambda i: (0, 0)),
            per_b, per_b,
            pl.BlockSpec((ne, d), lambda i: (0, 0)),
            pl.BlockSpec((tt, tt), lambda i: (0, 0)),
        ],
        out_specs=[pl.BlockSpec((tt, d), lambda i: (i, 0)),
                   pl.BlockSpec((1, ne, 1, tt), lambda i: (i, 0, 0, 0)),
                   pl.BlockSpec((1, ne, 1, tt), lambda i: (i, 0, 0, 0)),
                   pl.BlockSpec((1, ne, LANES), lambda i: (i, 0, 0))],
        compiler_params=_params("arbitrary"),
        name="moe_route",
    )(x2, g, sc, sh, rw_t, upper)


def _moe_kernel(cnt_ref, x_ref, gf_ref, h_ref, rank_ref, gate_ref, w1_ref, w3_ref, w2_ref,
                o_ref, acc_scr, xs_scr, y_scr, *, chunk):
    i, e, f = pl.program_id(0), pl.program_id(1), pl.program_id(2)
    ne, nf = pl.num_programs(1), pl.num_programs(2)

    @pl.when((e == 0) & (f == 0))
    def _():
        acc_scr[...] = jnp.zeros(acc_scr.shape, F32)

    count = cnt_ref[i * ne + e]
    rank = rank_ref[0, 0]
    gate = gate_ref[0, 0]
    n_small = (count + chunk - 1) // chunk
    n_big = (count + 2 * chunk - 1) // (2 * chunk)

    def one_hot(c, rows):
        r = c * rows + lax.broadcasted_iota(jnp.int32, (rows, 1), 0)
        return rank == r.astype(F32)

    def rows_of(c, rows):
        return pl.ds(pl.multiple_of(c * rows, rows), rows)

    @pl.when(f == 0)
    def _():
        h = h_ref[...]

        def gather(c, carry):
            p = jnp.where(one_hot(c, chunk), 1.0, 0.0).astype(BF16)
            xs_scr[rows_of(c, chunk), :] = _dot(p, h).astype(BF16)
            return carry

        lax.fori_loop(0, n_small, gather, 0)

        def clear(c, carry):
            y_scr[rows_of(c, 2 * chunk), :] = jnp.zeros((2 * chunk, y_scr.shape[1]), F32)
            return carry

        lax.fori_loop(0, n_big, clear, 0)

    def expert(c, carry):
        xs = xs_scr[rows_of(c, chunk), :]
        a = _dot(xs, w1_ref[0, 0])
        b = _dot(xs, w3_ref[0, 0])
        y_scr[rows_of(c, chunk), :] += _dot((a * jax.nn.sigmoid(a) * b).astype(BF16), w2_ref[0])
        return carry

    lax.fori_loop(0, n_small, expert, 0)

    @pl.when(f == nf - 1)
    def _():
        def scatter(c, carry):
            hit = one_hot(c, 2 * chunk)
            p = jnp.where(hit, 1.0, 0.0).astype(BF16)
            gcol = jnp.sum(jnp.where(hit, gate, 0.0), axis=-1, keepdims=True)
            acc_scr[...] += _dot(p, (y_scr[rows_of(c, 2 * chunk), :] * gcol).astype(BF16), TN)
            return carry

        lax.fori_loop(0, n_big, scatter, 0)

    @pl.when((e == ne - 1) & (f == nf - 1))
    def _():
        o_ref[...] = x_ref[...] + gf_ref[0] * acc_scr[...]


def _moe(counts, x2, gf, h2, rank, gate, w1, w3, w2, seq, tt, tf, chunk):
    n, d = x2.shape
    ne, _, dff = w1.shape
    tpb = seq // tt
    halves = lambda w: w.reshape(ne, d, dff // tf, tf).transpose(0, 2, 1, 3)
    w1, w3 = halves(w1), halves(w3)
    grid_spec = pltpu.PrefetchScalarGridSpec(
        num_scalar_prefetch=1,
        grid=(n // tt, ne, dff // tf),
        in_specs=[
            pl.BlockSpec((tt, d), lambda i, e, f, c: (i, 0)),
            pl.BlockSpec((1, 1, d), lambda i, e, f, c: (i // tpb, 0, 0)),
            pl.BlockSpec((tt, d), lambda i, e, f, c: (i, 0)),
            pl.BlockSpec((1, 1, 1, tt), lambda i, e, f, c: (i, e, 0, 0)),
            pl.BlockSpec((1, 1, 1, tt), lambda i, e, f, c: (i, e, 0, 0)),
            pl.BlockSpec((1, 1, d, tf), lambda i, e, f, c: (e, f, 0, 0)),
            pl.BlockSpec((1, 1, d, tf), lambda i, e, f, c: (e, f, 0, 0)),
            pl.BlockSpec((1, tf, d), lambda i, e, f, c: (e, f, 0)),
        ],
        out_specs=pl.BlockSpec((tt, d), lambda i, e, f, c: (i, 0)),
        scratch_shapes=[pltpu.VMEM((tt, d), F32), pltpu.VMEM((tt, d), BF16), pltpu.VMEM((tt, d), F32)],
    )
    return pl.pallas_call(
        functools.partial(_moe_kernel, chunk=chunk),
        out_shape=jax.ShapeDtypeStruct((n, d), F32),
        grid_spec=grid_spec,
        compiler_params=_params("arbitrary", "arbitrary", "arbitrary"),
        name="moe_experts",
    )(counts, x2, gf, h2, rank, gate, w1, w3, w2)


def _overlap_matrix(ncp, nc, n_sel):
    c0 = np.arange(ncp) * CMP_STRIDE
    c1 = c0 + CMP_LEN
    s0 = np.arange(n_sel) * SEL_LEN
    s1 = s0 + SEL_LEN
    ov = ((c0[:, None] < s1[None, :]) & (c1[:, None] > s0[None, :])).astype(np.float32)
    ov[nc:] = 0.0
    return jnp.asarray(ov.T)


def _gate_expand():
    pg = np.zeros((3, W_MIX, W_MIX), np.float32)
    for br in range(3):
        for h in range(N_HEADS):
            pg[br, GATE_LANE + 3 * h + br, h * HEAD_DIM:(h + 1) * HEAD_DIM] = 1.0
    return jnp.asarray(pg, BF16)


def _mixer_layer(x2, b, s, mod, norm_g, rope, w_in, qk_gain, pe_k, pe_v, ck1, ck2, cv1, cv2,
                 fox_b, w_branch, w_out):
    n, d = x2.shape
    sh_a, sc_a, g_a = mod[0], mod[1], mod[2]
    w_slab, gain, w_merge = _pack_w_in(w_in, qk_gain)
    bd = jnp.asarray(np.kron(np.eye(N_HEADS), np.full((HEAD_DIM, HEAD_DIM), 1.0 / HEAD_DIM)), BF16)
    outs = _proj(x2, norm_g, sc_a, sh_a, w_slab, gain, bd, rope, s, min(512, s))
    sl = [a.reshape(b, s, W_MIX) for a in outs[:N_SLABS]]
    dil_in = {1: (sl[S_BQ], sl[S_BK], sl[S_BV])}
    for di, dil in enumerate(_DILATIONS):
        dil_in[dil] = tuple(outs[N_SLABS + si * len(_DILATIONS) + di].reshape(b, s // dil, dil * W_MIX)
                            for si in range(len(_DIL_SLABS)))
    misc = sl[S_MISC]

    nch = s // CMP_STRIDE
    nc = nch - CMP_LEN // CMP_STRIDE + 1
    kc_raw = sl[S_KC][..., :HEAD_DIM]
    vc_raw = misc[..., VC_LANE:VC_LANE + HEAD_DIM]
    chunks = jnp.stack([kc_raw, vc_raw]).reshape(2, b, nch, CMP_STRIDE * HEAD_DIM)
    chunks_next = jnp.concatenate([chunks[:, :, 1:], jnp.zeros_like(chunks[:, :, :1])], axis=2)
    pe = jnp.stack([pe_k, pe_v]).reshape(2, 1, CMP_LEN * HEAD_DIM)
    kvc = jnp.tile(_compress(chunks, chunks_next, pe, jnp.stack([ck1, cv1]), jnp.stack([ck2, cv2])),
                   (1, 1, 1, N_HEADS))
    overlap = _overlap_matrix(nch, nc, s // SEL_LEN)
    o_cmp, selmask = _cmp_topk(sl[S_QNR], kvc[0], kvc[1].transpose(0, 2, 1).astype(BF16), overlap,
                               min(256, s), nc)
    tq, tk = min(256, s), min(512, s)
    o_sel = _causal_attn(sl[S_QR], sl[S_KSL], sl[S_VSL], tq, tk, sel=selmask)
    o_win = _window_attn(sl[S_QR], sl[S_KW], sl[S_VW], dil=1, window=NSA_WINDOW, tq=256, out_dtype=BF16)

    others = []
    for (wdw, dil) in DIL_CONFIGS[:0:-1]:
        others.append(_window_attn(*dil_in[dil], dil=dil, window=wdw // dil + 1, tq=256,
                                   out_dtype=F32, emit_lse=True))
    wdw, dil = DIL_CONFIGS[0]
    o_b = _window_attn(*dil_in[dil], dil=dil, window=wdw // dil + 1, tq=256, out_dtype=BF16, others=others)

    o_c = _stick_breaking(sl[S_CQ], sl[S_CK], sl[S_CV], min(256, s))

    bias_vec = jnp.zeros((1, W_MIX), F32).at[0, FOX_LANE:FOX_LANE + N_HEADS].set(fox_b)
    fcum = _fox_cumsum(misc, bias_vec, min(512, s))
    qk_bound = 1.02 * HEAD_DIM ** 0.5 * jnp.max(jnp.abs(qk_gain[6])) * jnp.max(jnp.abs(qk_gain[7])) + 0.05
    o_d = _causal_attn(sl[S_DQ], sl[S_DK], sl[S_DV], tq, tk, fcum=fcum[..., FOX_LANE:FOX_LANE + N_HEADS],
                       qk_bound=qk_bound)

    wb16 = w_branch.astype(BF16)
    flat = lambda a: a.reshape(n, W_MIX)
    return _merge(x2, norm_g, sc_a, sh_a, g_a, w_merge, flat(misc), _gate_expand(),
                  flat(o_cmp), flat(o_sel), flat(o_win), flat(o_b), flat(o_c), flat(o_d),
                  wb16[0], wb16[1], wb16[2], wb16[3], w_out.astype(BF16), s, min(512, s))


def kernel(x, c, positions, w_ada, b_ada, norm_mix, norm_ffn, w_in, qk_gain, nsa_pe_k, nsa_pe_v,
           nsa_ck_w1, nsa_ck_w2, nsa_cv_w1, nsa_cv_w2, fox_bias, w_branch, w_out,
           ffn_w1, ffn_w3, ffn_w2, router_w, moe_w1, moe_w3, moe_w2):
    b, s, d = x.shape
    depth = w_ada.shape[0]
    rope = _rope_tables(positions)
    mods = _ada(c, w_ada, b_ada).reshape(depth, b, 6, 1, d).transpose(0, 2, 1, 3, 4)
    x2 = x.reshape(b * s, d)
    for l in range(depth):
        mod = mods[l]
        x2 = _mixer_layer(x2, b, s, mod[0:3], norm_mix[l].reshape(1, d), rope, w_in[l], qk_gain[l],
                          nsa_pe_k[l], nsa_pe_v[l], nsa_ck_w1[l], nsa_ck_w2[l], nsa_cv_w1[l],
                          nsa_cv_w2[l], fox_bias[l], w_branch[l], w_out[l])
        sh_f, sc_f, g_f = mod[3], mod[4], mod[5]
        gn = norm_ffn[l].reshape(1, d)
        e = l // 2
        if l % 2 == 0:
            dff = ffn_w1.shape[2]
            x2 = _ffn(x2, gn, sc_f, sh_f, g_f, ffn_w1[e].astype(BF16), ffn_w3[e].astype(BF16),
                      ffn_w2[e].astype(BF16), s, min(1024, s), dff // 2)
        else:
            tt = min(1024, s)
            dff = moe_w1.shape[3]
            h2, rank, gate, cnt = _route(x2, gn, sc_f, sh_f, router_w[e].T, s, tt)
            counts = cnt[:, :, 0].astype(jnp.int32).reshape(-1)
            x2 = _moe(counts, x2, g_f, h2, rank, gate, moe_w1[e].astype(BF16), moe_w3[e].astype(BF16),
                      moe_w2[e].astype(BF16), s, tt, dff // 2, 128)
    return x2.reshape(b, s, d)
```

```python
import functools

import numpy as np
import jax
import jax.numpy as jnp
from jax import lax
from jax.experimental import pallas as pl
from jax.experimental.pallas import tpu as pltpu

F32 = jnp.float32
BF16 = jnp.bfloat16
HI = lax.Precision.HIGHEST

LANES = 128
VMEM_LIMIT = 52 * 1024 * 1024

HEAD_DIM = 64
HEAD_SHIFT = 6
N_HEADS = 4
W_MIX = N_HEADS * HEAD_DIM
N_MIXERS = 4
ROPE_THETA = 500000.0
ROPE_DIMS = HEAD_DIM // 4
ROPE_HALF = ROPE_DIMS // 2
EPS = 1e-6
LOG2E = 1.4426950408889634
NEG = -1e30
BIG = 1e30
CMP_LEN = 32
CMP_STRIDE = 16
SEL_LEN = 64
SEL_SHIFT = 6
TOPN = 16
SEL_NEG = 2.0 ** 30
NSA_WINDOW = 512
DIL_CONFIGS = ((128, 1), (512, 4), (2048, 16))
SB_STOP = -110.0
FOX_STOP = 108.0
GATE_LANE = 0
FOX_LANE = 3 * N_HEADS
VC_LANE = HEAD_DIM

NN = (((1,), (0,)), ((), ()))
NT = (((1,), (1,)), ((), ()))
TN = (((0,), (0,)), ((), ()))


def _dot(a, b, dims=NN, precision=None):
    return lax.dot_general(a, b, dims, precision=precision, preferred_element_type=F32)


def _dot_split(a, b_bf16, dims=NN):
    hi = a.astype(BF16)
    lo = (a - hi.astype(F32)).astype(BF16)
    return _dot(hi, b_bf16, dims) + _dot(lo, b_bf16, dims)


def _params(*sem):
    return pltpu.CompilerParams(dimension_semantics=sem, vmem_limit_bytes=VMEM_LIMIT)


def _mod_norm(x, g, sc, sh):
    ms = jnp.mean(x * x, axis=-1, keepdims=True)
    return (x * lax.rsqrt(ms + EPS) * g) * (1.0 + sc) + sh


def _head_masks():
    lane = lax.broadcasted_iota(jnp.int32, (1, W_MIX), 1)
    return [(lane >> HEAD_SHIFT) == h for h in range(N_HEADS)]


def _stack_masked(q, qm_scr, tq):
    for h, hm in enumerate(_head_masks()):
        qm_scr[h * tq:(h + 1) * tq, :] = jnp.where(hm, q, jnp.zeros_like(q))


def _pick_heads(stacked, tq, scale=None):
    out = None
    for h, hm in enumerate(_head_masks()):
        blk = stacked[h * tq:(h + 1) * tq, :]
        if scale is not None:
            blk = blk * scale[h]
        out = jnp.where(hm, blk, 0.0 if out is None else out)
    return out


def _ada_kernel(c_ref, w_ref, b_ref, o_ref):
    c = c_ref[...]
    ca = c * jax.nn.sigmoid(c)
    o_ref[0] = _dot(ca, w_ref[0], NN, HI) + b_ref[0]


def _ada(c, w_ada, b_ada):
    depth, d, n6 = w_ada.shape
    b = c.shape[0]
    tn = n6 // 4
    return pl.pallas_call(
        _ada_kernel,
        out_shape=jax.ShapeDtypeStruct((depth, b, n6), F32),
        grid=(depth, n6 // tn),
        in_specs=[
            pl.BlockSpec((b, d), lambda l, j: (0, 0)),
            pl.BlockSpec((1, d, tn), lambda l, j: (l, 0, j)),
            pl.BlockSpec((1, 1, tn), lambda l, j: (l, 0, j)),
        ],
        out_specs=pl.BlockSpec((1, b, tn), lambda l, j: (l, 0, j)),
        compiler_params=_params("arbitrary", "arbitrary"),
        name="ada_mod",
    )(c, w_ada, b_ada.reshape(depth, 1, n6))


_SLABS = (
    (True, False, F32), (True, True, BF16), (True, False, F32),
    (True, True, BF16), (True, True, BF16), (False, False, BF16), (False, False, BF16),
    (True, True, BF16), (True, True, BF16), (False, False, BF16),
    (False, False, BF16), (False, False, BF16), (False, False, BF16),
    (True, False, BF16), (True, False, BF16), (False, False, BF16),
    (False, False, F32),
)
N_SLABS = len(_SLABS)
(S_QNR, S_QR, S_KC, S_KSL, S_KW, S_VSL, S_VW, S_BQ, S_BK, S_BV, S_CQ, S_CK, S_CV,
 S_DQ, S_DK, S_DV, S_MISC) = range(N_SLABS)
_DIL_SLABS = (S_BQ, S_BK, S_BV)
_DILATIONS = tuple(d for (_, d) in DIL_CONFIGS if d > 1)


def _proj_kernel(x_ref, g_ref, sc_ref, sh_ref, w_ref, gain_ref, bd_ref, rope_ref, *out_refs):
    h = _mod_norm(x_ref[...], g_ref[...], sc_ref[0], sh_ref[0]).astype(BF16)
    bd = bd_ref[...]
    for s, (norm, rope, _) in enumerate(_SLABS):
        y = _dot(h, w_ref[:, s * W_MIX:(s + 1) * W_MIX])
        if norm:
            ms = _dot_split(y * y, bd)
            y = y * lax.rsqrt(ms + EPS)
        y = y * gain_ref[s]
        if rope:
            y = (y * rope_ref[0, 0] + pltpu.roll(y, W_MIX - ROPE_HALF, 1) * rope_ref[0, 1]
                 + pltpu.roll(y, ROPE_HALF, 1) * rope_ref[0, 2])
        out_refs[s][...] = y.astype(out_refs[s].dtype)
        if s in _DIL_SLABS:
            stage_scr = out_refs[-1]
            tt = y.shape[0]
            for c in range(W_MIX // LANES):
                stage_scr[c] = y[:, c * LANES:(c + 1) * LANES]
            for di, dil in enumerate(_DILATIONS):
                o_ref = out_refs[N_SLABS + _DIL_SLABS.index(s) * len(_DILATIONS) + di]
                for r in range(dil):
                    for c in range(W_MIX // LANES):
                        lo = r * W_MIX + c * LANES
                        o_ref[:, lo:lo + LANES] = (
                            stage_scr[c, pl.ds(r, tt // dil, stride=dil), :].astype(o_ref.dtype))


def _proj(x2, g, sc, sh, w_slab, gain, bd, rope, seq, tt):
    n, d = x2.shape
    tpb = seq // tt
    out_shape = [jax.ShapeDtypeStruct((n, W_MIX), dt) for (_, _, dt) in _SLABS]
    out_specs = [pl.BlockSpec((tt, W_MIX), lambda i: (i, 0)) for _ in _SLABS]
    for _ in _DIL_SLABS:
        for dil in _DILATIONS:
            out_shape.append(jax.ShapeDtypeStruct((n // dil, dil * W_MIX), BF16))
            out_specs.append(pl.BlockSpec((tt // dil, dil * W_MIX), lambda i: (i, 0)))
    return pl.pallas_call(
        _proj_kernel,
        out_shape=out_shape,
        grid=(n // tt,),
        in_specs=[
            pl.BlockSpec((tt, d), lambda i: (i, 0)),
            pl.BlockSpec((1, d), lambda i: (0, 0)),
            pl.BlockSpec((1, 1, d), lambda i: (i // tpb, 0, 0)),
            pl.BlockSpec((1, 1, d), lambda i: (i // tpb, 0, 0)),
            pl.BlockSpec((d, N_SLABS * W_MIX), lambda i: (0, 0)),
            pl.BlockSpec((N_SLABS, 1, W_MIX), lambda i: (0, 0, 0)),
            pl.BlockSpec((W_MIX, W_MIX), lambda i: (0, 0)),
            pl.BlockSpec((1, 3, tt, W_MIX), lambda i: (i // tpb, 0, i % tpb, 0)),
        ],
        out_specs=out_specs,
        scratch_shapes=[pltpu.VMEM((W_MIX // LANES, tt, LANES), F32)],
        compiler_params=_params("arbitrary"),
        name="in_proj",
    )(x2, g, sc, sh, w_slab, gain, bd, rope)


def _pack_w_in(w_in, qk_gain):
    d = w_in.shape[0]
    o = 0
    a_q = w_in[:, o:o + W_MIX]; o += W_MIX
    a_kv = w_in[:, o:o + 6 * HEAD_DIM]; o += 6 * HEAD_DIM
    a_g = w_in[:, o:o + 3 * N_HEADS]; o += 3 * N_HEADS
    b_qkv = w_in[:, o:o + 3 * W_MIX]; o += 3 * W_MIX
    c_qkv = w_in[:, o:o + 3 * W_MIX]; o += 3 * W_MIX
    d_qkv = w_in[:, o:o + 3 * W_MIX]; o += 3 * W_MIX
    d_f = w_in[:, o:o + N_HEADS]; o += N_HEADS
    w_merge = w_in[:, o:]
    kc, vc, ksl, vsl, kw, vw = (a_kv[:, i * HEAD_DIM:(i + 1) * HEAD_DIM] for i in range(6))
    zeros = lambda w: jnp.zeros((d, w), w_in.dtype)
    rep = lambda w: jnp.tile(w, (1, N_HEADS))
    misc = jnp.concatenate([a_g, d_f, zeros(VC_LANE - FOX_LANE - N_HEADS), vc, zeros(W_MIX - 2 * HEAD_DIM)], axis=1)
    slabs = [
        a_q, a_q, jnp.concatenate([kc, zeros(W_MIX - HEAD_DIM)], axis=1),
        rep(ksl), rep(kw), rep(vsl), rep(vw),
        b_qkv[:, :W_MIX], b_qkv[:, W_MIX:2 * W_MIX], b_qkv[:, 2 * W_MIX:],
        c_qkv[:, :W_MIX], c_qkv[:, W_MIX:2 * W_MIX], c_qkv[:, 2 * W_MIX:],
        d_qkv[:, :W_MIX], d_qkv[:, W_MIX:2 * W_MIX], d_qkv[:, 2 * W_MIX:],
        misc,
    ]
    w_slab = jnp.concatenate(slabs, axis=1).astype(BF16)
    scale = HEAD_DIM ** -0.5
    t4 = lambda gvec: jnp.tile(gvec, N_HEADS)
    one = jnp.ones((W_MIX,), F32)
    scale2 = scale * LOG2E
    gains = [
        t4(qk_gain[0]) * scale, t4(qk_gain[0]) * scale2,
        jnp.concatenate([qk_gain[1], jnp.ones((W_MIX - HEAD_DIM,), F32)]),
        t4(qk_gain[2]), t4(qk_gain[3]), one, one,
        t4(qk_gain[4]) * scale2, t4(qk_gain[5]), one,
        one * scale, one, one,
        t4(qk_gain[6]) * scale2, t4(qk_gain[7]), one,
        one,
    ]
    gain = jnp.stack(gains).reshape(N_SLABS, 1, W_MIX).astype(F32)
    return w_slab, gain, w_merge.astype(BF16)


def _rope_tables(positions):
    inv = ROPE_THETA ** (-jnp.arange(0, ROPE_DIMS, 2, dtype=F32) / ROPE_DIMS)
    ang = positions.astype(F32)[..., None] * inv
    cos, sin = jnp.cos(ang), jnp.sin(ang)
    b, s, _ = cos.shape
    pad1 = jnp.ones((b, s, HEAD_DIM - ROPE_DIMS), F32)
    pad0 = jnp.zeros((b, s, HEAD_DIM - ROPE_DIMS), F32)
    z8 = jnp.zeros_like(sin)
    c64 = jnp.concatenate([cos, cos, pad1], axis=-1)
    s1_64 = jnp.concatenate([-sin, z8, pad0], axis=-1)
    s2_64 = jnp.concatenate([z8, sin, pad0], axis=-1)
    heads = lambda t: jnp.tile(t, (1, 1, N_HEADS))
    return jnp.stack([heads(c64), heads(s1_64), heads(s2_64)], axis=1)


def _compress_kernel(a_ref, b_ref, pe_ref, w1_ref, w2_ref, o_ref):
    half = w1_ref.shape[1] // 2
    w1 = w1_ref[0]
    hid = (_dot(a_ref[0, 0], w1[:half], NN, HI) + _dot(b_ref[0, 0], w1[half:], NN, HI)
           + _dot(pe_ref[0], w1, NN, HI))
    o_ref[0, 0] = _dot(jax.nn.gelu(hid), w2_ref[0], NN, HI)


def _compress(ch, chn, pe, w1, w2):
    _, b, ncp, cw = ch.shape
    hid = w1.shape[2]
    return pl.pallas_call(
        _compress_kernel,
        out_shape=jax.ShapeDtypeStruct((2, b, ncp, HEAD_DIM), F32),
        grid=(2, b),
        in_specs=[
            pl.BlockSpec((1, 1, ncp, cw), lambda k, i: (k, i, 0, 0)),
            pl.BlockSpec((1, 1, ncp, cw), lambda k, i: (k, i, 0, 0)),
            pl.BlockSpec((1, 1, 2 * cw), lambda k, i: (k, 0, 0)),
            pl.BlockSpec((1, 2 * cw, hid), lambda k, i: (k, 0, 0)),
            pl.BlockSpec((1, hid, HEAD_DIM), lambda k, i: (k, 0, 0)),
        ],
        out_specs=pl.BlockSpec((1, 1, ncp, HEAD_DIM), lambda k, i: (k, i, 0, 0)),
        compiler_params=_params("arbitrary", "arbitrary"),
        name="nsa_compress",
    )(ch, chn, pe, w1, w2)


def _cmp_kernel(q_ref, kc_ref, vct_ref, ovt_ref, o_ref, sel_ref, qh_scr, ql_scr, acc_scr, *, tq, nc, n_sel):
    i = pl.program_id(1)
    ncp = kc_ref.shape[1]
    q = q_ref[0]
    q_hi = q.astype(BF16)
    _stack_masked(q_hi, qh_scr, tq)
    _stack_masked((q - q_hi.astype(F32)).astype(BF16), ql_scr, tq)
    kc = kc_ref[0]
    k_hi = kc.astype(BF16)
    k_lo = (kc - k_hi.astype(F32)).astype(BF16)
    ss = []
    for h in range(N_HEADS):
        rs = slice(h * tq, (h + 1) * tq)
        ss.append(_dot(k_hi, qh_scr[rs, :], NT) + _dot(k_lo, qh_scr[rs, :], NT)
                  + _dot(k_hi, ql_scr[rs, :], NT))
    t = i * tq + lax.broadcasted_iota(jnp.int32, (1, tq), 1)
    c = lax.broadcasted_iota(jnp.int32, (ncp, 1), 0)
    mask = (c * CMP_STRIDE + (CMP_LEN - 1) <= t) & (c < nc)
    psum = None
    ps = []
    for h in range(N_HEADS):
        sm = jnp.where(mask, ss[h], NEG)
        m = jnp.max(sm, axis=0, keepdims=True)
        e = jnp.where(mask, jnp.exp(sm - m), 0.0)
        l = jnp.sum(e, axis=0, keepdims=True)
        p = e / jnp.maximum(l, 1e-30)
        psum = p if psum is None else psum + p
        ps.append(p.astype(BF16))
    vct = vct_ref[0]
    for h in range(N_HEADS):
        rs = slice(h * HEAD_DIM, (h + 1) * HEAD_DIM)
        acc_scr[rs, :] = _dot(vct[rs, :], ps[h])
    o_ref[0] = acc_scr[...].T.astype(o_ref.dtype)
    imp = _dot(ovt_ref[...], psum, NN, HI)
    j = lax.broadcasted_iota(jnp.int32, (n_sel, 1), 0)
    cur = t >> SEL_SHIFT
    valid = j <= cur
    forced = (j == 0) | (j == cur) | (j == cur - 1)
    score = jnp.where(valid, jnp.where(forced, BIG, imp), NEG)
    sel = jnp.zeros((n_sel, tq), F32)
    jf = j.astype(F32)
    for _ in range(min(TOPN, n_sel)):
        mx = jnp.max(score, axis=0, keepdims=True)
        idx = jnp.min(jnp.where(score == mx, jf, float(n_sel)), axis=0, keepdims=True)
        pick = jf == idx
        sel = jnp.where(pick, 1.0, sel)
        score = jnp.where(pick, -3e38, score)
    sel_ref[0] = ((sel - 1.0) * SEL_NEG).T.astype(sel_ref.dtype)


def _cmp_topk(q_nr, kc_rep, vc_rep, overlap, tq, nc):
    b, seq, _ = q_nr.shape
    ncp = kc_rep.shape[1]
    n_sel = seq // SEL_LEN
    return pl.pallas_call(
        functools.partial(_cmp_kernel, tq=tq, nc=nc, n_sel=n_sel),
        out_shape=[jax.ShapeDtypeStruct((b, seq, W_MIX), BF16),
                   jax.ShapeDtypeStruct((b, seq, n_sel), BF16)],
        grid=(b, seq // tq),
        in_specs=[
            pl.BlockSpec((1, tq, W_MIX), lambda g, i: (g, i, 0)),
            pl.BlockSpec((1, ncp, W_MIX), lambda g, i: (g, 0, 0)),
            pl.BlockSpec((1, W_MIX, ncp), lambda g, i: (g, 0, 0)),
            pl.BlockSpec((n_sel, ncp), lambda g, i: (0, 0)),
        ],
        out_specs=[pl.BlockSpec((1, tq, W_MIX), lambda g, i: (g, i, 0)),
                   pl.BlockSpec((1, tq, n_sel), lambda g, i: (g, i, 0))],
        scratch_shapes=[pltpu.VMEM((N_HEADS * tq, W_MIX), BF16), pltpu.VMEM((N_HEADS * tq, W_MIX), BF16),
                        pltpu.VMEM((W_MIX, tq), F32)],
        compiler_params=_params("arbitrary", "arbitrary"),
        name="nsa_cmp_topk",
    )(q_nr, kc_rep, vc_rep, overlap)


def _causal_kernel(*refs, tq, tk, has_sel, has_bias):
    it = iter(refs)
    q_ref, k_ref, vt_ref = next(it), next(it), next(it)
    sel_ref = next(it) if has_sel else None
    fa_ref, fb_ref, fq_ref, thr_ref = (next(it) for _ in range(4)) if has_bias else (None,) * 4
    o_ref, qm_scr, acc_scr, sa_scr, sb_scr, m_scr, l_scr = (next(it) for _ in range(7))
    i = pl.program_id(1)
    t0 = i * tq
    lane = lax.broadcasted_iota(jnp.int32, (1, W_MIX), 1)
    if has_sel:
        n_sel = sel_ref.shape[2]
        qf = q_ref[0].astype(F32)
        selb = sel_ref[0].astype(F32)
        selb = jnp.concatenate([selb, jnp.zeros((tq, W_MIX - n_sel), F32)], axis=1)
        selb = pltpu.roll(selb, HEAD_DIM, 1)
        for h in range(N_HEADS):
            rot = qf if h == 0 else pltpu.roll(qf, W_MIX - h * HEAD_DIM, 1)
            qm_scr[h * tq:(h + 1) * tq, :] = jnp.where(lane < HEAD_DIM, rot, selb).astype(BF16)
    else:
        _stack_masked(q_ref[0], qm_scr, tq)
    acc_scr[...] = jnp.zeros(acc_scr.shape, F32)
    t_pos = t0 + lax.broadcasted_iota(jnp.int32, (1, tq), 1)

    def scores(kt, s_buf):
        k_t = k_ref[0, kt]
        if has_sel:
            blk = (kt * tk + lax.broadcasted_iota(jnp.int32, (tk, 1), 0)) >> SEL_SHIFT
            one_hot = jnp.where(blk == lane - HEAD_DIM, 1.0, 0.0).astype(BF16)
            k_t = jnp.where(lane < HEAD_DIM, k_t, one_hot)
        for h in range(N_HEADS):
            s = _dot(k_t, qm_scr[h * tq:(h + 1) * tq, :], NT)
            if has_bias:
                s = s + _dot(fa_ref[0, kt], fb_ref[0, h])
            s_buf[h] = s

    def update(kt, s_buf, diag):
        vt_t = vt_ref[0, kt]
        mask = None
        if diag:
            s_pos = kt * tk + lax.broadcasted_iota(jnp.int32, (tk, 1), 0)
            mask = s_pos <= t_pos
        ps, alphas = [], []
        for h in range(N_HEADS):
            s = s_buf[h]
            if mask is not None:
                s = jnp.where(mask, s, NEG)
            m_old = m_scr[h]
            m_new = jnp.maximum(m_old, jnp.max(s, axis=0, keepdims=True))
            p = jnp.exp2(s - m_new)
            alpha = jnp.exp2(m_old - m_new)
            m_scr[h] = m_new
            l_scr[h] = alpha * l_scr[h] + jnp.sum(p, axis=0, keepdims=True)
            ps.append(p.astype(BF16))
            alphas.append(alpha)
        for h in range(N_HEADS):
            rs = slice(h * HEAD_DIM, (h + 1) * HEAD_DIM)
            acc_scr[rs, :] = alphas[h] * acc_scr[rs, :] + _dot(vt_t[rs, :], ps[h])

    def live(kt_next):
        lane = lax.broadcasted_iota(jnp.int32, (1, thr_ref.shape[3]), 1)
        hit = None
        for h in range(N_HEADS):
            top = jnp.max(fq_ref[0, h] - m_scr[h], axis=-1, keepdims=True)
            need = top >= thr_ref[0, h]
            hit = need if hit is None else (hit | need)
        return jnp.max(jnp.where(hit & (lane == kt_next), 1.0, 0.0)) > 0.5

    n_last = t0 // tk
    m_scr[...] = jnp.full(m_scr.shape, NEG, F32)
    l_scr[...] = jnp.zeros(l_scr.shape, F32)
    scores(n_last, sa_scr)
    scores(jnp.maximum(n_last - 1, 0), sb_scr)
    update(n_last, sa_scr, True)

    def pair(j):
        kt = n_last - 1 - 2 * j
        scores(kt - 1, sa_scr)
        update(kt, sb_scr, False)
        scores(jnp.maximum(kt - 2, 0), sb_scr)
        update(kt - 1, sa_scr, False)
        return kt - 2

    n_pairs = n_last // 2
    if has_bias:
        def cond(state):
            j, go = state
            return (j < n_pairs) & go

        def body(state):
            j, _ = state
            return j + 1, live(pair(j))

        _, go = lax.while_loop(cond, body, (jnp.int32(0), live(n_last - 1)))
    else:
        lax.fori_loop(0, n_pairs, lambda j, c: (pair(j), c)[1], 0)
        go = True

    @pl.when((n_last % 2 == 1) & go)
    def _():
        update(0, sb_scr, False)

    ls = [l_scr[h] for h in range(N_HEADS)]
    for h in range(N_HEADS):
        rs = slice(h * HEAD_DIM, (h + 1) * HEAD_DIM)
        acc_scr[rs, :] = acc_scr[rs, :] / ls[h]
    o_ref[0] = acc_scr[...].T.astype(o_ref.dtype)


def _split3(x):
    def cut(v):
        bits = lax.bitcast_convert_type(v, jnp.uint32) & jnp.uint32(0xFFFF0000)
        return lax.bitcast_convert_type(bits, F32)

    hi = cut(x)
    r1 = x - hi
    mid = cut(r1)
    lo = r1 - mid
    return hi.astype(BF16), mid.astype(BF16), lo.astype(BF16)


def _causal_attn(q, k, v, tq, tk, sel=None, fcum=None, qk_bound=None):
    b, s, _ = q.shape
    nk = s // tk
    rows = N_HEADS * tq
    n_s = N_HEADS
    vt = v.reshape(b, nk, tk, W_MIX).transpose(0, 1, 3, 2)
    args = [q, k.reshape(b, nk, tk, W_MIX), vt]
    in_specs = [
        pl.BlockSpec((1, tq, W_MIX), lambda a, i: (a, i, 0)),
        pl.BlockSpec((1, nk, tk, W_MIX), lambda a, i: (a, 0, 0, 0)),
        pl.BlockSpec((1, nk, W_MIX, tk), lambda a, i: (a, 0, 0, 0)),
    ]
    if sel is not None:
        args.append(sel)
        in_specs.append(pl.BlockSpec((1, tq, sel.shape[2]), lambda a, i: (a, i, 0)))
    if fcum is not None:
        fcum = fcum * LOG2E
        qk_bound = qk_bound * LOG2E
        nf = 2 * 3 + 2
        ones = jnp.ones(fcum.shape, BF16)
        zero = jnp.zeros(fcum.shape, BF16)
        parts = _split3(fcum)
        key_f = jnp.stack([ones, ones, ones] + [-p for p in parts] + [zero, zero], axis=-1)
        qry_f = jnp.stack(list(parts) + [ones, ones, ones, zero, zero], axis=-1)
        fa = key_f.reshape(b, nk, tk, N_HEADS * nf)
        eye = jnp.eye(N_HEADS, dtype=BF16)
        fb = jnp.einsum('bshf,hg->bgshf', qry_f, eye).reshape(b, N_HEADS, s, N_HEADS * nf).transpose(0, 1, 3, 2)
        f_rows = fcum.transpose(0, 2, 1)
        f_end = f_rows[:, :, tk - 1::tk]
        thr = jnp.pad(f_end - (qk_bound + FOX_STOP * LOG2E), ((0, 0), (0, 0), (0, LANES - nk)),
                      constant_values=BIG).reshape(b, N_HEADS, 1, LANES)
        args += [fa, fb, f_rows.reshape(b, N_HEADS, 1, s), thr]
        in_specs += [pl.BlockSpec((1, nk, tk, N_HEADS * nf), lambda a, i: (a, 0, 0, 0)),
                     pl.BlockSpec((1, N_HEADS, N_HEADS * nf, tq), lambda a, i: (a, 0, 0, i)),
                     pl.BlockSpec((1, N_HEADS, 1, tq), lambda a, i: (a, 0, 0, i)),
                     pl.BlockSpec((1, N_HEADS, 1, LANES), lambda a, i: (a, 0, 0, 0))]
    return pl.pallas_call(
        functools.partial(_causal_kernel, tq=tq, tk=tk, has_sel=sel is not None, has_bias=fcum is not None),
        out_shape=jax.ShapeDtypeStruct((b, s, W_MIX), BF16),
        grid=(b, s // tq),
        in_specs=in_specs,
        out_specs=pl.BlockSpec((1, tq, W_MIX), lambda a, i: (a, i, 0)),
        scratch_shapes=[pltpu.VMEM((rows, W_MIX), BF16), pltpu.VMEM((W_MIX, tq), F32),
                        pltpu.VMEM((n_s, tk, tq), F32), pltpu.VMEM((n_s, tk, tq), F32),
                        pltpu.VMEM((N_HEADS, 1, tq), F32), pltpu.VMEM((N_HEADS, 1, tq), F32)],
        compiler_params=_params("arbitrary", "arbitrary"),
        name="causal_sel%d_bias%d" % (sel is not None, fcum is not None),
    )(*args)


def _window_kernel(*refs, tq, wk, pad, window, ls, emit_lse, other_dils):
    it = iter(refs)
    q_ref, k_ref, v_ref = next(it), next(it), next(it)
    n_other = len(other_dils)
    others_in = [(next(it), next(it)) for _ in range(n_other)]
    o_ref = next(it)
    lse_ref = next(it) if emit_lse else None
    qm_scr, p_scr = next(it), next(it)
    others = []
    if n_other:
        og_scr, lg_scr = next(it), next(it)
        for g, (dg, (og_ref, lg_ref)) in enumerate(zip(other_dils, others_in)):
            for r in range(dg):
                rows = pl.ds(r, tq // dg, stride=dg)
                for c in range(W_MIX // LANES):
                    lo = r * W_MIX + c * LANES
                    og_scr[g, c, rows, :] = og_ref[0, :, lo:lo + LANES]
                lg_scr[g, rows, :] = lg_ref[0, :, r * LANES:(r + 1) * LANES]
            others.append((og_scr.at[g], lg_scr.at[g]))
    t0 = pl.program_id(2) * tq
    start = pl.multiple_of(jnp.clip(t0 - pad, 0, ls - wk), LANES)
    _stack_masked(q_ref[0], qm_scr, tq)
    s_all = _dot(qm_scr[...], k_ref[0, pl.ds(start, wk), :], NT)
    t_pos = t0 + lax.broadcasted_iota(jnp.int32, (tq, 1), 0)
    s_pos = start + lax.broadcasted_iota(jnp.int32, (1, wk), 1)
    mask = (s_pos <= t_pos) & (t_pos - s_pos < window)
    ms, ls_ = [], []
    for h in range(N_HEADS):
        rs = slice(h * tq, (h + 1) * tq)
        s = jnp.where(mask, s_all[rs, :], NEG)
        m = jnp.max(s, axis=-1, keepdims=True)
        p = jnp.exp2(s - m)
        ms.append(m)
        ls_.append(jnp.sum(p, axis=-1, keepdims=True))
        p_scr[rs, :] = p.astype(BF16)
    pv = _dot(p_scr[...], v_ref[0, pl.ds(start, wk), :])
    if n_other:
        scales, dens = [], []
        lses = [m + jnp.log2(l) for m, l in zip(ms, ls_)]
        out = None
        hms = _head_masks()
        for h in range(N_HEADS):
            other_lse = [lr[:, h:h + 1] for (_, lr) in others]
            top = lses[h]
            for ol in other_lse:
                top = jnp.maximum(top, ol)
            w_self = jnp.exp2(lses[h] - top)
            num = pv[h * tq:(h + 1) * tq, :] * (w_self / ls_[h])
            den = w_self
            for (orf, _), ol in zip(others, other_lse):
                w = jnp.exp2(ol - top)
                num = num + w * jnp.concatenate([orf[c] for c in range(W_MIX // LANES)], axis=1)
                den = den + w
            out = jnp.where(hms[h], num / den, 0.0 if out is None else out)
        o_ref[0] = out.astype(o_ref.dtype)
    else:
        o_ref[0] = _pick_heads(pv, tq, [1.0 / l for l in ls_]).astype(o_ref.dtype)
    if emit_lse:
        lane = lax.broadcasted_iota(jnp.int32, (1, LANES), 1)
        tile = jnp.zeros((tq, LANES), F32)
        for h in range(N_HEADS):
            tile = jnp.where(lane == h, ms[h] + jnp.log2(ls_[h]), tile)
        lse_ref[0] = tile


def _window_attn(q, k, v, *, dil, window, tq, out_dtype, emit_lse=False, others=()):
    b, ls, _ = q.shape
    tq = min(tq, ls)
    pad = -(-(window - 1) // LANES) * LANES
    wk = min(tq + pad, ls)
    rows = N_HEADS * tq
    args = [q, k, v]
    in_specs = [
        pl.BlockSpec((1, tq, W_MIX), lambda a, r, i: (a, i, r)),
        pl.BlockSpec((1, ls, W_MIX), lambda a, r, i: (a, 0, r)),
        pl.BlockSpec((1, ls, W_MIX), lambda a, r, i: (a, 0, r)),
    ]
    scratch = [pltpu.VMEM((rows, W_MIX), BF16), pltpu.VMEM((rows, wk), BF16)]
    for (o_g, lse_g, dg) in others:
        args += [o_g, lse_g]
        in_specs += [pl.BlockSpec((1, tq // dg, dg * W_MIX), lambda a, r, i: (a, i, 0)),
                     pl.BlockSpec((1, tq // dg, dg * LANES), lambda a, r, i: (a, i, 0))]
    if others:
        scratch += [pltpu.VMEM((len(others), W_MIX // LANES, tq, LANES), F32),
                    pltpu.VMEM((len(others), tq, LANES), F32)]
    out_shape = [jax.ShapeDtypeStruct((b, ls, dil * W_MIX), out_dtype)]
    out_specs = [pl.BlockSpec((1, tq, W_MIX), lambda a, r, i: (a, i, r))]
    if emit_lse:
        out_shape.append(jax.ShapeDtypeStruct((b, ls, dil * LANES), F32))
        out_specs.append(pl.BlockSpec((1, tq, LANES), lambda a, r, i: (a, i, r)))
    res = pl.pallas_call(
        functools.partial(_window_kernel, tq=tq, wk=wk, pad=pad, window=window, ls=ls,
                          emit_lse=emit_lse, other_dils=tuple(dg for (_, _, dg) in others)),
        out_shape=out_shape,
        grid=(b, dil, ls // tq),
        in_specs=in_specs,
        out_specs=out_specs,
        scratch_shapes=scratch,
        compiler_params=_params("arbitrary", "arbitrary", "arbitrary"),
        name="window_d%d_w%d" % (dil, window),
    )(*args)
    return (res[0], res[1], dil) if emit_lse else res[0]


def _sb_kernel(q_ref, k_ref, v_ref, tri_ref, o_ref, qm_scr, a_scr, carry_scr, acc_scr, *, tq):
    i = pl.program_id(1)
    t0 = i * tq
    rows = N_HEADS * tq
    _stack_masked(q_ref[0], qm_scr, tq)
    carry_scr[...] = jnp.zeros(carry_scr.shape, F32)
    acc_scr[...] = jnp.zeros(acc_scr.shape, F32)
    r = lax.broadcasted_iota(jnp.int32, (rows, 1), 0)
    t_pos = t0 + (r & (tq - 1))

    def tile(kt, diag):
        z = _dot(qm_scr[...], k_ref[0, kt], NT)
        lg = -(jnp.maximum(z, 0.0) + jnp.log(1.0 + jnp.exp(-jnp.abs(z))))
        if diag:
            s_pos = kt * tq + lax.broadcasted_iota(jnp.int32, (1, tq), 1)
            strict = s_pos < t_pos
            lg = jnp.where(strict, lg, 0.0)
        cum = _dot_split(lg, tri_ref[...]) + carry_scr[...]
        a = jnp.exp(z + cum)
        if diag:
            a = jnp.where(strict, a, 0.0)
        a_scr[...] = a.astype(BF16)
        acc_scr[...] += _dot(a_scr[...], v_ref[0, kt])
        carry_scr[...] += jnp.sum(lg, axis=-1, keepdims=True)

    tile(i, True)

    def cond(state):
        j, top = state
        return (j < i) & (top > SB_STOP)

    def body(state):
        j, _ = state
        tile(i - 1 - j, False)
        return j + 1, jnp.max(carry_scr[...])

    lax.while_loop(cond, body, (jnp.int32(0), jnp.max(carry_scr[...])))
    o_ref[0] = _pick_heads(acc_scr[...], tq).astype(o_ref.dtype)


def _stick_breaking(q, k, v, tq):
    b, s, _ = q.shape
    nk = s // tq
    rows = N_HEADS * tq
    tri = jnp.asarray(np.tril(np.ones((tq, tq), np.float32)), BF16)
    return pl.pallas_call(
        functools.partial(_sb_kernel, tq=tq),
        out_shape=jax.ShapeDtypeStruct((b, s, W_MIX), BF16),
        grid=(b, nk),
        in_specs=[
            pl.BlockSpec((1, tq, W_MIX), lambda a, i: (a, i, 0)),
            pl.BlockSpec((1, nk, tq, W_MIX), lambda a, i: (a, 0, 0, 0)),
            pl.BlockSpec((1, nk, tq, W_MIX), lambda a, i: (a, 0, 0, 0)),
            pl.BlockSpec((tq, tq), lambda a, i: (0, 0)),
        ],
        out_specs=pl.BlockSpec((1, tq, W_MIX), lambda a, i: (a, i, 0)),
        scratch_shapes=[pltpu.VMEM((rows, W_MIX), BF16), pltpu.VMEM((rows, tq), BF16),
                        pltpu.VMEM((rows, 1), F32), pltpu.VMEM((rows, W_MIX), F32)],
        compiler_params=_params("arbitrary", "arbitrary"),
        name="stick_breaking",
    )(q, k.reshape(b, nk, tq, W_MIX), v.reshape(b, nk, tq, W_MIX), tri)


def _foxcum_kernel(x_ref, b_ref, tri_ref, o_ref, carry_scr):
    @pl.when(pl.program_id(1) == 0)
    def _():
        carry_scr[...] = jnp.zeros(carry_scr.shape, F32)

    z = x_ref[0] + b_ref[...]
    logf = jnp.minimum(z, 0.0) - jnp.log(1.0 + jnp.exp(-jnp.abs(z)))
    cum = _dot(tri_ref[...], logf, NN, HI) + carry_scr[...]
    o_ref[0] = cum
    carry_scr[...] = cum[cum.shape[0] - 1:, :]


def _fox_cumsum(misc, bias_vec, tc):
    b, s, w = misc.shape
    tri = jnp.asarray(np.tril(np.ones((tc, tc), np.float32)))
    return pl.pallas_call(
        _foxcum_kernel,
        out_shape=jax.ShapeDtypeStruct((b, s, w), F32),
        grid=(b, s // tc),
        in_specs=[
            pl.BlockSpec((1, tc, w), lambda a, i: (a, i, 0)),
            pl.BlockSpec((1, w), lambda a, i: (0, 0)),
            pl.BlockSpec((tc, tc), lambda a, i: (0, 0)),
        ],
        out_specs=pl.BlockSpec((1, tc, w), lambda a, i: (a, i, 0)),
        scratch_shapes=[pltpu.VMEM((1, w), F32)],
        compiler_params=_params("arbitrary", "arbitrary"),
        name="fox_cumsum",
    )(misc, bias_vec, tri)


def _merge_kernel(x_ref, g_ref, sc_ref, sh_ref, ga_ref, wm_ref, misc_ref, pg_ref,
                  ocmp_ref, osel_ref, owin_ref, ob_ref, oc_ref, od_ref,
                  wa_ref, wb_ref, wc_ref, wd_ref, wo_ref, o_ref):
    x = x_ref[...]
    d = x.shape[1]
    h = _mod_norm(x, g_ref[...], sc_ref[0], sh_ref[0]).astype(BF16)
    gate = jax.nn.sigmoid(misc_ref[...])
    o_a = (_dot_split(gate, pg_ref[0]) * ocmp_ref[...].astype(F32)
           + _dot_split(gate, pg_ref[1]) * osel_ref[...].astype(F32)
           + _dot_split(gate, pg_ref[2]) * owin_ref[...].astype(F32)).astype(BF16)
    mixed = jnp.zeros(x.shape, F32)
    for m, (o_m, w_ref) in enumerate(((o_a, wa_ref), (ob_ref[...], wb_ref),
                                      (oc_ref[...], wc_ref), (od_ref[...], wd_ref))):
        y = _dot(o_m, w_ref[...])
        gl = _dot(h, wm_ref[:, m * d:(m + 1) * d])
        mixed = mixed + jax.nn.sigmoid(gl) * y
    o_ref[...] = x + ga_ref[0] * _dot(mixed.astype(BF16), wo_ref[...])


def _merge(x2, g, sc, sh, ga, w_merge, misc, pg, o_cmp, o_sel, o_win, o_b, o_c, o_d,
           wa, wb, wc, wd, wo, seq, tt):
    n, d = x2.shape
    tpb = seq // tt
    row = lambda w: pl.BlockSpec((tt, w), lambda i: (i, 0))
    full = lambda a: pl.BlockSpec(a.shape, lambda i: (0,) * a.ndim)
    per_b = pl.BlockSpec((1, 1, d), lambda i: (i // tpb, 0, 0))
    return pl.pallas_call(
        _merge_kernel,
        out_shape=jax.ShapeDtypeStruct((n, d), F32),
        grid=(n // tt,),
        in_specs=[row(d), full(g), per_b, per_b, per_b, full(w_merge), row(W_MIX), full(pg)]
        + [row(W_MIX)] * 6 + [full(wa), full(wb), full(wc), full(wd), full(wo)],
        out_specs=row(d),
        compiler_params=_params("arbitrary"),
        name="merge_out",
    )(x2, g, sc, sh, ga, w_merge, misc, pg, o_cmp, o_sel, o_win, o_b, o_c, o_d, wa, wb, wc, wd, wo)


def _ffn_kernel(x_ref, g_ref, sc_ref, sh_ref, gf_ref, w1_ref, w3_ref, w2_ref, o_ref, h_scr, acc_scr):
    f = pl.program_id(1)

    @pl.when(f == 0)
    def _():
        h_scr[...] = _mod_norm(x_ref[...], g_ref[...], sc_ref[0], sh_ref[0]).astype(BF16)
        acc_scr[...] = jnp.zeros(acc_scr.shape, F32)

    h = h_scr[...]
    a = _dot(h, w1_ref[...])
    b = _dot(h, w3_ref[...])
    acc_scr[...] += _dot((a * jax.nn.sigmoid(a) * b).astype(BF16), w2_ref[...])

    @pl.when(f == pl.num_programs(1) - 1)
    def _():
        o_ref[...] = x_ref[...] + gf_ref[0] * acc_scr[...]


def _ffn(x2, g, sc, sh, gf, w1, w3, w2, seq, tt, tf):
    n, d = x2.shape
    dff = w1.shape[1]
    tpb = seq // tt
    per_b = pl.BlockSpec((1, 1, d), lambda i, f: (i // tpb, 0, 0))
    return pl.pallas_call(
        _ffn_kernel,
        out_shape=jax.ShapeDtypeStruct((n, d), F32),
        grid=(n // tt, dff // tf),
        in_specs=[
            pl.BlockSpec((tt, d), lambda i, f: (i, 0)),
            pl.BlockSpec((1, d), lambda i, f: (0, 0)),
            per_b, per_b, per_b,
            pl.BlockSpec((d, tf), lambda i, f: (0, f)),
            pl.BlockSpec((d, tf), lambda i, f: (0, f)),
            pl.BlockSpec((tf, d), lambda i, f: (f, 0)),
        ],
        out_specs=pl.BlockSpec((tt, d), lambda i, f: (i, 0)),
        scratch_shapes=[pltpu.VMEM((tt, d), BF16), pltpu.VMEM((tt, d), F32)],
        compiler_params=_params("arbitrary", "arbitrary"),
        name="ffn_swiglu",
    )(x2, g, sc, sh, gf, w1, w3, w2)


def _route_kernel(x_ref, g_ref, sc_ref, sh_ref, rw_ref, up_ref, h_ref, rank_ref, gate_ref, cnt_ref):
    hf = _mod_norm(x_ref[...], g_ref[...], sc_ref[0], sh_ref[0])
    h_ref[...] = hf.astype(BF16)
    logits = _dot(rw_ref[...], hf, NT, HI)
    ne, tt = logits.shape
    e_idx = lax.broadcasted_iota(jnp.int32, (ne, 1), 0).astype(F32)
    v1 = jnp.max(logits, axis=0, keepdims=True)
    i1 = jnp.min(jnp.where(logits == v1, e_idx, float(ne)), axis=0, keepdims=True)
    m1 = e_idx == i1
    rest = jnp.where(m1, -3e38, logits)
    v2 = jnp.max(rest, axis=0, keepdims=True)
    i2 = jnp.min(jnp.where(rest == v2, e_idx, float(ne)), axis=0, keepdims=True)
    m2 = e_idx == i2
    e2 = jnp.exp(v2 - v1)
    g1 = 1.0 / (1.0 + e2)
    g2 = e2 / (1.0 + e2)
    routed = m1 | m2
    rf = jnp.where(routed, 1.0, 0.0)
    rank = _dot(rf.astype(BF16), up_ref[...])
    rank = jnp.where(routed, rank, -1.0)
    gate = jnp.where(m1, g1, 0.0) + jnp.where(m2, g2, 0.0)
    for e in range(ne):
        rank_ref[0, e] = rank[e:e + 1, :]
        gate_ref[0, e] = gate[e:e + 1, :]
    cnt = jnp.sum(rf, axis=1, keepdims=True)
    cnt_ref[0] = jnp.broadcast_to(cnt, (ne, LANES))


def _route(x2, g, sc, sh, rw_t, seq, tt):
    n, d = x2.shape
    ne = rw_t.shape[0]
    tpb = seq // tt
    nt = n // tt
    upper = jnp.asarray(np.triu(np.ones((tt, tt), np.float32), 1), BF16)
    per_b = pl.BlockSpec((1, 1, d), lambda i: (i // tpb, 0, 0))
    return pl.pallas_call(
        _route_kernel,
        out_shape=[jax.ShapeDtypeStruct((n, d), BF16),
                   jax.ShapeDtypeStruct((nt, ne, 1, tt), F32),
                   jax.ShapeDtypeStruct((nt, ne, 1, tt), F32),
                   jax.ShapeDtypeStruct((nt, ne, LANES), F32)],
        grid=(nt,),
        in_specs=[
            pl.BlockSpec((tt, d), lambda i: (i, 0)),
            pl.BlockSpec((1, d), lambda i: (0, 0)),
            per_b, per_b,
            pl.BlockSpec((ne, d), lambda i: (0, 0)),
            pl.BlockSpec((tt, tt), lambda i: (0, 0)),
        ],
        out_specs=[pl.BlockSpec((tt, d), lambda i: (i, 0)),
                   pl.BlockSpec((1, ne, 1, tt), lambda i: (i, 0, 0, 0)),
                   pl.BlockSpec((1, ne, 1, tt), lambda i: (i, 0, 0, 0)),
                   pl.BlockSpec((1, ne, LANES), lambda i: (i, 0, 0))],
        compiler_params=_params("arbitrary"),
        name="moe_route",
    )(x2, g, sc, sh, rw_t, upper)


def _moe_kernel(cnt_ref, x_ref, gf_ref, h_ref, rank_ref, gate_ref, w1_ref, w3_ref, w2_ref,
                o_ref, acc_scr, xs_scr, y_scr, *, chunk):
    i, e, f = pl.program_id(0), pl.program_id(1), pl.program_id(2)
    ne, nf = pl.num_programs(1), pl.num_programs(2)

    @pl.when((e == 0) & (f == 0))
    def _():
        acc_scr[...] = jnp.zeros(acc_scr.shape, F32)

    count = cnt_ref[i * ne + e]
    rank = rank_ref[0, 0]
    gate = gate_ref[0, 0]
    n_small = (count + chunk - 1) // chunk
    n_big = (count + 2 * chunk - 1) // (2 * chunk)

    def one_hot(c, rows):
        r = c * rows + lax.broadcasted_iota(jnp.int32, (rows, 1), 0)
        return rank == r.astype(F32)

    def rows_of(c, rows):
        return pl.ds(pl.multiple_of(c * rows, rows), rows)

    @pl.when(f == 0)
    def _():
        h = h_ref[...]

        def gather(c, carry):
            p = jnp.where(one_hot(c, chunk), 1.0, 0.0).astype(BF16)
            xs_scr[rows_of(c, chunk), :] = _dot(p, h).astype(BF16)
            return carry

        lax.fori_loop(0, n_small, gather, 0)

        def clear(c, carry):
            y_scr[rows_of(c, 2 * chunk), :] = jnp.zeros((2 * chunk, y_scr.shape[1]), F32)
            return carry

        lax.fori_loop(0, n_big, clear, 0)

    def expert(c, carry):
        xs = xs_scr[rows_of(c, chunk), :]
        a = _dot(xs, w1_ref[0])
        b = _dot(xs, w3_ref[0])
        y_scr[rows_of(c, chunk), :] += _dot((a * jax.nn.sigmoid(a) * b).astype(BF16), w2_ref[0])
        return carry

    lax.fori_loop(0, n_small, expert, 0)

    @pl.when(f == nf - 1)
    def _():
        def scatter(c, carry):
            hit = one_hot(c, 2 * chunk)
            p = jnp.where(hit, 1.0, 0.0).astype(BF16)
            gcol = jnp.sum(jnp.where(hit, gate, 0.0), axis=-1, keepdims=True)
            acc_scr[...] += _dot(p, (y_scr[rows_of(c, 2 * chunk), :] * gcol).astype(BF16), TN)
            return carry

        lax.fori_loop(0, n_big, scatter, 0)

    @pl.when((e == ne - 1) & (f == nf - 1))
    def _():
        o_ref[...] = x_ref[...] + gf_ref[0] * acc_scr[...]


def _moe(counts, x2, gf, h2, rank, gate, w1, w3, w2, seq, tt, tf, chunk):
    n, d = x2.shape
    ne, _, dff = w1.shape
    tpb = seq // tt
    grid_spec = pltpu.PrefetchScalarGridSpec(
        num_scalar_prefetch=1,
        grid=(n // tt, ne, dff // tf),
        in_specs=[
            pl.BlockSpec((tt, d), lambda i, e, f, c: (i, 0)),
            pl.BlockSpec((1, 1, d), lambda i, e, f, c: (i // tpb, 0, 0)),
            pl.BlockSpec((tt, d), lambda i, e, f, c: (i, 0)),
            pl.BlockSpec((1, 1, 1, tt), lambda i, e, f, c: (i, e, 0, 0)),
            pl.BlockSpec((1, 1, 1, tt), lambda i, e, f, c: (i, e, 0, 0)),
            pl.BlockSpec((1, d, tf), lambda i, e, f, c: (e, 0, f)),
            pl.BlockSpec((1, d, tf), lambda i, e, f, c: (e, 0, f)),
            pl.BlockSpec((1, tf, d), lambda i, e, f, c: (e, f, 0)),
        ],
        out_specs=pl.BlockSpec((tt, d), lambda i, e, f, c: (i, 0)),
        scratch_shapes=[pltpu.VMEM((tt, d), F32), pltpu.VMEM((tt, d), BF16), pltpu.VMEM((tt, d), F32)],
    )
    return pl.pallas_call(
        functools.partial(_moe_kernel, chunk=chunk),
        out_shape=jax.ShapeDtypeStruct((n, d), F32),
        grid_spec=grid_spec,
        compiler_params=_params("arbitrary", "arbitrary", "arbitrary"),
        name="moe_experts",
    )(counts, x2, gf, h2, rank, gate, w1, w3, w2)


def _overlap_matrix(ncp, nc, n_sel):
    c0 = np.arange(ncp) * CMP_STRIDE
    c1 = c0 + CMP_LEN
    s0 = np.arange(n_sel) * SEL_LEN
    s1 = s0 + SEL_LEN
    ov = ((c0[:, None] < s1[None, :]) & (c1[:, None] > s0[None, :])).astype(np.float32)
    ov[nc:] = 0.0
    return jnp.asarray(ov.T)


def _gate_expand():
    pg = np.zeros((3, W_MIX, W_MIX), np.float32)
    for br in range(3):
        for h in range(N_HEADS):
            pg[br, GATE_LANE + 3 * h + br, h * HEAD_DIM:(h + 1) * HEAD_DIM] = 1.0
    return jnp.asarray(pg, BF16)


def _mixer_layer(x2, b, s, mod, norm_g, rope, w_in, qk_gain, pe_k, pe_v, ck1, ck2, cv1, cv2,
                 fox_b, w_branch, w_out):
    n, d = x2.shape
    sh_a, sc_a, g_a = mod[0], mod[1], mod[2]
    w_slab, gain, w_merge = _pack_w_in(w_in, qk_gain)
    bd = jnp.asarray(np.kron(np.eye(N_HEADS), np.full((HEAD_DIM, HEAD_DIM), 1.0 / HEAD_DIM)), BF16)
    outs = _proj(x2, norm_g, sc_a, sh_a, w_slab, gain, bd, rope, s, min(512, s))
    sl = [a.reshape(b, s, W_MIX) for a in outs[:N_SLABS]]
    dil_in = {1: (sl[S_BQ], sl[S_BK], sl[S_BV])}
    for di, dil in enumerate(_DILATIONS):
        dil_in[dil] = tuple(outs[N_SLABS + si * len(_DILATIONS) + di].reshape(b, s // dil, dil * W_MIX)
                            for si in range(len(_DIL_SLABS)))
    misc = sl[S_MISC]

    nch = s // CMP_STRIDE
    nc = nch - CMP_LEN // CMP_STRIDE + 1
    kc_raw = sl[S_KC][..., :HEAD_DIM]
    vc_raw = misc[..., VC_LANE:VC_LANE + HEAD_DIM]
    chunks = jnp.stack([kc_raw, vc_raw]).reshape(2, b, nch, CMP_STRIDE * HEAD_DIM)
    chunks_next = jnp.concatenate([chunks[:, :, 1:], jnp.zeros_like(chunks[:, :, :1])], axis=2)
    pe = jnp.stack([pe_k, pe_v]).reshape(2, 1, CMP_LEN * HEAD_DIM)
    kvc = jnp.tile(_compress(chunks, chunks_next, pe, jnp.stack([ck1, cv1]), jnp.stack([ck2, cv2])),
                   (1, 1, 1, N_HEADS))
    overlap = _overlap_matrix(nch, nc, s // SEL_LEN)
    o_cmp, selmask = _cmp_topk(sl[S_QNR], kvc[0], kvc[1].transpose(0, 2, 1).astype(BF16), overlap,
                               min(256, s), nc)
    tq, tk = min(256, s), min(512, s)
    o_sel = _causal_attn(sl[S_QR], sl[S_KSL], sl[S_VSL], tq, tk, sel=selmask)
    o_win = _window_attn(sl[S_QR], sl[S_KW], sl[S_VW], dil=1, window=NSA_WINDOW, tq=256, out_dtype=BF16)

    others = []
    for (wdw, dil) in DIL_CONFIGS[:0:-1]:
        others.append(_window_attn(*dil_in[dil], dil=dil, window=wdw // dil + 1, tq=256,
                                   out_dtype=F32, emit_lse=True))
    wdw, dil = DIL_CONFIGS[0]
    o_b = _window_attn(*dil_in[dil], dil=dil, window=wdw // dil + 1, tq=256, out_dtype=BF16, others=others)

    o_c = _stick_breaking(sl[S_CQ], sl[S_CK], sl[S_CV], min(256, s))

    bias_vec = jnp.zeros((1, W_MIX), F32).at[0, FOX_LANE:FOX_LANE + N_HEADS].set(fox_b)
    fcum = _fox_cumsum(misc, bias_vec, min(512, s))
    qk_bound = 1.02 * HEAD_DIM ** 0.5 * jnp.max(jnp.abs(qk_gain[6])) * jnp.max(jnp.abs(qk_gain[7])) + 0.05
    o_d = _causal_attn(sl[S_DQ], sl[S_DK], sl[S_DV], tq, tk, fcum=fcum[..., FOX_LANE:FOX_LANE + N_HEADS],
                       qk_bound=qk_bound)

    wb16 = w_branch.astype(BF16)
    flat = lambda a: a.reshape(n, W_MIX)
    return _merge(x2, norm_g, sc_a, sh_a, g_a, w_merge, flat(misc), _gate_expand(),
                  flat(o_cmp), flat(o_sel), flat(o_win), flat(o_b), flat(o_c), flat(o_d),
                  wb16[0], wb16[1], wb16[2], wb16[3], w_out.astype(BF16), s, min(512, s))


def kernel(x, c, positions, w_ada, b_ada, norm_mix, norm_ffn, w_in, qk_gain, nsa_pe_k, nsa_pe_v,
           nsa_ck_w1, nsa_ck_w2, nsa_cv_w1, nsa_cv_w2, fox_bias, w_branch, w_out,
           ffn_w1, ffn_w3, ffn_w2, router_w, moe_w1, moe_w3, moe_w2):
    b, s, d = x.shape
    depth = w_ada.shape[0]
    rope = _rope_tables(positions)
    mods = _ada(c, w_ada, b_ada).reshape(depth, b, 6, 1, d).transpose(0, 2, 1, 3, 4)
    x2 = x.reshape(b * s, d)
    for l in range(depth):
        mod = mods[l]
        x2 = _mixer_layer(x2, b, s, mod[0:3], norm_mix[l].reshape(1, d), rope, w_in[l], qk_gain[l],
                          nsa_pe_k[l], nsa_pe_v[l], nsa_ck_w1[l], nsa_ck_w2[l], nsa_cv_w1[l],
                          nsa_cv_w2[l], fox_bias[l], w_branch[l], w_out[l])
        sh_f, sc_f, g_f = mod[3], mod[4], mod[5]
        gn = norm_ffn[l].reshape(1, d)
        e = l // 2
        if l % 2 == 0:
            dff = ffn_w1.shape[2]
            x2 = _ffn(x2, gn, sc_f, sh_f, g_f, ffn_w1[e].astype(BF16), ffn_w3[e].astype(BF16),
                      ffn_w2[e].astype(BF16), s, min(1024, s), dff // 2)
        else:
            tt = min(1024, s)
            dff = moe_w1.shape[3]
            h2, rank, gate, cnt = _route(x2, gn, sc_f, sh_f, router_w[e].T, s, tt)
            counts = cnt[:, :, 0].astype(jnp.int32).reshape(-1)
            x2 = _moe(counts, x2, g_f, h2, rank, gate, moe_w1[e].astype(BF16), moe_w3[e].astype(BF16),
                      moe_w2[e].astype(BF16), s, tt, dff // 2, 128)
    return x2.reshape(b, s, d)
```

```python
import functools

import numpy as np
import jax
import jax.numpy as jnp
from jax import lax
from jax.experimental import pallas as pl
from jax.experimental.pallas import tpu as pltpu

F32 = jnp.float32
BF16 = jnp.bfloat16
HI = lax.Precision.HIGHEST

LANES = 128
VMEM_LIMIT = 52 * 1024 * 1024

HEAD_DIM = 64
HEAD_SHIFT = 6
N_HEADS = 4
W_MIX = N_HEADS * HEAD_DIM
N_MIXERS = 4
ROPE_THETA = 500000.0
ROPE_DIMS = HEAD_DIM // 4
ROPE_HALF = ROPE_DIMS // 2
EPS = 1e-6
LOG2E = 1.4426950408889634
NEG = -1e30
BIG = 1e30
CMP_LEN = 32
CMP_STRIDE = 16
SEL_LEN = 64
SEL_SHIFT = 6
TOPN = 16
SEL_NEG = 2.0 ** 30
NSA_WINDOW = 512
DIL_CONFIGS = ((128, 1), (512, 4), (2048, 16))
SB_STOP = -110.0
FOX_STOP = 108.0
GATE_LANE = 0
FOX_LANE = 3 * N_HEADS
VC_LANE = HEAD_DIM

NN = (((1,), (0,)), ((), ()))
NT = (((1,), (1,)), ((), ()))
TN = (((0,), (0,)), ((), ()))


def _dot(a, b, dims=NN, precision=None):
    return lax.dot_general(a, b, dims, precision=precision, preferred_element_type=F32)


def _dot_split(a, b_bf16, dims=NN):
    hi = a.astype(BF16)
    lo = (a - hi.astype(F32)).astype(BF16)
    return _dot(hi, b_bf16, dims) + _dot(lo, b_bf16, dims)


def _params(*sem):
    return pltpu.CompilerParams(dimension_semantics=sem, vmem_limit_bytes=VMEM_LIMIT)


def _mod_norm(x, g, sc, sh):
    ms = jnp.mean(x * x, axis=-1, keepdims=True)
    return (x * lax.rsqrt(ms + EPS) * g) * (1.0 + sc) + sh


def _head_masks():
    lane = lax.broadcasted_iota(jnp.int32, (1, W_MIX), 1)
    return [(lane >> HEAD_SHIFT) == h for h in range(N_HEADS)]


def _stack_masked(q, qm_scr, tq):
    for h, hm in enumerate(_head_masks()):
        qm_scr[h * tq:(h + 1) * tq, :] = jnp.where(hm, q, jnp.zeros_like(q))


def _pick_heads(stacked, tq, scale=None):
    out = None
    for h, hm in enumerate(_head_masks()):
        blk = stacked[h * tq:(h + 1) * tq, :]
        if scale is not None:
            blk = blk * scale[h]
        out = jnp.where(hm, blk, 0.0 if out is None else out)
    return out


def _ada_kernel(c_ref, w_ref, b_ref, o_ref):
    c = c_ref[...]
    ca = c * jax.nn.sigmoid(c)
    o_ref[0] = _dot(ca, w_ref[0], NN, HI) + b_ref[0]


def _ada(c, w_ada, b_ada):
    depth, d, n6 = w_ada.shape
    b = c.shape[0]
    tn = n6 // 4
    return pl.pallas_call(
        _ada_kernel,
        out_shape=jax.ShapeDtypeStruct((depth, b, n6), F32),
        grid=(depth, n6 // tn),
        in_specs=[
            pl.BlockSpec((b, d), lambda l, j: (0, 0)),
            pl.BlockSpec((1, d, tn), lambda l, j: (l, 0, j)),
            pl.BlockSpec((1, 1, tn), lambda l, j: (l, 0, j)),
        ],
        out_specs=pl.BlockSpec((1, b, tn), lambda l, j: (l, 0, j)),
        compiler_params=_params("arbitrary", "arbitrary"),
        name="ada_mod",
    )(c, w_ada, b_ada.reshape(depth, 1, n6))


_SLABS = (
    (True, False, F32), (True, True, BF16), (True, False, F32),
    (True, True, BF16), (True, True, BF16), (False, False, BF16), (False, False, BF16),
    (True, True, BF16), (True, True, BF16), (False, False, BF16),
    (False, False, BF16), (False, False, BF16), (False, False, BF16),
    (True, False, BF16), (True, False, BF16), (False, False, BF16),
    (False, False, F32),
)
N_SLABS = len(_SLABS)
(S_QNR, S_QR, S_KC, S_KSL, S_KW, S_VSL, S_VW, S_BQ, S_BK, S_BV, S_CQ, S_CK, S_CV,
 S_DQ, S_DK, S_DV, S_MISC) = range(N_SLABS)
_DIL_SLABS = (S_BQ, S_BK, S_BV)
_DILATIONS = tuple(d for (_, d) in DIL_CONFIGS if d > 1)


def _proj_kernel(x_ref, g_ref, sc_ref, sh_ref, w_ref, gain_ref, bd_ref, rope_ref, *out_refs):
    h = _mod_norm(x_ref[...], g_ref[...], sc_ref[0], sh_ref[0]).astype(BF16)
    bd = bd_ref[...]
    for s, (norm, rope, _) in enumerate(_SLABS):
        y = _dot(h, w_ref[:, s * W_MIX:(s + 1) * W_MIX])
        if norm:
            ms = _dot_split(y * y, bd)
            y = y * lax.rsqrt(ms + EPS)
        y = y * gain_ref[s]
        if rope:
            y = (y * rope_ref[0, 0] + pltpu.roll(y, W_MIX - ROPE_HALF, 1) * rope_ref[0, 1]
                 + pltpu.roll(y, ROPE_HALF, 1) * rope_ref[0, 2])
        out_refs[s][...] = y.astype(out_refs[s].dtype)
        if s in _DIL_SLABS:
            stage_scr = out_refs[-1]
            tt = y.shape[0]
            for c in range(W_MIX // LANES):
                stage_scr[c] = y[:, c * LANES:(c + 1) * LANES]
            for di, dil in enumerate(_DILATIONS):
                o_ref = out_refs[N_SLABS + _DIL_SLABS.index(s) * len(_DILATIONS) + di]
                for r in range(dil):
                    for c in range(W_MIX // LANES):
                        lo = r * W_MIX + c * LANES
                        o_ref[:, lo:lo + LANES] = (
                            stage_scr[c, pl.ds(r, tt // dil, stride=dil), :].astype(o_ref.dtype))


def _proj(x2, g, sc, sh, w_slab, gain, bd, rope, seq, tt):
    n, d = x2.shape
    tpb = seq // tt
    out_shape = [jax.ShapeDtypeStruct((n, W_MIX), dt) for (_, _, dt) in _SLABS]
    out_specs = [pl.BlockSpec((tt, W_MIX), lambda i: (i, 0)) for _ in _SLABS]
    for _ in _DIL_SLABS:
        for dil in _DILATIONS:
            out_shape.append(jax.ShapeDtypeStruct((n // dil, dil * W_MIX), BF16))
            out_specs.append(pl.BlockSpec((tt // dil, dil * W_MIX), lambda i: (i, 0)))
    return pl.pallas_call(
        _proj_kernel,
        out_shape=out_shape,
        grid=(n // tt,),
        in_specs=[
            pl.BlockSpec((tt, d), lambda i: (i, 0)),
            pl.BlockSpec((1, d), lambda i: (0, 0)),
            pl.BlockSpec((1, 1, d), lambda i: (i // tpb, 0, 0)),
            pl.BlockSpec((1, 1, d), lambda i: (i // tpb, 0, 0)),
            pl.BlockSpec((d, N_SLABS * W_MIX), lambda i: (0, 0)),
            pl.BlockSpec((N_SLABS, 1, W_MIX), lambda i: (0, 0, 0)),
            pl.BlockSpec((W_MIX, W_MIX), lambda i: (0, 0)),
            pl.BlockSpec((1, 3, tt, W_MIX), lambda i: (i // tpb, 0, i % tpb, 0)),
        ],
        out_specs=out_specs,
        scratch_shapes=[pltpu.VMEM((W_MIX // LANES, tt, LANES), F32)],
        compiler_params=_params("arbitrary"),
        name="in_proj",
    )(x2, g, sc, sh, w_slab, gain, bd, rope)


def _pack_w_in(w_in, qk_gain):
    d = w_in.shape[0]
    o = 0
    a_q = w_in[:, o:o + W_MIX]; o += W_MIX
    a_kv = w_in[:, o:o + 6 * HEAD_DIM]; o += 6 * HEAD_DIM
    a_g = w_in[:, o:o + 3 * N_HEADS]; o += 3 * N_HEADS
    b_qkv = w_in[:, o:o + 3 * W_MIX]; o += 3 * W_MIX
    c_qkv = w_in[:, o:o + 3 * W_MIX]; o += 3 * W_MIX
    d_qkv = w_in[:, o:o + 3 * W_MIX]; o += 3 * W_MIX
    d_f = w_in[:, o:o + N_HEADS]; o += N_HEADS
    w_merge = w_in[:, o:]
    kc, vc, ksl, vsl, kw, vw = (a_kv[:, i * HEAD_DIM:(i + 1) * HEAD_DIM] for i in range(6))
    zeros = lambda w: jnp.zeros((d, w), w_in.dtype)
    rep = lambda w: jnp.tile(w, (1, N_HEADS))
    misc = jnp.concatenate([a_g, d_f, zeros(VC_LANE - FOX_LANE - N_HEADS), vc, zeros(W_MIX - 2 * HEAD_DIM)], axis=1)
    slabs = [
        a_q, a_q, jnp.concatenate([kc, zeros(W_MIX - HEAD_DIM)], axis=1),
        rep(ksl), rep(kw), rep(vsl), rep(vw),
        b_qkv[:, :W_MIX], b_qkv[:, W_MIX:2 * W_MIX], b_qkv[:, 2 * W_MIX:],
        c_qkv[:, :W_MIX], c_qkv[:, W_MIX:2 * W_MIX], c_qkv[:, 2 * W_MIX:],
        d_qkv[:, :W_MIX], d_qkv[:, W_MIX:2 * W_MIX], d_qkv[:, 2 * W_MIX:],
        misc,
    ]
    w_slab = jnp.concatenate(slabs, axis=1).astype(BF16)
    scale = HEAD_DIM ** -0.5
    t4 = lambda gvec: jnp.tile(gvec, N_HEADS)
    one = jnp.ones((W_MIX,), F32)
    scale2 = scale * LOG2E
    gains = [
        t4(qk_gain[0]) * scale, t4(qk_gain[0]) * scale2,
        jnp.concatenate([qk_gain[1], jnp.ones((W_MIX - HEAD_DIM,), F32)]),
        t4(qk_gain[2]), t4(qk_gain[3]), one, one,
        t4(qk_gain[4]) * scale2, t4(qk_gain[5]), one,
        one * scale, one, one,
        t4(qk_gain[6]) * scale2, t4(qk_gain[7]), one,
        one,
    ]
    gain = jnp.stack(gains).reshape(N_SLABS, 1, W_MIX).astype(F32)
    return w_slab, gain, w_merge.astype(BF16)


def _rope_tables(positions):
    inv = ROPE_THETA ** (-jnp.arange(0, ROPE_DIMS, 2, dtype=F32) / ROPE_DIMS)
    ang = positions.astype(F32)[..., None] * inv
    cos, sin = jnp.cos(ang), jnp.sin(ang)
    b, s, _ = cos.shape
    pad1 = jnp.ones((b, s, HEAD_DIM - ROPE_DIMS), F32)
    pad0 = jnp.zeros((b, s, HEAD_DIM - ROPE_DIMS), F32)
    z8 = jnp.zeros_like(sin)
    c64 = jnp.concatenate([cos, cos, pad1], axis=-1)
    s1_64 = jnp.concatenate([-sin, z8, pad0], axis=-1)
    s2_64 = jnp.concatenate([z8, sin, pad0], axis=-1)
    heads = lambda t: jnp.tile(t, (1, 1, N_HEADS))
    return jnp.stack([heads(c64), heads(s1_64), heads(s2_64)], axis=1)


def _compress_kernel(a_ref, b_ref, pe_ref, w1_ref, w2_ref, o_ref):
    half = w1_ref.shape[1] // 2
    w1 = w1_ref[0]
    hid = (_dot(a_ref[0, 0], w1[:half], NN, HI) + _dot(b_ref[0, 0], w1[half:], NN, HI)
           + _dot(pe_ref[0], w1, NN, HI))
    o_ref[0, 0] = _dot(jax.nn.gelu(hid), w2_ref[0], NN, HI)


def _compress(ch, chn, pe, w1, w2):
    _, b, ncp, cw = ch.shape
    hid = w1.shape[2]
    return pl.pallas_call(
        _compress_kernel,
        out_shape=jax.ShapeDtypeStruct((2, b, ncp, HEAD_DIM), F32),
        grid=(2, b),
        in_specs=[
            pl.BlockSpec((1, 1, ncp, cw), lambda k, i: (k, i, 0, 0)),
            pl.BlockSpec((1, 1, ncp, cw), lambda k, i: (k, i, 0, 0)),
            pl.BlockSpec((1, 1, 2 * cw), lambda k, i: (k, 0, 0)),
            pl.BlockSpec((1, 2 * cw, hid), lambda k, i: (k, 0, 0)),
            pl.BlockSpec((1, hid, HEAD_DIM), lambda k, i: (k, 0, 0)),
        ],
        out_specs=pl.BlockSpec((1, 1, ncp, HEAD_DIM), lambda k, i: (k, i, 0, 0)),
        compiler_params=_params("arbitrary", "arbitrary"),
        name="nsa_compress",
    )(ch, chn, pe, w1, w2)


def _cmp_kernel(q_ref, kc_ref, vct_ref, ovt_ref, o_ref, sel_ref, qh_scr, ql_scr, acc_scr, *, tq, nc, n_sel):
    i = pl.program_id(1)
    ncp = kc_ref.shape[1]
    q = q_ref[0]
    q_hi = q.astype(BF16)
    _stack_masked(q_hi, qh_scr, tq)
    _stack_masked((q - q_hi.astype(F32)).astype(BF16), ql_scr, tq)
    kc = kc_ref[0]
    k_hi = kc.astype(BF16)
    k_lo = (kc - k_hi.astype(F32)).astype(BF16)
    ss = []
    for h in range(N_HEADS):
        rs = slice(h * tq, (h + 1) * tq)
        ss.append(_dot(k_hi, qh_scr[rs, :], NT) + _dot(k_lo, qh_scr[rs, :], NT)
                  + _dot(k_hi, ql_scr[rs, :], NT))
    t = i * tq + lax.broadcasted_iota(jnp.int32, (1, tq), 1)
    c = lax.broadcasted_iota(jnp.int32, (ncp, 1), 0)
    mask = (c * CMP_STRIDE + (CMP_LEN - 1) <= t) & (c < nc)
    psum = None
    ps = []
    for h in range(N_HEADS):
        sm = jnp.where(mask, ss[h], NEG)
        m = jnp.max(sm, axis=0, keepdims=True)
        e = jnp.where(mask, jnp.exp(sm - m), 0.0)
        l = jnp.sum(e, axis=0, keepdims=True)
        p = e / jnp.maximum(l, 1e-30)
        psum = p if psum is None else psum + p
        ps.append(p.astype(BF16))
    vct = vct_ref[0]
    for h in range(N_HEADS):
        rs = slice(h * HEAD_DIM, (h + 1) * HEAD_DIM)
        acc_scr[rs, :] = _dot(vct[rs, :], ps[h])
    o_ref[0] = acc_scr[...].T.astype(o_ref.dtype)
    imp = _dot(ovt_ref[...], psum, NN, HI)
    j = lax.broadcasted_iota(jnp.int32, (n_sel, 1), 0)
    cur = t >> SEL_SHIFT
    valid = j <= cur
    forced = (j == 0) | (j == cur) | (j == cur - 1)
    score = jnp.where(valid, jnp.where(forced, BIG, imp), NEG)
    sel = jnp.zeros((n_sel, tq), F32)
    jf = j.astype(F32)
    for _ in range(min(TOPN, n_sel)):
        mx = jnp.max(score, axis=0, keepdims=True)
        idx = jnp.min(jnp.where(score == mx, jf, float(n_sel)), axis=0, keepdims=True)
        pick = jf == idx
        sel = jnp.where(pick, 1.0, sel)
        score = jnp.where(pick, -3e38, score)
    sel_ref[0] = ((sel - 1.0) * SEL_NEG).T.astype(sel_ref.dtype)


def _cmp_topk(q_nr, kc_rep, vc_rep, overlap, tq, nc):
    b, seq, _ = q_nr.shape
    ncp = kc_rep.shape[1]
    n_sel = seq // SEL_LEN
    return pl.pallas_call(
        functools.partial(_cmp_kernel, tq=tq, nc=nc, n_sel=n_sel),
        out_shape=[jax.ShapeDtypeStruct((b, seq, W_MIX), BF16),
                   jax.ShapeDtypeStruct((b, seq, n_sel), BF16)],
        grid=(b, seq // tq),
        in_specs=[
            pl.BlockSpec((1, tq, W_MIX), lambda g, i: (g, i, 0)),
            pl.BlockSpec((1, ncp, W_MIX), lambda g, i: (g, 0, 0)),
            pl.BlockSpec((1, W_MIX, ncp), lambda g, i: (g, 0, 0)),
            pl.BlockSpec((n_sel, ncp), lambda g, i: (0, 0)),
        ],
        out_specs=[pl.BlockSpec((1, tq, W_MIX), lambda g, i: (g, i, 0)),
                   pl.BlockSpec((1, tq, n_sel), lambda g, i: (g, i, 0))],
        scratch_shapes=[pltpu.VMEM((N_HEADS * tq, W_MIX), BF16), pltpu.VMEM((N_HEADS * tq, W_MIX), BF16),
                        pltpu.VMEM((W_MIX, tq), F32)],
        compiler_params=_params("arbitrary", "arbitrary"),
        name="nsa_cmp_topk",
    )(q_nr, kc_rep, vc_rep, overlap)


def _causal_kernel(*refs, tq, tk, has_sel, has_bias):
    it = iter(refs)
    q_ref, k_ref, vt_ref = next(it), next(it), next(it)
    sel_ref = next(it) if has_sel else None
    fa_ref, fb_ref, fq_ref, thr_ref = (next(it) for _ in range(4)) if has_bias else (None,) * 4
    o_ref, qm_scr, acc_scr, sa_scr, sb_scr, m_scr, l_scr = (next(it) for _ in range(7))
    i = pl.program_id(1)
    t0 = i * tq
    lane = lax.broadcasted_iota(jnp.int32, (1, W_MIX), 1)
    if has_sel:
        n_sel = sel_ref.shape[2]
        qf = q_ref[0].astype(F32)
        selb = sel_ref[0].astype(F32)
        selb = jnp.concatenate([selb, jnp.zeros((tq, W_MIX - n_sel), F32)], axis=1)
        selb = pltpu.roll(selb, HEAD_DIM, 1)
        for h in range(N_HEADS):
            rot = qf if h == 0 else pltpu.roll(qf, W_MIX - h * HEAD_DIM, 1)
            qm_scr[h * tq:(h + 1) * tq, :] = jnp.where(lane < HEAD_DIM, rot, selb).astype(BF16)
    else:
        _stack_masked(q_ref[0], qm_scr, tq)
    acc_scr[...] = jnp.zeros(acc_scr.shape, F32)
    t_pos = t0 + lax.broadcasted_iota(jnp.int32, (1, tq), 1)

    def scores(kt, s_buf):
        k_t = k_ref[0, kt]
        if has_sel:
            blk = (kt * tk + lax.broadcasted_iota(jnp.int32, (tk, 1), 0)) >> SEL_SHIFT
            one_hot = jnp.where(blk == lane - HEAD_DIM, 1.0, 0.0).astype(BF16)
            k_t = jnp.where(lane < HEAD_DIM, k_t, one_hot)
        for h in range(N_HEADS):
            s = _dot(k_t, qm_scr[h * tq:(h + 1) * tq, :], NT)
            if has_bias:
                s = s + _dot(fa_ref[0, kt], fb_ref[0, h])
            s_buf[h] = s

    def update(kt, s_buf, diag):
        vt_t = vt_ref[0, kt]
        mask = None
        if diag:
            s_pos = kt * tk + lax.broadcasted_iota(jnp.int32, (tk, 1), 0)
            mask = s_pos <= t_pos
        ps, alphas = [], []
        for h in range(N_HEADS):
            s = s_buf[h]
            if mask is not None:
                s = jnp.where(mask, s, NEG)
            m_old = m_scr[h]
            m_new = jnp.maximum(m_old, jnp.max(s, axis=0, keepdims=True))
            p = jnp.exp2(s - m_new)
            alpha = jnp.exp2(m_old - m_new)
            m_scr[h] = m_new
            l_scr[h] = alpha * l_scr[h] + jnp.sum(p, axis=0, keepdims=True)
            ps.append(p.astype(BF16))
            alphas.append(alpha)
        for h in range(N_HEADS):
            rs = slice(h * HEAD_DIM, (h + 1) * HEAD_DIM)
            acc_scr[rs, :] = alphas[h] * acc_scr[rs, :] + _dot(vt_t[rs, :], ps[h])

    def live(kt_next):
        lane = lax.broadcasted_iota(jnp.int32, (1, thr_ref.shape[3]), 1)
        hit = None
        for h in range(N_HEADS):
            top = jnp.max(fq_ref[0, h] - m_scr[h], axis=-1, keepdims=True)
            need = top >= thr_ref[0, h]
            hit = need if hit is None else (hit | need)
        return jnp.max(jnp.where(hit & (lane == kt_next), 1.0, 0.0)) > 0.5

    n_last = t0 // tk
    m_scr[...] = jnp.full(m_scr.shape, NEG, F32)
    l_scr[...] = jnp.zeros(l_scr.shape, F32)
    scores(n_last, sa_scr)
    scores(jnp.maximum(n_last - 1, 0), sb_scr)
    update(n_last, sa_scr, True)

    def pair(j):
        kt = n_last - 1 - 2 * j
        scores(kt - 1, sa_scr)
        update(kt, sb_scr, False)
        scores(jnp.maximum(kt - 2, 0), sb_scr)
        update(kt - 1, sa_scr, False)
        return kt - 2

    n_pairs = n_last // 2
    if has_bias:
        def cond(state):
            j, go = state
            return (j < n_pairs) & go

        def body(state):
            j, _ = state
            return j + 1, live(pair(j))

        _, go = lax.while_loop(cond, body, (jnp.int32(0), live(n_last - 1)))
    else:
        lax.fori_loop(0, n_pairs, lambda j, c: (pair(j), c)[1], 0)
        go = True

    @pl.when((n_last % 2 == 1) & go)
    def _():
        update(0, sb_scr, False)

    ls = [l_scr[h] for h in range(N_HEADS)]
    for h in range(N_HEADS):
        rs = slice(h * HEAD_DIM, (h + 1) * HEAD_DIM)
        acc_scr[rs, :] = acc_scr[rs, :] / ls[h]
    o_ref[0] = acc_scr[...].T.astype(o_ref.dtype)


def _split3(x):
    def cut(v):
        bits = lax.bitcast_convert_type(v, jnp.uint32) & jnp.uint32(0xFFFF0000)
        return lax.bitcast_convert_type(bits, F32)

    hi = cut(x)
    r1 = x - hi
    mid = cut(r1)
    lo = r1 - mid
    return hi.astype(BF16), mid.astype(BF16), lo.astype(BF16)


def _causal_attn(q, k, v, tq, tk, sel=None, fcum=None, qk_bound=None):
    b, s, _ = q.shape
    nk = s // tk
    rows = N_HEADS * tq
    n_s = N_HEADS
    vt = v.reshape(b, nk, tk, W_MIX).transpose(0, 1, 3, 2)
    args = [q, k.reshape(b, nk, tk, W_MIX), vt]
    in_specs = [
        pl.BlockSpec((1, tq, W_MIX), lambda a, i: (a, i, 0)),
        pl.BlockSpec((1, nk, tk, W_MIX), lambda a, i: (a, 0, 0, 0)),
        pl.BlockSpec((1, nk, W_MIX, tk), lambda a, i: (a, 0, 0, 0)),
    ]
    if sel is not None:
        args.append(sel)
        in_specs.append(pl.BlockSpec((1, tq, sel.shape[2]), lambda a, i: (a, i, 0)))
    if fcum is not None:
        fcum = fcum * LOG2E
        qk_bound = qk_bound * LOG2E
        nf = 2 * 3 + 2
        ones = jnp.ones(fcum.shape, BF16)
        zero = jnp.zeros(fcum.shape, BF16)
        parts = _split3(fcum)
        key_f = jnp.stack([ones, ones, ones] + [-p for p in parts] + [zero, zero], axis=-1)
        qry_f = jnp.stack(list(parts) + [ones, ones, ones, zero, zero], axis=-1)
        fa = key_f.reshape(b, nk, tk, N_HEADS * nf)
        eye = jnp.eye(N_HEADS, dtype=BF16)
        fb = jnp.einsum('bshf,hg->bgshf', qry_f, eye).reshape(b, N_HEADS, s, N_HEADS * nf).transpose(0, 1, 3, 2)
        f_rows = fcum.transpose(0, 2, 1)
        f_end = f_rows[:, :, tk - 1::tk]
        thr = jnp.pad(f_end - (qk_bound + FOX_STOP * LOG2E), ((0, 0), (0, 0), (0, LANES - nk)),
                      constant_values=BIG).reshape(b, N_HEADS, 1, LANES)
        args += [fa, fb, f_rows.reshape(b, N_HEADS, 1, s), thr]
        in_specs += [pl.BlockSpec((1, nk, tk, N_HEADS * nf), lambda a, i: (a, 0, 0, 0)),
                     pl.BlockSpec((1, N_HEADS, N_HEADS * nf, tq), lambda a, i: (a, 0, 0, i)),
                     pl.BlockSpec((1, N_HEADS, 1, tq), lambda a, i: (a, 0, 0, i)),
                     pl.BlockSpec((1, N_HEADS, 1, LANES), lambda a, i: (a, 0, 0, 0))]
    return pl.pallas_call(
        functools.partial(_causal_kernel, tq=tq, tk=tk, has_sel=sel is not None, has_bias=fcum is not None),
        out_shape=jax.ShapeDtypeStruct((b, s, W_MIX), BF16),
        grid=(b, s // tq),
        in_specs=in_specs,
        out_specs=pl.BlockSpec((1, tq, W_MIX), lambda a, i: (a, i, 0)),
        scratch_shapes=[pltpu.VMEM((rows, W_MIX), BF16), pltpu.VMEM((W_MIX, tq), F32),
                        pltpu.VMEM((n_s, tk, tq), F32), pltpu.VMEM((n_s, tk, tq), F32),
                        pltpu.VMEM((N_HEADS, 1, tq), F32), pltpu.VMEM((N_HEADS, 1, tq), F32)],
        compiler_params=_params("arbitrary", "arbitrary"),
        name="causal_sel%d_bias%d" % (sel is not None, fcum is not None),
    )(*args)


def _window_kernel(*refs, tq, wk, pad, window, ls, emit_lse, other_dils):
    it = iter(refs)
    q_ref, k_ref, v_ref = next(it), next(it), next(it)
    n_other = len(other_dils)
    others_in = [(next(it), next(it)) for _ in range(n_other)]
    wexp_ref = next(it) if n_other else None
    o_ref = next(it)
    lse_ref = next(it) if emit_lse else None
    qm_scr, acc_scr = next(it), next(it)
    others = []
    if n_other:
        og_scr, lg_scr = next(it), next(it)
        for g, (dg, (og_ref, lg_ref)) in enumerate(zip(other_dils, others_in)):
            for r in range(dg):
                rows = pl.ds(r, tq // dg, stride=dg)
                for c in range(W_MIX // LANES):
                    lo = r * W_MIX + c * LANES
                    og_scr[g, c, rows, :] = og_ref[0, :, lo:lo + LANES]
                lg_scr[g, rows, :] = lg_ref[0, :, r * LANES:(r + 1) * LANES]
            others.append((og_scr.at[g], lg_scr.at[g]))
    t0 = pl.program_id(2) * tq
    start = pl.multiple_of(jnp.clip(t0 - pad, 0, ls - wk), LANES)
    _stack_masked(q_ref[0], qm_scr, tq)
    k_w = k_ref[0, pl.ds(start, wk), :]
    vt_w = v_ref[0, pl.ds(start, wk), :].astype(F32).T.astype(BF16)
    ss = [_dot(k_w, qm_scr[h * tq:(h + 1) * tq, :], NT) for h in range(N_HEADS)]
    t_pos = t0 + lax.broadcasted_iota(jnp.int32, (1, tq), 1)
    s_pos = start + lax.broadcasted_iota(jnp.int32, (wk, 1), 0)
    mask = (s_pos <= t_pos) & (t_pos - s_pos < window)
    ps, inv_ls, lses = [], [], []
    for h in range(N_HEADS):
        s = jnp.where(mask, ss[h], NEG)
        m = jnp.max(s, axis=0, keepdims=True)
        p = jnp.exp2(s - m)
        l = jnp.sum(p, axis=0, keepdims=True)
        ps.append(p.astype(BF16))
        inv_ls.append(1.0 / l)
        lses.append(m + jnp.log2(l))
    for h in range(N_HEADS):
        rs = slice(h * HEAD_DIM, (h + 1) * HEAD_DIM)
        acc_scr[rs, :] = _dot(vt_w[rs, :], ps[h]) * inv_ls[h]
    o_self = acc_scr[...].T
    if emit_lse or n_other:
        row = lax.broadcasted_iota(jnp.int32, (LANES, 1), 0)
        stat = jnp.zeros((LANES, tq), F32)
        for h in range(N_HEADS):
            stat = jnp.where(row == h, lses[h], stat)
        lse_tile = stat.T
    if n_other:
        lg_t = [lg[...].T for (_, lg) in others]
        row = lax.broadcasted_iota(jnp.int32, (LANES, 1), 0)
        wmat = jnp.zeros((LANES, tq), F32)
        for h in range(N_HEADS):
            group_lse = [lses[h]] + [t[h:h + 1, :] for t in lg_t]
            top = functools.reduce(jnp.maximum, group_lse)
            ws = [jnp.exp2(x - top) for x in group_lse]
            inv_den = 1.0 / functools.reduce(jnp.add, ws)
            for g, w in enumerate(ws):
                wmat = jnp.where(row == g * N_HEADS + h, w * inv_den, wmat)
        wt = wmat.T
        groups = [o_self] + [jnp.concatenate([og[c] for c in range(W_MIX // LANES)], axis=1)
                             for (og, _) in others]
        out = None
        for g, o_g in enumerate(groups):
            term = _dot_split(wt, wexp_ref[g]) * o_g
            out = term if out is None else out + term
        o_ref[0] = out.astype(o_ref.dtype)
    else:
        o_ref[0] = o_self.astype(o_ref.dtype)
    if emit_lse:
        lse_ref[0] = lse_tile


def _window_attn(q, k, v, *, dil, window, tq, out_dtype, emit_lse=False, others=()):
    b, ls, _ = q.shape
    tq = min(tq, ls)
    pad = -(-(window - 1) // LANES) * LANES
    wk = min(tq + pad, ls)
    rows = N_HEADS * tq
    args = [q, k, v]
    in_specs = [
        pl.BlockSpec((1, tq, W_MIX), lambda a, r, i: (a, i, r)),
        pl.BlockSpec((1, ls, W_MIX), lambda a, r, i: (a, 0, r)),
        pl.BlockSpec((1, ls, W_MIX), lambda a, r, i: (a, 0, r)),
    ]
    scratch = [pltpu.VMEM((rows, W_MIX), BF16), pltpu.VMEM((W_MIX, tq), F32)]
    for (o_g, lse_g, dg) in others:
        args += [o_g, lse_g]
        in_specs += [pl.BlockSpec((1, tq // dg, dg * W_MIX), lambda a, r, i: (a, i, 0)),
                     pl.BlockSpec((1, tq // dg, dg * LANES), lambda a, r, i: (a, i, 0))]
    if others:
        wexp = np.zeros((len(others) + 1, LANES, W_MIX), np.float32)
        for g in range(len(others) + 1):
            for h in range(N_HEADS):
                wexp[g, g * N_HEADS + h, h * HEAD_DIM:(h + 1) * HEAD_DIM] = 1.0
        args.append(jnp.asarray(wexp, BF16))
        in_specs.append(pl.BlockSpec(wexp.shape, lambda a, r, i: (0, 0, 0)))
        scratch += [pltpu.VMEM((len(others), W_MIX // LANES, tq, LANES), F32),
                    pltpu.VMEM((len(others), tq, LANES), F32)]
    out_shape = [jax.ShapeDtypeStruct((b, ls, dil * W_MIX), out_dtype)]
    out_specs = [pl.BlockSpec((1, tq, W_MIX), lambda a, r, i: (a, i, r))]
    if emit_lse:
        out_shape.append(jax.ShapeDtypeStruct((b, ls, dil * LANES), F32))
        out_specs.append(pl.BlockSpec((1, tq, LANES), lambda a, r, i: (a, i, r)))
    res = pl.pallas_call(
        functools.partial(_window_kernel, tq=tq, wk=wk, pad=pad, window=window, ls=ls,
                          emit_lse=emit_lse, other_dils=tuple(dg for (_, _, dg) in others)),
        out_shape=out_shape,
        grid=(b, dil, ls // tq),
        in_specs=in_specs,
        out_specs=out_specs,
        scratch_shapes=scratch,
        compiler_params=_params("arbitrary", "arbitrary", "arbitrary"),
        name="window_d%d_w%d" % (dil, window),
    )(*args)
    return (res[0], res[1], dil) if emit_lse else res[0]


def _sb_kernel(q_ref, k_ref, v_ref, tri_ref, o_ref, qm_scr, a_scr, carry_scr, acc_scr, *, tq):
    i = pl.program_id(1)
    t0 = i * tq
    rows = N_HEADS * tq
    _stack_masked(q_ref[0], qm_scr, tq)
    carry_scr[...] = jnp.zeros(carry_scr.shape, F32)
    acc_scr[...] = jnp.zeros(acc_scr.shape, F32)
    r = lax.broadcasted_iota(jnp.int32, (rows, 1), 0)
    t_pos = t0 + (r & (tq - 1))

    def tile(kt, diag):
        z = _dot(qm_scr[...], k_ref[0, kt], NT)
        lg = -(jnp.maximum(z, 0.0) + jnp.log(1.0 + jnp.exp(-jnp.abs(z))))
        if diag:
            s_pos = kt * tq + lax.broadcasted_iota(jnp.int32, (1, tq), 1)
            strict = s_pos < t_pos
            lg = jnp.where(strict, lg, 0.0)
        cum = _dot_split(lg, tri_ref[...]) + carry_scr[...]
        a = jnp.exp(z + cum)
        if diag:
            a = jnp.where(strict, a, 0.0)
        a_scr[...] = a.astype(BF16)
        acc_scr[...] += _dot(a_scr[...], v_ref[0, kt])
        carry_scr[...] += jnp.sum(lg, axis=-1, keepdims=True)

    tile(i, True)

    def cond(state):
        j, top = state
        return (j < i) & (top > SB_STOP)

    def body(state):
        j, _ = state
        tile(i - 1 - j, False)
        return j + 1, jnp.max(carry_scr[...])

    lax.while_loop(cond, body, (jnp.int32(0), jnp.max(carry_scr[...])))
    o_ref[0] = _pick_heads(acc_scr[...], tq).astype(o_ref.dtype)


def _stick_breaking(q, k, v, tq):
    b, s, _ = q.shape
    nk = s // tq
    rows = N_HEADS * tq
    tri = jnp.asarray(np.tril(np.ones((tq, tq), np.float32)), BF16)
    return pl.pallas_call(
        functools.partial(_sb_kernel, tq=tq),
        out_shape=jax.ShapeDtypeStruct((b, s, W_MIX), BF16),
        grid=(b, nk),
        in_specs=[
            pl.BlockSpec((1, tq, W_MIX), lambda a, i: (a, i, 0)),
            pl.BlockSpec((1, nk, tq, W_MIX), lambda a, i: (a, 0, 0, 0)),
            pl.BlockSpec((1, nk, tq, W_MIX), lambda a, i: (a, 0, 0, 0)),
            pl.BlockSpec((tq, tq), lambda a, i: (0, 0)),
        ],
        out_specs=pl.BlockSpec((1, tq, W_MIX), lambda a, i: (a, i, 0)),
        scratch_shapes=[pltpu.VMEM((rows, W_MIX), BF16), pltpu.VMEM((rows, tq), BF16),
                        pltpu.VMEM((rows, 1), F32), pltpu.VMEM((rows, W_MIX), F32)],
        compiler_params=_params("arbitrary", "arbitrary"),
        name="stick_breaking",
    )(q, k.reshape(b, nk, tq, W_MIX), v.reshape(b, nk, tq, W_MIX), tri)


def _foxcum_kernel(x_ref, b_ref, tri_ref, o_ref, carry_scr):
    @pl.when(pl.program_id(1) == 0)
    def _():
        carry_scr[...] = jnp.zeros(carry_scr.shape, F32)

    z = x_ref[0] + b_ref[...]
    logf = jnp.minimum(z, 0.0) - jnp.log(1.0 + jnp.exp(-jnp.abs(z)))
    cum = _dot(tri_ref[...], logf, NN, HI) + carry_scr[...]
    o_ref[0] = cum
    carry_scr[...] = cum[cum.shape[0] - 1:, :]


def _fox_cumsum(misc, bias_vec, tc):
    b, s, w = misc.shape
    tri = jnp.asarray(np.tril(np.ones((tc, tc), np.float32)))
    return pl.pallas_call(
        _foxcum_kernel,
        out_shape=jax.ShapeDtypeStruct((b, s, w), F32),
        grid=(b, s // tc),
        in_specs=[
            pl.BlockSpec((1, tc, w), lambda a, i: (a, i, 0)),
            pl.BlockSpec((1, w), lambda a, i: (0, 0)),
            pl.BlockSpec((tc, tc), lambda a, i: (0, 0)),
        ],
        out_specs=pl.BlockSpec((1, tc, w), lambda a, i: (a, i, 0)),
        scratch_shapes=[pltpu.VMEM((1, w), F32)],
        compiler_params=_params("arbitrary", "arbitrary"),
        name="fox_cumsum",
    )(misc, bias_vec, tri)


def _merge_kernel(x_ref, g_ref, sc_ref, sh_ref, ga_ref, wm_ref, misc_ref, pg_ref,
                  ocmp_ref, osel_ref, owin_ref, ob_ref, oc_ref, od_ref,
                  wa_ref, wb_ref, wc_ref, wd_ref, wo_ref, o_ref):
    x = x_ref[...]
    d = x.shape[1]
    h = _mod_norm(x, g_ref[...], sc_ref[0], sh_ref[0]).astype(BF16)
    gate = jax.nn.sigmoid(misc_ref[...])
    o_a = (_dot_split(gate, pg_ref[0]) * ocmp_ref[...].astype(F32)
           + _dot_split(gate, pg_ref[1]) * osel_ref[...].astype(F32)
           + _dot_split(gate, pg_ref[2]) * owin_ref[...].astype(F32)).astype(BF16)
    mixed = jnp.zeros(x.shape, F32)
    for m, (o_m, w_ref) in enumerate(((o_a, wa_ref), (ob_ref[...], wb_ref),
                                      (oc_ref[...], wc_ref), (od_ref[...], wd_ref))):
        y = _dot(o_m, w_ref[...])
        gl = _dot(h, wm_ref[:, m * d:(m + 1) * d])
        mixed = mixed + jax.nn.sigmoid(gl) * y
    o_ref[...] = x + ga_ref[0] * _dot(mixed.astype(BF16), wo_ref[...])


def _merge(x2, g, sc, sh, ga, w_merge, misc, pg, o_cmp, o_sel, o_win, o_b, o_c, o_d,
           wa, wb, wc, wd, wo, seq, tt):
    n, d = x2.shape
    tpb = seq // tt
    row = lambda w: pl.BlockSpec((tt, w), lambda i: (i, 0))
    full = lambda a: pl.BlockSpec(a.shape, lambda i: (0,) * a.ndim)
    per_b = pl.BlockSpec((1, 1, d), lambda i: (i // tpb, 0, 0))
    return pl.pallas_call(
        _merge_kernel,
        out_shape=jax.ShapeDtypeStruct((n, d), F32),
        grid=(n // tt,),
        in_specs=[row(d), full(g), per_b, per_b, per_b, full(w_merge), row(W_MIX), full(pg)]
        + [row(W_MIX)] * 6 + [full(wa), full(wb), full(wc), full(wd), full(wo)],
        out_specs=row(d),
        compiler_params=_params("arbitrary"),
        name="merge_out",
    )(x2, g, sc, sh, ga, w_merge, misc, pg, o_cmp, o_sel, o_win, o_b, o_c, o_d, wa, wb, wc, wd, wo)


def _ffn_kernel(x_ref, g_ref, sc_ref, sh_ref, gf_ref, w1_ref, w3_ref, w2_ref, o_ref, h_scr, acc_scr):
    f = pl.program_id(1)

    @pl.when(f == 0)
    def _():
        h_scr[...] = _mod_norm(x_ref[...], g_ref[...], sc_ref[0], sh_ref[0]).astype(BF16)
        acc_scr[...] = jnp.zeros(acc_scr.shape, F32)

    h = h_scr[...]
    a = _dot(h, w1_ref[...])
    b = _dot(h, w3_ref[...])
    acc_scr[...] += _dot((a * jax.nn.sigmoid(a) * b).astype(BF16), w2_ref[...])

    @pl.when(f == pl.num_programs(1) - 1)
    def _():
        o_ref[...] = x_ref[...] + gf_ref[0] * acc_scr[...]


def _ffn(x2, g, sc, sh, gf, w1, w3, w2, seq, tt, tf):
    n, d = x2.shape
    dff = w1.shape[1]
    tpb = seq // tt
    per_b = pl.BlockSpec((1, 1, d), lambda i, f: (i // tpb, 0, 0))
    return pl.pallas_call(
        _ffn_kernel,
        out_shape=jax.ShapeDtypeStruct((n, d), F32),
        grid=(n // tt, dff // tf),
        in_specs=[
            pl.BlockSpec((tt, d), lambda i, f: (i, 0)),
            pl.BlockSpec((1, d), lambda i, f: (0, 0)),
            per_b, per_b, per_b,
            pl.BlockSpec((d, tf), lambda i, f: (0, f)),
            pl.BlockSpec((d, tf), lambda i, f: (0, f)),
            pl.BlockSpec((tf, d), lambda i, f: (f, 0)),
        ],
        out_specs=pl.BlockSpec((tt, d), lambda i, f: (i, 0)),
        scratch_shapes=[pltpu.VMEM((tt, d), BF16), pltpu.VMEM((tt, d), F32)],
        compiler_params=_params("arbitrary", "arbitrary"),
        name="ffn_swiglu",
    )(x2, g, sc, sh, gf, w1, w3, w2)


def _route_kernel(x_ref, g_ref, sc_ref, sh_ref, rw_ref, up_ref, h_ref, rank_ref, gate_ref, cnt_ref):
    hf = _mod_norm(x_ref[...], g_ref[...], sc_ref[0], sh_ref[0])
    h_ref[...] = hf.astype(BF16)
    logits = _dot(rw_ref[...], hf, NT, HI)
    ne, tt = logits.shape
    e_idx = lax.broadcasted_iota(jnp.int32, (ne, 1), 0).astype(F32)
    v1 = jnp.max(logits, axis=0, keepdims=True)
    i1 = jnp.min(jnp.where(logits == v1, e_idx, float(ne)), axis=0, keepdims=True)
    m1 = e_idx == i1
    rest = jnp.where(m1, -3e38, logits)
    v2 = jnp.max(rest, axis=0, keepdims=True)
    i2 = jnp.min(jnp.where(rest == v2, e_idx, float(ne)), axis=0, keepdims=True)
    m2 = e_idx == i2
    e2 = jnp.exp(v2 - v1)
    g1 = 1.0 / (1.0 + e2)
    g2 = e2 / (1.0 + e2)
    routed = m1 | m2
    rf = jnp.where(routed, 1.0, 0.0)
    rank = _dot(rf.astype(BF16), up_ref[...])
    rank = jnp.where(routed, rank, -1.0)
    gate = jnp.where(m1, g1, 0.0) + jnp.where(m2, g2, 0.0)
    for e in range(ne):
        rank_ref[0, e] = rank[e:e + 1, :]
        gate_ref[0, e] = gate[e:e + 1, :]
    cnt = jnp.sum(rf, axis=1, keepdims=True)
    cnt_ref[0] = jnp.broadcast_to(cnt, (ne, LANES))


def _route(x2, g, sc, sh, rw_t, seq, tt):
    n, d = x2.shape
    ne = rw_t.shape[0]
    tpb = seq // tt
    nt = n // tt
    upper = jnp.asarray(np.triu(np.ones((tt, tt), np.float32), 1), BF16)
    per_b = pl.BlockSpec((1, 1, d), lambda i: (i // tpb, 0, 0))
    return pl.pallas_call(
        _route_kernel,
        out_shape=[jax.ShapeDtypeStruct((n, d), BF16),
                   jax.ShapeDtypeStruct((nt, ne, 1, tt), F32),
                   jax.ShapeDtypeStruct((nt, ne, 1, tt), F32),
                   jax.ShapeDtypeStruct((nt, ne, LANES), F32)],
        grid=(nt,),
        in_specs=[
            pl.BlockSpec((tt, d), lambda i: (i, 0)),
            pl.BlockSpec((1, d), lambda i: (0, 0)),
            per_b, per_b,
            pl.BlockSpec((ne, d), lambda i: (0, 0)),
            pl.BlockSpec((tt, tt), lambda i: (0, 0)),
        ],
        out_specs=[pl.BlockSpec((tt, d), lambda i: (i, 0)),
                   pl.BlockSpec((1, ne, 1, tt), lambda i: (i, 0, 0, 0)),
                   pl.BlockSpec((1, ne, 1, tt), lambda i: (i, 0, 0, 0)),
                   pl.BlockSpec((1, ne, LANES), lambda i: (i, 0, 0))],
        compiler_params=_params("arbitrary"),
        name="moe_route",
    )(x2, g, sc, sh, rw_t, upper)


def _moe_kernel(cnt_ref, x_ref, gf_ref, h_ref, rank_ref, gate_ref, w1_ref, w3_ref, w2_ref,
                o_ref, acc_scr, xs_scr, y_scr, *, chunk):
    i, e, f = pl.program_id(0), pl.program_id(1), pl.program_id(2)
    ne, nf = pl.num_programs(1), pl.num_programs(2)

    @pl.when((e == 0) & (f == 0))
    def _():
        acc_scr[...] = jnp.zeros(acc_scr.shape, F32)

    count = cnt_ref[i * ne + e]
    rank = rank_ref[0, 0]
    gate = gate_ref[0, 0]
    n_small = (count + chunk - 1) // chunk
    n_big = (count + 2 * chunk - 1) // (2 * chunk)

    def one_hot(c, rows):
        r = c * rows + lax.broadcasted_iota(jnp.int32, (rows, 1), 0)
        return rank == r.astype(F32)

    def rows_of(c, rows):
        return pl.ds(pl.multiple_of(c * rows, rows), rows)

    @pl.when(f == 0)
    def _():
        h = h_ref[...]

        def gather(c, carry):
            p = jnp.where(one_hot(c, chunk), 1.0, 0.0).astype(BF16)
            xs_scr[rows_of(c, chunk), :] = _dot(p, h).astype(BF16)
            return carry

        lax.fori_loop(0, n_small, gather, 0)

        def clear(c, carry):
            y_scr[rows_of(c, 2 * chunk), :] = jnp.zeros((2 * chunk, y_scr.shape[1]), F32)
            return carry

        lax.fori_loop(0, n_big, clear, 0)

    def expert(c, carry):
        xs = xs_scr[rows_of(c, chunk), :]
        a = _dot(xs, w1_ref[0])
        b = _dot(xs, w3_ref[0])
        y_scr[rows_of(c, chunk), :] += _dot((a * jax.nn.sigmoid(a) * b).astype(BF16), w2_ref[0])
        return carry

    lax.fori_loop(0, n_small, expert, 0)

    @pl.when(f == nf - 1)
    def _():
        def scatter(c, carry):
            hit = one_hot(c, 2 * chunk)
            p = jnp.where(hit, 1.0, 0.0).astype(BF16)
            gcol = jnp.sum(jnp.where(hit, gate, 0.0), axis=-1, keepdims=True)
            acc_scr[...] += _dot(p, (y_scr[rows_of(c, 2 * chunk), :] * gcol).astype(BF16), TN)
            return carry

        lax.fori_loop(0, n_big, scatter, 0)

    @pl.when((e == ne - 1) & (f == nf - 1))
    def _():
        o_ref[...] = x_ref[...] + gf_ref[0] * acc_scr[...]


def _moe(counts, x2, gf, h2, rank, gate, w1, w3, w2, seq, tt, tf, chunk):
    n, d = x2.shape
    ne, _, dff = w1.shape
    tpb = seq // tt
    grid_spec = pltpu.PrefetchScalarGridSpec(
        num_scalar_prefetch=1,
        grid=(n // tt, ne, dff // tf),
        in_specs=[
            pl.BlockSpec((tt, d), lambda i, e, f, c: (i, 0)),
            pl.BlockSpec((1, 1, d), lambda i, e, f, c: (i // tpb, 0, 0)),
            pl.BlockSpec((tt, d), lambda i, e, f, c: (i, 0)),
            pl.BlockSpec((1, 1, 1, tt), lambda i, e, f, c: (i, e, 0, 0)),
            pl.BlockSpec((1, 1, 1, tt), lambda i, e, f, c: (i, e, 0, 0)),
            pl.BlockSpec((1, d, tf), lambda i, e, f, c: (e, 0, f)),
            pl.BlockSpec((1, d, tf), lambda i, e, f, c: (e, 0, f)),
            pl.BlockSpec((1, tf, d), lambda i, e, f, c: (e, f, 0)),
        ],
        out_specs=pl.BlockSpec((tt, d), lambda i, e, f, c: (i, 0)),
        scratch_shapes=[pltpu.VMEM((tt, d), F32), pltpu.VMEM((tt, d), BF16), pltpu.VMEM((tt, d), F32)],
    )
    return pl.pallas_call(
        functools.partial(_moe_kernel, chunk=chunk),
        out_shape=jax.ShapeDtypeStruct((n, d), F32),
        grid_spec=grid_spec,
        compiler_params=_params("arbitrary", "arbitrary", "arbitrary"),
        name="moe_experts",
    )(counts, x2, gf, h2, rank, gate, w1, w3, w2)


def _overlap_matrix(ncp, nc, n_sel):
    c0 = np.arange(ncp) * CMP_STRIDE
    c1 = c0 + CMP_LEN
    s0 = np.arange(n_sel) * SEL_LEN
    s1 = s0 + SEL_LEN
    ov = ((c0[:, None] < s1[None, :]) & (c1[:, None] > s0[None, :])).astype(np.float32)
    ov[nc:] = 0.0
    return jnp.asarray(ov.T)


def _gate_expand():
    pg = np.zeros((3, W_MIX, W_MIX), np.float32)
    for br in range(3):
        for h in range(N_HEADS):
            pg[br, GATE_LANE + 3 * h + br, h * HEAD_DIM:(h + 1) * HEAD_DIM] = 1.0
    return jnp.asarray(pg, BF16)


def _mixer_layer(x2, b, s, mod, norm_g, rope, w_in, qk_gain, pe_k, pe_v, ck1, ck2, cv1, cv2,
                 fox_b, w_branch, w_out):
    n, d = x2.shape
    sh_a, sc_a, g_a = mod[0], mod[1], mod[2]
    w_slab, gain, w_merge = _pack_w_in(w_in, qk_gain)
    bd = jnp.asarray(np.kron(np.eye(N_HEADS), np.full((HEAD_DIM, HEAD_DIM), 1.0 / HEAD_DIM)), BF16)
    outs = _proj(x2, norm_g, sc_a, sh_a, w_slab, gain, bd, rope, s, min(512, s))
    sl = [a.reshape(b, s, W_MIX) for a in outs[:N_SLABS]]
    dil_in = {1: (sl[S_BQ], sl[S_BK], sl[S_BV])}
    for di, dil in enumerate(_DILATIONS):
        dil_in[dil] = tuple(outs[N_SLABS + si * len(_DILATIONS) + di].reshape(b, s // dil, dil * W_MIX)
                            for si in range(len(_DIL_SLABS)))
    misc = sl[S_MISC]

    nch = s // CMP_STRIDE
    nc = nch - CMP_LEN // CMP_STRIDE + 1
    kc_raw = sl[S_KC][..., :HEAD_DIM]
    vc_raw = misc[..., VC_LANE:VC_LANE + HEAD_DIM]
    chunks = jnp.stack([kc_raw, vc_raw]).reshape(2, b, nch, CMP_STRIDE * HEAD_DIM)
    chunks_next = jnp.concatenate([chunks[:, :, 1:], jnp.zeros_like(chunks[:, :, :1])], axis=2)
    pe = jnp.stack([pe_k, pe_v]).reshape(2, 1, CMP_LEN * HEAD_DIM)
    kvc = jnp.tile(_compress(chunks, chunks_next, pe, jnp.stack([ck1, cv1]), jnp.stack([ck2, cv2])),
                   (1, 1, 1, N_HEADS))
    overlap = _overlap_matrix(nch, nc, s // SEL_LEN)
    o_cmp, selmask = _cmp_topk(sl[S_QNR], kvc[0], kvc[1].transpose(0, 2, 1).astype(BF16), overlap,
                               min(256, s), nc)
    tq, tk = min(256, s), min(512, s)
    o_sel = _causal_attn(sl[S_QR], sl[S_KSL], sl[S_VSL], tq, tk, sel=selmask)
    o_win = _window_attn(sl[S_QR], sl[S_KW], sl[S_VW], dil=1, window=NSA_WINDOW, tq=256, out_dtype=BF16)

    others = []
    for (wdw, dil) in DIL_CONFIGS[:0:-1]:
        others.append(_window_attn(*dil_in[dil], dil=dil, window=wdw // dil + 1, tq=256,
                                   out_dtype=F32, emit_lse=True))
    wdw, dil = DIL_CONFIGS[0]
    o_b = _window_attn(*dil_in[dil], dil=dil, window=wdw // dil + 1, tq=256, out_dtype=BF16, others=others)

    o_c = _stick_breaking(sl[S_CQ], sl[S_CK], sl[S_CV], min(256, s))

    bias_vec = jnp.zeros((1, W_MIX), F32).at[0, FOX_LANE:FOX_LANE + N_HEADS].set(fox_b)
    fcum = _fox_cumsum(misc, bias_vec, min(512, s))
    qk_bound = 1.02 * HEAD_DIM ** 0.5 * jnp.max(jnp.abs(qk_gain[6])) * jnp.max(jnp.abs(qk_gain[7])) + 0.05
    o_d = _causal_attn(sl[S_DQ], sl[S_DK], sl[S_DV], tq, tk, fcum=fcum[..., FOX_LANE:FOX_LANE + N_HEADS],
                       qk_bound=qk_bound)

    wb16 = w_branch.astype(BF16)
    flat = lambda a: a.reshape(n, W_MIX)
    return _merge(x2, norm_g, sc_a, sh_a, g_a, w_merge, flat(misc), _gate_expand(),
                  flat(o_cmp), flat(o_sel), flat(o_win), flat(o_b), flat(o_c), flat(o_d),
                  wb16[0], wb16[1], wb16[2], wb16[3], w_out.astype(BF16), s, min(512, s))


def kernel(x, c, positions, w_ada, b_ada, norm_mix, norm_ffn, w_in, qk_gain, nsa_pe_k, nsa_pe_v,
           nsa_ck_w1, nsa_ck_w2, nsa_cv_w1, nsa_cv_w2, fox_bias, w_branch, w_out,
           ffn_w1, ffn_w3, ffn_w2, router_w, moe_w1, moe_w3, moe_w2):
    b, s, d = x.shape
    depth = w_ada.shape[0]
    rope = _rope_tables(positions)
    mods = _ada(c, w_ada, b_ada).reshape(depth, b, 6, 1, d).transpose(0, 2, 1, 3, 4)
    x2 = x.reshape(b * s, d)
    for l in range(depth):
        mod = mods[l]
        x2 = _mixer_layer(x2, b, s, mod[0:3], norm_mix[l].reshape(1, d), rope, w_in[l], qk_gain[l],
                          nsa_pe_k[l], nsa_pe_v[l], nsa_ck_w1[l], nsa_ck_w2[l], nsa_cv_w1[l],
                          nsa_cv_w2[l], fox_bias[l], w_branch[l], w_out[l])
        sh_f, sc_f, g_f = mod[3], mod[4], mod[5]
        gn = norm_ffn[l].reshape(1, d)
        e = l // 2
        if l % 2 == 0:
            dff = ffn_w1.shape[2]
            x2 = _ffn(x2, gn, sc_f, sh_f, g_f, ffn_w1[e].astype(BF16), ffn_w3[e].astype(BF16),
                      ffn_w2[e].astype(BF16), s, min(1024, s), dff // 2)
        else:
            tt = min(1024, s)
            dff = moe_w1.shape[3]
            h2, rank, gate, cnt = _route(x2, gn, sc_f, sh_f, router_w[e].T, s, tt)
            counts = cnt[:, :, 0].astype(jnp.int32).reshape(-1)
            x2 = _moe(counts, x2, g_f, h2, rank, gate, moe_w1[e].astype(BF16), moe_w3[e].astype(BF16),
                      moe_w2[e].astype(BF16), s, tt, dff // 2, 128)
    return x2.reshape(b, s, d)
```

```python
import functools

import numpy as np
import jax
import jax.numpy as jnp
from jax import lax
from jax.experimental import pallas as pl
from jax.experimental.pallas import tpu as pltpu

F32 = jnp.float32
BF16 = jnp.bfloat16
HI = lax.Precision.HIGHEST

LANES = 128
VMEM_LIMIT = 52 * 1024 * 1024

HEAD_DIM = 64
HEAD_SHIFT = 6
N_HEADS = 4
W_MIX = N_HEADS * HEAD_DIM
N_MIXERS = 4
ROPE_THETA = 500000.0
ROPE_DIMS = HEAD_DIM // 4
ROPE_HALF = ROPE_DIMS // 2
EPS = 1e-6
LOG2E = 1.4426950408889634
NEG = -1e30
BIG = 1e30
CMP_LEN = 32
CMP_STRIDE = 16
SEL_LEN = 64
SEL_SHIFT = 6
TOPN = 16
SEL_NEG = 2.0 ** 30
NSA_WINDOW = 512
DIL_CONFIGS = ((128, 1), (512, 4), (2048, 16))
SB_STOP = -110.0
FOX_STOP = 108.0
GATE_LANE = 0
FOX_LANE = 3 * N_HEADS
VC_LANE = HEAD_DIM

NN = (((1,), (0,)), ((), ()))
NT = (((1,), (1,)), ((), ()))
TN = (((0,), (0,)), ((), ()))


def _dot(a, b, dims=NN, precision=None):
    return lax.dot_general(a, b, dims, precision=precision, preferred_element_type=F32)


def _dot_split(a, b_bf16, dims=NN):
    hi = a.astype(BF16)
    lo = (a - hi.astype(F32)).astype(BF16)
    return _dot(hi, b_bf16, dims) + _dot(lo, b_bf16, dims)


def _params(*sem):
    return pltpu.CompilerParams(dimension_semantics=sem, vmem_limit_bytes=VMEM_LIMIT)


def _mod_norm(x, g, sc, sh):
    ms = jnp.mean(x * x, axis=-1, keepdims=True)
    return (x * lax.rsqrt(ms + EPS) * g) * (1.0 + sc) + sh


def _head_masks():
    lane = lax.broadcasted_iota(jnp.int32, (1, W_MIX), 1)
    return [(lane >> HEAD_SHIFT) == h for h in range(N_HEADS)]


def _stack_masked(q, qm_scr, tq):
    for h, hm in enumerate(_head_masks()):
        qm_scr[h * tq:(h + 1) * tq, :] = jnp.where(hm, q, jnp.zeros_like(q))


def _pick_heads(stacked, tq, scale=None):
    out = None
    for h, hm in enumerate(_head_masks()):
        blk = stacked[h * tq:(h + 1) * tq, :]
        if scale is not None:
            blk = blk * scale[h]
        out = jnp.where(hm, blk, 0.0 if out is None else out)
    return out


def _ada_kernel(c_ref, w_ref, b_ref, o_ref):
    c = c_ref[...]
    ca = c * jax.nn.sigmoid(c)
    o_ref[0] = _dot(ca, w_ref[0], NN, HI) + b_ref[0]


def _ada(c, w_ada, b_ada):
    depth, d, n6 = w_ada.shape
    b = c.shape[0]
    tn = n6 // 4
    return pl.pallas_call(
        _ada_kernel,
        out_shape=jax.ShapeDtypeStruct((depth, b, n6), F32),
        grid=(depth, n6 // tn),
        in_specs=[
            pl.BlockSpec((b, d), lambda l, j: (0, 0)),
            pl.BlockSpec((1, d, tn), lambda l, j: (l, 0, j)),
            pl.BlockSpec((1, 1, tn), lambda l, j: (l, 0, j)),
        ],
        out_specs=pl.BlockSpec((1, b, tn), lambda l, j: (l, 0, j)),
        compiler_params=_params("arbitrary", "arbitrary"),
        name="ada_mod",
    )(c, w_ada, b_ada.reshape(depth, 1, n6))


(W_QA, W_KC, W_KV, W_BQ, W_BK, W_BV, W_CQ, W_CK, W_CV, W_DQ, W_DK, W_DV, W_MISC) = range(13)
N_WSLABS = 13
KV_QK_LANES = 2 * HEAD_DIM
_SLABS = (
    (W_QA, "all", None, F32), (W_QA, "all", "all", BF16), (W_KC, "all", None, F32),
    (W_KV, "kv", "kv", BF16),
    (W_BQ, "all", "all", BF16), (W_BK, "all", "all", BF16), (W_BV, None, None, BF16),
    (W_CQ, None, None, BF16), (W_CK, None, None, BF16), (W_CV, None, None, BF16),
    (W_DQ, "all", None, BF16), (W_DK, "all", None, BF16), (W_DV, None, None, BF16),
    (W_MISC, None, None, F32),
)
N_SLABS = len(_SLABS)
(S_QNR, S_QR, S_KC, S_KV, S_BQ, S_BK, S_BV, S_CQ, S_CK, S_CV, S_DQ, S_DK, S_DV, S_MISC) = range(N_SLABS)
_DIL_SLABS = (S_BQ, S_BK, S_BV)
_DILATIONS = tuple(d for (_, d) in DIL_CONFIGS if d > 1)
KSL_LANE, KW_LANE, VSL_LANE, VW_LANE = (i * HEAD_DIM for i in range(4))


def _proj_kernel(x_ref, g_ref, sc_ref, sh_ref, w_ref, gain_ref, bd_ref, rope_ref, *out_refs):
    h = _mod_norm(x_ref[...], g_ref[...], sc_ref[0], sh_ref[0]).astype(BF16)
    bd = bd_ref[...]
    key_lanes = lax.broadcasted_iota(jnp.int32, (1, W_MIX), 1) < KV_QK_LANES
    last_w, y_n = None, None
    for s, (wi, norm, rope, _) in enumerate(_SLABS):
        if wi != last_w:
            y_n = _dot(h, w_ref[:, wi * W_MIX:(wi + 1) * W_MIX])
            if norm:
                normed = y_n * lax.rsqrt(_dot_split(y_n * y_n, bd) + EPS)
                y_n = normed if norm == "all" else jnp.where(key_lanes, normed, y_n)
            last_w = wi
        y = y_n * gain_ref[s]
        if rope:
            c, s1, s2 = rope_ref[0, 0], rope_ref[0, 1], rope_ref[0, 2]
            if rope == "kv":
                c, s1, s2 = jnp.where(key_lanes, c, 1.0), jnp.where(key_lanes, s1, 0.0), jnp.where(key_lanes, s2, 0.0)
            y = y * c + pltpu.roll(y, W_MIX - ROPE_HALF, 1) * s1 + pltpu.roll(y, ROPE_HALF, 1) * s2
        out_refs[s][...] = y.astype(out_refs[s].dtype)
        if s in _DIL_SLABS:
            stage_scr = out_refs[-1]
            tt = y.shape[0]
            for c in range(W_MIX // LANES):
                stage_scr[c] = y[:, c * LANES:(c + 1) * LANES]
            for di, dil in enumerate(_DILATIONS):
                o_ref = out_refs[N_SLABS + _DIL_SLABS.index(s) * len(_DILATIONS) + di]
                for r in range(dil):
                    for c in range(W_MIX // LANES):
                        lo = r * W_MIX + c * LANES
                        o_ref[:, lo:lo + LANES] = (
                            stage_scr[c, pl.ds(r, tt // dil, stride=dil), :].astype(o_ref.dtype))


def _proj(x2, g, sc, sh, w_slab, gain, bd, rope, seq, tt):
    n, d = x2.shape
    tpb = seq // tt
    out_shape = [jax.ShapeDtypeStruct((n, W_MIX), dt) for (_, _, _, dt) in _SLABS]
    out_specs = [pl.BlockSpec((tt, W_MIX), lambda i: (i, 0)) for _ in _SLABS]
    for _ in _DIL_SLABS:
        for dil in _DILATIONS:
            out_shape.append(jax.ShapeDtypeStruct((n // dil, dil * W_MIX), BF16))
            out_specs.append(pl.BlockSpec((tt // dil, dil * W_MIX), lambda i: (i, 0)))
    return pl.pallas_call(
        _proj_kernel,
        out_shape=out_shape,
        grid=(n // tt,),
        in_specs=[
            pl.BlockSpec((tt, d), lambda i: (i, 0)),
            pl.BlockSpec((1, d), lambda i: (0, 0)),
            pl.BlockSpec((1, 1, d), lambda i: (i // tpb, 0, 0)),
            pl.BlockSpec((1, 1, d), lambda i: (i // tpb, 0, 0)),
            pl.BlockSpec((d, N_WSLABS * W_MIX), lambda i: (0, 0)),
            pl.BlockSpec((N_SLABS, 1, W_MIX), lambda i: (0, 0, 0)),
            pl.BlockSpec((W_MIX, W_MIX), lambda i: (0, 0)),
            pl.BlockSpec((1, 3, tt, W_MIX), lambda i: (i // tpb, 0, i % tpb, 0)),
        ],
        out_specs=out_specs,
        scratch_shapes=[pltpu.VMEM((W_MIX // LANES, tt, LANES), F32)],
        compiler_params=_params("arbitrary"),
        name="in_proj",
    )(x2, g, sc, sh, w_slab, gain, bd, rope)


def _pack_w_in(w_in, qk_gain):
    d = w_in.shape[0]
    o = 0
    a_q = w_in[:, o:o + W_MIX]; o += W_MIX
    a_kv = w_in[:, o:o + 6 * HEAD_DIM]; o += 6 * HEAD_DIM
    a_g = w_in[:, o:o + 3 * N_HEADS]; o += 3 * N_HEADS
    b_qkv = w_in[:, o:o + 3 * W_MIX]; o += 3 * W_MIX
    c_qkv = w_in[:, o:o + 3 * W_MIX]; o += 3 * W_MIX
    d_qkv = w_in[:, o:o + 3 * W_MIX]; o += 3 * W_MIX
    d_f = w_in[:, o:o + N_HEADS]; o += N_HEADS
    w_merge = w_in[:, o:]
    kc, vc, ksl, vsl, kw, vw = (a_kv[:, i * HEAD_DIM:(i + 1) * HEAD_DIM] for i in range(6))
    zeros = lambda w: jnp.zeros((d, w), w_in.dtype)
    misc = jnp.concatenate([a_g, d_f, zeros(VC_LANE - FOX_LANE - N_HEADS), vc, zeros(W_MIX - 2 * HEAD_DIM)], axis=1)
    slabs = [
        a_q, jnp.concatenate([kc, zeros(W_MIX - HEAD_DIM)], axis=1),
        jnp.concatenate([ksl, kw, vsl, vw], axis=1),
        b_qkv[:, :W_MIX], b_qkv[:, W_MIX:2 * W_MIX], b_qkv[:, 2 * W_MIX:],
        c_qkv[:, :W_MIX], c_qkv[:, W_MIX:2 * W_MIX], c_qkv[:, 2 * W_MIX:],
        d_qkv[:, :W_MIX], d_qkv[:, W_MIX:2 * W_MIX], d_qkv[:, 2 * W_MIX:],
        misc,
    ]
    w_slab = jnp.concatenate(slabs, axis=1).astype(BF16)
    scale = HEAD_DIM ** -0.5
    t4 = lambda gvec: jnp.tile(gvec, N_HEADS)
    one = jnp.ones((W_MIX,), F32)
    scale2 = scale * LOG2E
    gains = [
        t4(qk_gain[0]) * scale, t4(qk_gain[0]) * scale2,
        jnp.concatenate([qk_gain[1], jnp.ones((W_MIX - HEAD_DIM,), F32)]),
        jnp.concatenate([qk_gain[2], qk_gain[3], jnp.ones((W_MIX - KV_QK_LANES,), F32)]),
        t4(qk_gain[4]) * scale2, t4(qk_gain[5]), one,
        one * scale, one, one,
        t4(qk_gain[6]) * scale2, t4(qk_gain[7]), one,
        one,
    ]
    gain = jnp.stack(gains).reshape(N_SLABS, 1, W_MIX).astype(F32)
    return w_slab, gain, w_merge.astype(BF16)


def _rope_tables(positions):
    inv = ROPE_THETA ** (-jnp.arange(0, ROPE_DIMS, 2, dtype=F32) / ROPE_DIMS)
    ang = positions.astype(F32)[..., None] * inv
    cos, sin = jnp.cos(ang), jnp.sin(ang)
    b, s, _ = cos.shape
    pad1 = jnp.ones((b, s, HEAD_DIM - ROPE_DIMS), F32)
    pad0 = jnp.zeros((b, s, HEAD_DIM - ROPE_DIMS), F32)
    z8 = jnp.zeros_like(sin)
    c64 = jnp.concatenate([cos, cos, pad1], axis=-1)
    s1_64 = jnp.concatenate([-sin, z8, pad0], axis=-1)
    s2_64 = jnp.concatenate([z8, sin, pad0], axis=-1)
    heads = lambda t: jnp.tile(t, (1, 1, N_HEADS))
    return jnp.stack([heads(c64), heads(s1_64), heads(s2_64)], axis=1)


def _compress_kernel(a_ref, b_ref, pe_ref, w1_ref, w2_ref, o_ref):
    half = w1_ref.shape[1] // 2
    w1 = w1_ref[0]
    hid = (_dot(a_ref[0, 0], w1[:half], NN, HI) + _dot(b_ref[0, 0], w1[half:], NN, HI)
           + _dot(pe_ref[0], w1, NN, HI))
    o_ref[0, 0] = _dot(jax.nn.gelu(hid), w2_ref[0], NN, HI)


def _compress(ch, chn, pe, w1, w2):
    _, b, ncp, cw = ch.shape
    hid = w1.shape[2]
    return pl.pallas_call(
        _compress_kernel,
        out_shape=jax.ShapeDtypeStruct((2, b, ncp, HEAD_DIM), F32),
        grid=(2, b),
        in_specs=[
            pl.BlockSpec((1, 1, ncp, cw), lambda k, i: (k, i, 0, 0)),
            pl.BlockSpec((1, 1, ncp, cw), lambda k, i: (k, i, 0, 0)),
            pl.BlockSpec((1, 1, 2 * cw), lambda k, i: (k, 0, 0)),
            pl.BlockSpec((1, 2 * cw, hid), lambda k, i: (k, 0, 0)),
            pl.BlockSpec((1, hid, HEAD_DIM), lambda k, i: (k, 0, 0)),
        ],
        out_specs=pl.BlockSpec((1, 1, ncp, HEAD_DIM), lambda k, i: (k, i, 0, 0)),
        compiler_params=_params("arbitrary", "arbitrary"),
        name="nsa_compress",
    )(ch, chn, pe, w1, w2)


def _cmp_kernel(q_ref, kc_ref, vct_ref, ovt_ref, o_ref, sel_ref, qh_scr, ql_scr, acc_scr, *, tq, nc, n_sel):
    i = pl.program_id(1)
    ncp = kc_ref.shape[1]
    q = q_ref[0]
    q_hi = q.astype(BF16)
    _stack_masked(q_hi, qh_scr, tq)
    _stack_masked((q - q_hi.astype(F32)).astype(BF16), ql_scr, tq)
    kc = kc_ref[0]
    k_hi = kc.astype(BF16)
    k_lo = (kc - k_hi.astype(F32)).astype(BF16)
    ss = []
    for h in range(N_HEADS):
        rs = slice(h * tq, (h + 1) * tq)
        ss.append(_dot(k_hi, qh_scr[rs, :], NT) + _dot(k_lo, qh_scr[rs, :], NT)
                  + _dot(k_hi, ql_scr[rs, :], NT))
    t = i * tq + lax.broadcasted_iota(jnp.int32, (1, tq), 1)
    c = lax.broadcasted_iota(jnp.int32, (ncp, 1), 0)
    mask = (c * CMP_STRIDE + (CMP_LEN - 1) <= t) & (c < nc)
    psum = None
    ps = []
    for h in range(N_HEADS):
        sm = jnp.where(mask, ss[h], NEG)
        m = jnp.max(sm, axis=0, keepdims=True)
        e = jnp.where(mask, jnp.exp(sm - m), 0.0)
        l = jnp.sum(e, axis=0, keepdims=True)
        p = e / jnp.maximum(l, 1e-30)
        psum = p if psum is None else psum + p
        ps.append(p.astype(BF16))
    vct = vct_ref[0]
    for h in range(N_HEADS):
        rs = slice(h * HEAD_DIM, (h + 1) * HEAD_DIM)
        acc_scr[rs, :] = _dot(vct[rs, :], ps[h])
    o_ref[0] = acc_scr[...].T.astype(o_ref.dtype)
    imp = _dot(ovt_ref[...], psum, NN, HI)
    j = lax.broadcasted_iota(jnp.int32, (n_sel, 1), 0)
    cur = t >> SEL_SHIFT
    valid = j <= cur
    forced = (j == 0) | (j == cur) | (j == cur - 1)
    score = jnp.where(valid, jnp.where(forced, BIG, imp), NEG)
    sel = jnp.zeros((n_sel, tq), F32)
    jf = j.astype(F32)
    for _ in range(min(TOPN, n_sel)):
        mx = jnp.max(score, axis=0, keepdims=True)
        idx = jnp.min(jnp.where(score == mx, jf, float(n_sel)), axis=0, keepdims=True)
        pick = jf == idx
        sel = jnp.where(pick, 1.0, sel)
        score = jnp.where(pick, -3e38, score)
    sel_ref[0] = ((sel - 1.0) * SEL_NEG).T.astype(sel_ref.dtype)


def _cmp_topk(q_nr, kc_rep, vc_rep, overlap, tq, nc):
    b, seq, _ = q_nr.shape
    ncp = kc_rep.shape[1]
    n_sel = seq // SEL_LEN
    return pl.pallas_call(
        functools.partial(_cmp_kernel, tq=tq, nc=nc, n_sel=n_sel),
        out_shape=[jax.ShapeDtypeStruct((b, seq, W_MIX), BF16),
                   jax.ShapeDtypeStruct((b, seq, n_sel), BF16)],
        grid=(b, seq // tq),
        in_specs=[
            pl.BlockSpec((1, tq, W_MIX), lambda g, i: (g, i, 0)),
            pl.BlockSpec((1, ncp, W_MIX), lambda g, i: (g, 0, 0)),
            pl.BlockSpec((1, W_MIX, ncp), lambda g, i: (g, 0, 0)),
            pl.BlockSpec((n_sel, ncp), lambda g, i: (0, 0)),
        ],
        out_specs=[pl.BlockSpec((1, tq, W_MIX), lambda g, i: (g, i, 0)),
                   pl.BlockSpec((1, tq, n_sel), lambda g, i: (g, i, 0))],
        scratch_shapes=[pltpu.VMEM((N_HEADS * tq, W_MIX), BF16), pltpu.VMEM((N_HEADS * tq, W_MIX), BF16),
                        pltpu.VMEM((W_MIX, tq), F32)],
        compiler_params=_params("arbitrary", "arbitrary"),
        name="nsa_cmp_topk",
    )(q_nr, kc_rep, vc_rep, overlap)


def _causal_kernel(*refs, tq, tk, has_sel, has_bias):
    it = iter(refs)
    q_ref, k_ref, vt_ref = next(it), next(it), next(it)
    sel_ref = next(it) if has_sel else None
    fa_ref, fb_ref, fq_ref, thr_ref = (next(it) for _ in range(4)) if has_bias else (None,) * 4
    o_ref, qm_scr, acc_scr, sa_scr, sb_scr, m_scr, l_scr = (next(it) for _ in range(7))
    i = pl.program_id(1)
    t0 = i * tq
    lane = lax.broadcasted_iota(jnp.int32, (1, W_MIX), 1)
    if has_sel:
        n_sel = sel_ref.shape[2]
        qf = q_ref[0].astype(F32)
        selb = sel_ref[0].astype(F32)
        selb = jnp.concatenate([selb, jnp.zeros((tq, W_MIX - n_sel), F32)], axis=1)
        selb = pltpu.roll(selb, HEAD_DIM, 1)
        for h in range(N_HEADS):
            rot = qf if h == 0 else pltpu.roll(qf, W_MIX - h * HEAD_DIM, 1)
            qm_scr[h * tq:(h + 1) * tq, :] = jnp.where(lane < HEAD_DIM, rot, selb).astype(BF16)
    else:
        _stack_masked(q_ref[0], qm_scr, tq)
    acc_scr[...] = jnp.zeros(acc_scr.shape, F32)
    t_pos = t0 + lax.broadcasted_iota(jnp.int32, (1, tq), 1)

    def scores(kt, s_buf):
        k_t = k_ref[0, kt]
        if has_sel:
            blk = (kt * tk + lax.broadcasted_iota(jnp.int32, (tk, 1), 0)) >> SEL_SHIFT
            one_hot = jnp.where(blk == lane - HEAD_DIM, 1.0, 0.0).astype(BF16)
            k_t = jnp.where(lane < HEAD_DIM, k_t, one_hot)
        for h in range(N_HEADS):
            s = _dot(k_t, qm_scr[h * tq:(h + 1) * tq, :], NT)
            if has_bias:
                s = s + _dot(fa_ref[0, kt], fb_ref[0, h])
            s_buf[h] = s

    def update(kt, s_buf, diag):
        vt_t = vt_ref[0, kt]
        mask = None
        if diag:
            s_pos = kt * tk + lax.broadcasted_iota(jnp.int32, (tk, 1), 0)
            mask = s_pos <= t_pos
        ps, alphas = [], []
        for h in range(N_HEADS):
            s = s_buf[h]
            if mask is not None:
                s = jnp.where(mask, s, NEG)
            m_old = m_scr[h]
            m_new = jnp.maximum(m_old, jnp.max(s, axis=0, keepdims=True))
            p = jnp.exp2(s - m_new)
            alpha = jnp.exp2(m_old - m_new)
            m_scr[h] = m_new
            l_scr[h] = alpha * l_scr[h] + jnp.sum(p, axis=0, keepdims=True)
            ps.append(p.astype(BF16))
            alphas.append(alpha)
        for h in range(N_HEADS):
            rs = slice(h * HEAD_DIM, (h + 1) * HEAD_DIM)
            vs = slice(VSL_LANE, VSL_LANE + HEAD_DIM) if has_sel else rs
            acc_scr[rs, :] = alphas[h] * acc_scr[rs, :] + _dot(vt_t[vs, :], ps[h])

    def live(kt_next):
        lane = lax.broadcasted_iota(jnp.int32, (1, thr_ref.shape[3]), 1)
        hit = None
        for h in range(N_HEADS):
            top = jnp.max(fq_ref[0, h] - m_scr[h], axis=-1, keepdims=True)
            need = top >= thr_ref[0, h]
            hit = need if hit is None else (hit | need)
        return jnp.max(jnp.where(hit & (lane == kt_next), 1.0, 0.0)) > 0.5

    n_last = t0 // tk
    m_scr[...] = jnp.full(m_scr.shape, NEG, F32)
    l_scr[...] = jnp.zeros(l_scr.shape, F32)
    scores(n_last, sa_scr)
    scores(jnp.maximum(n_last - 1, 0), sb_scr)
    update(n_last, sa_scr, True)

    def pair(j):
        kt = n_last - 1 - 2 * j
        scores(kt - 1, sa_scr)
        update(kt, sb_scr, False)
        scores(jnp.maximum(kt - 2, 0), sb_scr)
        update(kt - 1, sa_scr, False)
        return kt - 2

    n_pairs = n_last // 2
    if has_bias:
        def cond(state):
            j, go = state
            return (j < n_pairs) & go

        def body(state):
            j, _ = state
            return j + 1, live(pair(j))

        _, go = lax.while_loop(cond, body, (jnp.int32(0), live(n_last - 1)))
    else:
        lax.fori_loop(0, n_pairs, lambda j, c: (pair(j), c)[1], 0)
        go = True

    @pl.when((n_last % 2 == 1) & go)
    def _():
        update(0, sb_scr, False)

    ls = [l_scr[h] for h in range(N_HEADS)]
    for h in range(N_HEADS):
        rs = slice(h * HEAD_DIM, (h + 1) * HEAD_DIM)
        acc_scr[rs, :] = acc_scr[rs, :] / ls[h]
    o_ref[0] = acc_scr[...].T.astype(o_ref.dtype)


def _split3(x):
    def cut(v):
        bits = lax.bitcast_convert_type(v, jnp.uint32) & jnp.uint32(0xFFFF0000)
        return lax.bitcast_convert_type(bits, F32)

    hi = cut(x)
    r1 = x - hi
    mid = cut(r1)
    lo = r1 - mid
    return hi.astype(BF16), mid.astype(BF16), lo.astype(BF16)


def _causal_attn(q, k, v, tq, tk, sel=None, fcum=None, qk_bound=None):
    b, s, _ = q.shape
    nk = s // tk
    rows = N_HEADS * tq
    n_s = N_HEADS
    vt = v.reshape(b, nk, tk, W_MIX).transpose(0, 1, 3, 2)
    args = [q, k.reshape(b, nk, tk, W_MIX), vt]
    in_specs = [
        pl.BlockSpec((1, tq, W_MIX), lambda a, i: (a, i, 0)),
        pl.BlockSpec((1, nk, tk, W_MIX), lambda a, i: (a, 0, 0, 0)),
        pl.BlockSpec((1, nk, W_MIX, tk), lambda a, i: (a, 0, 0, 0)),
    ]
    if sel is not None:
        args.append(sel)
        in_specs.append(pl.BlockSpec((1, tq, sel.shape[2]), lambda a, i: (a, i, 0)))
    if fcum is not None:
        fcum = fcum * LOG2E
        qk_bound = qk_bound * LOG2E
        nf = 2 * 3 + 2
        ones = jnp.ones(fcum.shape, BF16)
        zero = jnp.zeros(fcum.shape, BF16)
        parts = _split3(fcum)
        key_f = jnp.stack([ones, ones, ones] + [-p for p in parts] + [zero, zero], axis=-1)
        qry_f = jnp.stack(list(parts) + [ones, ones, ones, zero, zero], axis=-1)
        fa = key_f.reshape(b, nk, tk, N_HEADS * nf)
        eye = jnp.eye(N_HEADS, dtype=BF16)
        fb = jnp.einsum('bshf,hg->bgshf', qry_f, eye).reshape(b, N_HEADS, s, N_HEADS * nf).transpose(0, 1, 3, 2)
        f_rows = fcum.transpose(0, 2, 1)
        f_end = f_rows[:, :, tk - 1::tk]
        thr = jnp.pad(f_end - (qk_bound + FOX_STOP * LOG2E), ((0, 0), (0, 0), (0, LANES - nk)),
                      constant_values=BIG).reshape(b, N_HEADS, 1, LANES)
        args += [fa, fb, f_rows.reshape(b, N_HEADS, 1, s), thr]
        in_specs += [pl.BlockSpec((1, nk, tk, N_HEADS * nf), lambda a, i: (a, 0, 0, 0)),
                     pl.BlockSpec((1, N_HEADS, N_HEADS * nf, tq), lambda a, i: (a, 0, 0, i)),
                     pl.BlockSpec((1, N_HEADS, 1, tq), lambda a, i: (a, 0, 0, i)),
                     pl.BlockSpec((1, N_HEADS, 1, LANES), lambda a, i: (a, 0, 0, 0))]
    return pl.pallas_call(
        functools.partial(_causal_kernel, tq=tq, tk=tk, has_sel=sel is not None, has_bias=fcum is not None),
        out_shape=jax.ShapeDtypeStruct((b, s, W_MIX), BF16),
        grid=(b, s // tq),
        in_specs=in_specs,
        out_specs=pl.BlockSpec((1, tq, W_MIX), lambda a, i: (a, i, 0)),
        scratch_shapes=[pltpu.VMEM((rows, W_MIX), BF16), pltpu.VMEM((W_MIX, tq), F32),
                        pltpu.VMEM((n_s, tk, tq), F32), pltpu.VMEM((n_s, tk, tq), F32),
                        pltpu.VMEM((N_HEADS, 1, tq), F32), pltpu.VMEM((N_HEADS, 1, tq), F32)],
        compiler_params=_params("arbitrary", "arbitrary"),
        name="causal_sel%d_bias%d" % (sel is not None, fcum is not None),
    )(*args)


def _window_kernel(*refs, tq, wk, pad, window, ls, emit_lse, other_dils, kv_lanes):
    it = iter(refs)
    q_ref, k_ref, v_ref = next(it), next(it), next(it)
    n_other = len(other_dils)
    others_in = [(next(it), next(it)) for _ in range(n_other)]
    wexp_ref = next(it) if n_other else None
    o_ref = next(it)
    lse_ref = next(it) if emit_lse else None
    qm_scr, acc_scr = next(it), next(it)
    others = []
    if n_other:
        og_scr, lg_scr = next(it), next(it)
        for g, (dg, (og_ref, lg_ref)) in enumerate(zip(other_dils, others_in)):
            for r in range(dg):
                rows = pl.ds(r, tq // dg, stride=dg)
                for c in range(W_MIX // LANES):
                    lo = r * W_MIX + c * LANES
                    og_scr[g, c, rows, :] = og_ref[0, :, lo:lo + LANES]
                lg_scr[g, rows, :] = lg_ref[0, :, r * LANES:(r + 1) * LANES]
            others.append((og_scr.at[g], lg_scr.at[g]))
    t0 = pl.program_id(2) * tq
    start = pl.multiple_of(jnp.clip(t0 - pad, 0, ls - wk), LANES)
    if kv_lanes is None:
        _stack_masked(q_ref[0], qm_scr, tq)
    else:
        lane = lax.broadcasted_iota(jnp.int32, (1, W_MIX), 1)
        on_key = (lane >= kv_lanes[0]) & (lane < kv_lanes[0] + HEAD_DIM)
        qf = q_ref[0].astype(F32)
        for h in range(N_HEADS):
            shift = (kv_lanes[0] - h * HEAD_DIM) % W_MIX
            rot = pltpu.roll(qf, shift, 1) if shift else qf
            qm_scr[h * tq:(h + 1) * tq, :] = jnp.where(on_key, rot, 0.0).astype(BF16)
    k_w = k_ref[0, pl.ds(start, wk), :]
    vt_w = v_ref[0, pl.ds(start, wk), :].astype(F32).T.astype(BF16)
    ss = [_dot(k_w, qm_scr[h * tq:(h + 1) * tq, :], NT) for h in range(N_HEADS)]
    t_pos = t0 + lax.broadcasted_iota(jnp.int32, (1, tq), 1)
    s_pos = start + lax.broadcasted_iota(jnp.int32, (wk, 1), 0)
    mask = (s_pos <= t_pos) & (t_pos - s_pos < window)
    ps, inv_ls, lses = [], [], []
    for h in range(N_HEADS):
        s = jnp.where(mask, ss[h], NEG)
        m = jnp.max(s, axis=0, keepdims=True)
        p = jnp.exp2(s - m)
        l = jnp.sum(p, axis=0, keepdims=True)
        ps.append(p.astype(BF16))
        inv_ls.append(1.0 / l)
        lses.append(m + jnp.log2(l))
    for h in range(N_HEADS):
        rs = slice(h * HEAD_DIM, (h + 1) * HEAD_DIM)
        vs = rs if kv_lanes is None else slice(kv_lanes[1], kv_lanes[1] + HEAD_DIM)
        acc_scr[rs, :] = _dot(vt_w[vs, :], ps[h]) * inv_ls[h]
    o_self = acc_scr[...].T
    if emit_lse or n_other:
        row = lax.broadcasted_iota(jnp.int32, (LANES, 1), 0)
        stat = jnp.zeros((LANES, tq), F32)
        for h in range(N_HEADS):
            stat = jnp.where(row == h, lses[h], stat)
        lse_tile = stat.T
    if n_other:
        lg_t = [lg[...].T for (_, lg) in others]
        row = lax.broadcasted_iota(jnp.int32, (LANES, 1), 0)
        wmat = jnp.zeros((LANES, tq), F32)
        for h in range(N_HEADS):
            group_lse = [lses[h]] + [t[h:h + 1, :] for t in lg_t]
            top = functools.reduce(jnp.maximum, group_lse)
            ws = [jnp.exp2(x - top) for x in group_lse]
            inv_den = 1.0 / functools.reduce(jnp.add, ws)
            for g, w in enumerate(ws):
                wmat = jnp.where(row == g * N_HEADS + h, w * inv_den, wmat)
        wt = wmat.T
        groups = [o_self] + [jnp.concatenate([og[c] for c in range(W_MIX // LANES)], axis=1)
                             for (og, _) in others]
        out = None
        for g, o_g in enumerate(groups):
            term = _dot_split(wt, wexp_ref[g]) * o_g
            out = term if out is None else out + term
        o_ref[0] = out.astype(o_ref.dtype)
    else:
        o_ref[0] = o_self.astype(o_ref.dtype)
    if emit_lse:
        lse_ref[0] = lse_tile


def _window_attn(q, k, v, *, dil, window, tq, out_dtype, emit_lse=False, others=(), kv_lanes=None):
    b, ls, _ = q.shape
    tq = min(tq, ls)
    pad = -(-(window - 1) // LANES) * LANES
    wk = min(tq + pad, ls)
    rows = N_HEADS * tq
    args = [q, k, v]
    in_specs = [
        pl.BlockSpec((1, tq, W_MIX), lambda a, r, i: (a, i, r)),
        pl.BlockSpec((1, ls, W_MIX), lambda a, r, i: (a, 0, r)),
        pl.BlockSpec((1, ls, W_MIX), lambda a, r, i: (a, 0, r)),
    ]
    scratch = [pltpu.VMEM((rows, W_MIX), BF16), pltpu.VMEM((W_MIX, tq), F32)]
    for (o_g, lse_g, dg) in others:
        args += [o_g, lse_g]
        in_specs += [pl.BlockSpec((1, tq // dg, dg * W_MIX), lambda a, r, i: (a, i, 0)),
                     pl.BlockSpec((1, tq // dg, dg * LANES), lambda a, r, i: (a, i, 0))]
    if others:
        wexp = np.zeros((len(others) + 1, LANES, W_MIX), np.float32)
        for g in range(len(others) + 1):
            for h in range(N_HEADS):
                wexp[g, g * N_HEADS + h, h * HEAD_DIM:(h + 1) * HEAD_DIM] = 1.0
        args.append(jnp.asarray(wexp, BF16))
        in_specs.append(pl.BlockSpec(wexp.shape, lambda a, r, i: (0, 0, 0)))
        scratch += [pltpu.VMEM((len(others), W_MIX // LANES, tq, LANES), F32),
                    pltpu.VMEM((len(others), tq, LANES), F32)]
    out_shape = [jax.ShapeDtypeStruct((b, ls, dil * W_MIX), out_dtype)]
    out_specs = [pl.BlockSpec((1, tq, W_MIX), lambda a, r, i: (a, i, r))]
    if emit_lse:
        out_shape.append(jax.ShapeDtypeStruct((b, ls, dil * LANES), F32))
        out_specs.append(pl.BlockSpec((1, tq, LANES), lambda a, r, i: (a, i, r)))
    res = pl.pallas_call(
        functools.partial(_window_kernel, tq=tq, wk=wk, pad=pad, window=window, ls=ls,
                          emit_lse=emit_lse, other_dils=tuple(dg for (_, _, dg) in others),
                          kv_lanes=kv_lanes),
        out_shape=out_shape,
        grid=(b, dil, ls // tq),
        in_specs=in_specs,
        out_specs=out_specs,
        scratch_shapes=scratch,
        compiler_params=_params("arbitrary", "arbitrary", "arbitrary"),
        name="window_d%d_w%d" % (dil, window),
    )(*args)
    return (res[0], res[1], dil) if emit_lse else res[0]


def _sb_kernel(q_ref, k_ref, v_ref, tri_ref, o_ref, qm_scr, a_scr, carry_scr, acc_scr, *, tq):
    i = pl.program_id(1)
    t0 = i * tq
    rows = N_HEADS * tq
    _stack_masked(q_ref[0], qm_scr, tq)
    carry_scr[...] = jnp.zeros(carry_scr.shape, F32)
    acc_scr[...] = jnp.zeros(acc_scr.shape, F32)
    r = lax.broadcasted_iota(jnp.int32, (rows, 1), 0)
    t_pos = t0 + (r & (tq - 1))

    def tile(kt, diag):
        z = _dot(qm_scr[...], k_ref[0, kt], NT)
        lg = -(jnp.maximum(z, 0.0) + jnp.log(1.0 + jnp.exp(-jnp.abs(z))))
        if diag:
            s_pos = kt * tq + lax.broadcasted_iota(jnp.int32, (1, tq), 1)
            strict = s_pos < t_pos
            lg = jnp.where(strict, lg, 0.0)
        cum = _dot_split(lg, tri_ref[...]) + carry_scr[...]
        a = jnp.exp(z + cum)
        if diag:
            a = jnp.where(strict, a, 0.0)
        a_scr[...] = a.astype(BF16)
        acc_scr[...] += _dot(a_scr[...], v_ref[0, kt])
        carry_scr[...] += jnp.sum(lg, axis=-1, keepdims=True)

    tile(i, True)

    def cond(state):
        j, top = state
        return (j < i) & (top > SB_STOP)

    def body(state):
        j, _ = state
        tile(i - 1 - j, False)
        return j + 1, jnp.max(carry_scr[...])

    lax.while_loop(cond, body, (jnp.int32(0), jnp.max(carry_scr[...])))
    o_ref[0] = _pick_heads(acc_scr[...], tq).astype(o_ref.dtype)


def _stick_breaking(q, k, v, tq):
    b, s, _ = q.shape
    nk = s // tq
    rows = N_HEADS * tq
    tri = jnp.asarray(np.tril(np.ones((tq, tq), np.float32)), BF16)
    return pl.pallas_call(
        functools.partial(_sb_kernel, tq=tq),
        out_shape=jax.ShapeDtypeStruct((b, s, W_MIX), BF16),
        grid=(b, nk),
        in_specs=[
            pl.BlockSpec((1, tq, W_MIX), lambda a, i: (a, i, 0)),
            pl.BlockSpec((1, nk, tq, W_MIX), lambda a, i: (a, 0, 0, 0)),
            pl.BlockSpec((1, nk, tq, W_MIX), lambda a, i: (a, 0, 0, 0)),
            pl.BlockSpec((tq, tq), lambda a, i: (0, 0)),
        ],
        out_specs=pl.BlockSpec((1, tq, W_MIX), lambda a, i: (a, i, 0)),
        scratch_shapes=[pltpu.VMEM((rows, W_MIX), BF16), pltpu.VMEM((rows, tq), BF16),
                        pltpu.VMEM((rows, 1), F32), pltpu.VMEM((rows, W_MIX), F32)],
        compiler_params=_params("arbitrary", "arbitrary"),
        name="stick_breaking",
    )(q, k.reshape(b, nk, tq, W_MIX), v.reshape(b, nk, tq, W_MIX), tri)


def _foxcum_kernel(x_ref, b_ref, tri_ref, o_ref, carry_scr):
    @pl.when(pl.program_id(1) == 0)
    def _():
        carry_scr[...] = jnp.zeros(carry_scr.shape, F32)

    z = x_ref[0] + b_ref[...]
    logf = jnp.minimum(z, 0.0) - jnp.log(1.0 + jnp.exp(-jnp.abs(z)))
    cum = _dot(tri_ref[...], logf, NN, HI) + carry_scr[...]
    o_ref[0] = cum
    carry_scr[...] = cum[cum.shape[0] - 1:, :]


def _fox_cumsum(misc, bias_vec, tc):
    b, s, w = misc.shape
    tri = jnp.asarray(np.tril(np.ones((tc, tc), np.float32)))
    return pl.pallas_call(
        _foxcum_kernel,
        out_shape=jax.ShapeDtypeStruct((b, s, w), F32),
        grid=(b, s // tc),
        in_specs=[
            pl.BlockSpec((1, tc, w), lambda a, i: (a, i, 0)),
            pl.BlockSpec((1, w), lambda a, i: (0, 0)),
            pl.BlockSpec((tc, tc), lambda a, i: (0, 0)),
        ],
        out_specs=pl.BlockSpec((1, tc, w), lambda a, i: (a, i, 0)),
        scratch_shapes=[pltpu.VMEM((1, w), F32)],
        compiler_params=_params("arbitrary", "arbitrary"),
        name="fox_cumsum",
    )(misc, bias_vec, tri)


def _merge_kernel(x_ref, g_ref, sc_ref, sh_ref, ga_ref, wm_ref, misc_ref, pg_ref,
                  ocmp_ref, osel_ref, owin_ref, ob_ref, oc_ref, od_ref,
                  wa_ref, wb_ref, wc_ref, wd_ref, wo_ref, o_ref):
    x = x_ref[...]
    d = x.shape[1]
    h = _mod_norm(x, g_ref[...], sc_ref[0], sh_ref[0]).astype(BF16)
    gate = jax.nn.sigmoid(misc_ref[...])
    o_a = (_dot_split(gate, pg_ref[0]) * ocmp_ref[...].astype(F32)
           + _dot_split(gate, pg_ref[1]) * osel_ref[...].astype(F32)
           + _dot_split(gate, pg_ref[2]) * owin_ref[...].astype(F32)).astype(BF16)
    mixed = jnp.zeros(x.shape, F32)
    for m, (o_m, w_ref) in enumerate(((o_a, wa_ref), (ob_ref[...], wb_ref),
                                      (oc_ref[...], wc_ref), (od_ref[...], wd_ref))):
        y = _dot(o_m, w_ref[...])
        gl = _dot(h, wm_ref[:, m * d:(m + 1) * d])
        mixed = mixed + jax.nn.sigmoid(gl) * y
    o_ref[...] = x + ga_ref[0] * _dot(mixed.astype(BF16), wo_ref[...])


def _merge(x2, g, sc, sh, ga, w_merge, misc, pg, o_cmp, o_sel, o_win, o_b, o_c, o_d,
           wa, wb, wc, wd, wo, seq, tt):
    n, d = x2.shape
    tpb = seq // tt
    row = lambda w: pl.BlockSpec((tt, w), lambda i: (i, 0))
    full = lambda a: pl.BlockSpec(a.shape, lambda i: (0,) * a.ndim)
    per_b = pl.BlockSpec((1, 1, d), lambda i: (i // tpb, 0, 0))
    return pl.pallas_call(
        _merge_kernel,
        out_shape=jax.ShapeDtypeStruct((n, d), F32),
        grid=(n // tt,),
        in_specs=[row(d), full(g), per_b, per_b, per_b, full(w_merge), row(W_MIX), full(pg)]
        + [row(W_MIX)] * 6 + [full(wa), full(wb), full(wc), full(wd), full(wo)],
        out_specs=row(d),
        compiler_params=_params("arbitrary"),
        name="merge_out",
    )(x2, g, sc, sh, ga, w_merge, misc, pg, o_cmp, o_sel, o_win, o_b, o_c, o_d, wa, wb, wc, wd, wo)


def _ffn_kernel(x_ref, g_ref, sc_ref, sh_ref, gf_ref, w1_ref, w3_ref, w2_ref, o_ref, h_scr, acc_scr):
    f = pl.program_id(1)

    @pl.when(f == 0)
    def _():
        h_scr[...] = _mod_norm(x_ref[...], g_ref[...], sc_ref[0], sh_ref[0]).astype(BF16)
        acc_scr[...] = jnp.zeros(acc_scr.shape, F32)

    h = h_scr[...]
    a = _dot(h, w1_ref[...])
    b = _dot(h, w3_ref[...])
    acc_scr[...] += _dot((a * jax.nn.sigmoid(a) * b).astype(BF16), w2_ref[...])

    @pl.when(f == pl.num_programs(1) - 1)
    def _():
        o_ref[...] = x_ref[...] + gf_ref[0] * acc_scr[...]


def _ffn(x2, g, sc, sh, gf, w1, w3, w2, seq, tt, tf):
    n, d = x2.shape
    dff = w1.shape[1]
    tpb = seq // tt
    per_b = pl.BlockSpec((1, 1, d), lambda i, f: (i // tpb, 0, 0))
    return pl.pallas_call(
        _ffn_kernel,
        out_shape=jax.ShapeDtypeStruct((n, d), F32),
        grid=(n // tt, dff // tf),
        in_specs=[
            pl.BlockSpec((tt, d), lambda i, f: (i, 0)),
            pl.BlockSpec((1, d), lambda i, f: (0, 0)),
            per_b, per_b, per_b,
            pl.BlockSpec((d, tf), lambda i, f: (0, f)),
            pl.BlockSpec((d, tf), lambda i, f: (0, f)),
            pl.BlockSpec((tf, d), lambda i, f: (f, 0)),
        ],
        out_specs=pl.BlockSpec((tt, d), lambda i, f: (i, 0)),
        scratch_shapes=[pltpu.VMEM((tt, d), BF16), pltpu.VMEM((tt, d), F32)],
        compiler_params=_params("arbitrary", "arbitrary"),
        name="ffn_swiglu",
    )(x2, g, sc, sh, gf, w1, w3, w2)


def _route_kernel(x_ref, g_ref, sc_ref, sh_ref, rw_ref, up_ref, h_ref, rank_ref, gate_ref, cnt_ref):
    hf = _mod_norm(x_ref[...], g_ref[...], sc_ref[0], sh_ref[0])
    h_ref[...] = hf.astype(BF16)
    logits = _dot(rw_ref[...], hf, NT, HI)
    ne, tt = logits.shape
    e_idx = lax.broadcasted_iota(jnp.int32, (ne, 1), 0).astype(F32)
    v1 = jnp.max(logits, axis=0, keepdims=True)
    i1 = jnp.min(jnp.where(logits == v1, e_idx, float(ne)), axis=0, keepdims=True)
    m1 = e_idx == i1
    rest = jnp.where(m1, -3e38, logits)
    v2 = jnp.max(rest, axis=0, keepdims=True)
    i2 = jnp.min(jnp.where(rest == v2, e_idx, float(ne)), axis=0, keepdims=True)
    m2 = e_idx == i2
    e2 = jnp.exp(v2 - v1)
    g1 = 1.0 / (1.0 + e2)
    g2 = e2 / (1.0 + e2)
    routed = m1 | m2
    rf = jnp.where(routed, 1.0, 0.0)
    rank = _dot(rf.astype(BF16), up_ref[...])
    rank = jnp.where(routed, rank, -1.0)
    gate = jnp.where(m1, g1, 0.0) + jnp.where(m2, g2, 0.0)
    for e in range(ne):
        rank_ref[0, e] = rank[e:e + 1, :]
        gate_ref[0, e] = gate[e:e + 1, :]
    cnt = jnp.sum(rf, axis=1, keepdims=True)
    cnt_ref[0] = jnp.broadcast_to(cnt, (ne, LANES))


def _route(x2, g, sc, sh, rw_t, seq, tt):
    n, d = x2.shape
    ne = rw_t.shape[0]
    tpb = seq // tt
    nt = n // tt
    upper = jnp.asarray(np.triu(np.ones((tt, tt), np.float32), 1), BF16)
    per_b = pl.BlockSpec((1, 1, d), lambda i: (i // tpb, 0, 0))
    return pl.pallas_call(
        _route_kernel,
        out_shape=[jax.ShapeDtypeStruct((n, d), BF16),
                   jax.ShapeDtypeStruct((nt, ne, 1, tt), F32),
                   jax.ShapeDtypeStruct((nt, ne, 1, tt), F32),
                   jax.ShapeDtypeStruct((nt, ne, LANES), F32)],
        grid=(nt,),
        in_specs=[
            pl.BlockSpec((tt, d), lambda i: (i, 0)),
            pl.BlockSpec((1, d), lambda i: (0, 0)),
            per_b, per_b,
            pl.BlockSpec((ne, d), lambda i: (0, 0)),
            pl.BlockSpec((tt, tt), lambda i: (0, 0)),
        ],
        out_specs=[pl.BlockSpec((tt, d), lambda i: (i, 0)),
                   pl.BlockSpec((1, ne, 1, tt), lambda i: (i, 0, 0, 0)),
                   pl.BlockSpec((1, ne, 1, tt), lambda i: (i, 0, 0, 0)),
                   pl.BlockSpec((1, ne, LANES), lambda i: (i, 0, 0))],
        compiler_params=_params("arbitrary"),
        name="moe_route",
    )(x2, g, sc, sh, rw_t, upper)


def _moe_kernel(cnt_ref, x_ref, gf_ref, h_ref, rank_ref, gate_ref, w1_ref, w3_ref, w2_ref,
                o_ref, acc_scr, xs_scr, y_scr, *, chunk):
    i, e, f = pl.program_id(0), pl.program_id(1), pl.program_id(2)
    ne, nf = pl.num_programs(1), pl.num_programs(2)

    @pl.when((e == 0) & (f == 0))
    def _():
        acc_scr[...] = jnp.zeros(acc_scr.shape, F32)

    count = cnt_ref[i * ne + e]
    rank = rank_ref[0, 0]
    gate = gate_ref[0, 0]
    n_small = (count + chunk - 1) // chunk
    n_big = (count + 2 * chunk - 1) // (2 * chunk)

    def one_hot(c, rows):
        r = c * rows + lax.broadcasted_iota(jnp.int32, (rows, 1), 0)
        return rank == r.astype(F32)

    def rows_of(c, rows):
        return pl.ds(pl.multiple_of(c * rows, rows), rows)

    @pl.when(f == 0)
    def _():
        h = h_ref[...]

        def gather(c, carry):
            p = jnp.where(one_hot(c, chunk), 1.0, 0.0).astype(BF16)
            xs_scr[rows_of(c, chunk), :] = _dot(p, h).astype(BF16)
            return carry

        lax.fori_loop(0, n_small, gather, 0)

        def clear(c, carry):
            y_scr[rows_of(c, 2 * chunk), :] = jnp.zeros((2 * chunk, y_scr.shape[1]), F32)
            return carry

        lax.fori_loop(0, n_big, clear, 0)

    def expert(c, carry):
        xs = xs_scr[rows_of(c, chunk), :]
        a = _dot(xs, w1_ref[0])
        b = _dot(xs, w3_ref[0])
        y_scr[rows_of(c, chunk), :] += _dot((a * jax.nn.sigmoid(a) * b).astype(BF16), w2_ref[0])
        return carry

    lax.fori_loop(0, n_small, expert, 0)

    @pl.when(f == nf - 1)
    def _():
        def scatter(c, carry):
            hit = one_hot(c, 2 * chunk)
            p = jnp.where(hit, 1.0, 0.0).astype(BF16)
            gcol = jnp.sum(jnp.where(hit, gate, 0.0), axis=-1, keepdims=True)
            acc_scr[...] += _dot(p, (y_scr[rows_of(c, 2 * chunk), :] * gcol).astype(BF16), TN)
            return carry

        lax.fori_loop(0, n_big, scatter, 0)

    @pl.when((e == ne - 1) & (f == nf - 1))
    def _():
        o_ref[...] = x_ref[...] + gf_ref[0] * acc_scr[...]


def _moe(counts, x2, gf, h2, rank, gate, w1, w3, w2, seq, tt, tf, chunk):
    n, d = x2.shape
    ne, _, dff = w1.shape
    tpb = seq // tt
    grid_spec = pltpu.PrefetchScalarGridSpec(
        num_scalar_prefetch=1,
        grid=(n // tt, ne, dff // tf),
        in_specs=[
            pl.BlockSpec((tt, d), lambda i, e, f, c: (i, 0)),
            pl.BlockSpec((1, 1, d), lambda i, e, f, c: (i // tpb, 0, 0)),
            pl.BlockSpec((tt, d), lambda i, e, f, c: (i, 0)),
            pl.BlockSpec((1, 1, 1, tt), lambda i, e, f, c: (i, e, 0, 0)),
            pl.BlockSpec((1, 1, 1, tt), lambda i, e, f, c: (i, e, 0, 0)),
            pl.BlockSpec((1, d, tf), lambda i, e, f, c: (e, 0, f)),
            pl.BlockSpec((1, d, tf), lambda i, e, f, c: (e, 0, f)),
            pl.BlockSpec((1, tf, d), lambda i, e, f, c: (e, f, 0)),
        ],
        out_specs=pl.BlockSpec((tt, d), lambda i, e, f, c: (i, 0)),
        scratch_shapes=[pltpu.VMEM((tt, d), F32), pltpu.VMEM((tt, d), BF16), pltpu.VMEM((tt, d), F32)],
    )
    return pl.pallas_call(
        functools.partial(_moe_kernel, chunk=chunk),
        out_shape=jax.ShapeDtypeStruct((n, d), F32),
        grid_spec=grid_spec,
        compiler_params=_params("arbitrary", "arbitrary", "arbitrary"),
        name="moe_experts",
    )(counts, x2, gf, h2, rank, gate, w1, w3, w2)


def _overlap_matrix(ncp, nc, n_sel):
    c0 = np.arange(ncp) * CMP_STRIDE
    c1 = c0 + CMP_LEN
    s0 = np.arange(n_sel) * SEL_LEN
    s1 = s0 + SEL_LEN
    ov = ((c0[:, None] < s1[None, :]) & (c1[:, None] > s0[None, :])).astype(np.float32)
    ov[nc:] = 0.0
    return jnp.asarray(ov.T)


def _gate_expand():
    pg = np.zeros((3, W_MIX, W_MIX), np.float32)
    for br in range(3):
        for h in range(N_HEADS):
            pg[br, GATE_LANE + 3 * h + br, h * HEAD_DIM:(h + 1) * HEAD_DIM] = 1.0
    return jnp.asarray(pg, BF16)


def _mixer_layer(x2, b, s, mod, norm_g, rope, w_in, qk_gain, pe_k, pe_v, ck1, ck2, cv1, cv2,
                 fox_b, w_branch, w_out):
    n, d = x2.shape
    sh_a, sc_a, g_a = mod[0], mod[1], mod[2]
    w_slab, gain, w_merge = _pack_w_in(w_in, qk_gain)
    bd = jnp.asarray(np.kron(np.eye(N_HEADS), np.full((HEAD_DIM, HEAD_DIM), 1.0 / HEAD_DIM)), BF16)
    outs = _proj(x2, norm_g, sc_a, sh_a, w_slab, gain, bd, rope, s, min(512, s))
    sl = [a.reshape(b, s, W_MIX) for a in outs[:N_SLABS]]
    dil_in = {1: (sl[S_BQ], sl[S_BK], sl[S_BV])}
    for di, dil in enumerate(_DILATIONS):
        dil_in[dil] = tuple(outs[N_SLABS + si * len(_DILATIONS) + di].reshape(b, s // dil, dil * W_MIX)
                            for si in range(len(_DIL_SLABS)))
    misc = sl[S_MISC]

    nch = s // CMP_STRIDE
    nc = nch - CMP_LEN // CMP_STRIDE + 1
    kc_raw = sl[S_KC][..., :HEAD_DIM]
    vc_raw = misc[..., VC_LANE:VC_LANE + HEAD_DIM]
    chunks = jnp.stack([kc_raw, vc_raw]).reshape(2, b, nch, CMP_STRIDE * HEAD_DIM)
    chunks_next = jnp.concatenate([chunks[:, :, 1:], jnp.zeros_like(chunks[:, :, :1])], axis=2)
    pe = jnp.stack([pe_k, pe_v]).reshape(2, 1, CMP_LEN * HEAD_DIM)
    kvc = jnp.tile(_compress(chunks, chunks_next, pe, jnp.stack([ck1, cv1]), jnp.stack([ck2, cv2])),
                   (1, 1, 1, N_HEADS))
    overlap = _overlap_matrix(nch, nc, s // SEL_LEN)
    o_cmp, selmask = _cmp_topk(sl[S_QNR], kvc[0], kvc[1].transpose(0, 2, 1).astype(BF16), overlap,
                               min(256, s), nc)
    tq, tk = min(256, s), min(512, s)
    o_sel = _causal_attn(sl[S_QR], sl[S_KV], sl[S_KV], tq, tk, sel=selmask)
    o_win = _window_attn(sl[S_QR], sl[S_KV], sl[S_KV], dil=1, window=NSA_WINDOW, tq=256, out_dtype=BF16,
                         kv_lanes=(KW_LANE, VW_LANE))

    others = []
    for (wdw, dil) in DIL_CONFIGS[:0:-1]:
        others.append(_window_attn(*dil_in[dil], dil=dil, window=wdw // dil + 1, tq=256,
                                   out_dtype=F32, emit_lse=True))
    wdw, dil = DIL_CONFIGS[0]
    o_b = _window_attn(*dil_in[dil], dil=dil, window=wdw // dil + 1, tq=256, out_dtype=BF16, others=others)

    o_c = _stick_breaking(sl[S_CQ], sl[S_CK], sl[S_CV], min(256, s))

    bias_vec = jnp.zeros((1, W_MIX), F32).at[0, FOX_LANE:FOX_LANE + N_HEADS].set(fox_b)
    fcum = _fox_cumsum(misc, bias_vec, min(512, s))
    qk_bound = 1.02 * HEAD_DIM ** 0.5 * jnp.max(jnp.abs(qk_gain[6])) * jnp.max(jnp.abs(qk_gain[7])) + 0.05
    o_d = _causal_attn(sl[S_DQ], sl[S_DK], sl[S_DV], tq, tk, fcum=fcum[..., FOX_LANE:FOX_LANE + N_HEADS],
                       qk_bound=qk_bound)

    wb16 = w_branch.astype(BF16)
    flat = lambda a: a.reshape(n, W_MIX)
    return _merge(x2, norm_g, sc_a, sh_a, g_a, w_merge, flat(misc), _gate_expand(),
                  flat(o_cmp), flat(o_sel), flat(o_win), flat(o_b), flat(o_c), flat(o_d),
                  wb16[0], wb16[1], wb16[2], wb16[3], w_out.astype(BF16), s, min(512, s))


def kernel(x, c, positions, w_ada, b_ada, norm_mix, norm_ffn, w_in, qk_gain, nsa_pe_k, nsa_pe_v,
           nsa_ck_w1, nsa_ck_w2, nsa_cv_w1, nsa_cv_w2, fox_bias, w_branch, w_out,
           ffn_w1, ffn_w3, ffn_w2, router_w, moe_w1, moe_w3, moe_w2):
    b, s, d = x.shape
    depth = w_ada.shape[0]
    rope = _rope_tables(positions)
    mods = _ada(c, w_ada, b_ada).reshape(depth, b, 6, 1, d).transpose(0, 2, 1, 3, 4)
    x2 = x.reshape(b * s, d)
    for l in range(depth):
        mod = mods[l]
        x2 = _mixer_layer(x2, b, s, mod[0:3], norm_mix[l].reshape(1, d), rope, w_in[l], qk_gain[l],
                          nsa_pe_k[l], nsa_pe_v[l], nsa_ck_w1[l], nsa_ck_w2[l], nsa_cv_w1[l],
                          nsa_cv_w2[l], fox_bias[l], w_branch[l], w_out[l])
        sh_f, sc_f, g_f = mod[3], mod[4], mod[5]
        gn = norm_ffn[l].reshape(1, d)
        e = l // 2
        if l % 2 == 0:
            dff = ffn_w1.shape[2]
            x2 = _ffn(x2, gn, sc_f, sh_f, g_f, ffn_w1[e].astype(BF16), ffn_w3[e].astype(BF16),
                      ffn_w2[e].astype(BF16), s, min(1024, s), dff // 2)
        else:
            tt = min(1024, s)
            dff = moe_w1.shape[3]
            h2, rank, gate, cnt = _route(x2, gn, sc_f, sh_f, router_w[e].T, s, tt)
            counts = cnt[:, :, 0].astype(jnp.int32).reshape(-1)
            x2 = _moe(counts, x2, g_f, h2, rank, gate, moe_w1[e].astype(BF16), moe_w3[e].astype(BF16),
                      moe_w2[e].astype(BF16), s, tt, dff // 2, 128)
    return x2.reshape(b, s, d)
```

```python
import functools

import numpy as np
import jax
import jax.numpy as jnp
from jax import lax
from jax.experimental import pallas as pl
from jax.experimental.pallas import tpu as pltpu

F32 = jnp.float32
BF16 = jnp.bfloat16
HI = lax.Precision.HIGHEST

LANES = 128
VMEM_LIMIT = 52 * 1024 * 1024

HEAD_DIM = 64
HEAD_SHIFT = 6
N_HEADS = 4
W_MIX = N_HEADS * HEAD_DIM
N_MIXERS = 4
ROPE_THETA = 500000.0
ROPE_DIMS = HEAD_DIM // 4
ROPE_HALF = ROPE_DIMS // 2
EPS = 1e-6
LOG2E = 1.4426950408889634
NEG = -1e30
BIG = 1e30
CMP_LEN = 32
CMP_STRIDE = 16
SEL_LEN = 64
SEL_SHIFT = 6
TOPN = 16
SEL_NEG = 2.0 ** 30
NSA_WINDOW = 512
DIL_CONFIGS = ((128, 1), (512, 4), (2048, 16))
SB_STOP = -110.0
FOX_STOP = 108.0
GATE_LANE = 0
FOX_LANE = 3 * N_HEADS
VC_LANE = HEAD_DIM

NN = (((1,), (0,)), ((), ()))
NT = (((1,), (1,)), ((), ()))
TN = (((0,), (0,)), ((), ()))


def _dot(a, b, dims=NN, precision=None):
    return lax.dot_general(a, b, dims, precision=precision, preferred_element_type=F32)


def _dot_split(a, b_bf16, dims=NN):
    hi = a.astype(BF16)
    lo = (a - hi.astype(F32)).astype(BF16)
    return _dot(hi, b_bf16, dims) + _dot(lo, b_bf16, dims)


def _params(*sem):
    return pltpu.CompilerParams(dimension_semantics=sem, vmem_limit_bytes=VMEM_LIMIT)


def _mod_norm(x, g, sc, sh):
    ms = jnp.mean(x * x, axis=-1, keepdims=True)
    return (x * lax.rsqrt(ms + EPS) * g) * (1.0 + sc) + sh


def _head_masks():
    lane = lax.broadcasted_iota(jnp.int32, (1, W_MIX), 1)
    return [(lane >> HEAD_SHIFT) == h for h in range(N_HEADS)]


def _stack_masked(q, qm_scr, tq):
    for h, hm in enumerate(_head_masks()):
        qm_scr[h * tq:(h + 1) * tq, :] = jnp.where(hm, q, jnp.zeros_like(q))


def _pick_heads(stacked, tq, scale=None):
    out = None
    for h, hm in enumerate(_head_masks()):
        blk = stacked[h * tq:(h + 1) * tq, :]
        if scale is not None:
            blk = blk * scale[h]
        out = jnp.where(hm, blk, 0.0 if out is None else out)
    return out


def _ada_kernel(c_ref, w_ref, b_ref, o_ref):
    c = c_ref[...]
    ca = c * jax.nn.sigmoid(c)
    o_ref[0] = _dot(ca, w_ref[0], NN, HI) + b_ref[0]


def _ada(c, w_ada, b_ada):
    depth, d, n6 = w_ada.shape
    b = c.shape[0]
    tn = n6 // 4
    return pl.pallas_call(
        _ada_kernel,
        out_shape=jax.ShapeDtypeStruct((depth, b, n6), F32),
        grid=(depth, n6 // tn),
        in_specs=[
            pl.BlockSpec((b, d), lambda l, j: (0, 0)),
            pl.BlockSpec((1, d, tn), lambda l, j: (l, 0, j)),
            pl.BlockSpec((1, 1, tn), lambda l, j: (l, 0, j)),
        ],
        out_specs=pl.BlockSpec((1, b, tn), lambda l, j: (l, 0, j)),
        compiler_params=_params("arbitrary", "arbitrary"),
        name="ada_mod",
    )(c, w_ada, b_ada.reshape(depth, 1, n6))


(W_QA, W_KC, W_KV, W_BQ, W_BK, W_BV, W_CQ, W_CK, W_CV, W_DQ, W_DK, W_DV, W_MISC) = range(13)
N_WSLABS = 13
KV_QK_LANES = 2 * HEAD_DIM
_SLABS = (
    (W_QA, "all", None, F32), (W_QA, "all", "all", BF16), (W_KC, "all", None, F32),
    (W_KV, "kv", "kv", BF16),
    (W_BQ, "all", "all", BF16), (W_BK, "all", "all", BF16), (W_BV, None, None, BF16),
    (W_CQ, None, None, BF16), (W_CK, None, None, BF16), (W_CV, None, None, BF16),
    (W_DQ, "all", None, BF16), (W_DK, "all", None, BF16), (W_DV, None, None, BF16),
    (W_MISC, None, None, F32),
)
N_SLABS = len(_SLABS)
(S_QNR, S_QR, S_KC, S_KV, S_BQ, S_BK, S_BV, S_CQ, S_CK, S_CV, S_DQ, S_DK, S_DV, S_MISC) = range(N_SLABS)
_DIL_SLABS = (S_BQ, S_BK, S_BV)
_DILATIONS = tuple(d for (_, d) in DIL_CONFIGS if d > 1)
KSL_LANE, KW_LANE, VSL_LANE, VW_LANE = (i * HEAD_DIM for i in range(4))


def _proj_kernel(x_ref, g_ref, sc_ref, sh_ref, w_ref, gain_ref, bd_ref, rope_ref, *out_refs):
    h = _mod_norm(x_ref[...], g_ref[...], sc_ref[0], sh_ref[0]).astype(BF16)
    bd = bd_ref[...]
    key_lanes = lax.broadcasted_iota(jnp.int32, (1, W_MIX), 1) < KV_QK_LANES
    last_w, y_n = None, None
    raw = [_dot(h, w_ref[:, wi * W_MIX:(wi + 1) * W_MIX]) for wi in range(N_WSLABS)]
    for s, (wi, norm, rope, _) in enumerate(_SLABS):
        if wi != last_w:
            y_n = raw[wi]
            if norm:
                normed = y_n * lax.rsqrt(_dot_split(y_n * y_n, bd) + EPS)
                y_n = normed if norm == "all" else jnp.where(key_lanes, normed, y_n)
            last_w = wi
        y = y_n * gain_ref[s]
        if rope:
            c, s1, s2 = rope_ref[0, 0], rope_ref[0, 1], rope_ref[0, 2]
            if rope == "kv":
                c, s1, s2 = jnp.where(key_lanes, c, 1.0), jnp.where(key_lanes, s1, 0.0), jnp.where(key_lanes, s2, 0.0)
            y = y * c + pltpu.roll(y, W_MIX - ROPE_HALF, 1) * s1 + pltpu.roll(y, ROPE_HALF, 1) * s2
        out_refs[s][...] = y.astype(out_refs[s].dtype)
        if s in _DIL_SLABS:
            stage_scr = out_refs[-1]
            tt = y.shape[0]
            for c in range(W_MIX // LANES):
                stage_scr[c] = y[:, c * LANES:(c + 1) * LANES]
            for di, dil in enumerate(_DILATIONS):
                o_ref = out_refs[N_SLABS + _DIL_SLABS.index(s) * len(_DILATIONS) + di]
                for r in range(dil):
                    for c in range(W_MIX // LANES):
                        lo = r * W_MIX + c * LANES
                        o_ref[:, lo:lo + LANES] = (
                            stage_scr[c, pl.ds(r, tt // dil, stride=dil), :].astype(o_ref.dtype))


def _proj(x2, g, sc, sh, w_slab, gain, bd, rope, seq, tt):
    n, d = x2.shape
    tpb = seq // tt
    out_shape = [jax.ShapeDtypeStruct((n, W_MIX), dt) for (_, _, _, dt) in _SLABS]
    out_specs = [pl.BlockSpec((tt, W_MIX), lambda i: (i, 0)) for _ in _SLABS]
    for _ in _DIL_SLABS:
        for dil in _DILATIONS:
            out_shape.append(jax.ShapeDtypeStruct((n // dil, dil * W_MIX), BF16))
            out_specs.append(pl.BlockSpec((tt // dil, dil * W_MIX), lambda i: (i, 0)))
    return pl.pallas_call(
        _proj_kernel,
        out_shape=out_shape,
        grid=(n // tt,),
        in_specs=[
            pl.BlockSpec((tt, d), lambda i: (i, 0)),
            pl.BlockSpec((1, d), lambda i: (0, 0)),
            pl.BlockSpec((1, 1, d), lambda i: (i // tpb, 0, 0)),
            pl.BlockSpec((1, 1, d), lambda i: (i // tpb, 0, 0)),
            pl.BlockSpec((d, N_WSLABS * W_MIX), lambda i: (0, 0)),
            pl.BlockSpec((N_SLABS, 1, W_MIX), lambda i: (0, 0, 0)),
            pl.BlockSpec((W_MIX, W_MIX), lambda i: (0, 0)),
            pl.BlockSpec((1, 3, tt, W_MIX), lambda i: (i // tpb, 0, i % tpb, 0)),
        ],
        out_specs=out_specs,
        scratch_shapes=[pltpu.VMEM((W_MIX // LANES, tt, LANES), F32)],
        compiler_params=_params("arbitrary"),
        name="in_proj",
    )(x2, g, sc, sh, w_slab, gain, bd, rope)


def _pack_w_in(w_in, qk_gain):
    d = w_in.shape[0]
    o = 0
    a_q = w_in[:, o:o + W_MIX]; o += W_MIX
    a_kv = w_in[:, o:o + 6 * HEAD_DIM]; o += 6 * HEAD_DIM
    a_g = w_in[:, o:o + 3 * N_HEADS]; o += 3 * N_HEADS
    b_qkv = w_in[:, o:o + 3 * W_MIX]; o += 3 * W_MIX
    c_qkv = w_in[:, o:o + 3 * W_MIX]; o += 3 * W_MIX
    d_qkv = w_in[:, o:o + 3 * W_MIX]; o += 3 * W_MIX
    d_f = w_in[:, o:o + N_HEADS]; o += N_HEADS
    w_merge = w_in[:, o:]
    kc, vc, ksl, vsl, kw, vw = (a_kv[:, i * HEAD_DIM:(i + 1) * HEAD_DIM] for i in range(6))
    zeros = lambda w: jnp.zeros((d, w), w_in.dtype)
    misc = jnp.concatenate([a_g, d_f, zeros(VC_LANE - FOX_LANE - N_HEADS), vc, zeros(W_MIX - 2 * HEAD_DIM)], axis=1)
    slabs = [
        a_q, jnp.concatenate([kc, zeros(W_MIX - HEAD_DIM)], axis=1),
        jnp.concatenate([ksl, kw, vsl, vw], axis=1),
        b_qkv[:, :W_MIX], b_qkv[:, W_MIX:2 * W_MIX], b_qkv[:, 2 * W_MIX:],
        c_qkv[:, :W_MIX], c_qkv[:, W_MIX:2 * W_MIX], c_qkv[:, 2 * W_MIX:],
        d_qkv[:, :W_MIX], d_qkv[:, W_MIX:2 * W_MIX], d_qkv[:, 2 * W_MIX:],
        misc,
    ]
    w_slab = jnp.concatenate(slabs, axis=1).astype(BF16)
    scale = HEAD_DIM ** -0.5
    t4 = lambda gvec: jnp.tile(gvec, N_HEADS)
    one = jnp.ones((W_MIX,), F32)
    scale2 = scale * LOG2E
    gains = [
        t4(qk_gain[0]) * scale, t4(qk_gain[0]) * scale2,
        jnp.concatenate([qk_gain[1], jnp.ones((W_MIX - HEAD_DIM,), F32)]),
        jnp.concatenate([qk_gain[2], qk_gain[3], jnp.ones((W_MIX - KV_QK_LANES,), F32)]),
        t4(qk_gain[4]) * scale2, t4(qk_gain[5]), one,
        one * scale, one, one,
        t4(qk_gain[6]) * scale2, t4(qk_gain[7]), one,
        one,
    ]
    gain = jnp.stack(gains).reshape(N_SLABS, 1, W_MIX).astype(F32)
    return w_slab, gain, w_merge.astype(BF16)


def _rope_tables(positions):
    inv = ROPE_THETA ** (-jnp.arange(0, ROPE_DIMS, 2, dtype=F32) / ROPE_DIMS)
    ang = positions.astype(F32)[..., None] * inv
    cos, sin = jnp.cos(ang), jnp.sin(ang)
    b, s, _ = cos.shape
    pad1 = jnp.ones((b, s, HEAD_DIM - ROPE_DIMS), F32)
    pad0 = jnp.zeros((b, s, HEAD_DIM - ROPE_DIMS), F32)
    z8 = jnp.zeros_like(sin)
    c64 = jnp.concatenate([cos, cos, pad1], axis=-1)
    s1_64 = jnp.concatenate([-sin, z8, pad0], axis=-1)
    s2_64 = jnp.concatenate([z8, sin, pad0], axis=-1)
    heads = lambda t: jnp.tile(t, (1, 1, N_HEADS))
    return jnp.stack([heads(c64), heads(s1_64), heads(s2_64)], axis=1)


def _compress_kernel(a_ref, b_ref, pe_ref, w1_ref, w2_ref, o_ref):
    half = w1_ref.shape[1] // 2
    w1 = w1_ref[0]
    hid = (_dot(a_ref[0, 0], w1[:half], NN, HI) + _dot(b_ref[0, 0], w1[half:], NN, HI)
           + _dot(pe_ref[0], w1, NN, HI))
    o_ref[0, 0] = _dot(jax.nn.gelu(hid), w2_ref[0], NN, HI)


def _compress(ch, chn, pe, w1, w2):
    _, b, ncp, cw = ch.shape
    hid = w1.shape[2]
    return pl.pallas_call(
        _compress_kernel,
        out_shape=jax.ShapeDtypeStruct((2, b, ncp, HEAD_DIM), F32),
        grid=(2, b),
        in_specs=[
            pl.BlockSpec((1, 1, ncp, cw), lambda k, i: (k, i, 0, 0)),
            pl.BlockSpec((1, 1, ncp, cw), lambda k, i: (k, i, 0, 0)),
            pl.BlockSpec((1, 1, 2 * cw), lambda k, i: (k, 0, 0)),
            pl.BlockSpec((1, 2 * cw, hid), lambda k, i: (k, 0, 0)),
            pl.BlockSpec((1, hid, HEAD_DIM), lambda k, i: (k, 0, 0)),
        ],
        out_specs=pl.BlockSpec((1, 1, ncp, HEAD_DIM), lambda k, i: (k, i, 0, 0)),
        compiler_params=_params("arbitrary", "arbitrary"),
        name="nsa_compress",
    )(ch, chn, pe, w1, w2)


def _cmp_kernel(q_ref, kc_ref, vct_ref, ovt_ref, o_ref, sel_ref, qh_scr, ql_scr, acc_scr, *, tq, nc, n_sel):
    i = pl.program_id(1)
    ncp = kc_ref.shape[1]
    q = q_ref[0]
    q_hi = q.astype(BF16)
    _stack_masked(q_hi, qh_scr, tq)
    _stack_masked((q - q_hi.astype(F32)).astype(BF16), ql_scr, tq)
    kc = kc_ref[0]
    k_hi = kc.astype(BF16)
    k_lo = (kc - k_hi.astype(F32)).astype(BF16)
    ss = []
    for h in range(N_HEADS):
        rs = slice(h * tq, (h + 1) * tq)
        ss.append(_dot(k_hi, qh_scr[rs, :], NT) + _dot(k_lo, qh_scr[rs, :], NT)
                  + _dot(k_hi, ql_scr[rs, :], NT))
    t = i * tq + lax.broadcasted_iota(jnp.int32, (1, tq), 1)
    c = lax.broadcasted_iota(jnp.int32, (ncp, 1), 0)
    mask = (c * CMP_STRIDE + (CMP_LEN - 1) <= t) & (c < nc)
    psum = None
    ps = []
    for h in range(N_HEADS):
        sm = jnp.where(mask, ss[h], NEG)
        m = jnp.max(sm, axis=0, keepdims=True)
        e = jnp.where(mask, jnp.exp(sm - m), 0.0)
        l = jnp.sum(e, axis=0, keepdims=True)
        p = e / jnp.maximum(l, 1e-30)
        psum = p if psum is None else psum + p
        ps.append(p.astype(BF16))
    vct = vct_ref[0]
    for h in range(N_HEADS):
        rs = slice(h * HEAD_DIM, (h + 1) * HEAD_DIM)
        acc_scr[rs, :] = _dot(vct[rs, :], ps[h])
    o_ref[0] = acc_scr[...].T.astype(o_ref.dtype)
    imp = _dot(ovt_ref[...], psum, NN, HI)
    j = lax.broadcasted_iota(jnp.int32, (n_sel, 1), 0)
    cur = t >> SEL_SHIFT
    valid = j <= cur
    forced = (j == 0) | (j == cur) | (j == cur - 1)
    score = jnp.where(valid, jnp.where(forced, BIG, imp), NEG)
    sel = jnp.zeros((n_sel, tq), F32)
    jf = j.astype(F32)
    for _ in range(min(TOPN, n_sel)):
        mx = jnp.max(score, axis=0, keepdims=True)
        idx = jnp.min(jnp.where(score == mx, jf, float(n_sel)), axis=0, keepdims=True)
        pick = jf == idx
        sel = jnp.where(pick, 1.0, sel)
        score = jnp.where(pick, -3e38, score)
    sel_ref[0] = ((sel - 1.0) * SEL_NEG).T.astype(sel_ref.dtype)


def _cmp_topk(q_nr, kc_rep, vc_rep, overlap, tq, nc):
    b, seq, _ = q_nr.shape
    ncp = kc_rep.shape[1]
    n_sel = seq // SEL_LEN
    return pl.pallas_call(
        functools.partial(_cmp_kernel, tq=tq, nc=nc, n_sel=n_sel),
        out_shape=[jax.ShapeDtypeStruct((b, seq, W_MIX), BF16),
                   jax.ShapeDtypeStruct((b, seq, n_sel), BF16)],
        grid=(b, seq // tq),
        in_specs=[
            pl.BlockSpec((1, tq, W_MIX), lambda g, i: (g, i, 0)),
            pl.BlockSpec((1, ncp, W_MIX), lambda g, i: (g, 0, 0)),
            pl.BlockSpec((1, W_MIX, ncp), lambda g, i: (g, 0, 0)),
            pl.BlockSpec((n_sel, ncp), lambda g, i: (0, 0)),
        ],
        out_specs=[pl.BlockSpec((1, tq, W_MIX), lambda g, i: (g, i, 0)),
                   pl.BlockSpec((1, tq, n_sel), lambda g, i: (g, i, 0))],
        scratch_shapes=[pltpu.VMEM((N_HEADS * tq, W_MIX), BF16), pltpu.VMEM((N_HEADS * tq, W_MIX), BF16),
                        pltpu.VMEM((W_MIX, tq), F32)],
        compiler_params=_params("arbitrary", "arbitrary"),
        name="nsa_cmp_topk",
    )(q_nr, kc_rep, vc_rep, overlap)


def _causal_kernel(*refs, tq, tk, has_sel, has_bias):
    it = iter(refs)
    q_ref, k_ref, vt_ref = next(it), next(it), next(it)
    sel_ref = next(it) if has_sel else None
    fa_ref, fb_ref, fq_ref, thr_ref = (next(it) for _ in range(4)) if has_bias else (None,) * 4
    o_ref, qm_scr, acc_scr, sa_scr, sb_scr, m_scr, l_scr = (next(it) for _ in range(7))
    i = pl.program_id(1)
    t0 = i * tq
    lane = lax.broadcasted_iota(jnp.int32, (1, W_MIX), 1)
    if has_sel:
        n_sel = sel_ref.shape[2]
        qf = q_ref[0].astype(F32)
        selb = sel_ref[0].astype(F32)
        selb = jnp.concatenate([selb, jnp.zeros((tq, W_MIX - n_sel), F32)], axis=1)
        selb = pltpu.roll(selb, HEAD_DIM, 1)
        for h in range(N_HEADS):
            rot = qf if h == 0 else pltpu.roll(qf, W_MIX - h * HEAD_DIM, 1)
            qm_scr[h * tq:(h + 1) * tq, :] = jnp.where(lane < HEAD_DIM, rot, selb).astype(BF16)
    else:
        _stack_masked(q_ref[0], qm_scr, tq)
    acc_scr[...] = jnp.zeros(acc_scr.shape, F32)
    t_pos = t0 + lax.broadcasted_iota(jnp.int32, (1, tq), 1)

    def scores(kt, s_buf):
        k_t = k_ref[0, kt]
        if has_sel:
            blk = (kt * tk + lax.broadcasted_iota(jnp.int32, (tk, 1), 0)) >> SEL_SHIFT
            one_hot = jnp.where(blk == lane - HEAD_DIM, 1.0, 0.0).astype(BF16)
            k_t = jnp.where(lane < HEAD_DIM, k_t, one_hot)
        for h in range(N_HEADS):
            s = _dot(k_t, qm_scr[h * tq:(h + 1) * tq, :], NT)
            if has_bias:
                s = s + _dot(fa_ref[0, kt], fb_ref[0, h])
            s_buf[h] = s

    def update(kt, s_buf, diag):
        vt_t = vt_ref[0, kt]
        mask = None
        if diag:
            s_pos = kt * tk + lax.broadcasted_iota(jnp.int32, (tk, 1), 0)
            mask = s_pos <= t_pos
        ps, alphas = [], []
        for h in range(N_HEADS):
            s = s_buf[h]
            if mask is not None:
                s = jnp.where(mask, s, NEG)
            m_old = m_scr[h]
            m_new = jnp.maximum(m_old, jnp.max(s, axis=0, keepdims=True))
            p = jnp.exp2(s - m_new)
            alpha = jnp.exp2(m_old - m_new)
            m_scr[h] = m_new
            l_scr[h] = alpha * l_scr[h] + jnp.sum(p, axis=0, keepdims=True)
            ps.append(p.astype(BF16))
            alphas.append(alpha)
        for h in range(N_HEADS):
            rs = slice(h * HEAD_DIM, (h + 1) * HEAD_DIM)
            vs = slice(VSL_LANE, VSL_LANE + HEAD_DIM) if has_sel else rs
            acc_scr[rs, :] = alphas[h] * acc_scr[rs, :] + _dot(vt_t[vs, :], ps[h])

    def live(kt_next):
        lane = lax.broadcasted_iota(jnp.int32, (1, thr_ref.shape[3]), 1)
        hit = None
        for h in range(N_HEADS):
            top = jnp.max(fq_ref[0, h] - m_scr[h], axis=-1, keepdims=True)
            need = top >= thr_ref[0, h]
            hit = need if hit is None else (hit | need)
        return jnp.max(jnp.where(hit & (lane == kt_next), 1.0, 0.0)) > 0.5

    n_last = t0 // tk
    m_scr[...] = jnp.full(m_scr.shape, NEG, F32)
    l_scr[...] = jnp.zeros(l_scr.shape, F32)
    scores(n_last, sa_scr)
    scores(jnp.maximum(n_last - 1, 0), sb_scr)
    update(n_last, sa_scr, True)

    def pair(j):
        kt = n_last - 1 - 2 * j
        scores(kt - 1, sa_scr)
        update(kt, sb_scr, False)
        scores(jnp.maximum(kt - 2, 0), sb_scr)
        update(kt - 1, sa_scr, False)
        return kt - 2

    n_pairs = n_last // 2
    if has_bias:
        def cond(state):
            j, go = state
            return (j < n_pairs) & go

        def body(state):
            j, _ = state
            return j + 1, live(pair(j))

        _, go = lax.while_loop(cond, body, (jnp.int32(0), live(n_last - 1)))
    else:
        lax.fori_loop(0, n_pairs, lambda j, c: (pair(j), c)[1], 0)
        go = True

    @pl.when((n_last % 2 == 1) & go)
    def _():
        update(0, sb_scr, False)

    ls = [l_scr[h] for h in range(N_HEADS)]
    for h in range(N_HEADS):
        rs = slice(h * HEAD_DIM, (h + 1) * HEAD_DIM)
        acc_scr[rs, :] = acc_scr[rs, :] / ls[h]
    o_ref[0] = acc_scr[...].T.astype(o_ref.dtype)


def _split3(x):
    def cut(v):
        bits = lax.bitcast_convert_type(v, jnp.uint32) & jnp.uint32(0xFFFF0000)
        return lax.bitcast_convert_type(bits, F32)

    hi = cut(x)
    r1 = x - hi
    mid = cut(r1)
    lo = r1 - mid
    return hi.astype(BF16), mid.astype(BF16), lo.astype(BF16)


def _causal_attn(q, k, v, tq, tk, sel=None, fcum=None, qk_bound=None):
    b, s, _ = q.shape
    nk = s // tk
    rows = N_HEADS * tq
    n_s = N_HEADS
    vt = v.reshape(b, nk, tk, W_MIX).transpose(0, 1, 3, 2)
    args = [q, k.reshape(b, nk, tk, W_MIX), vt]
    in_specs = [
        pl.BlockSpec((1, tq, W_MIX), lambda a, i: (a, i, 0)),
        pl.BlockSpec((1, nk, tk, W_MIX), lambda a, i: (a, 0, 0, 0)),
        pl.BlockSpec((1, nk, W_MIX, tk), lambda a, i: (a, 0, 0, 0)),
    ]
    if sel is not None:
        args.append(sel)
        in_specs.append(pl.BlockSpec((1, tq, sel.shape[2]), lambda a, i: (a, i, 0)))
    if fcum is not None:
        fcum = fcum * LOG2E
        qk_bound = qk_bound * LOG2E
        nf = 2 * 3 + 2
        ones = jnp.ones(fcum.shape, BF16)
        zero = jnp.zeros(fcum.shape, BF16)
        parts = _split3(fcum)
        key_f = jnp.stack([ones, ones, ones] + [-p for p in parts] + [zero, zero], axis=-1)
        qry_f = jnp.stack(list(parts) + [ones, ones, ones, zero, zero], axis=-1)
        fa = key_f.reshape(b, nk, tk, N_HEADS * nf)
        eye = jnp.eye(N_HEADS, dtype=BF16)
        fb = jnp.einsum('bshf,hg->bgshf', qry_f, eye).reshape(b, N_HEADS, s, N_HEADS * nf).transpose(0, 1, 3, 2)
        f_rows = fcum.transpose(0, 2, 1)
        f_end = f_rows[:, :, tk - 1::tk]
        thr = jnp.pad(f_end - (qk_bound + FOX_STOP * LOG2E), ((0, 0), (0, 0), (0, LANES - nk)),
                      constant_values=BIG).reshape(b, N_HEADS, 1, LANES)
        args += [fa, fb, f_rows.reshape(b, N_HEADS, 1, s), thr]
        in_specs += [pl.BlockSpec((1, nk, tk, N_HEADS * nf), lambda a, i: (a, 0, 0, 0)),
                     pl.BlockSpec((1, N_HEADS, N_HEADS * nf, tq), lambda a, i: (a, 0, 0, i)),
                     pl.BlockSpec((1, N_HEADS, 1, tq), lambda a, i: (a, 0, 0, i)),
                     pl.BlockSpec((1, N_HEADS, 1, LANES), lambda a, i: (a, 0, 0, 0))]
    return pl.pallas_call(
        functools.partial(_causal_kernel, tq=tq, tk=tk, has_sel=sel is not None, has_bias=fcum is not None),
        out_shape=jax.ShapeDtypeStruct((b, s, W_MIX), BF16),
        grid=(b, s // tq),
        in_specs=in_specs,
        out_specs=pl.BlockSpec((1, tq, W_MIX), lambda a, i: (a, i, 0)),
        scratch_shapes=[pltpu.VMEM((rows, W_MIX), BF16), pltpu.VMEM((W_MIX, tq), F32),
                        pltpu.VMEM((n_s, tk, tq), F32), pltpu.VMEM((n_s, tk, tq), F32),
                        pltpu.VMEM((N_HEADS, 1, tq), F32), pltpu.VMEM((N_HEADS, 1, tq), F32)],
        compiler_params=_params("arbitrary", "arbitrary"),
        name="causal_sel%d_bias%d" % (sel is not None, fcum is not None),
    )(*args)


def _window_kernel(*refs, tq, wk, pad, window, ls, emit_lse, other_dils, kv_lanes):
    it = iter(refs)
    q_ref, k_ref, v_ref = next(it), next(it), next(it)
    n_other = len(other_dils)
    others_in = [(next(it), next(it)) for _ in range(n_other)]
    wexp_ref = next(it) if n_other else None
    o_ref = next(it)
    lse_ref = next(it) if emit_lse else None
    qm_scr, acc_scr = next(it), next(it)
    others = []
    if n_other:
        og_scr, lg_scr = next(it), next(it)
        for g, (dg, (og_ref, lg_ref)) in enumerate(zip(other_dils, others_in)):
            for r in range(dg):
                rows = pl.ds(r, tq // dg, stride=dg)
                for c in range(W_MIX // LANES):
                    lo = r * W_MIX + c * LANES
                    og_scr[g, c, rows, :] = og_ref[0, :, lo:lo + LANES]
                lg_scr[g, rows, :] = lg_ref[0, :, r * LANES:(r + 1) * LANES]
            others.append((og_scr.at[g], lg_scr.at[g]))
    t0 = pl.program_id(2) * tq
    start = pl.multiple_of(jnp.clip(t0 - pad, 0, ls - wk), LANES)
    if kv_lanes is None:
        _stack_masked(q_ref[0], qm_scr, tq)
    else:
        lane = lax.broadcasted_iota(jnp.int32, (1, W_MIX), 1)
        on_key = (lane >= kv_lanes[0]) & (lane < kv_lanes[0] + HEAD_DIM)
        qf = q_ref[0].astype(F32)
        for h in range(N_HEADS):
            shift = (kv_lanes[0] - h * HEAD_DIM) % W_MIX
            rot = pltpu.roll(qf, shift, 1) if shift else qf
            qm_scr[h * tq:(h + 1) * tq, :] = jnp.where(on_key, rot, 0.0).astype(BF16)
    k_w = k_ref[0, pl.ds(start, wk), :]
    vt_w = v_ref[0, pl.ds(start, wk), :].astype(F32).T.astype(BF16)
    ss = [_dot(k_w, qm_scr[h * tq:(h + 1) * tq, :], NT) for h in range(N_HEADS)]
    t_pos = t0 + lax.broadcasted_iota(jnp.int32, (1, tq), 1)
    s_pos = start + lax.broadcasted_iota(jnp.int32, (wk, 1), 0)
    mask = (s_pos <= t_pos) & (t_pos - s_pos < window)
    ps, inv_ls, lses = [], [], []
    for h in range(N_HEADS):
        s = jnp.where(mask, ss[h], NEG)
        m = jnp.max(s, axis=0, keepdims=True)
        p = jnp.exp2(s - m)
        l = jnp.sum(p, axis=0, keepdims=True)
        ps.append(p.astype(BF16))
        inv_ls.append(1.0 / l)
        lses.append(m + jnp.log2(l))
    for h in range(N_HEADS):
        rs = slice(h * HEAD_DIM, (h + 1) * HEAD_DIM)
        vs = rs if kv_lanes is None else slice(kv_lanes[1], kv_lanes[1] + HEAD_DIM)
        acc_scr[rs, :] = _dot(vt_w[vs, :], ps[h]) * inv_ls[h]
    o_self = acc_scr[...].T
    if emit_lse or n_other:
        row = lax.broadcasted_iota(jnp.int32, (LANES, 1), 0)
        stat = jnp.zeros((LANES, tq), F32)
        for h in range(N_HEADS):
            stat = jnp.where(row == h, lses[h], stat)
        lse_tile = stat.T
    if n_other:
        lg_t = [lg[...].T for (_, lg) in others]
        row = lax.broadcasted_iota(jnp.int32, (LANES, 1), 0)
        wmat = jnp.zeros((LANES, tq), F32)
        for h in range(N_HEADS):
            group_lse = [lses[h]] + [t[h:h + 1, :] for t in lg_t]
            top = functools.reduce(jnp.maximum, group_lse)
            ws = [jnp.exp2(x - top) for x in group_lse]
            inv_den = 1.0 / functools.reduce(jnp.add, ws)
            for g, w in enumerate(ws):
                wmat = jnp.where(row == g * N_HEADS + h, w * inv_den, wmat)
        wt = wmat.T
        groups = [o_self] + [jnp.concatenate([og[c] for c in range(W_MIX // LANES)], axis=1)
                             for (og, _) in others]
        out = None
        for g, o_g in enumerate(groups):
            term = _dot_split(wt, wexp_ref[g]) * o_g
            out = term if out is None else out + term
        o_ref[0] = out.astype(o_ref.dtype)
    else:
        o_ref[0] = o_self.astype(o_ref.dtype)
    if emit_lse:
        lse_ref[0] = lse_tile


def _window_attn(q, k, v, *, dil, window, tq, out_dtype, emit_lse=False, others=(), kv_lanes=None):
    b, ls, _ = q.shape
    tq = min(tq, ls)
    pad = -(-(window - 1) // LANES) * LANES
    wk = min(tq + pad, ls)
    rows = N_HEADS * tq
    args = [q, k, v]
    in_specs = [
        pl.BlockSpec((1, tq, W_MIX), lambda a, r, i: (a, i, r)),
        pl.BlockSpec((1, ls, W_MIX), lambda a, r, i: (a, 0, r)),
        pl.BlockSpec((1, ls, W_MIX), lambda a, r, i: (a, 0, r)),
    ]
    scratch = [pltpu.VMEM((rows, W_MIX), BF16), pltpu.VMEM((W_MIX, tq), F32)]
    for (o_g, lse_g, dg) in others:
        args += [o_g, lse_g]
        in_specs += [pl.BlockSpec((1, tq // dg, dg * W_MIX), lambda a, r, i: (a, i, 0)),
                     pl.BlockSpec((1, tq // dg, dg * LANES), lambda a, r, i: (a, i, 0))]
    if others:
        wexp = np.zeros((len(others) + 1, LANES, W_MIX), np.float32)
        for g in range(len(others) + 1):
            for h in range(N_HEADS):
                wexp[g, g * N_HEADS + h, h * HEAD_DIM:(h + 1) * HEAD_DIM] = 1.0
        args.append(jnp.asarray(wexp, BF16))
        in_specs.append(pl.BlockSpec(wexp.shape, lambda a, r, i: (0, 0, 0)))
        scratch += [pltpu.VMEM((len(others), W_MIX // LANES, tq, LANES), F32),
                    pltpu.VMEM((len(others), tq, LANES), F32)]
    out_shape = [jax.ShapeDtypeStruct((b, ls, dil * W_MIX), out_dtype)]
    out_specs = [pl.BlockSpec((1, tq, W_MIX), lambda a, r, i: (a, i, r))]
    if emit_lse:
        out_shape.append(jax.ShapeDtypeStruct((b, ls, dil * LANES), F32))
        out_specs.append(pl.BlockSpec((1, tq, LANES), lambda a, r, i: (a, i, r)))
    res = pl.pallas_call(
        functools.partial(_window_kernel, tq=tq, wk=wk, pad=pad, window=window, ls=ls,
                          emit_lse=emit_lse, other_dils=tuple(dg for (_, _, dg) in others),
                          kv_lanes=kv_lanes),
        out_shape=out_shape,
        grid=(b, dil, ls // tq),
        in_specs=in_specs,
        out_specs=out_specs,
        scratch_shapes=scratch,
        compiler_params=_params("arbitrary", "arbitrary", "arbitrary"),
        name="window_d%d_w%d" % (dil, window),
    )(*args)
    return (res[0], res[1], dil) if emit_lse else res[0]


def _sb_kernel(q_ref, k_ref, vt_ref, tri_ref, o_ref, qm_scr, carry_scr, acc_scr, *, tq):
    i = pl.program_id(1)
    t0 = i * tq
    _stack_masked(q_ref[0], qm_scr, tq)
    carry_scr[...] = jnp.zeros(carry_scr.shape, F32)
    acc_scr[...] = jnp.zeros(acc_scr.shape, F32)
    t_pos = t0 + lax.broadcasted_iota(jnp.int32, (1, tq), 1)

    def tile(kt, diag):
        k_t = k_ref[0, kt]
        vt_t = vt_ref[0, kt]
        tri = tri_ref[...]
        zs = [_dot(k_t, qm_scr[h * tq:(h + 1) * tq, :], NT) for h in range(N_HEADS)]
        if diag:
            s_pos = kt * tq + lax.broadcasted_iota(jnp.int32, (tq, 1), 0)
            strict = s_pos < t_pos
        weights = []
        for h in range(N_HEADS):
            z = zs[h]
            lg = -(jnp.maximum(z, 0.0) + jnp.log(1.0 + jnp.exp(-jnp.abs(z))))
            if diag:
                lg = jnp.where(strict, lg, 0.0)
            hi = lg.astype(BF16)
            lo = (lg - hi.astype(F32)).astype(BF16)
            cum = _dot(tri, hi) + _dot(tri, lo) + carry_scr[h]
            a = jnp.exp(z + cum)
            if diag:
                a = jnp.where(strict, a, 0.0)
            weights.append(a.astype(BF16))
            carry_scr[h] += jnp.sum(lg, axis=0, keepdims=True)
        for h in range(N_HEADS):
            rs = slice(h * HEAD_DIM, (h + 1) * HEAD_DIM)
            acc_scr[rs, :] += _dot(vt_t[rs, :], weights[h])

    tile(i, True)

    def cond(state):
        j, top = state
        return (j < i) & (top > SB_STOP)

    def body(state):
        j, _ = state
        tile(i - 1 - j, False)
        return j + 1, jnp.max(carry_scr[...])

    lax.while_loop(cond, body, (jnp.int32(0), jnp.max(carry_scr[...])))
    o_ref[0] = acc_scr[...].T.astype(o_ref.dtype)


def _stick_breaking(q, k, v, tq):
    b, s, _ = q.shape
    nk = s // tq
    rows = N_HEADS * tq
    tri = jnp.asarray(np.triu(np.ones((tq, tq), np.float32)), BF16)
    vt = v.reshape(b, nk, tq, W_MIX).transpose(0, 1, 3, 2)
    return pl.pallas_call(
        functools.partial(_sb_kernel, tq=tq),
        out_shape=jax.ShapeDtypeStruct((b, s, W_MIX), BF16),
        grid=(b, nk),
        in_specs=[
            pl.BlockSpec((1, tq, W_MIX), lambda a, i: (a, i, 0)),
            pl.BlockSpec((1, nk, tq, W_MIX), lambda a, i: (a, 0, 0, 0)),
            pl.BlockSpec((1, nk, W_MIX, tq), lambda a, i: (a, 0, 0, 0)),
            pl.BlockSpec((tq, tq), lambda a, i: (0, 0)),
        ],
        out_specs=pl.BlockSpec((1, tq, W_MIX), lambda a, i: (a, i, 0)),
        scratch_shapes=[pltpu.VMEM((rows, W_MIX), BF16), pltpu.VMEM((N_HEADS, 1, tq), F32),
                        pltpu.VMEM((W_MIX, tq), F32)],
        compiler_params=_params("arbitrary", "arbitrary"),
        name="stick_breaking",
    )(q, k.reshape(b, nk, tq, W_MIX), vt, tri)


def _foxcum_kernel(x_ref, b_ref, tri_ref, o_ref, carry_scr):
    @pl.when(pl.program_id(1) == 0)
    def _():
        carry_scr[...] = jnp.zeros(carry_scr.shape, F32)

    z = x_ref[0] + b_ref[...]
    logf = jnp.minimum(z, 0.0) - jnp.log(1.0 + jnp.exp(-jnp.abs(z)))
    cum = _dot(tri_ref[...], logf, NN, HI) + carry_scr[...]
    o_ref[0] = cum
    carry_scr[...] = cum[cum.shape[0] - 1:, :]


def _fox_cumsum(misc, bias_vec, tc):
    b, s, w = misc.shape
    tri = jnp.asarray(np.tril(np.ones((tc, tc), np.float32)))
    return pl.pallas_call(
        _foxcum_kernel,
        out_shape=jax.ShapeDtypeStruct((b, s, w), F32),
        grid=(b, s // tc),
        in_specs=[
            pl.BlockSpec((1, tc, w), lambda a, i: (a, i, 0)),
            pl.BlockSpec((1, w), lambda a, i: (0, 0)),
            pl.BlockSpec((tc, tc), lambda a, i: (0, 0)),
        ],
        out_specs=pl.BlockSpec((1, tc, w), lambda a, i: (a, i, 0)),
        scratch_shapes=[pltpu.VMEM((1, w), F32)],
        compiler_params=_params("arbitrary", "arbitrary"),
        name="fox_cumsum",
    )(misc, bias_vec, tri)


def _merge_kernel(x_ref, g_ref, sc_ref, sh_ref, ga_ref, wm_ref, misc_ref, pg_ref,
                  ocmp_ref, osel_ref, owin_ref, ob_ref, oc_ref, od_ref,
                  wa_ref, wb_ref, wc_ref, wd_ref, wo_ref, o_ref):
    x = x_ref[...]
    d = x.shape[1]
    h = _mod_norm(x, g_ref[...], sc_ref[0], sh_ref[0]).astype(BF16)
    gate = jax.nn.sigmoid(misc_ref[...])
    o_a = (_dot_split(gate, pg_ref[0]) * ocmp_ref[...].astype(F32)
           + _dot_split(gate, pg_ref[1]) * osel_ref[...].astype(F32)
           + _dot_split(gate, pg_ref[2]) * owin_ref[...].astype(F32)).astype(BF16)
    mixed = jnp.zeros(x.shape, F32)
    for m, (o_m, w_ref) in enumerate(((o_a, wa_ref), (ob_ref[...], wb_ref),
                                      (oc_ref[...], wc_ref), (od_ref[...], wd_ref))):
        y = _dot(o_m, w_ref[...])
        gl = _dot(h, wm_ref[:, m * d:(m + 1) * d])
        mixed = mixed + jax.nn.sigmoid(gl) * y
    o_ref[...] = x + ga_ref[0] * _dot(mixed.astype(BF16), wo_ref[...])


def _merge(x2, g, sc, sh, ga, w_merge, misc, pg, o_cmp, o_sel, o_win, o_b, o_c, o_d,
           wa, wb, wc, wd, wo, seq, tt):
    n, d = x2.shape
    tpb = seq // tt
    row = lambda w: pl.BlockSpec((tt, w), lambda i: (i, 0))
    full = lambda a: pl.BlockSpec(a.shape, lambda i: (0,) * a.ndim)
    per_b = pl.BlockSpec((1, 1, d), lambda i: (i // tpb, 0, 0))
    return pl.pallas_call(
        _merge_kernel,
        out_shape=jax.ShapeDtypeStruct((n, d), F32),
        grid=(n // tt,),
        in_specs=[row(d), full(g), per_b, per_b, per_b, full(w_merge), row(W_MIX), full(pg)]
        + [row(W_MIX)] * 6 + [full(wa), full(wb), full(wc), full(wd), full(wo)],
        out_specs=row(d),
        compiler_params=_params("arbitrary"),
        name="merge_out",
    )(x2, g, sc, sh, ga, w_merge, misc, pg, o_cmp, o_sel, o_win, o_b, o_c, o_d, wa, wb, wc, wd, wo)


def _ffn_kernel(x_ref, g_ref, sc_ref, sh_ref, gf_ref, w1_ref, w3_ref, w2_ref, o_ref, h_scr, acc_scr):
    f = pl.program_id(1)

    @pl.when(f == 0)
    def _():
        h_scr[...] = _mod_norm(x_ref[...], g_ref[...], sc_ref[0], sh_ref[0]).astype(BF16)
        acc_scr[...] = jnp.zeros(acc_scr.shape, F32)

    h = h_scr[...]
    a = _dot(h, w1_ref[...])
    b = _dot(h, w3_ref[...])
    acc_scr[...] += _dot((a * jax.nn.sigmoid(a) * b).astype(BF16), w2_ref[...])

    @pl.when(f == pl.num_programs(1) - 1)
    def _():
        o_ref[...] = x_ref[...] + gf_ref[0] * acc_scr[...]


def _ffn(x2, g, sc, sh, gf, w1, w3, w2, seq, tt, tf):
    n, d = x2.shape
    dff = w1.shape[1]
    tpb = seq // tt
    per_b = pl.BlockSpec((1, 1, d), lambda i, f: (i // tpb, 0, 0))
    return pl.pallas_call(
        _ffn_kernel,
        out_shape=jax.ShapeDtypeStruct((n, d), F32),
        grid=(n // tt, dff // tf),
        in_specs=[
            pl.BlockSpec((tt, d), lambda i, f: (i, 0)),
            pl.BlockSpec((1, d), lambda i, f: (0, 0)),
            per_b, per_b, per_b,
            pl.BlockSpec((d, tf), lambda i, f: (0, f)),
            pl.BlockSpec((d, tf), lambda i, f: (0, f)),
            pl.BlockSpec((tf, d), lambda i, f: (f, 0)),
        ],
        out_specs=pl.BlockSpec((tt, d), lambda i, f: (i, 0)),
        scratch_shapes=[pltpu.VMEM((tt, d), BF16), pltpu.VMEM((tt, d), F32)],
        compiler_params=_params("arbitrary", "arbitrary"),
        name="ffn_swiglu",
    )(x2, g, sc, sh, gf, w1, w3, w2)


def _route_kernel(x_ref, g_ref, sc_ref, sh_ref, rw_ref, up_ref, h_ref, rank_ref, gate_ref, cnt_ref):
    hf = _mod_norm(x_ref[...], g_ref[...], sc_ref[0], sh_ref[0])
    h_ref[...] = hf.astype(BF16)
    logits = _dot(rw_ref[...], hf, NT, HI)
    ne, tt = logits.shape
    e_idx = lax.broadcasted_iota(jnp.int32, (ne, 1), 0).astype(F32)
    v1 = jnp.max(logits, axis=0, keepdims=True)
    i1 = jnp.min(jnp.where(logits == v1, e_idx, float(ne)), axis=0, keepdims=True)
    m1 = e_idx == i1
    rest = jnp.where(m1, -3e38, logits)
    v2 = jnp.max(rest, axis=0, keepdims=True)
    i2 = jnp.min(jnp.where(rest == v2, e_idx, float(ne)), axis=0, keepdims=True)
    m2 = e_idx == i2
    e2 = jnp.exp(v2 - v1)
    g1 = 1.0 / (1.0 + e2)
    g2 = e2 / (1.0 + e2)
    routed = m1 | m2
    rf = jnp.where(routed, 1.0, 0.0)
    rank = _dot(rf.astype(BF16), up_ref[...])
    rank = jnp.where(routed, rank, -1.0)
    gate = jnp.where(m1, g1, 0.0) + jnp.where(m2, g2, 0.0)
    for e in range(ne):
        rank_ref[0, e] = rank[e:e + 1, :]
        gate_ref[0, e] = gate[e:e + 1, :]
    cnt = jnp.sum(rf, axis=1, keepdims=True)
    cnt_ref[0] = jnp.broadcast_to(cnt, (ne, LANES))


def _route(x2, g, sc, sh, rw_t, seq, tt):
    n, d = x2.shape
    ne = rw_t.shape[0]
    tpb = seq // tt
    nt = n // tt
    upper = jnp.asarray(np.triu(np.ones((tt, tt), np.float32), 1), BF16)
    per_b = pl.BlockSpec((1, 1, d), lambda i: (i // tpb, 0, 0))
    return pl.pallas_call(
        _route_kernel,
        out_shape=[jax.ShapeDtypeStruct((n, d), BF16),
                   jax.ShapeDtypeStruct((nt, ne, 1, tt), F32),
                   jax.ShapeDtypeStruct((nt, ne, 1, tt), F32),
                   jax.ShapeDtypeStruct((nt, ne, LANES), F32)],
        grid=(nt,),
        in_specs=[
            pl.BlockSpec((tt, d), lambda i: (i, 0)),
            pl.BlockSpec((1, d), lambda i: (0, 0)),
            per_b, per_b,
            pl.BlockSpec((ne, d), lambda i: (0, 0)),
            pl.BlockSpec((tt, tt), lambda i: (0, 0)),
        ],
        out_specs=[pl.BlockSpec((tt, d), lambda i: (i, 0)),
                   pl.BlockSpec((1, ne, 1, tt), lambda i: (i, 0, 0, 0)),
                   pl.BlockSpec((1, ne, 1, tt), lambda i: (i, 0, 0, 0)),
                   pl.BlockSpec((1, ne, LANES), lambda i: (i, 0, 0))],
        compiler_params=_params("arbitrary"),
        name="moe_route",
    )(x2, g, sc, sh, rw_t, upper)


def _moe_kernel(cnt_ref, x_ref, gf_ref, h_ref, rank_ref, gate_ref, w1_ref, w3_ref, w2_ref,
                o_ref, acc_scr, xs_scr, y_scr, *, chunk):
    i, e, f = pl.program_id(0), pl.program_id(1), pl.program_id(2)
    ne, nf = pl.num_programs(1), pl.num_programs(2)

    @pl.when((e == 0) & (f == 0))
    def _():
        acc_scr[...] = jnp.zeros(acc_scr.shape, F32)

    count = cnt_ref[i * ne + e]
    rank = rank_ref[0, 0]
    gate = gate_ref[0, 0]
    n_small = (count + chunk - 1) // chunk
    n_big = (count + 2 * chunk - 1) // (2 * chunk)

    def one_hot(c, rows):
        r = c * rows + lax.broadcasted_iota(jnp.int32, (rows, 1), 0)
        return rank == r.astype(F32)

    def rows_of(c, rows):
        return pl.ds(pl.multiple_of(c * rows, rows), rows)

    @pl.when(f == 0)
    def _():
        h = h_ref[...]

        def gather(c, carry):
            p = jnp.where(one_hot(c, chunk), 1.0, 0.0).astype(BF16)
            xs_scr[rows_of(c, chunk), :] = _dot(p, h).astype(BF16)
            return carry

        lax.fori_loop(0, n_small, gather, 0)

        def clear(c, carry):
            y_scr[rows_of(c, 2 * chunk), :] = jnp.zeros((2 * chunk, y_scr.shape[1]), F32)
            return carry

        lax.fori_loop(0, n_big, clear, 0)

    def expert(c, carry):
        xs = xs_scr[rows_of(c, chunk), :]
        a = _dot(xs, w1_ref[0])
        b = _dot(xs, w3_ref[0])
        y_scr[rows_of(c, chunk), :] += _dot((a * jax.nn.sigmoid(a) * b).astype(BF16), w2_ref[0])
        return carry

    lax.fori_loop(0, n_small, expert, 0)

    @pl.when(f == nf - 1)
    def _():
        def scatter(c, carry):
            hit = one_hot(c, 2 * chunk)
            p = jnp.where(hit, 1.0, 0.0).astype(BF16)
            gcol = jnp.sum(jnp.where(hit, gate, 0.0), axis=-1, keepdims=True)
            acc_scr[...] += _dot(p, (y_scr[rows_of(c, 2 * chunk), :] * gcol).astype(BF16), TN)
            return carry

        lax.fori_loop(0, n_big, scatter, 0)

    @pl.when((e == ne - 1) & (f == nf - 1))
    def _():
        o_ref[...] = x_ref[...] + gf_ref[0] * acc_scr[...]


def _moe(counts, x2, gf, h2, rank, gate, w1, w3, w2, seq, tt, tf, chunk):
    n, d = x2.shape
    ne, _, dff = w1.shape
    tpb = seq // tt
    grid_spec = pltpu.PrefetchScalarGridSpec(
        num_scalar_prefetch=1,
        grid=(n // tt, ne, dff // tf),
        in_specs=[
            pl.BlockSpec((tt, d), lambda i, e, f, c: (i, 0)),
            pl.BlockSpec((1, 1, d), lambda i, e, f, c: (i // tpb, 0, 0)),
            pl.BlockSpec((tt, d), lambda i, e, f, c: (i, 0)),
            pl.BlockSpec((1, 1, 1, tt), lambda i, e, f, c: (i, e, 0, 0)),
            pl.BlockSpec((1, 1, 1, tt), lambda i, e, f, c: (i, e, 0, 0)),
            pl.BlockSpec((1, d, tf), lambda i, e, f, c: (e, 0, f)),
            pl.BlockSpec((1, d, tf), lambda i, e, f, c: (e, 0, f)),
            pl.BlockSpec((1, tf, d), lambda i, e, f, c: (e, f, 0)),
        ],
        out_specs=pl.BlockSpec((tt, d), lambda i, e, f, c: (i, 0)),
        scratch_shapes=[pltpu.VMEM((tt, d), F32), pltpu.VMEM((tt, d), BF16), pltpu.VMEM((tt, d), F32)],
    )
    return pl.pallas_call(
        functools.partial(_moe_kernel, chunk=chunk),
        out_shape=jax.ShapeDtypeStruct((n, d), F32),
        grid_spec=grid_spec,
        compiler_params=_params("arbitrary", "arbitrary", "arbitrary"),
        name="moe_experts",
    )(counts, x2, gf, h2, rank, gate, w1, w3, w2)


def _overlap_matrix(ncp, nc, n_sel):
    c0 = np.arange(ncp) * CMP_STRIDE
    c1 = c0 + CMP_LEN
    s0 = np.arange(n_sel) * SEL_LEN
    s1 = s0 + SEL_LEN
    ov = ((c0[:, None] < s1[None, :]) & (c1[:, None] > s0[None, :])).astype(np.float32)
    ov[nc:] = 0.0
    return jnp.asarray(ov.T)


def _gate_expand():
    pg = np.zeros((3, W_MIX, W_MIX), np.float32)
    for br in range(3):
        for h in range(N_HEADS):
            pg[br, GATE_LANE + 3 * h + br, h * HEAD_DIM:(h + 1) * HEAD_DIM] = 1.0
    return jnp.asarray(pg, BF16)


def _mixer_layer(x2, b, s, mod, norm_g, rope, w_in, qk_gain, pe_k, pe_v, ck1, ck2, cv1, cv2,
                 fox_b, w_branch, w_out):
    n, d = x2.shape
    sh_a, sc_a, g_a = mod[0], mod[1], mod[2]
    w_slab, gain, w_merge = _pack_w_in(w_in, qk_gain)
    bd = jnp.asarray(np.kron(np.eye(N_HEADS), np.full((HEAD_DIM, HEAD_DIM), 1.0 / HEAD_DIM)), BF16)
    outs = _proj(x2, norm_g, sc_a, sh_a, w_slab, gain, bd, rope, s, min(512, s))
    sl = [a.reshape(b, s, W_MIX) for a in outs[:N_SLABS]]
    dil_in = {1: (sl[S_BQ], sl[S_BK], sl[S_BV])}
    for di, dil in enumerate(_DILATIONS):
        dil_in[dil] = tuple(outs[N_SLABS + si * len(_DILATIONS) + di].reshape(b, s // dil, dil * W_MIX)
                            for si in range(len(_DIL_SLABS)))
    misc = sl[S_MISC]

    nch = s // CMP_STRIDE
    nc = nch - CMP_LEN // CMP_STRIDE + 1
    kc_raw = sl[S_KC][..., :HEAD_DIM]
    vc_raw = misc[..., VC_LANE:VC_LANE + HEAD_DIM]
    chunks = jnp.stack([kc_raw, vc_raw]).reshape(2, b, nch, CMP_STRIDE * HEAD_DIM)
    chunks_next = jnp.concatenate([chunks[:, :, 1:], jnp.zeros_like(chunks[:, :, :1])], axis=2)
    pe = jnp.stack([pe_k, pe_v]).reshape(2, 1, CMP_LEN * HEAD_DIM)
    kvc = jnp.tile(_compress(chunks, chunks_next, pe, jnp.stack([ck1, cv1]), jnp.stack([ck2, cv2])),
                   (1, 1, 1, N_HEADS))
    overlap = _overlap_matrix(nch, nc, s // SEL_LEN)
    o_cmp, selmask = _cmp_topk(sl[S_QNR], kvc[0], kvc[1].transpose(0, 2, 1).astype(BF16), overlap,
                               min(256, s), nc)
    tq, tk = min(256, s), min(512, s)
    o_sel = _causal_attn(sl[S_QR], sl[S_KV], sl[S_KV], tq, tk, sel=selmask)
    o_win = _window_attn(sl[S_QR], sl[S_KV], sl[S_KV], dil=1, window=NSA_WINDOW, tq=256, out_dtype=BF16,
                         kv_lanes=(KW_LANE, VW_LANE))

    others = []
    for (wdw, dil) in DIL_CONFIGS[:0:-1]:
        others.append(_window_attn(*dil_in[dil], dil=dil, window=wdw // dil + 1, tq=256,
                                   out_dtype=F32, emit_lse=True))
    wdw, dil = DIL_CONFIGS[0]
    o_b = _window_attn(*dil_in[dil], dil=dil, window=wdw // dil + 1, tq=256, out_dtype=BF16, others=others)

    o_c = _stick_breaking(sl[S_CQ], sl[S_CK], sl[S_CV], min(256, s))

    bias_vec = jnp.zeros((1, W_MIX), F32).at[0, FOX_LANE:FOX_LANE + N_HEADS].set(fox_b)
    fcum = _fox_cumsum(misc, bias_vec, min(512, s))
    qk_bound = 1.02 * HEAD_DIM ** 0.5 * jnp.max(jnp.abs(qk_gain[6])) * jnp.max(jnp.abs(qk_gain[7])) + 0.05
    o_d = _causal_attn(sl[S_DQ], sl[S_DK], sl[S_DV], tq, tk, fcum=fcum[..., FOX_LANE:FOX_LANE + N_HEADS],
                       qk_bound=qk_bound)

    wb16 = w_branch.astype(BF16)
    flat = lambda a: a.reshape(n, W_MIX)
    return _merge(x2, norm_g, sc_a, sh_a, g_a, w_merge, flat(misc), _gate_expand(),
                  flat(o_cmp), flat(o_sel), flat(o_win), flat(o_b), flat(o_c), flat(o_d),
                  wb16[0], wb16[1], wb16[2], wb16[3], w_out.astype(BF16), s, min(512, s))


def kernel(x, c, positions, w_ada, b_ada, norm_mix, norm_ffn, w_in, qk_gain, nsa_pe_k, nsa_pe_v,
           nsa_ck_w1, nsa_ck_w2, nsa_cv_w1, nsa_cv_w2, fox_bias, w_branch, w_out,
           ffn_w1, ffn_w3, ffn_w2, router_w, moe_w1, moe_w3, moe_w2):
    b, s, d = x.shape
    depth = w_ada.shape[0]
    rope = _rope_tables(positions)
    mods = _ada(c, w_ada, b_ada).reshape(depth, b, 6, 1, d).transpose(0, 2, 1, 3, 4)
    x2 = x.reshape(b * s, d)
    for l in range(depth):
        mod = mods[l]
        x2 = _mixer_layer(x2, b, s, mod[0:3], norm_mix[l].reshape(1, d), rope, w_in[l], qk_gain[l],
                          nsa_pe_k[l], nsa_pe_v[l], nsa_ck_w1[l], nsa_ck_w2[l], nsa_cv_w1[l],
                          nsa_cv_w2[l], fox_bias[l], w_branch[l], w_out[l])
        sh_f, sc_f, g_f = mod[3], mod[4], mod[5]
        gn = norm_ffn[l].reshape(1, d)
        e = l // 2
        if l % 2 == 0:
            dff = ffn_w1.shape[2]
            x2 = _ffn(x2, gn, sc_f, sh_f, g_f, ffn_w1[e].astype(BF16), ffn_w3[e].astype(BF16),
                      ffn_w2[e].astype(BF16), s, min(1024, s), dff // 2)
        else:
            tt = min(1024, s)
            dff = moe_w1.shape[3]
            h2, rank, gate, cnt = _route(x2, gn, sc_f, sh_f, router_w[e].T, s, tt)
            counts = cnt[:, :, 0].astype(jnp.int32).reshape(-1)
            x2 = _moe(counts, x2, g_f, h2, rank, gate, moe_w1[e].astype(BF16), moe_w3[e].astype(BF16),
                      moe_w2[e].astype(BF16), s, tt, dff // 2, 128)
    return x2.reshape(b, s, d)
```

```python
import functools

import numpy as np
import jax
import jax.numpy as jnp
from jax import lax
from jax.experimental import pallas as pl
from jax.experimental.pallas import tpu as pltpu

F32 = jnp.float32
BF16 = jnp.bfloat16
HI = lax.Precision.HIGHEST

LANES = 128
VMEM_LIMIT = 52 * 1024 * 1024

HEAD_DIM = 64
HEAD_SHIFT = 6
N_HEADS = 4
W_MIX = N_HEADS * HEAD_DIM
N_MIXERS = 4
ROPE_THETA = 500000.0
ROPE_DIMS = HEAD_DIM // 4
ROPE_HALF = ROPE_DIMS // 2
EPS = 1e-6
LOG2E = 1.4426950408889634
NEG = -1e30
BIG = 1e30
CMP_LEN = 32
CMP_STRIDE = 16
SEL_LEN = 64
SEL_SHIFT = 6
TOPN = 16
SEL_NEG = 2.0 ** 30
NSA_WINDOW = 512
DIL_CONFIGS = ((128, 1), (512, 4), (2048, 16))
SB_STOP = -110.0
FOX_STOP = 108.0
GATE_LANE = 0
FOX_LANE = 3 * N_HEADS
VC_LANE = HEAD_DIM

NN = (((1,), (0,)), ((), ()))
NT = (((1,), (1,)), ((), ()))
TN = (((0,), (0,)), ((), ()))


def _dot(a, b, dims=NN, precision=None):
    return lax.dot_general(a, b, dims, precision=precision, preferred_element_type=F32)


def _dot_split(a, b_bf16, dims=NN):
    hi = a.astype(BF16)
    lo = (a - hi.astype(F32)).astype(BF16)
    return _dot(hi, b_bf16, dims) + _dot(lo, b_bf16, dims)


def _params(*sem):
    return pltpu.CompilerParams(dimension_semantics=sem, vmem_limit_bytes=VMEM_LIMIT)


def _mod_norm(x, g, sc, sh):
    ms = jnp.mean(x * x, axis=-1, keepdims=True)
    return (x * lax.rsqrt(ms + EPS) * g) * (1.0 + sc) + sh


def _head_masks():
    lane = lax.broadcasted_iota(jnp.int32, (1, W_MIX), 1)
    return [(lane >> HEAD_SHIFT) == h for h in range(N_HEADS)]


def _stack_masked(q, qm_scr, tq):
    for h, hm in enumerate(_head_masks()):
        qm_scr[h * tq:(h + 1) * tq, :] = jnp.where(hm, q, jnp.zeros_like(q))


def _pick_heads(stacked, tq, scale=None):
    out = None
    for h, hm in enumerate(_head_masks()):
        blk = stacked[h * tq:(h + 1) * tq, :]
        if scale is not None:
            blk = blk * scale[h]
        out = jnp.where(hm, blk, 0.0 if out is None else out)
    return out


def _ada_kernel(c_ref, w_ref, b_ref, o_ref):
    c = c_ref[...]
    ca = c * jax.nn.sigmoid(c)
    o_ref[0] = _dot(ca, w_ref[0], NN, HI) + b_ref[0]


def _ada(c, w_ada, b_ada):
    depth, d, n6 = w_ada.shape
    b = c.shape[0]
    tn = n6 // 4
    return pl.pallas_call(
        _ada_kernel,
        out_shape=jax.ShapeDtypeStruct((depth, b, n6), F32),
        grid=(depth, n6 // tn),
        in_specs=[
            pl.BlockSpec((b, d), lambda l, j: (0, 0)),
            pl.BlockSpec((1, d, tn), lambda l, j: (l, 0, j)),
            pl.BlockSpec((1, 1, tn), lambda l, j: (l, 0, j)),
        ],
        out_specs=pl.BlockSpec((1, b, tn), lambda l, j: (l, 0, j)),
        compiler_params=_params("arbitrary", "arbitrary"),
        name="ada_mod",
    )(c, w_ada, b_ada.reshape(depth, 1, n6))


(W_QA, W_KC, W_KV, W_BQ, W_BK, W_BV, W_CQ, W_CK, W_CV, W_DQ, W_DK, W_DV, W_MISC) = range(13)
N_WSLABS = 13
KV_QK_LANES = 2 * HEAD_DIM
_SLABS = (
    (W_QA, "all", None, F32), (W_QA, "all", "all", BF16), (W_KC, "all", None, F32),
    (W_KV, "kv", "kv", BF16),
    (W_BQ, "all", "all", BF16), (W_BK, "all", "all", BF16), (W_BV, None, None, BF16),
    (W_CQ, None, None, BF16), (W_CK, None, None, BF16), (W_CV, None, None, BF16),
    (W_DQ, "all", None, BF16), (W_DK, "all", None, BF16), (W_DV, None, None, BF16),
    (W_MISC, None, None, F32),
)
N_SLABS = len(_SLABS)
(S_QNR, S_QR, S_KC, S_KV, S_BQ, S_BK, S_BV, S_CQ, S_CK, S_CV, S_DQ, S_DK, S_DV, S_MISC) = range(N_SLABS)
_DIL_SLABS = (S_BQ, S_BK, S_BV)
_DILATIONS = tuple(d for (_, d) in DIL_CONFIGS if d > 1)
KSL_LANE, KW_LANE, VSL_LANE, VW_LANE = (i * HEAD_DIM for i in range(4))


def _proj_kernel(x_ref, g_ref, sc_ref, sh_ref, w_ref, gain_ref, bd_ref, rope_ref, *out_refs):
    h = _mod_norm(x_ref[...], g_ref[...], sc_ref[0], sh_ref[0]).astype(BF16)
    bd = bd_ref[...]
    key_lanes = lax.broadcasted_iota(jnp.int32, (1, W_MIX), 1) < KV_QK_LANES
    last_w, y_n = None, None
    raw = [_dot(h, w_ref[:, wi * W_MIX:(wi + 1) * W_MIX]) for wi in range(N_WSLABS)]
    for s, (wi, norm, rope, _) in enumerate(_SLABS):
        if wi != last_w:
            y_n = raw[wi]
            if norm:
                normed = y_n * lax.rsqrt(_dot_split(y_n * y_n, bd) + EPS)
                y_n = normed if norm == "all" else jnp.where(key_lanes, normed, y_n)
            last_w = wi
        y = y_n * gain_ref[s]
        if rope:
            c, s1, s2 = rope_ref[0, 0], rope_ref[0, 1], rope_ref[0, 2]
            if rope == "kv":
                c, s1, s2 = jnp.where(key_lanes, c, 1.0), jnp.where(key_lanes, s1, 0.0), jnp.where(key_lanes, s2, 0.0)
            y = y * c + pltpu.roll(y, W_MIX - ROPE_HALF, 1) * s1 + pltpu.roll(y, ROPE_HALF, 1) * s2
        out_refs[s][...] = y.astype(out_refs[s].dtype)
        if s in _DIL_SLABS:
            stage_scr = out_refs[-1]
            tt = y.shape[0]
            for c in range(W_MIX // LANES):
                stage_scr[c] = y[:, c * LANES:(c + 1) * LANES]
            for di, dil in enumerate(_DILATIONS):
                o_ref = out_refs[N_SLABS + _DIL_SLABS.index(s) * len(_DILATIONS) + di]
                for r in range(dil):
                    for c in range(W_MIX // LANES):
                        lo = r * W_MIX + c * LANES
                        o_ref[:, lo:lo + LANES] = (
                            stage_scr[c, pl.ds(r, tt // dil, stride=dil), :].astype(o_ref.dtype))


def _proj(x2, g, sc, sh, w_slab, gain, bd, rope, seq, tt):
    n, d = x2.shape
    tpb = seq // tt
    out_shape = [jax.ShapeDtypeStruct((n, W_MIX), dt) for (_, _, _, dt) in _SLABS]
    out_specs = [pl.BlockSpec((tt, W_MIX), lambda i: (i, 0)) for _ in _SLABS]
    for _ in _DIL_SLABS:
        for dil in _DILATIONS:
            out_shape.append(jax.ShapeDtypeStruct((n // dil, dil * W_MIX), BF16))
            out_specs.append(pl.BlockSpec((tt // dil, dil * W_MIX), lambda i: (i, 0)))
    return pl.pallas_call(
        _proj_kernel,
        out_shape=out_shape,
        grid=(n // tt,),
        in_specs=[
            pl.BlockSpec((tt, d), lambda i: (i, 0)),
            pl.BlockSpec((1, d), lambda i: (0, 0)),
            pl.BlockSpec((1, 1, d), lambda i: (i // tpb, 0, 0)),
            pl.BlockSpec((1, 1, d), lambda i: (i // tpb, 0, 0)),
            pl.BlockSpec((d, N_WSLABS * W_MIX), lambda i: (0, 0)),
            pl.BlockSpec((N_SLABS, 1, W_MIX), lambda i: (0, 0, 0)),
            pl.BlockSpec((W_MIX, W_MIX), lambda i: (0, 0)),
            pl.BlockSpec((1, 3, tt, W_MIX), lambda i: (i // tpb, 0, i % tpb, 0)),
        ],
        out_specs=out_specs,
        scratch_shapes=[pltpu.VMEM((W_MIX // LANES, tt, LANES), F32)],
        compiler_params=_params("arbitrary"),
        name="in_proj",
    )(x2, g, sc, sh, w_slab, gain, bd, rope)


def _pack_w_in(w_in, qk_gain):
    d = w_in.shape[0]
    o = 0
    a_q = w_in[:, o:o + W_MIX]; o += W_MIX
    a_kv = w_in[:, o:o + 6 * HEAD_DIM]; o += 6 * HEAD_DIM
    a_g = w_in[:, o:o + 3 * N_HEADS]; o += 3 * N_HEADS
    b_qkv = w_in[:, o:o + 3 * W_MIX]; o += 3 * W_MIX
    c_qkv = w_in[:, o:o + 3 * W_MIX]; o += 3 * W_MIX
    d_qkv = w_in[:, o:o + 3 * W_MIX]; o += 3 * W_MIX
    d_f = w_in[:, o:o + N_HEADS]; o += N_HEADS
    w_merge = w_in[:, o:]
    kc, vc, ksl, vsl, kw, vw = (a_kv[:, i * HEAD_DIM:(i + 1) * HEAD_DIM] for i in range(6))
    zeros = lambda w: jnp.zeros((d, w), w_in.dtype)
    misc = jnp.concatenate([a_g, d_f, zeros(VC_LANE - FOX_LANE - N_HEADS), vc, zeros(W_MIX - 2 * HEAD_DIM)], axis=1)
    slabs = [
        a_q, jnp.concatenate([kc, zeros(W_MIX - HEAD_DIM)], axis=1),
        jnp.concatenate([ksl, kw, vsl, vw], axis=1),
        b_qkv[:, :W_MIX], b_qkv[:, W_MIX:2 * W_MIX], b_qkv[:, 2 * W_MIX:],
        c_qkv[:, :W_MIX], c_qkv[:, W_MIX:2 * W_MIX], c_qkv[:, 2 * W_MIX:],
        d_qkv[:, :W_MIX], d_qkv[:, W_MIX:2 * W_MIX], d_qkv[:, 2 * W_MIX:],
        misc,
    ]
    w_slab = jnp.concatenate(slabs, axis=1).astype(BF16)
    scale = HEAD_DIM ** -0.5
    t4 = lambda gvec: jnp.tile(gvec, N_HEADS)
    one = jnp.ones((W_MIX,), F32)
    scale2 = scale * LOG2E
    gains = [
        t4(qk_gain[0]) * scale, t4(qk_gain[0]) * scale2,
        jnp.concatenate([qk_gain[1], jnp.ones((W_MIX - HEAD_DIM,), F32)]),
        jnp.concatenate([qk_gain[2], qk_gain[3], jnp.ones((W_MIX - KV_QK_LANES,), F32)]),
        t4(qk_gain[4]) * scale2, t4(qk_gain[5]), one,
        one * scale, one, one,
        t4(qk_gain[6]) * scale2, t4(qk_gain[7]), one,
        one,
    ]
    gain = jnp.stack(gains).reshape(N_SLABS, 1, W_MIX).astype(F32)
    return w_slab, gain, w_merge.astype(BF16)


def _rope_tables(positions):
    inv = ROPE_THETA ** (-jnp.arange(0, ROPE_DIMS, 2, dtype=F32) / ROPE_DIMS)
    ang = positions.astype(F32)[..., None] * inv
    cos, sin = jnp.cos(ang), jnp.sin(ang)
    b, s, _ = cos.shape
    pad1 = jnp.ones((b, s, HEAD_DIM - ROPE_DIMS), F32)
    pad0 = jnp.zeros((b, s, HEAD_DIM - ROPE_DIMS), F32)
    z8 = jnp.zeros_like(sin)
    c64 = jnp.concatenate([cos, cos, pad1], axis=-1)
    s1_64 = jnp.concatenate([-sin, z8, pad0], axis=-1)
    s2_64 = jnp.concatenate([z8, sin, pad0], axis=-1)
    heads = lambda t: jnp.tile(t, (1, 1, N_HEADS))
    return jnp.stack([heads(c64), heads(s1_64), heads(s2_64)], axis=1)


def _compress_kernel(a_ref, b_ref, pe_ref, w1_ref, w2_ref, o_ref):
    half = w1_ref.shape[1] // 2
    w1 = w1_ref[0]
    hid = (_dot(a_ref[0, 0], w1[:half], NN, HI) + _dot(b_ref[0, 0], w1[half:], NN, HI)
           + _dot(pe_ref[0], w1, NN, HI))
    o_ref[0, 0] = _dot(jax.nn.gelu(hid), w2_ref[0], NN, HI)


def _compress(ch, chn, pe, w1, w2):
    _, b, ncp, cw = ch.shape
    hid = w1.shape[2]
    return pl.pallas_call(
        _compress_kernel,
        out_shape=jax.ShapeDtypeStruct((2, b, ncp, HEAD_DIM), F32),
        grid=(2, b),
        in_specs=[
            pl.BlockSpec((1, 1, ncp, cw), lambda k, i: (k, i, 0, 0)),
            pl.BlockSpec((1, 1, ncp, cw), lambda k, i: (k, i, 0, 0)),
            pl.BlockSpec((1, 1, 2 * cw), lambda k, i: (k, 0, 0)),
            pl.BlockSpec((1, 2 * cw, hid), lambda k, i: (k, 0, 0)),
            pl.BlockSpec((1, hid, HEAD_DIM), lambda k, i: (k, 0, 0)),
        ],
        out_specs=pl.BlockSpec((1, 1, ncp, HEAD_DIM), lambda k, i: (k, i, 0, 0)),
        compiler_params=_params("arbitrary", "arbitrary"),
        name="nsa_compress",
    )(ch, chn, pe, w1, w2)


def _cmp_kernel(q_ref, kc_ref, vct_ref, ovt_ref, o_ref, sel_ref, qh_scr, ql_scr, acc_scr, *, tq, nc, n_sel):
    i = pl.program_id(1)
    ncp = kc_ref.shape[1]
    q = q_ref[0]
    q_hi = q.astype(BF16)
    _stack_masked(q_hi, qh_scr, tq)
    _stack_masked((q - q_hi.astype(F32)).astype(BF16), ql_scr, tq)
    kc = kc_ref[0]
    k_hi = kc.astype(BF16)
    k_lo = (kc - k_hi.astype(F32)).astype(BF16)
    ss = []
    for h in range(N_HEADS):
        rs = slice(h * tq, (h + 1) * tq)
        ss.append(_dot(k_hi, qh_scr[rs, :], NT) + _dot(k_lo, qh_scr[rs, :], NT)
                  + _dot(k_hi, ql_scr[rs, :], NT))
    t = i * tq + lax.broadcasted_iota(jnp.int32, (1, tq), 1)
    c = lax.broadcasted_iota(jnp.int32, (ncp, 1), 0)
    mask = (c * CMP_STRIDE + (CMP_LEN - 1) <= t) & (c < nc)
    psum = None
    ps = []
    for h in range(N_HEADS):
        sm = jnp.where(mask, ss[h], NEG)
        m = jnp.max(sm, axis=0, keepdims=True)
        e = jnp.where(mask, jnp.exp(sm - m), 0.0)
        l = jnp.sum(e, axis=0, keepdims=True)
        p = e / jnp.maximum(l, 1e-30)
        psum = p if psum is None else psum + p
        ps.append(p.astype(BF16))
    vct = vct_ref[0]
    for h in range(N_HEADS):
        rs = slice(h * HEAD_DIM, (h + 1) * HEAD_DIM)
        acc_scr[rs, :] = _dot(vct[rs, :], ps[h])
    o_ref[0] = acc_scr[...].T.astype(o_ref.dtype)
    imp = _dot(ovt_ref[...], psum, NN, HI)
    j = lax.broadcasted_iota(jnp.int32, (n_sel, 1), 0)
    cur = t >> SEL_SHIFT
    valid = j <= cur
    forced = (j == 0) | (j == cur) | (j == cur - 1)
    picked = valid & forced
    score = jnp.where(valid, jnp.where(forced, -3e38, imp), NEG)
    sel = jnp.where(picked, 1.0, 0.0)
    jf = j.astype(F32)
    for _ in range(min(TOPN, n_sel) - 3):
        mx = jnp.max(score, axis=0, keepdims=True)
        idx = jnp.min(jnp.where(score == mx, jf, float(n_sel)), axis=0, keepdims=True)
        pick = jf == idx
        sel = jnp.where(pick, 1.0, sel)
        score = jnp.where(pick, -3e38, score)
    sel_ref[0] = ((sel - 1.0) * SEL_NEG).T.astype(sel_ref.dtype)


def _cmp_topk(q_nr, kc_rep, vc_rep, overlap, tq, nc):
    b, seq, _ = q_nr.shape
    ncp = kc_rep.shape[1]
    n_sel = seq // SEL_LEN
    return pl.pallas_call(
        functools.partial(_cmp_kernel, tq=tq, nc=nc, n_sel=n_sel),
        out_shape=[jax.ShapeDtypeStruct((b, seq, W_MIX), BF16),
                   jax.ShapeDtypeStruct((b, seq, n_sel), BF16)],
        grid=(b, seq // tq),
        in_specs=[
            pl.BlockSpec((1, tq, W_MIX), lambda g, i: (g, i, 0)),
            pl.BlockSpec((1, ncp, W_MIX), lambda g, i: (g, 0, 0)),
            pl.BlockSpec((1, W_MIX, ncp), lambda g, i: (g, 0, 0)),
            pl.BlockSpec((n_sel, ncp), lambda g, i: (0, 0)),
        ],
        out_specs=[pl.BlockSpec((1, tq, W_MIX), lambda g, i: (g, i, 0)),
                   pl.BlockSpec((1, tq, n_sel), lambda g, i: (g, i, 0))],
        scratch_shapes=[pltpu.VMEM((N_HEADS * tq, W_MIX), BF16), pltpu.VMEM((N_HEADS * tq, W_MIX), BF16),
                        pltpu.VMEM((W_MIX, tq), F32)],
        compiler_params=_params("arbitrary", "arbitrary"),
        name="nsa_cmp_topk",
    )(q_nr, kc_rep, vc_rep, overlap)


def _causal_kernel(*refs, tq, tk, has_sel, has_bias):
    it = iter(refs)
    q_ref, k_ref, vt_ref = next(it), next(it), next(it)
    sel_ref = next(it) if has_sel else None
    fa_ref, fb_ref, fq_ref, thr_ref = (next(it) for _ in range(4)) if has_bias else (None,) * 4
    o_ref, qm_scr, acc_scr, sa_scr, sb_scr, m_scr, l_scr = (next(it) for _ in range(7))
    i = pl.program_id(1)
    t0 = i * tq
    lane = lax.broadcasted_iota(jnp.int32, (1, W_MIX), 1)
    if has_sel:
        n_sel = sel_ref.shape[2]
        qf = q_ref[0].astype(F32)
        selb = sel_ref[0].astype(F32)
        selb = jnp.concatenate([selb, jnp.zeros((tq, W_MIX - n_sel), F32)], axis=1)
        selb = pltpu.roll(selb, HEAD_DIM, 1)
        for h in range(N_HEADS):
            rot = qf if h == 0 else pltpu.roll(qf, W_MIX - h * HEAD_DIM, 1)
            qm_scr[h * tq:(h + 1) * tq, :] = jnp.where(lane < HEAD_DIM, rot, selb).astype(BF16)
    else:
        _stack_masked(q_ref[0], qm_scr, tq)
    acc_scr[...] = jnp.zeros(acc_scr.shape, F32)
    t_pos = t0 + lax.broadcasted_iota(jnp.int32, (1, tq), 1)

    def scores(kt, s_buf):
        k_t = k_ref[0, kt]
        for h in range(N_HEADS):
            s = _dot(k_t, qm_scr[h * tq:(h + 1) * tq, :], NT)
            if has_bias:
                s = s + _dot(fa_ref[0, kt], fb_ref[0, h])
            s_buf[h] = s

    def update(kt, s_buf, diag):
        vt_t = vt_ref[0, kt]
        mask = None
        if diag:
            s_pos = kt * tk + lax.broadcasted_iota(jnp.int32, (tk, 1), 0)
            mask = s_pos <= t_pos
        ps, alphas = [], []
        for h in range(N_HEADS):
            s = s_buf[h]
            if mask is not None:
                s = jnp.where(mask, s, NEG)
            m_old = m_scr[h]
            m_new = jnp.maximum(m_old, jnp.max(s, axis=0, keepdims=True))
            alpha = jnp.exp2(m_old - m_new)
            m_scr[h] = m_new
            p = jnp.exp2(s - m_new)
            l_scr[h] = alpha * l_scr[h] + jnp.sum(p, axis=0, keepdims=True)
            ps.append(p.astype(BF16))
            alphas.append(alpha)
        for h in range(N_HEADS):
            rs = slice(h * HEAD_DIM, (h + 1) * HEAD_DIM)
            vs = slice(VSL_LANE, VSL_LANE + HEAD_DIM) if has_sel else rs
            acc_scr[rs, :] = alphas[h] * acc_scr[rs, :] + _dot(vt_t[vs, :], ps[h])

    def live(kt_next):
        lane = lax.broadcasted_iota(jnp.int32, (1, thr_ref.shape[3]), 1)
        hit = None
        for h in range(N_HEADS):
            top = jnp.max(fq_ref[0, h] - m_scr[h], axis=-1, keepdims=True)
            need = top >= thr_ref[0, h]
            hit = need if hit is None else (hit | need)
        return jnp.max(jnp.where(hit & (lane == kt_next), 1.0, 0.0)) > 0.5

    n_last = t0 // tk
    m_scr[...] = jnp.full(m_scr.shape, NEG, F32)
    l_scr[...] = jnp.zeros(l_scr.shape, F32)
    scores(n_last, sa_scr)
    scores(jnp.maximum(n_last - 1, 0), sb_scr)
    update(n_last, sa_scr, True)

    def pair(j):
        kt = n_last - 1 - 2 * j
        scores(kt - 1, sa_scr)
        update(kt, sb_scr, False)
        scores(jnp.maximum(kt - 2, 0), sb_scr)
        update(kt - 1, sa_scr, False)
        return kt - 2

    n_pairs = n_last // 2
    if has_bias:
        def cond(state):
            j, go = state
            return (j < n_pairs) & go

        def body(state):
            j, _ = state
            return j + 1, live(pair(j))

        _, go = lax.while_loop(cond, body, (jnp.int32(0), live(n_last - 1)))
    else:
        lax.fori_loop(0, n_pairs, lambda j, c: (pair(j), c)[1], 0)
        go = True

    @pl.when((n_last % 2 == 1) & go)
    def _():
        update(0, sb_scr, False)

    ls = [l_scr[h] for h in range(N_HEADS)]
    for h in range(N_HEADS):
        rs = slice(h * HEAD_DIM, (h + 1) * HEAD_DIM)
        acc_scr[rs, :] = acc_scr[rs, :] / ls[h]
    o_ref[0] = acc_scr[...].T.astype(o_ref.dtype)


def _split3(x):
    def cut(v):
        bits = lax.bitcast_convert_type(v, jnp.uint32) & jnp.uint32(0xFFFF0000)
        return lax.bitcast_convert_type(bits, F32)

    hi = cut(x)
    r1 = x - hi
    mid = cut(r1)
    lo = r1 - mid
    return hi.astype(BF16), mid.astype(BF16), lo.astype(BF16)


def _causal_attn(q, k, v, tq, tk, sel=None, fcum=None, qk_bound=None):
    b, s, _ = q.shape
    nk = s // tk
    rows = N_HEADS * tq
    n_s = N_HEADS
    vt = v.reshape(b, nk, tk, W_MIX).transpose(0, 1, 3, 2)
    args = [q, k.reshape(b, nk, tk, W_MIX), vt]
    in_specs = [
        pl.BlockSpec((1, tq, W_MIX), lambda a, i: (a, i, 0)),
        pl.BlockSpec((1, nk, tk, W_MIX), lambda a, i: (a, 0, 0, 0)),
        pl.BlockSpec((1, nk, W_MIX, tk), lambda a, i: (a, 0, 0, 0)),
    ]
    if sel is not None:
        args.append(sel)
        in_specs.append(pl.BlockSpec((1, tq, sel.shape[2]), lambda a, i: (a, i, 0)))
    if fcum is not None:
        fcum = fcum * LOG2E
        qk_bound = qk_bound * LOG2E
        nf = 2 * 3 + 2
        ones = jnp.ones(fcum.shape, BF16)
        zero = jnp.zeros(fcum.shape, BF16)
        parts = _split3(fcum)
        key_f = jnp.stack([ones, ones, ones] + [-p for p in parts] + [zero, zero], axis=-1)
        qry_f = jnp.stack(list(parts) + [ones, ones, ones, zero, zero], axis=-1)
        fa = key_f.reshape(b, nk, tk, N_HEADS * nf)
        eye = jnp.eye(N_HEADS, dtype=BF16)
        fb = jnp.einsum('bshf,hg->bgshf', qry_f, eye).reshape(b, N_HEADS, s, N_HEADS * nf).transpose(0, 1, 3, 2)
        f_rows = fcum.transpose(0, 2, 1)
        f_end = f_rows[:, :, tk - 1::tk]
        thr = jnp.pad(f_end - (qk_bound + FOX_STOP * LOG2E), ((0, 0), (0, 0), (0, LANES - nk)),
                      constant_values=BIG).reshape(b, N_HEADS, 1, LANES)
        args += [fa, fb, f_rows.reshape(b, N_HEADS, 1, s), thr]
        in_specs += [pl.BlockSpec((1, nk, tk, N_HEADS * nf), lambda a, i: (a, 0, 0, 0)),
                     pl.BlockSpec((1, N_HEADS, N_HEADS * nf, tq), lambda a, i: (a, 0, 0, i)),
                     pl.BlockSpec((1, N_HEADS, 1, tq), lambda a, i: (a, 0, 0, i)),
                     pl.BlockSpec((1, N_HEADS, 1, LANES), lambda a, i: (a, 0, 0, 0))]
    return pl.pallas_call(
        functools.partial(_causal_kernel, tq=tq, tk=tk, has_sel=sel is not None, has_bias=fcum is not None),
        out_shape=jax.ShapeDtypeStruct((b, s, W_MIX), BF16),
        grid=(b, s // tq),
        in_specs=in_specs,
        out_specs=pl.BlockSpec((1, tq, W_MIX), lambda a, i: (a, i, 0)),
        scratch_shapes=[pltpu.VMEM((rows, W_MIX), BF16), pltpu.VMEM((W_MIX, tq), F32),
                        pltpu.VMEM((n_s, tk, tq), F32), pltpu.VMEM((n_s, tk, tq), F32),
                        pltpu.VMEM((N_HEADS, 1, tq), F32), pltpu.VMEM((N_HEADS, 1, tq), F32)],
        compiler_params=_params("arbitrary", "arbitrary"),
        name="causal_sel%d_bias%d" % (sel is not None, fcum is not None),
    )(*args)


def _window_kernel(*refs, tq, wk, pad, window, ls, emit_lse, other_dils, kv_lanes):
    it = iter(refs)
    q_ref, k_ref, v_ref = next(it), next(it), next(it)
    n_other = len(other_dils)
    others_in = [(next(it), next(it)) for _ in range(n_other)]
    wexp_ref = next(it) if n_other else None
    o_ref = next(it)
    lse_ref = next(it) if emit_lse else None
    qm_scr, acc_scr = next(it), next(it)
    others = []
    if n_other:
        og_scr, lg_scr = next(it), next(it)
        for g, (dg, (og_ref, lg_ref)) in enumerate(zip(other_dils, others_in)):
            for r in range(dg):
                rows = pl.ds(r, tq // dg, stride=dg)
                for c in range(W_MIX // LANES):
                    lo = r * W_MIX + c * LANES
                    og_scr[g, c, rows, :] = og_ref[0, :, lo:lo + LANES]
                lg_scr[g, rows, :] = lg_ref[0, :, r * LANES:(r + 1) * LANES]
            others.append((og_scr.at[g], lg_scr.at[g]))
    t0 = pl.program_id(2) * tq
    start = pl.multiple_of(jnp.clip(t0 - pad, 0, ls - wk), LANES)
    if kv_lanes is None:
        _stack_masked(q_ref[0], qm_scr, tq)
    else:
        lane = lax.broadcasted_iota(jnp.int32, (1, W_MIX), 1)
        on_key = (lane >= kv_lanes[0]) & (lane < kv_lanes[0] + HEAD_DIM)
        qf = q_ref[0].astype(F32)
        for h in range(N_HEADS):
            shift = (kv_lanes[0] - h * HEAD_DIM) % W_MIX
            rot = pltpu.roll(qf, shift, 1) if shift else qf
            qm_scr[h * tq:(h + 1) * tq, :] = jnp.where(on_key, rot, 0.0).astype(BF16)
    k_w = k_ref[0, pl.ds(start, wk), :]
    vt_w = v_ref[0, pl.ds(start, wk), :].astype(F32).T.astype(BF16)
    ss = [_dot(k_w, qm_scr[h * tq:(h + 1) * tq, :], NT) for h in range(N_HEADS)]
    t_pos = t0 + lax.broadcasted_iota(jnp.int32, (1, tq), 1)
    s_pos = start + lax.broadcasted_iota(jnp.int32, (wk, 1), 0)
    mask = (s_pos <= t_pos) & (t_pos - s_pos < window)
    ps, inv_ls, lses = [], [], []
    for h in range(N_HEADS):
        s = jnp.where(mask, ss[h], NEG)
        m = jnp.max(s, axis=0, keepdims=True)
        p = jnp.exp2(s - m)
        l = jnp.sum(p, axis=0, keepdims=True)
        ps.append(p.astype(BF16))
        inv_ls.append(1.0 / l)
        lses.append(m + jnp.log2(l))
    for h in range(N_HEADS):
        rs = slice(h * HEAD_DIM, (h + 1) * HEAD_DIM)
        vs = rs if kv_lanes is None else slice(kv_lanes[1], kv_lanes[1] + HEAD_DIM)
        acc_scr[rs, :] = _dot(vt_w[vs, :], ps[h]) * inv_ls[h]
    o_self = acc_scr[...].T
    if emit_lse or n_other:
        row = lax.broadcasted_iota(jnp.int32, (LANES, 1), 0)
        stat = jnp.zeros((LANES, tq), F32)
        for h in range(N_HEADS):
            stat = jnp.where(row == h, lses[h], stat)
        lse_tile = stat.T
    if n_other:
        lg_t = [lg[...].T for (_, lg) in others]
        row = lax.broadcasted_iota(jnp.int32, (LANES, 1), 0)
        wmat = jnp.zeros((LANES, tq), F32)
        for h in range(N_HEADS):
            group_lse = [lses[h]] + [t[h:h + 1, :] for t in lg_t]
            top = functools.reduce(jnp.maximum, group_lse)
            ws = [jnp.exp2(x - top) for x in group_lse]
            inv_den = 1.0 / functools.reduce(jnp.add, ws)
            for g, w in enumerate(ws):
                wmat = jnp.where(row == g * N_HEADS + h, w * inv_den, wmat)
        wt = wmat.T
        groups = [o_self] + [jnp.concatenate([og[c] for c in range(W_MIX // LANES)], axis=1)
                             for (og, _) in others]
        out = None
        for g, o_g in enumerate(groups):
            term = _dot_split(wt, wexp_ref[g]) * o_g
            out = term if out is None else out + term
        o_ref[0] = out.astype(o_ref.dtype)
    else:
        o_ref[0] = o_self.astype(o_ref.dtype)
    if emit_lse:
        lse_ref[0] = lse_tile


def _window_attn(q, k, v, *, dil, window, tq, out_dtype, emit_lse=False, others=(), kv_lanes=None):
    b, ls, _ = q.shape
    tq = min(tq, ls)
    pad = -(-(window - 1) // LANES) * LANES
    wk = min(tq + pad, ls)
    rows = N_HEADS * tq
    args = [q, k, v]
    in_specs = [
        pl.BlockSpec((1, tq, W_MIX), lambda a, r, i: (a, i, r)),
        pl.BlockSpec((1, ls, W_MIX), lambda a, r, i: (a, 0, r)),
        pl.BlockSpec((1, ls, W_MIX), lambda a, r, i: (a, 0, r)),
    ]
    scratch = [pltpu.VMEM((rows, W_MIX), BF16), pltpu.VMEM((W_MIX, tq), F32)]
    for (o_g, lse_g, dg) in others:
        args += [o_g, lse_g]
        in_specs += [pl.BlockSpec((1, tq // dg, dg * W_MIX), lambda a, r, i: (a, i, 0)),
                     pl.BlockSpec((1, tq // dg, dg * LANES), lambda a, r, i: (a, i, 0))]
    if others:
        wexp = np.zeros((len(others) + 1, LANES, W_MIX), np.float32)
        for g in range(len(others) + 1):
            for h in range(N_HEADS):
                wexp[g, g * N_HEADS + h, h * HEAD_DIM:(h + 1) * HEAD_DIM] = 1.0
        args.append(jnp.asarray(wexp, BF16))
        in_specs.append(pl.BlockSpec(wexp.shape, lambda a, r, i: (0, 0, 0)))
        scratch += [pltpu.VMEM((len(others), W_MIX // LANES, tq, LANES), F32),
                    pltpu.VMEM((len(others), tq, LANES), F32)]
    out_shape = [jax.ShapeDtypeStruct((b, ls, dil * W_MIX), out_dtype)]
    out_specs = [pl.BlockSpec((1, tq, W_MIX), lambda a, r, i: (a, i, r))]
    if emit_lse:
        out_shape.append(jax.ShapeDtypeStruct((b, ls, dil * LANES), F32))
        out_specs.append(pl.BlockSpec((1, tq, LANES), lambda a, r, i: (a, i, r)))
    res = pl.pallas_call(
        functools.partial(_window_kernel, tq=tq, wk=wk, pad=pad, window=window, ls=ls,
                          emit_lse=emit_lse, other_dils=tuple(dg for (_, _, dg) in others),
                          kv_lanes=kv_lanes),
        out_shape=out_shape,
        grid=(b, dil, ls // tq),
        in_specs=in_specs,
        out_specs=out_specs,
        scratch_shapes=scratch,
        compiler_params=_params("arbitrary", "arbitrary", "arbitrary"),
        name="window_d%d_w%d" % (dil, window),
    )(*args)
    return (res[0], res[1], dil) if emit_lse else res[0]


def _sb_kernel(q_ref, k_ref, vt_ref, tri_ref, o_ref, qm_scr, carry_scr, acc_scr, *, tq):
    i = pl.program_id(1)
    t0 = i * tq
    _stack_masked(q_ref[0], qm_scr, tq)
    carry_scr[...] = jnp.zeros(carry_scr.shape, F32)
    acc_scr[...] = jnp.zeros(acc_scr.shape, F32)
    t_pos = t0 + lax.broadcasted_iota(jnp.int32, (1, tq), 1)

    def tile(kt, diag):
        k_t = k_ref[0, kt]
        vt_t = vt_ref[0, kt]
        tri = tri_ref[...]
        zs = [_dot(k_t, qm_scr[h * tq:(h + 1) * tq, :], NT) for h in range(N_HEADS)]
        if diag:
            s_pos = kt * tq + lax.broadcasted_iota(jnp.int32, (tq, 1), 0)
            strict = s_pos < t_pos
        weights = []
        for h in range(N_HEADS):
            z = zs[h]
            lg = -(jnp.maximum(z, 0.0) + jnp.log(1.0 + jnp.exp(-jnp.abs(z))))
            if diag:
                lg = jnp.where(strict, lg, 0.0)
            hi = lg.astype(BF16)
            lo = (lg - hi.astype(F32)).astype(BF16)
            cum = _dot(tri, hi) + _dot(tri, lo) + carry_scr[h]
            a = jnp.exp(z + cum)
            if diag:
                a = jnp.where(strict, a, 0.0)
            weights.append(a.astype(BF16))
            carry_scr[h] += jnp.sum(lg, axis=0, keepdims=True)
        for h in range(N_HEADS):
            rs = slice(h * HEAD_DIM, (h + 1) * HEAD_DIM)
            acc_scr[rs, :] += _dot(vt_t[rs, :], weights[h])

    tile(i, True)

    def cond(state):
        j, top = state
        return (j < i) & (top > SB_STOP)

    def body(state):
        j, _ = state
        tile(i - 1 - j, False)
        return j + 1, jnp.max(carry_scr[...])

    lax.while_loop(cond, body, (jnp.int32(0), jnp.max(carry_scr[...])))
    o_ref[0] = acc_scr[...].T.astype(o_ref.dtype)


def _stick_breaking(q, k, v, tq):
    b, s, _ = q.shape
    nk = s // tq
    rows = N_HEADS * tq
    tri = jnp.asarray(np.triu(np.ones((tq, tq), np.float32)), BF16)
    vt = v.reshape(b, nk, tq, W_MIX).transpose(0, 1, 3, 2)
    return pl.pallas_call(
        functools.partial(_sb_kernel, tq=tq),
        out_shape=jax.ShapeDtypeStruct((b, s, W_MIX), BF16),
        grid=(b, nk),
        in_specs=[
            pl.BlockSpec((1, tq, W_MIX), lambda a, i: (a, i, 0)),
            pl.BlockSpec((1, nk, tq, W_MIX), lambda a, i: (a, 0, 0, 0)),
            pl.BlockSpec((1, nk, W_MIX, tq), lambda a, i: (a, 0, 0, 0)),
            pl.BlockSpec((tq, tq), lambda a, i: (0, 0)),
        ],
        out_specs=pl.BlockSpec((1, tq, W_MIX), lambda a, i: (a, i, 0)),
        scratch_shapes=[pltpu.VMEM((rows, W_MIX), BF16), pltpu.VMEM((N_HEADS, 1, tq), F32),
                        pltpu.VMEM((W_MIX, tq), F32)],
        compiler_params=_params("arbitrary", "arbitrary"),
        name="stick_breaking",
    )(q, k.reshape(b, nk, tq, W_MIX), vt, tri)


def _foxcum_kernel(x_ref, b_ref, tri_ref, o_ref, carry_scr):
    @pl.when(pl.program_id(1) == 0)
    def _():
        carry_scr[...] = jnp.zeros(carry_scr.shape, F32)

    z = x_ref[0] + b_ref[...]
    logf = jnp.minimum(z, 0.0) - jnp.log(1.0 + jnp.exp(-jnp.abs(z)))
    cum = _dot(tri_ref[...], logf, NN, HI) + carry_scr[...]
    o_ref[0] = cum
    carry_scr[...] = cum[cum.shape[0] - 1:, :]


def _fox_cumsum(misc, bias_vec, tc):
    b, s, w = misc.shape
    tri = jnp.asarray(np.tril(np.ones((tc, tc), np.float32)))
    return pl.pallas_call(
        _foxcum_kernel,
        out_shape=jax.ShapeDtypeStruct((b, s, w), F32),
        grid=(b, s // tc),
        in_specs=[
            pl.BlockSpec((1, tc, w), lambda a, i: (a, i, 0)),
            pl.BlockSpec((1, w), lambda a, i: (0, 0)),
            pl.BlockSpec((tc, tc), lambda a, i: (0, 0)),
        ],
        out_specs=pl.BlockSpec((1, tc, w), lambda a, i: (a, i, 0)),
        scratch_shapes=[pltpu.VMEM((1, w), F32)],
        compiler_params=_params("arbitrary", "arbitrary"),
        name="fox_cumsum",
    )(misc, bias_vec, tri)


def _merge_kernel(x_ref, g_ref, sc_ref, sh_ref, ga_ref, wm_ref, misc_ref, pg_ref,
                  ocmp_ref, osel_ref, owin_ref, ob_ref, oc_ref, od_ref,
                  wa_ref, wb_ref, wc_ref, wd_ref, wo_ref, o_ref):
    x = x_ref[...]
    d = x.shape[1]
    h = _mod_norm(x, g_ref[...], sc_ref[0], sh_ref[0]).astype(BF16)
    gate = jax.nn.sigmoid(misc_ref[...])
    o_a = (_dot_split(gate, pg_ref[0]) * ocmp_ref[...].astype(F32)
           + _dot_split(gate, pg_ref[1]) * osel_ref[...].astype(F32)
           + _dot_split(gate, pg_ref[2]) * owin_ref[...].astype(F32)).astype(BF16)
    mixed = jnp.zeros(x.shape, F32)
    for m, (o_m, w_ref) in enumerate(((o_a, wa_ref), (ob_ref[...], wb_ref),
                                      (oc_ref[...], wc_ref), (od_ref[...], wd_ref))):
        y = _dot(o_m, w_ref[...])
        gl = _dot(h, wm_ref[:, m * d:(m + 1) * d])
        mixed = mixed + jax.nn.sigmoid(gl) * y
    o_ref[...] = x + ga_ref[0] * _dot(mixed.astype(BF16), wo_ref[...])


def _merge(x2, g, sc, sh, ga, w_merge, misc, pg, o_cmp, o_sel, o_win, o_b, o_c, o_d,
           wa, wb, wc, wd, wo, seq, tt):
    n, d = x2.shape
    tpb = seq // tt
    row = lambda w: pl.BlockSpec((tt, w), lambda i: (i, 0))
    full = lambda a: pl.BlockSpec(a.shape, lambda i: (0,) * a.ndim)
    per_b = pl.BlockSpec((1, 1, d), lambda i: (i // tpb, 0, 0))
    return pl.pallas_call(
        _merge_kernel,
        out_shape=jax.ShapeDtypeStruct((n, d), F32),
        grid=(n // tt,),
        in_specs=[row(d), full(g), per_b, per_b, per_b, full(w_merge), row(W_MIX), full(pg)]
        + [row(W_MIX)] * 6 + [full(wa), full(wb), full(wc), full(wd), full(wo)],
        out_specs=row(d),
        compiler_params=_params("arbitrary"),
        name="merge_out",
    )(x2, g, sc, sh, ga, w_merge, misc, pg, o_cmp, o_sel, o_win, o_b, o_c, o_d, wa, wb, wc, wd, wo)


def _ffn_kernel(x_ref, g_ref, sc_ref, sh_ref, gf_ref, w1_ref, w3_ref, w2_ref, o_ref, h_scr, acc_scr):
    f = pl.program_id(1)

    @pl.when(f == 0)
    def _():
        h_scr[...] = _mod_norm(x_ref[...], g_ref[...], sc_ref[0], sh_ref[0]).astype(BF16)
        acc_scr[...] = jnp.zeros(acc_scr.shape, F32)

    h = h_scr[...]
    a = _dot(h, w1_ref[...])
    b = _dot(h, w3_ref[...])
    acc_scr[...] += _dot((a * jax.nn.sigmoid(a) * b).astype(BF16), w2_ref[...])

    @pl.when(f == pl.num_programs(1) - 1)
    def _():
        o_ref[...] = x_ref[...] + gf_ref[0] * acc_scr[...]


def _ffn(x2, g, sc, sh, gf, w1, w3, w2, seq, tt, tf):
    n, d = x2.shape
    dff = w1.shape[1]
    tpb = seq // tt
    per_b = pl.BlockSpec((1, 1, d), lambda i, f: (i // tpb, 0, 0))
    return pl.pallas_call(
        _ffn_kernel,
        out_shape=jax.ShapeDtypeStruct((n, d), F32),
        grid=(n // tt, dff // tf),
        in_specs=[
            pl.BlockSpec((tt, d), lambda i, f: (i, 0)),
            pl.BlockSpec((1, d), lambda i, f: (0, 0)),
            per_b, per_b, per_b,
            pl.BlockSpec((d, tf), lambda i, f: (0, f)),
            pl.BlockSpec((d, tf), lambda i, f: (0, f)),
            pl.BlockSpec((tf, d), lambda i, f: (f, 0)),
        ],
        out_specs=pl.BlockSpec((tt, d), lambda i, f: (i, 0)),
        scratch_shapes=[pltpu.VMEM((tt, d), BF16), pltpu.VMEM((tt, d), F32)],
        compiler_params=_params("arbitrary", "arbitrary"),
        name="ffn_swiglu",
    )(x2, g, sc, sh, gf, w1, w3, w2)


def _route_kernel(x_ref, g_ref, sc_ref, sh_ref, rw_ref, up_ref, h_ref, rank_ref, gate_ref, cnt_ref):
    hf = _mod_norm(x_ref[...], g_ref[...], sc_ref[0], sh_ref[0])
    h_ref[...] = hf.astype(BF16)
    logits = _dot(rw_ref[...], hf, NT, HI)
    ne, tt = logits.shape
    e_idx = lax.broadcasted_iota(jnp.int32, (ne, 1), 0).astype(F32)
    v1 = jnp.max(logits, axis=0, keepdims=True)
    i1 = jnp.min(jnp.where(logits == v1, e_idx, float(ne)), axis=0, keepdims=True)
    m1 = e_idx == i1
    rest = jnp.where(m1, -3e38, logits)
    v2 = jnp.max(rest, axis=0, keepdims=True)
    i2 = jnp.min(jnp.where(rest == v2, e_idx, float(ne)), axis=0, keepdims=True)
    m2 = e_idx == i2
    e2 = jnp.exp(v2 - v1)
    g1 = 1.0 / (1.0 + e2)
    g2 = e2 / (1.0 + e2)
    routed = m1 | m2
    rf = jnp.where(routed, 1.0, 0.0)
    rank = _dot(rf.astype(BF16), up_ref[...])
    rank = jnp.where(routed, rank, -1.0)
    gate = jnp.where(m1, g1, 0.0) + jnp.where(m2, g2, 0.0)
    for e in range(ne):
        rank_ref[0, e] = rank[e:e + 1, :]
        gate_ref[0, e] = gate[e:e + 1, :]
    cnt = jnp.sum(rf, axis=1, keepdims=True)
    cnt_ref[0] = jnp.broadcast_to(cnt, (ne, LANES))


def _route(x2, g, sc, sh, rw_t, seq, tt):
    n, d = x2.shape
    ne = rw_t.shape[0]
    tpb = seq // tt
    nt = n // tt
    upper = jnp.asarray(np.triu(np.ones((tt, tt), np.float32), 1), BF16)
    per_b = pl.BlockSpec((1, 1, d), lambda i: (i // tpb, 0, 0))
    return pl.pallas_call(
        _route_kernel,
        out_shape=[jax.ShapeDtypeStruct((n, d), BF16),
                   jax.ShapeDtypeStruct((nt, ne, 1, tt), F32),
                   jax.ShapeDtypeStruct((nt, ne, 1, tt), F32),
                   jax.ShapeDtypeStruct((nt, ne, LANES), F32)],
        grid=(nt,),
        in_specs=[
            pl.BlockSpec((tt, d), lambda i: (i, 0)),
            pl.BlockSpec((1, d), lambda i: (0, 0)),
            per_b, per_b,
            pl.BlockSpec((ne, d), lambda i: (0, 0)),
            pl.BlockSpec((tt, tt), lambda i: (0, 0)),
        ],
        out_specs=[pl.BlockSpec((tt, d), lambda i: (i, 0)),
                   pl.BlockSpec((1, ne, 1, tt), lambda i: (i, 0, 0, 0)),
                   pl.BlockSpec((1, ne, 1, tt), lambda i: (i, 0, 0, 0)),
                   pl.BlockSpec((1, ne, LANES), lambda i: (i, 0, 0))],
        compiler_params=_params("arbitrary"),
        name="moe_route",
    )(x2, g, sc, sh, rw_t, upper)


def _moe_kernel(cnt_ref, x_ref, gf_ref, h_ref, rank_ref, gate_ref, w1_ref, w3_ref, w2_ref,
                o_ref, acc_scr, xs_scr, y_scr, *, chunk):
    i, e, f = pl.program_id(0), pl.program_id(1), pl.program_id(2)
    ne, nf = pl.num_programs(1), pl.num_programs(2)

    @pl.when((e == 0) & (f == 0))
    def _():
        acc_scr[...] = jnp.zeros(acc_scr.shape, F32)

    count = cnt_ref[i * ne + e]
    rank = rank_ref[0, 0]
    gate = gate_ref[0, 0]
    n_small = (count + chunk - 1) // chunk
    n_big = (count + 2 * chunk - 1) // (2 * chunk)

    def one_hot(c, rows):
        r = c * rows + lax.broadcasted_iota(jnp.int32, (rows, 1), 0)
        return rank == r.astype(F32)

    def rows_of(c, rows):
        return pl.ds(pl.multiple_of(c * rows, rows), rows)

    @pl.when(f == 0)
    def _():
        h = h_ref[...]

        def gather(c, carry):
            p = jnp.where(one_hot(c, chunk), 1.0, 0.0).astype(BF16)
            xs_scr[rows_of(c, chunk), :] = _dot(p, h).astype(BF16)
            return carry

        lax.fori_loop(0, n_small, gather, 0)

        def clear(c, carry):
            y_scr[rows_of(c, 2 * chunk), :] = jnp.zeros((2 * chunk, y_scr.shape[1]), F32)
            return carry

        lax.fori_loop(0, n_big, clear, 0)

    def expert(c, carry):
        xs = xs_scr[rows_of(c, chunk), :]
        a = _dot(xs, w1_ref[0])
        b = _dot(xs, w3_ref[0])
        y_scr[rows_of(c, chunk), :] += _dot((a * jax.nn.sigmoid(a) * b).astype(BF16), w2_ref[0])
        return carry

    lax.fori_loop(0, n_small, expert, 0)

    @pl.when(f == nf - 1)
    def _():
        def scatter(c, carry):
            hit = one_hot(c, 2 * chunk)
            p = jnp.where(hit, 1.0, 0.0).astype(BF16)
            gcol = jnp.sum(jnp.where(hit, gate, 0.0), axis=-1, keepdims=True)
            acc_scr[...] += _dot(p, (y_scr[rows_of(c, 2 * chunk), :] * gcol).astype(BF16), TN)
            return carry

        lax.fori_loop(0, n_big, scatter, 0)

    @pl.when((e == ne - 1) & (f == nf - 1))
    def _():
        o_ref[...] = x_ref[...] + gf_ref[0] * acc_scr[...]


def _moe(counts, x2, gf, h2, rank, gate, w1, w3, w2, seq, tt, tf, chunk):
    n, d = x2.shape
    ne, _, dff = w1.shape
    tpb = seq // tt
    grid_spec = pltpu.PrefetchScalarGridSpec(
        num_scalar_prefetch=1,
        grid=(n // tt, ne, dff // tf),
        in_specs=[
            pl.BlockSpec((tt, d), lambda i, e, f, c: (i, 0)),
            pl.BlockSpec((1, 1, d), lambda i, e, f, c: (i // tpb, 0, 0)),
            pl.BlockSpec((tt, d), lambda i, e, f, c: (i, 0)),
            pl.BlockSpec((1, 1, 1, tt), lambda i, e, f, c: (i, e, 0, 0)),
            pl.BlockSpec((1, 1, 1, tt), lambda i, e, f, c: (i, e, 0, 0)),
            pl.BlockSpec((1, d, tf), lambda i, e, f, c: (e, 0, f)),
            pl.BlockSpec((1, d, tf), lambda i, e, f, c: (e, 0, f)),
            pl.BlockSpec((1, tf, d), lambda i, e, f, c: (e, f, 0)),
        ],
        out_specs=pl.BlockSpec((tt, d), lambda i, e, f, c: (i, 0)),
        scratch_shapes=[pltpu.VMEM((tt, d), F32), pltpu.VMEM((tt, d), BF16), pltpu.VMEM((tt, d), F32)],
    )
    return pl.pallas_call(
        functools.partial(_moe_kernel, chunk=chunk),
        out_shape=jax.ShapeDtypeStruct((n, d), F32),
        grid_spec=grid_spec,
        compiler_params=_params("arbitrary", "arbitrary", "arbitrary"),
        name="moe_experts",
    )(counts, x2, gf, h2, rank, gate, w1, w3, w2)


def _overlap_matrix(ncp, nc, n_sel):
    c0 = np.arange(ncp) * CMP_STRIDE
    c1 = c0 + CMP_LEN
    s0 = np.arange(n_sel) * SEL_LEN
    s1 = s0 + SEL_LEN
    ov = ((c0[:, None] < s1[None, :]) & (c1[:, None] > s0[None, :])).astype(np.float32)
    ov[nc:] = 0.0
    return jnp.asarray(ov.T)


def _gate_expand():
    pg = np.zeros((3, W_MIX, W_MIX), np.float32)
    for br in range(3):
        for h in range(N_HEADS):
            pg[br, GATE_LANE + 3 * h + br, h * HEAD_DIM:(h + 1) * HEAD_DIM] = 1.0
    return jnp.asarray(pg, BF16)


def _mixer_layer(x2, b, s, mod, norm_g, rope, w_in, qk_gain, pe_k, pe_v, ck1, ck2, cv1, cv2,
                 fox_b, w_branch, w_out):
    n, d = x2.shape
    sh_a, sc_a, g_a = mod[0], mod[1], mod[2]
    w_slab, gain, w_merge = _pack_w_in(w_in, qk_gain)
    bd = jnp.asarray(np.kron(np.eye(N_HEADS), np.full((HEAD_DIM, HEAD_DIM), 1.0 / HEAD_DIM)), BF16)
    outs = _proj(x2, norm_g, sc_a, sh_a, w_slab, gain, bd, rope, s, min(512, s))
    sl = [a.reshape(b, s, W_MIX) for a in outs[:N_SLABS]]
    dil_in = {1: (sl[S_BQ], sl[S_BK], sl[S_BV])}
    for di, dil in enumerate(_DILATIONS):
        dil_in[dil] = tuple(outs[N_SLABS + si * len(_DILATIONS) + di].reshape(b, s // dil, dil * W_MIX)
                            for si in range(len(_DIL_SLABS)))
    misc = sl[S_MISC]

    nch = s // CMP_STRIDE
    nc = nch - CMP_LEN // CMP_STRIDE + 1
    kc_raw = sl[S_KC][..., :HEAD_DIM]
    vc_raw = misc[..., VC_LANE:VC_LANE + HEAD_DIM]
    chunks = jnp.stack([kc_raw, vc_raw]).reshape(2, b, nch, CMP_STRIDE * HEAD_DIM)
    chunks_next = jnp.concatenate([chunks[:, :, 1:], jnp.zeros_like(chunks[:, :, :1])], axis=2)
    pe = jnp.stack([pe_k, pe_v]).reshape(2, 1, CMP_LEN * HEAD_DIM)
    kvc = jnp.tile(_compress(chunks, chunks_next, pe, jnp.stack([ck1, cv1]), jnp.stack([ck2, cv2])),
                   (1, 1, 1, N_HEADS))
    overlap = _overlap_matrix(nch, nc, s // SEL_LEN)
    o_cmp, selmask = _cmp_topk(sl[S_QNR], kvc[0], kvc[1].transpose(0, 2, 1).astype(BF16), overlap,
                               min(256, s), nc)
    tq, tk = min(256, s), min(512, s)
    lane = jnp.arange(W_MIX)[None, None, :]
    block_id = (jnp.arange(s) >> SEL_SHIFT)[None, :, None]
    k_sel = jnp.where(lane < HEAD_DIM, sl[S_KV], (block_id == lane - HEAD_DIM).astype(BF16))
    o_sel = _causal_attn(sl[S_QR], k_sel, sl[S_KV], tq, tk, sel=selmask)
    o_win = _window_attn(sl[S_QR], sl[S_KV], sl[S_KV], dil=1, window=NSA_WINDOW, tq=256, out_dtype=BF16,
                         kv_lanes=(KW_LANE, VW_LANE))

    others = []
    for (wdw, dil) in DIL_CONFIGS[:0:-1]:
        others.append(_window_attn(*dil_in[dil], dil=dil, window=wdw // dil + 1, tq=256,
                                   out_dtype=F32, emit_lse=True))
    wdw, dil = DIL_CONFIGS[0]
    o_b = _window_attn(*dil_in[dil], dil=dil, window=wdw // dil + 1, tq=256, out_dtype=BF16, others=others)

    o_c = _stick_breaking(sl[S_CQ], sl[S_CK], sl[S_CV], min(256, s))

    bias_vec = jnp.zeros((1, W_MIX), F32).at[0, FOX_LANE:FOX_LANE + N_HEADS].set(fox_b)
    fcum = _fox_cumsum(misc, bias_vec, min(512, s))
    qk_bound = 1.02 * HEAD_DIM ** 0.5 * jnp.max(jnp.abs(qk_gain[6])) * jnp.max(jnp.abs(qk_gain[7])) + 0.05
    o_d = _causal_attn(sl[S_DQ], sl[S_DK], sl[S_DV], tq, tk, fcum=fcum[..., FOX_LANE:FOX_LANE + N_HEADS],
                       qk_bound=qk_bound)

    wb16 = w_branch.astype(BF16)
    flat = lambda a: a.reshape(n, W_MIX)
    return _merge(x2, norm_g, sc_a, sh_a, g_a, w_merge, flat(misc), _gate_expand(),
                  flat(o_cmp), flat(o_sel), flat(o_win), flat(o_b), flat(o_c), flat(o_d),
                  wb16[0], wb16[1], wb16[2], wb16[3], w_out.astype(BF16), s, min(512, s))


def kernel(x, c, positions, w_ada, b_ada, norm_mix, norm_ffn, w_in, qk_gain, nsa_pe_k, nsa_pe_v,
           nsa_ck_w1, nsa_ck_w2, nsa_cv_w1, nsa_cv_w2, fox_bias, w_branch, w_out,
           ffn_w1, ffn_w3, ffn_w2, router_w, moe_w1, moe_w3, moe_w2):
    b, s, d = x.shape
    depth = w_ada.shape[0]
    rope = _rope_tables(positions)
    mods = _ada(c, w_ada, b_ada).reshape(depth, b, 6, 1, d).transpose(0, 2, 1, 3, 4)
    x2 = x.reshape(b * s, d)
    for l in range(depth):
        mod = mods[l]
        x2 = _mixer_layer(x2, b, s, mod[0:3], norm_mix[l].reshape(1, d), rope, w_in[l], qk_gain[l],
                          nsa_pe_k[l], nsa_pe_v[l], nsa_ck_w1[l], nsa_ck_w2[l], nsa_cv_w1[l],
                          nsa_cv_w2[l], fox_bias[l], w_branch[l], w_out[l])
        sh_f, sc_f, g_f = mod[3], mod[4], mod[5]
        gn = norm_ffn[l].reshape(1, d)
        e = l // 2
        if l % 2 == 0:
            dff = ffn_w1.shape[2]
            x2 = _ffn(x2, gn, sc_f, sh_f, g_f, ffn_w1[e].astype(BF16), ffn_w3[e].astype(BF16),
                      ffn_w2[e].astype(BF16), s, min(1024, s), dff // 2)
        else:
            tt = min(1024, s)
            dff = moe_w1.shape[3]
            h2, rank, gate, cnt = _route(x2, gn, sc_f, sh_f, router_w[e].T, s, tt)
            counts = cnt[:, :, 0].astype(jnp.int32).reshape(-1)
            x2 = _moe(counts, x2, g_f, h2, rank, gate, moe_w1[e].astype(BF16), moe_w3[e].astype(BF16),
                      moe_w2[e].astype(BF16), s, tt, dff // 2, 128)
    return x2.reshape(b, s, d)
```

```python
import functools

import numpy as np
import jax
import jax.numpy as jnp
from jax import lax
from jax.experimental import pallas as pl
from jax.experimental.pallas import tpu as pltpu

F32 = jnp.float32
BF16 = jnp.bfloat16
HI = lax.Precision.HIGHEST

LANES = 128
VMEM_LIMIT = 52 * 1024 * 1024

HEAD_DIM = 64
HEAD_SHIFT = 6
N_HEADS = 4
W_MIX = N_HEADS * HEAD_DIM
N_MIXERS = 4
ROPE_THETA = 500000.0
ROPE_DIMS = HEAD_DIM // 4
ROPE_HALF = ROPE_DIMS // 2
EPS = 1e-6
LOG2E = 1.4426950408889634
NEG = -1e30
BIG = 1e30
CMP_LEN = 32
CMP_STRIDE = 16
SEL_LEN = 64
SEL_SHIFT = 6
TOPN = 16
SEL_NEG = 2.0 ** 60
NSA_WINDOW = 512
DIL_CONFIGS = ((128, 1), (512, 4), (2048, 16))
SB_STOP = -110.0
FOX_STOP = 108.0
GATE_LANE = 0
FOX_LANE = 3 * N_HEADS
VC_LANE = HEAD_DIM

NN = (((1,), (0,)), ((), ()))
NT = (((1,), (1,)), ((), ()))
TN = (((0,), (0,)), ((), ()))


def _dot(a, b, dims=NN, precision=None):
    return lax.dot_general(a, b, dims, precision=precision, preferred_element_type=F32)


def _dot_split(a, b_bf16, dims=NN):
    hi = a.astype(BF16)
    lo = (a - hi.astype(F32)).astype(BF16)
    return _dot(hi, b_bf16, dims) + _dot(lo, b_bf16, dims)


def _params(*sem):
    return pltpu.CompilerParams(dimension_semantics=sem, vmem_limit_bytes=VMEM_LIMIT)


def _mod_norm(x, g, sc, sh):
    ms = jnp.mean(x * x, axis=-1, keepdims=True)
    return (x * lax.rsqrt(ms + EPS) * g) * (1.0 + sc) + sh


def _head_masks():
    lane = lax.broadcasted_iota(jnp.int32, (1, W_MIX), 1)
    return [(lane >> HEAD_SHIFT) == h for h in range(N_HEADS)]


def _stack_masked(q, qm_scr, tq):
    for h, hm in enumerate(_head_masks()):
        qm_scr[h * tq:(h + 1) * tq, :] = jnp.where(hm, q, jnp.zeros_like(q))


def _pick_heads(stacked, tq, scale=None):
    out = None
    for h, hm in enumerate(_head_masks()):
        blk = stacked[h * tq:(h + 1) * tq, :]
        if scale is not None:
            blk = blk * scale[h]
        out = jnp.where(hm, blk, 0.0 if out is None else out)
    return out


def _ada_kernel(c_ref, w_ref, b_ref, o_ref):
    c = c_ref[...]
    ca = c * jax.nn.sigmoid(c)
    o_ref[0] = _dot(ca, w_ref[0], NN, HI) + b_ref[0]


def _ada(c, w_ada, b_ada):
    depth, d, n6 = w_ada.shape
    b = c.shape[0]
    tn = n6 // 4
    return pl.pallas_call(
        _ada_kernel,
        out_shape=jax.ShapeDtypeStruct((depth, b, n6), F32),
        grid=(depth, n6 // tn),
        in_specs=[
            pl.BlockSpec((b, d), lambda l, j: (0, 0)),
            pl.BlockSpec((1, d, tn), lambda l, j: (l, 0, j)),
            pl.BlockSpec((1, 1, tn), lambda l, j: (l, 0, j)),
        ],
        out_specs=pl.BlockSpec((1, b, tn), lambda l, j: (l, 0, j)),
        compiler_params=_params("arbitrary", "arbitrary"),
        name="ada_mod",
    )(c, w_ada, b_ada.reshape(depth, 1, n6))


(W_QA, W_KC, W_KV, W_BQ, W_BK, W_BV, W_CQ, W_CK, W_CV, W_DQ, W_DK, W_DV, W_MISC) = range(13)
N_WSLABS = 13
KV_QK_LANES = 2 * HEAD_DIM
_SLABS = (
    (W_QA, "all", None, F32), (W_QA, "all", "all", BF16), (W_KC, "all", None, F32),
    (W_KV, "kv", "kv", BF16),
    (W_BQ, "all", "all", BF16), (W_BK, "all", "all", BF16), (W_BV, None, None, BF16),
    (W_CQ, None, None, BF16), (W_CK, None, None, BF16), (W_CV, None, None, BF16),
    (W_DQ, "all", None, BF16), (W_DK, "all", None, BF16), (W_DV, None, None, BF16),
    (W_MISC, None, None, F32),
)
N_SLABS = len(_SLABS)
(S_QNR, S_QR, S_KC, S_KV, S_BQ, S_BK, S_BV, S_CQ, S_CK, S_CV, S_DQ, S_DK, S_DV, S_MISC) = range(N_SLABS)
_DIL_SLABS = (S_BQ, S_BK, S_BV)
_DILATIONS = tuple(d for (_, d) in DIL_CONFIGS if d > 1)
KSL_LANE, KW_LANE, VSL_LANE, VW_LANE = (i * HEAD_DIM for i in range(4))


def _proj_kernel(x_ref, g_ref, sc_ref, sh_ref, w_ref, gain_ref, bd_ref, rope_ref, *out_refs):
    h = _mod_norm(x_ref[...], g_ref[...], sc_ref[0], sh_ref[0]).astype(BF16)
    bd = bd_ref[...]
    key_lanes = lax.broadcasted_iota(jnp.int32, (1, W_MIX), 1) < KV_QK_LANES
    last_w, y_n = None, None
    raw = [_dot(h, w_ref[:, wi * W_MIX:(wi + 1) * W_MIX]) for wi in range(N_WSLABS)]
    for s, (wi, norm, rope, _) in enumerate(_SLABS):
        if wi != last_w:
            y_n = raw[wi]
            if norm:
                normed = y_n * lax.rsqrt(_dot_split(y_n * y_n, bd) + EPS)
                y_n = normed if norm == "all" else jnp.where(key_lanes, normed, y_n)
            last_w = wi
        y = y_n * gain_ref[s]
        if rope:
            c, s1, s2 = rope_ref[0, 0], rope_ref[0, 1], rope_ref[0, 2]
            if rope == "kv":
                c, s1, s2 = jnp.where(key_lanes, c, 1.0), jnp.where(key_lanes, s1, 0.0), jnp.where(key_lanes, s2, 0.0)
            y = y * c + pltpu.roll(y, W_MIX - ROPE_HALF, 1) * s1 + pltpu.roll(y, ROPE_HALF, 1) * s2
        out_refs[s][...] = y.astype(out_refs[s].dtype)
        if s in _DIL_SLABS:
            stage_scr = out_refs[-1]
            tt = y.shape[0]
            for c in range(W_MIX // LANES):
                stage_scr[c] = y[:, c * LANES:(c + 1) * LANES]
            for di, dil in enumerate(_DILATIONS):
                o_ref = out_refs[N_SLABS + _DIL_SLABS.index(s) * len(_DILATIONS) + di]
                for r in range(dil):
                    for c in range(W_MIX // LANES):
                        lo = r * W_MIX + c * LANES
                        o_ref[:, lo:lo + LANES] = (
                            stage_scr[c, pl.ds(r, tt // dil, stride=dil), :].astype(o_ref.dtype))


def _proj(x2, g, sc, sh, w_slab, gain, bd, rope, seq, tt):
    n, d = x2.shape
    tpb = seq // tt
    out_shape = [jax.ShapeDtypeStruct((n, W_MIX), dt) for (_, _, _, dt) in _SLABS]
    out_specs = [pl.BlockSpec((tt, W_MIX), lambda i: (i, 0)) for _ in _SLABS]
    for _ in _DIL_SLABS:
        for dil in _DILATIONS:
            out_shape.append(jax.ShapeDtypeStruct((n // dil, dil * W_MIX), BF16))
            out_specs.append(pl.BlockSpec((tt // dil, dil * W_MIX), lambda i: (i, 0)))
    return pl.pallas_call(
        _proj_kernel,
        out_shape=out_shape,
        grid=(n // tt,),
        in_specs=[
            pl.BlockSpec((tt, d), lambda i: (i, 0)),
            pl.BlockSpec((1, d), lambda i: (0, 0)),
            pl.BlockSpec((1, 1, d), lambda i: (i // tpb, 0, 0)),
            pl.BlockSpec((1, 1, d), lambda i: (i // tpb, 0, 0)),
            pl.BlockSpec((d, N_WSLABS * W_MIX), lambda i: (0, 0)),
            pl.BlockSpec((N_SLABS, 1, W_MIX), lambda i: (0, 0, 0)),
            pl.BlockSpec((W_MIX, W_MIX), lambda i: (0, 0)),
            pl.BlockSpec((1, 3, tt, W_MIX), lambda i: (i // tpb, 0, i % tpb, 0)),
        ],
        out_specs=out_specs,
        scratch_shapes=[pltpu.VMEM((W_MIX // LANES, tt, LANES), F32)],
        compiler_params=_params("arbitrary"),
        name="in_proj",
    )(x2, g, sc, sh, w_slab, gain, bd, rope)


def _pack_w_in(w_in, qk_gain):
    d = w_in.shape[0]
    o = 0
    a_q = w_in[:, o:o + W_MIX]; o += W_MIX
    a_kv = w_in[:, o:o + 6 * HEAD_DIM]; o += 6 * HEAD_DIM
    a_g = w_in[:, o:o + 3 * N_HEADS]; o += 3 * N_HEADS
    b_qkv = w_in[:, o:o + 3 * W_MIX]; o += 3 * W_MIX
    c_qkv = w_in[:, o:o + 3 * W_MIX]; o += 3 * W_MIX
    d_qkv = w_in[:, o:o + 3 * W_MIX]; o += 3 * W_MIX
    d_f = w_in[:, o:o + N_HEADS]; o += N_HEADS
    w_merge = w_in[:, o:]
    kc, vc, ksl, vsl, kw, vw = (a_kv[:, i * HEAD_DIM:(i + 1) * HEAD_DIM] for i in range(6))
    zeros = lambda w: jnp.zeros((d, w), w_in.dtype)
    misc = jnp.concatenate([a_g, d_f, zeros(VC_LANE - FOX_LANE - N_HEADS), vc, zeros(W_MIX - 2 * HEAD_DIM)], axis=1)
    slabs = [
        a_q, jnp.concatenate([kc, zeros(W_MIX - HEAD_DIM)], axis=1),
        jnp.concatenate([ksl, kw, vsl, vw], axis=1),
        b_qkv[:, :W_MIX], b_qkv[:, W_MIX:2 * W_MIX], b_qkv[:, 2 * W_MIX:],
        c_qkv[:, :W_MIX], c_qkv[:, W_MIX:2 * W_MIX], c_qkv[:, 2 * W_MIX:],
        d_qkv[:, :W_MIX], d_qkv[:, W_MIX:2 * W_MIX], d_qkv[:, 2 * W_MIX:],
        misc,
    ]
    w_slab = jnp.concatenate(slabs, axis=1).astype(BF16)
    scale = HEAD_DIM ** -0.5
    t4 = lambda gvec: jnp.tile(gvec, N_HEADS)
    one = jnp.ones((W_MIX,), F32)
    scale2 = scale * LOG2E
    gains = [
        t4(qk_gain[0]) * scale, t4(qk_gain[0]) * scale2,
        jnp.concatenate([qk_gain[1], jnp.ones((W_MIX - HEAD_DIM,), F32)]),
        jnp.concatenate([qk_gain[2], qk_gain[3], jnp.ones((W_MIX - KV_QK_LANES,), F32)]),
        t4(qk_gain[4]) * scale2, t4(qk_gain[5]), one,
        one * scale, one, one,
        t4(qk_gain[6]) * scale2, t4(qk_gain[7]), one,
        one,
    ]
    gain = jnp.stack(gains).reshape(N_SLABS, 1, W_MIX).astype(F32)
    return w_slab, gain, w_merge.astype(BF16)


def _rope_tables(positions):
    inv = ROPE_THETA ** (-jnp.arange(0, ROPE_DIMS, 2, dtype=F32) / ROPE_DIMS)
    ang = positions.astype(F32)[..., None] * inv
    cos, sin = jnp.cos(ang), jnp.sin(ang)
    b, s, _ = cos.shape
    pad1 = jnp.ones((b, s, HEAD_DIM - ROPE_DIMS), F32)
    pad0 = jnp.zeros((b, s, HEAD_DIM - ROPE_DIMS), F32)
    z8 = jnp.zeros_like(sin)
    c64 = jnp.concatenate([cos, cos, pad1], axis=-1)
    s1_64 = jnp.concatenate([-sin, z8, pad0], axis=-1)
    s2_64 = jnp.concatenate([z8, sin, pad0], axis=-1)
    heads = lambda t: jnp.tile(t, (1, 1, N_HEADS))
    return jnp.stack([heads(c64), heads(s1_64), heads(s2_64)], axis=1)


def _compress_kernel(a_ref, b_ref, pe_ref, w1_ref, w2_ref, o_ref):
    half = w1_ref.shape[1] // 2
    w1 = w1_ref[0]
    hid = (_dot(a_ref[0, 0], w1[:half], NN, HI) + _dot(b_ref[0, 0], w1[half:], NN, HI)
           + _dot(pe_ref[0], w1, NN, HI))
    o_ref[0, 0] = _dot(jax.nn.gelu(hid), w2_ref[0], NN, HI)


def _compress(ch, chn, pe, w1, w2):
    _, b, ncp, cw = ch.shape
    hid = w1.shape[2]
    return pl.pallas_call(
        _compress_kernel,
        out_shape=jax.ShapeDtypeStruct((2, b, ncp, HEAD_DIM), F32),
        grid=(2, b),
        in_specs=[
            pl.BlockSpec((1, 1, ncp, cw), lambda k, i: (k, i, 0, 0)),
            pl.BlockSpec((1, 1, ncp, cw), lambda k, i: (k, i, 0, 0)),
            pl.BlockSpec((1, 1, 2 * cw), lambda k, i: (k, 0, 0)),
            pl.BlockSpec((1, 2 * cw, hid), lambda k, i: (k, 0, 0)),
            pl.BlockSpec((1, hid, HEAD_DIM), lambda k, i: (k, 0, 0)),
        ],
        out_specs=pl.BlockSpec((1, 1, ncp, HEAD_DIM), lambda k, i: (k, i, 0, 0)),
        compiler_params=_params("arbitrary", "arbitrary"),
        name="nsa_compress",
    )(ch, chn, pe, w1, w2)


def _cmp_kernel(q_ref, kc_ref, vct_ref, ovt_ref, o_ref, sel_ref, qh_scr, ql_scr, acc_scr, *, tq, nc, n_sel):
    i = pl.program_id(1)
    ncp = kc_ref.shape[1]
    q = q_ref[0]
    q_hi = q.astype(BF16)
    _stack_masked(q_hi, qh_scr, tq)
    _stack_masked((q - q_hi.astype(F32)).astype(BF16), ql_scr, tq)
    kc = kc_ref[0]
    k_hi = kc.astype(BF16)
    k_lo = (kc - k_hi.astype(F32)).astype(BF16)
    ss = []
    for h in range(N_HEADS):
        rs = slice(h * tq, (h + 1) * tq)
        ss.append(_dot(k_hi, qh_scr[rs, :], NT) + _dot(k_lo, qh_scr[rs, :], NT)
                  + _dot(k_hi, ql_scr[rs, :], NT))
    t = i * tq + lax.broadcasted_iota(jnp.int32, (1, tq), 1)
    c = lax.broadcasted_iota(jnp.int32, (ncp, 1), 0)
    mask = (c * CMP_STRIDE + (CMP_LEN - 1) <= t) & (c < nc)
    psum = None
    ps = []
    for h in range(N_HEADS):
        sm = jnp.where(mask, ss[h], NEG)
        m = jnp.max(sm, axis=0, keepdims=True)
        e = jnp.where(mask, jnp.exp(sm - m), 0.0)
        l = jnp.sum(e, axis=0, keepdims=True)
        p = e * (1.0 / jnp.maximum(l, 1e-30))
        psum = p if psum is None else psum + p
        ps.append(p.astype(BF16))
    vct = vct_ref[0]
    for h in range(N_HEADS):
        rs = slice(h * HEAD_DIM, (h + 1) * HEAD_DIM)
        acc_scr[rs, :] = _dot(vct[rs, :], ps[h])
    o_ref[0] = acc_scr[...].T.astype(o_ref.dtype)
    imp = _dot(ovt_ref[...], psum, NN, HI)
    j = lax.broadcasted_iota(jnp.int32, (n_sel, 1), 0)
    cur = t >> SEL_SHIFT
    valid = j <= cur
    forced = (j == 0) | (j == cur) | (j == cur - 1)
    picked = valid & forced
    score = jnp.where(valid, jnp.where(forced, -3e38, imp), NEG)
    sel = jnp.where(picked, 1.0, 0.0)
    jf = j.astype(F32)
    for _ in range(min(TOPN, n_sel) - 3):
        mx = jnp.max(score, axis=0, keepdims=True)
        idx = jnp.min(jnp.where(score == mx, jf, float(n_sel)), axis=0, keepdims=True)
        pick = jf == idx
        sel = jnp.where(pick, 1.0, sel)
        score = jnp.where(pick, -3e38, score)
    sel_ref[0] = ((sel - 1.0) * SEL_NEG).T.astype(sel_ref.dtype)


def _cmp_topk(q_nr, kc_rep, vc_rep, overlap, tq, nc):
    b, seq, _ = q_nr.shape
    ncp = kc_rep.shape[1]
    n_sel = seq // SEL_LEN
    return pl.pallas_call(
        functools.partial(_cmp_kernel, tq=tq, nc=nc, n_sel=n_sel),
        out_shape=[jax.ShapeDtypeStruct((b, seq, W_MIX), BF16),
                   jax.ShapeDtypeStruct((b, seq, n_sel), BF16)],
        grid=(b, seq // tq),
        in_specs=[
            pl.BlockSpec((1, tq, W_MIX), lambda g, i: (g, i, 0)),
            pl.BlockSpec((1, ncp, W_MIX), lambda g, i: (g, 0, 0)),
            pl.BlockSpec((1, W_MIX, ncp), lambda g, i: (g, 0, 0)),
            pl.BlockSpec((n_sel, ncp), lambda g, i: (0, 0)),
        ],
        out_specs=[pl.BlockSpec((1, tq, W_MIX), lambda g, i: (g, i, 0)),
                   pl.BlockSpec((1, tq, n_sel), lambda g, i: (g, i, 0))],
        scratch_shapes=[pltpu.VMEM((N_HEADS * tq, W_MIX), BF16), pltpu.VMEM((N_HEADS * tq, W_MIX), BF16),
                        pltpu.VMEM((W_MIX, tq), F32)],
        compiler_params=_params("arbitrary", "arbitrary"),
        name="nsa_cmp_topk",
    )(q_nr, kc_rep, vc_rep, overlap)


def _causal_kernel(*refs, tq, tk, has_sel, has_bias):
    it = iter(refs)
    q_ref, k_ref, vt_ref = next(it), next(it), next(it)
    sel_ref = next(it) if has_sel else None
    fa_ref, fb_ref, fq_ref, thr_ref = (next(it) for _ in range(4)) if has_bias else (None,) * 4
    o_ref, qm_scr, acc_scr, sa_scr, sb_scr, m_scr, l_scr = (next(it) for _ in range(7))
    i = pl.program_id(1)
    t0 = i * tq
    lane = lax.broadcasted_iota(jnp.int32, (1, W_MIX), 1)
    if has_sel:
        n_sel = sel_ref.shape[2]
        qf = q_ref[0].astype(F32)
        selb = sel_ref[0].astype(F32)
        selb = jnp.concatenate([selb, jnp.zeros((tq, W_MIX - n_sel), F32)], axis=1)
        selb = pltpu.roll(selb, HEAD_DIM, 1)
        for h in range(N_HEADS):
            rot = qf if h == 0 else pltpu.roll(qf, W_MIX - h * HEAD_DIM, 1)
            qm_scr[h * tq:(h + 1) * tq, :] = jnp.where(lane < HEAD_DIM, rot, selb).astype(BF16)
    else:
        _stack_masked(q_ref[0], qm_scr, tq)
    acc_scr[...] = jnp.zeros(acc_scr.shape, F32)
    t_pos = t0 + lax.broadcasted_iota(jnp.int32, (1, tq), 1)

    def scores(kt, s_buf):
        k_t = k_ref[0, kt]
        if has_sel:
            blk = (kt * tk + lax.broadcasted_iota(jnp.int32, (tk, 1), 0)) >> SEL_SHIFT
            one_hot = jnp.where(blk == lane - HEAD_DIM, 1.0, 0.0).astype(BF16)
            k_t = jnp.where(lane < HEAD_DIM, k_t, one_hot)
        for h in range(N_HEADS):
            s = _dot(k_t, qm_scr[h * tq:(h + 1) * tq, :], NT)
            if has_bias:
                s = s + _dot(fa_ref[0, kt], fb_ref[0, h])
            s_buf[h] = s

    def update(kt, s_buf, diag):
        vt_t = vt_ref[0, kt]
        mask = None
        if diag:
            s_pos = kt * tk + lax.broadcasted_iota(jnp.int32, (tk, 1), 0)
            mask = s_pos <= t_pos
        ps, alphas = [], []
        for h in range(N_HEADS):
            s = s_buf[h]
            if mask is not None:
                s = jnp.where(mask, s, NEG)
            m_old = m_scr[h]
            m_new = jnp.maximum(m_old, jnp.max(s, axis=0, keepdims=True))
            alpha = jnp.exp2(m_old - m_new)
            m_scr[h] = m_new
            p = jnp.exp2(s - m_new)
            l_scr[h] = alpha * l_scr[h] + jnp.sum(p, axis=0, keepdims=True)
            ps.append(p.astype(BF16))
            alphas.append(alpha)
        for h in range(N_HEADS):
            rs = slice(h * HEAD_DIM, (h + 1) * HEAD_DIM)
            vs = slice(VSL_LANE, VSL_LANE + HEAD_DIM) if has_sel else rs
            acc_scr[rs, :] = alphas[h] * acc_scr[rs, :] + _dot(vt_t[vs, :], ps[h])

    def live(kt_next):
        lane = lax.broadcasted_iota(jnp.int32, (1, thr_ref.shape[3]), 1)
        hit = None
        for h in range(N_HEADS):
            top = jnp.max(fq_ref[0, h] - m_scr[h], axis=-1, keepdims=True)
            need = top >= thr_ref[0, h]
            hit = need if hit is None else (hit | need)
        return jnp.max(jnp.where(hit & (lane == kt_next), 1.0, 0.0)) > 0.5

    n_last = t0 // tk
    m_scr[...] = jnp.full(m_scr.shape, NEG, F32)
    l_scr[...] = jnp.zeros(l_scr.shape, F32)
    scores(n_last, sa_scr)
    scores(jnp.maximum(n_last - 1, 0), sb_scr)
    update(n_last, sa_scr, True)

    def pair(j):
        kt = n_last - 1 - 2 * j
        scores(kt - 1, sa_scr)
        update(kt, sb_scr, False)
        scores(jnp.maximum(kt - 2, 0), sb_scr)
        update(kt - 1, sa_scr, False)
        return kt - 2

    n_pairs = n_last // 2
    if has_bias:
        def cond(state):
            j, go = state
            return (j < n_pairs) & go

        def body(state):
            j, _ = state
            return j + 1, live(pair(j))

        _, go = lax.while_loop(cond, body, (jnp.int32(0), live(n_last - 1)))
    else:
        lax.fori_loop(0, n_pairs, lambda j, c: (pair(j), c)[1], 0)
        go = True

    @pl.when((n_last % 2 == 1) & go)
    def _():
        update(0, sb_scr, False)

    ls = [l_scr[h] for h in range(N_HEADS)]
    for h in range(N_HEADS):
        rs = slice(h * HEAD_DIM, (h + 1) * HEAD_DIM)
        acc_scr[rs, :] = acc_scr[rs, :] / ls[h]
    o_ref[0] = acc_scr[...].T.astype(o_ref.dtype)


def _split3(x):
    def cut(v):
        bits = lax.bitcast_convert_type(v, jnp.uint32) & jnp.uint32(0xFFFF0000)
        return lax.bitcast_convert_type(bits, F32)

    hi = cut(x)
    r1 = x - hi
    mid = cut(r1)
    lo = r1 - mid
    return hi.astype(BF16), mid.astype(BF16), lo.astype(BF16)


def _causal_attn(q, k, v, tq, tk, sel=None, fcum=None, qk_bound=None):
    b, s, _ = q.shape
    nk = s // tk
    rows = N_HEADS * tq
    n_s = N_HEADS
    vt = v.reshape(b, nk, tk, W_MIX).transpose(0, 1, 3, 2)
    args = [q, k.reshape(b, nk, tk, W_MIX), vt]
    in_specs = [
        pl.BlockSpec((1, tq, W_MIX), lambda a, i: (a, i, 0)),
        pl.BlockSpec((1, nk, tk, W_MIX), lambda a, i: (a, 0, 0, 0)),
        pl.BlockSpec((1, nk, W_MIX, tk), lambda a, i: (a, 0, 0, 0)),
    ]
    if sel is not None:
        args.append(sel)
        in_specs.append(pl.BlockSpec((1, tq, sel.shape[2]), lambda a, i: (a, i, 0)))
    if fcum is not None:
        fcum = fcum * LOG2E
        qk_bound = qk_bound * LOG2E
        nf = 2 * 3 + 2
        ones = jnp.ones(fcum.shape, BF16)
        zero = jnp.zeros(fcum.shape, BF16)
        parts = _split3(fcum)
        key_f = jnp.stack([ones, ones, ones] + [-p for p in parts] + [zero, zero], axis=-1)
        qry_f = jnp.stack(list(parts) + [ones, ones, ones, zero, zero], axis=-1)
        fa = key_f.reshape(b, nk, tk, N_HEADS * nf)
        eye = jnp.eye(N_HEADS, dtype=BF16)
        fb = jnp.einsum('bshf,hg->bgshf', qry_f, eye).reshape(b, N_HEADS, s, N_HEADS * nf).transpose(0, 1, 3, 2)
        f_rows = fcum.transpose(0, 2, 1)
        f_end = f_rows[:, :, tk - 1::tk]
        thr = jnp.pad(f_end - (qk_bound + FOX_STOP * LOG2E), ((0, 0), (0, 0), (0, LANES - nk)),
                      constant_values=BIG).reshape(b, N_HEADS, 1, LANES)
        args += [fa, fb, f_rows.reshape(b, N_HEADS, 1, s), thr]
        in_specs += [pl.BlockSpec((1, nk, tk, N_HEADS * nf), lambda a, i: (a, 0, 0, 0)),
                     pl.BlockSpec((1, N_HEADS, N_HEADS * nf, tq), lambda a, i: (a, 0, 0, i)),
                     pl.BlockSpec((1, N_HEADS, 1, tq), lambda a, i: (a, 0, 0, i)),
                     pl.BlockSpec((1, N_HEADS, 1, LANES), lambda a, i: (a, 0, 0, 0))]
    return pl.pallas_call(
        functools.partial(_causal_kernel, tq=tq, tk=tk, has_sel=sel is not None, has_bias=fcum is not None),
        out_shape=jax.ShapeDtypeStruct((b, s, W_MIX), BF16),
        grid=(b, s // tq),
        in_specs=in_specs,
        out_specs=pl.BlockSpec((1, tq, W_MIX), lambda a, i: (a, i, 0)),
        scratch_shapes=[pltpu.VMEM((rows, W_MIX), BF16), pltpu.VMEM((W_MIX, tq), F32),
                        pltpu.VMEM((n_s, tk, tq), F32), pltpu.VMEM((n_s, tk, tq), F32),
                        pltpu.VMEM((N_HEADS, 1, tq), F32), pltpu.VMEM((N_HEADS, 1, tq), F32)],
        compiler_params=_params("arbitrary", "arbitrary"),
        name="causal_sel%d_bias%d" % (sel is not None, fcum is not None),
    )(*args)


def _window_kernel(*refs, tq, wk, pad, window, ls, emit_lse, other_dils, kv_lanes):
    it = iter(refs)
    q_ref, k_ref, v_ref = next(it), next(it), next(it)
    n_other = len(other_dils)
    others_in = [(next(it), next(it)) for _ in range(n_other)]
    wexp_ref = next(it) if n_other else None
    o_ref = next(it)
    lse_ref = next(it) if emit_lse else None
    qm_scr, acc_scr = next(it), next(it)
    others = []
    if n_other:
        og_scr, lg_scr = next(it), next(it)
        for g, (dg, (og_ref, lg_ref)) in enumerate(zip(other_dils, others_in)):
            for r in range(dg):
                rows = pl.ds(r, tq // dg, stride=dg)
                for c in range(W_MIX // LANES):
                    lo = r * W_MIX + c * LANES
                    og_scr[g, c, rows, :] = og_ref[0, :, lo:lo + LANES]
                lg_scr[g, rows, :] = lg_ref[0, :, r * LANES:(r + 1) * LANES]
            others.append((og_scr.at[g], lg_scr.at[g]))
    t0 = pl.program_id(2) * tq
    start = pl.multiple_of(jnp.clip(t0 - pad, 0, ls - wk), LANES)
    if kv_lanes is None:
        _stack_masked(q_ref[0], qm_scr, tq)
    else:
        lane = lax.broadcasted_iota(jnp.int32, (1, W_MIX), 1)
        on_key = (lane >= kv_lanes[0]) & (lane < kv_lanes[0] + HEAD_DIM)
        qf = q_ref[0].astype(F32)
        for h in range(N_HEADS):
            shift = (kv_lanes[0] - h * HEAD_DIM) % W_MIX
            rot = pltpu.roll(qf, shift, 1) if shift else qf
            qm_scr[h * tq:(h + 1) * tq, :] = jnp.where(on_key, rot, 0.0).astype(BF16)
    k_w = k_ref[0, pl.ds(start, wk), :]
    vt_w = v_ref[0, pl.ds(start, wk), :].astype(F32).T.astype(BF16)
    ss = [_dot(k_w, qm_scr[h * tq:(h + 1) * tq, :], NT) for h in range(N_HEADS)]
    t_pos = t0 + lax.broadcasted_iota(jnp.int32, (1, tq), 1)
    s_pos = start + lax.broadcasted_iota(jnp.int32, (wk, 1), 0)
    mask = (s_pos <= t_pos) & (t_pos - s_pos < window)
    ps, inv_ls, lses = [], [], []
    for h in range(N_HEADS):
        s = jnp.where(mask, ss[h], NEG)
        m = jnp.max(s, axis=0, keepdims=True)
        p = jnp.exp2(s - m)
        l = jnp.sum(p, axis=0, keepdims=True)
        ps.append(p.astype(BF16))
        inv_ls.append(1.0 / l)
        lses.append(m + jnp.log2(l))
    for h in range(N_HEADS):
        rs = slice(h * HEAD_DIM, (h + 1) * HEAD_DIM)
        vs = rs if kv_lanes is None else slice(kv_lanes[1], kv_lanes[1] + HEAD_DIM)
        acc_scr[rs, :] = _dot(vt_w[vs, :], ps[h]) * inv_ls[h]
    o_self = acc_scr[...].T
    if emit_lse or n_other:
        row = lax.broadcasted_iota(jnp.int32, (LANES, 1), 0)
        stat = jnp.zeros((LANES, tq), F32)
        for h in range(N_HEADS):
            stat = jnp.where(row == h, lses[h], stat)
        lse_tile = stat.T
    if n_other:
        lg_t = [lg[...].T for (_, lg) in others]
        row = lax.broadcasted_iota(jnp.int32, (LANES, 1), 0)
        wmat = jnp.zeros((LANES, tq), F32)
        for h in range(N_HEADS):
            group_lse = [lses[h]] + [t[h:h + 1, :] for t in lg_t]
            top = functools.reduce(jnp.maximum, group_lse)
            ws = [jnp.exp2(x - top) for x in group_lse]
            inv_den = 1.0 / functools.reduce(jnp.add, ws)
            for g, w in enumerate(ws):
                wmat = jnp.where(row == g * N_HEADS + h, w * inv_den, wmat)
        wt = wmat.T
        groups = [o_self] + [jnp.concatenate([og[c] for c in range(W_MIX // LANES)], axis=1)
                             for (og, _) in others]
        out = None
        for g, o_g in enumerate(groups):
            term = _dot_split(wt, wexp_ref[g]) * o_g
            out = term if out is None else out + term
        o_ref[0] = out.astype(o_ref.dtype)
    else:
        o_ref[0] = o_self.astype(o_ref.dtype)
    if emit_lse:
        lse_ref[0] = lse_tile


def _window_attn(q, k, v, *, dil, window, tq, out_dtype, emit_lse=False, others=(), kv_lanes=None):
    b, ls, _ = q.shape
    tq = min(tq, ls)
    pad = -(-(window - 1) // LANES) * LANES
    wk = min(tq + pad, ls)
    rows = N_HEADS * tq
    args = [q, k, v]
    in_specs = [
        pl.BlockSpec((1, tq, W_MIX), lambda a, r, i: (a, i, r)),
        pl.BlockSpec((1, ls, W_MIX), lambda a, r, i: (a, 0, r)),
        pl.BlockSpec((1, ls, W_MIX), lambda a, r, i: (a, 0, r)),
    ]
    scratch = [pltpu.VMEM((rows, W_MIX), BF16), pltpu.VMEM((W_MIX, tq), F32)]
    for (o_g, lse_g, dg) in others:
        args += [o_g, lse_g]
        in_specs += [pl.BlockSpec((1, tq // dg, dg * W_MIX), lambda a, r, i: (a, i, 0)),
                     pl.BlockSpec((1, tq // dg, dg * LANES), lambda a, r, i: (a, i, 0))]
    if others:
        wexp = np.zeros((len(others) + 1, LANES, W_MIX), np.float32)
        for g in range(len(others) + 1):
            for h in range(N_HEADS):
                wexp[g, g * N_HEADS + h, h * HEAD_DIM:(h + 1) * HEAD_DIM] = 1.0
        args.append(jnp.asarray(wexp, BF16))
        in_specs.append(pl.BlockSpec(wexp.shape, lambda a, r, i: (0, 0, 0)))
        scratch += [pltpu.VMEM((len(others), W_MIX // LANES, tq, LANES), F32),
                    pltpu.VMEM((len(others), tq, LANES), F32)]
    out_shape = [jax.ShapeDtypeStruct((b, ls, dil * W_MIX), out_dtype)]
    out_specs = [pl.BlockSpec((1, tq, W_MIX), lambda a, r, i: (a, i, r))]
    if emit_lse:
        out_shape.append(jax.ShapeDtypeStruct((b, ls, dil * LANES), F32))
        out_specs.append(pl.BlockSpec((1, tq, LANES), lambda a, r, i: (a, i, r)))
    res = pl.pallas_call(
        functools.partial(_window_kernel, tq=tq, wk=wk, pad=pad, window=window, ls=ls,
                          emit_lse=emit_lse, other_dils=tuple(dg for (_, _, dg) in others),
                          kv_lanes=kv_lanes),
        out_shape=out_shape,
        grid=(b, dil, ls // tq),
        in_specs=in_specs,
        out_specs=out_specs,
        scratch_shapes=scratch,
        compiler_params=_params("arbitrary", "arbitrary", "arbitrary"),
        name="window_d%d_w%d" % (dil, window),
    )(*args)
    return (res[0], res[1], dil) if emit_lse else res[0]


def _sb_kernel(q_ref, k_ref, vt_ref, tri_ref, o_ref, qm_scr, carry_scr, acc_scr, *, tq):
    i = pl.program_id(1)
    t0 = i * tq
    _stack_masked(q_ref[0], qm_scr, tq)
    carry_scr[...] = jnp.zeros(carry_scr.shape, F32)
    acc_scr[...] = jnp.zeros(acc_scr.shape, F32)
    t_pos = t0 + lax.broadcasted_iota(jnp.int32, (1, tq), 1)

    def tile(kt, diag):
        k_t = k_ref[0, kt]
        vt_t = vt_ref[0, kt]
        tri = tri_ref[...]
        zs = [_dot(k_t, qm_scr[h * tq:(h + 1) * tq, :], NT) for h in range(N_HEADS)]
        if diag:
            s_pos = kt * tq + lax.broadcasted_iota(jnp.int32, (tq, 1), 0)
            strict = s_pos < t_pos
        weights = []
        for h in range(N_HEADS):
            z = zs[h]
            lg = -(jnp.maximum(z, 0.0) + jnp.log(1.0 + jnp.exp(-jnp.abs(z))))
            if diag:
                lg = jnp.where(strict, lg, 0.0)
            hi = lg.astype(BF16)
            lo = (lg - hi.astype(F32)).astype(BF16)
            cum = _dot(tri, hi) + _dot(tri, lo) + carry_scr[h]
            a = jnp.exp(z + cum)
            if diag:
                a = jnp.where(strict, a, 0.0)
            weights.append(a.astype(BF16))
            carry_scr[h] += jnp.sum(lg, axis=0, keepdims=True)
        for h in range(N_HEADS):
            rs = slice(h * HEAD_DIM, (h + 1) * HEAD_DIM)
            acc_scr[rs, :] += _dot(vt_t[rs, :], weights[h])

    tile(i, True)

    def cond(state):
        j, top = state
        return (j < i) & (top > SB_STOP)

    def body(state):
        j, _ = state
        tile(i - 1 - j, False)
        return j + 1, jnp.max(carry_scr[...])

    lax.while_loop(cond, body, (jnp.int32(0), jnp.max(carry_scr[...])))
    o_ref[0] = acc_scr[...].T.astype(o_ref.dtype)


def _stick_breaking(q, k, v, tq):
    b, s, _ = q.shape
    nk = s // tq
    rows = N_HEADS * tq
    tri = jnp.asarray(np.triu(np.ones((tq, tq), np.float32)), BF16)
    vt = v.reshape(b, nk, tq, W_MIX).transpose(0, 1, 3, 2)
    return pl.pallas_call(
        functools.partial(_sb_kernel, tq=tq),
        out_shape=jax.ShapeDtypeStruct((b, s, W_MIX), BF16),
        grid=(b, nk),
        in_specs=[
            pl.BlockSpec((1, tq, W_MIX), lambda a, i: (a, i, 0)),
            pl.BlockSpec((1, nk, tq, W_MIX), lambda a, i: (a, 0, 0, 0)),
            pl.BlockSpec((1, nk, W_MIX, tq), lambda a, i: (a, 0, 0, 0)),
            pl.BlockSpec((tq, tq), lambda a, i: (0, 0)),
        ],
        out_specs=pl.BlockSpec((1, tq, W_MIX), lambda a, i: (a, i, 0)),
        scratch_shapes=[pltpu.VMEM((rows, W_MIX), BF16), pltpu.VMEM((N_HEADS, 1, tq), F32),
                        pltpu.VMEM((W_MIX, tq), F32)],
        compiler_params=_params("arbitrary", "arbitrary"),
        name="stick_breaking",
    )(q, k.reshape(b, nk, tq, W_MIX), vt, tri)


def _foxcum_kernel(x_ref, b_ref, tri_ref, o_ref, carry_scr):
    @pl.when(pl.program_id(1) == 0)
    def _():
        carry_scr[...] = jnp.zeros(carry_scr.shape, F32)

    z = x_ref[0] + b_ref[...]
    logf = jnp.minimum(z, 0.0) - jnp.log(1.0 + jnp.exp(-jnp.abs(z)))
    tri = tri_ref[...]
    hi = logf.astype(BF16)
    r1 = logf - hi.astype(F32)
    mid = r1.astype(BF16)
    lo = (r1 - mid.astype(F32)).astype(BF16)
    cum = _dot(tri, hi) + _dot(tri, mid) + _dot(tri, lo) + carry_scr[...]
    o_ref[0] = cum
    carry_scr[...] = cum[cum.shape[0] - 1:, :]


def _fox_cumsum(misc, bias_vec, tc):
    b, s, w = misc.shape
    tri = jnp.asarray(np.tril(np.ones((tc, tc), np.float32)), BF16)
    return pl.pallas_call(
        _foxcum_kernel,
        out_shape=jax.ShapeDtypeStruct((b, s, w), F32),
        grid=(b, s // tc),
        in_specs=[
            pl.BlockSpec((1, tc, w), lambda a, i: (a, i, 0)),
            pl.BlockSpec((1, w), lambda a, i: (0, 0)),
            pl.BlockSpec((tc, tc), lambda a, i: (0, 0)),
        ],
        out_specs=pl.BlockSpec((1, tc, w), lambda a, i: (a, i, 0)),
        scratch_shapes=[pltpu.VMEM((1, w), F32)],
        compiler_params=_params("arbitrary", "arbitrary"),
        name="fox_cumsum",
    )(misc, bias_vec, tri)


def _merge_kernel(x_ref, g_ref, sc_ref, sh_ref, ga_ref, wm_ref, misc_ref, pg_ref,
                  ocmp_ref, osel_ref, owin_ref, ob_ref, oc_ref, od_ref,
                  wa_ref, wb_ref, wc_ref, wd_ref, wo_ref, o_ref):
    x = x_ref[...]
    d = x.shape[1]
    h = _mod_norm(x, g_ref[...], sc_ref[0], sh_ref[0]).astype(BF16)
    gate = jax.nn.sigmoid(misc_ref[...])
    o_a = (_dot_split(gate, pg_ref[0]) * ocmp_ref[...].astype(F32)
           + _dot_split(gate, pg_ref[1]) * osel_ref[...].astype(F32)
           + _dot_split(gate, pg_ref[2]) * owin_ref[...].astype(F32)).astype(BF16)
    mixed = jnp.zeros(x.shape, F32)
    for m, (o_m, w_ref) in enumerate(((o_a, wa_ref), (ob_ref[...], wb_ref),
                                      (oc_ref[...], wc_ref), (od_ref[...], wd_ref))):
        y = _dot(o_m, w_ref[...])
        gl = _dot(h, wm_ref[:, m * d:(m + 1) * d])
        mixed = mixed + jax.nn.sigmoid(gl) * y
    o_ref[...] = x + ga_ref[0] * _dot(mixed.astype(BF16), wo_ref[...])


def _merge(x2, g, sc, sh, ga, w_merge, misc, pg, o_cmp, o_sel, o_win, o_b, o_c, o_d,
           wa, wb, wc, wd, wo, seq, tt):
    n, d = x2.shape
    tpb = seq // tt
    row = lambda w: pl.BlockSpec((tt, w), lambda i: (i, 0))
    full = lambda a: pl.BlockSpec(a.shape, lambda i: (0,) * a.ndim)
    per_b = pl.BlockSpec((1, 1, d), lambda i: (i // tpb, 0, 0))
    return pl.pallas_call(
        _merge_kernel,
        out_shape=jax.ShapeDtypeStruct((n, d), F32),
        grid=(n // tt,),
        in_specs=[row(d), full(g), per_b, per_b, per_b, full(w_merge), row(W_MIX), full(pg)]
        + [row(W_MIX)] * 6 + [full(wa), full(wb), full(wc), full(wd), full(wo)],
        out_specs=row(d),
        compiler_params=_params("arbitrary"),
        name="merge_out",
    )(x2, g, sc, sh, ga, w_merge, misc, pg, o_cmp, o_sel, o_win, o_b, o_c, o_d, wa, wb, wc, wd, wo)


def _ffn_kernel(x_ref, g_ref, sc_ref, sh_ref, gf_ref, w1_ref, w3_ref, w2_ref, o_ref, h_scr, acc_scr):
    f = pl.program_id(1)

    @pl.when(f == 0)
    def _():
        h_scr[...] = _mod_norm(x_ref[...], g_ref[...], sc_ref[0], sh_ref[0]).astype(BF16)
        acc_scr[...] = jnp.zeros(acc_scr.shape, F32)

    h = h_scr[...]
    a = _dot(h, w1_ref[...])
    b = _dot(h, w3_ref[...])
    acc_scr[...] += _dot((a * jax.nn.sigmoid(a) * b).astype(BF16), w2_ref[...])

    @pl.when(f == pl.num_programs(1) - 1)
    def _():
        o_ref[...] = x_ref[...] + gf_ref[0] * acc_scr[...]


def _ffn(x2, g, sc, sh, gf, w1, w3, w2, seq, tt, tf):
    n, d = x2.shape
    dff = w1.shape[1]
    tpb = seq // tt
    per_b = pl.BlockSpec((1, 1, d), lambda i, f: (i // tpb, 0, 0))
    return pl.pallas_call(
        _ffn_kernel,
        out_shape=jax.ShapeDtypeStruct((n, d), F32),
        grid=(n // tt, dff // tf),
        in_specs=[
            pl.BlockSpec((tt, d), lambda i, f: (i, 0)),
            pl.BlockSpec((1, d), lambda i, f: (0, 0)),
            per_b, per_b, per_b,
            pl.BlockSpec((d, tf), lambda i, f: (0, f)),
            pl.BlockSpec((d, tf), lambda i, f: (0, f)),
            pl.BlockSpec((tf, d), lambda i, f: (f, 0)),
        ],
        out_specs=pl.BlockSpec((tt, d), lambda i, f: (i, 0)),
        scratch_shapes=[pltpu.VMEM((tt, d), BF16), pltpu.VMEM((tt, d), F32)],
        compiler_params=_params("arbitrary", "arbitrary"),
        name="ffn_swiglu",
    )(x2, g, sc, sh, gf, w1, w3, w2)


def _route_kernel(x_ref, g_ref, sc_ref, sh_ref, rw_ref, up_ref, h_ref, rank_ref, gate_ref, cnt_ref):
    hf = _mod_norm(x_ref[...], g_ref[...], sc_ref[0], sh_ref[0])
    h_ref[...] = hf.astype(BF16)
    logits = _dot(rw_ref[...], hf, NT, HI)
    ne, tt = logits.shape
    e_idx = lax.broadcasted_iota(jnp.int32, (ne, 1), 0).astype(F32)
    v1 = jnp.max(logits, axis=0, keepdims=True)
    i1 = jnp.min(jnp.where(logits == v1, e_idx, float(ne)), axis=0, keepdims=True)
    m1 = e_idx == i1
    rest = jnp.where(m1, -3e38, logits)
    v2 = jnp.max(rest, axis=0, keepdims=True)
    i2 = jnp.min(jnp.where(rest == v2, e_idx, float(ne)), axis=0, keepdims=True)
    m2 = e_idx == i2
    e2 = jnp.exp(v2 - v1)
    g1 = 1.0 / (1.0 + e2)
    g2 = e2 / (1.0 + e2)
    routed = m1 | m2
    rf = jnp.where(routed, 1.0, 0.0)
    rank = _dot(rf.astype(BF16), up_ref[...])
    rank = jnp.where(routed, rank, -1.0)
    gate = jnp.where(m1, g1, 0.0) + jnp.where(m2, g2, 0.0)
    for e in range(ne):
        rank_ref[0, e] = rank[e:e + 1, :]
        gate_ref[0, e] = gate[e:e + 1, :]
    cnt = jnp.sum(rf, axis=1, keepdims=True)
    cnt_ref[0] = jnp.broadcast_to(cnt, (ne, LANES))


def _route(x2, g, sc, sh, rw_t, seq, tt):
    n, d = x2.shape
    ne = rw_t.shape[0]
    tpb = seq // tt
    nt = n // tt
    upper = jnp.asarray(np.triu(np.ones((tt, tt), np.float32), 1), BF16)
    per_b = pl.BlockSpec((1, 1, d), lambda i: (i // tpb, 0, 0))
    return pl.pallas_call(
        _route_kernel,
        out_shape=[jax.ShapeDtypeStruct((n, d), BF16),
                   jax.ShapeDtypeStruct((nt, ne, 1, tt), F32),
                   jax.ShapeDtypeStruct((nt, ne, 1, tt), F32),
                   jax.ShapeDtypeStruct((nt, ne, LANES), F32)],
        grid=(nt,),
        in_specs=[
            pl.BlockSpec((tt, d), lambda i: (i, 0)),
            pl.BlockSpec((1, d), lambda i: (0, 0)),
            per_b, per_b,
            pl.BlockSpec((ne, d), lambda i: (0, 0)),
            pl.BlockSpec((tt, tt), lambda i: (0, 0)),
        ],
        out_specs=[pl.BlockSpec((tt, d), lambda i: (i, 0)),
                   pl.BlockSpec((1, ne, 1, tt), lambda i: (i, 0, 0, 0)),
                   pl.BlockSpec((1, ne, 1, tt), lambda i: (i, 0, 0, 0)),
                   pl.BlockSpec((1, ne, LANES), lambda i: (i, 0, 0))],
        compiler_params=_params("arbitrary"),
        name="moe_route",
    )(x2, g, sc, sh, rw_t, upper)


def _moe_kernel(cnt_ref, x_ref, gf_ref, h_ref, rank_ref, gate_ref, w1_ref, w3_ref, w2_ref,
                o_ref, acc_scr, xs_scr, y_scr, *, chunk):
    i, e, f = pl.program_id(0), pl.program_id(1), pl.program_id(2)
    ne, nf = pl.num_programs(1), pl.num_programs(2)

    @pl.when((e == 0) & (f == 0))
    def _():
        acc_scr[...] = jnp.zeros(acc_scr.shape, F32)

    count = cnt_ref[i * ne + e]
    rank = rank_ref[0, 0]
    gate = gate_ref[0, 0]
    n_small = (count + chunk - 1) // chunk
    n_big = (count + 2 * chunk - 1) // (2 * chunk)

    def one_hot(c, rows):
        r = c * rows + lax.broadcasted_iota(jnp.int32, (rows, 1), 0)
        return rank == r.astype(F32)

    def rows_of(c, rows):
        return pl.ds(pl.multiple_of(c * rows, rows), rows)

    @pl.when(f == 0)
    def _():
        h = h_ref[...]

        def gather(c, carry):
            p = jnp.where(one_hot(c, chunk), 1.0, 0.0).astype(BF16)
            xs_scr[rows_of(c, chunk), :] = _dot(p, h).astype(BF16)
            return carry

        lax.fori_loop(0, n_small, gather, 0)

        def clear(c, carry):
            y_scr[rows_of(c, 2 * chunk), :] = jnp.zeros((2 * chunk, y_scr.shape[1]), F32)
            return carry

        lax.fori_loop(0, n_big, clear, 0)

    def expert(c, carry):
        xs = xs_scr[rows_of(c, chunk), :]
        a = _dot(xs, w1_ref[0])
        b = _dot(xs, w3_ref[0])
        y_scr[rows_of(c, chunk), :] += _dot((a * jax.nn.sigmoid(a) * b).astype(BF16), w2_ref[0])
        return carry

    lax.fori_loop(0, n_small, expert, 0)

    @pl.when(f == nf - 1)
    def _():
        def scatter(c, carry):
            hit = one_hot(c, 2 * chunk)
            p = jnp.where(hit, 1.0, 0.0).astype(BF16)
            gcol = jnp.sum(jnp.where(hit, gate, 0.0), axis=-1, keepdims=True)
            acc_scr[...] += _dot(p, (y_scr[rows_of(c, 2 * chunk), :] * gcol).astype(BF16), TN)
            return carry

        lax.fori_loop(0, n_big, scatter, 0)

    @pl.when((e == ne - 1) & (f == nf - 1))
    def _():
        o_ref[...] = x_ref[...] + gf_ref[0] * acc_scr[...]


def _moe(counts, x2, gf, h2, rank, gate, w1, w3, w2, seq, tt, tf, chunk):
    n, d = x2.shape
    ne, _, dff = w1.shape
    tpb = seq // tt
    grid_spec = pltpu.PrefetchScalarGridSpec(
        num_scalar_prefetch=1,
        grid=(n // tt, ne, dff // tf),
        in_specs=[
            pl.BlockSpec((tt, d), lambda i, e, f, c: (i, 0)),
            pl.BlockSpec((1, 1, d), lambda i, e, f, c: (i // tpb, 0, 0)),
            pl.BlockSpec((tt, d), lambda i, e, f, c: (i, 0)),
            pl.BlockSpec((1, 1, 1, tt), lambda i, e, f, c: (i, e, 0, 0)),
            pl.BlockSpec((1, 1, 1, tt), lambda i, e, f, c: (i, e, 0, 0)),
            pl.BlockSpec((1, d, tf), lambda i, e, f, c: (e, 0, f)),
            pl.BlockSpec((1, d, tf), lambda i, e, f, c: (e, 0, f)),
            pl.BlockSpec((1, tf, d), lambda i, e, f, c: (e, f, 0)),
        ],
        out_specs=pl.BlockSpec((tt, d), lambda i, e, f, c: (i, 0)),
        scratch_shapes=[pltpu.VMEM((tt, d), F32), pltpu.VMEM((tt, d), BF16), pltpu.VMEM((tt, d), F32)],
    )
    return pl.pallas_call(
        functools.partial(_moe_kernel, chunk=chunk),
        out_shape=jax.ShapeDtypeStruct((n, d), F32),
        grid_spec=grid_spec,
        compiler_params=_params("arbitrary", "arbitrary", "arbitrary"),
        name="moe_experts",
    )(counts, x2, gf, h2, rank, gate, w1, w3, w2)


def _overlap_matrix(ncp, nc, n_sel):
    c0 = np.arange(ncp) * CMP_STRIDE
    c1 = c0 + CMP_LEN
    s0 = np.arange(n_sel) * SEL_LEN
    s1 = s0 + SEL_LEN
    ov = ((c0[:, None] < s1[None, :]) & (c1[:, None] > s0[None, :])).astype(np.float32)
    ov[nc:] = 0.0
    return jnp.asarray(ov.T)


def _gate_expand():
    pg = np.zeros((3, W_MIX, W_MIX), np.float32)
    for br in range(3):
        for h in range(N_HEADS):
            pg[br, GATE_LANE + 3 * h + br, h * HEAD_DIM:(h + 1) * HEAD_DIM] = 1.0
    return jnp.asarray(pg, BF16)


def _mixer_layer(x2, b, s, mod, norm_g, rope, w_in, qk_gain, pe_k, pe_v, ck1, ck2, cv1, cv2,
                 fox_b, w_branch, w_out):
    n, d = x2.shape
    sh_a, sc_a, g_a = mod[0], mod[1], mod[2]
    w_slab, gain, w_merge = _pack_w_in(w_in, qk_gain)
    bd = jnp.asarray(np.kron(np.eye(N_HEADS), np.full((HEAD_DIM, HEAD_DIM), 1.0 / HEAD_DIM)), BF16)
    outs = _proj(x2, norm_g, sc_a, sh_a, w_slab, gain, bd, rope, s, min(512, s))
    sl = [a.reshape(b, s, W_MIX) for a in outs[:N_SLABS]]
    dil_in = {1: (sl[S_BQ], sl[S_BK], sl[S_BV])}
    for di, dil in enumerate(_DILATIONS):
        dil_in[dil] = tuple(outs[N_SLABS + si * len(_DILATIONS) + di].reshape(b, s // dil, dil * W_MIX)
                            for si in range(len(_DIL_SLABS)))
    misc = sl[S_MISC]

    nch = s // CMP_STRIDE
    nc = nch - CMP_LEN // CMP_STRIDE + 1
    kc_raw = sl[S_KC][..., :HEAD_DIM]
    vc_raw = misc[..., VC_LANE:VC_LANE + HEAD_DIM]
    chunks = jnp.stack([kc_raw, vc_raw]).reshape(2, b, nch, CMP_STRIDE * HEAD_DIM)
    chunks_next = jnp.concatenate([chunks[:, :, 1:], jnp.zeros_like(chunks[:, :, :1])], axis=2)
    pe = jnp.stack([pe_k, pe_v]).reshape(2, 1, CMP_LEN * HEAD_DIM)
    kvc = jnp.tile(_compress(chunks, chunks_next, pe, jnp.stack([ck1, cv1]), jnp.stack([ck2, cv2])),
                   (1, 1, 1, N_HEADS))
    overlap = _overlap_matrix(nch, nc, s // SEL_LEN)
    o_cmp, selmask = _cmp_topk(sl[S_QNR], kvc[0], kvc[1].transpose(0, 2, 1).astype(BF16), overlap,
                               min(256, s), nc)
    tq, tk = min(256, s), min(512, s)
    o_sel = _causal_attn(sl[S_QR], sl[S_KV], sl[S_KV], tq, tk, sel=selmask)
    o_win = _window_attn(sl[S_QR], sl[S_KV], sl[S_KV], dil=1, window=NSA_WINDOW, tq=256, out_dtype=BF16,
                         kv_lanes=(KW_LANE, VW_LANE))

    others = []
    for (wdw, dil) in DIL_CONFIGS[:0:-1]:
        others.append(_window_attn(*dil_in[dil], dil=dil, window=wdw // dil + 1, tq=256,
                                   out_dtype=F32, emit_lse=True))
    wdw, dil = DIL_CONFIGS[0]
    o_b = _window_attn(*dil_in[dil], dil=dil, window=wdw // dil + 1, tq=256, out_dtype=BF16, others=others)

    o_c = _stick_breaking(sl[S_CQ], sl[S_CK], sl[S_CV], min(256, s))

    bias_vec = jnp.zeros((1, W_MIX), F32).at[0, FOX_LANE:FOX_LANE + N_HEADS].set(fox_b)
    fcum = _fox_cumsum(misc, bias_vec, min(512, s))
    qk_bound = 1.02 * HEAD_DIM ** 0.5 * jnp.max(jnp.abs(qk_gain[6])) * jnp.max(jnp.abs(qk_gain[7])) + 0.05
    o_d = _causal_attn(sl[S_DQ], sl[S_DK], sl[S_DV], tq, tk, fcum=fcum[..., FOX_LANE:FOX_LANE + N_HEADS],
                       qk_bound=qk_bound)

    wb16 = w_branch.astype(BF16)
    flat = lambda a: a.reshape(n, W_MIX)
    return _merge(x2, norm_g, sc_a, sh_a, g_a, w_merge, flat(misc), _gate_expand(),
                  flat(o_cmp), flat(o_sel), flat(o_win), flat(o_b), flat(o_c), flat(o_d),
                  wb16[0], wb16[1], wb16[2], wb16[3], w_out.astype(BF16), s, min(512, s))


def kernel(x, c, positions, w_ada, b_ada, norm_mix, norm_ffn, w_in, qk_gain, nsa_pe_k, nsa_pe_v,
           nsa_ck_w1, nsa_ck_w2, nsa_cv_w1, nsa_cv_w2, fox_bias, w_branch, w_out,
           ffn_w1, ffn_w3, ffn_w2, router_w, moe_w1, moe_w3, moe_w2):
    b, s, d = x.shape
    depth = w_ada.shape[0]
    rope = _rope_tables(positions)
    mods = _ada(c, w_ada, b_ada).reshape(depth, b, 6, 1, d).transpose(0, 2, 1, 3, 4)
    x2 = x.reshape(b * s, d)
    for l in range(depth):
        mod = mods[l]
        x2 = _mixer_layer(x2, b, s, mod[0:3], norm_mix[l].reshape(1, d), rope, w_in[l], qk_gain[l],
                          nsa_pe_k[l], nsa_pe_v[l], nsa_ck_w1[l], nsa_ck_w2[l], nsa_cv_w1[l],
                          nsa_cv_w2[l], fox_bias[l], w_branch[l], w_out[l])
        sh_f, sc_f, g_f = mod[3], mod[4], mod[5]
        gn = norm_ffn[l].reshape(1, d)
        e = l // 2
        if l % 2 == 0:
            dff = ffn_w1.shape[2]
            x2 = _ffn(x2, gn, sc_f, sh_f, g_f, ffn_w1[e].astype(BF16), ffn_w3[e].astype(BF16),
                      ffn_w2[e].astype(BF16), s, min(1024, s), dff // 2)
        else:
            tt = min(1024, s)
            dff = moe_w1.shape[3]
            h2, rank, gate, cnt = _route(x2, gn, sc_f, sh_f, router_w[e].T, s, tt)
            counts = cnt[:, :, 0].astype(jnp.int32).reshape(-1)
            x2 = _moe(counts, x2, g_f, h2, rank, gate, moe_w1[e].astype(BF16), moe_w3[e].astype(BF16),
                      moe_w2[e].astype(BF16), s, tt, dff // 2, 128)
    return x2.reshape(b, s, d)
```

```python
import functools

import numpy as np
import jax
import jax.numpy as jnp
from jax import lax
from jax.experimental import pallas as pl
from jax.experimental.pallas import tpu as pltpu

F32 = jnp.float32
BF16 = jnp.bfloat16
HI = lax.Precision.HIGHEST

LANES = 128
VMEM_LIMIT = 52 * 1024 * 1024

HEAD_DIM = 64
HEAD_SHIFT = 6
N_HEADS = 4
W_MIX = N_HEADS * HEAD_DIM
N_MIXERS = 4
ROPE_THETA = 500000.0
ROPE_DIMS = HEAD_DIM // 4
ROPE_HALF = ROPE_DIMS // 2
EPS = 1e-6
LOG2E = 1.4426950408889634
NEG = -1e30
BIG = 1e30
CMP_LEN = 32
CMP_STRIDE = 16
SEL_LEN = 64
SEL_SHIFT = 6
TOPN = 16
SEL_NEG = 2.0 ** 60
NSA_WINDOW = 512
DIL_CONFIGS = ((128, 1), (512, 4), (2048, 16))
SB_STOP = -110.0
FOX_STOP = 108.0
GATE_LANE = 0
FOX_LANE = 3 * N_HEADS
VC_LANE = HEAD_DIM

NN = (((1,), (0,)), ((), ()))
NT = (((1,), (1,)), ((), ()))
TN = (((0,), (0,)), ((), ()))


def _dot(a, b, dims=NN, precision=None):
    return lax.dot_general(a, b, dims, precision=precision, preferred_element_type=F32)


def _dot_split(a, b_bf16, dims=NN):
    hi = a.astype(BF16)
    lo = (a - hi.astype(F32)).astype(BF16)
    return _dot(hi, b_bf16, dims) + _dot(lo, b_bf16, dims)


def _params(*sem):
    return pltpu.CompilerParams(dimension_semantics=sem, vmem_limit_bytes=VMEM_LIMIT)


def _mod_norm(x, g, sc, sh):
    ms = jnp.mean(x * x, axis=-1, keepdims=True)
    return (x * lax.rsqrt(ms + EPS) * g) * (1.0 + sc) + sh


def _head_masks():
    lane = lax.broadcasted_iota(jnp.int32, (1, W_MIX), 1)
    return [(lane >> HEAD_SHIFT) == h for h in range(N_HEADS)]


def _stack_masked(q, qm_scr, tq):
    for h, hm in enumerate(_head_masks()):
        qm_scr[h * tq:(h + 1) * tq, :] = jnp.where(hm, q, jnp.zeros_like(q))


def _pick_heads(stacked, tq, scale=None):
    out = None
    for h, hm in enumerate(_head_masks()):
        blk = stacked[h * tq:(h + 1) * tq, :]
        if scale is not None:
            blk = blk * scale[h]
        out = jnp.where(hm, blk, 0.0 if out is None else out)
    return out


def _ada_kernel(c_ref, w_ref, b_ref, o_ref):
    c = c_ref[...]
    ca = c * jax.nn.sigmoid(c)
    o_ref[0] = _dot(ca, w_ref[0], NN, HI) + b_ref[0]


def _ada(c, w_ada, b_ada):
    depth, d, n6 = w_ada.shape
    b = c.shape[0]
    tn = n6 // 4
    return pl.pallas_call(
        _ada_kernel,
        out_shape=jax.ShapeDtypeStruct((depth, b, n6), F32),
        grid=(depth, n6 // tn),
        in_specs=[
            pl.BlockSpec((b, d), lambda l, j: (0, 0)),
            pl.BlockSpec((1, d, tn), lambda l, j: (l, 0, j)),
            pl.BlockSpec((1, 1, tn), lambda l, j: (l, 0, j)),
        ],
        out_specs=pl.BlockSpec((1, b, tn), lambda l, j: (l, 0, j)),
        compiler_params=_params("arbitrary", "arbitrary"),
        name="ada_mod",
    )(c, w_ada, b_ada.reshape(depth, 1, n6))


(W_QA, W_KC, W_KV, W_BQ, W_BK, W_BV, W_CQ, W_CK, W_CV, W_DQ, W_DK, W_DV, W_MISC) = range(13)
N_WSLABS = 13
KV_QK_LANES = 2 * HEAD_DIM
_SLABS = (
    (W_QA, "all", None, F32), (W_QA, "all", "all", BF16), (W_KC, "all", None, F32),
    (W_KV, "kv", "kv", BF16),
    (W_BQ, "all", "all", BF16), (W_BK, "all", "all", BF16), (W_BV, None, None, BF16),
    (W_CQ, None, None, BF16), (W_CK, None, None, BF16), (W_CV, None, None, BF16),
    (W_DQ, "all", None, BF16), (W_DK, "all", None, BF16), (W_DV, None, None, BF16),
    (W_MISC, None, None, F32),
)
N_SLABS = len(_SLABS)
(S_QNR, S_QR, S_KC, S_KV, S_BQ, S_BK, S_BV, S_CQ, S_CK, S_CV, S_DQ, S_DK, S_DV, S_MISC) = range(N_SLABS)
_DIL_SLABS = (S_BQ, S_BK, S_BV)
_DILATIONS = tuple(d for (_, d) in DIL_CONFIGS if d > 1)
KSL_LANE, KW_LANE, VSL_LANE, VW_LANE = (i * HEAD_DIM for i in range(4))


def _proj_kernel(x_ref, g_ref, sc_ref, sh_ref, w_ref, gain_ref, bd_ref, rope_ref, *out_refs):
    h = _mod_norm(x_ref[...], g_ref[...], sc_ref[0], sh_ref[0]).astype(BF16)
    bd = bd_ref[...]
    key_lanes = lax.broadcasted_iota(jnp.int32, (1, W_MIX), 1) < KV_QK_LANES
    last_w, y_n = None, None
    raw = [_dot(h, w_ref[:, wi * W_MIX:(wi + 1) * W_MIX]) for wi in range(N_WSLABS)]
    for s, (wi, norm, rope, _) in enumerate(_SLABS):
        if wi != last_w:
            y_n = raw[wi]
            if norm:
                normed = y_n * lax.rsqrt(_dot_split(y_n * y_n, bd) + EPS)
                y_n = normed if norm == "all" else jnp.where(key_lanes, normed, y_n)
            last_w = wi
        y = y_n * gain_ref[s]
        if rope:
            c, s1, s2 = rope_ref[0, 0], rope_ref[0, 1], rope_ref[0, 2]
            if rope == "kv":
                c, s1, s2 = jnp.where(key_lanes, c, 1.0), jnp.where(key_lanes, s1, 0.0), jnp.where(key_lanes, s2, 0.0)
            y = y * c + pltpu.roll(y, W_MIX - ROPE_HALF, 1) * s1 + pltpu.roll(y, ROPE_HALF, 1) * s2
        out_refs[s][...] = y.astype(out_refs[s].dtype)
        if s in _DIL_SLABS:
            stage_scr = out_refs[-1]
            tt = y.shape[0]
            for c in range(W_MIX // LANES):
                stage_scr[c] = y[:, c * LANES:(c + 1) * LANES]
            for di, dil in enumerate(_DILATIONS):
                o_ref = out_refs[N_SLABS + _DIL_SLABS.index(s) * len(_DILATIONS) + di]
                for r in range(dil):
                    for c in range(W_MIX // LANES):
                        lo = r * W_MIX + c * LANES
                        o_ref[:, lo:lo + LANES] = (
                            stage_scr[c, pl.ds(r, tt // dil, stride=dil), :].astype(o_ref.dtype))


def _proj(x2, g, sc, sh, w_slab, gain, bd, rope, seq, tt):
    n, d = x2.shape
    tpb = seq // tt
    out_shape = [jax.ShapeDtypeStruct((n, W_MIX), dt) for (_, _, _, dt) in _SLABS]
    out_specs = [pl.BlockSpec((tt, W_MIX), lambda i: (i, 0)) for _ in _SLABS]
    for _ in _DIL_SLABS:
        for dil in _DILATIONS:
            out_shape.append(jax.ShapeDtypeStruct((n // dil, dil * W_MIX), BF16))
            out_specs.append(pl.BlockSpec((tt // dil, dil * W_MIX), lambda i: (i, 0)))
    return pl.pallas_call(
        _proj_kernel,
        out_shape=out_shape,
        grid=(n // tt,),
        in_specs=[
            pl.BlockSpec((tt, d), lambda i: (i, 0)),
            pl.BlockSpec((1, d), lambda i: (0, 0)),
            pl.BlockSpec((1, 1, d), lambda i: (i // tpb, 0, 0)),
            pl.BlockSpec((1, 1, d), lambda i: (i // tpb, 0, 0)),
            pl.BlockSpec((d, N_WSLABS * W_MIX), lambda i: (0, 0)),
            pl.BlockSpec((N_SLABS, 1, W_MIX), lambda i: (0, 0, 0)),
            pl.BlockSpec((W_MIX, W_MIX), lambda i: (0, 0)),
            pl.BlockSpec((1, 3, tt, W_MIX), lambda i: (i // tpb, 0, i % tpb, 0)),
        ],
        out_specs=out_specs,
        scratch_shapes=[pltpu.VMEM((W_MIX // LANES, tt, LANES), F32)],
        compiler_params=_params("arbitrary"),
        name="in_proj",
    )(x2, g, sc, sh, w_slab, gain, bd, rope)


def _pack_w_in(w_in, qk_gain):
    d = w_in.shape[0]
    o = 0
    a_q = w_in[:, o:o + W_MIX]; o += W_MIX
    a_kv = w_in[:, o:o + 6 * HEAD_DIM]; o += 6 * HEAD_DIM
    a_g = w_in[:, o:o + 3 * N_HEADS]; o += 3 * N_HEADS
    b_qkv = w_in[:, o:o + 3 * W_MIX]; o += 3 * W_MIX
    c_qkv = w_in[:, o:o + 3 * W_MIX]; o += 3 * W_MIX
    d_qkv = w_in[:, o:o + 3 * W_MIX]; o += 3 * W_MIX
    d_f = w_in[:, o:o + N_HEADS]; o += N_HEADS
    w_merge = w_in[:, o:]
    kc, vc, ksl, vsl, kw, vw = (a_kv[:, i * HEAD_DIM:(i + 1) * HEAD_DIM] for i in range(6))
    zeros = lambda w: jnp.zeros((d, w), w_in.dtype)
    misc = jnp.concatenate([a_g, d_f, zeros(VC_LANE - FOX_LANE - N_HEADS), vc, zeros(W_MIX - 2 * HEAD_DIM)], axis=1)
    slabs = [
        a_q, jnp.concatenate([kc, zeros(W_MIX - HEAD_DIM)], axis=1),
        jnp.concatenate([ksl, kw, vsl, vw], axis=1),
        b_qkv[:, :W_MIX], b_qkv[:, W_MIX:2 * W_MIX], b_qkv[:, 2 * W_MIX:],
        c_qkv[:, :W_MIX], c_qkv[:, W_MIX:2 * W_MIX], c_qkv[:, 2 * W_MIX:],
        d_qkv[:, :W_MIX], d_qkv[:, W_MIX:2 * W_MIX], d_qkv[:, 2 * W_MIX:],
        misc,
    ]
    w_slab = jnp.concatenate(slabs, axis=1).astype(BF16)
    scale = HEAD_DIM ** -0.5
    t4 = lambda gvec: jnp.tile(gvec, N_HEADS)
    one = jnp.ones((W_MIX,), F32)
    scale2 = scale * LOG2E
    gains = [
        t4(qk_gain[0]) * scale, t4(qk_gain[0]) * scale2,
        jnp.concatenate([qk_gain[1], jnp.ones((W_MIX - HEAD_DIM,), F32)]),
        jnp.concatenate([qk_gain[2], qk_gain[3], jnp.ones((W_MIX - KV_QK_LANES,), F32)]),
        t4(qk_gain[4]) * scale2, t4(qk_gain[5]), one,
        one * scale, one, one,
        t4(qk_gain[6]) * scale2, t4(qk_gain[7]), one,
        one,
    ]
    gain = jnp.stack(gains).reshape(N_SLABS, 1, W_MIX).astype(F32)
    return w_slab, gain, w_merge.astype(BF16)


def _rope_tables(positions):
    inv = ROPE_THETA ** (-jnp.arange(0, ROPE_DIMS, 2, dtype=F32) / ROPE_DIMS)
    ang = positions.astype(F32)[..., None] * inv
    cos, sin = jnp.cos(ang), jnp.sin(ang)
    b, s, _ = cos.shape
    pad1 = jnp.ones((b, s, HEAD_DIM - ROPE_DIMS), F32)
    pad0 = jnp.zeros((b, s, HEAD_DIM - ROPE_DIMS), F32)
    z8 = jnp.zeros_like(sin)
    c64 = jnp.concatenate([cos, cos, pad1], axis=-1)
    s1_64 = jnp.concatenate([-sin, z8, pad0], axis=-1)
    s2_64 = jnp.concatenate([z8, sin, pad0], axis=-1)
    heads = lambda t: jnp.tile(t, (1, 1, N_HEADS))
    return jnp.stack([heads(c64), heads(s1_64), heads(s2_64)], axis=1)


def _compress_kernel(a_ref, b_ref, pe_ref, w1_ref, w2_ref, o_ref):
    half = w1_ref.shape[1] // 2
    w1 = w1_ref[0]
    hid = (_dot(a_ref[0, 0], w1[:half], NN, HI) + _dot(b_ref[0, 0], w1[half:], NN, HI)
           + _dot(pe_ref[0], w1, NN, HI))
    o_ref[0, 0] = _dot(jax.nn.gelu(hid), w2_ref[0], NN, HI)


def _compress(ch, chn, pe, w1, w2):
    _, b, ncp, cw = ch.shape
    hid = w1.shape[2]
    return pl.pallas_call(
        _compress_kernel,
        out_shape=jax.ShapeDtypeStruct((2, b, ncp, HEAD_DIM), F32),
        grid=(2, b),
        in_specs=[
            pl.BlockSpec((1, 1, ncp, cw), lambda k, i: (k, i, 0, 0)),
            pl.BlockSpec((1, 1, ncp, cw), lambda k, i: (k, i, 0, 0)),
            pl.BlockSpec((1, 1, 2 * cw), lambda k, i: (k, 0, 0)),
            pl.BlockSpec((1, 2 * cw, hid), lambda k, i: (k, 0, 0)),
            pl.BlockSpec((1, hid, HEAD_DIM), lambda k, i: (k, 0, 0)),
        ],
        out_specs=pl.BlockSpec((1, 1, ncp, HEAD_DIM), lambda k, i: (k, i, 0, 0)),
        compiler_params=_params("arbitrary", "arbitrary"),
        name="nsa_compress",
    )(ch, chn, pe, w1, w2)


def _cmp_kernel(q_ref, kc_ref, vct_ref, ovt_ref, o_ref, sel_ref, qh_scr, ql_scr, acc_scr, *, tq, nc, n_sel):
    i = pl.program_id(1)
    ncp = kc_ref.shape[1]
    q = q_ref[0]
    q_hi = q.astype(BF16)
    _stack_masked(q_hi, qh_scr, tq)
    _stack_masked((q - q_hi.astype(F32)).astype(BF16), ql_scr, tq)
    kc = kc_ref[0]
    k_hi = kc.astype(BF16)
    k_lo = (kc - k_hi.astype(F32)).astype(BF16)
    ss = []
    for h in range(N_HEADS):
        rs = slice(h * tq, (h + 1) * tq)
        ss.append(_dot(k_hi, qh_scr[rs, :], NT) + _dot(k_lo, qh_scr[rs, :], NT)
                  + _dot(k_hi, ql_scr[rs, :], NT))
    t = i * tq + lax.broadcasted_iota(jnp.int32, (1, tq), 1)
    c = lax.broadcasted_iota(jnp.int32, (ncp, 1), 0)
    mask = (c * CMP_STRIDE + (CMP_LEN - 1) <= t) & (c < nc)
    psum = None
    ps = []
    for h in range(N_HEADS):
        sm = jnp.where(mask, ss[h], NEG)
        m = jnp.max(sm, axis=0, keepdims=True)
        e = jnp.where(mask, jnp.exp(sm - m), 0.0)
        l = jnp.sum(e, axis=0, keepdims=True)
        p = e * (1.0 / jnp.maximum(l, 1e-30))
        psum = p if psum is None else psum + p
        ps.append(p.astype(BF16))
    vct = vct_ref[0]
    for h in range(N_HEADS):
        rs = slice(h * HEAD_DIM, (h + 1) * HEAD_DIM)
        acc_scr[rs, :] = _dot(vct[rs, :], ps[h])
    o_ref[0] = acc_scr[...].T.astype(o_ref.dtype)
    imp = _dot(ovt_ref[...], psum, NN, HI)
    j = lax.broadcasted_iota(jnp.int32, (n_sel, 1), 0)
    cur = t >> SEL_SHIFT
    valid = j <= cur
    forced = (j == 0) | (j == cur) | (j == cur - 1)
    picked = valid & forced
    score = jnp.where(valid, jnp.where(forced, -3e38, imp), NEG)
    sel = jnp.where(picked, 1.0, 0.0)
    jf = j.astype(F32)
    for _ in range(min(TOPN, n_sel) - 3):
        mx = jnp.max(score, axis=0, keepdims=True)
        idx = jnp.min(jnp.where(score == mx, jf, float(n_sel)), axis=0, keepdims=True)
        pick = jf == idx
        sel = jnp.where(pick, 1.0, sel)
        score = jnp.where(pick, -3e38, score)
    sel_ref[0] = ((sel - 1.0) * SEL_NEG).T.astype(sel_ref.dtype)


def _cmp_topk(q_nr, kc_rep, vc_rep, overlap, tq, nc):
    b, seq, _ = q_nr.shape
    ncp = kc_rep.shape[1]
    n_sel = seq // SEL_LEN
    return pl.pallas_call(
        functools.partial(_cmp_kernel, tq=tq, nc=nc, n_sel=n_sel),
        out_shape=[jax.ShapeDtypeStruct((b, seq, W_MIX), BF16),
                   jax.ShapeDtypeStruct((b, seq, n_sel), BF16)],
        grid=(b, seq // tq),
        in_specs=[
            pl.BlockSpec((1, tq, W_MIX), lambda g, i: (g, i, 0)),
            pl.BlockSpec((1, ncp, W_MIX), lambda g, i: (g, 0, 0)),
            pl.BlockSpec((1, W_MIX, ncp), lambda g, i: (g, 0, 0)),
            pl.BlockSpec((n_sel, ncp), lambda g, i: (0, 0)),
        ],
        out_specs=[pl.BlockSpec((1, tq, W_MIX), lambda g, i: (g, i, 0)),
                   pl.BlockSpec((1, tq, n_sel), lambda g, i: (g, i, 0))],
        scratch_shapes=[pltpu.VMEM((N_HEADS * tq, W_MIX), BF16), pltpu.VMEM((N_HEADS * tq, W_MIX), BF16),
                        pltpu.VMEM((W_MIX, tq), F32)],
        compiler_params=_params("arbitrary", "arbitrary"),
        name="nsa_cmp_topk",
    )(q_nr, kc_rep, vc_rep, overlap)


def _causal_kernel(*refs, tq, tk, has_sel, has_bias):
    it = iter(refs)
    q_ref, k_ref, vt_ref = next(it), next(it), next(it)
    sel_ref = next(it) if has_sel else None
    fa_ref, fb_ref, fq_ref, thr_ref = (next(it) for _ in range(4)) if has_bias else (None,) * 4
    o_ref, qm_scr, acc_scr, sa_scr, sb_scr, m_scr, l_scr = (next(it) for _ in range(7))
    i = pl.program_id(1)
    t0 = i * tq
    lane = lax.broadcasted_iota(jnp.int32, (1, W_MIX), 1)
    if has_sel:
        n_sel = sel_ref.shape[2]
        qf = q_ref[0].astype(F32)
        selb = sel_ref[0].astype(F32)
        selb = jnp.concatenate([selb, jnp.zeros((tq, W_MIX - n_sel), F32)], axis=1)
        selb = pltpu.roll(selb, HEAD_DIM, 1)
        for h in range(N_HEADS):
            rot = qf if h == 0 else pltpu.roll(qf, W_MIX - h * HEAD_DIM, 1)
            qm_scr[h * tq:(h + 1) * tq, :] = jnp.where(lane < HEAD_DIM, rot, selb).astype(BF16)
    else:
        _stack_masked(q_ref[0], qm_scr, tq)
    acc_scr[...] = jnp.zeros(acc_scr.shape, F32)
    t_pos = t0 + lax.broadcasted_iota(jnp.int32, (1, tq), 1)

    def scores(kt, s_buf):
        k_t = k_ref[0, kt]
        if has_sel:
            blk = (kt * tk + lax.broadcasted_iota(jnp.int32, (tk, 1), 0)) >> SEL_SHIFT
            one_hot = jnp.where(blk == lane - HEAD_DIM, 1.0, 0.0).astype(BF16)
            k_t = jnp.where(lane < HEAD_DIM, k_t, one_hot)
        for h in range(N_HEADS):
            s = _dot(k_t, qm_scr[h * tq:(h + 1) * tq, :], NT)
            if has_bias:
                s = s + _dot(fa_ref[0, kt], fb_ref[0, h])
            s_buf[h] = s

    def update(kt, s_buf, diag):
        vt_t = vt_ref[0, kt]
        mask = None
        if diag:
            s_pos = kt * tk + lax.broadcasted_iota(jnp.int32, (tk, 1), 0)
            mask = s_pos <= t_pos
        ps, alphas = [], []
        for h in range(N_HEADS):
            s = s_buf[h]
            if mask is not None:
                s = jnp.where(mask, s, NEG)
            m_old = m_scr[h]
            m_new = jnp.maximum(m_old, jnp.max(s, axis=0, keepdims=True))
            alpha = jnp.exp2(m_old - m_new)
            m_scr[h] = m_new
            p = jnp.exp2(s - m_new)
            l_scr[h] = alpha * l_scr[h] + jnp.sum(p, axis=0, keepdims=True)
            ps.append(p.astype(BF16))
            alphas.append(alpha)
        for h in range(N_HEADS):
            rs = slice(h * HEAD_DIM, (h + 1) * HEAD_DIM)
            vs = slice(VSL_LANE, VSL_LANE + HEAD_DIM) if has_sel else rs
            acc_scr[rs, :] = alphas[h] * acc_scr[rs, :] + _dot(vt_t[vs, :], ps[h])

    def live(kt_next):
        lane = lax.broadcasted_iota(jnp.int32, (1, thr_ref.shape[3]), 1)
        hit = None
        for h in range(N_HEADS):
            top = jnp.max(fq_ref[0, h] - m_scr[h], axis=-1, keepdims=True)
            need = top >= thr_ref[0, h]
            hit = need if hit is None else (hit | need)
        return jnp.max(jnp.where(hit & (lane == kt_next), 1.0, 0.0)) > 0.5

    n_last = t0 // tk
    m_scr[...] = jnp.full(m_scr.shape, NEG, F32)
    l_scr[...] = jnp.zeros(l_scr.shape, F32)
    scores(n_last, sa_scr)
    scores(jnp.maximum(n_last - 1, 0), sb_scr)
    update(n_last, sa_scr, True)

    def pair(j):
        kt = n_last - 1 - 2 * j
        scores(kt - 1, sa_scr)
        update(kt, sb_scr, False)
        scores(jnp.maximum(kt - 2, 0), sb_scr)
        update(kt - 1, sa_scr, False)
        return kt - 2

    n_pairs = n_last // 2
    if has_bias:
        def cond(state):
            j, go = state
            return (j < n_pairs) & go

        def body(state):
            j, _ = state
            return j + 1, live(pair(j))

        _, go = lax.while_loop(cond, body, (jnp.int32(0), live(n_last - 1)))
    else:
        lax.fori_loop(0, n_pairs, lambda j, c: (pair(j), c)[1], 0)
        go = True

    @pl.when((n_last % 2 == 1) & go)
    def _():
        update(0, sb_scr, False)

    ls = [l_scr[h] for h in range(N_HEADS)]
    for h in range(N_HEADS):
        rs = slice(h * HEAD_DIM, (h + 1) * HEAD_DIM)
        acc_scr[rs, :] = acc_scr[rs, :] / ls[h]
    o_ref[0] = acc_scr[...].T.astype(o_ref.dtype)


def _split3(x):
    def cut(v):
        bits = lax.bitcast_convert_type(v, jnp.uint32) & jnp.uint32(0xFFFF0000)
        return lax.bitcast_convert_type(bits, F32)

    hi = cut(x)
    r1 = x - hi
    mid = cut(r1)
    lo = r1 - mid
    return hi.astype(BF16), mid.astype(BF16), lo.astype(BF16)


def _causal_attn(q, k, v, tq, tk, sel=None, fcum=None, qk_bound=None):
    b, s, _ = q.shape
    nk = s // tk
    rows = N_HEADS * tq
    n_s = N_HEADS
    vt = v.reshape(b, nk, tk, W_MIX).transpose(0, 1, 3, 2)
    args = [q, k.reshape(b, nk, tk, W_MIX), vt]
    in_specs = [
        pl.BlockSpec((1, tq, W_MIX), lambda a, i: (a, i, 0)),
        pl.BlockSpec((1, nk, tk, W_MIX), lambda a, i: (a, 0, 0, 0)),
        pl.BlockSpec((1, nk, W_MIX, tk), lambda a, i: (a, 0, 0, 0)),
    ]
    if sel is not None:
        args.append(sel)
        in_specs.append(pl.BlockSpec((1, tq, sel.shape[2]), lambda a, i: (a, i, 0)))
    if fcum is not None:
        fcum = fcum * LOG2E
        qk_bound = qk_bound * LOG2E
        nf = 2 * 3 + 2
        ones = jnp.ones(fcum.shape, BF16)
        zero = jnp.zeros(fcum.shape, BF16)
        parts = _split3(fcum)
        key_f = jnp.stack([ones, ones, ones] + [-p for p in parts] + [zero, zero], axis=-1)
        qry_f = jnp.stack(list(parts) + [ones, ones, ones, zero, zero], axis=-1)
        fa = key_f.reshape(b, nk, tk, N_HEADS * nf)
        eye = jnp.eye(N_HEADS, dtype=BF16)
        fb = jnp.einsum('bshf,hg->bgshf', qry_f, eye).reshape(b, N_HEADS, s, N_HEADS * nf).transpose(0, 1, 3, 2)
        f_rows = fcum.transpose(0, 2, 1)
        f_end = f_rows[:, :, tk - 1::tk]
        thr = jnp.pad(f_end - (qk_bound + FOX_STOP * LOG2E), ((0, 0), (0, 0), (0, LANES - nk)),
                      constant_values=BIG).reshape(b, N_HEADS, 1, LANES)
        args += [fa, fb, f_rows.reshape(b, N_HEADS, 1, s), thr]
        in_specs += [pl.BlockSpec((1, nk, tk, N_HEADS * nf), lambda a, i: (a, 0, 0, 0)),
                     pl.BlockSpec((1, N_HEADS, N_HEADS * nf, tq), lambda a, i: (a, 0, 0, i)),
                     pl.BlockSpec((1, N_HEADS, 1, tq), lambda a, i: (a, 0, 0, i)),
                     pl.BlockSpec((1, N_HEADS, 1, LANES), lambda a, i: (a, 0, 0, 0))]
    return pl.pallas_call(
        functools.partial(_causal_kernel, tq=tq, tk=tk, has_sel=sel is not None, has_bias=fcum is not None),
        out_shape=jax.ShapeDtypeStruct((b, s, W_MIX), BF16),
        grid=(b, s // tq),
        in_specs=in_specs,
        out_specs=pl.BlockSpec((1, tq, W_MIX), lambda a, i: (a, i, 0)),
        scratch_shapes=[pltpu.VMEM((rows, W_MIX), BF16), pltpu.VMEM((W_MIX, tq), F32),
                        pltpu.VMEM((n_s, tk, tq), F32), pltpu.VMEM((n_s, tk, tq), F32),
                        pltpu.VMEM((N_HEADS, 1, tq), F32), pltpu.VMEM((N_HEADS, 1, tq), F32)],
        compiler_params=_params("arbitrary", "arbitrary"),
        name="causal_sel%d_bias%d" % (sel is not None, fcum is not None),
    )(*args)


def _window_kernel(*refs, tq, wk, pad, window, ls, emit_lse, other_dils, kv_lanes):
    it = iter(refs)
    q_ref, k_ref, v_ref = next(it), next(it), next(it)
    n_other = len(other_dils)
    others_in = [(next(it), next(it)) for _ in range(n_other)]
    wexp_ref = next(it) if n_other else None
    o_ref = next(it)
    lse_ref = next(it) if emit_lse else None
    qm_scr, acc_scr = next(it), next(it)
    others = []
    if n_other:
        og_scr, lg_scr = next(it), next(it)
        for g, (dg, (og_ref, lg_ref)) in enumerate(zip(other_dils, others_in)):
            for r in range(dg):
                rows = pl.ds(r, tq // dg, stride=dg)
                for c in range(W_MIX // LANES):
                    lo = r * W_MIX + c * LANES
                    og_scr[g, c, rows, :] = og_ref[0, :, lo:lo + LANES]
                lg_scr[g, rows, :] = lg_ref[0, :, r * LANES:(r + 1) * LANES]
            others.append((og_scr.at[g], lg_scr.at[g]))
    t0 = pl.program_id(2) * tq
    start = pl.multiple_of(jnp.clip(t0 - pad, 0, ls - wk), LANES)
    if kv_lanes is None:
        _stack_masked(q_ref[0], qm_scr, tq)
    else:
        lane = lax.broadcasted_iota(jnp.int32, (1, W_MIX), 1)
        on_key = (lane >= kv_lanes[0]) & (lane < kv_lanes[0] + HEAD_DIM)
        qf = q_ref[0].astype(F32)
        for h in range(N_HEADS):
            shift = (kv_lanes[0] - h * HEAD_DIM) % W_MIX
            rot = pltpu.roll(qf, shift, 1) if shift else qf
            qm_scr[h * tq:(h + 1) * tq, :] = jnp.where(on_key, rot, 0.0).astype(BF16)
    k_w = k_ref[0, pl.ds(start, wk), :]
    vt_w = v_ref[0, pl.ds(start, wk), :].astype(F32).T.astype(BF16)
    ss = [_dot(k_w, qm_scr[h * tq:(h + 1) * tq, :], NT) for h in range(N_HEADS)]
    t_pos = t0 + lax.broadcasted_iota(jnp.int32, (1, tq), 1)
    s_pos = start + lax.broadcasted_iota(jnp.int32, (wk, 1), 0)
    mask = (s_pos <= t_pos) & (t_pos - s_pos < window)
    ps, inv_ls, lses = [], [], []
    for h in range(N_HEADS):
        s = jnp.where(mask, ss[h], NEG)
        m = jnp.max(s, axis=0, keepdims=True)
        p = jnp.exp2(s - m)
        l = jnp.sum(p, axis=0, keepdims=True)
        ps.append(p.astype(BF16))
        inv_ls.append(1.0 / l)
        lses.append(m + jnp.log2(l))
    for h in range(N_HEADS):
        rs = slice(h * HEAD_DIM, (h + 1) * HEAD_DIM)
        vs = rs if kv_lanes is None else slice(kv_lanes[1], kv_lanes[1] + HEAD_DIM)
        acc_scr[rs, :] = _dot(vt_w[vs, :], ps[h]) * inv_ls[h]
    o_self = acc_scr[...].T
    if emit_lse or n_other:
        row = lax.broadcasted_iota(jnp.int32, (LANES, 1), 0)
        stat = jnp.zeros((LANES, tq), F32)
        for h in range(N_HEADS):
            stat = jnp.where(row == h, lses[h], stat)
        lse_tile = stat.T
    if n_other:
        lg_t = [lg[...].T for (_, lg) in others]
        row = lax.broadcasted_iota(jnp.int32, (LANES, 1), 0)
        wmat = jnp.zeros((LANES, tq), F32)
        for h in range(N_HEADS):
            group_lse = [lses[h]] + [t[h:h + 1, :] for t in lg_t]
            top = functools.reduce(jnp.maximum, group_lse)
            ws = [jnp.exp2(x - top) for x in group_lse]
            inv_den = 1.0 / functools.reduce(jnp.add, ws)
            for g, w in enumerate(ws):
                wmat = jnp.where(row == g * N_HEADS + h, w * inv_den, wmat)
        wt = wmat.T
        groups = [o_self] + [jnp.concatenate([og[c] for c in range(W_MIX // LANES)], axis=1)
                             for (og, _) in others]
        out = None
        for g, o_g in enumerate(groups):
            term = _dot_split(wt, wexp_ref[g]) * o_g
            out = term if out is None else out + term
        o_ref[0] = out.astype(o_ref.dtype)
    else:
        o_ref[0] = o_self.astype(o_ref.dtype)
    if emit_lse:
        lse_ref[0] = lse_tile


def _window_attn(q, k, v, *, dil, window, tq, out_dtype, emit_lse=False, others=(), kv_lanes=None):
    b, ls, _ = q.shape
    tq = min(tq, ls)
    pad = -(-(window - 1) // LANES) * LANES
    wk = min(tq + pad, ls)
    rows = N_HEADS * tq
    args = [q, k, v]
    in_specs = [
        pl.BlockSpec((1, tq, W_MIX), lambda a, r, i: (a, i, r)),
        pl.BlockSpec((1, ls, W_MIX), lambda a, r, i: (a, 0, r)),
        pl.BlockSpec((1, ls, W_MIX), lambda a, r, i: (a, 0, r)),
    ]
    scratch = [pltpu.VMEM((rows, W_MIX), BF16), pltpu.VMEM((W_MIX, tq), F32)]
    for (o_g, lse_g, dg) in others:
        args += [o_g, lse_g]
        in_specs += [pl.BlockSpec((1, tq // dg, dg * W_MIX), lambda a, r, i: (a, i, 0)),
                     pl.BlockSpec((1, tq // dg, dg * LANES), lambda a, r, i: (a, i, 0))]
    if others:
        wexp = np.zeros((len(others) + 1, LANES, W_MIX), np.float32)
        for g in range(len(others) + 1):
            for h in range(N_HEADS):
                wexp[g, g * N_HEADS + h, h * HEAD_DIM:(h + 1) * HEAD_DIM] = 1.0
        args.append(jnp.asarray(wexp, BF16))
        in_specs.append(pl.BlockSpec(wexp.shape, lambda a, r, i: (0, 0, 0)))
        scratch += [pltpu.VMEM((len(others), W_MIX // LANES, tq, LANES), F32),
                    pltpu.VMEM((len(others), tq, LANES), F32)]
    out_shape = [jax.ShapeDtypeStruct((b, ls, dil * W_MIX), out_dtype)]
    out_specs = [pl.BlockSpec((1, tq, W_MIX), lambda a, r, i: (a, i, r))]
    if emit_lse:
        out_shape.append(jax.ShapeDtypeStruct((b, ls, dil * LANES), F32))
        out_specs.append(pl.BlockSpec((1, tq, LANES), lambda a, r, i: (a, i, r)))
    res = pl.pallas_call(
        functools.partial(_window_kernel, tq=tq, wk=wk, pad=pad, window=window, ls=ls,
                          emit_lse=emit_lse, other_dils=tuple(dg for (_, _, dg) in others),
                          kv_lanes=kv_lanes),
        out_shape=out_shape,
        grid=(b, dil, ls // tq),
        in_specs=in_specs,
        out_specs=out_specs,
        scratch_shapes=scratch,
        compiler_params=_params("arbitrary", "arbitrary", "arbitrary"),
        name="window_d%d_w%d" % (dil, window),
    )(*args)
    return (res[0], res[1], dil) if emit_lse else res[0]


def _sb_kernel(q_ref, k_ref, vt_ref, tri_ref, o_ref, qm_scr, carry_scr, acc_scr, *, tq):
    i = pl.program_id(1)
    t0 = i * tq
    _stack_masked(q_ref[0], qm_scr, tq)
    carry_scr[...] = jnp.zeros(carry_scr.shape, F32)
    acc_scr[...] = jnp.zeros(acc_scr.shape, F32)
    t_pos = t0 + lax.broadcasted_iota(jnp.int32, (1, tq), 1)

    def tile(kt, diag):
        k_t = k_ref[0, kt]
        vt_t = vt_ref[0, kt]
        tri = tri_ref[...]
        zs = [_dot(k_t, qm_scr[h * tq:(h + 1) * tq, :], NT) for h in range(N_HEADS)]
        if diag:
            s_pos = kt * tq + lax.broadcasted_iota(jnp.int32, (tq, 1), 0)
            strict = s_pos < t_pos
        weights = []
        for h in range(N_HEADS):
            z = zs[h]
            lg = -(jnp.maximum(z, 0.0) + jnp.log(1.0 + jnp.exp(-jnp.abs(z))))
            if diag:
                lg = jnp.where(strict, lg, 0.0)
            hi = lg.astype(BF16)
            lo = (lg - hi.astype(F32)).astype(BF16)
            cum = _dot(tri, hi) + _dot(tri, lo) + carry_scr[h]
            a = jnp.exp(z + cum)
            if diag:
                a = jnp.where(strict, a, 0.0)
            weights.append(a.astype(BF16))
            carry_scr[h] += jnp.sum(lg, axis=0, keepdims=True)
        for h in range(N_HEADS):
            rs = slice(h * HEAD_DIM, (h + 1) * HEAD_DIM)
            acc_scr[rs, :] += _dot(vt_t[rs, :], weights[h])

    tile(i, True)

    def cond(state):
        j, top = state
        return (j < i) & (top > SB_STOP)

    def body(state):
        j, _ = state
        tile(i - 1 - j, False)
        return j + 1, jnp.max(carry_scr[...])

    lax.while_loop(cond, body, (jnp.int32(0), jnp.max(carry_scr[...])))
    o_ref[0] = acc_scr[...].T.astype(o_ref.dtype)


def _stick_breaking(q, k, v, tq):
    b, s, _ = q.shape
    nk = s // tq
    rows = N_HEADS * tq
    tri = jnp.asarray(np.triu(np.ones((tq, tq), np.float32)), BF16)
    vt = v.reshape(b, nk, tq, W_MIX).transpose(0, 1, 3, 2)
    return pl.pallas_call(
        functools.partial(_sb_kernel, tq=tq),
        out_shape=jax.ShapeDtypeStruct((b, s, W_MIX), BF16),
        grid=(b, nk),
        in_specs=[
            pl.BlockSpec((1, tq, W_MIX), lambda a, i: (a, i, 0)),
            pl.BlockSpec((1, nk, tq, W_MIX), lambda a, i: (a, 0, 0, 0)),
            pl.BlockSpec((1, nk, W_MIX, tq), lambda a, i: (a, 0, 0, 0)),
            pl.BlockSpec((tq, tq), lambda a, i: (0, 0)),
        ],
        out_specs=pl.BlockSpec((1, tq, W_MIX), lambda a, i: (a, i, 0)),
        scratch_shapes=[pltpu.VMEM((rows, W_MIX), BF16), pltpu.VMEM((N_HEADS, 1, tq), F32),
                        pltpu.VMEM((W_MIX, tq), F32)],
        compiler_params=_params("arbitrary", "arbitrary"),
        name="stick_breaking",
    )(q, k.reshape(b, nk, tq, W_MIX), vt, tri)


def _foxcum_kernel(x_ref, b_ref, tri_ref, o_ref, carry_scr):
    @pl.when(pl.program_id(1) == 0)
    def _():
        carry_scr[...] = jnp.zeros(carry_scr.shape, F32)

    z = x_ref[0] + b_ref[...]
    logf = jnp.minimum(z, 0.0) - jnp.log(1.0 + jnp.exp(-jnp.abs(z)))
    tri = tri_ref[...]
    hi = logf.astype(BF16)
    r1 = logf - hi.astype(F32)
    mid = r1.astype(BF16)
    lo = (r1 - mid.astype(F32)).astype(BF16)
    cum = _dot(tri, hi) + _dot(tri, mid) + _dot(tri, lo) + carry_scr[...]
    o_ref[0] = cum
    carry_scr[...] = cum[cum.shape[0] - 1:, :]


def _fox_cumsum(misc, bias_vec, tc):
    b, s, w = misc.shape
    tri = jnp.asarray(np.tril(np.ones((tc, tc), np.float32)), BF16)
    return pl.pallas_call(
        _foxcum_kernel,
        out_shape=jax.ShapeDtypeStruct((b, s, w), F32),
        grid=(b, s // tc),
        in_specs=[
            pl.BlockSpec((1, tc, w), lambda a, i: (a, i, 0)),
            pl.BlockSpec((1, w), lambda a, i: (0, 0)),
            pl.BlockSpec((tc, tc), lambda a, i: (0, 0)),
        ],
        out_specs=pl.BlockSpec((1, tc, w), lambda a, i: (a, i, 0)),
        scratch_shapes=[pltpu.VMEM((1, w), F32)],
        compiler_params=_params("arbitrary", "arbitrary"),
        name="fox_cumsum",
    )(misc, bias_vec, tri)


def _merge_kernel(x_ref, g_ref, sc_ref, sh_ref, ga_ref, wm_ref, misc_ref, pg_ref,
                  ocmp_ref, osel_ref, owin_ref, ob_ref, oc_ref, od_ref,
                  wa_ref, wb_ref, wc_ref, wd_ref, wo_ref, o_ref):
    x = x_ref[...]
    d = x.shape[1]
    h = _mod_norm(x, g_ref[...], sc_ref[0], sh_ref[0]).astype(BF16)
    gate = jax.nn.sigmoid(misc_ref[...])
    o_a = (_dot_split(gate, pg_ref[0]) * ocmp_ref[...].astype(F32)
           + _dot_split(gate, pg_ref[1]) * osel_ref[...].astype(F32)
           + _dot_split(gate, pg_ref[2]) * owin_ref[...].astype(F32)).astype(BF16)
    mixed = jnp.zeros(x.shape, F32)
    for m, (o_m, w_ref) in enumerate(((o_a, wa_ref), (ob_ref[...], wb_ref),
                                      (oc_ref[...], wc_ref), (od_ref[...], wd_ref))):
        y = _dot(o_m, w_ref[...])
        gl = _dot(h, wm_ref[:, m * d:(m + 1) * d])
        mixed = mixed + jax.nn.sigmoid(gl) * y
    o_ref[...] = x + ga_ref[0] * _dot(mixed.astype(BF16), wo_ref[...])


def _merge(x2, g, sc, sh, ga, w_merge, misc, pg, o_cmp, o_sel, o_win, o_b, o_c, o_d,
           wa, wb, wc, wd, wo, seq, tt):
    n, d = x2.shape
    tpb = seq // tt
    row = lambda w: pl.BlockSpec((tt, w), lambda i: (i, 0))
    full = lambda a: pl.BlockSpec(a.shape, lambda i: (0,) * a.ndim)
    per_b = pl.BlockSpec((1, 1, d), lambda i: (i // tpb, 0, 0))
    return pl.pallas_call(
        _merge_kernel,
        out_shape=jax.ShapeDtypeStruct((n, d), F32),
        grid=(n // tt,),
        in_specs=[row(d), full(g), per_b, per_b, per_b, full(w_merge), row(W_MIX), full(pg)]
        + [row(W_MIX)] * 6 + [full(wa), full(wb), full(wc), full(wd), full(wo)],
        out_specs=row(d),
        compiler_params=_params("arbitrary"),
        name="merge_out",
    )(x2, g, sc, sh, ga, w_merge, misc, pg, o_cmp, o_sel, o_win, o_b, o_c, o_d, wa, wb, wc, wd, wo)


def _ffn_kernel(x_ref, g_ref, sc_ref, sh_ref, gf_ref, w1_ref, w3_ref, w2_ref, o_ref, h_scr, acc_scr):
    f = pl.program_id(1)

    @pl.when(f == 0)
    def _():
        h_scr[...] = _mod_norm(x_ref[...], g_ref[...], sc_ref[0], sh_ref[0]).astype(BF16)
        acc_scr[...] = jnp.zeros(acc_scr.shape, F32)

    h = h_scr[...]
    a = _dot(h, w1_ref[...])
    b = _dot(h, w3_ref[...])
    acc_scr[...] += _dot((a * jax.nn.sigmoid(a) * b).astype(BF16), w2_ref[...])

    @pl.when(f == pl.num_programs(1) - 1)
    def _():
        o_ref[...] = x_ref[...] + gf_ref[0] * acc_scr[...]


def _ffn(x2, g, sc, sh, gf, w1, w3, w2, seq, tt, tf):
    n, d = x2.shape
    dff = w1.shape[1]
    tpb = seq // tt
    per_b = pl.BlockSpec((1, 1, d), lambda i, f: (i // tpb, 0, 0))
    return pl.pallas_call(
        _ffn_kernel,
        out_shape=jax.ShapeDtypeStruct((n, d), F32),
        grid=(n // tt, dff // tf),
        in_specs=[
            pl.BlockSpec((tt, d), lambda i, f: (i, 0)),
            pl.BlockSpec((1, d), lambda i, f: (0, 0)),
            per_b, per_b, per_b,
            pl.BlockSpec((d, tf), lambda i, f: (0, f)),
            pl.BlockSpec((d, tf), lambda i, f: (0, f)),
            pl.BlockSpec((tf, d), lambda i, f: (f, 0)),
        ],
        out_specs=pl.BlockSpec((tt, d), lambda i, f: (i, 0)),
        scratch_shapes=[pltpu.VMEM((tt, d), BF16), pltpu.VMEM((tt, d), F32)],
        compiler_params=_params("arbitrary", "arbitrary"),
        name="ffn_swiglu",
    )(x2, g, sc, sh, gf, w1, w3, w2)


def _route_kernel(x_ref, g_ref, sc_ref, sh_ref, rw_ref, up_ref, h_ref, rank_ref, gate_ref, cnt_ref):
    hf = _mod_norm(x_ref[...], g_ref[...], sc_ref[0], sh_ref[0])
    h_ref[...] = hf.astype(BF16)
    logits = _dot(rw_ref[...], hf, NT, HI)
    ne, tt = logits.shape
    e_idx = lax.broadcasted_iota(jnp.int32, (ne, 1), 0).astype(F32)
    v1 = jnp.max(logits, axis=0, keepdims=True)
    i1 = jnp.min(jnp.where(logits == v1, e_idx, float(ne)), axis=0, keepdims=True)
    m1 = e_idx == i1
    rest = jnp.where(m1, -3e38, logits)
    v2 = jnp.max(rest, axis=0, keepdims=True)
    i2 = jnp.min(jnp.where(rest == v2, e_idx, float(ne)), axis=0, keepdims=True)
    m2 = e_idx == i2
    e2 = jnp.exp(v2 - v1)
    g1 = 1.0 / (1.0 + e2)
    g2 = e2 / (1.0 + e2)
    routed = m1 | m2
    rf = jnp.where(routed, 1.0, 0.0)
    rank = _dot(rf.astype(BF16), up_ref[...])
    rank = jnp.where(routed, rank, -1.0)
    gate = jnp.where(m1, g1, 0.0) + jnp.where(m2, g2, 0.0)
    for e in range(ne):
        rank_ref[0, e] = rank[e:e + 1, :]
        gate_ref[0, e] = gate[e:e + 1, :]
    cnt = jnp.sum(rf, axis=1, keepdims=True)
    cnt_ref[0] = jnp.broadcast_to(cnt, (ne, LANES))


def _route(x2, g, sc, sh, rw_t, seq, tt):
    n, d = x2.shape
    ne = rw_t.shape[0]
    tpb = seq // tt
    nt = n // tt
    upper = jnp.asarray(np.triu(np.ones((tt, tt), np.float32), 1), BF16)
    per_b = pl.BlockSpec((1, 1, d), lambda i: (i // tpb, 0, 0))
    return pl.pallas_call(
        _route_kernel,
        out_shape=[jax.ShapeDtypeStruct((n, d), BF16),
                   jax.ShapeDtypeStruct((nt, ne, 1, tt), F32),
                   jax.ShapeDtypeStruct((nt, ne, 1, tt), F32),
                   jax.ShapeDtypeStruct((nt, ne, LANES), F32)],
        grid=(nt,),
        in_specs=[
            pl.BlockSpec((tt, d), lambda i: (i, 0)),
            pl.BlockSpec((1, d), lambda i: (0, 0)),
            per_b, per_b,
            pl.BlockSpec((ne, d), lambda i: (0, 0)),
            pl.BlockSpec((tt, tt), lambda i: (0, 0)),
        ],
        out_specs=[pl.BlockSpec((tt, d), lambda i: (i, 0)),
                   pl.BlockSpec((1, ne, 1, tt), lambda i: (i, 0, 0, 0)),
                   pl.BlockSpec((1, ne, 1, tt), lambda i: (i, 0, 0, 0)),
                   pl.BlockSpec((1, ne, LANES), lambda i: (i, 0, 0))],
        compiler_params=_params("arbitrary"),
        name="moe_route",
    )(x2, g, sc, sh, rw_t, upper)


def _moe_kernel(cnt_ref, x_ref, gf_ref, h_ref, rank_ref, gate_ref, w1_ref, w3_ref, w2_ref,
                o_ref, acc_scr, xs_scr, y_scr, *, chunk):
    i, e, f = pl.program_id(0), pl.program_id(1), pl.program_id(2)
    ne, nf = pl.num_programs(1), pl.num_programs(2)

    @pl.when((e == 0) & (f == 0))
    def _():
        acc_scr[...] = jnp.zeros(acc_scr.shape, F32)

    count = cnt_ref[i * ne + e]
    rank = rank_ref[0, 0]
    gate = gate_ref[0, 0]
    n_small = (count + chunk - 1) // chunk
    n_big = (count + 2 * chunk - 1) // (2 * chunk)

    def one_hot(c, rows):
        r = c * rows + lax.broadcasted_iota(jnp.int32, (rows, 1), 0)
        return rank == r.astype(F32)

    def rows_of(c, rows):
        return pl.ds(pl.multiple_of(c * rows, rows), rows)

    @pl.when(f == 0)
    def _():
        h = h_ref[...]

        def gather(c, carry):
            p = jnp.where(one_hot(c, chunk), 1.0, 0.0).astype(BF16)
            xs_scr[rows_of(c, chunk), :] = _dot(p, h).astype(BF16)
            return carry

        lax.fori_loop(0, n_small, gather, 0)

        def clear(c, carry):
            y_scr[rows_of(c, 2 * chunk), :] = jnp.zeros((2 * chunk, y_scr.shape[1]), F32)
            return carry

        lax.fori_loop(0, n_big, clear, 0)

    def expert(c, carry):
        xs = xs_scr[rows_of(c, chunk), :]
        a = _dot(xs, w1_ref[0])
        b = _dot(xs, w3_ref[0])
        y_scr[rows_of(c, chunk), :] += _dot((a * jax.nn.sigmoid(a) * b).astype(BF16), w2_ref[0])
        return carry

    lax.fori_loop(0, n_small, expert, 0)

    @pl.when(f == nf - 1)
    def _():
        def scatter(c, carry):
            hit = one_hot(c, 2 * chunk)
            p = jnp.where(hit, 1.0, 0.0).astype(BF16)
            gcol = jnp.sum(jnp.where(hit, gate, 0.0), axis=-1, keepdims=True)
            acc_scr[...] += _dot(p, (y_scr[rows_of(c, 2 * chunk), :] * gcol).astype(BF16), TN)
            return carry

        lax.fori_loop(0, n_big, scatter, 0)

    @pl.when((e == ne - 1) & (f == nf - 1))
    def _():
        o_ref[...] = x_ref[...] + gf_ref[0] * acc_scr[...]


def _moe(counts, x2, gf, h2, rank, gate, w1, w3, w2, seq, tt, tf, chunk):
    n, d = x2.shape
    ne, _, dff = w1.shape
    tpb = seq // tt
    grid_spec = pltpu.PrefetchScalarGridSpec(
        num_scalar_prefetch=1,
        grid=(n // tt, ne, dff // tf),
        in_specs=[
            pl.BlockSpec((tt, d), lambda i, e, f, c: (i, 0)),
            pl.BlockSpec((1, 1, d), lambda i, e, f, c: (i // tpb, 0, 0)),
            pl.BlockSpec((tt, d), lambda i, e, f, c: (i, 0)),
            pl.BlockSpec((1, 1, 1, tt), lambda i, e, f, c: (i, e, 0, 0)),
            pl.BlockSpec((1, 1, 1, tt), lambda i, e, f, c: (i, e, 0, 0)),
            pl.BlockSpec((1, d, tf), lambda i, e, f, c: (e, 0, f)),
            pl.BlockSpec((1, d, tf), lambda i, e, f, c: (e, 0, f)),
            pl.BlockSpec((1, tf, d), lambda i, e, f, c: (e, f, 0)),
        ],
        out_specs=pl.BlockSpec((tt, d), lambda i, e, f, c: (i, 0)),
        scratch_shapes=[pltpu.VMEM((tt, d), F32), pltpu.VMEM((tt, d), BF16), pltpu.VMEM((tt, d), F32)],
    )
    return pl.pallas_call(
        functools.partial(_moe_kernel, chunk=chunk),
        out_shape=jax.ShapeDtypeStruct((n, d), F32),
        grid_spec=grid_spec,
        compiler_params=_params("arbitrary", "arbitrary", "arbitrary"),
        name="moe_experts",
    )(counts, x2, gf, h2, rank, gate, w1, w3, w2)


def _overlap_matrix(ncp, nc, n_sel):
    c0 = np.arange(ncp) * CMP_STRIDE
    c1 = c0 + CMP_LEN
    s0 = np.arange(n_sel) * SEL_LEN
    s1 = s0 + SEL_LEN
    ov = ((c0[:, None] < s1[None, :]) & (c1[:, None] > s0[None, :])).astype(np.float32)
    ov[nc:] = 0.0
    return jnp.asarray(ov.T)


def _gate_expand():
    pg = np.zeros((3, W_MIX, W_MIX), np.float32)
    for br in range(3):
        for h in range(N_HEADS):
            pg[br, GATE_LANE + 3 * h + br, h * HEAD_DIM:(h + 1) * HEAD_DIM] = 1.0
    return jnp.asarray(pg, BF16)


def _mixer_layer(x2, b, s, mod, norm_g, rope, w_in, qk_gain, pe_k, pe_v, ck1, ck2, cv1, cv2,
                 fox_b, w_branch, w_out):
    n, d = x2.shape
    sh_a, sc_a, g_a = mod[0], mod[1], mod[2]
    w_slab, gain, w_merge = _pack_w_in(w_in, qk_gain)
    bd = jnp.asarray(np.kron(np.eye(N_HEADS), np.full((HEAD_DIM, HEAD_DIM), 1.0 / HEAD_DIM)), BF16)
    outs = _proj(x2, norm_g, sc_a, sh_a, w_slab, gain, bd, rope, s, min(512, s))
    sl = [a.reshape(b, s, W_MIX) for a in outs[:N_SLABS]]
    dil_in = {1: (sl[S_BQ], sl[S_BK], sl[S_BV])}
    for di, dil in enumerate(_DILATIONS):
        dil_in[dil] = tuple(outs[N_SLABS + si * len(_DILATIONS) + di].reshape(b, s // dil, dil * W_MIX)
                            for si in range(len(_DIL_SLABS)))
    misc = sl[S_MISC]

    nch = s // CMP_STRIDE
    nc = nch - CMP_LEN // CMP_STRIDE + 1
    kc_raw = sl[S_KC][..., :HEAD_DIM]
    vc_raw = misc[..., VC_LANE:VC_LANE + HEAD_DIM]
    chunks = jnp.stack([kc_raw, vc_raw]).reshape(2, b, nch, CMP_STRIDE * HEAD_DIM)
    chunks_next = jnp.concatenate([chunks[:, :, 1:], jnp.zeros_like(chunks[:, :, :1])], axis=2)
    pe = jnp.stack([pe_k, pe_v]).reshape(2, 1, CMP_LEN * HEAD_DIM)
    kvc = jnp.tile(_compress(chunks, chunks_next, pe, jnp.stack([ck1, cv1]), jnp.stack([ck2, cv2])),
                   (1, 1, 1, N_HEADS))
    overlap = _overlap_matrix(nch, nc, s // SEL_LEN)
    o_cmp, selmask = _cmp_topk(sl[S_QNR], kvc[0], kvc[1].transpose(0, 2, 1).astype(BF16), overlap,
                               min(512, s), nc)
    tq, tk = min(512, s), min(512, s)
    o_sel = _causal_attn(sl[S_QR], sl[S_KV], sl[S_KV], tq, tk, sel=selmask)
    o_win = _window_attn(sl[S_QR], sl[S_KV], sl[S_KV], dil=1, window=NSA_WINDOW, tq=256, out_dtype=BF16,
                         kv_lanes=(KW_LANE, VW_LANE))

    others = []
    for (wdw, dil) in DIL_CONFIGS[:0:-1]:
        others.append(_window_attn(*dil_in[dil], dil=dil, window=wdw // dil + 1,
                                   tq=512 if s // dil <= 512 else 256,
                                   out_dtype=F32, emit_lse=True))
    wdw, dil = DIL_CONFIGS[0]
    o_b = _window_attn(*dil_in[dil], dil=dil, window=wdw // dil + 1, tq=256, out_dtype=BF16, others=others)

    o_c = _stick_breaking(sl[S_CQ], sl[S_CK], sl[S_CV], min(256, s))

    bias_vec = jnp.zeros((1, W_MIX), F32).at[0, FOX_LANE:FOX_LANE + N_HEADS].set(fox_b)
    fcum = _fox_cumsum(misc, bias_vec, min(512, s))
    qk_bound = 1.02 * HEAD_DIM ** 0.5 * jnp.max(jnp.abs(qk_gain[6])) * jnp.max(jnp.abs(qk_gain[7])) + 0.05
    o_d = _causal_attn(sl[S_DQ], sl[S_DK], sl[S_DV], tq, tk, fcum=fcum[..., FOX_LANE:FOX_LANE + N_HEADS],
                       qk_bound=qk_bound)

    wb16 = w_branch.astype(BF16)
    flat = lambda a: a.reshape(n, W_MIX)
    return _merge(x2, norm_g, sc_a, sh_a, g_a, w_merge, flat(misc), _gate_expand(),
                  flat(o_cmp), flat(o_sel), flat(o_win), flat(o_b), flat(o_c), flat(o_d),
                  wb16[0], wb16[1], wb16[2], wb16[3], w_out.astype(BF16), s, min(512, s))


def kernel(x, c, positions, w_ada, b_ada, norm_mix, norm_ffn, w_in, qk_gain, nsa_pe_k, nsa_pe_v,
           nsa_ck_w1, nsa_ck_w2, nsa_cv_w1, nsa_cv_w2, fox_bias, w_branch, w_out,
           ffn_w1, ffn_w3, ffn_w2, router_w, moe_w1, moe_w3, moe_w2):
    b, s, d = x.shape
    depth = w_ada.shape[0]
    rope = _rope_tables(positions)
    mods = _ada(c, w_ada, b_ada).reshape(depth, b, 6, 1, d).transpose(0, 2, 1, 3, 4)
    x2 = x.reshape(b * s, d)
    for l in range(depth):
        mod = mods[l]
        x2 = _mixer_layer(x2, b, s, mod[0:3], norm_mix[l].reshape(1, d), rope, w_in[l], qk_gain[l],
                          nsa_pe_k[l], nsa_pe_v[l], nsa_ck_w1[l], nsa_ck_w2[l], nsa_cv_w1[l],
                          nsa_cv_w2[l], fox_bias[l], w_branch[l], w_out[l])
        sh_f, sc_f, g_f = mod[3], mod[4], mod[5]
        gn = norm_ffn[l].reshape(1, d)
        e = l // 2
        if l % 2 == 0:
            dff = ffn_w1.shape[2]
            x2 = _ffn(x2, gn, sc_f, sh_f, g_f, ffn_w1[e].astype(BF16), ffn_w3[e].astype(BF16),
                      ffn_w2[e].astype(BF16), s, min(1024, s), dff // 2)
        else:
            tt = min(1024, s)
            dff = moe_w1.shape[3]
            h2, rank, gate, cnt = _route(x2, gn, sc_f, sh_f, router_w[e].T, s, tt)
            counts = cnt[:, :, 0].astype(jnp.int32).reshape(-1)
            x2 = _moe(counts, x2, g_f, h2, rank, gate, moe_w1[e].astype(BF16), moe_w3[e].astype(BF16),
                      moe_w2[e].astype(BF16), s, tt, dff // 2, 128)
    return x2.reshape(b, s, d)
```

```python
import functools
from typing import NamedTuple

import numpy as np
import jax
import jax.numpy as jnp
from jax import lax
from jax.experimental import pallas as pl
from jax.experimental.pallas import tpu as pltpu

F32 = jnp.float32
BF16 = jnp.bfloat16
HI = lax.Precision.HIGHEST

LANES = 128
VMEM_LIMIT = 52 * 1024 * 1024

HEAD_DIM = 64
HEAD_SHIFT = 6
N_HEADS = 4
W_MIX = N_HEADS * HEAD_DIM
N_MIXERS = 4
ROPE_THETA = 500000.0
ROPE_DIMS = HEAD_DIM // 4
ROPE_HALF = ROPE_DIMS // 2
EPS = 1e-6
LOG2E = 1.4426950408889634
NEG = -1e30
BIG = 1e30
CMP_LEN = 32
CMP_STRIDE = 16
SEL_LEN = 64
SEL_SHIFT = 6
TOPN = 16
SEL_NEG = 2.0 ** 60
NSA_WINDOW = 512
DIL_CONFIGS = ((128, 1), (512, 4), (2048, 16))
SB_STOP = -110.0
FOX_STOP = 108.0
GATE_LANE = 0
FOX_LANE = 3 * N_HEADS
VC_LANE = HEAD_DIM

class _Tiles(NamedTuple):
    proj: int
    cum: int
    ffn: int
    moe_chunk: int
    causal_q: int
    causal_k: int
    window_q: int
    short_seq: int
    sb: int


def _tiles(s):
    return _Tiles(proj=min(512, s), cum=min(512, s), ffn=min(1024, s), moe_chunk=128,
                  causal_q=min(512, s), causal_k=min(512, s), window_q=256, short_seq=512,
                  sb=min(256, s))


NN = (((1,), (0,)), ((), ()))
NT = (((1,), (1,)), ((), ()))
TN = (((0,), (0,)), ((), ()))


def _dot(a, b, dims=NN, precision=None):
    return lax.dot_general(a, b, dims, precision=precision, preferred_element_type=F32)


def _dot_split(a, b_bf16, dims=NN):
    hi = a.astype(BF16)
    lo = (a - hi.astype(F32)).astype(BF16)
    return _dot(hi, b_bf16, dims) + _dot(lo, b_bf16, dims)


def _params(*sem):
    return pltpu.CompilerParams(dimension_semantics=sem, vmem_limit_bytes=VMEM_LIMIT)


def _mod_norm(x, g, sc, sh):
    ms = jnp.mean(x * x, axis=-1, keepdims=True)
    return (x * lax.rsqrt(ms + EPS) * g) * (1.0 + sc) + sh


def _head_masks():
    lane = lax.broadcasted_iota(jnp.int32, (1, W_MIX), 1)
    return [(lane >> HEAD_SHIFT) == h for h in range(N_HEADS)]


def _stack_masked(q, qm_scr, tq):
    for h, hm in enumerate(_head_masks()):
        qm_scr[h * tq:(h + 1) * tq, :] = jnp.where(hm, q, jnp.zeros_like(q))


def _pick_heads(stacked, tq, scale=None):
    out = None
    for h, hm in enumerate(_head_masks()):
        blk = stacked[h * tq:(h + 1) * tq, :]
        if scale is not None:
            blk = blk * scale[h]
        out = jnp.where(hm, blk, 0.0 if out is None else out)
    return out


def _ada_kernel(c_ref, w_ref, b_ref, o_ref):
    c = c_ref[...]
    ca = c * jax.nn.sigmoid(c)
    o_ref[0] = _dot(ca, w_ref[0], NN, HI) + b_ref[0]


def _ada(c, w_ada, b_ada):
    depth, d, n6 = w_ada.shape
    b = c.shape[0]
    tn = n6 // 4
    return pl.pallas_call(
        _ada_kernel,
        out_shape=jax.ShapeDtypeStruct((depth, b, n6), F32),
        grid=(depth, n6 // tn),
        in_specs=[
            pl.BlockSpec((b, d), lambda l, j: (0, 0)),
            pl.BlockSpec((1, d, tn), lambda l, j: (l, 0, j)),
            pl.BlockSpec((1, 1, tn), lambda l, j: (l, 0, j)),
        ],
        out_specs=pl.BlockSpec((1, b, tn), lambda l, j: (l, 0, j)),
        compiler_params=_params("arbitrary", "arbitrary"),
        name="ada_mod",
    )(c, w_ada, b_ada.reshape(depth, 1, n6))


(W_QA, W_KC, W_KV, W_BQ, W_BK, W_BV, W_CQ, W_CK, W_CV, W_DQ, W_DK, W_DV, W_MISC) = range(13)
N_WSLABS = 13
KV_QK_LANES = 2 * HEAD_DIM
_SLABS = (
    (W_QA, "all", None, F32), (W_QA, "all", "all", BF16), (W_KC, "all", None, F32),
    (W_KV, "kv", "kv", BF16),
    (W_BQ, "all", "all", BF16), (W_BK, "all", "all", BF16), (W_BV, None, None, BF16),
    (W_CQ, None, None, BF16), (W_CK, None, None, BF16), (W_CV, None, None, BF16),
    (W_DQ, "all", None, BF16), (W_DK, "all", None, BF16), (W_DV, None, None, BF16),
    (W_MISC, None, None, F32),
)
N_SLABS = len(_SLABS)
(S_QNR, S_QR, S_KC, S_KV, S_BQ, S_BK, S_BV, S_CQ, S_CK, S_CV, S_DQ, S_DK, S_DV, S_MISC) = range(N_SLABS)
_DIL_SLABS = (S_BQ, S_BK, S_BV)
_DILATIONS = tuple(d for (_, d) in DIL_CONFIGS if d > 1)
KSL_LANE, KW_LANE, VSL_LANE, VW_LANE = (i * HEAD_DIM for i in range(4))


def _proj_kernel(x_ref, g_ref, sc_ref, sh_ref, w_ref, gain_ref, bd_ref, rope_ref, *out_refs):
    h = _mod_norm(x_ref[...], g_ref[...], sc_ref[0], sh_ref[0]).astype(BF16)
    bd = bd_ref[...]
    key_lanes = lax.broadcasted_iota(jnp.int32, (1, W_MIX), 1) < KV_QK_LANES
    last_w, y_n = None, None
    raw = [_dot(h, w_ref[:, wi * W_MIX:(wi + 1) * W_MIX]) for wi in range(N_WSLABS)]
    for s, (wi, norm, rope, _) in enumerate(_SLABS):
        if wi != last_w:
            y_n = raw[wi]
            if norm:
                normed = y_n * lax.rsqrt(_dot_split(y_n * y_n, bd) + EPS)
                y_n = normed if norm == "all" else jnp.where(key_lanes, normed, y_n)
            last_w = wi
        y = y_n * gain_ref[s]
        if rope:
            c, s1, s2 = rope_ref[0, 0], rope_ref[0, 1], rope_ref[0, 2]
            if rope == "kv":
                c, s1, s2 = jnp.where(key_lanes, c, 1.0), jnp.where(key_lanes, s1, 0.0), jnp.where(key_lanes, s2, 0.0)
            y = y * c + pltpu.roll(y, W_MIX - ROPE_HALF, 1) * s1 + pltpu.roll(y, ROPE_HALF, 1) * s2
        out_refs[s][...] = y.astype(out_refs[s].dtype)
        if s in _DIL_SLABS:
            stage_scr = out_refs[-1]
            tt = y.shape[0]
            for c in range(W_MIX // LANES):
                stage_scr[c] = y[:, c * LANES:(c + 1) * LANES]
            for di, dil in enumerate(_DILATIONS):
                o_ref = out_refs[N_SLABS + _DIL_SLABS.index(s) * len(_DILATIONS) + di]
                for r in range(dil):
                    for c in range(W_MIX // LANES):
                        lo = r * W_MIX + c * LANES
                        o_ref[:, lo:lo + LANES] = (
                            stage_scr[c, pl.ds(r, tt // dil, stride=dil), :].astype(o_ref.dtype))


def _proj(x2, g, sc, sh, w_slab, gain, bd, rope, seq, tt):
    n, d = x2.shape
    tpb = seq // tt
    out_shape = [jax.ShapeDtypeStruct((n, W_MIX), dt) for (_, _, _, dt) in _SLABS]
    out_specs = [pl.BlockSpec((tt, W_MIX), lambda i: (i, 0)) for _ in _SLABS]
    for _ in _DIL_SLABS:
        for dil in _DILATIONS:
            out_shape.append(jax.ShapeDtypeStruct((n // dil, dil * W_MIX), BF16))
            out_specs.append(pl.BlockSpec((tt // dil, dil * W_MIX), lambda i: (i, 0)))
    return pl.pallas_call(
        _proj_kernel,
        out_shape=out_shape,
        grid=(n // tt,),
        in_specs=[
            pl.BlockSpec((tt, d), lambda i: (i, 0)),
            pl.BlockSpec((1, d), lambda i: (0, 0)),
            pl.BlockSpec((1, 1, d), lambda i: (i // tpb, 0, 0)),
            pl.BlockSpec((1, 1, d), lambda i: (i // tpb, 0, 0)),
            pl.BlockSpec((d, N_WSLABS * W_MIX), lambda i: (0, 0)),
            pl.BlockSpec((N_SLABS, 1, W_MIX), lambda i: (0, 0, 0)),
            pl.BlockSpec((W_MIX, W_MIX), lambda i: (0, 0)),
            pl.BlockSpec((1, 3, tt, W_MIX), lambda i: (i // tpb, 0, i % tpb, 0)),
        ],
        out_specs=out_specs,
        scratch_shapes=[pltpu.VMEM((W_MIX // LANES, tt, LANES), F32)],
        compiler_params=_params("arbitrary"),
        name="in_proj",
    )(x2, g, sc, sh, w_slab, gain, bd, rope)


def _pack_w_in(w_in, qk_gain):
    d = w_in.shape[0]
    o = 0
    a_q = w_in[:, o:o + W_MIX]; o += W_MIX
    a_kv = w_in[:, o:o + 6 * HEAD_DIM]; o += 6 * HEAD_DIM
    a_g = w_in[:, o:o + 3 * N_HEADS]; o += 3 * N_HEADS
    b_qkv = w_in[:, o:o + 3 * W_MIX]; o += 3 * W_MIX
    c_qkv = w_in[:, o:o + 3 * W_MIX]; o += 3 * W_MIX
    d_qkv = w_in[:, o:o + 3 * W_MIX]; o += 3 * W_MIX
    d_f = w_in[:, o:o + N_HEADS]; o += N_HEADS
    w_merge = w_in[:, o:]
    kc, vc, ksl, vsl, kw, vw = (a_kv[:, i * HEAD_DIM:(i + 1) * HEAD_DIM] for i in range(6))
    zeros = lambda w: jnp.zeros((d, w), w_in.dtype)
    misc = jnp.concatenate([a_g, d_f, zeros(VC_LANE - FOX_LANE - N_HEADS), vc, zeros(W_MIX - 2 * HEAD_DIM)], axis=1)
    slabs = [
        a_q, jnp.concatenate([kc, zeros(W_MIX - HEAD_DIM)], axis=1),
        jnp.concatenate([ksl, kw, vsl, vw], axis=1),
        b_qkv[:, :W_MIX], b_qkv[:, W_MIX:2 * W_MIX], b_qkv[:, 2 * W_MIX:],
        c_qkv[:, :W_MIX], c_qkv[:, W_MIX:2 * W_MIX], c_qkv[:, 2 * W_MIX:],
        d_qkv[:, :W_MIX], d_qkv[:, W_MIX:2 * W_MIX], d_qkv[:, 2 * W_MIX:],
        misc,
    ]
    w_slab = jnp.concatenate(slabs, axis=1).astype(BF16)
    scale = HEAD_DIM ** -0.5
    t4 = lambda gvec: jnp.tile(gvec, N_HEADS)
    one = jnp.ones((W_MIX,), F32)
    scale2 = scale * LOG2E
    gains = [
        t4(qk_gain[0]) * scale, t4(qk_gain[0]) * scale2,
        jnp.concatenate([qk_gain[1], jnp.ones((W_MIX - HEAD_DIM,), F32)]),
        jnp.concatenate([qk_gain[2], qk_gain[3], jnp.ones((W_MIX - KV_QK_LANES,), F32)]),
        t4(qk_gain[4]) * scale2, t4(qk_gain[5]), one,
        one * scale, one, one,
        t4(qk_gain[6]) * scale2, t4(qk_gain[7]), one,
        one,
    ]
    gain = jnp.stack(gains).reshape(N_SLABS, 1, W_MIX).astype(F32)
    return w_slab, gain, w_merge.astype(BF16)


def _rope_tables(positions):
    inv = ROPE_THETA ** (-jnp.arange(0, ROPE_DIMS, 2, dtype=F32) / ROPE_DIMS)
    ang = positions.astype(F32)[..., None] * inv
    cos, sin = jnp.cos(ang), jnp.sin(ang)
    b, s, _ = cos.shape
    pad1 = jnp.ones((b, s, HEAD_DIM - ROPE_DIMS), F32)
    pad0 = jnp.zeros((b, s, HEAD_DIM - ROPE_DIMS), F32)
    z8 = jnp.zeros_like(sin)
    c64 = jnp.concatenate([cos, cos, pad1], axis=-1)
    s1_64 = jnp.concatenate([-sin, z8, pad0], axis=-1)
    s2_64 = jnp.concatenate([z8, sin, pad0], axis=-1)
    heads = lambda t: jnp.tile(t, (1, 1, N_HEADS))
    return jnp.stack([heads(c64), heads(s1_64), heads(s2_64)], axis=1)


def _compress_kernel(a_ref, b_ref, pe_ref, w1_ref, w2_ref, o_ref):
    half = w1_ref.shape[1] // 2
    w1 = w1_ref[0]
    hid = (_dot(a_ref[0, 0], w1[:half], NN, HI) + _dot(b_ref[0, 0], w1[half:], NN, HI)
           + _dot(pe_ref[0], w1, NN, HI))
    o_ref[0, 0] = _dot(jax.nn.gelu(hid), w2_ref[0], NN, HI)


def _compress(ch, chn, pe, w1, w2):
    _, b, ncp, cw = ch.shape
    hid = w1.shape[2]
    return pl.pallas_call(
        _compress_kernel,
        out_shape=jax.ShapeDtypeStruct((2, b, ncp, HEAD_DIM), F32),
        grid=(2, b),
        in_specs=[
            pl.BlockSpec((1, 1, ncp, cw), lambda k, i: (k, i, 0, 0)),
            pl.BlockSpec((1, 1, ncp, cw), lambda k, i: (k, i, 0, 0)),
            pl.BlockSpec((1, 1, 2 * cw), lambda k, i: (k, 0, 0)),
            pl.BlockSpec((1, 2 * cw, hid), lambda k, i: (k, 0, 0)),
            pl.BlockSpec((1, hid, HEAD_DIM), lambda k, i: (k, 0, 0)),
        ],
        out_specs=pl.BlockSpec((1, 1, ncp, HEAD_DIM), lambda k, i: (k, i, 0, 0)),
        compiler_params=_params("arbitrary", "arbitrary"),
        name="nsa_compress",
    )(ch, chn, pe, w1, w2)


def _cmp_kernel(q_ref, kc_ref, vct_ref, ovt_ref, o_ref, sel_ref, qh_scr, ql_scr, acc_scr, *, tq, nc, n_sel):
    i = pl.program_id(1)
    ncp = kc_ref.shape[1]
    q = q_ref[0]
    q_hi = q.astype(BF16)
    _stack_masked(q_hi, qh_scr, tq)
    _stack_masked((q - q_hi.astype(F32)).astype(BF16), ql_scr, tq)
    kc = kc_ref[0]
    k_hi = kc.astype(BF16)
    k_lo = (kc - k_hi.astype(F32)).astype(BF16)
    ss = []
    for h in range(N_HEADS):
        rs = slice(h * tq, (h + 1) * tq)
        ss.append(_dot(k_hi, qh_scr[rs, :], NT) + _dot(k_lo, qh_scr[rs, :], NT)
                  + _dot(k_hi, ql_scr[rs, :], NT))
    t = i * tq + lax.broadcasted_iota(jnp.int32, (1, tq), 1)
    c = lax.broadcasted_iota(jnp.int32, (ncp, 1), 0)
    mask = (c * CMP_STRIDE + (CMP_LEN - 1) <= t) & (c < nc)
    psum = None
    ps = []
    for h in range(N_HEADS):
        sm = jnp.where(mask, ss[h], NEG)
        m = jnp.max(sm, axis=0, keepdims=True)
        e = jnp.where(mask, jnp.exp(sm - m), 0.0)
        l = jnp.sum(e, axis=0, keepdims=True)
        p = e * (1.0 / jnp.maximum(l, 1e-30))
        psum = p if psum is None else psum + p
        ps.append(p.astype(BF16))
    vct = vct_ref[0]
    for h in range(N_HEADS):
        rs = slice(h * HEAD_DIM, (h + 1) * HEAD_DIM)
        acc_scr[rs, :] = _dot(vct[rs, :], ps[h])
    o_ref[0] = acc_scr[...].T.astype(o_ref.dtype)
    imp = _dot(ovt_ref[...], psum, NN, HI)
    j = lax.broadcasted_iota(jnp.int32, (n_sel, 1), 0)
    cur = t >> SEL_SHIFT
    valid = j <= cur
    forced = (j == 0) | (j == cur) | (j == cur - 1)
    picked = valid & forced
    score = jnp.where(valid, jnp.where(forced, -3e38, imp), NEG)
    sel = jnp.where(picked, 1.0, 0.0)
    jf = j.astype(F32)
    for _ in range(min(TOPN, n_sel) - 3):
        mx = jnp.max(score, axis=0, keepdims=True)
        idx = jnp.min(jnp.where(score == mx, jf, float(n_sel)), axis=0, keepdims=True)
        pick = jf == idx
        sel = jnp.where(pick, 1.0, sel)
        score = jnp.where(pick, -3e38, score)
    sel_ref[0] = ((sel - 1.0) * SEL_NEG).T.astype(sel_ref.dtype)


def _cmp_topk(q_nr, kc_rep, vc_rep, overlap, tq, nc):
    b, seq, _ = q_nr.shape
    ncp = kc_rep.shape[1]
    n_sel = seq // SEL_LEN
    return pl.pallas_call(
        functools.partial(_cmp_kernel, tq=tq, nc=nc, n_sel=n_sel),
        out_shape=[jax.ShapeDtypeStruct((b, seq, W_MIX), BF16),
                   jax.ShapeDtypeStruct((b, seq, n_sel), BF16)],
        grid=(b, seq // tq),
        in_specs=[
            pl.BlockSpec((1, tq, W_MIX), lambda g, i: (g, i, 0)),
            pl.BlockSpec((1, ncp, W_MIX), lambda g, i: (g, 0, 0)),
            pl.BlockSpec((1, W_MIX, ncp), lambda g, i: (g, 0, 0)),
            pl.BlockSpec((n_sel, ncp), lambda g, i: (0, 0)),
        ],
        out_specs=[pl.BlockSpec((1, tq, W_MIX), lambda g, i: (g, i, 0)),
                   pl.BlockSpec((1, tq, n_sel), lambda g, i: (g, i, 0))],
        scratch_shapes=[pltpu.VMEM((N_HEADS * tq, W_MIX), BF16), pltpu.VMEM((N_HEADS * tq, W_MIX), BF16),
                        pltpu.VMEM((W_MIX, tq), F32)],
        compiler_params=_params("arbitrary", "arbitrary"),
        name="nsa_cmp_topk",
    )(q_nr, kc_rep, vc_rep, overlap)


def _causal_kernel(*refs, tq, tk, has_sel, has_bias):
    it = iter(refs)
    q_ref, k_ref, vt_ref = next(it), next(it), next(it)
    sel_ref = next(it) if has_sel else None
    fa_ref, fb_ref, fq_ref, thr_ref = (next(it) for _ in range(4)) if has_bias else (None,) * 4
    o_ref, qm_scr, acc_scr, sa_scr, sb_scr, m_scr, l_scr = (next(it) for _ in range(7))
    i = pl.program_id(1)
    t0 = i * tq
    lane = lax.broadcasted_iota(jnp.int32, (1, W_MIX), 1)
    if has_sel:
        n_sel = sel_ref.shape[2]
        qf = q_ref[0].astype(F32)
        selb = sel_ref[0].astype(F32)
        selb = jnp.concatenate([selb, jnp.zeros((tq, W_MIX - n_sel), F32)], axis=1)
        selb = pltpu.roll(selb, HEAD_DIM, 1)
        for h in range(N_HEADS):
            rot = qf if h == 0 else pltpu.roll(qf, W_MIX - h * HEAD_DIM, 1)
            qm_scr[h * tq:(h + 1) * tq, :] = jnp.where(lane < HEAD_DIM, rot, selb).astype(BF16)
    else:
        _stack_masked(q_ref[0], qm_scr, tq)
    acc_scr[...] = jnp.zeros(acc_scr.shape, F32)
    t_pos = t0 + lax.broadcasted_iota(jnp.int32, (1, tq), 1)

    def scores(kt, s_buf):
        k_t = k_ref[0, kt]
        if has_sel:
            blk = (kt * tk + lax.broadcasted_iota(jnp.int32, (tk, 1), 0)) >> SEL_SHIFT
            one_hot = jnp.where(blk == lane - HEAD_DIM, 1.0, 0.0).astype(BF16)
            k_t = jnp.where(lane < HEAD_DIM, k_t, one_hot)
        for h in range(N_HEADS):
            s = _dot(k_t, qm_scr[h * tq:(h + 1) * tq, :], NT)
            if has_bias:
                s = s + _dot(fa_ref[0, kt], fb_ref[0, h])
            s_buf[h] = s

    def update(kt, s_buf, diag):
        vt_t = vt_ref[0, kt]
        mask = None
        if diag:
            s_pos = kt * tk + lax.broadcasted_iota(jnp.int32, (tk, 1), 0)
            mask = s_pos <= t_pos
        ps, alphas = [], []
        for h in range(N_HEADS):
            s = s_buf[h]
            if mask is not None:
                s = jnp.where(mask, s, NEG)
            m_old = m_scr[h]
            m_new = jnp.maximum(m_old, jnp.max(s, axis=0, keepdims=True))
            alpha = jnp.exp2(m_old - m_new)
            m_scr[h] = m_new
            p = jnp.exp2(s - m_new)
            l_scr[h] = alpha * l_scr[h] + jnp.sum(p, axis=0, keepdims=True)
            ps.append(p.astype(BF16))
            alphas.append(alpha)
        for h in range(N_HEADS):
            rs = slice(h * HEAD_DIM, (h + 1) * HEAD_DIM)
            vs = slice(VSL_LANE, VSL_LANE + HEAD_DIM) if has_sel else rs
            acc_scr[rs, :] = alphas[h] * acc_scr[rs, :] + _dot(vt_t[vs, :], ps[h])

    def live(kt_next):
        lane = lax.broadcasted_iota(jnp.int32, (1, thr_ref.shape[3]), 1)
        hit = None
        for h in range(N_HEADS):
            top = jnp.max(fq_ref[0, h] - m_scr[h], axis=-1, keepdims=True)
            need = top >= thr_ref[0, h]
            hit = need if hit is None else (hit | need)
        return jnp.max(jnp.where(hit & (lane == kt_next), 1.0, 0.0)) > 0.5

    n_last = t0 // tk
    m_scr[...] = jnp.full(m_scr.shape, NEG, F32)
    l_scr[...] = jnp.zeros(l_scr.shape, F32)
    scores(n_last, sa_scr)
    scores(jnp.maximum(n_last - 1, 0), sb_scr)
    update(n_last, sa_scr, True)

    def pair(j):
        kt = n_last - 1 - 2 * j
        scores(kt - 1, sa_scr)
        update(kt, sb_scr, False)
        scores(jnp.maximum(kt - 2, 0), sb_scr)
        update(kt - 1, sa_scr, False)
        return kt - 2

    n_pairs = n_last // 2
    if has_bias:
        def cond(state):
            j, go = state
            return (j < n_pairs) & go

        def body(state):
            j, _ = state
            return j + 1, live(pair(j))

        _, go = lax.while_loop(cond, body, (jnp.int32(0), live(n_last - 1)))
    else:
        lax.fori_loop(0, n_pairs, lambda j, c: (pair(j), c)[1], 0)
        go = True

    @pl.when((n_last % 2 == 1) & go)
    def _():
        update(0, sb_scr, False)

    ls = [l_scr[h] for h in range(N_HEADS)]
    for h in range(N_HEADS):
        rs = slice(h * HEAD_DIM, (h + 1) * HEAD_DIM)
        acc_scr[rs, :] = acc_scr[rs, :] / ls[h]
    o_ref[0] = acc_scr[...].T.astype(o_ref.dtype)


def _split3(x):
    def cut(v):
        bits = lax.bitcast_convert_type(v, jnp.uint32) & jnp.uint32(0xFFFF0000)
        return lax.bitcast_convert_type(bits, F32)

    hi = cut(x)
    r1 = x - hi
    mid = cut(r1)
    lo = r1 - mid
    return hi.astype(BF16), mid.astype(BF16), lo.astype(BF16)


def _causal_attn(q, k, v, tq, tk, sel=None, fcum=None, qk_bound=None):
    b, s, _ = q.shape
    assert tq <= tk
    nk = s // tk
    rows = N_HEADS * tq
    n_s = N_HEADS
    vt = v.reshape(b, nk, tk, W_MIX).transpose(0, 1, 3, 2)
    args = [q, k.reshape(b, nk, tk, W_MIX), vt]
    in_specs = [
        pl.BlockSpec((1, tq, W_MIX), lambda a, i: (a, i, 0)),
        pl.BlockSpec((1, nk, tk, W_MIX), lambda a, i: (a, 0, 0, 0)),
        pl.BlockSpec((1, nk, W_MIX, tk), lambda a, i: (a, 0, 0, 0)),
    ]
    if sel is not None:
        args.append(sel)
        in_specs.append(pl.BlockSpec((1, tq, sel.shape[2]), lambda a, i: (a, i, 0)))
    if fcum is not None:
        fcum = fcum * LOG2E
        qk_bound = qk_bound * LOG2E
        nf = 2 * 3 + 2
        ones = jnp.ones(fcum.shape, BF16)
        zero = jnp.zeros(fcum.shape, BF16)
        parts = _split3(fcum)
        key_f = jnp.stack([ones, ones, ones] + [-p for p in parts] + [zero, zero], axis=-1)
        qry_f = jnp.stack(list(parts) + [ones, ones, ones, zero, zero], axis=-1)
        fa = key_f.reshape(b, nk, tk, N_HEADS * nf)
        eye = jnp.eye(N_HEADS, dtype=BF16)
        fb = jnp.einsum('bshf,hg->bgshf', qry_f, eye).reshape(b, N_HEADS, s, N_HEADS * nf).transpose(0, 1, 3, 2)
        f_rows = fcum.transpose(0, 2, 1)
        f_end = f_rows[:, :, tk - 1::tk]
        thr = jnp.pad(f_end - (qk_bound + FOX_STOP * LOG2E), ((0, 0), (0, 0), (0, LANES - nk)),
                      constant_values=BIG).reshape(b, N_HEADS, 1, LANES)
        args += [fa, fb, f_rows.reshape(b, N_HEADS, 1, s), thr]
        in_specs += [pl.BlockSpec((1, nk, tk, N_HEADS * nf), lambda a, i: (a, 0, 0, 0)),
                     pl.BlockSpec((1, N_HEADS, N_HEADS * nf, tq), lambda a, i: (a, 0, 0, i)),
                     pl.BlockSpec((1, N_HEADS, 1, tq), lambda a, i: (a, 0, 0, i)),
                     pl.BlockSpec((1, N_HEADS, 1, LANES), lambda a, i: (a, 0, 0, 0))]
    return pl.pallas_call(
        functools.partial(_causal_kernel, tq=tq, tk=tk, has_sel=sel is not None, has_bias=fcum is not None),
        out_shape=jax.ShapeDtypeStruct((b, s, W_MIX), BF16),
        grid=(b, s // tq),
        in_specs=in_specs,
        out_specs=pl.BlockSpec((1, tq, W_MIX), lambda a, i: (a, i, 0)),
        scratch_shapes=[pltpu.VMEM((rows, W_MIX), BF16), pltpu.VMEM((W_MIX, tq), F32),
                        pltpu.VMEM((n_s, tk, tq), F32), pltpu.VMEM((n_s, tk, tq), F32),
                        pltpu.VMEM((N_HEADS, 1, tq), F32), pltpu.VMEM((N_HEADS, 1, tq), F32)],
        compiler_params=_params("arbitrary", "arbitrary"),
        name="causal_sel%d_bias%d" % (sel is not None, fcum is not None),
    )(*args)


def _window_kernel(*refs, tq, wk, pad, window, ls, emit_lse, other_dils, kv_lanes):
    it = iter(refs)
    q_ref, k_ref, v_ref = next(it), next(it), next(it)
    n_other = len(other_dils)
    others_in = [(next(it), next(it)) for _ in range(n_other)]
    wexp_ref = next(it) if n_other else None
    o_ref = next(it)
    lse_ref = next(it) if emit_lse else None
    qm_scr, acc_scr = next(it), next(it)
    others = []
    if n_other:
        og_scr, lg_scr = next(it), next(it)
        for g, (dg, (og_ref, lg_ref)) in enumerate(zip(other_dils, others_in)):
            for r in range(dg):
                rows = pl.ds(r, tq // dg, stride=dg)
                for c in range(W_MIX // LANES):
                    lo = r * W_MIX + c * LANES
                    og_scr[g, c, rows, :] = og_ref[0, :, lo:lo + LANES]
                lg_scr[g, rows, :] = lg_ref[0, :, r * LANES:(r + 1) * LANES]
            others.append((og_scr.at[g], lg_scr.at[g]))
    t0 = pl.program_id(2) * tq
    start = pl.multiple_of(jnp.clip(t0 - pad, 0, ls - wk), LANES)
    if kv_lanes is None:
        _stack_masked(q_ref[0], qm_scr, tq)
    else:
        lane = lax.broadcasted_iota(jnp.int32, (1, W_MIX), 1)
        on_key = (lane >= kv_lanes[0]) & (lane < kv_lanes[0] + HEAD_DIM)
        qf = q_ref[0].astype(F32)
        for h in range(N_HEADS):
            shift = (kv_lanes[0] - h * HEAD_DIM) % W_MIX
            rot = pltpu.roll(qf, shift, 1) if shift else qf
            qm_scr[h * tq:(h + 1) * tq, :] = jnp.where(on_key, rot, 0.0).astype(BF16)
    k_w = k_ref[0, pl.ds(start, wk), :]
    vt_w = v_ref[0, pl.ds(start, wk), :].astype(F32).T.astype(BF16)
    ss = [_dot(k_w, qm_scr[h * tq:(h + 1) * tq, :], NT) for h in range(N_HEADS)]
    t_pos = t0 + lax.broadcasted_iota(jnp.int32, (1, tq), 1)
    s_pos = start + lax.broadcasted_iota(jnp.int32, (wk, 1), 0)
    mask = (s_pos <= t_pos) & (t_pos - s_pos < window)
    ps, inv_ls, lses = [], [], []
    for h in range(N_HEADS):
        s = jnp.where(mask, ss[h], NEG)
        m = jnp.max(s, axis=0, keepdims=True)
        p = jnp.exp2(s - m)
        l = jnp.sum(p, axis=0, keepdims=True)
        ps.append(p.astype(BF16))
        inv_ls.append(1.0 / l)
        lses.append(m + jnp.log2(l))
    for h in range(N_HEADS):
        rs = slice(h * HEAD_DIM, (h + 1) * HEAD_DIM)
        vs = rs if kv_lanes is None else slice(kv_lanes[1], kv_lanes[1] + HEAD_DIM)
        acc_scr[rs, :] = _dot(vt_w[vs, :], ps[h]) * inv_ls[h]
    o_self = acc_scr[...].T
    if emit_lse or n_other:
        row = lax.broadcasted_iota(jnp.int32, (LANES, 1), 0)
        stat = jnp.zeros((LANES, tq), F32)
        for h in range(N_HEADS):
            stat = jnp.where(row == h, lses[h], stat)
        lse_tile = stat.T
    if n_other:
        lg_t = [lg[...].T for (_, lg) in others]
        row = lax.broadcasted_iota(jnp.int32, (LANES, 1), 0)
        wmat = jnp.zeros((LANES, tq), F32)
        for h in range(N_HEADS):
            group_lse = [lses[h]] + [t[h:h + 1, :] for t in lg_t]
            top = functools.reduce(jnp.maximum, group_lse)
            ws = [jnp.exp2(x - top) for x in group_lse]
            inv_den = 1.0 / functools.reduce(jnp.add, ws)
            for g, w in enumerate(ws):
                wmat = jnp.where(row == g * N_HEADS + h, w * inv_den, wmat)
        wt = wmat.T
        groups = [o_self] + [jnp.concatenate([og[c] for c in range(W_MIX // LANES)], axis=1)
                             for (og, _) in others]
        out = None
        for g, o_g in enumerate(groups):
            term = _dot_split(wt, wexp_ref[g]) * o_g
            out = term if out is None else out + term
        o_ref[0] = out.astype(o_ref.dtype)
    else:
        o_ref[0] = o_self.astype(o_ref.dtype)
    if emit_lse:
        lse_ref[0] = lse_tile


def _window_attn(q, k, v, *, dil, window, tq, out_dtype, emit_lse=False, others=(), kv_lanes=None):
    b, ls, _ = q.shape
    tq = min(tq, ls)
    pad = -(-(window - 1) // LANES) * LANES
    wk = min(tq + pad, ls)
    rows = N_HEADS * tq
    args = [q, k, v]
    in_specs = [
        pl.BlockSpec((1, tq, W_MIX), lambda a, r, i: (a, i, r)),
        pl.BlockSpec((1, ls, W_MIX), lambda a, r, i: (a, 0, r)),
        pl.BlockSpec((1, ls, W_MIX), lambda a, r, i: (a, 0, r)),
    ]
    scratch = [pltpu.VMEM((rows, W_MIX), BF16), pltpu.VMEM((W_MIX, tq), F32)]
    for (o_g, lse_g, dg) in others:
        args += [o_g, lse_g]
        in_specs += [pl.BlockSpec((1, tq // dg, dg * W_MIX), lambda a, r, i: (a, i, 0)),
                     pl.BlockSpec((1, tq // dg, dg * LANES), lambda a, r, i: (a, i, 0))]
    if others:
        wexp = np.zeros((len(others) + 1, LANES, W_MIX), np.float32)
        for g in range(len(others) + 1):
            for h in range(N_HEADS):
                wexp[g, g * N_HEADS + h, h * HEAD_DIM:(h + 1) * HEAD_DIM] = 1.0
        args.append(jnp.asarray(wexp, BF16))
        in_specs.append(pl.BlockSpec(wexp.shape, lambda a, r, i: (0, 0, 0)))
        scratch += [pltpu.VMEM((len(others), W_MIX // LANES, tq, LANES), F32),
                    pltpu.VMEM((len(others), tq, LANES), F32)]
    out_shape = [jax.ShapeDtypeStruct((b, ls, dil * W_MIX), out_dtype)]
    out_specs = [pl.BlockSpec((1, tq, W_MIX), lambda a, r, i: (a, i, r))]
    if emit_lse:
        out_shape.append(jax.ShapeDtypeStruct((b, ls, dil * LANES), F32))
        out_specs.append(pl.BlockSpec((1, tq, LANES), lambda a, r, i: (a, i, r)))
    res = pl.pallas_call(
        functools.partial(_window_kernel, tq=tq, wk=wk, pad=pad, window=window, ls=ls,
                          emit_lse=emit_lse, other_dils=tuple(dg for (_, _, dg) in others),
                          kv_lanes=kv_lanes),
        out_shape=out_shape,
        grid=(b, dil, ls // tq),
        in_specs=in_specs,
        out_specs=out_specs,
        scratch_shapes=scratch,
        compiler_params=_params("arbitrary", "arbitrary", "arbitrary"),
        name="window_d%d_w%d" % (dil, window),
    )(*args)
    return (res[0], res[1], dil) if emit_lse else res[0]


def _sb_kernel(q_ref, k_ref, vt_ref, tri_ref, o_ref, qm_scr, carry_scr, acc_scr, *, tq, tk):
    i = pl.program_id(1)
    t0 = i * tq
    _stack_masked(q_ref[0], qm_scr, tq)
    carry_scr[...] = jnp.zeros(carry_scr.shape, F32)
    acc_scr[...] = jnp.zeros(acc_scr.shape, F32)
    t_pos = t0 + lax.broadcasted_iota(jnp.int32, (1, tq), 1)

    def tile(kt, diag):
        k_t = k_ref[0, kt]
        vt_t = vt_ref[0, kt]
        tri = tri_ref[...]
        zs = [_dot(k_t, qm_scr[h * tq:(h + 1) * tq, :], NT) for h in range(N_HEADS)]
        if diag:
            s_pos = kt * tk + lax.broadcasted_iota(jnp.int32, (tk, 1), 0)
            strict = s_pos < t_pos
        weights = []
        for h in range(N_HEADS):
            z = zs[h]
            lg = -(jnp.maximum(z, 0.0) + jnp.log(1.0 + jnp.exp(-jnp.abs(z))))
            if diag:
                lg = jnp.where(strict, lg, 0.0)
            hi = lg.astype(BF16)
            lo = (lg - hi.astype(F32)).astype(BF16)
            cum = _dot(tri, hi) + _dot(tri, lo) + carry_scr[h]
            a = jnp.exp(z + cum)
            if diag:
                a = jnp.where(strict, a, 0.0)
            weights.append(a.astype(BF16))
            carry_scr[h] += jnp.sum(lg, axis=0, keepdims=True)
        for h in range(N_HEADS):
            rs = slice(h * HEAD_DIM, (h + 1) * HEAD_DIM)
            acc_scr[rs, :] += _dot(vt_t[rs, :], weights[h])

    n_diag = tq // tk
    for d in reversed(range(n_diag)):
        tile(i * n_diag + d, True)
    n_below = i * n_diag

    def cond(state):
        j, top = state
        return (j < n_below) & (top > SB_STOP)

    def body(state):
        j, _ = state
        tile(n_below - 1 - j, False)
        return j + 1, jnp.max(carry_scr[...])

    lax.while_loop(cond, body, (jnp.int32(0), jnp.max(carry_scr[...])))
    o_ref[0] = acc_scr[...].T.astype(o_ref.dtype)


def _stick_breaking(q, k, v, tq, tk):
    b, s, _ = q.shape
    assert tq % tk == 0
    nk = s // tk
    rows = N_HEADS * tq
    tri = jnp.asarray(np.triu(np.ones((tk, tk), np.float32)), BF16)
    vt = v.reshape(b, nk, tk, W_MIX).transpose(0, 1, 3, 2)
    return pl.pallas_call(
        functools.partial(_sb_kernel, tq=tq, tk=tk),
        out_shape=jax.ShapeDtypeStruct((b, s, W_MIX), BF16),
        grid=(b, s // tq),
        in_specs=[
            pl.BlockSpec((1, tq, W_MIX), lambda a, i: (a, i, 0)),
            pl.BlockSpec((1, nk, tk, W_MIX), lambda a, i: (a, 0, 0, 0)),
            pl.BlockSpec((1, nk, W_MIX, tk), lambda a, i: (a, 0, 0, 0)),
            pl.BlockSpec((tk, tk), lambda a, i: (0, 0)),
        ],
        out_specs=pl.BlockSpec((1, tq, W_MIX), lambda a, i: (a, i, 0)),
        scratch_shapes=[pltpu.VMEM((rows, W_MIX), BF16), pltpu.VMEM((N_HEADS, 1, tq), F32),
                        pltpu.VMEM((W_MIX, tq), F32)],
        compiler_params=_params("arbitrary", "arbitrary"),
        name="stick_breaking",
    )(q, k.reshape(b, nk, tk, W_MIX), vt, tri)


def _foxcum_kernel(x_ref, b_ref, tri_ref, o_ref, carry_scr):
    @pl.when(pl.program_id(1) == 0)
    def _():
        carry_scr[...] = jnp.zeros(carry_scr.shape, F32)

    z = x_ref[0] + b_ref[...]
    logf = jnp.minimum(z, 0.0) - jnp.log(1.0 + jnp.exp(-jnp.abs(z)))
    tri = tri_ref[...]
    hi = logf.astype(BF16)
    r1 = logf - hi.astype(F32)
    mid = r1.astype(BF16)
    lo = (r1 - mid.astype(F32)).astype(BF16)
    cum = _dot(tri, hi) + _dot(tri, mid) + _dot(tri, lo) + carry_scr[...]
    o_ref[0] = cum
    carry_scr[...] = cum[cum.shape[0] - 1:, :]


def _fox_cumsum(misc, bias_vec, tc):
    b, s, w = misc.shape
    tri = jnp.asarray(np.tril(np.ones((tc, tc), np.float32)), BF16)
    return pl.pallas_call(
        _foxcum_kernel,
        out_shape=jax.ShapeDtypeStruct((b, s, w), F32),
        grid=(b, s // tc),
        in_specs=[
            pl.BlockSpec((1, tc, w), lambda a, i: (a, i, 0)),
            pl.BlockSpec((1, w), lambda a, i: (0, 0)),
            pl.BlockSpec((tc, tc), lambda a, i: (0, 0)),
        ],
        out_specs=pl.BlockSpec((1, tc, w), lambda a, i: (a, i, 0)),
        scratch_shapes=[pltpu.VMEM((1, w), F32)],
        compiler_params=_params("arbitrary", "arbitrary"),
        name="fox_cumsum",
    )(misc, bias_vec, tri)


def _merge_kernel(x_ref, g_ref, sc_ref, sh_ref, ga_ref, wm_ref, misc_ref, pg_ref,
                  ocmp_ref, osel_ref, owin_ref, ob_ref, oc_ref, od_ref,
                  wa_ref, wb_ref, wc_ref, wd_ref, wo_ref, o_ref):
    x = x_ref[...]
    d = x.shape[1]
    h = _mod_norm(x, g_ref[...], sc_ref[0], sh_ref[0]).astype(BF16)
    gate = jax.nn.sigmoid(misc_ref[...])
    o_a = (_dot_split(gate, pg_ref[0]) * ocmp_ref[...].astype(F32)
           + _dot_split(gate, pg_ref[1]) * osel_ref[...].astype(F32)
           + _dot_split(gate, pg_ref[2]) * owin_ref[...].astype(F32)).astype(BF16)
    mixed = jnp.zeros(x.shape, F32)
    for m, (o_m, w_ref) in enumerate(((o_a, wa_ref), (ob_ref[...], wb_ref),
                                      (oc_ref[...], wc_ref), (od_ref[...], wd_ref))):
        y = _dot(o_m, w_ref[...])
        gl = _dot(h, wm_ref[:, m * d:(m + 1) * d])
        mixed = mixed + jax.nn.sigmoid(gl) * y
    o_ref[...] = x + ga_ref[0] * _dot(mixed.astype(BF16), wo_ref[...])


def _merge(x2, g, sc, sh, ga, w_merge, misc, pg, o_cmp, o_sel, o_win, o_b, o_c, o_d,
           wa, wb, wc, wd, wo, seq, tt):
    n, d = x2.shape
    tpb = seq // tt
    row = lambda w: pl.BlockSpec((tt, w), lambda i: (i, 0))
    full = lambda a: pl.BlockSpec(a.shape, lambda i: (0,) * a.ndim)
    per_b = pl.BlockSpec((1, 1, d), lambda i: (i // tpb, 0, 0))
    return pl.pallas_call(
        _merge_kernel,
        out_shape=jax.ShapeDtypeStruct((n, d), F32),
        grid=(n // tt,),
        in_specs=[row(d), full(g), per_b, per_b, per_b, full(w_merge), row(W_MIX), full(pg)]
        + [row(W_MIX)] * 6 + [full(wa), full(wb), full(wc), full(wd), full(wo)],
        out_specs=row(d),
        compiler_params=_params("arbitrary"),
        name="merge_out",
    )(x2, g, sc, sh, ga, w_merge, misc, pg, o_cmp, o_sel, o_win, o_b, o_c, o_d, wa, wb, wc, wd, wo)


def _ffn_kernel(x_ref, g_ref, sc_ref, sh_ref, gf_ref, w1_ref, w3_ref, w2_ref, o_ref, h_scr, acc_scr):
    f = pl.program_id(1)

    @pl.when(f == 0)
    def _():
        h_scr[...] = _mod_norm(x_ref[...], g_ref[...], sc_ref[0], sh_ref[0]).astype(BF16)
        acc_scr[...] = jnp.zeros(acc_scr.shape, F32)

    h = h_scr[...]
    a = _dot(h, w1_ref[...])
    b = _dot(h, w3_ref[...])
    acc_scr[...] += _dot((a * jax.nn.sigmoid(a) * b).astype(BF16), w2_ref[...])

    @pl.when(f == pl.num_programs(1) - 1)
    def _():
        o_ref[...] = x_ref[...] + gf_ref[0] * acc_scr[...]


def _ffn(x2, g, sc, sh, gf, w1, w3, w2, seq, tt, tf):
    n, d = x2.shape
    dff = w1.shape[1]
    tpb = seq // tt
    per_b = pl.BlockSpec((1, 1, d), lambda i, f: (i // tpb, 0, 0))
    return pl.pallas_call(
        _ffn_kernel,
        out_shape=jax.ShapeDtypeStruct((n, d), F32),
        grid=(n // tt, dff // tf),
        in_specs=[
            pl.BlockSpec((tt, d), lambda i, f: (i, 0)),
            pl.BlockSpec((1, d), lambda i, f: (0, 0)),
            per_b, per_b, per_b,
            pl.BlockSpec((d, tf), lambda i, f: (0, f)),
            pl.BlockSpec((d, tf), lambda i, f: (0, f)),
            pl.BlockSpec((tf, d), lambda i, f: (f, 0)),
        ],
        out_specs=pl.BlockSpec((tt, d), lambda i, f: (i, 0)),
        scratch_shapes=[pltpu.VMEM((tt, d), BF16), pltpu.VMEM((tt, d), F32)],
        compiler_params=_params("arbitrary", "arbitrary"),
        name="ffn_swiglu",
    )(x2, g, sc, sh, gf, w1, w3, w2)


def _route_kernel(x_ref, g_ref, sc_ref, sh_ref, rw_ref, up_ref, h_ref, rank_ref, gate_ref, cnt_ref):
    hf = _mod_norm(x_ref[...], g_ref[...], sc_ref[0], sh_ref[0])
    h_ref[...] = hf.astype(BF16)
    logits = _dot(rw_ref[...], hf, NT, HI)
    ne, tt = logits.shape
    e_idx = lax.broadcasted_iota(jnp.int32, (ne, 1), 0).astype(F32)
    v1 = jnp.max(logits, axis=0, keepdims=True)
    i1 = jnp.min(jnp.where(logits == v1, e_idx, float(ne)), axis=0, keepdims=True)
    m1 = e_idx == i1
    rest = jnp.where(m1, -3e38, logits)
    v2 = jnp.max(rest, axis=0, keepdims=True)
    i2 = jnp.min(jnp.where(rest == v2, e_idx, float(ne)), axis=0, keepdims=True)
    m2 = e_idx == i2
    e2 = jnp.exp(v2 - v1)
    g1 = 1.0 / (1.0 + e2)
    g2 = e2 / (1.0 + e2)
    routed = m1 | m2
    rf = jnp.where(routed, 1.0, 0.0)
    rank = _dot(rf.astype(BF16), up_ref[...])
    rank = jnp.where(routed, rank, -1.0)
    gate = jnp.where(m1, g1, 0.0) + jnp.where(m2, g2, 0.0)
    for e in range(ne):
        rank_ref[0, e] = rank[e:e + 1, :]
        gate_ref[0, e] = gate[e:e + 1, :]
    cnt = jnp.sum(rf, axis=1, keepdims=True)
    cnt_ref[0] = jnp.broadcast_to(cnt, (ne, LANES))


def _route(x2, g, sc, sh, rw_t, seq, tt):
    n, d = x2.shape
    ne = rw_t.shape[0]
    tpb = seq // tt
    nt = n // tt
    upper = jnp.asarray(np.triu(np.ones((tt, tt), np.float32), 1), BF16)
    per_b = pl.BlockSpec((1, 1, d), lambda i: (i // tpb, 0, 0))
    return pl.pallas_call(
        _route_kernel,
        out_shape=[jax.ShapeDtypeStruct((n, d), BF16),
                   jax.ShapeDtypeStruct((nt, ne, 1, tt), F32),
                   jax.ShapeDtypeStruct((nt, ne, 1, tt), F32),
                   jax.ShapeDtypeStruct((nt, ne, LANES), F32)],
        grid=(nt,),
        in_specs=[
            pl.BlockSpec((tt, d), lambda i: (i, 0)),
            pl.BlockSpec((1, d), lambda i: (0, 0)),
            per_b, per_b,
            pl.BlockSpec((ne, d), lambda i: (0, 0)),
            pl.BlockSpec((tt, tt), lambda i: (0, 0)),
        ],
        out_specs=[pl.BlockSpec((tt, d), lambda i: (i, 0)),
                   pl.BlockSpec((1, ne, 1, tt), lambda i: (i, 0, 0, 0)),
                   pl.BlockSpec((1, ne, 1, tt), lambda i: (i, 0, 0, 0)),
                   pl.BlockSpec((1, ne, LANES), lambda i: (i, 0, 0))],
        compiler_params=_params("arbitrary"),
        name="moe_route",
    )(x2, g, sc, sh, rw_t, upper)


def _moe_kernel(cnt_ref, x_ref, gf_ref, h_ref, rank_ref, gate_ref, w1_ref, w3_ref, w2_ref,
                o_ref, acc_scr, xs_scr, y_scr, *, chunk):
    i, e, f = pl.program_id(0), pl.program_id(1), pl.program_id(2)
    ne, nf = pl.num_programs(1), pl.num_programs(2)

    @pl.when((e == 0) & (f == 0))
    def _():
        acc_scr[...] = jnp.zeros(acc_scr.shape, F32)

    count = cnt_ref[i * ne + e]
    rank = rank_ref[0, 0]
    gate = gate_ref[0, 0]
    n_small = (count + chunk - 1) // chunk
    n_big = (count + 2 * chunk - 1) // (2 * chunk)

    def one_hot(c, rows):
        r = c * rows + lax.broadcasted_iota(jnp.int32, (rows, 1), 0)
        return rank == r.astype(F32)

    def rows_of(c, rows):
        return pl.ds(pl.multiple_of(c * rows, rows), rows)

    @pl.when(f == 0)
    def _():
        h = h_ref[...]

        def gather(c, carry):
            p = jnp.where(one_hot(c, chunk), 1.0, 0.0).astype(BF16)
            xs_scr[rows_of(c, chunk), :] = _dot(p, h).astype(BF16)
            return carry

        lax.fori_loop(0, n_small, gather, 0)

        def clear(c, carry):
            y_scr[rows_of(c, 2 * chunk), :] = jnp.zeros((2 * chunk, y_scr.shape[1]), F32)
            return carry

        lax.fori_loop(0, n_big, clear, 0)

    def expert(c, carry):
        xs = xs_scr[rows_of(c, chunk), :]
        a = _dot(xs, w1_ref[0])
        b = _dot(xs, w3_ref[0])
        y_scr[rows_of(c, chunk), :] += _dot((a * jax.nn.sigmoid(a) * b).astype(BF16), w2_ref[0])
        return carry

    lax.fori_loop(0, n_small, expert, 0)

    @pl.when(f == nf - 1)
    def _():
        def scatter(c, carry):
            hit = one_hot(c, 2 * chunk)
            p = jnp.where(hit, 1.0, 0.0).astype(BF16)
            gcol = jnp.sum(jnp.where(hit, gate, 0.0), axis=-1, keepdims=True)
            acc_scr[...] += _dot(p, (y_scr[rows_of(c, 2 * chunk), :] * gcol).astype(BF16), TN)
            return carry

        lax.fori_loop(0, n_big, scatter, 0)

    @pl.when((e == ne - 1) & (f == nf - 1))
    def _():
        o_ref[...] = x_ref[...] + gf_ref[0] * acc_scr[...]


def _moe(counts, x2, gf, h2, rank, gate, w1, w3, w2, seq, tt, tf, chunk):
    n, d = x2.shape
    ne, _, dff = w1.shape
    tpb = seq // tt
    grid_spec = pltpu.PrefetchScalarGridSpec(
        num_scalar_prefetch=1,
        grid=(n // tt, ne, dff // tf),
        in_specs=[
            pl.BlockSpec((tt, d), lambda i, e, f, c: (i, 0)),
            pl.BlockSpec((1, 1, d), lambda i, e, f, c: (i // tpb, 0, 0)),
            pl.BlockSpec((tt, d), lambda i, e, f, c: (i, 0)),
            pl.BlockSpec((1, 1, 1, tt), lambda i, e, f, c: (i, e, 0, 0)),
            pl.BlockSpec((1, 1, 1, tt), lambda i, e, f, c: (i, e, 0, 0)),
            pl.BlockSpec((1, d, tf), lambda i, e, f, c: (e, 0, f)),
            pl.BlockSpec((1, d, tf), lambda i, e, f, c: (e, 0, f)),
            pl.BlockSpec((1, tf, d), lambda i, e, f, c: (e, f, 0)),
        ],
        out_specs=pl.BlockSpec((tt, d), lambda i, e, f, c: (i, 0)),
        scratch_shapes=[pltpu.VMEM((tt, d), F32), pltpu.VMEM((tt, d), BF16), pltpu.VMEM((tt, d), F32)],
    )
    return pl.pallas_call(
        functools.partial(_moe_kernel, chunk=chunk),
        out_shape=jax.ShapeDtypeStruct((n, d), F32),
        grid_spec=grid_spec,
        compiler_params=_params("arbitrary", "arbitrary", "arbitrary"),
        name="moe_experts",
    )(counts, x2, gf, h2, rank, gate, w1, w3, w2)


def _overlap_matrix(ncp, nc, n_sel):
    c0 = np.arange(ncp) * CMP_STRIDE
    c1 = c0 + CMP_LEN
    s0 = np.arange(n_sel) * SEL_LEN
    s1 = s0 + SEL_LEN
    ov = ((c0[:, None] < s1[None, :]) & (c1[:, None] > s0[None, :])).astype(np.float32)
    ov[nc:] = 0.0
    return jnp.asarray(ov.T)


def _gate_expand():
    pg = np.zeros((3, W_MIX, W_MIX), np.float32)
    for br in range(3):
        for h in range(N_HEADS):
            pg[br, GATE_LANE + 3 * h + br, h * HEAD_DIM:(h + 1) * HEAD_DIM] = 1.0
    return jnp.asarray(pg, BF16)


def _mixer_layer(x2, b, s, mod, norm_g, rope, w_in, qk_gain, pe_k, pe_v, ck1, ck2, cv1, cv2,
                 fox_b, w_branch, w_out):
    n, d = x2.shape
    sh_a, sc_a, g_a = mod[0], mod[1], mod[2]
    w_slab, gain, w_merge = _pack_w_in(w_in, qk_gain)
    bd = jnp.asarray(np.kron(np.eye(N_HEADS), np.full((HEAD_DIM, HEAD_DIM), 1.0 / HEAD_DIM)), BF16)
    tiles = _tiles(s)
    outs = _proj(x2, norm_g, sc_a, sh_a, w_slab, gain, bd, rope, s, tiles.proj)
    sl = [a.reshape(b, s, W_MIX) for a in outs[:N_SLABS]]
    dil_in = {1: (sl[S_BQ], sl[S_BK], sl[S_BV])}
    for di, dil in enumerate(_DILATIONS):
        dil_in[dil] = tuple(outs[N_SLABS + si * len(_DILATIONS) + di].reshape(b, s // dil, dil * W_MIX)
                            for si in range(len(_DIL_SLABS)))
    misc = sl[S_MISC]

    nch = s // CMP_STRIDE
    nc = nch - CMP_LEN // CMP_STRIDE + 1
    kc_raw = sl[S_KC][..., :HEAD_DIM]
    vc_raw = misc[..., VC_LANE:VC_LANE + HEAD_DIM]
    chunks = jnp.stack([kc_raw, vc_raw]).reshape(2, b, nch, CMP_STRIDE * HEAD_DIM)
    chunks_next = jnp.concatenate([chunks[:, :, 1:], jnp.zeros_like(chunks[:, :, :1])], axis=2)
    pe = jnp.stack([pe_k, pe_v]).reshape(2, 1, CMP_LEN * HEAD_DIM)
    kvc = jnp.tile(_compress(chunks, chunks_next, pe, jnp.stack([ck1, cv1]), jnp.stack([ck2, cv2])),
                   (1, 1, 1, N_HEADS))
    overlap = _overlap_matrix(nch, nc, s // SEL_LEN)
    o_cmp, selmask = _cmp_topk(sl[S_QNR], kvc[0], kvc[1].transpose(0, 2, 1).astype(BF16), overlap,
                               tiles.causal_q, nc)
    tq, tk = tiles.causal_q, tiles.causal_k
    o_sel = _causal_attn(sl[S_QR], sl[S_KV], sl[S_KV], tq, tk, sel=selmask)
    o_win = _window_attn(sl[S_QR], sl[S_KV], sl[S_KV], dil=1, window=NSA_WINDOW, tq=tiles.window_q,
                         out_dtype=BF16,
                         kv_lanes=(KW_LANE, VW_LANE))

    others = []
    for (wdw, dil) in DIL_CONFIGS[:0:-1]:
        others.append(_window_attn(*dil_in[dil], dil=dil, window=wdw // dil + 1,
                                   tq=s // dil if s // dil <= tiles.short_seq else tiles.window_q,
                                   out_dtype=F32, emit_lse=True))
    wdw, dil = DIL_CONFIGS[0]
    o_b = _window_attn(*dil_in[dil], dil=dil, window=wdw // dil + 1, tq=tiles.window_q, out_dtype=BF16,
                       others=others)

    o_c = _stick_breaking(sl[S_CQ], sl[S_CK], sl[S_CV], tiles.sb, tiles.sb)

    bias_vec = jnp.zeros((1, W_MIX), F32).at[0, FOX_LANE:FOX_LANE + N_HEADS].set(fox_b)
    fcum = _fox_cumsum(misc, bias_vec, tiles.cum)
    qk_bound = 1.02 * HEAD_DIM ** 0.5 * jnp.max(jnp.abs(qk_gain[6])) * jnp.max(jnp.abs(qk_gain[7])) + 0.05
    o_d = _causal_attn(sl[S_DQ], sl[S_DK], sl[S_DV], tq, tk, fcum=fcum[..., FOX_LANE:FOX_LANE + N_HEADS],
                       qk_bound=qk_bound)

    wb16 = w_branch.astype(BF16)
    flat = lambda a: a.reshape(n, W_MIX)
    return _merge(x2, norm_g, sc_a, sh_a, g_a, w_merge, flat(misc), _gate_expand(),
                  flat(o_cmp), flat(o_sel), flat(o_win), flat(o_b), flat(o_c), flat(o_d),
                  wb16[0], wb16[1], wb16[2], wb16[3], w_out.astype(BF16), s, tiles.proj)


def kernel(x, c, positions, w_ada, b_ada, norm_mix, norm_ffn, w_in, qk_gain, nsa_pe_k, nsa_pe_v,
           nsa_ck_w1, nsa_ck_w2, nsa_cv_w1, nsa_cv_w2, fox_bias, w_branch, w_out,
           ffn_w1, ffn_w3, ffn_w2, router_w, moe_w1, moe_w3, moe_w2):
    b, s, d = x.shape
    depth = w_ada.shape[0]
    rope = _rope_tables(positions)
    mods = _ada(c, w_ada, b_ada).reshape(depth, b, 6, 1, d).transpose(0, 2, 1, 3, 4)
    x2 = x.reshape(b * s, d)
    for l in range(depth):
        mod = mods[l]
        x2 = _mixer_layer(x2, b, s, mod[0:3], norm_mix[l].reshape(1, d), rope, w_in[l], qk_gain[l],
                          nsa_pe_k[l], nsa_pe_v[l], nsa_ck_w1[l], nsa_ck_w2[l], nsa_cv_w1[l],
                          nsa_cv_w2[l], fox_bias[l], w_branch[l], w_out[l])
        sh_f, sc_f, g_f = mod[3], mod[4], mod[5]
        gn = norm_ffn[l].reshape(1, d)
        e = l // 2
        tiles = _tiles(s)
        if l % 2 == 0:
            dff = ffn_w1.shape[2]
            x2 = _ffn(x2, gn, sc_f, sh_f, g_f, ffn_w1[e].astype(BF16), ffn_w3[e].astype(BF16),
                      ffn_w2[e].astype(BF16), s, tiles.ffn, dff // 2)
        else:
            dff = moe_w1.shape[3]
            h2, rank, gate, cnt = _route(x2, gn, sc_f, sh_f, router_w[e].T, s, tiles.ffn)
            counts = cnt[:, :, 0].astype(jnp.int32).reshape(-1)
            x2 = _moe(counts, x2, g_f, h2, rank, gate, moe_w1[e].astype(BF16), moe_w3[e].astype(BF16),
                      moe_w2[e].astype(BF16), s, tiles.ffn, dff // 2, tiles.moe_chunk)
    return x2.reshape(b, s, d)
```

```python
import functools
from typing import NamedTuple

import numpy as np
import jax
import jax.numpy as jnp
from jax import lax
from jax.experimental import pallas as pl
from jax.experimental.pallas import tpu as pltpu

F32 = jnp.float32
BF16 = jnp.bfloat16
HI = lax.Precision.HIGHEST

LANES = 128
VMEM_LIMIT = 52 * 1024 * 1024

HEAD_DIM = 64
HEAD_SHIFT = 6
N_HEADS = 4
W_MIX = N_HEADS * HEAD_DIM
N_MIXERS = 4
ROPE_THETA = 500000.0
ROPE_DIMS = HEAD_DIM // 4
ROPE_HALF = ROPE_DIMS // 2
EPS = 1e-6
LOG2E = 1.4426950408889634
NEG = -1e30
BIG = 1e30
CMP_LEN = 32
CMP_STRIDE = 16
SEL_LEN = 64
SEL_SHIFT = 6
TOPN = 16
SEL_NEG = 2.0 ** 60
SHIFT_LANE = HEAD_DIM + 128
FIXED_MAX_LIMIT = 40.0
NSA_WINDOW = 512
DIL_CONFIGS = ((128, 1), (512, 4), (2048, 16))
SB_STOP = -110.0
FOX_STOP = 108.0
GATE_LANE = 0
FOX_LANE = 3 * N_HEADS
VC_LANE = HEAD_DIM

class _Tiles(NamedTuple):
    proj: int
    cum: int
    ffn: int
    moe_chunk: int
    causal_q: int
    causal_k: int
    window_q: int
    short_seq: int
    sb: int


def _tiles(s):
    return _Tiles(proj=min(512, s), cum=min(512, s), ffn=min(1024, s), moe_chunk=128,
                  causal_q=min(512, s), causal_k=min(512, s), window_q=256, short_seq=512,
                  sb=min(256, s))


NN = (((1,), (0,)), ((), ()))
NT = (((1,), (1,)), ((), ()))
TN = (((0,), (0,)), ((), ()))


def _dot(a, b, dims=NN, precision=None):
    return lax.dot_general(a, b, dims, precision=precision, preferred_element_type=F32)


def _dot_split(a, b_bf16, dims=NN):
    hi = a.astype(BF16)
    lo = (a - hi.astype(F32)).astype(BF16)
    return _dot(hi, b_bf16, dims) + _dot(lo, b_bf16, dims)


def _params(*sem):
    return pltpu.CompilerParams(dimension_semantics=sem, vmem_limit_bytes=VMEM_LIMIT)


def _mod_norm(x, g, sc, sh):
    ms = jnp.mean(x * x, axis=-1, keepdims=True)
    return (x * lax.rsqrt(ms + EPS) * g) * (1.0 + sc) + sh


def _head_masks():
    lane = lax.broadcasted_iota(jnp.int32, (1, W_MIX), 1)
    return [(lane >> HEAD_SHIFT) == h for h in range(N_HEADS)]


def _stack_masked(q, qm_scr, tq):
    for h, hm in enumerate(_head_masks()):
        qm_scr[h * tq:(h + 1) * tq, :] = jnp.where(hm, q, jnp.zeros_like(q))


def _pick_heads(stacked, tq, scale=None):
    out = None
    for h, hm in enumerate(_head_masks()):
        blk = stacked[h * tq:(h + 1) * tq, :]
        if scale is not None:
            blk = blk * scale[h]
        out = jnp.where(hm, blk, 0.0 if out is None else out)
    return out


def _ada_kernel(c_ref, w_ref, b_ref, o_ref):
    c = c_ref[...]
    ca = c * jax.nn.sigmoid(c)
    o_ref[0] = _dot(ca, w_ref[0], NN, HI) + b_ref[0]


def _ada(c, w_ada, b_ada):
    depth, d, n6 = w_ada.shape
    b = c.shape[0]
    tn = n6 // 4
    return pl.pallas_call(
        _ada_kernel,
        out_shape=jax.ShapeDtypeStruct((depth, b, n6), F32),
        grid=(depth, n6 // tn),
        in_specs=[
            pl.BlockSpec((b, d), lambda l, j: (0, 0)),
            pl.BlockSpec((1, d, tn), lambda l, j: (l, 0, j)),
            pl.BlockSpec((1, 1, tn), lambda l, j: (l, 0, j)),
        ],
        out_specs=pl.BlockSpec((1, b, tn), lambda l, j: (l, 0, j)),
        compiler_params=_params("arbitrary", "arbitrary"),
        name="ada_mod",
    )(c, w_ada, b_ada.reshape(depth, 1, n6))


(W_QA, W_KC, W_KV, W_BQ, W_BK, W_BV, W_CQ, W_CK, W_CV, W_DQ, W_DK, W_DV, W_MISC) = range(13)
N_WSLABS = 13
KV_QK_LANES = 2 * HEAD_DIM
_SLABS = (
    (W_QA, "all", None, F32), (W_QA, "all", "all", BF16), (W_KC, "all", None, F32),
    (W_KV, "kv", "kv", BF16),
    (W_BQ, "all", "all", BF16), (W_BK, "all", "all", BF16), (W_BV, None, None, BF16),
    (W_CQ, None, None, BF16), (W_CK, None, None, BF16), (W_CV, None, None, BF16),
    (W_DQ, "all", None, BF16), (W_DK, "all", None, BF16), (W_DV, None, None, BF16),
    (W_MISC, None, None, F32),
)
N_SLABS = len(_SLABS)
(S_QNR, S_QR, S_KC, S_KV, S_BQ, S_BK, S_BV, S_CQ, S_CK, S_CV, S_DQ, S_DK, S_DV, S_MISC) = range(N_SLABS)
_DIL_SLABS = (S_BQ, S_BK, S_BV)
_DILATIONS = tuple(d for (_, d) in DIL_CONFIGS if d > 1)
KSL_LANE, KW_LANE, VSL_LANE, VW_LANE = (i * HEAD_DIM for i in range(4))


def _proj_kernel(x_ref, g_ref, sc_ref, sh_ref, w_ref, gain_ref, bd_ref, rope_ref, *out_refs):
    h = _mod_norm(x_ref[...], g_ref[...], sc_ref[0], sh_ref[0]).astype(BF16)
    bd = bd_ref[...]
    key_lanes = lax.broadcasted_iota(jnp.int32, (1, W_MIX), 1) < KV_QK_LANES
    last_w, y_n = None, None
    raw = [_dot(h, w_ref[:, wi * W_MIX:(wi + 1) * W_MIX]) for wi in range(N_WSLABS)]
    for s, (wi, norm, rope, _) in enumerate(_SLABS):
        if wi != last_w:
            y_n = raw[wi]
            if norm:
                normed = y_n * lax.rsqrt(_dot_split(y_n * y_n, bd) + EPS)
                y_n = normed if norm == "all" else jnp.where(key_lanes, normed, y_n)
            last_w = wi
        y = y_n * gain_ref[s]
        if rope:
            c, s1, s2 = rope_ref[0, 0], rope_ref[0, 1], rope_ref[0, 2]
            if rope == "kv":
                c, s1, s2 = jnp.where(key_lanes, c, 1.0), jnp.where(key_lanes, s1, 0.0), jnp.where(key_lanes, s2, 0.0)
            y = y * c + pltpu.roll(y, W_MIX - ROPE_HALF, 1) * s1 + pltpu.roll(y, ROPE_HALF, 1) * s2
        out_refs[s][...] = y.astype(out_refs[s].dtype)
        if s in _DIL_SLABS:
            stage_scr = out_refs[-1]
            tt = y.shape[0]
            for c in range(W_MIX // LANES):
                stage_scr[c] = y[:, c * LANES:(c + 1) * LANES]
            for di, dil in enumerate(_DILATIONS):
                o_ref = out_refs[N_SLABS + _DIL_SLABS.index(s) * len(_DILATIONS) + di]
                for r in range(dil):
                    for c in range(W_MIX // LANES):
                        lo = r * W_MIX + c * LANES
                        o_ref[:, lo:lo + LANES] = (
                            stage_scr[c, pl.ds(r, tt // dil, stride=dil), :].astype(o_ref.dtype))


def _proj(x2, g, sc, sh, w_slab, gain, bd, rope, seq, tt):
    n, d = x2.shape
    tpb = seq // tt
    out_shape = [jax.ShapeDtypeStruct((n, W_MIX), dt) for (_, _, _, dt) in _SLABS]
    out_specs = [pl.BlockSpec((tt, W_MIX), lambda i: (i, 0)) for _ in _SLABS]
    for _ in _DIL_SLABS:
        for dil in _DILATIONS:
            out_shape.append(jax.ShapeDtypeStruct((n // dil, dil * W_MIX), BF16))
            out_specs.append(pl.BlockSpec((tt // dil, dil * W_MIX), lambda i: (i, 0)))
    return pl.pallas_call(
        _proj_kernel,
        out_shape=out_shape,
        grid=(n // tt,),
        in_specs=[
            pl.BlockSpec((tt, d), lambda i: (i, 0)),
            pl.BlockSpec((1, d), lambda i: (0, 0)),
            pl.BlockSpec((1, 1, d), lambda i: (i // tpb, 0, 0)),
            pl.BlockSpec((1, 1, d), lambda i: (i // tpb, 0, 0)),
            pl.BlockSpec((d, N_WSLABS * W_MIX), lambda i: (0, 0)),
            pl.BlockSpec((N_SLABS, 1, W_MIX), lambda i: (0, 0, 0)),
            pl.BlockSpec((W_MIX, W_MIX), lambda i: (0, 0)),
            pl.BlockSpec((1, 3, tt, W_MIX), lambda i: (i // tpb, 0, i % tpb, 0)),
        ],
        out_specs=out_specs,
        scratch_shapes=[pltpu.VMEM((W_MIX // LANES, tt, LANES), F32)],
        compiler_params=_params("arbitrary"),
        name="in_proj",
    )(x2, g, sc, sh, w_slab, gain, bd, rope)


def _pack_w_in(w_in, qk_gain):
    d = w_in.shape[0]
    o = 0
    a_q = w_in[:, o:o + W_MIX]; o += W_MIX
    a_kv = w_in[:, o:o + 6 * HEAD_DIM]; o += 6 * HEAD_DIM
    a_g = w_in[:, o:o + 3 * N_HEADS]; o += 3 * N_HEADS
    b_qkv = w_in[:, o:o + 3 * W_MIX]; o += 3 * W_MIX
    c_qkv = w_in[:, o:o + 3 * W_MIX]; o += 3 * W_MIX
    d_qkv = w_in[:, o:o + 3 * W_MIX]; o += 3 * W_MIX
    d_f = w_in[:, o:o + N_HEADS]; o += N_HEADS
    w_merge = w_in[:, o:]
    kc, vc, ksl, vsl, kw, vw = (a_kv[:, i * HEAD_DIM:(i + 1) * HEAD_DIM] for i in range(6))
    zeros = lambda w: jnp.zeros((d, w), w_in.dtype)
    misc = jnp.concatenate([a_g, d_f, zeros(VC_LANE - FOX_LANE - N_HEADS), vc, zeros(W_MIX - 2 * HEAD_DIM)], axis=1)
    slabs = [
        a_q, jnp.concatenate([kc, zeros(W_MIX - HEAD_DIM)], axis=1),
        jnp.concatenate([ksl, kw, vsl, vw], axis=1),
        b_qkv[:, :W_MIX], b_qkv[:, W_MIX:2 * W_MIX], b_qkv[:, 2 * W_MIX:],
        c_qkv[:, :W_MIX], c_qkv[:, W_MIX:2 * W_MIX], c_qkv[:, 2 * W_MIX:],
        d_qkv[:, :W_MIX], d_qkv[:, W_MIX:2 * W_MIX], d_qkv[:, 2 * W_MIX:],
        misc,
    ]
    w_slab = jnp.concatenate(slabs, axis=1).astype(BF16)
    scale = HEAD_DIM ** -0.5
    t4 = lambda gvec: jnp.tile(gvec, N_HEADS)
    one = jnp.ones((W_MIX,), F32)
    scale2 = scale * LOG2E
    gains = [
        t4(qk_gain[0]) * scale, t4(qk_gain[0]) * scale2,
        jnp.concatenate([qk_gain[1], jnp.ones((W_MIX - HEAD_DIM,), F32)]),
        jnp.concatenate([qk_gain[2], qk_gain[3], jnp.ones((W_MIX - KV_QK_LANES,), F32)]),
        t4(qk_gain[4]) * scale2, t4(qk_gain[5]), one,
        one * scale, one, one,
        t4(qk_gain[6]) * scale2, t4(qk_gain[7]), one,
        one,
    ]
    gain = jnp.stack(gains).reshape(N_SLABS, 1, W_MIX).astype(F32)
    return w_slab, gain, w_merge.astype(BF16)


def _rope_tables(positions):
    inv = ROPE_THETA ** (-jnp.arange(0, ROPE_DIMS, 2, dtype=F32) / ROPE_DIMS)
    ang = positions.astype(F32)[..., None] * inv
    cos, sin = jnp.cos(ang), jnp.sin(ang)
    b, s, _ = cos.shape
    pad1 = jnp.ones((b, s, HEAD_DIM - ROPE_DIMS), F32)
    pad0 = jnp.zeros((b, s, HEAD_DIM - ROPE_DIMS), F32)
    z8 = jnp.zeros_like(sin)
    c64 = jnp.concatenate([cos, cos, pad1], axis=-1)
    s1_64 = jnp.concatenate([-sin, z8, pad0], axis=-1)
    s2_64 = jnp.concatenate([z8, sin, pad0], axis=-1)
    heads = lambda t: jnp.tile(t, (1, 1, N_HEADS))
    return jnp.stack([heads(c64), heads(s1_64), heads(s2_64)], axis=1)


def _compress_kernel(a_ref, b_ref, pe_ref, w1_ref, w2_ref, o_ref):
    half = w1_ref.shape[1] // 2
    w1 = w1_ref[0]
    hid = (_dot(a_ref[0, 0], w1[:half], NN, HI) + _dot(b_ref[0, 0], w1[half:], NN, HI)
           + _dot(pe_ref[0], w1, NN, HI))
    o_ref[0, 0] = _dot(jax.nn.gelu(hid), w2_ref[0], NN, HI)


def _compress(ch, chn, pe, w1, w2):
    _, b, ncp, cw = ch.shape
    hid = w1.shape[2]
    return pl.pallas_call(
        _compress_kernel,
        out_shape=jax.ShapeDtypeStruct((2, b, ncp, HEAD_DIM), F32),
        grid=(2, b),
        in_specs=[
            pl.BlockSpec((1, 1, ncp, cw), lambda k, i: (k, i, 0, 0)),
            pl.BlockSpec((1, 1, ncp, cw), lambda k, i: (k, i, 0, 0)),
            pl.BlockSpec((1, 1, 2 * cw), lambda k, i: (k, 0, 0)),
            pl.BlockSpec((1, 2 * cw, hid), lambda k, i: (k, 0, 0)),
            pl.BlockSpec((1, hid, HEAD_DIM), lambda k, i: (k, 0, 0)),
        ],
        out_specs=pl.BlockSpec((1, 1, ncp, HEAD_DIM), lambda k, i: (k, i, 0, 0)),
        compiler_params=_params("arbitrary", "arbitrary"),
        name="nsa_compress",
    )(ch, chn, pe, w1, w2)


def _cmp_kernel(q_ref, kc_ref, vct_ref, ovt_ref, o_ref, sel_ref, qh_scr, ql_scr, acc_scr, *, tq, nc, n_sel):
    i = pl.program_id(1)
    ncp = kc_ref.shape[1]
    q = q_ref[0]
    q_hi = q.astype(BF16)
    _stack_masked(q_hi, qh_scr, tq)
    _stack_masked((q - q_hi.astype(F32)).astype(BF16), ql_scr, tq)
    kc = kc_ref[0]
    k_hi = kc.astype(BF16)
    k_lo = (kc - k_hi.astype(F32)).astype(BF16)
    ss = []
    for h in range(N_HEADS):
        rs = slice(h * tq, (h + 1) * tq)
        ss.append(_dot(k_hi, qh_scr[rs, :], NT) + _dot(k_lo, qh_scr[rs, :], NT)
                  + _dot(k_hi, ql_scr[rs, :], NT))
    t = i * tq + lax.broadcasted_iota(jnp.int32, (1, tq), 1)
    c = lax.broadcasted_iota(jnp.int32, (ncp, 1), 0)
    mask = (c * CMP_STRIDE + (CMP_LEN - 1) <= t) & (c < nc)
    psum = None
    ps = []
    for h in range(N_HEADS):
        sm = jnp.where(mask, ss[h], NEG)
        m = jnp.max(sm, axis=0, keepdims=True)
        e = jnp.where(mask, jnp.exp(sm - m), 0.0)
        l = jnp.sum(e, axis=0, keepdims=True)
        p = e * (1.0 / jnp.maximum(l, 1e-30))
        psum = p if psum is None else psum + p
        ps.append(p.astype(BF16))
    vct = vct_ref[0]
    for h in range(N_HEADS):
        rs = slice(h * HEAD_DIM, (h + 1) * HEAD_DIM)
        acc_scr[rs, :] = _dot(vct[rs, :], ps[h])
    o_ref[0] = acc_scr[...].T.astype(o_ref.dtype)
    imp = _dot(ovt_ref[...], psum, NN, HI)
    j = lax.broadcasted_iota(jnp.int32, (n_sel, 1), 0)
    cur = t >> SEL_SHIFT
    valid = j <= cur
    forced = (j == 0) | (j == cur) | (j == cur - 1)
    picked = valid & forced
    score = jnp.where(valid, jnp.where(forced, -3e38, imp), NEG)
    sel = jnp.where(picked, 1.0, 0.0)
    jf = j.astype(F32)
    for _ in range(min(TOPN, n_sel) - 3):
        mx = jnp.max(score, axis=0, keepdims=True)
        idx = jnp.min(jnp.where(score == mx, jf, float(n_sel)), axis=0, keepdims=True)
        pick = jf == idx
        sel = jnp.where(pick, 1.0, sel)
        score = jnp.where(pick, -3e38, score)
    sel_ref[0] = ((sel - 1.0) * SEL_NEG).T.astype(sel_ref.dtype)


def _cmp_topk(q_nr, kc_rep, vc_rep, overlap, tq, nc):
    b, seq, _ = q_nr.shape
    ncp = kc_rep.shape[1]
    n_sel = seq // SEL_LEN
    return pl.pallas_call(
        functools.partial(_cmp_kernel, tq=tq, nc=nc, n_sel=n_sel),
        out_shape=[jax.ShapeDtypeStruct((b, seq, W_MIX), BF16),
                   jax.ShapeDtypeStruct((b, seq, n_sel), BF16)],
        grid=(b, seq // tq),
        in_specs=[
            pl.BlockSpec((1, tq, W_MIX), lambda g, i: (g, i, 0)),
            pl.BlockSpec((1, ncp, W_MIX), lambda g, i: (g, 0, 0)),
            pl.BlockSpec((1, W_MIX, ncp), lambda g, i: (g, 0, 0)),
            pl.BlockSpec((n_sel, ncp), lambda g, i: (0, 0)),
        ],
        out_specs=[pl.BlockSpec((1, tq, W_MIX), lambda g, i: (g, i, 0)),
                   pl.BlockSpec((1, tq, n_sel), lambda g, i: (g, i, 0))],
        scratch_shapes=[pltpu.VMEM((N_HEADS * tq, W_MIX), BF16), pltpu.VMEM((N_HEADS * tq, W_MIX), BF16),
                        pltpu.VMEM((W_MIX, tq), F32)],
        compiler_params=_params("arbitrary", "arbitrary"),
        name="nsa_cmp_topk",
    )(q_nr, kc_rep, vc_rep, overlap)


def _causal_kernel(*refs, tq, tk, has_sel, has_bias, fixed_max=False):
    it = iter(refs)
    q_ref, k_ref, vt_ref = next(it), next(it), next(it)
    sel_ref = next(it) if has_sel else None
    shift_ref = next(it) if fixed_max else None
    fa_ref, fb_ref, fq_ref, thr_ref = (next(it) for _ in range(4)) if has_bias else (None,) * 4
    o_ref, qm_scr, acc_scr, sa_scr, sb_scr, m_scr, l_scr = (next(it) for _ in range(7))
    i = pl.program_id(1)
    t0 = i * tq
    lane = lax.broadcasted_iota(jnp.int32, (1, W_MIX), 1)
    if has_sel:
        n_sel = sel_ref.shape[2]
        qf = q_ref[0].astype(F32)
        selb = sel_ref[0].astype(F32)
        selb = jnp.concatenate([selb, jnp.zeros((tq, W_MIX - n_sel), F32)], axis=1)
        selb = pltpu.roll(selb, HEAD_DIM, 1)
        if fixed_max:
            selb = selb + shift_ref[...]
        for h in range(N_HEADS):
            rot = qf if h == 0 else pltpu.roll(qf, W_MIX - h * HEAD_DIM, 1)
            qm_scr[h * tq:(h + 1) * tq, :] = jnp.where(lane < HEAD_DIM, rot, selb).astype(BF16)
    else:
        _stack_masked(q_ref[0], qm_scr, tq)
    acc_scr[...] = jnp.zeros(acc_scr.shape, F32)
    t_pos = t0 + lax.broadcasted_iota(jnp.int32, (1, tq), 1)

    def scores(kt, s_buf):
        k_t = k_ref[0, kt]
        if has_sel:
            blk = (kt * tk + lax.broadcasted_iota(jnp.int32, (tk, 1), 0)) >> SEL_SHIFT
            hot = blk == lane - HEAD_DIM
            if fixed_max:
                hot = hot | (lane == SHIFT_LANE)
            one_hot = jnp.where(hot, 1.0, 0.0).astype(BF16)
            k_t = jnp.where(lane < HEAD_DIM, k_t, one_hot)
        for h in range(N_HEADS):
            s = _dot(k_t, qm_scr[h * tq:(h + 1) * tq, :], NT)
            if has_bias:
                s = s + _dot(fa_ref[0, kt], fb_ref[0, h])
            s_buf[h] = s

    def update(kt, s_buf, diag):
        vt_t = vt_ref[0, kt]
        mask = None
        if diag:
            s_pos = kt * tk + lax.broadcasted_iota(jnp.int32, (tk, 1), 0)
            mask = s_pos <= t_pos
        ps, alphas = [], []
        for h in range(N_HEADS):
            s = s_buf[h]
            if mask is not None:
                s = jnp.where(mask, s, NEG)
            if fixed_max:
                p = jnp.exp2(s)
                l_scr[h] += jnp.sum(p, axis=0, keepdims=True)
                ps.append(p.astype(BF16))
                continue
            m_old = m_scr[h]
            m_new = jnp.maximum(m_old, jnp.max(s, axis=0, keepdims=True))
            alpha = jnp.exp2(m_old - m_new)
            m_scr[h] = m_new
            p = jnp.exp2(s - m_new)
            l_scr[h] = alpha * l_scr[h] + jnp.sum(p, axis=0, keepdims=True)
            ps.append(p.astype(BF16))
            alphas.append(alpha)
        for h in range(N_HEADS):
            rs = slice(h * HEAD_DIM, (h + 1) * HEAD_DIM)
            vs = slice(VSL_LANE, VSL_LANE + HEAD_DIM) if has_sel else rs
            if fixed_max:
                acc_scr[rs, :] += _dot(vt_t[vs, :], ps[h])
            else:
                acc_scr[rs, :] = alphas[h] * acc_scr[rs, :] + _dot(vt_t[vs, :], ps[h])

    def live(kt_next):
        lane = lax.broadcasted_iota(jnp.int32, (1, thr_ref.shape[3]), 1)
        hit = None
        for h in range(N_HEADS):
            top = jnp.max(fq_ref[0, h] - m_scr[h], axis=-1, keepdims=True)
            need = top >= thr_ref[0, h]
            hit = need if hit is None else (hit | need)
        return jnp.max(jnp.where(hit & (lane == kt_next), 1.0, 0.0)) > 0.5

    n_last = t0 // tk
    m_scr[...] = jnp.full(m_scr.shape, NEG, F32)
    l_scr[...] = jnp.zeros(l_scr.shape, F32)
    scores(n_last, sa_scr)
    scores(jnp.maximum(n_last - 1, 0), sb_scr)
    update(n_last, sa_scr, True)

    def pair(j):
        kt = n_last - 1 - 2 * j
        scores(kt - 1, sa_scr)
        update(kt, sb_scr, False)
        scores(jnp.maximum(kt - 2, 0), sb_scr)
        update(kt - 1, sa_scr, False)
        return kt - 2

    n_pairs = n_last // 2
    if has_bias:
        def cond(state):
            j, go = state
            return (j < n_pairs) & go

        def body(state):
            j, _ = state
            return j + 1, live(pair(j))

        _, go = lax.while_loop(cond, body, (jnp.int32(0), live(n_last - 1)))
    else:
        lax.fori_loop(0, n_pairs, lambda j, c: (pair(j), c)[1], 0)
        go = True

    @pl.when((n_last % 2 == 1) & go)
    def _():
        update(0, sb_scr, False)

    ls = [l_scr[h] for h in range(N_HEADS)]
    for h in range(N_HEADS):
        rs = slice(h * HEAD_DIM, (h + 1) * HEAD_DIM)
        acc_scr[rs, :] = acc_scr[rs, :] / ls[h]
    o_ref[0] = acc_scr[...].T.astype(o_ref.dtype)


def _split3(x):
    def cut(v):
        bits = lax.bitcast_convert_type(v, jnp.uint32) & jnp.uint32(0xFFFF0000)
        return lax.bitcast_convert_type(bits, F32)

    hi = cut(x)
    r1 = x - hi
    mid = cut(r1)
    lo = r1 - mid
    return hi.astype(BF16), mid.astype(BF16), lo.astype(BF16)


def _causal_attn(q, k, v, tq, tk, sel=None, fcum=None, qk_bound=None, sel_bound=None):
    b, s, _ = q.shape
    assert tq <= tk
    nk = s // tk
    rows = N_HEADS * tq
    n_s = N_HEADS
    vt = v.reshape(b, nk, tk, W_MIX).transpose(0, 1, 3, 2)
    args = [q, k.reshape(b, nk, tk, W_MIX), vt]
    in_specs = [
        pl.BlockSpec((1, tq, W_MIX), lambda a, i: (a, i, 0)),
        pl.BlockSpec((1, nk, tk, W_MIX), lambda a, i: (a, 0, 0, 0)),
        pl.BlockSpec((1, nk, W_MIX, tk), lambda a, i: (a, 0, 0, 0)),
    ]
    if sel is not None:
        args.append(sel)
        in_specs.append(pl.BlockSpec((1, tq, sel.shape[2]), lambda a, i: (a, i, 0)))
    if fcum is not None:
        fcum = fcum * LOG2E
        qk_bound = qk_bound * LOG2E
        nf = 2 * 3 + 2
        ones = jnp.ones(fcum.shape, BF16)
        zero = jnp.zeros(fcum.shape, BF16)
        parts = _split3(fcum)
        key_f = jnp.stack([ones, ones, ones] + [-p for p in parts] + [zero, zero], axis=-1)
        qry_f = jnp.stack(list(parts) + [ones, ones, ones, zero, zero], axis=-1)
        fa = key_f.reshape(b, nk, tk, N_HEADS * nf)
        eye = jnp.eye(N_HEADS, dtype=BF16)
        fb = jnp.einsum('bshf,hg->bgshf', qry_f, eye).reshape(b, N_HEADS, s, N_HEADS * nf).transpose(0, 1, 3, 2)
        f_rows = fcum.transpose(0, 2, 1)
        f_end = f_rows[:, :, tk - 1::tk]
        thr = jnp.pad(f_end - (qk_bound + FOX_STOP * LOG2E), ((0, 0), (0, 0), (0, LANES - nk)),
                      constant_values=BIG).reshape(b, N_HEADS, 1, LANES)
        args += [fa, fb, f_rows.reshape(b, N_HEADS, 1, s), thr]
        in_specs += [pl.BlockSpec((1, nk, tk, N_HEADS * nf), lambda a, i: (a, 0, 0, 0)),
                     pl.BlockSpec((1, N_HEADS, N_HEADS * nf, tq), lambda a, i: (a, 0, 0, i)),
                     pl.BlockSpec((1, N_HEADS, 1, tq), lambda a, i: (a, 0, 0, i)),
                     pl.BlockSpec((1, N_HEADS, 1, LANES), lambda a, i: (a, 0, 0, 0))]
    def call(call_args, call_specs, fixed_max):
        return pl.pallas_call(
            functools.partial(_causal_kernel, tq=tq, tk=tk, has_sel=sel is not None,
                              has_bias=fcum is not None, fixed_max=fixed_max),
            out_shape=jax.ShapeDtypeStruct((b, s, W_MIX), BF16),
            grid=(b, s // tq),
            in_specs=call_specs,
            out_specs=pl.BlockSpec((1, tq, W_MIX), lambda a, i: (a, i, 0)),
            scratch_shapes=[pltpu.VMEM((rows, W_MIX), BF16), pltpu.VMEM((W_MIX, tq), F32),
                            pltpu.VMEM((n_s, tk, tq), F32), pltpu.VMEM((n_s, tk, tq), F32),
                            pltpu.VMEM((N_HEADS, 1, tq), F32), pltpu.VMEM((N_HEADS, 1, tq), F32)],
            compiler_params=_params("arbitrary", "arbitrary"),
            name="causal_sel%d_bias%d_fixed%d" % (sel is not None, fcum is not None, fixed_max),
        )(*call_args)

    if sel_bound is None:
        return call(args, in_specs, False)
    shift = jnp.zeros((1, W_MIX), F32).at[0, SHIFT_LANE].set(-sel_bound)
    shift_spec = pl.BlockSpec((1, W_MIX), lambda a, i: (0, 0))
    return lax.cond(sel_bound <= FIXED_MAX_LIMIT,
                    lambda: call(args + [shift], in_specs + [shift_spec], True),
                    lambda: call(args, in_specs, False))


def _window_kernel(*refs, tq, wk, pad, window, ls, emit_lse, other_dils, kv_lanes):
    it = iter(refs)
    q_ref, k_ref, v_ref = next(it), next(it), next(it)
    n_other = len(other_dils)
    others_in = [(next(it), next(it)) for _ in range(n_other)]
    wexp_ref = next(it) if n_other else None
    o_ref = next(it)
    lse_ref = next(it) if emit_lse else None
    qm_scr, acc_scr = next(it), next(it)
    others = []
    if n_other:
        og_scr, lg_scr = next(it), next(it)
        for g, (dg, (og_ref, lg_ref)) in enumerate(zip(other_dils, others_in)):
            for r in range(dg):
                rows = pl.ds(r, tq // dg, stride=dg)
                for c in range(W_MIX // LANES):
                    lo = r * W_MIX + c * LANES
                    og_scr[g, c, rows, :] = og_ref[0, :, lo:lo + LANES]
                lg_scr[g, rows, :] = lg_ref[0, :, r * LANES:(r + 1) * LANES]
            others.append((og_scr.at[g], lg_scr.at[g]))
    t0 = pl.program_id(2) * tq
    start = pl.multiple_of(jnp.clip(t0 - pad, 0, ls - wk), LANES)
    if kv_lanes is None:
        _stack_masked(q_ref[0], qm_scr, tq)
    else:
        lane = lax.broadcasted_iota(jnp.int32, (1, W_MIX), 1)
        on_key = (lane >= kv_lanes[0]) & (lane < kv_lanes[0] + HEAD_DIM)
        qf = q_ref[0].astype(F32)
        for h in range(N_HEADS):
            shift = (kv_lanes[0] - h * HEAD_DIM) % W_MIX
            rot = pltpu.roll(qf, shift, 1) if shift else qf
            qm_scr[h * tq:(h + 1) * tq, :] = jnp.where(on_key, rot, 0.0).astype(BF16)
    k_w = k_ref[0, pl.ds(start, wk), :]
    vt_w = v_ref[0, pl.ds(start, wk), :].astype(F32).T.astype(BF16)
    ss = [_dot(k_w, qm_scr[h * tq:(h + 1) * tq, :], NT) for h in range(N_HEADS)]
    t_pos = t0 + lax.broadcasted_iota(jnp.int32, (1, tq), 1)
    s_pos = start + lax.broadcasted_iota(jnp.int32, (wk, 1), 0)
    mask = (s_pos <= t_pos) & (t_pos - s_pos < window)
    ps, inv_ls, lses = [], [], []
    for h in range(N_HEADS):
        s = jnp.where(mask, ss[h], NEG)
        m = jnp.max(s, axis=0, keepdims=True)
        p = jnp.exp2(s - m)
        l = jnp.sum(p, axis=0, keepdims=True)
        ps.append(p.astype(BF16))
        inv_ls.append(1.0 / l)
        lses.append(m + jnp.log2(l))
    for h in range(N_HEADS):
        rs = slice(h * HEAD_DIM, (h + 1) * HEAD_DIM)
        vs = rs if kv_lanes is None else slice(kv_lanes[1], kv_lanes[1] + HEAD_DIM)
        acc_scr[rs, :] = _dot(vt_w[vs, :], ps[h]) * inv_ls[h]
    o_self = acc_scr[...].T
    if emit_lse or n_other:
        row = lax.broadcasted_iota(jnp.int32, (LANES, 1), 0)
        stat = jnp.zeros((LANES, tq), F32)
        for h in range(N_HEADS):
            stat = jnp.where(row == h, lses[h], stat)
        lse_tile = stat.T
    if n_other:
        lg_t = [lg[...].T for (_, lg) in others]
        row = lax.broadcasted_iota(jnp.int32, (LANES, 1), 0)
        wmat = jnp.zeros((LANES, tq), F32)
        for h in range(N_HEADS):
            group_lse = [lses[h]] + [t[h:h + 1, :] for t in lg_t]
            top = functools.reduce(jnp.maximum, group_lse)
            ws = [jnp.exp2(x - top) for x in group_lse]
            inv_den = 1.0 / functools.reduce(jnp.add, ws)
            for g, w in enumerate(ws):
                wmat = jnp.where(row == g * N_HEADS + h, w * inv_den, wmat)
        wt = wmat.T
        groups = [o_self] + [jnp.concatenate([og[c] for c in range(W_MIX // LANES)], axis=1)
                             for (og, _) in others]
        out = None
        for g, o_g in enumerate(groups):
            term = _dot_split(wt, wexp_ref[g]) * o_g
            out = term if out is None else out + term
        o_ref[0] = out.astype(o_ref.dtype)
    else:
        o_ref[0] = o_self.astype(o_ref.dtype)
    if emit_lse:
        lse_ref[0] = lse_tile


def _window_attn(q, k, v, *, dil, window, tq, out_dtype, emit_lse=False, others=(), kv_lanes=None):
    b, ls, _ = q.shape
    tq = min(tq, ls)
    pad = -(-(window - 1) // LANES) * LANES
    wk = min(tq + pad, ls)
    rows = N_HEADS * tq
    args = [q, k, v]
    in_specs = [
        pl.BlockSpec((1, tq, W_MIX), lambda a, r, i: (a, i, r)),
        pl.BlockSpec((1, ls, W_MIX), lambda a, r, i: (a, 0, r)),
        pl.BlockSpec((1, ls, W_MIX), lambda a, r, i: (a, 0, r)),
    ]
    scratch = [pltpu.VMEM((rows, W_MIX), BF16), pltpu.VMEM((W_MIX, tq), F32)]
    for (o_g, lse_g, dg) in others:
        args += [o_g, lse_g]
        in_specs += [pl.BlockSpec((1, tq // dg, dg * W_MIX), lambda a, r, i: (a, i, 0)),
                     pl.BlockSpec((1, tq // dg, dg * LANES), lambda a, r, i: (a, i, 0))]
    if others:
        wexp = np.zeros((len(others) + 1, LANES, W_MIX), np.float32)
        for g in range(len(others) + 1):
            for h in range(N_HEADS):
                wexp[g, g * N_HEADS + h, h * HEAD_DIM:(h + 1) * HEAD_DIM] = 1.0
        args.append(jnp.asarray(wexp, BF16))
        in_specs.append(pl.BlockSpec(wexp.shape, lambda a, r, i: (0, 0, 0)))
        scratch += [pltpu.VMEM((len(others), W_MIX // LANES, tq, LANES), F32),
                    pltpu.VMEM((len(others), tq, LANES), F32)]
    out_shape = [jax.ShapeDtypeStruct((b, ls, dil * W_MIX), out_dtype)]
    out_specs = [pl.BlockSpec((1, tq, W_MIX), lambda a, r, i: (a, i, r))]
    if emit_lse:
        out_shape.append(jax.ShapeDtypeStruct((b, ls, dil * LANES), F32))
        out_specs.append(pl.BlockSpec((1, tq, LANES), lambda a, r, i: (a, i, r)))
    res = pl.pallas_call(
        functools.partial(_window_kernel, tq=tq, wk=wk, pad=pad, window=window, ls=ls,
                          emit_lse=emit_lse, other_dils=tuple(dg for (_, _, dg) in others),
                          kv_lanes=kv_lanes),
        out_shape=out_shape,
        grid=(b, dil, ls // tq),
        in_specs=in_specs,
        out_specs=out_specs,
        scratch_shapes=scratch,
        compiler_params=_params("arbitrary", "arbitrary", "arbitrary"),
        name="window_d%d_w%d" % (dil, window),
    )(*args)
    return (res[0], res[1], dil) if emit_lse else res[0]


def _sb_kernel(q_ref, k_ref, vt_ref, tri_ref, o_ref, qm_scr, carry_scr, acc_scr, *, tq, tk):
    i = pl.program_id(1)
    t0 = i * tq
    _stack_masked(q_ref[0], qm_scr, tq)
    carry_scr[...] = jnp.zeros(carry_scr.shape, F32)
    acc_scr[...] = jnp.zeros(acc_scr.shape, F32)
    t_pos = t0 + lax.broadcasted_iota(jnp.int32, (1, tq), 1)

    def tile(kt, diag):
        k_t = k_ref[0, kt]
        vt_t = vt_ref[0, kt]
        tri = tri_ref[...]
        zs = [_dot(k_t, qm_scr[h * tq:(h + 1) * tq, :], NT) for h in range(N_HEADS)]
        if diag:
            s_pos = kt * tk + lax.broadcasted_iota(jnp.int32, (tk, 1), 0)
            strict = s_pos < t_pos
        weights, his, los = [], [], []
        for h in range(N_HEADS):
            z = zs[h]
            lg = -(jnp.maximum(z, 0.0) + jnp.log(1.0 + jnp.exp(-jnp.abs(z))))
            if diag:
                lg = jnp.where(strict, lg, 0.0)
            hi = lg.astype(BF16)
            his.append(hi)
            los.append((lg - hi.astype(F32)).astype(BF16))
        cums = [_dot(tri, his[h]) + _dot(tri, los[h]) for h in range(N_HEADS)]
        for h in range(N_HEADS):
            a = jnp.exp(zs[h] + cums[h] + carry_scr[h])
            if diag:
                a = jnp.where(strict, a, 0.0)
            weights.append(a.astype(BF16))
            carry_scr[h] += cums[h][0:1, :]
        for h in range(N_HEADS):
            rs = slice(h * HEAD_DIM, (h + 1) * HEAD_DIM)
            acc_scr[rs, :] += _dot(vt_t[rs, :], weights[h])

    n_diag = tq // tk
    for d in reversed(range(n_diag)):
        tile(i * n_diag + d, True)
    n_below = i * n_diag

    def cond(state):
        j, top = state
        return (j < n_below) & (top > SB_STOP)

    def body(state):
        j, _ = state
        tile(n_below - 1 - j, False)
        return j + 1, jnp.max(carry_scr[...])

    lax.while_loop(cond, body, (jnp.int32(0), jnp.max(carry_scr[...])))
    o_ref[0] = acc_scr[...].T.astype(o_ref.dtype)


def _stick_breaking(q, k, v, tq, tk):
    b, s, _ = q.shape
    assert tq % tk == 0
    nk = s // tk
    rows = N_HEADS * tq
    tri = jnp.asarray(np.triu(np.ones((tk, tk), np.float32)), BF16)
    vt = v.reshape(b, nk, tk, W_MIX).transpose(0, 1, 3, 2)
    return pl.pallas_call(
        functools.partial(_sb_kernel, tq=tq, tk=tk),
        out_shape=jax.ShapeDtypeStruct((b, s, W_MIX), BF16),
        grid=(b, s // tq),
        in_specs=[
            pl.BlockSpec((1, tq, W_MIX), lambda a, i: (a, i, 0)),
            pl.BlockSpec((1, nk, tk, W_MIX), lambda a, i: (a, 0, 0, 0)),
            pl.BlockSpec((1, nk, W_MIX, tk), lambda a, i: (a, 0, 0, 0)),
            pl.BlockSpec((tk, tk), lambda a, i: (0, 0)),
        ],
        out_specs=pl.BlockSpec((1, tq, W_MIX), lambda a, i: (a, i, 0)),
        scratch_shapes=[pltpu.VMEM((rows, W_MIX), BF16), pltpu.VMEM((N_HEADS, 1, tq), F32),
                        pltpu.VMEM((W_MIX, tq), F32)],
        compiler_params=_params("arbitrary", "arbitrary"),
        name="stick_breaking",
    )(q, k.reshape(b, nk, tk, W_MIX), vt, tri)


def _foxcum_kernel(x_ref, b_ref, tri_ref, o_ref, carry_scr):
    @pl.when(pl.program_id(1) == 0)
    def _():
        carry_scr[...] = jnp.zeros(carry_scr.shape, F32)

    z = x_ref[0] + b_ref[...]
    logf = jnp.minimum(z, 0.0) - jnp.log(1.0 + jnp.exp(-jnp.abs(z)))
    tri = tri_ref[...]
    hi = logf.astype(BF16)
    r1 = logf - hi.astype(F32)
    mid = r1.astype(BF16)
    lo = (r1 - mid.astype(F32)).astype(BF16)
    cum = _dot(tri, hi) + _dot(tri, mid) + _dot(tri, lo) + carry_scr[...]
    o_ref[0] = cum
    carry_scr[...] = cum[cum.shape[0] - 1:, :]


def _fox_cumsum(misc, bias_vec, tc):
    b, s, w = misc.shape
    tri = jnp.asarray(np.tril(np.ones((tc, tc), np.float32)), BF16)
    return pl.pallas_call(
        _foxcum_kernel,
        out_shape=jax.ShapeDtypeStruct((b, s, w), F32),
        grid=(b, s // tc),
        in_specs=[
            pl.BlockSpec((1, tc, w), lambda a, i: (a, i, 0)),
            pl.BlockSpec((1, w), lambda a, i: (0, 0)),
            pl.BlockSpec((tc, tc), lambda a, i: (0, 0)),
        ],
        out_specs=pl.BlockSpec((1, tc, w), lambda a, i: (a, i, 0)),
        scratch_shapes=[pltpu.VMEM((1, w), F32)],
        compiler_params=_params("arbitrary", "arbitrary"),
        name="fox_cumsum",
    )(misc, bias_vec, tri)


def _merge_kernel(x_ref, g_ref, sc_ref, sh_ref, ga_ref, wm_ref, misc_ref, pg_ref,
                  ocmp_ref, osel_ref, owin_ref, ob_ref, oc_ref, od_ref,
                  wa_ref, wb_ref, wc_ref, wd_ref, wo_ref, o_ref):
    x = x_ref[...]
    d = x.shape[1]
    h = _mod_norm(x, g_ref[...], sc_ref[0], sh_ref[0]).astype(BF16)
    gate = jax.nn.sigmoid(misc_ref[...])
    o_a = (_dot_split(gate, pg_ref[0]) * ocmp_ref[...].astype(F32)
           + _dot_split(gate, pg_ref[1]) * osel_ref[...].astype(F32)
           + _dot_split(gate, pg_ref[2]) * owin_ref[...].astype(F32)).astype(BF16)
    mixed = jnp.zeros(x.shape, F32)
    for m, (o_m, w_ref) in enumerate(((o_a, wa_ref), (ob_ref[...], wb_ref),
                                      (oc_ref[...], wc_ref), (od_ref[...], wd_ref))):
        y = _dot(o_m, w_ref[...])
        gl = _dot(h, wm_ref[:, m * d:(m + 1) * d])
        mixed = mixed + jax.nn.sigmoid(gl) * y
    o_ref[...] = x + ga_ref[0] * _dot(mixed.astype(BF16), wo_ref[...])


def _merge(x2, g, sc, sh, ga, w_merge, misc, pg, o_cmp, o_sel, o_win, o_b, o_c, o_d,
           wa, wb, wc, wd, wo, seq, tt):
    n, d = x2.shape
    tpb = seq // tt
    row = lambda w: pl.BlockSpec((tt, w), lambda i: (i, 0))
    full = lambda a: pl.BlockSpec(a.shape, lambda i: (0,) * a.ndim)
    per_b = pl.BlockSpec((1, 1, d), lambda i: (i // tpb, 0, 0))
    return pl.pallas_call(
        _merge_kernel,
        out_shape=jax.ShapeDtypeStruct((n, d), F32),
        grid=(n // tt,),
        in_specs=[row(d), full(g), per_b, per_b, per_b, full(w_merge), row(W_MIX), full(pg)]
        + [row(W_MIX)] * 6 + [full(wa), full(wb), full(wc), full(wd), full(wo)],
        out_specs=row(d),
        compiler_params=_params("arbitrary"),
        name="merge_out",
    )(x2, g, sc, sh, ga, w_merge, misc, pg, o_cmp, o_sel, o_win, o_b, o_c, o_d, wa, wb, wc, wd, wo)


def _ffn_kernel(x_ref, g_ref, sc_ref, sh_ref, gf_ref, w1_ref, w3_ref, w2_ref, o_ref, h_scr, acc_scr):
    f = pl.program_id(1)

    @pl.when(f == 0)
    def _():
        h_scr[...] = _mod_norm(x_ref[...], g_ref[...], sc_ref[0], sh_ref[0]).astype(BF16)
        acc_scr[...] = jnp.zeros(acc_scr.shape, F32)

    h = h_scr[...]
    a = _dot(h, w1_ref[...])
    b = _dot(h, w3_ref[...])
    acc_scr[...] += _dot((a * jax.nn.sigmoid(a) * b).astype(BF16), w2_ref[...])

    @pl.when(f == pl.num_programs(1) - 1)
    def _():
        o_ref[...] = x_ref[...] + gf_ref[0] * acc_scr[...]


def _ffn(x2, g, sc, sh, gf, w1, w3, w2, seq, tt, tf):
    n, d = x2.shape
    dff = w1.shape[1]
    tpb = seq // tt
    per_b = pl.BlockSpec((1, 1, d), lambda i, f: (i // tpb, 0, 0))
    return pl.pallas_call(
        _ffn_kernel,
        out_shape=jax.ShapeDtypeStruct((n, d), F32),
        grid=(n // tt, dff // tf),
        in_specs=[
            pl.BlockSpec((tt, d), lambda i, f: (i, 0)),
            pl.BlockSpec((1, d), lambda i, f: (0, 0)),
            per_b, per_b, per_b,
            pl.BlockSpec((d, tf), lambda i, f: (0, f)),
            pl.BlockSpec((d, tf), lambda i, f: (0, f)),
            pl.BlockSpec((tf, d), lambda i, f: (f, 0)),
        ],
        out_specs=pl.BlockSpec((tt, d), lambda i, f: (i, 0)),
        scratch_shapes=[pltpu.VMEM((tt, d), BF16), pltpu.VMEM((tt, d), F32)],
        compiler_params=_params("arbitrary", "arbitrary"),
        name="ffn_swiglu",
    )(x2, g, sc, sh, gf, w1, w3, w2)


def _route_kernel(x_ref, g_ref, sc_ref, sh_ref, rw_ref, up_ref, h_ref, rank_ref, gate_ref, cnt_ref):
    hf = _mod_norm(x_ref[...], g_ref[...], sc_ref[0], sh_ref[0])
    h_ref[...] = hf.astype(BF16)
    logits = _dot(rw_ref[...], hf, NT, HI)
    ne, tt = logits.shape
    e_idx = lax.broadcasted_iota(jnp.int32, (ne, 1), 0).astype(F32)
    v1 = jnp.max(logits, axis=0, keepdims=True)
    i1 = jnp.min(jnp.where(logits == v1, e_idx, float(ne)), axis=0, keepdims=True)
    m1 = e_idx == i1
    rest = jnp.where(m1, -3e38, logits)
    v2 = jnp.max(rest, axis=0, keepdims=True)
    i2 = jnp.min(jnp.where(rest == v2, e_idx, float(ne)), axis=0, keepdims=True)
    m2 = e_idx == i2
    e2 = jnp.exp(v2 - v1)
    g1 = 1.0 / (1.0 + e2)
    g2 = e2 / (1.0 + e2)
    routed = m1 | m2
    rf = jnp.where(routed, 1.0, 0.0)
    rank = _dot(rf.astype(BF16), up_ref[...])
    rank = jnp.where(routed, rank, -1.0)
    gate = jnp.where(m1, g1, 0.0) + jnp.where(m2, g2, 0.0)
    for e in range(ne):
        rank_ref[0, e] = rank[e:e + 1, :]
        gate_ref[0, e] = gate[e:e + 1, :]
    cnt = jnp.sum(rf, axis=1, keepdims=True)
    cnt_ref[0] = jnp.broadcast_to(cnt, (ne, LANES))


def _route(x2, g, sc, sh, rw_t, seq, tt):
    n, d = x2.shape
    ne = rw_t.shape[0]
    tpb = seq // tt
    nt = n // tt
    upper = jnp.asarray(np.triu(np.ones((tt, tt), np.float32), 1), BF16)
    per_b = pl.BlockSpec((1, 1, d), lambda i: (i // tpb, 0, 0))
    return pl.pallas_call(
        _route_kernel,
        out_shape=[jax.ShapeDtypeStruct((n, d), BF16),
                   jax.ShapeDtypeStruct((nt, ne, 1, tt), F32),
                   jax.ShapeDtypeStruct((nt, ne, 1, tt), F32),
                   jax.ShapeDtypeStruct((nt, ne, LANES), F32)],
        grid=(nt,),
        in_specs=[
            pl.BlockSpec((tt, d), lambda i: (i, 0)),
            pl.BlockSpec((1, d), lambda i: (0, 0)),
            per_b, per_b,
            pl.BlockSpec((ne, d), lambda i: (0, 0)),
            pl.BlockSpec((tt, tt), lambda i: (0, 0)),
        ],
        out_specs=[pl.BlockSpec((tt, d), lambda i: (i, 0)),
                   pl.BlockSpec((1, ne, 1, tt), lambda i: (i, 0, 0, 0)),
                   pl.BlockSpec((1, ne, 1, tt), lambda i: (i, 0, 0, 0)),
                   pl.BlockSpec((1, ne, LANES), lambda i: (i, 0, 0))],
        compiler_params=_params("arbitrary"),
        name="moe_route",
    )(x2, g, sc, sh, rw_t, upper)


def _moe_kernel(cnt_ref, x_ref, gf_ref, h_ref, rank_ref, gate_ref, w1_ref, w3_ref, w2_ref,
                o_ref, acc_scr, xs_scr, y_scr, *, chunk):
    i, e, f = pl.program_id(0), pl.program_id(1), pl.program_id(2)
    ne, nf = pl.num_programs(1), pl.num_programs(2)

    @pl.when((e == 0) & (f == 0))
    def _():
        acc_scr[...] = jnp.zeros(acc_scr.shape, F32)

    count = cnt_ref[i * ne + e]
    rank = rank_ref[0, 0]
    gate = gate_ref[0, 0]
    n_small = (count + chunk - 1) // chunk
    n_big = (count + 2 * chunk - 1) // (2 * chunk)

    def one_hot(c, rows):
        r = c * rows + lax.broadcasted_iota(jnp.int32, (rows, 1), 0)
        return rank == r.astype(F32)

    def rows_of(c, rows):
        return pl.ds(pl.multiple_of(c * rows, rows), rows)

    @pl.when(f == 0)
    def _():
        h = h_ref[...]

        def gather(c, carry):
            p = jnp.where(one_hot(c, chunk), 1.0, 0.0).astype(BF16)
            xs_scr[rows_of(c, chunk), :] = _dot(p, h).astype(BF16)
            return carry

        lax.fori_loop(0, n_small, gather, 0)

        def clear(c, carry):
            y_scr[rows_of(c, 2 * chunk), :] = jnp.zeros((2 * chunk, y_scr.shape[1]), F32)
            return carry

        lax.fori_loop(0, n_big, clear, 0)

    def expert(c, carry):
        xs = xs_scr[rows_of(c, chunk), :]
        a = _dot(xs, w1_ref[0])
        b = _dot(xs, w3_ref[0])
        y_scr[rows_of(c, chunk), :] += _dot((a * jax.nn.sigmoid(a) * b).astype(BF16), w2_ref[0])
        return carry

    lax.fori_loop(0, n_small, expert, 0)

    @pl.when(f == nf - 1)
    def _():
        def scatter(c, carry):
            hit = one_hot(c, 2 * chunk)
            p = jnp.where(hit, 1.0, 0.0).astype(BF16)
            gcol = jnp.sum(jnp.where(hit, gate, 0.0), axis=-1, keepdims=True)
            acc_scr[...] += _dot(p, (y_scr[rows_of(c, 2 * chunk), :] * gcol).astype(BF16), TN)
            return carry

        lax.fori_loop(0, n_big, scatter, 0)

    @pl.when((e == ne - 1) & (f == nf - 1))
    def _():
        o_ref[...] = x_ref[...] + gf_ref[0] * acc_scr[...]


def _moe(counts, x2, gf, h2, rank, gate, w1, w3, w2, seq, tt, tf, chunk):
    n, d = x2.shape
    ne, _, dff = w1.shape
    tpb = seq // tt
    grid_spec = pltpu.PrefetchScalarGridSpec(
        num_scalar_prefetch=1,
        grid=(n // tt, ne, dff // tf),
        in_specs=[
            pl.BlockSpec((tt, d), lambda i, e, f, c: (i, 0)),
            pl.BlockSpec((1, 1, d), lambda i, e, f, c: (i // tpb, 0, 0)),
            pl.BlockSpec((tt, d), lambda i, e, f, c: (i, 0)),
            pl.BlockSpec((1, 1, 1, tt), lambda i, e, f, c: (i, e, 0, 0)),
            pl.BlockSpec((1, 1, 1, tt), lambda i, e, f, c: (i, e, 0, 0)),
            pl.BlockSpec((1, d, tf), lambda i, e, f, c: (e, 0, f)),
            pl.BlockSpec((1, d, tf), lambda i, e, f, c: (e, 0, f)),
            pl.BlockSpec((1, tf, d), lambda i, e, f, c: (e, f, 0)),
        ],
        out_specs=pl.BlockSpec((tt, d), lambda i, e, f, c: (i, 0)),
        scratch_shapes=[pltpu.VMEM((tt, d), F32), pltpu.VMEM((tt, d), BF16), pltpu.VMEM((tt, d), F32)],
    )
    return pl.pallas_call(
        functools.partial(_moe_kernel, chunk=chunk),
        out_shape=jax.ShapeDtypeStruct((n, d), F32),
        grid_spec=grid_spec,
        compiler_params=_params("arbitrary", "arbitrary", "arbitrary"),
        name="moe_experts",
    )(counts, x2, gf, h2, rank, gate, w1, w3, w2)


def _overlap_matrix(ncp, nc, n_sel):
    c0 = np.arange(ncp) * CMP_STRIDE
    c1 = c0 + CMP_LEN
    s0 = np.arange(n_sel) * SEL_LEN
    s1 = s0 + SEL_LEN
    ov = ((c0[:, None] < s1[None, :]) & (c1[:, None] > s0[None, :])).astype(np.float32)
    ov[nc:] = 0.0
    return jnp.asarray(ov.T)


def _gate_expand():
    pg = np.zeros((3, W_MIX, W_MIX), np.float32)
    for br in range(3):
        for h in range(N_HEADS):
            pg[br, GATE_LANE + 3 * h + br, h * HEAD_DIM:(h + 1) * HEAD_DIM] = 1.0
    return jnp.asarray(pg, BF16)


def _mixer_layer(x2, b, s, mod, norm_g, rope, w_in, qk_gain, pe_k, pe_v, ck1, ck2, cv1, cv2,
                 fox_b, w_branch, w_out):
    n, d = x2.shape
    sh_a, sc_a, g_a = mod[0], mod[1], mod[2]
    w_slab, gain, w_merge = _pack_w_in(w_in, qk_gain)
    bd = jnp.asarray(np.kron(np.eye(N_HEADS), np.full((HEAD_DIM, HEAD_DIM), 1.0 / HEAD_DIM)), BF16)
    tiles = _tiles(s)
    outs = _proj(x2, norm_g, sc_a, sh_a, w_slab, gain, bd, rope, s, tiles.proj)
    sl = [a.reshape(b, s, W_MIX) for a in outs[:N_SLABS]]
    dil_in = {1: (sl[S_BQ], sl[S_BK], sl[S_BV])}
    for di, dil in enumerate(_DILATIONS):
        dil_in[dil] = tuple(outs[N_SLABS + si * len(_DILATIONS) + di].reshape(b, s // dil, dil * W_MIX)
                            for si in range(len(_DIL_SLABS)))
    misc = sl[S_MISC]

    nch = s // CMP_STRIDE
    nc = nch - CMP_LEN // CMP_STRIDE + 1
    kc_raw = sl[S_KC][..., :HEAD_DIM]
    vc_raw = misc[..., VC_LANE:VC_LANE + HEAD_DIM]
    chunks = jnp.stack([kc_raw, vc_raw]).reshape(2, b, nch, CMP_STRIDE * HEAD_DIM)
    chunks_next = jnp.concatenate([chunks[:, :, 1:], jnp.zeros_like(chunks[:, :, :1])], axis=2)
    pe = jnp.stack([pe_k, pe_v]).reshape(2, 1, CMP_LEN * HEAD_DIM)
    kvc = jnp.tile(_compress(chunks, chunks_next, pe, jnp.stack([ck1, cv1]), jnp.stack([ck2, cv2])),
                   (1, 1, 1, N_HEADS))
    overlap = _overlap_matrix(nch, nc, s // SEL_LEN)
    o_cmp, selmask = _cmp_topk(sl[S_QNR], kvc[0], kvc[1].transpose(0, 2, 1).astype(BF16), overlap,
                               tiles.causal_q, nc)
    tq, tk = tiles.causal_q, tiles.causal_k
    sel_bound = (1.02 * LOG2E * HEAD_DIM ** 0.5 * jnp.max(jnp.abs(qk_gain[0])) * jnp.max(jnp.abs(qk_gain[2]))
                 + 0.1)
    o_sel = _causal_attn(sl[S_QR], sl[S_KV], sl[S_KV], tq, tk, sel=selmask, sel_bound=sel_bound)
    o_win = _window_attn(sl[S_QR], sl[S_KV], sl[S_KV], dil=1, window=NSA_WINDOW, tq=tiles.window_q,
                         out_dtype=BF16,
                         kv_lanes=(KW_LANE, VW_LANE))

    others = []
    for (wdw, dil) in DIL_CONFIGS[:0:-1]:
        others.append(_window_attn(*dil_in[dil], dil=dil, window=wdw // dil + 1,
                                   tq=s // dil if s // dil <= tiles.short_seq else tiles.window_q,
                                   out_dtype=F32, emit_lse=True))
    wdw, dil = DIL_CONFIGS[0]
    o_b = _window_attn(*dil_in[dil], dil=dil, window=wdw // dil + 1, tq=tiles.window_q, out_dtype=BF16,
                       others=others)

    o_c = _stick_breaking(sl[S_CQ], sl[S_CK], sl[S_CV], tiles.sb, tiles.sb)

    bias_vec = jnp.zeros((1, W_MIX), F32).at[0, FOX_LANE:FOX_LANE + N_HEADS].set(fox_b)
    fcum = _fox_cumsum(misc, bias_vec, tiles.cum)
    qk_bound = 1.02 * HEAD_DIM ** 0.5 * jnp.max(jnp.abs(qk_gain[6])) * jnp.max(jnp.abs(qk_gain[7])) + 0.05
    o_d = _causal_attn(sl[S_DQ], sl[S_DK], sl[S_DV], tq, tk, fcum=fcum[..., FOX_LANE:FOX_LANE + N_HEADS],
                       qk_bound=qk_bound)

    wb16 = w_branch.astype(BF16)
    flat = lambda a: a.reshape(n, W_MIX)
    return _merge(x2, norm_g, sc_a, sh_a, g_a, w_merge, flat(misc), _gate_expand(),
                  flat(o_cmp), flat(o_sel), flat(o_win), flat(o_b), flat(o_c), flat(o_d),
                  wb16[0], wb16[1], wb16[2], wb16[3], w_out.astype(BF16), s, tiles.proj)


def kernel(x, c, positions, w_ada, b_ada, norm_mix, norm_ffn, w_in, qk_gain, nsa_pe_k, nsa_pe_v,
           nsa_ck_w1, nsa_ck_w2, nsa_cv_w1, nsa_cv_w2, fox_bias, w_branch, w_out,
           ffn_w1, ffn_w3, ffn_w2, router_w, moe_w1, moe_w3, moe_w2):
    b, s, d = x.shape
    depth = w_ada.shape[0]
    rope = _rope_tables(positions)
    mods = _ada(c, w_ada, b_ada).reshape(depth, b, 6, 1, d).transpose(0, 2, 1, 3, 4)
    x2 = x.reshape(b * s, d)
    for l in range(depth):
        mod = mods[l]
        x2 = _mixer_layer(x2, b, s, mod[0:3], norm_mix[l].reshape(1, d), rope, w_in[l], qk_gain[l],
                          nsa_pe_k[l], nsa_pe_v[l], nsa_ck_w1[l], nsa_ck_w2[l], nsa_cv_w1[l],
                          nsa_cv_w2[l], fox_bias[l], w_branch[l], w_out[l])
        sh_f, sc_f, g_f = mod[3], mod[4], mod[5]
        gn = norm_ffn[l].reshape(1, d)
        e = l // 2
        tiles = _tiles(s)
        if l % 2 == 0:
            dff = ffn_w1.shape[2]
            x2 = _ffn(x2, gn, sc_f, sh_f, g_f, ffn_w1[e].astype(BF16), ffn_w3[e].astype(BF16),
                      ffn_w2[e].astype(BF16), s, tiles.ffn, dff // 2)
        else:
            dff = moe_w1.shape[3]
            h2, rank, gate, cnt = _route(x2, gn, sc_f, sh_f, router_w[e].T, s, tiles.ffn)
            counts = cnt[:, :, 0].astype(jnp.int32).reshape(-1)
            x2 = _moe(counts, x2, g_f, h2, rank, gate, moe_w1[e].astype(BF16), moe_w3[e].astype(BF16),
                      moe_w2[e].astype(BF16), s, tiles.ffn, dff // 2, tiles.moe_chunk)
    return x2.reshape(b, s, d)
```

```python
import functools
from typing import NamedTuple

import numpy as np
import jax
import jax.numpy as jnp
from jax import lax
from jax.experimental import pallas as pl
from jax.experimental.pallas import tpu as pltpu

F32 = jnp.float32
BF16 = jnp.bfloat16
HI = lax.Precision.HIGHEST

LANES = 128
VMEM_LIMIT = 52 * 1024 * 1024

HEAD_DIM = 64
HEAD_SHIFT = 6
N_HEADS = 4
W_MIX = N_HEADS * HEAD_DIM
N_MIXERS = 4
ROPE_THETA = 500000.0
ROPE_DIMS = HEAD_DIM // 4
ROPE_HALF = ROPE_DIMS // 2
EPS = 1e-6
LOG2E = 1.4426950408889634
NEG = -1e30
BIG = 1e30
CMP_LEN = 32
CMP_STRIDE = 16
SEL_LEN = 64
SEL_SHIFT = 6
TOPN = 16
SEL_NEG = 2.0 ** 60
SHIFT_LANE = HEAD_DIM + 128
FIXED_MAX_LIMIT = 40.0
NSA_WINDOW = 512
DIL_CONFIGS = ((128, 1), (512, 4), (2048, 16))
SB_STOP = -110.0
FOX_STOP = 108.0
GATE_LANE = 0
FOX_LANE = 3 * N_HEADS
VC_LANE = HEAD_DIM

class _Tiles(NamedTuple):
    proj: int
    cum: int
    ffn: int
    moe_chunk: int
    causal_q: int
    causal_k: int
    window_q: int
    short_seq: int
    sb: int


def _tiles(s):
    return _Tiles(proj=min(512, s), cum=min(512, s), ffn=min(1024, s), moe_chunk=128,
                  causal_q=min(512, s), causal_k=min(512, s), window_q=256, short_seq=512,
                  sb=min(256, s))


NN = (((1,), (0,)), ((), ()))
NT = (((1,), (1,)), ((), ()))
TN = (((0,), (0,)), ((), ()))


def _dot(a, b, dims=NN, precision=None):
    return lax.dot_general(a, b, dims, precision=precision, preferred_element_type=F32)


def _dot_split(a, b_bf16, dims=NN):
    hi = a.astype(BF16)
    lo = (a - hi.astype(F32)).astype(BF16)
    return _dot(hi, b_bf16, dims) + _dot(lo, b_bf16, dims)


def _params(*sem):
    return pltpu.CompilerParams(dimension_semantics=sem, vmem_limit_bytes=VMEM_LIMIT)


def _mod_norm(x, g, sc, sh):
    ms = jnp.mean(x * x, axis=-1, keepdims=True)
    return (x * lax.rsqrt(ms + EPS) * g) * (1.0 + sc) + sh


def _head_masks():
    lane = lax.broadcasted_iota(jnp.int32, (1, W_MIX), 1)
    return [(lane >> HEAD_SHIFT) == h for h in range(N_HEADS)]


def _stack_masked(q, qm_scr, tq):
    for h, hm in enumerate(_head_masks()):
        qm_scr[h * tq:(h + 1) * tq, :] = jnp.where(hm, q, jnp.zeros_like(q))


def _pick_heads(stacked, tq, scale=None):
    out = None
    for h, hm in enumerate(_head_masks()):
        blk = stacked[h * tq:(h + 1) * tq, :]
        if scale is not None:
            blk = blk * scale[h]
        out = jnp.where(hm, blk, 0.0 if out is None else out)
    return out


def _ada_kernel(c_ref, w_ref, b_ref, o_ref):
    c = c_ref[...]
    ca = c * jax.nn.sigmoid(c)
    o_ref[0] = _dot(ca, w_ref[0], NN, HI) + b_ref[0]


def _ada(c, w_ada, b_ada):
    depth, d, n6 = w_ada.shape
    b = c.shape[0]
    tn = n6 // 4
    return pl.pallas_call(
        _ada_kernel,
        out_shape=jax.ShapeDtypeStruct((depth, b, n6), F32),
        grid=(depth, n6 // tn),
        in_specs=[
            pl.BlockSpec((b, d), lambda l, j: (0, 0)),
            pl.BlockSpec((1, d, tn), lambda l, j: (l, 0, j)),
            pl.BlockSpec((1, 1, tn), lambda l, j: (l, 0, j)),
        ],
        out_specs=pl.BlockSpec((1, b, tn), lambda l, j: (l, 0, j)),
        compiler_params=_params("arbitrary", "arbitrary"),
        name="ada_mod",
    )(c, w_ada, b_ada.reshape(depth, 1, n6))


(W_QA, W_KC, W_KV, W_BQ, W_BK, W_BV, W_CQ, W_CK, W_CV, W_DQ, W_DK, W_DV, W_MISC) = range(13)
N_WSLABS = 13
KV_QK_LANES = 2 * HEAD_DIM
_SLABS = (
    (W_QA, "all", None, F32), (W_QA, "all", "all", BF16), (W_KC, "all", None, F32),
    (W_KV, "kv", "kv", BF16),
    (W_BQ, "all", "all", BF16), (W_BK, "all", "all", BF16), (W_BV, None, None, BF16),
    (W_CQ, None, None, BF16), (W_CK, None, None, BF16), (W_CV, None, None, BF16),
    (W_DQ, "all", None, BF16), (W_DK, "all", None, BF16), (W_DV, None, None, BF16),
    (W_MISC, None, None, F32),
)
N_SLABS = len(_SLABS)
(S_QNR, S_QR, S_KC, S_KV, S_BQ, S_BK, S_BV, S_CQ, S_CK, S_CV, S_DQ, S_DK, S_DV, S_MISC) = range(N_SLABS)
_DIL_SLABS = (S_BQ, S_BK, S_BV)
_DILATIONS = tuple(d for (_, d) in DIL_CONFIGS if d > 1)
KSL_LANE, KW_LANE, VSL_LANE, VW_LANE = (i * HEAD_DIM for i in range(4))


def _proj_kernel(x_ref, g_ref, sc_ref, sh_ref, w_ref, gain_ref, bd_ref, rope_ref, *out_refs):
    h = _mod_norm(x_ref[...], g_ref[...], sc_ref[0], sh_ref[0]).astype(BF16)
    bd = bd_ref[...]
    key_lanes = lax.broadcasted_iota(jnp.int32, (1, W_MIX), 1) < KV_QK_LANES
    last_w, y_n = None, None
    raw = [_dot(h, w_ref[:, wi * W_MIX:(wi + 1) * W_MIX]) for wi in range(N_WSLABS)]
    for s, (wi, norm, rope, _) in enumerate(_SLABS):
        if wi != last_w:
            y_n = raw[wi]
            if norm:
                normed = y_n * lax.rsqrt(_dot_split(y_n * y_n, bd) + EPS)
                y_n = normed if norm == "all" else jnp.where(key_lanes, normed, y_n)
            last_w = wi
        y = y_n * gain_ref[s]
        if rope:
            c, s1, s2 = rope_ref[0, 0], rope_ref[0, 1], rope_ref[0, 2]
            if rope == "kv":
                c, s1, s2 = jnp.where(key_lanes, c, 1.0), jnp.where(key_lanes, s1, 0.0), jnp.where(key_lanes, s2, 0.0)
            y = y * c + pltpu.roll(y, W_MIX - ROPE_HALF, 1) * s1 + pltpu.roll(y, ROPE_HALF, 1) * s2
        out_refs[s][...] = y.astype(out_refs[s].dtype)
        if s in _DIL_SLABS:
            stage_scr = out_refs[-1]
            tt = y.shape[0]
            for c in range(W_MIX // LANES):
                stage_scr[c] = y[:, c * LANES:(c + 1) * LANES]
            for di, dil in enumerate(_DILATIONS):
                o_ref = out_refs[N_SLABS + _DIL_SLABS.index(s) * len(_DILATIONS) + di]
                for r in range(dil):
                    for c in range(W_MIX // LANES):
                        lo = r * W_MIX + c * LANES
                        o_ref[:, lo:lo + LANES] = (
                            stage_scr[c, pl.ds(r, tt // dil, stride=dil), :].astype(o_ref.dtype))


def _proj(x2, g, sc, sh, w_slab, gain, bd, rope, seq, tt):
    n, d = x2.shape
    tpb = seq // tt
    out_shape = [jax.ShapeDtypeStruct((n, W_MIX), dt) for (_, _, _, dt) in _SLABS]
    out_specs = [pl.BlockSpec((tt, W_MIX), lambda i: (i, 0)) for _ in _SLABS]
    for _ in _DIL_SLABS:
        for dil in _DILATIONS:
            out_shape.append(jax.ShapeDtypeStruct((n // dil, dil * W_MIX), BF16))
            out_specs.append(pl.BlockSpec((tt // dil, dil * W_MIX), lambda i: (i, 0)))
    return pl.pallas_call(
        _proj_kernel,
        out_shape=out_shape,
        grid=(n // tt,),
        in_specs=[
            pl.BlockSpec((tt, d), lambda i: (i, 0)),
            pl.BlockSpec((1, d), lambda i: (0, 0)),
            pl.BlockSpec((1, 1, d), lambda i: (i // tpb, 0, 0)),
            pl.BlockSpec((1, 1, d), lambda i: (i // tpb, 0, 0)),
            pl.BlockSpec((d, N_WSLABS * W_MIX), lambda i: (0, 0)),
            pl.BlockSpec((N_SLABS, 1, W_MIX), lambda i: (0, 0, 0)),
            pl.BlockSpec((W_MIX, W_MIX), lambda i: (0, 0)),
            pl.BlockSpec((1, 3, tt, W_MIX), lambda i: (i // tpb, 0, i % tpb, 0)),
        ],
        out_specs=out_specs,
        scratch_shapes=[pltpu.VMEM((W_MIX // LANES, tt, LANES), F32)],
        compiler_params=_params("arbitrary"),
        name="in_proj",
    )(x2, g, sc, sh, w_slab, gain, bd, rope)


def _pack_w_in(w_in, qk_gain):
    d = w_in.shape[0]
    o = 0
    a_q = w_in[:, o:o + W_MIX]; o += W_MIX
    a_kv = w_in[:, o:o + 6 * HEAD_DIM]; o += 6 * HEAD_DIM
    a_g = w_in[:, o:o + 3 * N_HEADS]; o += 3 * N_HEADS
    b_qkv = w_in[:, o:o + 3 * W_MIX]; o += 3 * W_MIX
    c_qkv = w_in[:, o:o + 3 * W_MIX]; o += 3 * W_MIX
    d_qkv = w_in[:, o:o + 3 * W_MIX]; o += 3 * W_MIX
    d_f = w_in[:, o:o + N_HEADS]; o += N_HEADS
    w_merge = w_in[:, o:]
    kc, vc, ksl, vsl, kw, vw = (a_kv[:, i * HEAD_DIM:(i + 1) * HEAD_DIM] for i in range(6))
    zeros = lambda w: jnp.zeros((d, w), w_in.dtype)
    misc = jnp.concatenate([a_g, d_f, zeros(VC_LANE - FOX_LANE - N_HEADS), vc, zeros(W_MIX - 2 * HEAD_DIM)], axis=1)
    slabs = [
        a_q, jnp.concatenate([kc, zeros(W_MIX - HEAD_DIM)], axis=1),
        jnp.concatenate([ksl, kw, vsl, vw], axis=1),
        b_qkv[:, :W_MIX], b_qkv[:, W_MIX:2 * W_MIX], b_qkv[:, 2 * W_MIX:],
        c_qkv[:, :W_MIX], c_qkv[:, W_MIX:2 * W_MIX], c_qkv[:, 2 * W_MIX:],
        d_qkv[:, :W_MIX], d_qkv[:, W_MIX:2 * W_MIX], d_qkv[:, 2 * W_MIX:],
        misc,
    ]
    w_slab = jnp.concatenate(slabs, axis=1).astype(BF16)
    scale = HEAD_DIM ** -0.5
    t4 = lambda gvec: jnp.tile(gvec, N_HEADS)
    one = jnp.ones((W_MIX,), F32)
    scale2 = scale * LOG2E
    gains = [
        t4(qk_gain[0]) * scale, t4(qk_gain[0]) * scale2,
        jnp.concatenate([qk_gain[1], jnp.ones((W_MIX - HEAD_DIM,), F32)]),
        jnp.concatenate([qk_gain[2], qk_gain[3], jnp.ones((W_MIX - KV_QK_LANES,), F32)]),
        t4(qk_gain[4]) * scale2, t4(qk_gain[5]), one,
        one * scale, one, one,
        t4(qk_gain[6]) * scale2, t4(qk_gain[7]), one,
        one,
    ]
    gain = jnp.stack(gains).reshape(N_SLABS, 1, W_MIX).astype(F32)
    return w_slab, gain, w_merge.astype(BF16)


def _rope_tables(positions):
    inv = ROPE_THETA ** (-jnp.arange(0, ROPE_DIMS, 2, dtype=F32) / ROPE_DIMS)
    ang = positions.astype(F32)[..., None] * inv
    cos, sin = jnp.cos(ang), jnp.sin(ang)
    b, s, _ = cos.shape
    pad1 = jnp.ones((b, s, HEAD_DIM - ROPE_DIMS), F32)
    pad0 = jnp.zeros((b, s, HEAD_DIM - ROPE_DIMS), F32)
    z8 = jnp.zeros_like(sin)
    c64 = jnp.concatenate([cos, cos, pad1], axis=-1)
    s1_64 = jnp.concatenate([-sin, z8, pad0], axis=-1)
    s2_64 = jnp.concatenate([z8, sin, pad0], axis=-1)
    heads = lambda t: jnp.tile(t, (1, 1, N_HEADS))
    return jnp.stack([heads(c64), heads(s1_64), heads(s2_64)], axis=1)


def _compress_kernel(a_ref, b_ref, pe_ref, w1_ref, w2_ref, o_ref):
    half = w1_ref.shape[1] // 2
    w1 = w1_ref[0]
    hid = (_dot(a_ref[0, 0], w1[:half], NN, HI) + _dot(b_ref[0, 0], w1[half:], NN, HI)
           + _dot(pe_ref[0], w1, NN, HI))
    o_ref[0, 0] = _dot(jax.nn.gelu(hid), w2_ref[0], NN, HI)


def _compress(ch, chn, pe, w1, w2):
    _, b, ncp, cw = ch.shape
    hid = w1.shape[2]
    return pl.pallas_call(
        _compress_kernel,
        out_shape=jax.ShapeDtypeStruct((2, b, ncp, HEAD_DIM), F32),
        grid=(2, b),
        in_specs=[
            pl.BlockSpec((1, 1, ncp, cw), lambda k, i: (k, i, 0, 0)),
            pl.BlockSpec((1, 1, ncp, cw), lambda k, i: (k, i, 0, 0)),
            pl.BlockSpec((1, 1, 2 * cw), lambda k, i: (k, 0, 0)),
            pl.BlockSpec((1, 2 * cw, hid), lambda k, i: (k, 0, 0)),
            pl.BlockSpec((1, hid, HEAD_DIM), lambda k, i: (k, 0, 0)),
        ],
        out_specs=pl.BlockSpec((1, 1, ncp, HEAD_DIM), lambda k, i: (k, i, 0, 0)),
        compiler_params=_params("arbitrary", "arbitrary"),
        name="nsa_compress",
    )(ch, chn, pe, w1, w2)


def _cmp_kernel(q_ref, kc_ref, vct_ref, ovt_ref, o_ref, sel_ref, qh_scr, ql_scr, acc_scr, *, tq, nc, n_sel):
    i = pl.program_id(1)
    ncp = kc_ref.shape[1]
    q = q_ref[0]
    q_hi = q.astype(BF16)
    _stack_masked(q_hi, qh_scr, tq)
    _stack_masked((q - q_hi.astype(F32)).astype(BF16), ql_scr, tq)
    kc = kc_ref[0]
    k_hi = kc.astype(BF16)
    k_lo = (kc - k_hi.astype(F32)).astype(BF16)
    ss = []
    for h in range(N_HEADS):
        rs = slice(h * tq, (h + 1) * tq)
        ss.append(_dot(k_hi, qh_scr[rs, :], NT) + _dot(k_lo, qh_scr[rs, :], NT)
                  + _dot(k_hi, ql_scr[rs, :], NT))
    t = i * tq + lax.broadcasted_iota(jnp.int32, (1, tq), 1)
    c = lax.broadcasted_iota(jnp.int32, (ncp, 1), 0)
    mask = (c * CMP_STRIDE + (CMP_LEN - 1) <= t) & (c < nc)
    psum = None
    ps = []
    for h in range(N_HEADS):
        sm = jnp.where(mask, ss[h], NEG)
        m = jnp.max(sm, axis=0, keepdims=True)
        e = jnp.where(mask, jnp.exp(sm - m), 0.0)
        l = jnp.sum(e, axis=0, keepdims=True)
        p = e * (1.0 / jnp.maximum(l, 1e-30))
        psum = p if psum is None else psum + p
        ps.append(p.astype(BF16))
    vct = vct_ref[0]
    for h in range(N_HEADS):
        rs = slice(h * HEAD_DIM, (h + 1) * HEAD_DIM)
        acc_scr[rs, :] = _dot(vct[rs, :], ps[h])
    o_ref[0] = acc_scr[...].T.astype(o_ref.dtype)
    imp = _dot(ovt_ref[...], psum, NN, HI)
    j = lax.broadcasted_iota(jnp.int32, (n_sel, 1), 0)
    cur = t >> SEL_SHIFT
    valid = j <= cur
    forced = (j == 0) | (j == cur) | (j == cur - 1)
    picked = valid & forced
    score = jnp.where(valid, jnp.where(forced, -3e38, imp), NEG)
    sel = jnp.where(picked, 1.0, 0.0)
    jf = j.astype(F32)
    for _ in range(min(TOPN, n_sel) - 3):
        mx = jnp.max(score, axis=0, keepdims=True)
        idx = jnp.min(jnp.where(score == mx, jf, float(n_sel)), axis=0, keepdims=True)
        pick = jf == idx
        sel = jnp.where(pick, 1.0, sel)
        score = jnp.where(pick, -3e38, score)
    sel_ref[0] = ((sel - 1.0) * SEL_NEG).T.astype(sel_ref.dtype)


def _cmp_topk(q_nr, kc_rep, vc_rep, overlap, tq, nc):
    b, seq, _ = q_nr.shape
    ncp = kc_rep.shape[1]
    n_sel = seq // SEL_LEN
    return pl.pallas_call(
        functools.partial(_cmp_kernel, tq=tq, nc=nc, n_sel=n_sel),
        out_shape=[jax.ShapeDtypeStruct((b, seq, W_MIX), BF16),
                   jax.ShapeDtypeStruct((b, seq, n_sel), BF16)],
        grid=(b, seq // tq),
        in_specs=[
            pl.BlockSpec((1, tq, W_MIX), lambda g, i: (g, i, 0)),
            pl.BlockSpec((1, ncp, W_MIX), lambda g, i: (g, 0, 0)),
            pl.BlockSpec((1, W_MIX, ncp), lambda g, i: (g, 0, 0)),
            pl.BlockSpec((n_sel, ncp), lambda g, i: (0, 0)),
        ],
        out_specs=[pl.BlockSpec((1, tq, W_MIX), lambda g, i: (g, i, 0)),
                   pl.BlockSpec((1, tq, n_sel), lambda g, i: (g, i, 0))],
        scratch_shapes=[pltpu.VMEM((N_HEADS * tq, W_MIX), BF16), pltpu.VMEM((N_HEADS * tq, W_MIX), BF16),
                        pltpu.VMEM((W_MIX, tq), F32)],
        compiler_params=_params("arbitrary", "arbitrary"),
        name="nsa_cmp_topk",
    )(q_nr, kc_rep, vc_rep, overlap)


def _causal_kernel(*refs, tq, tk, has_sel, has_bias, fixed_max=False):
    it = iter(refs)
    q_ref, k_ref, vt_ref = next(it), next(it), next(it)
    sel_ref = next(it) if has_sel else None
    shift_ref = next(it) if fixed_max else None
    fa_ref, fb_ref, fq_ref, thr_ref = (next(it) for _ in range(4)) if has_bias else (None,) * 4
    o_ref, qm_scr, acc_scr, sa_scr, sb_scr, m_scr, l_scr = (next(it) for _ in range(7))
    i = pl.program_id(1)
    t0 = i * tq
    lane = lax.broadcasted_iota(jnp.int32, (1, W_MIX), 1)
    if has_sel:
        n_sel = sel_ref.shape[2]
        qf = q_ref[0].astype(F32)
        selb = sel_ref[0].astype(F32)
        selb = jnp.concatenate([selb, jnp.zeros((tq, W_MIX - n_sel), F32)], axis=1)
        selb = pltpu.roll(selb, HEAD_DIM, 1)
        if fixed_max:
            selb = selb + shift_ref[...]
        for h in range(N_HEADS):
            rot = qf if h == 0 else pltpu.roll(qf, W_MIX - h * HEAD_DIM, 1)
            qm_scr[h * tq:(h + 1) * tq, :] = jnp.where(lane < HEAD_DIM, rot, selb).astype(BF16)
    else:
        _stack_masked(q_ref[0], qm_scr, tq)
    acc_scr[...] = jnp.zeros(acc_scr.shape, F32)
    t_pos = t0 + lax.broadcasted_iota(jnp.int32, (1, tq), 1)

    def scores(kt, s_buf):
        k_t = k_ref[0, kt]
        if has_sel:
            blk = (kt * tk + lax.broadcasted_iota(jnp.int32, (tk, 1), 0)) >> SEL_SHIFT
            hot = blk == lane - HEAD_DIM
            if fixed_max:
                hot = hot | (lane == SHIFT_LANE)
            one_hot = jnp.where(hot, 1.0, 0.0).astype(BF16)
            k_t = jnp.where(lane < HEAD_DIM, k_t, one_hot)
        for h in range(N_HEADS):
            s = _dot(k_t, qm_scr[h * tq:(h + 1) * tq, :], NT)
            if has_bias:
                s = s + _dot(fa_ref[0, kt], fb_ref[0, h])
            s_buf[h] = s

    def update(kt, s_buf, diag):
        vt_t = vt_ref[0, kt]
        mask = None
        if diag:
            s_pos = kt * tk + lax.broadcasted_iota(jnp.int32, (tk, 1), 0)
            mask = s_pos <= t_pos
        ps, alphas = [], []
        for h in range(N_HEADS):
            s = s_buf[h]
            if mask is not None:
                s = jnp.where(mask, s, NEG)
            if fixed_max:
                p = jnp.exp2(s)
                l_scr[h] += jnp.sum(p, axis=0, keepdims=True)
                ps.append(p.astype(BF16))
                continue
            m_old = m_scr[h]
            m_new = jnp.maximum(m_old, jnp.max(s, axis=0, keepdims=True))
            alpha = jnp.exp2(m_old - m_new)
            m_scr[h] = m_new
            p = jnp.exp2(s - m_new)
            l_scr[h] = alpha * l_scr[h] + jnp.sum(p, axis=0, keepdims=True)
            ps.append(p.astype(BF16))
            alphas.append(alpha)
        for h in range(N_HEADS):
            rs = slice(h * HEAD_DIM, (h + 1) * HEAD_DIM)
            vs = slice(VSL_LANE, VSL_LANE + HEAD_DIM) if has_sel else rs
            if fixed_max:
                acc_scr[rs, :] += _dot(vt_t[vs, :], ps[h])
            else:
                acc_scr[rs, :] = alphas[h] * acc_scr[rs, :] + _dot(vt_t[vs, :], ps[h])

    def live(kt_next):
        lane = lax.broadcasted_iota(jnp.int32, (1, thr_ref.shape[3]), 1)
        hit = None
        for h in range(N_HEADS):
            top = jnp.max(fq_ref[0, h] - m_scr[h], axis=-1, keepdims=True)
            need = top >= thr_ref[0, h]
            hit = need if hit is None else (hit | need)
        return jnp.max(jnp.where(hit & (lane == kt_next), 1.0, 0.0)) > 0.5

    n_last = t0 // tk
    m_scr[...] = jnp.full(m_scr.shape, NEG, F32)
    l_scr[...] = jnp.zeros(l_scr.shape, F32)
    scores(n_last, sa_scr)
    scores(jnp.maximum(n_last - 1, 0), sb_scr)
    update(n_last, sa_scr, True)

    def pair(j):
        kt = n_last - 1 - 2 * j
        scores(kt - 1, sa_scr)
        update(kt, sb_scr, False)
        scores(jnp.maximum(kt - 2, 0), sb_scr)
        update(kt - 1, sa_scr, False)
        return kt - 2

    n_pairs = n_last // 2
    if has_bias:
        def cond(state):
            j, go = state
            return (j < n_pairs) & go

        def body(state):
            j, _ = state
            return j + 1, live(pair(j))

        _, go = lax.while_loop(cond, body, (jnp.int32(0), live(n_last - 1)))
    else:
        lax.fori_loop(0, n_pairs, lambda j, c: (pair(j), c)[1], 0)
        go = True

    @pl.when((n_last % 2 == 1) & go)
    def _():
        update(0, sb_scr, False)

    ls = [l_scr[h] for h in range(N_HEADS)]
    for h in range(N_HEADS):
        rs = slice(h * HEAD_DIM, (h + 1) * HEAD_DIM)
        acc_scr[rs, :] = acc_scr[rs, :] / ls[h]
    o_ref[0] = acc_scr[...].T.astype(o_ref.dtype)


def _split3(x):
    def cut(v):
        bits = lax.bitcast_convert_type(v, jnp.uint32) & jnp.uint32(0xFFFF0000)
        return lax.bitcast_convert_type(bits, F32)

    hi = cut(x)
    r1 = x - hi
    mid = cut(r1)
    lo = r1 - mid
    return hi.astype(BF16), mid.astype(BF16), lo.astype(BF16)


def _causal_attn(q, k, v, tq, tk, sel=None, fcum=None, qk_bound=None, sel_bound=None):
    b, s, _ = q.shape
    assert tq <= tk
    nk = s // tk
    rows = N_HEADS * tq
    n_s = N_HEADS
    vt = v.reshape(b, nk, tk, W_MIX).transpose(0, 1, 3, 2)
    args = [q, k.reshape(b, nk, tk, W_MIX), vt]
    in_specs = [
        pl.BlockSpec((1, tq, W_MIX), lambda a, i: (a, i, 0)),
        pl.BlockSpec((1, nk, tk, W_MIX), lambda a, i: (a, 0, 0, 0)),
        pl.BlockSpec((1, nk, W_MIX, tk), lambda a, i: (a, 0, 0, 0)),
    ]
    if sel is not None:
        args.append(sel)
        in_specs.append(pl.BlockSpec((1, tq, sel.shape[2]), lambda a, i: (a, i, 0)))
    if fcum is not None:
        fcum = fcum * LOG2E
        qk_bound = qk_bound * LOG2E
        nf = 2 * 3 + 2
        ones = jnp.ones(fcum.shape, BF16)
        zero = jnp.zeros(fcum.shape, BF16)
        parts = _split3(fcum)
        key_f = jnp.stack([ones, ones, ones] + [-p for p in parts] + [zero, zero], axis=-1)
        qry_f = jnp.stack(list(parts) + [ones, ones, ones, zero, zero], axis=-1)
        fa = key_f.reshape(b, nk, tk, N_HEADS * nf)
        eye = jnp.eye(N_HEADS, dtype=BF16)
        fb = jnp.einsum('bshf,hg->bgshf', qry_f, eye).reshape(b, N_HEADS, s, N_HEADS * nf).transpose(0, 1, 3, 2)
        f_rows = fcum.transpose(0, 2, 1)
        f_end = f_rows[:, :, tk - 1::tk]
        thr = jnp.pad(f_end - (qk_bound + FOX_STOP * LOG2E), ((0, 0), (0, 0), (0, LANES - nk)),
                      constant_values=BIG).reshape(b, N_HEADS, 1, LANES)
        args += [fa, fb, f_rows.reshape(b, N_HEADS, 1, s), thr]
        in_specs += [pl.BlockSpec((1, nk, tk, N_HEADS * nf), lambda a, i: (a, 0, 0, 0)),
                     pl.BlockSpec((1, N_HEADS, N_HEADS * nf, tq), lambda a, i: (a, 0, 0, i)),
                     pl.BlockSpec((1, N_HEADS, 1, tq), lambda a, i: (a, 0, 0, i)),
                     pl.BlockSpec((1, N_HEADS, 1, LANES), lambda a, i: (a, 0, 0, 0))]
    def call(call_args, call_specs, fixed_max):
        return pl.pallas_call(
            functools.partial(_causal_kernel, tq=tq, tk=tk, has_sel=sel is not None,
                              has_bias=fcum is not None, fixed_max=fixed_max),
            out_shape=jax.ShapeDtypeStruct((b, s, W_MIX), BF16),
            grid=(b, s // tq),
            in_specs=call_specs,
            out_specs=pl.BlockSpec((1, tq, W_MIX), lambda a, i: (a, i, 0)),
            scratch_shapes=[pltpu.VMEM((rows, W_MIX), BF16), pltpu.VMEM((W_MIX, tq), F32),
                            pltpu.VMEM((n_s, tk, tq), F32), pltpu.VMEM((n_s, tk, tq), F32),
                            pltpu.VMEM((N_HEADS, 1, tq), F32), pltpu.VMEM((N_HEADS, 1, tq), F32)],
            compiler_params=_params("arbitrary", "arbitrary"),
            name="causal_sel%d_bias%d_fixed%d" % (sel is not None, fcum is not None, fixed_max),
        )(*call_args)

    if sel_bound is None:
        return call(args, in_specs, False)
    shift = jnp.zeros((1, W_MIX), F32).at[0, SHIFT_LANE].set(-sel_bound)
    shift_spec = pl.BlockSpec((1, W_MIX), lambda a, i: (0, 0))
    return lax.cond(sel_bound <= FIXED_MAX_LIMIT,
                    lambda: call(args + [shift], in_specs + [shift_spec], True),
                    lambda: call(args, in_specs, False))


def _window_kernel(*refs, tq, wk, pad, window, ls, emit_lse, other_dils, kv_lanes, fixed_max):
    it = iter(refs)
    q_ref, k_ref, v_ref = next(it), next(it), next(it)
    n_other = len(other_dils)
    others_in = [(next(it), next(it)) for _ in range(n_other)]
    wexp_ref = next(it) if n_other else None
    shift_ref = next(it) if fixed_max else None
    o_ref = next(it)
    lse_ref = next(it) if emit_lse else None
    qm_scr, acc_scr = next(it), next(it)
    others = []
    if n_other:
        og_scr, lg_scr = next(it), next(it)
        for g, (dg, (og_ref, lg_ref)) in enumerate(zip(other_dils, others_in)):
            for r in range(dg):
                rows = pl.ds(r, tq // dg, stride=dg)
                for c in range(W_MIX // LANES):
                    lo = r * W_MIX + c * LANES
                    og_scr[g, c, rows, :] = og_ref[0, :, lo:lo + LANES]
                lg_scr[g, rows, :] = lg_ref[0, :, r * LANES:(r + 1) * LANES]
            others.append((og_scr.at[g], lg_scr.at[g]))
    t0 = pl.program_id(2) * tq
    start = pl.multiple_of(jnp.clip(t0 - pad, 0, ls - wk), LANES)
    if kv_lanes is None:
        _stack_masked(q_ref[0], qm_scr, tq)
    else:
        lane = lax.broadcasted_iota(jnp.int32, (1, W_MIX), 1)
        on_key = (lane >= kv_lanes[0]) & (lane < kv_lanes[0] + HEAD_DIM)
        qf = q_ref[0].astype(F32)
        for h in range(N_HEADS):
            shift = (kv_lanes[0] - h * HEAD_DIM) % W_MIX
            rot = pltpu.roll(qf, shift, 1) if shift else qf
            qm_scr[h * tq:(h + 1) * tq, :] = jnp.where(on_key, rot, 0.0).astype(BF16)
    k_w = k_ref[0, pl.ds(start, wk), :]
    vt_w = v_ref[0, pl.ds(start, wk), :].astype(F32).T.astype(BF16)
    ss = [_dot(k_w, qm_scr[h * tq:(h + 1) * tq, :], NT) for h in range(N_HEADS)]
    t_pos = t0 + lax.broadcasted_iota(jnp.int32, (1, tq), 1)
    s_pos = start + lax.broadcasted_iota(jnp.int32, (wk, 1), 0)
    mask = (s_pos <= t_pos) & (t_pos - s_pos < window)
    ps, inv_ls, lses = [], [], []
    for h in range(N_HEADS):
        s = jnp.where(mask, ss[h], NEG)
        m = shift_ref[:, 0:1] if fixed_max else jnp.max(s, axis=0, keepdims=True)
        p = jnp.exp2(s - m)
        l = jnp.sum(p, axis=0, keepdims=True)
        ps.append(p.astype(BF16))
        inv_ls.append(1.0 / l)
        lses.append(m + jnp.log2(l))
    for h in range(N_HEADS):
        rs = slice(h * HEAD_DIM, (h + 1) * HEAD_DIM)
        vs = rs if kv_lanes is None else slice(kv_lanes[1], kv_lanes[1] + HEAD_DIM)
        acc_scr[rs, :] = _dot(vt_w[vs, :], ps[h]) * inv_ls[h]
    o_self = acc_scr[...].T
    if emit_lse or n_other:
        row = lax.broadcasted_iota(jnp.int32, (LANES, 1), 0)
        stat = jnp.zeros((LANES, tq), F32)
        for h in range(N_HEADS):
            stat = jnp.where(row == h, lses[h], stat)
        lse_tile = stat.T
    if n_other:
        lg_t = [lg[...].T for (_, lg) in others]
        row = lax.broadcasted_iota(jnp.int32, (LANES, 1), 0)
        wmat = jnp.zeros((LANES, tq), F32)
        for h in range(N_HEADS):
            group_lse = [lses[h]] + [t[h:h + 1, :] for t in lg_t]
            top = functools.reduce(jnp.maximum, group_lse)
            ws = [jnp.exp2(x - top) for x in group_lse]
            inv_den = 1.0 / functools.reduce(jnp.add, ws)
            for g, w in enumerate(ws):
                wmat = jnp.where(row == g * N_HEADS + h, w * inv_den, wmat)
        wt = wmat.T
        groups = [o_self] + [jnp.concatenate([og[c] for c in range(W_MIX // LANES)], axis=1)
                             for (og, _) in others]
        out = None
        for g, o_g in enumerate(groups):
            term = _dot_split(wt, wexp_ref[g]) * o_g
            out = term if out is None else out + term
        o_ref[0] = out.astype(o_ref.dtype)
    else:
        o_ref[0] = o_self.astype(o_ref.dtype)
    if emit_lse:
        lse_ref[0] = lse_tile


def _window_attn(q, k, v, *, dil, window, tq, out_dtype, score_bound, emit_lse=False, others=(),
                 kv_lanes=None):
    b, ls, _ = q.shape
    tq = min(tq, ls)
    pad = -(-(window - 1) // LANES) * LANES
    wk = min(tq + pad, ls)
    rows = N_HEADS * tq
    args = [q, k, v]
    in_specs = [
        pl.BlockSpec((1, tq, W_MIX), lambda a, r, i: (a, i, r)),
        pl.BlockSpec((1, ls, W_MIX), lambda a, r, i: (a, 0, r)),
        pl.BlockSpec((1, ls, W_MIX), lambda a, r, i: (a, 0, r)),
    ]
    scratch = [pltpu.VMEM((rows, W_MIX), BF16), pltpu.VMEM((W_MIX, tq), F32)]
    for (o_g, lse_g, dg) in others:
        args += [o_g, lse_g]
        in_specs += [pl.BlockSpec((1, tq // dg, dg * W_MIX), lambda a, r, i: (a, i, 0)),
                     pl.BlockSpec((1, tq // dg, dg * LANES), lambda a, r, i: (a, i, 0))]
    if others:
        wexp = np.zeros((len(others) + 1, LANES, W_MIX), np.float32)
        for g in range(len(others) + 1):
            for h in range(N_HEADS):
                wexp[g, g * N_HEADS + h, h * HEAD_DIM:(h + 1) * HEAD_DIM] = 1.0
        args.append(jnp.asarray(wexp, BF16))
        in_specs.append(pl.BlockSpec(wexp.shape, lambda a, r, i: (0, 0, 0)))
        scratch += [pltpu.VMEM((len(others), W_MIX // LANES, tq, LANES), F32),
                    pltpu.VMEM((len(others), tq, LANES), F32)]
    out_shape = [jax.ShapeDtypeStruct((b, ls, dil * W_MIX), out_dtype)]
    out_specs = [pl.BlockSpec((1, tq, W_MIX), lambda a, r, i: (a, i, r))]
    if emit_lse:
        out_shape.append(jax.ShapeDtypeStruct((b, ls, dil * LANES), F32))
        out_specs.append(pl.BlockSpec((1, tq, LANES), lambda a, r, i: (a, i, r)))
    def call(call_args, call_specs, fixed_max):
        return pl.pallas_call(
            functools.partial(_window_kernel, tq=tq, wk=wk, pad=pad, window=window, ls=ls,
                              emit_lse=emit_lse, other_dils=tuple(dg for (_, _, dg) in others),
                              kv_lanes=kv_lanes, fixed_max=fixed_max),
            out_shape=out_shape,
            grid=(b, dil, ls // tq),
            in_specs=call_specs,
            out_specs=out_specs,
            scratch_shapes=scratch,
            compiler_params=_params("arbitrary", "arbitrary", "arbitrary"),
            name="window_d%d_w%d_fixed%d" % (dil, window, fixed_max),
        )(*call_args)

    shift = jnp.full((1, LANES), score_bound, F32)
    shift_spec = pl.BlockSpec((1, LANES), lambda a, r, i: (0, 0))
    res = lax.cond(score_bound <= FIXED_MAX_LIMIT,
                   lambda: call(args + [shift], in_specs + [shift_spec], True),
                   lambda: call(args, in_specs, False))
    return (res[0], res[1], dil) if emit_lse else res[0]


def _sb_kernel(q_ref, k_ref, vt_ref, tri_ref, o_ref, qm_scr, carry_scr, acc_scr, *, tq, tk):
    i = pl.program_id(1)
    t0 = i * tq
    _stack_masked(q_ref[0], qm_scr, tq)
    carry_scr[...] = jnp.zeros(carry_scr.shape, F32)
    acc_scr[...] = jnp.zeros(acc_scr.shape, F32)
    t_pos = t0 + lax.broadcasted_iota(jnp.int32, (1, tq), 1)

    def tile(kt, diag):
        k_t = k_ref[0, kt]
        vt_t = vt_ref[0, kt]
        tri = tri_ref[...]
        zs = [_dot(k_t, qm_scr[h * tq:(h + 1) * tq, :], NT) for h in range(N_HEADS)]
        if diag:
            s_pos = kt * tk + lax.broadcasted_iota(jnp.int32, (tk, 1), 0)
            strict = s_pos < t_pos
        weights, his, los = [], [], []
        for h in range(N_HEADS):
            z = zs[h]
            lg = -(jnp.maximum(z, 0.0) + jnp.log(1.0 + jnp.exp(-jnp.abs(z))))
            if diag:
                lg = jnp.where(strict, lg, 0.0)
            hi = lg.astype(BF16)
            his.append(hi)
            los.append((lg - hi.astype(F32)).astype(BF16))
        cums = [_dot(tri, his[h]) + _dot(tri, los[h]) for h in range(N_HEADS)]
        for h in range(N_HEADS):
            a = jnp.exp(zs[h] + cums[h] + carry_scr[h])
            if diag:
                a = jnp.where(strict, a, 0.0)
            weights.append(a.astype(BF16))
            carry_scr[h] += cums[h][0:1, :]
        for h in range(N_HEADS):
            rs = slice(h * HEAD_DIM, (h + 1) * HEAD_DIM)
            acc_scr[rs, :] += _dot(vt_t[rs, :], weights[h])

    n_diag = tq // tk
    for d in reversed(range(n_diag)):
        tile(i * n_diag + d, True)
    n_below = i * n_diag

    def cond(state):
        j, top = state
        return (j < n_below) & (top > SB_STOP)

    def body(state):
        j, _ = state
        tile(n_below - 1 - j, False)
        return j + 1, jnp.max(carry_scr[...])

    lax.while_loop(cond, body, (jnp.int32(0), jnp.max(carry_scr[...])))
    o_ref[0] = acc_scr[...].T.astype(o_ref.dtype)


def _stick_breaking(q, k, v, tq, tk):
    b, s, _ = q.shape
    assert tq % tk == 0
    nk = s // tk
    rows = N_HEADS * tq
    tri = jnp.asarray(np.triu(np.ones((tk, tk), np.float32)), BF16)
    vt = v.reshape(b, nk, tk, W_MIX).transpose(0, 1, 3, 2)
    return pl.pallas_call(
        functools.partial(_sb_kernel, tq=tq, tk=tk),
        out_shape=jax.ShapeDtypeStruct((b, s, W_MIX), BF16),
        grid=(b, s // tq),
        in_specs=[
            pl.BlockSpec((1, tq, W_MIX), lambda a, i: (a, i, 0)),
            pl.BlockSpec((1, nk, tk, W_MIX), lambda a, i: (a, 0, 0, 0)),
            pl.BlockSpec((1, nk, W_MIX, tk), lambda a, i: (a, 0, 0, 0)),
            pl.BlockSpec((tk, tk), lambda a, i: (0, 0)),
        ],
        out_specs=pl.BlockSpec((1, tq, W_MIX), lambda a, i: (a, i, 0)),
        scratch_shapes=[pltpu.VMEM((rows, W_MIX), BF16), pltpu.VMEM((N_HEADS, 1, tq), F32),
                        pltpu.VMEM((W_MIX, tq), F32)],
        compiler_params=_params("arbitrary", "arbitrary"),
        name="stick_breaking",
    )(q, k.reshape(b, nk, tk, W_MIX), vt, tri)


def _foxcum_kernel(x_ref, b_ref, tri_ref, o_ref, carry_scr):
    @pl.when(pl.program_id(1) == 0)
    def _():
        carry_scr[...] = jnp.zeros(carry_scr.shape, F32)

    z = x_ref[0] + b_ref[...]
    logf = jnp.minimum(z, 0.0) - jnp.log(1.0 + jnp.exp(-jnp.abs(z)))
    tri = tri_ref[...]
    hi = logf.astype(BF16)
    r1 = logf - hi.astype(F32)
    mid = r1.astype(BF16)
    lo = (r1 - mid.astype(F32)).astype(BF16)
    cum = _dot(tri, hi) + _dot(tri, mid) + _dot(tri, lo) + carry_scr[...]
    o_ref[0] = cum
    carry_scr[...] = cum[cum.shape[0] - 1:, :]


def _fox_cumsum(misc, bias_vec, tc):
    b, s, w = misc.shape
    tri = jnp.asarray(np.tril(np.ones((tc, tc), np.float32)), BF16)
    return pl.pallas_call(
        _foxcum_kernel,
        out_shape=jax.ShapeDtypeStruct((b, s, w), F32),
        grid=(b, s // tc),
        in_specs=[
            pl.BlockSpec((1, tc, w), lambda a, i: (a, i, 0)),
            pl.BlockSpec((1, w), lambda a, i: (0, 0)),
            pl.BlockSpec((tc, tc), lambda a, i: (0, 0)),
        ],
        out_specs=pl.BlockSpec((1, tc, w), lambda a, i: (a, i, 0)),
        scratch_shapes=[pltpu.VMEM((1, w), F32)],
        compiler_params=_params("arbitrary", "arbitrary"),
        name="fox_cumsum",
    )(misc, bias_vec, tri)


def _merge_kernel(x_ref, g_ref, sc_ref, sh_ref, ga_ref, wm_ref, misc_ref, pg_ref,
                  ocmp_ref, osel_ref, owin_ref, ob_ref, oc_ref, od_ref,
                  wa_ref, wb_ref, wc_ref, wd_ref, wo_ref, o_ref):
    x = x_ref[...]
    d = x.shape[1]
    h = _mod_norm(x, g_ref[...], sc_ref[0], sh_ref[0]).astype(BF16)
    gate = jax.nn.sigmoid(misc_ref[...])
    o_a = (_dot_split(gate, pg_ref[0]) * ocmp_ref[...].astype(F32)
           + _dot_split(gate, pg_ref[1]) * osel_ref[...].astype(F32)
           + _dot_split(gate, pg_ref[2]) * owin_ref[...].astype(F32)).astype(BF16)
    mixed = jnp.zeros(x.shape, F32)
    for m, (o_m, w_ref) in enumerate(((o_a, wa_ref), (ob_ref[...], wb_ref),
                                      (oc_ref[...], wc_ref), (od_ref[...], wd_ref))):
        y = _dot(o_m, w_ref[...])
        gl = _dot(h, wm_ref[:, m * d:(m + 1) * d])
        mixed = mixed + jax.nn.sigmoid(gl) * y
    o_ref[...] = x + ga_ref[0] * _dot(mixed.astype(BF16), wo_ref[...])


def _merge(x2, g, sc, sh, ga, w_merge, misc, pg, o_cmp, o_sel, o_win, o_b, o_c, o_d,
           wa, wb, wc, wd, wo, seq, tt):
    n, d = x2.shape
    tpb = seq // tt
    row = lambda w: pl.BlockSpec((tt, w), lambda i: (i, 0))
    full = lambda a: pl.BlockSpec(a.shape, lambda i: (0,) * a.ndim)
    per_b = pl.BlockSpec((1, 1, d), lambda i: (i // tpb, 0, 0))
    return pl.pallas_call(
        _merge_kernel,
        out_shape=jax.ShapeDtypeStruct((n, d), F32),
        grid=(n // tt,),
        in_specs=[row(d), full(g), per_b, per_b, per_b, full(w_merge), row(W_MIX), full(pg)]
        + [row(W_MIX)] * 6 + [full(wa), full(wb), full(wc), full(wd), full(wo)],
        out_specs=row(d),
        compiler_params=_params("arbitrary"),
        name="merge_out",
    )(x2, g, sc, sh, ga, w_merge, misc, pg, o_cmp, o_sel, o_win, o_b, o_c, o_d, wa, wb, wc, wd, wo)


def _ffn_kernel(x_ref, g_ref, sc_ref, sh_ref, gf_ref, w1_ref, w3_ref, w2_ref, o_ref, h_scr, acc_scr):
    f = pl.program_id(1)

    @pl.when(f == 0)
    def _():
        h_scr[...] = _mod_norm(x_ref[...], g_ref[...], sc_ref[0], sh_ref[0]).astype(BF16)
        acc_scr[...] = jnp.zeros(acc_scr.shape, F32)

    h = h_scr[...]
    a = _dot(h, w1_ref[...])
    b = _dot(h, w3_ref[...])
    acc_scr[...] += _dot((a * jax.nn.sigmoid(a) * b).astype(BF16), w2_ref[...])

    @pl.when(f == pl.num_programs(1) - 1)
    def _():
        o_ref[...] = x_ref[...] + gf_ref[0] * acc_scr[...]


def _ffn(x2, g, sc, sh, gf, w1, w3, w2, seq, tt, tf):
    n, d = x2.shape
    dff = w1.shape[1]
    tpb = seq // tt
    per_b = pl.BlockSpec((1, 1, d), lambda i, f: (i // tpb, 0, 0))
    return pl.pallas_call(
        _ffn_kernel,
        out_shape=jax.ShapeDtypeStruct((n, d), F32),
        grid=(n // tt, dff // tf),
        in_specs=[
            pl.BlockSpec((tt, d), lambda i, f: (i, 0)),
            pl.BlockSpec((1, d), lambda i, f: (0, 0)),
            per_b, per_b, per_b,
            pl.BlockSpec((d, tf), lambda i, f: (0, f)),
            pl.BlockSpec((d, tf), lambda i, f: (0, f)),
            pl.BlockSpec((tf, d), lambda i, f: (f, 0)),
        ],
        out_specs=pl.BlockSpec((tt, d), lambda i, f: (i, 0)),
        scratch_shapes=[pltpu.VMEM((tt, d), BF16), pltpu.VMEM((tt, d), F32)],
        compiler_params=_params("arbitrary", "arbitrary"),
        name="ffn_swiglu",
    )(x2, g, sc, sh, gf, w1, w3, w2)


def _route_kernel(x_ref, g_ref, sc_ref, sh_ref, rw_ref, up_ref, h_ref, rank_ref, gate_ref, cnt_ref):
    hf = _mod_norm(x_ref[...], g_ref[...], sc_ref[0], sh_ref[0])
    h_ref[...] = hf.astype(BF16)
    logits = _dot(rw_ref[...], hf, NT, HI)
    ne, tt = logits.shape
    e_idx = lax.broadcasted_iota(jnp.int32, (ne, 1), 0).astype(F32)
    v1 = jnp.max(logits, axis=0, keepdims=True)
    i1 = jnp.min(jnp.where(logits == v1, e_idx, float(ne)), axis=0, keepdims=True)
    m1 = e_idx == i1
    rest = jnp.where(m1, -3e38, logits)
    v2 = jnp.max(rest, axis=0, keepdims=True)
    i2 = jnp.min(jnp.where(rest == v2, e_idx, float(ne)), axis=0, keepdims=True)
    m2 = e_idx == i2
    e2 = jnp.exp(v2 - v1)
    g1 = 1.0 / (1.0 + e2)
    g2 = e2 / (1.0 + e2)
    routed = m1 | m2
    rf = jnp.where(routed, 1.0, 0.0)
    rank = _dot(rf.astype(BF16), up_ref[...])
    rank = jnp.where(routed, rank, -1.0)
    gate = jnp.where(m1, g1, 0.0) + jnp.where(m2, g2, 0.0)
    for e in range(ne):
        rank_ref[0, e] = rank[e:e + 1, :]
        gate_ref[0, e] = gate[e:e + 1, :]
    cnt = jnp.sum(rf, axis=1, keepdims=True)
    cnt_ref[0] = jnp.broadcast_to(cnt, (ne, LANES))


def _route(x2, g, sc, sh, rw_t, seq, tt):
    n, d = x2.shape
    ne = rw_t.shape[0]
    tpb = seq // tt
    nt = n // tt
    upper = jnp.asarray(np.triu(np.ones((tt, tt), np.float32), 1), BF16)
    per_b = pl.BlockSpec((1, 1, d), lambda i: (i // tpb, 0, 0))
    return pl.pallas_call(
        _route_kernel,
        out_shape=[jax.ShapeDtypeStruct((n, d), BF16),
                   jax.ShapeDtypeStruct((nt, ne, 1, tt), F32),
                   jax.ShapeDtypeStruct((nt, ne, 1, tt), F32),
                   jax.ShapeDtypeStruct((nt, ne, LANES), F32)],
        grid=(nt,),
        in_specs=[
            pl.BlockSpec((tt, d), lambda i: (i, 0)),
            pl.BlockSpec((1, d), lambda i: (0, 0)),
            per_b, per_b,
            pl.BlockSpec((ne, d), lambda i: (0, 0)),
            pl.BlockSpec((tt, tt), lambda i: (0, 0)),
        ],
        out_specs=[pl.BlockSpec((tt, d), lambda i: (i, 0)),
                   pl.BlockSpec((1, ne, 1, tt), lambda i: (i, 0, 0, 0)),
                   pl.BlockSpec((1, ne, 1, tt), lambda i: (i, 0, 0, 0)),
                   pl.BlockSpec((1, ne, LANES), lambda i: (i, 0, 0))],
        compiler_params=_params("arbitrary"),
        name="moe_route",
    )(x2, g, sc, sh, rw_t, upper)


def _moe_kernel(cnt_ref, x_ref, gf_ref, h_ref, rank_ref, gate_ref, w1_ref, w3_ref, w2_ref,
                o_ref, acc_scr, xs_scr, y_scr, *, chunk):
    i, e, f = pl.program_id(0), pl.program_id(1), pl.program_id(2)
    ne, nf = pl.num_programs(1), pl.num_programs(2)

    @pl.when((e == 0) & (f == 0))
    def _():
        acc_scr[...] = jnp.zeros(acc_scr.shape, F32)

    count = cnt_ref[i * ne + e]
    rank = rank_ref[0, 0]
    gate = gate_ref[0, 0]
    n_small = (count + chunk - 1) // chunk
    n_big = (count + 2 * chunk - 1) // (2 * chunk)

    def one_hot(c, rows):
        r = c * rows + lax.broadcasted_iota(jnp.int32, (rows, 1), 0)
        return rank == r.astype(F32)

    def rows_of(c, rows):
        return pl.ds(pl.multiple_of(c * rows, rows), rows)

    @pl.when(f == 0)
    def _():
        h = h_ref[...]

        def gather(c, carry):
            p = jnp.where(one_hot(c, chunk), 1.0, 0.0).astype(BF16)
            xs_scr[rows_of(c, chunk), :] = _dot(p, h).astype(BF16)
            return carry

        lax.fori_loop(0, n_small, gather, 0)

        def clear(c, carry):
            y_scr[rows_of(c, 2 * chunk), :] = jnp.zeros((2 * chunk, y_scr.shape[1]), F32)
            return carry

        lax.fori_loop(0, n_big, clear, 0)

    def expert(c, carry):
        xs = xs_scr[rows_of(c, chunk), :]
        a = _dot(xs, w1_ref[0])
        b = _dot(xs, w3_ref[0])
        y_scr[rows_of(c, chunk), :] += _dot((a * jax.nn.sigmoid(a) * b).astype(BF16), w2_ref[0])
        return carry

    lax.fori_loop(0, n_small, expert, 0)

    @pl.when(f == nf - 1)
    def _():
        def scatter(c, carry):
            hit = one_hot(c, 2 * chunk)
            p = jnp.where(hit, 1.0, 0.0).astype(BF16)
            gcol = jnp.sum(jnp.where(hit, gate, 0.0), axis=-1, keepdims=True)
            acc_scr[...] += _dot(p, (y_scr[rows_of(c, 2 * chunk), :] * gcol).astype(BF16), TN)
            return carry

        lax.fori_loop(0, n_big, scatter, 0)

    @pl.when((e == ne - 1) & (f == nf - 1))
    def _():
        o_ref[...] = x_ref[...] + gf_ref[0] * acc_scr[...]


def _moe(counts, x2, gf, h2, rank, gate, w1, w3, w2, seq, tt, tf, chunk):
    n, d = x2.shape
    ne, _, dff = w1.shape
    tpb = seq // tt
    grid_spec = pltpu.PrefetchScalarGridSpec(
        num_scalar_prefetch=1,
        grid=(n // tt, ne, dff // tf),
        in_specs=[
            pl.BlockSpec((tt, d), lambda i, e, f, c: (i, 0)),
            pl.BlockSpec((1, 1, d), lambda i, e, f, c: (i // tpb, 0, 0)),
            pl.BlockSpec((tt, d), lambda i, e, f, c: (i, 0)),
            pl.BlockSpec((1, 1, 1, tt), lambda i, e, f, c: (i, e, 0, 0)),
            pl.BlockSpec((1, 1, 1, tt), lambda i, e, f, c: (i, e, 0, 0)),
            pl.BlockSpec((1, d, tf), lambda i, e, f, c: (e, 0, f)),
            pl.BlockSpec((1, d, tf), lambda i, e, f, c: (e, 0, f)),
            pl.BlockSpec((1, tf, d), lambda i, e, f, c: (e, f, 0)),
        ],
        out_specs=pl.BlockSpec((tt, d), lambda i, e, f, c: (i, 0)),
        scratch_shapes=[pltpu.VMEM((tt, d), F32), pltpu.VMEM((tt, d), BF16), pltpu.VMEM((tt, d), F32)],
    )
    return pl.pallas_call(
        functools.partial(_moe_kernel, chunk=chunk),
        out_shape=jax.ShapeDtypeStruct((n, d), F32),
        grid_spec=grid_spec,
        compiler_params=_params("arbitrary", "arbitrary", "arbitrary"),
        name="moe_experts",
    )(counts, x2, gf, h2, rank, gate, w1, w3, w2)


def _overlap_matrix(ncp, nc, n_sel):
    c0 = np.arange(ncp) * CMP_STRIDE
    c1 = c0 + CMP_LEN
    s0 = np.arange(n_sel) * SEL_LEN
    s1 = s0 + SEL_LEN
    ov = ((c0[:, None] < s1[None, :]) & (c1[:, None] > s0[None, :])).astype(np.float32)
    ov[nc:] = 0.0
    return jnp.asarray(ov.T)


def _gate_expand():
    pg = np.zeros((3, W_MIX, W_MIX), np.float32)
    for br in range(3):
        for h in range(N_HEADS):
            pg[br, GATE_LANE + 3 * h + br, h * HEAD_DIM:(h + 1) * HEAD_DIM] = 1.0
    return jnp.asarray(pg, BF16)


def _mixer_layer(x2, b, s, mod, norm_g, rope, w_in, qk_gain, pe_k, pe_v, ck1, ck2, cv1, cv2,
                 fox_b, w_branch, w_out):
    n, d = x2.shape
    sh_a, sc_a, g_a = mod[0], mod[1], mod[2]
    w_slab, gain, w_merge = _pack_w_in(w_in, qk_gain)
    bd = jnp.asarray(np.kron(np.eye(N_HEADS), np.full((HEAD_DIM, HEAD_DIM), 1.0 / HEAD_DIM)), BF16)
    tiles = _tiles(s)
    outs = _proj(x2, norm_g, sc_a, sh_a, w_slab, gain, bd, rope, s, tiles.proj)
    sl = [a.reshape(b, s, W_MIX) for a in outs[:N_SLABS]]
    dil_in = {1: (sl[S_BQ], sl[S_BK], sl[S_BV])}
    for di, dil in enumerate(_DILATIONS):
        dil_in[dil] = tuple(outs[N_SLABS + si * len(_DILATIONS) + di].reshape(b, s // dil, dil * W_MIX)
                            for si in range(len(_DIL_SLABS)))
    misc = sl[S_MISC]

    nch = s // CMP_STRIDE
    nc = nch - CMP_LEN // CMP_STRIDE + 1
    kc_raw = sl[S_KC][..., :HEAD_DIM]
    vc_raw = misc[..., VC_LANE:VC_LANE + HEAD_DIM]
    chunks = jnp.stack([kc_raw, vc_raw]).reshape(2, b, nch, CMP_STRIDE * HEAD_DIM)
    chunks_next = jnp.concatenate([chunks[:, :, 1:], jnp.zeros_like(chunks[:, :, :1])], axis=2)
    pe = jnp.stack([pe_k, pe_v]).reshape(2, 1, CMP_LEN * HEAD_DIM)
    kvc = jnp.tile(_compress(chunks, chunks_next, pe, jnp.stack([ck1, cv1]), jnp.stack([ck2, cv2])),
                   (1, 1, 1, N_HEADS))
    overlap = _overlap_matrix(nch, nc, s // SEL_LEN)
    o_cmp, selmask = _cmp_topk(sl[S_QNR], kvc[0], kvc[1].transpose(0, 2, 1).astype(BF16), overlap,
                               tiles.causal_q, nc)
    tq, tk = tiles.causal_q, tiles.causal_k
    def score_bound(gq, gk):
        return 1.02 * LOG2E * HEAD_DIM ** 0.5 * jnp.max(jnp.abs(qk_gain[gq])) * jnp.max(jnp.abs(qk_gain[gk])) + 0.1

    o_sel = _causal_attn(sl[S_QR], sl[S_KV], sl[S_KV], tq, tk, sel=selmask, sel_bound=score_bound(0, 2))
    o_win = _window_attn(sl[S_QR], sl[S_KV], sl[S_KV], dil=1, window=NSA_WINDOW, tq=tiles.window_q,
                         out_dtype=BF16, score_bound=score_bound(0, 3), kv_lanes=(KW_LANE, VW_LANE))

    others = []
    for (wdw, dil) in DIL_CONFIGS[:0:-1]:
        others.append(_window_attn(*dil_in[dil], dil=dil, window=wdw // dil + 1,
                                   tq=s // dil if s // dil <= tiles.short_seq else tiles.window_q,
                                   out_dtype=F32, score_bound=score_bound(4, 5), emit_lse=True))
    wdw, dil = DIL_CONFIGS[0]
    o_b = _window_attn(*dil_in[dil], dil=dil, window=wdw // dil + 1, tq=tiles.window_q, out_dtype=BF16,
                       score_bound=score_bound(4, 5), others=others)

    o_c = _stick_breaking(sl[S_CQ], sl[S_CK], sl[S_CV], tiles.sb, tiles.sb)

    bias_vec = jnp.zeros((1, W_MIX), F32).at[0, FOX_LANE:FOX_LANE + N_HEADS].set(fox_b)
    fcum = _fox_cumsum(misc, bias_vec, tiles.cum)
    qk_bound = 1.02 * HEAD_DIM ** 0.5 * jnp.max(jnp.abs(qk_gain[6])) * jnp.max(jnp.abs(qk_gain[7])) + 0.05
    o_d = _causal_attn(sl[S_DQ], sl[S_DK], sl[S_DV], tq, tk, fcum=fcum[..., FOX_LANE:FOX_LANE + N_HEADS],
                       qk_bound=qk_bound)

    wb16 = w_branch.astype(BF16)
    flat = lambda a: a.reshape(n, W_MIX)
    return _merge(x2, norm_g, sc_a, sh_a, g_a, w_merge, flat(misc), _gate_expand(),
                  flat(o_cmp), flat(o_sel), flat(o_win), flat(o_b), flat(o_c), flat(o_d),
                  wb16[0], wb16[1], wb16[2], wb16[3], w_out.astype(BF16), s, tiles.proj)


def kernel(x, c, positions, w_ada, b_ada, norm_mix, norm_ffn, w_in, qk_gain, nsa_pe_k, nsa_pe_v,
           nsa_ck_w1, nsa_ck_w2, nsa_cv_w1, nsa_cv_w2, fox_bias, w_branch, w_out,
           ffn_w1, ffn_w3, ffn_w2, router_w, moe_w1, moe_w3, moe_w2):
    b, s, d = x.shape
    depth = w_ada.shape[0]
    rope = _rope_tables(positions)
    mods = _ada(c, w_ada, b_ada).reshape(depth, b, 6, 1, d).transpose(0, 2, 1, 3, 4)
    x2 = x.reshape(b * s, d)
    for l in range(depth):
        mod = mods[l]
        x2 = _mixer_layer(x2, b, s, mod[0:3], norm_mix[l].reshape(1, d), rope, w_in[l], qk_gain[l],
                          nsa_pe_k[l], nsa_pe_v[l], nsa_ck_w1[l], nsa_ck_w2[l], nsa_cv_w1[l],
                          nsa_cv_w2[l], fox_bias[l], w_branch[l], w_out[l])
        sh_f, sc_f, g_f = mod[3], mod[4], mod[5]
        gn = norm_ffn[l].reshape(1, d)
        e = l // 2
        tiles = _tiles(s)
        if l % 2 == 0:
            dff = ffn_w1.shape[2]
            x2 = _ffn(x2, gn, sc_f, sh_f, g_f, ffn_w1[e].astype(BF16), ffn_w3[e].astype(BF16),
                      ffn_w2[e].astype(BF16), s, tiles.ffn, dff // 2)
        else:
            dff = moe_w1.shape[3]
            h2, rank, gate, cnt = _route(x2, gn, sc_f, sh_f, router_w[e].T, s, tiles.ffn)
            counts = cnt[:, :, 0].astype(jnp.int32).reshape(-1)
            x2 = _moe(counts, x2, g_f, h2, rank, gate, moe_w1[e].astype(BF16), moe_w3[e].astype(BF16),
                      moe_w2[e].astype(BF16), s, tiles.ffn, dff // 2, tiles.moe_chunk)
    return x2.reshape(b, s, d)
```

```python
import functools
from typing import NamedTuple

import numpy as np
import jax
import jax.numpy as jnp
from jax import lax
from jax.experimental import pallas as pl
from jax.experimental.pallas import tpu as pltpu

F32 = jnp.float32
BF16 = jnp.bfloat16
HI = lax.Precision.HIGHEST

LANES = 128
VMEM_LIMIT = 52 * 1024 * 1024

HEAD_DIM = 64
HEAD_SHIFT = 6
N_HEADS = 4
W_MIX = N_HEADS * HEAD_DIM
N_MIXERS = 4
ROPE_THETA = 500000.0
ROPE_DIMS = HEAD_DIM // 4
ROPE_HALF = ROPE_DIMS // 2
EPS = 1e-6
LOG2E = 1.4426950408889634
NEG = -1e30
BIG = 1e30
CMP_LEN = 32
CMP_STRIDE = 16
SEL_LEN = 64
SEL_SHIFT = 6
TOPN = 16
SEL_NEG = 2.0 ** 60
SHIFT_LANE = HEAD_DIM + 128
FIXED_MAX_LIMIT = 40.0
NSA_WINDOW = 512
DIL_CONFIGS = ((128, 1), (512, 4), (2048, 16))
SB_STOP = -110.0
FOX_STOP = 108.0
N_FEAT = 8
GATE_LANE = 0
FOX_LANE = 3 * N_HEADS
VC_LANE = HEAD_DIM

class _Tiles(NamedTuple):
    proj: int
    cum: int
    ffn: int
    moe_chunk: int
    causal_q: int
    causal_k: int
    window_q: int
    short_seq: int
    sb: int


def _tiles(s):
    return _Tiles(proj=min(512, s), cum=min(512, s), ffn=min(1024, s), moe_chunk=128,
                  causal_q=min(512, s), causal_k=min(512, s), window_q=256, short_seq=512,
                  sb=min(256, s))


NN = (((1,), (0,)), ((), ()))
NT = (((1,), (1,)), ((), ()))
TN = (((0,), (0,)), ((), ()))


def _dot(a, b, dims=NN, precision=None):
    return lax.dot_general(a, b, dims, precision=precision, preferred_element_type=F32)


def _dot_split(a, b_bf16, dims=NN):
    hi = a.astype(BF16)
    lo = (a - hi.astype(F32)).astype(BF16)
    return _dot(hi, b_bf16, dims) + _dot(lo, b_bf16, dims)


def _params(*sem):
    return pltpu.CompilerParams(dimension_semantics=sem, vmem_limit_bytes=VMEM_LIMIT)


def _mod_norm(x, g, sc, sh):
    ms = jnp.mean(x * x, axis=-1, keepdims=True)
    return (x * lax.rsqrt(ms + EPS) * g) * (1.0 + sc) + sh


def _head_masks():
    lane = lax.broadcasted_iota(jnp.int32, (1, W_MIX), 1)
    return [(lane >> HEAD_SHIFT) == h for h in range(N_HEADS)]


def _stack_masked(q, qm_scr, tq):
    for h, hm in enumerate(_head_masks()):
        qm_scr[h * tq:(h + 1) * tq, :] = jnp.where(hm, q, jnp.zeros_like(q))


def _pick_heads(stacked, tq, scale=None):
    out = None
    for h, hm in enumerate(_head_masks()):
        blk = stacked[h * tq:(h + 1) * tq, :]
        if scale is not None:
            blk = blk * scale[h]
        out = jnp.where(hm, blk, 0.0 if out is None else out)
    return out


def _ada_kernel(c_ref, w_ref, b_ref, o_ref):
    c = c_ref[...]
    ca = c * jax.nn.sigmoid(c)
    o_ref[0] = _dot(ca, w_ref[0], NN, HI) + b_ref[0]


def _ada(c, w_ada, b_ada):
    depth, d, n6 = w_ada.shape
    b = c.shape[0]
    tn = n6 // 4
    return pl.pallas_call(
        _ada_kernel,
        out_shape=jax.ShapeDtypeStruct((depth, b, n6), F32),
        grid=(depth, n6 // tn),
        in_specs=[
            pl.BlockSpec((b, d), lambda l, j: (0, 0)),
            pl.BlockSpec((1, d, tn), lambda l, j: (l, 0, j)),
            pl.BlockSpec((1, 1, tn), lambda l, j: (l, 0, j)),
        ],
        out_specs=pl.BlockSpec((1, b, tn), lambda l, j: (l, 0, j)),
        compiler_params=_params("arbitrary", "arbitrary"),
        name="ada_mod",
    )(c, w_ada, b_ada.reshape(depth, 1, n6))


(W_QA, W_KC, W_KV, W_BQ, W_BK, W_BV, W_CQ, W_CK, W_CV, W_DQ, W_DK, W_DV, W_MISC) = range(13)
N_WSLABS = 13
KV_QK_LANES = 2 * HEAD_DIM
_SLABS = (
    (W_QA, "all", None, F32), (W_QA, "all", "all", BF16), (W_KC, "all", None, F32),
    (W_KV, "kv", "kv", BF16),
    (W_BQ, "all", "all", BF16), (W_BK, "all", "all", BF16), (W_BV, None, None, BF16),
    (W_CQ, None, None, BF16), (W_CK, None, None, BF16), (W_CV, None, None, BF16),
    (W_DQ, "all", None, BF16), (W_DK, "all", None, BF16), (W_DV, None, None, BF16),
    (W_MISC, None, None, F32),
)
N_SLABS = len(_SLABS)
(S_QNR, S_QR, S_KC, S_KV, S_BQ, S_BK, S_BV, S_CQ, S_CK, S_CV, S_DQ, S_DK, S_DV, S_MISC) = range(N_SLABS)
_DIL_SLABS = (S_BQ, S_BK, S_BV)
_DILATIONS = tuple(d for (_, d) in DIL_CONFIGS if d > 1)
KSL_LANE, KW_LANE, VSL_LANE, VW_LANE = (i * HEAD_DIM for i in range(4))


def _proj_kernel(x_ref, g_ref, sc_ref, sh_ref, w_ref, gain_ref, bd_ref, rope_ref, *out_refs):
    h = _mod_norm(x_ref[...], g_ref[...], sc_ref[0], sh_ref[0]).astype(BF16)
    bd = bd_ref[...]
    key_lanes = lax.broadcasted_iota(jnp.int32, (1, W_MIX), 1) < KV_QK_LANES
    last_w, y_n = None, None
    raw = [_dot(h, w_ref[:, wi * W_MIX:(wi + 1) * W_MIX]) for wi in range(N_WSLABS)]
    for s, (wi, norm, rope, _) in enumerate(_SLABS):
        if wi != last_w:
            y_n = raw[wi]
            if norm:
                normed = y_n * lax.rsqrt(_dot_split(y_n * y_n, bd) + EPS)
                y_n = normed if norm == "all" else jnp.where(key_lanes, normed, y_n)
            last_w = wi
        y = y_n * gain_ref[s]
        if rope:
            c, s1, s2 = rope_ref[0, 0], rope_ref[0, 1], rope_ref[0, 2]
            if rope == "kv":
                c, s1, s2 = jnp.where(key_lanes, c, 1.0), jnp.where(key_lanes, s1, 0.0), jnp.where(key_lanes, s2, 0.0)
            y = y * c + pltpu.roll(y, W_MIX - ROPE_HALF, 1) * s1 + pltpu.roll(y, ROPE_HALF, 1) * s2
        out_refs[s][...] = y.astype(out_refs[s].dtype)
        if s in _DIL_SLABS:
            stage_scr = out_refs[-1]
            tt = y.shape[0]
            for c in range(W_MIX // LANES):
                stage_scr[c] = y[:, c * LANES:(c + 1) * LANES]
            for di, dil in enumerate(_DILATIONS):
                o_ref = out_refs[N_SLABS + _DIL_SLABS.index(s) * len(_DILATIONS) + di]
                for r in range(dil):
                    for c in range(W_MIX // LANES):
                        lo = r * W_MIX + c * LANES
                        o_ref[:, lo:lo + LANES] = (
                            stage_scr[c, pl.ds(r, tt // dil, stride=dil), :].astype(o_ref.dtype))


def _proj(x2, g, sc, sh, w_slab, gain, bd, rope, seq, tt):
    n, d = x2.shape
    tpb = seq // tt
    out_shape = [jax.ShapeDtypeStruct((n, W_MIX), dt) for (_, _, _, dt) in _SLABS]
    out_specs = [pl.BlockSpec((tt, W_MIX), lambda i: (i, 0)) for _ in _SLABS]
    for _ in _DIL_SLABS:
        for dil in _DILATIONS:
            out_shape.append(jax.ShapeDtypeStruct((n // dil, dil * W_MIX), BF16))
            out_specs.append(pl.BlockSpec((tt // dil, dil * W_MIX), lambda i: (i, 0)))
    return pl.pallas_call(
        _proj_kernel,
        out_shape=out_shape,
        grid=(n // tt,),
        in_specs=[
            pl.BlockSpec((tt, d), lambda i: (i, 0)),
            pl.BlockSpec((1, d), lambda i: (0, 0)),
            pl.BlockSpec((1, 1, d), lambda i: (i // tpb, 0, 0)),
            pl.BlockSpec((1, 1, d), lambda i: (i // tpb, 0, 0)),
            pl.BlockSpec((d, N_WSLABS * W_MIX), lambda i: (0, 0)),
            pl.BlockSpec((N_SLABS, 1, W_MIX), lambda i: (0, 0, 0)),
            pl.BlockSpec((W_MIX, W_MIX), lambda i: (0, 0)),
            pl.BlockSpec((1, 3, tt, W_MIX), lambda i: (i // tpb, 0, i % tpb, 0)),
        ],
        out_specs=out_specs,
        scratch_shapes=[pltpu.VMEM((W_MIX // LANES, tt, LANES), F32)],
        compiler_params=_params("arbitrary"),
        name="in_proj",
    )(x2, g, sc, sh, w_slab, gain, bd, rope)


def _pack_w_in(w_in, qk_gain):
    d = w_in.shape[0]
    o = 0
    a_q = w_in[:, o:o + W_MIX]; o += W_MIX
    a_kv = w_in[:, o:o + 6 * HEAD_DIM]; o += 6 * HEAD_DIM
    a_g = w_in[:, o:o + 3 * N_HEADS]; o += 3 * N_HEADS
    b_qkv = w_in[:, o:o + 3 * W_MIX]; o += 3 * W_MIX
    c_qkv = w_in[:, o:o + 3 * W_MIX]; o += 3 * W_MIX
    d_qkv = w_in[:, o:o + 3 * W_MIX]; o += 3 * W_MIX
    d_f = w_in[:, o:o + N_HEADS]; o += N_HEADS
    w_merge = w_in[:, o:]
    kc, vc, ksl, vsl, kw, vw = (a_kv[:, i * HEAD_DIM:(i + 1) * HEAD_DIM] for i in range(6))
    zeros = lambda w: jnp.zeros((d, w), w_in.dtype)
    misc = jnp.concatenate([a_g, d_f, zeros(VC_LANE - FOX_LANE - N_HEADS), vc, zeros(W_MIX - 2 * HEAD_DIM)], axis=1)
    slabs = [
        a_q, jnp.concatenate([kc, zeros(W_MIX - HEAD_DIM)], axis=1),
        jnp.concatenate([ksl, kw, vsl, vw], axis=1),
        b_qkv[:, :W_MIX], b_qkv[:, W_MIX:2 * W_MIX], b_qkv[:, 2 * W_MIX:],
        c_qkv[:, :W_MIX], c_qkv[:, W_MIX:2 * W_MIX], c_qkv[:, 2 * W_MIX:],
        d_qkv[:, :W_MIX], d_qkv[:, W_MIX:2 * W_MIX], d_qkv[:, 2 * W_MIX:],
        misc,
    ]
    w_slab = jnp.concatenate(slabs, axis=1).astype(BF16)
    scale = HEAD_DIM ** -0.5
    t4 = lambda gvec: jnp.tile(gvec, N_HEADS)
    one = jnp.ones((W_MIX,), F32)
    scale2 = scale * LOG2E
    gains = [
        t4(qk_gain[0]) * scale, t4(qk_gain[0]) * scale2,
        jnp.concatenate([qk_gain[1], jnp.ones((W_MIX - HEAD_DIM,), F32)]),
        jnp.concatenate([qk_gain[2], qk_gain[3], jnp.ones((W_MIX - KV_QK_LANES,), F32)]),
        t4(qk_gain[4]) * scale2, t4(qk_gain[5]), one,
        one * scale, one, one,
        t4(qk_gain[6]) * scale2, t4(qk_gain[7]), one,
        one,
    ]
    gain = jnp.stack(gains).reshape(N_SLABS, 1, W_MIX).astype(F32)
    return w_slab, gain, w_merge.astype(BF16)


def _rope_tables(positions):
    inv = ROPE_THETA ** (-jnp.arange(0, ROPE_DIMS, 2, dtype=F32) / ROPE_DIMS)
    ang = positions.astype(F32)[..., None] * inv
    cos, sin = jnp.cos(ang), jnp.sin(ang)
    b, s, _ = cos.shape
    pad1 = jnp.ones((b, s, HEAD_DIM - ROPE_DIMS), F32)
    pad0 = jnp.zeros((b, s, HEAD_DIM - ROPE_DIMS), F32)
    z8 = jnp.zeros_like(sin)
    c64 = jnp.concatenate([cos, cos, pad1], axis=-1)
    s1_64 = jnp.concatenate([-sin, z8, pad0], axis=-1)
    s2_64 = jnp.concatenate([z8, sin, pad0], axis=-1)
    heads = lambda t: jnp.tile(t, (1, 1, N_HEADS))
    return jnp.stack([heads(c64), heads(s1_64), heads(s2_64)], axis=1)


def _compress_kernel(a_ref, b_ref, pe_ref, w1_ref, w2_ref, o_ref):
    half = w1_ref.shape[1] // 2
    w1 = w1_ref[0]
    hid = (_dot(a_ref[0, 0], w1[:half], NN, HI) + _dot(b_ref[0, 0], w1[half:], NN, HI)
           + _dot(pe_ref[0], w1, NN, HI))
    o_ref[0, 0] = _dot(jax.nn.gelu(hid), w2_ref[0], NN, HI)


def _compress(ch, chn, pe, w1, w2):
    _, b, ncp, cw = ch.shape
    hid = w1.shape[2]
    return pl.pallas_call(
        _compress_kernel,
        out_shape=jax.ShapeDtypeStruct((2, b, ncp, HEAD_DIM), F32),
        grid=(2, b),
        in_specs=[
            pl.BlockSpec((1, 1, ncp, cw), lambda k, i: (k, i, 0, 0)),
            pl.BlockSpec((1, 1, ncp, cw), lambda k, i: (k, i, 0, 0)),
            pl.BlockSpec((1, 1, 2 * cw), lambda k, i: (k, 0, 0)),
            pl.BlockSpec((1, 2 * cw, hid), lambda k, i: (k, 0, 0)),
            pl.BlockSpec((1, hid, HEAD_DIM), lambda k, i: (k, 0, 0)),
        ],
        out_specs=pl.BlockSpec((1, 1, ncp, HEAD_DIM), lambda k, i: (k, i, 0, 0)),
        compiler_params=_params("arbitrary", "arbitrary"),
        name="nsa_compress",
    )(ch, chn, pe, w1, w2)


def _cmp_kernel(q_ref, kc_ref, vct_ref, ovt_ref, o_ref, sel_ref, qh_scr, ql_scr, acc_scr, *, tq, nc, n_sel):
    i = pl.program_id(1)
    ncp = kc_ref.shape[1]
    q = q_ref[0]
    q_hi = q.astype(BF16)
    _stack_masked(q_hi, qh_scr, tq)
    _stack_masked((q - q_hi.astype(F32)).astype(BF16), ql_scr, tq)
    kc = kc_ref[0]
    k_hi = kc.astype(BF16)
    k_lo = (kc - k_hi.astype(F32)).astype(BF16)
    ss = []
    for h in range(N_HEADS):
        rs = slice(h * tq, (h + 1) * tq)
        ss.append(_dot(k_hi, qh_scr[rs, :], NT) + _dot(k_lo, qh_scr[rs, :], NT)
                  + _dot(k_hi, ql_scr[rs, :], NT))
    t = i * tq + lax.broadcasted_iota(jnp.int32, (1, tq), 1)
    c = lax.broadcasted_iota(jnp.int32, (ncp, 1), 0)
    mask = (c * CMP_STRIDE + (CMP_LEN - 1) <= t) & (c < nc)
    psum = None
    ps = []
    for h in range(N_HEADS):
        sm = jnp.where(mask, ss[h], NEG)
        m = jnp.max(sm, axis=0, keepdims=True)
        e = jnp.where(mask, jnp.exp(sm - m), 0.0)
        l = jnp.sum(e, axis=0, keepdims=True)
        p = e * (1.0 / jnp.maximum(l, 1e-30))
        psum = p if psum is None else psum + p
        ps.append(p.astype(BF16))
    vct = vct_ref[0]
    for h in range(N_HEADS):
        rs = slice(h * HEAD_DIM, (h + 1) * HEAD_DIM)
        acc_scr[rs, :] = _dot(vct[rs, :], ps[h])
    o_ref[0] = acc_scr[...].T.astype(o_ref.dtype)
    imp = _dot(ovt_ref[...], psum, NN, HI)
    j = lax.broadcasted_iota(jnp.int32, (n_sel, 1), 0)
    cur = t >> SEL_SHIFT
    valid = j <= cur
    forced = (j == 0) | (j == cur) | (j == cur - 1)
    picked = valid & forced
    score = jnp.where(valid, jnp.where(forced, -3e38, imp), NEG)
    sel = jnp.where(picked, 1.0, 0.0)
    jf = j.astype(F32)
    for _ in range(min(TOPN, n_sel) - 3):
        mx = jnp.max(score, axis=0, keepdims=True)
        idx = jnp.min(jnp.where(score == mx, jf, float(n_sel)), axis=0, keepdims=True)
        pick = jf == idx
        sel = jnp.where(pick, 1.0, sel)
        score = jnp.where(pick, -3e38, score)
    sel_ref[0] = ((sel - 1.0) * SEL_NEG).T.astype(sel_ref.dtype)


def _cmp_topk(q_nr, kc_rep, vc_rep, overlap, tq, nc):
    b, seq, _ = q_nr.shape
    ncp = kc_rep.shape[1]
    n_sel = seq // SEL_LEN
    return pl.pallas_call(
        functools.partial(_cmp_kernel, tq=tq, nc=nc, n_sel=n_sel),
        out_shape=[jax.ShapeDtypeStruct((b, seq, W_MIX), BF16),
                   jax.ShapeDtypeStruct((b, seq, n_sel), BF16)],
        grid=(b, seq // tq),
        in_specs=[
            pl.BlockSpec((1, tq, W_MIX), lambda g, i: (g, i, 0)),
            pl.BlockSpec((1, ncp, W_MIX), lambda g, i: (g, 0, 0)),
            pl.BlockSpec((1, W_MIX, ncp), lambda g, i: (g, 0, 0)),
            pl.BlockSpec((n_sel, ncp), lambda g, i: (0, 0)),
        ],
        out_specs=[pl.BlockSpec((1, tq, W_MIX), lambda g, i: (g, i, 0)),
                   pl.BlockSpec((1, tq, n_sel), lambda g, i: (g, i, 0))],
        scratch_shapes=[pltpu.VMEM((N_HEADS * tq, W_MIX), BF16), pltpu.VMEM((N_HEADS * tq, W_MIX), BF16),
                        pltpu.VMEM((W_MIX, tq), F32)],
        compiler_params=_params("arbitrary", "arbitrary"),
        name="nsa_cmp_topk",
    )(q_nr, kc_rep, vc_rep, overlap)


def _causal_kernel(*refs, tq, tk, has_sel, has_bias, fixed_max=False):
    it = iter(refs)
    q_ref, k_ref, vt_ref = next(it), next(it), next(it)
    sel_ref = next(it) if has_sel else None
    shift_ref = next(it) if fixed_max else None
    kf_ref, qf_ref, fq_ref, thr_ref = (next(it) for _ in range(4)) if has_bias else (None,) * 4
    o_ref, qm_scr, acc_scr, sa_scr, sb_scr, m_scr, l_scr = (next(it) for _ in range(7))
    i = pl.program_id(1)
    t0 = i * tq
    lane = lax.broadcasted_iota(jnp.int32, (1, W_MIX), 1)
    if has_sel:
        n_sel = sel_ref.shape[2]
        qf = q_ref[0].astype(F32)
        selb = sel_ref[0].astype(F32)
        selb = jnp.concatenate([selb, jnp.zeros((tq, W_MIX - n_sel), F32)], axis=1)
        selb = pltpu.roll(selb, HEAD_DIM, 1)
        if fixed_max:
            selb = selb + shift_ref[...]
        for h in range(N_HEADS):
            rot = qf if h == 0 else pltpu.roll(qf, W_MIX - h * HEAD_DIM, 1)
            qm_scr[h * tq:(h + 1) * tq, :] = jnp.where(lane < HEAD_DIM, rot, selb).astype(BF16)
    elif has_bias:
        q, qfeat = q_ref[0], qf_ref[0]
        heads = _head_masks()
        feats = [(lane >= ((h + 1) % N_HEADS) * HEAD_DIM) & (lane < ((h + 1) % N_HEADS) * HEAD_DIM + N_FEAT)
                 for h in range(N_HEADS)]
        for h in range(N_HEADS):
            qm_scr[h * tq:(h + 1) * tq, :] = jnp.where(heads[h], q, jnp.where(feats[h], qfeat, jnp.zeros_like(q)))
    else:
        _stack_masked(q_ref[0], qm_scr, tq)
    acc_scr[...] = jnp.zeros(acc_scr.shape, F32)
    t_pos = t0 + lax.broadcasted_iota(jnp.int32, (1, tq), 1)

    def scores(kt, s_buf):
        k_t = k_ref[0, kt]
        if has_sel:
            blk = (kt * tk + lax.broadcasted_iota(jnp.int32, (tk, 1), 0)) >> SEL_SHIFT
            hot = blk == lane - HEAD_DIM
            if fixed_max:
                hot = hot | (lane == SHIFT_LANE)
            one_hot = jnp.where(hot, 1.0, 0.0).astype(BF16)
            k_t = jnp.where(lane < HEAD_DIM, k_t, one_hot)
        if has_bias:
            kfeat = kf_ref[0, kt]
        for h in range(N_HEADS):
            k_h = jnp.where(feats[h], kfeat, k_t) if has_bias else k_t
            s_buf[h] = _dot(k_h, qm_scr[h * tq:(h + 1) * tq, :], NT)

    def update(kt, s_buf, diag):
        vt_t = vt_ref[0, kt]
        mask = None
        if diag:
            s_pos = kt * tk + lax.broadcasted_iota(jnp.int32, (tk, 1), 0)
            mask = s_pos <= t_pos
        ps, alphas = [], []
        for h in range(N_HEADS):
            s = s_buf[h]
            if mask is not None:
                s = jnp.where(mask, s, NEG)
            if fixed_max:
                p = jnp.exp2(s)
                l_scr[h] += jnp.sum(p, axis=0, keepdims=True)
                ps.append(p.astype(BF16))
                continue
            m_old = m_scr[h]
            m_new = jnp.maximum(m_old, jnp.max(s, axis=0, keepdims=True))
            alpha = jnp.exp2(m_old - m_new)
            m_scr[h] = m_new
            p = jnp.exp2(s - m_new)
            l_scr[h] = alpha * l_scr[h] + jnp.sum(p, axis=0, keepdims=True)
            ps.append(p.astype(BF16))
            alphas.append(alpha)
        for h in range(N_HEADS):
            rs = slice(h * HEAD_DIM, (h + 1) * HEAD_DIM)
            vs = slice(VSL_LANE, VSL_LANE + HEAD_DIM) if has_sel else rs
            if fixed_max:
                acc_scr[rs, :] += _dot(vt_t[vs, :], ps[h])
            else:
                acc_scr[rs, :] = alphas[h] * acc_scr[rs, :] + _dot(vt_t[vs, :], ps[h])

    def live(kt_next):
        lane = lax.broadcasted_iota(jnp.int32, (1, thr_ref.shape[3]), 1)
        hit = None
        for h in range(N_HEADS):
            top = jnp.max(fq_ref[0, h] - m_scr[h], axis=-1, keepdims=True)
            need = top >= thr_ref[0, h]
            hit = need if hit is None else (hit | need)
        return jnp.max(jnp.where(hit & (lane == kt_next), 1.0, 0.0)) > 0.5

    n_last = t0 // tk
    m_scr[...] = jnp.full(m_scr.shape, NEG, F32)
    l_scr[...] = jnp.zeros(l_scr.shape, F32)
    scores(n_last, sa_scr)
    scores(jnp.maximum(n_last - 1, 0), sb_scr)
    update(n_last, sa_scr, True)

    def pair(j):
        kt = n_last - 1 - 2 * j
        scores(kt - 1, sa_scr)
        update(kt, sb_scr, False)
        scores(jnp.maximum(kt - 2, 0), sb_scr)
        update(kt - 1, sa_scr, False)
        return kt - 2

    n_pairs = n_last // 2
    if has_bias:
        def cond(state):
            j, go = state
            return (j < n_pairs) & go

        def body(state):
            j, _ = state
            return j + 1, live(pair(j))

        _, go = lax.while_loop(cond, body, (jnp.int32(0), live(n_last - 1)))
    else:
        lax.fori_loop(0, n_pairs, lambda j, c: (pair(j), c)[1], 0)
        go = True

    @pl.when((n_last % 2 == 1) & go)
    def _():
        update(0, sb_scr, False)

    ls = [l_scr[h] for h in range(N_HEADS)]
    for h in range(N_HEADS):
        rs = slice(h * HEAD_DIM, (h + 1) * HEAD_DIM)
        acc_scr[rs, :] = acc_scr[rs, :] / ls[h]
    o_ref[0] = acc_scr[...].T.astype(o_ref.dtype)


def _split3(x):
    def cut(v):
        bits = lax.bitcast_convert_type(v, jnp.uint32) & jnp.uint32(0xFFFF0000)
        return lax.bitcast_convert_type(bits, F32)

    hi = cut(x)
    r1 = x - hi
    mid = cut(r1)
    lo = r1 - mid
    return hi.astype(BF16), mid.astype(BF16), lo.astype(BF16)


def _causal_attn(q, k, v, tq, tk, sel=None, fcum=None, qk_bound=None, sel_bound=None):
    b, s, _ = q.shape
    assert tq <= tk
    nk = s // tk
    rows = N_HEADS * tq
    n_s = N_HEADS
    vt = v.reshape(b, nk, tk, W_MIX).transpose(0, 1, 3, 2)
    args = [q, k.reshape(b, nk, tk, W_MIX), vt]
    in_specs = [
        pl.BlockSpec((1, tq, W_MIX), lambda a, i: (a, i, 0)),
        pl.BlockSpec((1, nk, tk, W_MIX), lambda a, i: (a, 0, 0, 0)),
        pl.BlockSpec((1, nk, W_MIX, tk), lambda a, i: (a, 0, 0, 0)),
    ]
    if sel is not None:
        args.append(sel)
        in_specs.append(pl.BlockSpec((1, tq, sel.shape[2]), lambda a, i: (a, i, 0)))
    if fcum is not None:
        fcum = fcum * LOG2E
        qk_bound = qk_bound * LOG2E
        ones = jnp.ones(fcum.shape, BF16)
        zero = jnp.zeros(fcum.shape, BF16)
        parts = _split3(fcum)
        key_f = jnp.stack([ones, ones, ones] + [-p for p in parts] + [zero, zero], axis=-1)
        qry_f = jnp.stack(list(parts) + [ones, ones, ones, zero, zero], axis=-1)

        def to_slab(f):
            f = jnp.pad(f, ((0, 0), (0, 0), (0, 0), (0, HEAD_DIM - N_FEAT)))
            return jnp.roll(f, 1, axis=2).reshape(b, s, W_MIX)

        kf, qf = to_slab(key_f).reshape(b, nk, tk, W_MIX), to_slab(qry_f)
        f_rows = fcum.transpose(0, 2, 1)
        f_end = f_rows[:, :, tk - 1::tk]
        thr = jnp.pad(f_end - (qk_bound + FOX_STOP * LOG2E), ((0, 0), (0, 0), (0, LANES - nk)),
                      constant_values=BIG).reshape(b, N_HEADS, 1, LANES)
        args += [kf, qf, f_rows.reshape(b, N_HEADS, 1, s), thr]
        in_specs += [pl.BlockSpec((1, nk, tk, W_MIX), lambda a, i: (a, 0, 0, 0)),
                     pl.BlockSpec((1, tq, W_MIX), lambda a, i: (a, i, 0)),
                     pl.BlockSpec((1, N_HEADS, 1, tq), lambda a, i: (a, 0, 0, i)),
                     pl.BlockSpec((1, N_HEADS, 1, LANES), lambda a, i: (a, 0, 0, 0))]
    def call(call_args, call_specs, fixed_max):
        return pl.pallas_call(
            functools.partial(_causal_kernel, tq=tq, tk=tk, has_sel=sel is not None,
                              has_bias=fcum is not None, fixed_max=fixed_max),
            out_shape=jax.ShapeDtypeStruct((b, s, W_MIX), BF16),
            grid=(b, s // tq),
            in_specs=call_specs,
            out_specs=pl.BlockSpec((1, tq, W_MIX), lambda a, i: (a, i, 0)),
            scratch_shapes=[pltpu.VMEM((rows, W_MIX), BF16), pltpu.VMEM((W_MIX, tq), F32),
                            pltpu.VMEM((n_s, tk, tq), F32), pltpu.VMEM((n_s, tk, tq), F32),
                            pltpu.VMEM((N_HEADS, 1, tq), F32), pltpu.VMEM((N_HEADS, 1, tq), F32)],
            compiler_params=_params("arbitrary", "arbitrary"),
            name="causal_sel%d_bias%d_fixed%d" % (sel is not None, fcum is not None, fixed_max),
        )(*call_args)

    if sel_bound is None:
        return call(args, in_specs, False)
    shift = jnp.zeros((1, W_MIX), F32).at[0, SHIFT_LANE].set(-sel_bound)
    shift_spec = pl.BlockSpec((1, W_MIX), lambda a, i: (0, 0))
    return lax.cond(sel_bound <= FIXED_MAX_LIMIT,
                    lambda: call(args + [shift], in_specs + [shift_spec], True),
                    lambda: call(args, in_specs, False))


def _window_kernel(*refs, tq, wk, pad, window, ls, emit_lse, other_dils, kv_lanes, fixed_max):
    it = iter(refs)
    q_ref, k_ref, v_ref = next(it), next(it), next(it)
    n_other = len(other_dils)
    others_in = [(next(it), next(it)) for _ in range(n_other)]
    wexp_ref = next(it) if n_other else None
    shift_ref = next(it) if fixed_max else None
    o_ref = next(it)
    lse_ref = next(it) if emit_lse else None
    qm_scr, acc_scr = next(it), next(it)
    others = []
    if n_other:
        og_scr, lg_scr = next(it), next(it)
        for g, (dg, (og_ref, lg_ref)) in enumerate(zip(other_dils, others_in)):
            for r in range(dg):
                rows = pl.ds(r, tq // dg, stride=dg)
                for c in range(W_MIX // LANES):
                    lo = r * W_MIX + c * LANES
                    og_scr[g, c, rows, :] = og_ref[0, :, lo:lo + LANES]
                lg_scr[g, rows, :] = lg_ref[0, :, r * LANES:(r + 1) * LANES]
            others.append((og_scr.at[g], lg_scr.at[g]))
    t0 = pl.program_id(2) * tq
    start = pl.multiple_of(jnp.clip(t0 - pad, 0, ls - wk), LANES)
    if kv_lanes is None:
        _stack_masked(q_ref[0], qm_scr, tq)
    else:
        lane = lax.broadcasted_iota(jnp.int32, (1, W_MIX), 1)
        on_key = (lane >= kv_lanes[0]) & (lane < kv_lanes[0] + HEAD_DIM)
        qf = q_ref[0].astype(F32)
        for h in range(N_HEADS):
            shift = (kv_lanes[0] - h * HEAD_DIM) % W_MIX
            rot = pltpu.roll(qf, shift, 1) if shift else qf
            qm_scr[h * tq:(h + 1) * tq, :] = jnp.where(on_key, rot, 0.0).astype(BF16)
    k_w = k_ref[0, pl.ds(start, wk), :]
    vt_w = v_ref[0, pl.ds(start, wk), :].astype(F32).T.astype(BF16)
    ss = [_dot(k_w, qm_scr[h * tq:(h + 1) * tq, :], NT) for h in range(N_HEADS)]
    t_pos = t0 + lax.broadcasted_iota(jnp.int32, (1, tq), 1)
    s_pos = start + lax.broadcasted_iota(jnp.int32, (wk, 1), 0)
    mask = (s_pos <= t_pos) & (t_pos - s_pos < window)
    ps, inv_ls, lses = [], [], []
    for h in range(N_HEADS):
        s = jnp.where(mask, ss[h], NEG)
        m = shift_ref[:, 0:1] if fixed_max else jnp.max(s, axis=0, keepdims=True)
        p = jnp.exp2(s - m)
        l = jnp.sum(p, axis=0, keepdims=True)
        ps.append(p.astype(BF16))
        inv_ls.append(1.0 / l)
        lses.append(m + jnp.log2(l))
    for h in range(N_HEADS):
        rs = slice(h * HEAD_DIM, (h + 1) * HEAD_DIM)
        vs = rs if kv_lanes is None else slice(kv_lanes[1], kv_lanes[1] + HEAD_DIM)
        acc_scr[rs, :] = _dot(vt_w[vs, :], ps[h]) * inv_ls[h]
    o_self = acc_scr[...].T
    if emit_lse or n_other:
        row = lax.broadcasted_iota(jnp.int32, (LANES, 1), 0)
        stat = jnp.zeros((LANES, tq), F32)
        for h in range(N_HEADS):
            stat = jnp.where(row == h, lses[h], stat)
        lse_tile = stat.T
    if n_other:
        lg_t = [lg[...].T for (_, lg) in others]
        row = lax.broadcasted_iota(jnp.int32, (LANES, 1), 0)
        wmat = jnp.zeros((LANES, tq), F32)
        for h in range(N_HEADS):
            group_lse = [lses[h]] + [t[h:h + 1, :] for t in lg_t]
            top = functools.reduce(jnp.maximum, group_lse)
            ws = [jnp.exp2(x - top) for x in group_lse]
            inv_den = 1.0 / functools.reduce(jnp.add, ws)
            for g, w in enumerate(ws):
                wmat = jnp.where(row == g * N_HEADS + h, w * inv_den, wmat)
        wt = wmat.T
        groups = [o_self] + [jnp.concatenate([og[c] for c in range(W_MIX // LANES)], axis=1)
                             for (og, _) in others]
        out = None
        for g, o_g in enumerate(groups):
            term = _dot_split(wt, wexp_ref[g]) * o_g
            out = term if out is None else out + term
        o_ref[0] = out.astype(o_ref.dtype)
    else:
        o_ref[0] = o_self.astype(o_ref.dtype)
    if emit_lse:
        lse_ref[0] = lse_tile


def _window_attn(q, k, v, *, dil, window, tq, out_dtype, score_bound, emit_lse=False, others=(),
                 kv_lanes=None):
    b, ls, _ = q.shape
    tq = min(tq, ls)
    pad = -(-(window - 1) // LANES) * LANES
    wk = min(tq + pad, ls)
    rows = N_HEADS * tq
    args = [q, k, v]
    in_specs = [
        pl.BlockSpec((1, tq, W_MIX), lambda a, r, i: (a, i, r)),
        pl.BlockSpec((1, ls, W_MIX), lambda a, r, i: (a, 0, r)),
        pl.BlockSpec((1, ls, W_MIX), lambda a, r, i: (a, 0, r)),
    ]
    scratch = [pltpu.VMEM((rows, W_MIX), BF16), pltpu.VMEM((W_MIX, tq), F32)]
    for (o_g, lse_g, dg) in others:
        args += [o_g, lse_g]
        in_specs += [pl.BlockSpec((1, tq // dg, dg * W_MIX), lambda a, r, i: (a, i, 0)),
                     pl.BlockSpec((1, tq // dg, dg * LANES), lambda a, r, i: (a, i, 0))]
    if others:
        wexp = np.zeros((len(others) + 1, LANES, W_MIX), np.float32)
        for g in range(len(others) + 1):
            for h in range(N_HEADS):
                wexp[g, g * N_HEADS + h, h * HEAD_DIM:(h + 1) * HEAD_DIM] = 1.0
        args.append(jnp.asarray(wexp, BF16))
        in_specs.append(pl.BlockSpec(wexp.shape, lambda a, r, i: (0, 0, 0)))
        scratch += [pltpu.VMEM((len(others), W_MIX // LANES, tq, LANES), F32),
                    pltpu.VMEM((len(others), tq, LANES), F32)]
    out_shape = [jax.ShapeDtypeStruct((b, ls, dil * W_MIX), out_dtype)]
    out_specs = [pl.BlockSpec((1, tq, W_MIX), lambda a, r, i: (a, i, r))]
    if emit_lse:
        out_shape.append(jax.ShapeDtypeStruct((b, ls, dil * LANES), F32))
        out_specs.append(pl.BlockSpec((1, tq, LANES), lambda a, r, i: (a, i, r)))
    def call(call_args, call_specs, fixed_max):
        return pl.pallas_call(
            functools.partial(_window_kernel, tq=tq, wk=wk, pad=pad, window=window, ls=ls,
                              emit_lse=emit_lse, other_dils=tuple(dg for (_, _, dg) in others),
                              kv_lanes=kv_lanes, fixed_max=fixed_max),
            out_shape=out_shape,
            grid=(b, dil, ls // tq),
            in_specs=call_specs,
            out_specs=out_specs,
            scratch_shapes=scratch,
            compiler_params=_params("arbitrary", "arbitrary", "arbitrary"),
            name="window_d%d_w%d_fixed%d" % (dil, window, fixed_max),
        )(*call_args)

    shift = jnp.full((1, LANES), score_bound, F32)
    shift_spec = pl.BlockSpec((1, LANES), lambda a, r, i: (0, 0))
    res = lax.cond(score_bound <= FIXED_MAX_LIMIT,
                   lambda: call(args + [shift], in_specs + [shift_spec], True),
                   lambda: call(args, in_specs, False))
    return (res[0], res[1], dil) if emit_lse else res[0]


def _sb_kernel(q_ref, k_ref, vt_ref, tri_ref, o_ref, qm_scr, carry_scr, acc_scr, *, tq, tk):
    i = pl.program_id(1)
    t0 = i * tq
    _stack_masked(q_ref[0], qm_scr, tq)
    carry_scr[...] = jnp.zeros(carry_scr.shape, F32)
    acc_scr[...] = jnp.zeros(acc_scr.shape, F32)
    t_pos = t0 + lax.broadcasted_iota(jnp.int32, (1, tq), 1)

    def tile(kt, diag):
        k_t = k_ref[0, kt]
        vt_t = vt_ref[0, kt]
        tri = tri_ref[...]
        zs = [_dot(k_t, qm_scr[h * tq:(h + 1) * tq, :], NT) for h in range(N_HEADS)]
        if diag:
            s_pos = kt * tk + lax.broadcasted_iota(jnp.int32, (tk, 1), 0)
            strict = s_pos < t_pos
        weights, his, los = [], [], []
        for h in range(N_HEADS):
            z = zs[h]
            lg = -(jnp.maximum(z, 0.0) + jnp.log(1.0 + jnp.exp(-jnp.abs(z))))
            if diag:
                lg = jnp.where(strict, lg, 0.0)
            hi = lg.astype(BF16)
            his.append(hi)
            los.append((lg - hi.astype(F32)).astype(BF16))
        cums = [_dot(tri, his[h]) + _dot(tri, los[h]) for h in range(N_HEADS)]
        for h in range(N_HEADS):
            a = jnp.exp(zs[h] + cums[h] + carry_scr[h])
            if diag:
                a = jnp.where(strict, a, 0.0)
            weights.append(a.astype(BF16))
            carry_scr[h] += cums[h][0:1, :]
        for h in range(N_HEADS):
            rs = slice(h * HEAD_DIM, (h + 1) * HEAD_DIM)
            acc_scr[rs, :] += _dot(vt_t[rs, :], weights[h])

    n_diag = tq // tk
    for d in reversed(range(n_diag)):
        tile(i * n_diag + d, True)
    n_below = i * n_diag

    def cond(state):
        j, top = state
        return (j < n_below) & (top > SB_STOP)

    def body(state):
        j, _ = state
        tile(n_below - 1 - j, False)
        return j + 1, jnp.max(carry_scr[...])

    lax.while_loop(cond, body, (jnp.int32(0), jnp.max(carry_scr[...])))
    o_ref[0] = acc_scr[...].T.astype(o_ref.dtype)


def _stick_breaking(q, k, v, tq, tk):
    b, s, _ = q.shape
    assert tq % tk == 0
    nk = s // tk
    rows = N_HEADS * tq
    tri = jnp.asarray(np.triu(np.ones((tk, tk), np.float32)), BF16)
    vt = v.reshape(b, nk, tk, W_MIX).transpose(0, 1, 3, 2)
    return pl.pallas_call(
        functools.partial(_sb_kernel, tq=tq, tk=tk),
        out_shape=jax.ShapeDtypeStruct((b, s, W_MIX), BF16),
        grid=(b, s // tq),
        in_specs=[
            pl.BlockSpec((1, tq, W_MIX), lambda a, i: (a, i, 0)),
            pl.BlockSpec((1, nk, tk, W_MIX), lambda a, i: (a, 0, 0, 0)),
            pl.BlockSpec((1, nk, W_MIX, tk), lambda a, i: (a, 0, 0, 0)),
            pl.BlockSpec((tk, tk), lambda a, i: (0, 0)),
        ],
        out_specs=pl.BlockSpec((1, tq, W_MIX), lambda a, i: (a, i, 0)),
        scratch_shapes=[pltpu.VMEM((rows, W_MIX), BF16), pltpu.VMEM((N_HEADS, 1, tq), F32),
                        pltpu.VMEM((W_MIX, tq), F32)],
        compiler_params=_params("arbitrary", "arbitrary"),
        name="stick_breaking",
    )(q, k.reshape(b, nk, tk, W_MIX), vt, tri)


def _foxcum_kernel(x_ref, b_ref, tri_ref, o_ref, carry_scr):
    @pl.when(pl.program_id(1) == 0)
    def _():
        carry_scr[...] = jnp.zeros(carry_scr.shape, F32)

    z = x_ref[0] + b_ref[...]
    logf = jnp.minimum(z, 0.0) - jnp.log(1.0 + jnp.exp(-jnp.abs(z)))
    tri = tri_ref[...]
    hi = logf.astype(BF16)
    r1 = logf - hi.astype(F32)
    mid = r1.astype(BF16)
    lo = (r1 - mid.astype(F32)).astype(BF16)
    cum = _dot(tri, hi) + _dot(tri, mid) + _dot(tri, lo) + carry_scr[...]
    o_ref[0] = cum
    carry_scr[...] = cum[cum.shape[0] - 1:, :]


def _fox_cumsum(misc, bias_vec, tc):
    b, s, w = misc.shape
    tri = jnp.asarray(np.tril(np.ones((tc, tc), np.float32)), BF16)
    return pl.pallas_call(
        _foxcum_kernel,
        out_shape=jax.ShapeDtypeStruct((b, s, w), F32),
        grid=(b, s // tc),
        in_specs=[
            pl.BlockSpec((1, tc, w), lambda a, i: (a, i, 0)),
            pl.BlockSpec((1, w), lambda a, i: (0, 0)),
            pl.BlockSpec((tc, tc), lambda a, i: (0, 0)),
        ],
        out_specs=pl.BlockSpec((1, tc, w), lambda a, i: (a, i, 0)),
        scratch_shapes=[pltpu.VMEM((1, w), F32)],
        compiler_params=_params("arbitrary", "arbitrary"),
        name="fox_cumsum",
    )(misc, bias_vec, tri)


def _merge_kernel(x_ref, g_ref, sc_ref, sh_ref, ga_ref, wm_ref, misc_ref, pg_ref,
                  ocmp_ref, osel_ref, owin_ref, ob_ref, oc_ref, od_ref,
                  wa_ref, wb_ref, wc_ref, wd_ref, wo_ref, o_ref):
    x = x_ref[...]
    d = x.shape[1]
    h = _mod_norm(x, g_ref[...], sc_ref[0], sh_ref[0]).astype(BF16)
    gate = jax.nn.sigmoid(misc_ref[...])
    o_a = (_dot_split(gate, pg_ref[0]) * ocmp_ref[...].astype(F32)
           + _dot_split(gate, pg_ref[1]) * osel_ref[...].astype(F32)
           + _dot_split(gate, pg_ref[2]) * owin_ref[...].astype(F32)).astype(BF16)
    mixed = jnp.zeros(x.shape, F32)
    for m, (o_m, w_ref) in enumerate(((o_a, wa_ref), (ob_ref[...], wb_ref),
                                      (oc_ref[...], wc_ref), (od_ref[...], wd_ref))):
        y = _dot(o_m, w_ref[...])
        gl = _dot(h, wm_ref[:, m * d:(m + 1) * d])
        mixed = mixed + jax.nn.sigmoid(gl) * y
    o_ref[...] = x + ga_ref[0] * _dot(mixed.astype(BF16), wo_ref[...])


def _merge(x2, g, sc, sh, ga, w_merge, misc, pg, o_cmp, o_sel, o_win, o_b, o_c, o_d,
           wa, wb, wc, wd, wo, seq, tt):
    n, d = x2.shape
    tpb = seq // tt
    row = lambda w: pl.BlockSpec((tt, w), lambda i: (i, 0))
    full = lambda a: pl.BlockSpec(a.shape, lambda i: (0,) * a.ndim)
    per_b = pl.BlockSpec((1, 1, d), lambda i: (i // tpb, 0, 0))
    return pl.pallas_call(
        _merge_kernel,
        out_shape=jax.ShapeDtypeStruct((n, d), F32),
        grid=(n // tt,),
        in_specs=[row(d), full(g), per_b, per_b, per_b, full(w_merge), row(W_MIX), full(pg)]
        + [row(W_MIX)] * 6 + [full(wa), full(wb), full(wc), full(wd), full(wo)],
        out_specs=row(d),
        compiler_params=_params("arbitrary"),
        name="merge_out",
    )(x2, g, sc, sh, ga, w_merge, misc, pg, o_cmp, o_sel, o_win, o_b, o_c, o_d, wa, wb, wc, wd, wo)


def _ffn_kernel(x_ref, g_ref, sc_ref, sh_ref, gf_ref, w1_ref, w3_ref, w2_ref, o_ref, h_scr, acc_scr):
    f = pl.program_id(1)

    @pl.when(f == 0)
    def _():
        h_scr[...] = _mod_norm(x_ref[...], g_ref[...], sc_ref[0], sh_ref[0]).astype(BF16)
        acc_scr[...] = jnp.zeros(acc_scr.shape, F32)

    h = h_scr[...]
    a = _dot(h, w1_ref[...])
    b = _dot(h, w3_ref[...])
    acc_scr[...] += _dot((a * jax.nn.sigmoid(a) * b).astype(BF16), w2_ref[...])

    @pl.when(f == pl.num_programs(1) - 1)
    def _():
        o_ref[...] = x_ref[...] + gf_ref[0] * acc_scr[...]


def _ffn(x2, g, sc, sh, gf, w1, w3, w2, seq, tt, tf):
    n, d = x2.shape
    dff = w1.shape[1]
    tpb = seq // tt
    per_b = pl.BlockSpec((1, 1, d), lambda i, f: (i // tpb, 0, 0))
    return pl.pallas_call(
        _ffn_kernel,
        out_shape=jax.ShapeDtypeStruct((n, d), F32),
        grid=(n // tt, dff // tf),
        in_specs=[
            pl.BlockSpec((tt, d), lambda i, f: (i, 0)),
            pl.BlockSpec((1, d), lambda i, f: (0, 0)),
            per_b, per_b, per_b,
            pl.BlockSpec((d, tf), lambda i, f: (0, f)),
            pl.BlockSpec((d, tf), lambda i, f: (0, f)),
            pl.BlockSpec((tf, d), lambda i, f: (f, 0)),
        ],
        out_specs=pl.BlockSpec((tt, d), lambda i, f: (i, 0)),
        scratch_shapes=[pltpu.VMEM((tt, d), BF16), pltpu.VMEM((tt, d), F32)],
        compiler_params=_params("arbitrary", "arbitrary"),
        name="ffn_swiglu",
    )(x2, g, sc, sh, gf, w1, w3, w2)


def _route_kernel(x_ref, g_ref, sc_ref, sh_ref, rw_ref, up_ref, h_ref, rank_ref, gate_ref, cnt_ref):
    hf = _mod_norm(x_ref[...], g_ref[...], sc_ref[0], sh_ref[0])
    h_ref[...] = hf.astype(BF16)
    logits = _dot(rw_ref[...], hf, NT, HI)
    ne, tt = logits.shape
    e_idx = lax.broadcasted_iota(jnp.int32, (ne, 1), 0).astype(F32)
    v1 = jnp.max(logits, axis=0, keepdims=True)
    i1 = jnp.min(jnp.where(logits == v1, e_idx, float(ne)), axis=0, keepdims=True)
    m1 = e_idx == i1
    rest = jnp.where(m1, -3e38, logits)
    v2 = jnp.max(rest, axis=0, keepdims=True)
    i2 = jnp.min(jnp.where(rest == v2, e_idx, float(ne)), axis=0, keepdims=True)
    m2 = e_idx == i2
    e2 = jnp.exp(v2 - v1)
    g1 = 1.0 / (1.0 + e2)
    g2 = e2 / (1.0 + e2)
    routed = m1 | m2
    rf = jnp.where(routed, 1.0, 0.0)
    rank = _dot(rf.astype(BF16), up_ref[...])
    rank = jnp.where(routed, rank, -1.0)
    gate = jnp.where(m1, g1, 0.0) + jnp.where(m2, g2, 0.0)
    for e in range(ne):
        rank_ref[0, e] = rank[e:e + 1, :]
        gate_ref[0, e] = gate[e:e + 1, :]
    cnt = jnp.sum(rf, axis=1, keepdims=True)
    cnt_ref[0] = jnp.broadcast_to(cnt, (ne, LANES))


def _route(x2, g, sc, sh, rw_t, seq, tt):
    n, d = x2.shape
    ne = rw_t.shape[0]
    tpb = seq // tt
    nt = n // tt
    upper = jnp.asarray(np.triu(np.ones((tt, tt), np.float32), 1), BF16)
    per_b = pl.BlockSpec((1, 1, d), lambda i: (i // tpb, 0, 0))
    return pl.pallas_call(
        _route_kernel,
        out_shape=[jax.ShapeDtypeStruct((n, d), BF16),
                   jax.ShapeDtypeStruct((nt, ne, 1, tt), F32),
                   jax.ShapeDtypeStruct((nt, ne, 1, tt), F32),
                   jax.ShapeDtypeStruct((nt, ne, LANES), F32)],
        grid=(nt,),
        in_specs=[
            pl.BlockSpec((tt, d), lambda i: (i, 0)),
            pl.BlockSpec((1, d), lambda i: (0, 0)),
            per_b, per_b,
            pl.BlockSpec((ne, d), lambda i: (0, 0)),
            pl.BlockSpec((tt, tt), lambda i: (0, 0)),
        ],
        out_specs=[pl.BlockSpec((tt, d), lambda i: (i, 0)),
                   pl.BlockSpec((1, ne, 1, tt), lambda i: (i, 0, 0, 0)),
                   pl.BlockSpec((1, ne, 1, tt), lambda i: (i, 0, 0, 0)),
                   pl.BlockSpec((1, ne, LANES), lambda i: (i, 0, 0))],
        compiler_params=_params("arbitrary"),
        name="moe_route",
    )(x2, g, sc, sh, rw_t, upper)


def _moe_kernel(cnt_ref, x_ref, gf_ref, h_ref, rank_ref, gate_ref, w1_ref, w3_ref, w2_ref,
                o_ref, acc_scr, xs_scr, y_scr, *, chunk):
    i, e, f = pl.program_id(0), pl.program_id(1), pl.program_id(2)
    ne, nf = pl.num_programs(1), pl.num_programs(2)

    @pl.when((e == 0) & (f == 0))
    def _():
        acc_scr[...] = jnp.zeros(acc_scr.shape, F32)

    count = cnt_ref[i * ne + e]
    rank = rank_ref[0, 0]
    gate = gate_ref[0, 0]
    n_small = (count + chunk - 1) // chunk
    n_big = (count + 2 * chunk - 1) // (2 * chunk)

    def one_hot(c, rows):
        r = c * rows + lax.broadcasted_iota(jnp.int32, (rows, 1), 0)
        return rank == r.astype(F32)

    def rows_of(c, rows):
        return pl.ds(pl.multiple_of(c * rows, rows), rows)

    @pl.when(f == 0)
    def _():
        h = h_ref[...]

        def gather(c, carry):
            p = jnp.where(one_hot(c, chunk), 1.0, 0.0).astype(BF16)
            xs_scr[rows_of(c, chunk), :] = _dot(p, h).astype(BF16)
            return carry

        lax.fori_loop(0, n_small, gather, 0)

        def clear(c, carry):
            y_scr[rows_of(c, 2 * chunk), :] = jnp.zeros((2 * chunk, y_scr.shape[1]), F32)
            return carry

        lax.fori_loop(0, n_big, clear, 0)

    def expert(c, carry):
        xs = xs_scr[rows_of(c, chunk), :]
        a = _dot(xs, w1_ref[0])
        b = _dot(xs, w3_ref[0])
        y_scr[rows_of(c, chunk), :] += _dot((a * jax.nn.sigmoid(a) * b).astype(BF16), w2_ref[0])
        return carry

    lax.fori_loop(0, n_small, expert, 0)

    @pl.when(f == nf - 1)
    def _():
        def scatter(c, carry):
            hit = one_hot(c, 2 * chunk)
            p = jnp.where(hit, 1.0, 0.0).astype(BF16)
            gcol = jnp.sum(jnp.where(hit, gate, 0.0), axis=-1, keepdims=True)
            acc_scr[...] += _dot(p, (y_scr[rows_of(c, 2 * chunk), :] * gcol).astype(BF16), TN)
            return carry

        lax.fori_loop(0, n_big, scatter, 0)

    @pl.when((e == ne - 1) & (f == nf - 1))
    def _():
        o_ref[...] = x_ref[...] + gf_ref[0] * acc_scr[...]


def _moe(counts, x2, gf, h2, rank, gate, w1, w3, w2, seq, tt, tf, chunk):
    n, d = x2.shape
    ne, _, dff = w1.shape
    tpb = seq // tt
    grid_spec = pltpu.PrefetchScalarGridSpec(
        num_scalar_prefetch=1,
        grid=(n // tt, ne, dff // tf),
        in_specs=[
            pl.BlockSpec((tt, d), lambda i, e, f, c: (i, 0)),
            pl.BlockSpec((1, 1, d), lambda i, e, f, c: (i // tpb, 0, 0)),
            pl.BlockSpec((tt, d), lambda i, e, f, c: (i, 0)),
            pl.BlockSpec((1, 1, 1, tt), lambda i, e, f, c: (i, e, 0, 0)),
            pl.BlockSpec((1, 1, 1, tt), lambda i, e, f, c: (i, e, 0, 0)),
            pl.BlockSpec((1, d, tf), lambda i, e, f, c: (e, 0, f)),
            pl.BlockSpec((1, d, tf), lambda i, e, f, c: (e, 0, f)),
            pl.BlockSpec((1, tf, d), lambda i, e, f, c: (e, f, 0)),
        ],
        out_specs=pl.BlockSpec((tt, d), lambda i, e, f, c: (i, 0)),
        scratch_shapes=[pltpu.VMEM((tt, d), F32), pltpu.VMEM((tt, d), BF16), pltpu.VMEM((tt, d), F32)],
    )
    return pl.pallas_call(
        functools.partial(_moe_kernel, chunk=chunk),
        out_shape=jax.ShapeDtypeStruct((n, d), F32),
        grid_spec=grid_spec,
        compiler_params=_params("arbitrary", "arbitrary", "arbitrary"),
        name="moe_experts",
    )(counts, x2, gf, h2, rank, gate, w1, w3, w2)


def _overlap_matrix(ncp, nc, n_sel):
    c0 = np.arange(ncp) * CMP_STRIDE
    c1 = c0 + CMP_LEN
    s0 = np.arange(n_sel) * SEL_LEN
    s1 = s0 + SEL_LEN
    ov = ((c0[:, None] < s1[None, :]) & (c1[:, None] > s0[None, :])).astype(np.float32)
    ov[nc:] = 0.0
    return jnp.asarray(ov.T)


def _gate_expand():
    pg = np.zeros((3, W_MIX, W_MIX), np.float32)
    for br in range(3):
        for h in range(N_HEADS):
            pg[br, GATE_LANE + 3 * h + br, h * HEAD_DIM:(h + 1) * HEAD_DIM] = 1.0
    return jnp.asarray(pg, BF16)


def _mixer_layer(x2, b, s, mod, norm_g, rope, w_in, qk_gain, pe_k, pe_v, ck1, ck2, cv1, cv2,
                 fox_b, w_branch, w_out):
    n, d = x2.shape
    sh_a, sc_a, g_a = mod[0], mod[1], mod[2]
    w_slab, gain, w_merge = _pack_w_in(w_in, qk_gain)
    bd = jnp.asarray(np.kron(np.eye(N_HEADS), np.full((HEAD_DIM, HEAD_DIM), 1.0 / HEAD_DIM)), BF16)
    tiles = _tiles(s)
    outs = _proj(x2, norm_g, sc_a, sh_a, w_slab, gain, bd, rope, s, tiles.proj)
    sl = [a.reshape(b, s, W_MIX) for a in outs[:N_SLABS]]
    dil_in = {1: (sl[S_BQ], sl[S_BK], sl[S_BV])}
    for di, dil in enumerate(_DILATIONS):
        dil_in[dil] = tuple(outs[N_SLABS + si * len(_DILATIONS) + di].reshape(b, s // dil, dil * W_MIX)
                            for si in range(len(_DIL_SLABS)))
    misc = sl[S_MISC]

    nch = s // CMP_STRIDE
    nc = nch - CMP_LEN // CMP_STRIDE + 1
    kc_raw = sl[S_KC][..., :HEAD_DIM]
    vc_raw = misc[..., VC_LANE:VC_LANE + HEAD_DIM]
    chunks = jnp.stack([kc_raw, vc_raw]).reshape(2, b, nch, CMP_STRIDE * HEAD_DIM)
    chunks_next = jnp.concatenate([chunks[:, :, 1:], jnp.zeros_like(chunks[:, :, :1])], axis=2)
    pe = jnp.stack([pe_k, pe_v]).reshape(2, 1, CMP_LEN * HEAD_DIM)
    kvc = jnp.tile(_compress(chunks, chunks_next, pe, jnp.stack([ck1, cv1]), jnp.stack([ck2, cv2])),
                   (1, 1, 1, N_HEADS))
    overlap = _overlap_matrix(nch, nc, s // SEL_LEN)
    o_cmp, selmask = _cmp_topk(sl[S_QNR], kvc[0], kvc[1].transpose(0, 2, 1).astype(BF16), overlap,
                               tiles.causal_q, nc)
    tq, tk = tiles.causal_q, tiles.causal_k
    def score_bound(gq, gk):
        return 1.02 * LOG2E * HEAD_DIM ** 0.5 * jnp.max(jnp.abs(qk_gain[gq])) * jnp.max(jnp.abs(qk_gain[gk])) + 0.1

    o_sel = _causal_attn(sl[S_QR], sl[S_KV], sl[S_KV], tq, tk, sel=selmask, sel_bound=score_bound(0, 2))
    o_win = _window_attn(sl[S_QR], sl[S_KV], sl[S_KV], dil=1, window=NSA_WINDOW, tq=tiles.window_q,
                         out_dtype=BF16, score_bound=score_bound(0, 3), kv_lanes=(KW_LANE, VW_LANE))

    others = []
    for (wdw, dil) in DIL_CONFIGS[:0:-1]:
        others.append(_window_attn(*dil_in[dil], dil=dil, window=wdw // dil + 1,
                                   tq=s // dil if s // dil <= tiles.short_seq else tiles.window_q,
                                   out_dtype=F32, score_bound=score_bound(4, 5), emit_lse=True))
    wdw, dil = DIL_CONFIGS[0]
    o_b = _window_attn(*dil_in[dil], dil=dil, window=wdw // dil + 1, tq=tiles.window_q, out_dtype=BF16,
                       score_bound=score_bound(4, 5), others=others)

    o_c = _stick_breaking(sl[S_CQ], sl[S_CK], sl[S_CV], tiles.sb, tiles.sb)

    bias_vec = jnp.zeros((1, W_MIX), F32).at[0, FOX_LANE:FOX_LANE + N_HEADS].set(fox_b)
    fcum = _fox_cumsum(misc, bias_vec, tiles.cum)
    qk_bound = 1.02 * HEAD_DIM ** 0.5 * jnp.max(jnp.abs(qk_gain[6])) * jnp.max(jnp.abs(qk_gain[7])) + 0.05
    o_d = _causal_attn(sl[S_DQ], sl[S_DK], sl[S_DV], tq, tk, fcum=fcum[..., FOX_LANE:FOX_LANE + N_HEADS],
                       qk_bound=qk_bound)

    wb16 = w_branch.astype(BF16)
    flat = lambda a: a.reshape(n, W_MIX)
    return _merge(x2, norm_g, sc_a, sh_a, g_a, w_merge, flat(misc), _gate_expand(),
                  flat(o_cmp), flat(o_sel), flat(o_win), flat(o_b), flat(o_c), flat(o_d),
                  wb16[0], wb16[1], wb16[2], wb16[3], w_out.astype(BF16), s, tiles.proj)


def kernel(x, c, positions, w_ada, b_ada, norm_mix, norm_ffn, w_in, qk_gain, nsa_pe_k, nsa_pe_v,
           nsa_ck_w1, nsa_ck_w2, nsa_cv_w1, nsa_cv_w2, fox_bias, w_branch, w_out,
           ffn_w1, ffn_w3, ffn_w2, router_w, moe_w1, moe_w3, moe_w2):
    b, s, d = x.shape
    depth = w_ada.shape[0]
    rope = _rope_tables(positions)
    mods = _ada(c, w_ada, b_ada).reshape(depth, b, 6, 1, d).transpose(0, 2, 1, 3, 4)
    x2 = x.reshape(b * s, d)
    for l in range(depth):
        mod = mods[l]
        x2 = _mixer_layer(x2, b, s, mod[0:3], norm_mix[l].reshape(1, d), rope, w_in[l], qk_gain[l],
                          nsa_pe_k[l], nsa_pe_v[l], nsa_ck_w1[l], nsa_ck_w2[l], nsa_cv_w1[l],
                          nsa_cv_w2[l], fox_bias[l], w_branch[l], w_out[l])
        sh_f, sc_f, g_f = mod[3], mod[4], mod[5]
        gn = norm_ffn[l].reshape(1, d)
        e = l // 2
        tiles = _tiles(s)
        if l % 2 == 0:
            dff = ffn_w1.shape[2]
            x2 = _ffn(x2, gn, sc_f, sh_f, g_f, ffn_w1[e].astype(BF16), ffn_w3[e].astype(BF16),
                      ffn_w2[e].astype(BF16), s, tiles.ffn, dff // 2)
        else:
            dff = moe_w1.shape[3]
            h2, rank, gate, cnt = _route(x2, gn, sc_f, sh_f, router_w[e].T, s, tiles.ffn)
            counts = cnt[:, :, 0].astype(jnp.int32).reshape(-1)
            x2 = _moe(counts, x2, g_f, h2, rank, gate, moe_w1[e].astype(BF16), moe_w3[e].astype(BF16),
                      moe_w2[e].astype(BF16), s, tiles.ffn, dff // 2, tiles.moe_chunk)
    return x2.reshape(b, s, d)
```

```python
import functools
from typing import NamedTuple

import numpy as np
import jax
import jax.numpy as jnp
from jax import lax
from jax.experimental import pallas as pl
from jax.experimental.pallas import tpu as pltpu

F32 = jnp.float32
BF16 = jnp.bfloat16
HI = lax.Precision.HIGHEST

LANES = 128
VMEM_LIMIT = 52 * 1024 * 1024

HEAD_DIM = 64
HEAD_SHIFT = 6
N_HEADS = 4
W_MIX = N_HEADS * HEAD_DIM
N_MIXERS = 4
ROPE_THETA = 500000.0
ROPE_DIMS = HEAD_DIM // 4
ROPE_HALF = ROPE_DIMS // 2
EPS = 1e-6
LOG2E = 1.4426950408889634
NEG = -1e30
BIG = 1e30
CMP_LEN = 32
CMP_STRIDE = 16
SEL_LEN = 64
SEL_SHIFT = 6
TOPN = 16
SEL_NEG = 2.0 ** 60
SHIFT_LANE = HEAD_DIM + 128
FIXED_MAX_LIMIT = 40.0
NSA_WINDOW = 512
DIL_CONFIGS = ((128, 1), (512, 4), (2048, 16))
SB_STOP = -110.0
FOX_STOP = 108.0
N_FEAT = 8
GATE_LANE = 0
FOX_LANE = 3 * N_HEADS
VC_LANE = HEAD_DIM

class _Tiles(NamedTuple):
    proj: int
    cum: int
    ffn: int
    moe_chunk: int
    causal_q: int
    causal_k: int
    window_q: int
    short_seq: int
    sb: int


def _tiles(s):
    return _Tiles(proj=min(512, s), cum=min(512, s), ffn=min(1024, s), moe_chunk=128,
                  causal_q=min(512, s), causal_k=min(512, s), window_q=256, short_seq=512,
                  sb=min(256, s))


NN = (((1,), (0,)), ((), ()))
NT = (((1,), (1,)), ((), ()))
TN = (((0,), (0,)), ((), ()))


def _dot(a, b, dims=NN, precision=None):
    return lax.dot_general(a, b, dims, precision=precision, preferred_element_type=F32)


def _dot_split(a, b_bf16, dims=NN):
    hi = a.astype(BF16)
    lo = (a - hi.astype(F32)).astype(BF16)
    return _dot(hi, b_bf16, dims) + _dot(lo, b_bf16, dims)


def _params(*sem):
    return pltpu.CompilerParams(dimension_semantics=sem, vmem_limit_bytes=VMEM_LIMIT)


def _mod_norm(x, g, sc, sh):
    ms = jnp.mean(x * x, axis=-1, keepdims=True)
    return (x * lax.rsqrt(ms + EPS) * g) * (1.0 + sc) + sh


def _head_masks():
    lane = lax.broadcasted_iota(jnp.int32, (1, W_MIX), 1)
    return [(lane >> HEAD_SHIFT) == h for h in range(N_HEADS)]


def _stack_masked(q, qm_scr, tq):
    for h, hm in enumerate(_head_masks()):
        qm_scr[h * tq:(h + 1) * tq, :] = jnp.where(hm, q, jnp.zeros_like(q))


def _ada_kernel(c_ref, w_ref, b_ref, o_ref):
    c = c_ref[...]
    ca = c * jax.nn.sigmoid(c)
    o_ref[0] = _dot(ca, w_ref[0], NN, HI) + b_ref[0]


def _ada(c, w_ada, b_ada):
    depth, d, n6 = w_ada.shape
    b = c.shape[0]
    tn = n6 // 4
    return pl.pallas_call(
        _ada_kernel,
        out_shape=jax.ShapeDtypeStruct((depth, b, n6), F32),
        grid=(depth, n6 // tn),
        in_specs=[
            pl.BlockSpec((b, d), lambda l, j: (0, 0)),
            pl.BlockSpec((1, d, tn), lambda l, j: (l, 0, j)),
            pl.BlockSpec((1, 1, tn), lambda l, j: (l, 0, j)),
        ],
        out_specs=pl.BlockSpec((1, b, tn), lambda l, j: (l, 0, j)),
        compiler_params=_params("arbitrary", "arbitrary"),
        name="ada_mod",
    )(c, w_ada, b_ada.reshape(depth, 1, n6))


(W_QA, W_KC, W_KV, W_BQ, W_BK, W_BV, W_CQ, W_CK, W_CV, W_DQ, W_DK, W_DV, W_MISC) = range(13)
N_WSLABS = 13
KV_QK_LANES = 2 * HEAD_DIM
_SLABS = (
    (W_QA, "all", None, F32), (W_QA, "all", "all", BF16), (W_KC, "all", None, F32),
    (W_KV, "kv", "kv", BF16),
    (W_BQ, "all", "all", BF16), (W_BK, "all", "all", BF16), (W_BV, None, None, BF16),
    (W_CQ, None, None, BF16), (W_CK, None, None, BF16), (W_CV, None, None, BF16),
    (W_DQ, "all", None, BF16), (W_DK, "all", None, BF16), (W_DV, None, None, BF16),
    (W_MISC, None, None, F32),
)
N_SLABS = len(_SLABS)
(S_QNR, S_QR, S_KC, S_KV, S_BQ, S_BK, S_BV, S_CQ, S_CK, S_CV, S_DQ, S_DK, S_DV, S_MISC) = range(N_SLABS)
_DIL_SLABS = (S_BQ, S_BK, S_BV)
_DILATIONS = tuple(d for (_, d) in DIL_CONFIGS if d > 1)
KSL_LANE, KW_LANE, VSL_LANE, VW_LANE = (i * HEAD_DIM for i in range(4))


def _proj_kernel(x_ref, g_ref, sc_ref, sh_ref, w_ref, gain_ref, bd_ref, rope_ref, *out_refs):
    h = _mod_norm(x_ref[...], g_ref[...], sc_ref[0], sh_ref[0]).astype(BF16)
    bd = bd_ref[...]
    key_lanes = lax.broadcasted_iota(jnp.int32, (1, W_MIX), 1) < KV_QK_LANES
    last_w, y_n = None, None
    raw = [_dot(h, w_ref[:, wi * W_MIX:(wi + 1) * W_MIX]) for wi in range(N_WSLABS)]
    for s, (wi, norm, rope, _) in enumerate(_SLABS):
        if wi != last_w:
            y_n = raw[wi]
            if norm:
                normed = y_n * lax.rsqrt(_dot_split(y_n * y_n, bd) + EPS)
                y_n = normed if norm == "all" else jnp.where(key_lanes, normed, y_n)
            last_w = wi
        y = y_n * gain_ref[s]
        if rope:
            c, s1, s2 = rope_ref[0, 0], rope_ref[0, 1], rope_ref[0, 2]
            if rope == "kv":
                c, s1, s2 = jnp.where(key_lanes, c, 1.0), jnp.where(key_lanes, s1, 0.0), jnp.where(key_lanes, s2, 0.0)
            y = y * c + pltpu.roll(y, W_MIX - ROPE_HALF, 1) * s1 + pltpu.roll(y, ROPE_HALF, 1) * s2
        out_refs[s][...] = y.astype(out_refs[s].dtype)
        if s in _DIL_SLABS:
            stage_scr = out_refs[-1]
            tt = y.shape[0]
            for c in range(W_MIX // LANES):
                stage_scr[c] = y[:, c * LANES:(c + 1) * LANES]
            for di, dil in enumerate(_DILATIONS):
                o_ref = out_refs[N_SLABS + _DIL_SLABS.index(s) * len(_DILATIONS) + di]
                for r in range(dil):
                    for c in range(W_MIX // LANES):
                        lo = r * W_MIX + c * LANES
                        o_ref[:, lo:lo + LANES] = (
                            stage_scr[c, pl.ds(r, tt // dil, stride=dil), :].astype(o_ref.dtype))


def _proj(x2, g, sc, sh, w_slab, gain, bd, rope, seq, tt):
    n, d = x2.shape
    tpb = seq // tt
    out_shape = [jax.ShapeDtypeStruct((n, W_MIX), dt) for (_, _, _, dt) in _SLABS]
    out_specs = [pl.BlockSpec((tt, W_MIX), lambda i: (i, 0)) for _ in _SLABS]
    for _ in _DIL_SLABS:
        for dil in _DILATIONS:
            out_shape.append(jax.ShapeDtypeStruct((n // dil, dil * W_MIX), BF16))
            out_specs.append(pl.BlockSpec((tt // dil, dil * W_MIX), lambda i: (i, 0)))
    return pl.pallas_call(
        _proj_kernel,
        out_shape=out_shape,
        grid=(n // tt,),
        in_specs=[
            pl.BlockSpec((tt, d), lambda i: (i, 0)),
            pl.BlockSpec((1, d), lambda i: (0, 0)),
            pl.BlockSpec((1, 1, d), lambda i: (i // tpb, 0, 0)),
            pl.BlockSpec((1, 1, d), lambda i: (i // tpb, 0, 0)),
            pl.BlockSpec((d, N_WSLABS * W_MIX), lambda i: (0, 0)),
            pl.BlockSpec((N_SLABS, 1, W_MIX), lambda i: (0, 0, 0)),
            pl.BlockSpec((W_MIX, W_MIX), lambda i: (0, 0)),
            pl.BlockSpec((1, 3, tt, W_MIX), lambda i: (i // tpb, 0, i % tpb, 0)),
        ],
        out_specs=out_specs,
        scratch_shapes=[pltpu.VMEM((W_MIX // LANES, tt, LANES), F32)],
        compiler_params=_params("arbitrary"),
        name="in_proj",
    )(x2, g, sc, sh, w_slab, gain, bd, rope)


def _pack_w_in(w_in, qk_gain):
    d = w_in.shape[0]
    o = 0
    a_q = w_in[:, o:o + W_MIX]; o += W_MIX
    a_kv = w_in[:, o:o + 6 * HEAD_DIM]; o += 6 * HEAD_DIM
    a_g = w_in[:, o:o + 3 * N_HEADS]; o += 3 * N_HEADS
    b_qkv = w_in[:, o:o + 3 * W_MIX]; o += 3 * W_MIX
    c_qkv = w_in[:, o:o + 3 * W_MIX]; o += 3 * W_MIX
    d_qkv = w_in[:, o:o + 3 * W_MIX]; o += 3 * W_MIX
    d_f = w_in[:, o:o + N_HEADS]; o += N_HEADS
    w_merge = w_in[:, o:]
    kc, vc, ksl, vsl, kw, vw = (a_kv[:, i * HEAD_DIM:(i + 1) * HEAD_DIM] for i in range(6))
    zeros = lambda w: jnp.zeros((d, w), w_in.dtype)
    misc = jnp.concatenate([a_g, d_f, zeros(VC_LANE - FOX_LANE - N_HEADS), vc, zeros(W_MIX - 2 * HEAD_DIM)], axis=1)
    slabs = [
        a_q, jnp.concatenate([kc, zeros(W_MIX - HEAD_DIM)], axis=1),
        jnp.concatenate([ksl, kw, vsl, vw], axis=1),
        b_qkv[:, :W_MIX], b_qkv[:, W_MIX:2 * W_MIX], b_qkv[:, 2 * W_MIX:],
        c_qkv[:, :W_MIX], c_qkv[:, W_MIX:2 * W_MIX], c_qkv[:, 2 * W_MIX:],
        d_qkv[:, :W_MIX], d_qkv[:, W_MIX:2 * W_MIX], d_qkv[:, 2 * W_MIX:],
        misc,
    ]
    w_slab = jnp.concatenate(slabs, axis=1).astype(BF16)
    scale = HEAD_DIM ** -0.5
    t4 = lambda gvec: jnp.tile(gvec, N_HEADS)
    one = jnp.ones((W_MIX,), F32)
    scale2 = scale * LOG2E
    gains = [
        t4(qk_gain[0]) * scale, t4(qk_gain[0]) * scale2,
        jnp.concatenate([qk_gain[1], jnp.ones((W_MIX - HEAD_DIM,), F32)]),
        jnp.concatenate([qk_gain[2], qk_gain[3], jnp.ones((W_MIX - KV_QK_LANES,), F32)]),
        t4(qk_gain[4]) * scale2, t4(qk_gain[5]), one,
        one * scale, one, one,
        t4(qk_gain[6]) * scale2, t4(qk_gain[7]), one,
        one,
    ]
    gain = jnp.stack(gains).reshape(N_SLABS, 1, W_MIX).astype(F32)
    return w_slab, gain, w_merge.astype(BF16)


def _rope_tables(positions):
    inv = ROPE_THETA ** (-jnp.arange(0, ROPE_DIMS, 2, dtype=F32) / ROPE_DIMS)
    ang = positions.astype(F32)[..., None] * inv
    cos, sin = jnp.cos(ang), jnp.sin(ang)
    b, s, _ = cos.shape
    pad1 = jnp.ones((b, s, HEAD_DIM - ROPE_DIMS), F32)
    pad0 = jnp.zeros((b, s, HEAD_DIM - ROPE_DIMS), F32)
    z8 = jnp.zeros_like(sin)
    c64 = jnp.concatenate([cos, cos, pad1], axis=-1)
    s1_64 = jnp.concatenate([-sin, z8, pad0], axis=-1)
    s2_64 = jnp.concatenate([z8, sin, pad0], axis=-1)
    heads = lambda t: jnp.tile(t, (1, 1, N_HEADS))
    return jnp.stack([heads(c64), heads(s1_64), heads(s2_64)], axis=1)


def _compress_kernel(a_ref, b_ref, pe_ref, w1_ref, w2_ref, o_ref):
    half = w1_ref.shape[1] // 2
    w1 = w1_ref[0]
    hid = (_dot(a_ref[0, 0], w1[:half], NN, HI) + _dot(b_ref[0, 0], w1[half:], NN, HI)
           + _dot(pe_ref[0], w1, NN, HI))
    o_ref[0, 0] = _dot(jax.nn.gelu(hid), w2_ref[0], NN, HI)


def _compress(ch, chn, pe, w1, w2):
    _, b, ncp, cw = ch.shape
    hid = w1.shape[2]
    return pl.pallas_call(
        _compress_kernel,
        out_shape=jax.ShapeDtypeStruct((2, b, ncp, HEAD_DIM), F32),
        grid=(2, b),
        in_specs=[
            pl.BlockSpec((1, 1, ncp, cw), lambda k, i: (k, i, 0, 0)),
            pl.BlockSpec((1, 1, ncp, cw), lambda k, i: (k, i, 0, 0)),
            pl.BlockSpec((1, 1, 2 * cw), lambda k, i: (k, 0, 0)),
            pl.BlockSpec((1, 2 * cw, hid), lambda k, i: (k, 0, 0)),
            pl.BlockSpec((1, hid, HEAD_DIM), lambda k, i: (k, 0, 0)),
        ],
        out_specs=pl.BlockSpec((1, 1, ncp, HEAD_DIM), lambda k, i: (k, i, 0, 0)),
        compiler_params=_params("arbitrary", "arbitrary"),
        name="nsa_compress",
    )(ch, chn, pe, w1, w2)


def _cmp_kernel(q_ref, kc_ref, vct_ref, ovt_ref, o_ref, sel_ref, qh_scr, ql_scr, acc_scr, *, tq, nc, n_sel):
    i = pl.program_id(1)
    ncp = kc_ref.shape[1]
    q = q_ref[0]
    q_hi = q.astype(BF16)
    _stack_masked(q_hi, qh_scr, tq)
    _stack_masked((q - q_hi.astype(F32)).astype(BF16), ql_scr, tq)
    kc = kc_ref[0]
    k_hi = kc.astype(BF16)
    k_lo = (kc - k_hi.astype(F32)).astype(BF16)
    ss = []
    for h in range(N_HEADS):
        rs = slice(h * tq, (h + 1) * tq)
        ss.append(_dot(k_hi, qh_scr[rs, :], NT) + _dot(k_lo, qh_scr[rs, :], NT)
                  + _dot(k_hi, ql_scr[rs, :], NT))
    t = i * tq + lax.broadcasted_iota(jnp.int32, (1, tq), 1)
    c = lax.broadcasted_iota(jnp.int32, (ncp, 1), 0)
    mask = (c * CMP_STRIDE + (CMP_LEN - 1) <= t) & (c < nc)
    psum = None
    ps = []
    for h in range(N_HEADS):
        sm = jnp.where(mask, ss[h], NEG)
        m = jnp.max(sm, axis=0, keepdims=True)
        e = jnp.where(mask, jnp.exp(sm - m), 0.0)
        l = jnp.sum(e, axis=0, keepdims=True)
        p = e * (1.0 / jnp.maximum(l, 1e-30))
        psum = p if psum is None else psum + p
        ps.append(p.astype(BF16))
    vct = vct_ref[0]
    for h in range(N_HEADS):
        rs = slice(h * HEAD_DIM, (h + 1) * HEAD_DIM)
        acc_scr[rs, :] = _dot(vct[rs, :], ps[h])
    o_ref[0] = acc_scr[...].T.astype(o_ref.dtype)
    imp = _dot(ovt_ref[...], psum, NN, HI)
    j = lax.broadcasted_iota(jnp.int32, (n_sel, 1), 0)
    cur = t >> SEL_SHIFT
    valid = j <= cur
    forced = (j == 0) | (j == cur) | (j == cur - 1)
    picked = valid & forced
    score = jnp.where(valid, jnp.where(forced, -3e38, imp), NEG)
    sel = jnp.where(picked, 1.0, 0.0)
    jf = j.astype(F32)
    for _ in range(min(TOPN, n_sel) - 3):
        mx = jnp.max(score, axis=0, keepdims=True)
        idx = jnp.min(jnp.where(score == mx, jf, float(n_sel)), axis=0, keepdims=True)
        pick = jf == idx
        sel = jnp.where(pick, 1.0, sel)
        score = jnp.where(pick, -3e38, score)
    sel_ref[0] = ((sel - 1.0) * SEL_NEG).T.astype(sel_ref.dtype)


def _cmp_topk(q_nr, kc_rep, vc_rep, overlap, tq, nc):
    b, seq, _ = q_nr.shape
    ncp = kc_rep.shape[1]
    n_sel = seq // SEL_LEN
    return pl.pallas_call(
        functools.partial(_cmp_kernel, tq=tq, nc=nc, n_sel=n_sel),
        out_shape=[jax.ShapeDtypeStruct((b, seq, W_MIX), BF16),
                   jax.ShapeDtypeStruct((b, seq, n_sel), BF16)],
        grid=(b, seq // tq),
        in_specs=[
            pl.BlockSpec((1, tq, W_MIX), lambda g, i: (g, i, 0)),
            pl.BlockSpec((1, ncp, W_MIX), lambda g, i: (g, 0, 0)),
            pl.BlockSpec((1, W_MIX, ncp), lambda g, i: (g, 0, 0)),
            pl.BlockSpec((n_sel, ncp), lambda g, i: (0, 0)),
        ],
        out_specs=[pl.BlockSpec((1, tq, W_MIX), lambda g, i: (g, i, 0)),
                   pl.BlockSpec((1, tq, n_sel), lambda g, i: (g, i, 0))],
        scratch_shapes=[pltpu.VMEM((N_HEADS * tq, W_MIX), BF16), pltpu.VMEM((N_HEADS * tq, W_MIX), BF16),
                        pltpu.VMEM((W_MIX, tq), F32)],
        compiler_params=_params("arbitrary", "arbitrary"),
        name="nsa_cmp_topk",
    )(q_nr, kc_rep, vc_rep, overlap)


def _causal_kernel(*refs, tq, tk, has_sel, has_bias, fixed_max=False):
    it = iter(refs)
    q_ref, k_ref, vt_ref = next(it), next(it), next(it)
    sel_ref = next(it) if has_sel else None
    shift_ref = next(it) if fixed_max else None
    kf_ref, qf_ref, fq_ref, thr_ref = (next(it) for _ in range(4)) if has_bias else (None,) * 4
    o_ref, qm_scr, acc_scr, sa_scr, sb_scr, m_scr, l_scr = (next(it) for _ in range(7))
    i = pl.program_id(1)
    t0 = i * tq
    lane = lax.broadcasted_iota(jnp.int32, (1, W_MIX), 1)
    if has_sel:
        n_sel = sel_ref.shape[2]
        qf = q_ref[0].astype(F32)
        selb = sel_ref[0].astype(F32)
        selb = jnp.concatenate([selb, jnp.zeros((tq, W_MIX - n_sel), F32)], axis=1)
        selb = pltpu.roll(selb, HEAD_DIM, 1)
        if fixed_max:
            selb = selb + shift_ref[...]
        for h in range(N_HEADS):
            rot = qf if h == 0 else pltpu.roll(qf, W_MIX - h * HEAD_DIM, 1)
            qm_scr[h * tq:(h + 1) * tq, :] = jnp.where(lane < HEAD_DIM, rot, selb).astype(BF16)
    elif has_bias:
        q, qfeat = q_ref[0], qf_ref[0]
        heads = _head_masks()
        feats = [(lane >= ((h + 1) % N_HEADS) * HEAD_DIM) & (lane < ((h + 1) % N_HEADS) * HEAD_DIM + N_FEAT)
                 for h in range(N_HEADS)]
        for h in range(N_HEADS):
            qm_scr[h * tq:(h + 1) * tq, :] = jnp.where(heads[h], q, jnp.where(feats[h], qfeat, jnp.zeros_like(q)))
    else:
        _stack_masked(q_ref[0], qm_scr, tq)
    acc_scr[...] = jnp.zeros(acc_scr.shape, F32)
    t_pos = t0 + lax.broadcasted_iota(jnp.int32, (1, tq), 1)

    def scores(kt, s_buf):
        k_t = k_ref[0, kt]
        if has_sel:
            blk = (kt * tk + lax.broadcasted_iota(jnp.int32, (tk, 1), 0)) >> SEL_SHIFT
            hot = blk == lane - HEAD_DIM
            if fixed_max:
                hot = hot | (lane == SHIFT_LANE)
            one_hot = jnp.where(hot, 1.0, 0.0).astype(BF16)
            k_t = jnp.where(lane < HEAD_DIM, k_t, one_hot)
        if has_bias:
            kfeat = kf_ref[0, kt]
        for h in range(N_HEADS):
            k_h = jnp.where(feats[h], kfeat, k_t) if has_bias else k_t
            s_buf[h] = _dot(k_h, qm_scr[h * tq:(h + 1) * tq, :], NT)

    def update(kt, s_buf, diag):
        vt_t = vt_ref[0, kt]
        mask = None
        if diag:
            s_pos = kt * tk + lax.broadcasted_iota(jnp.int32, (tk, 1), 0)
            mask = s_pos <= t_pos
        ps, alphas = [], []
        for h in range(N_HEADS):
            s = s_buf[h]
            if mask is not None:
                s = jnp.where(mask, s, NEG)
            if fixed_max:
                p = jnp.exp2(s)
                l_scr[h] += jnp.sum(p, axis=0, keepdims=True)
                ps.append(p.astype(BF16))
                continue
            m_old = m_scr[h]
            m_new = jnp.maximum(m_old, jnp.max(s, axis=0, keepdims=True))
            alpha = jnp.exp2(m_old - m_new)
            m_scr[h] = m_new
            p = jnp.exp2(s - m_new)
            l_scr[h] = alpha * l_scr[h] + jnp.sum(p, axis=0, keepdims=True)
            ps.append(p.astype(BF16))
            alphas.append(alpha)
        for h in range(N_HEADS):
            rs = slice(h * HEAD_DIM, (h + 1) * HEAD_DIM)
            vs = slice(VSL_LANE, VSL_LANE + HEAD_DIM) if has_sel else rs
            if fixed_max:
                acc_scr[rs, :] += _dot(vt_t[vs, :], ps[h])
            else:
                acc_scr[rs, :] = alphas[h] * acc_scr[rs, :] + _dot(vt_t[vs, :], ps[h])

    def live(kt_next):
        lane = lax.broadcasted_iota(jnp.int32, (1, thr_ref.shape[3]), 1)
        hit = None
        for h in range(N_HEADS):
            top = jnp.max(fq_ref[0, h] - m_scr[h], axis=-1, keepdims=True)
            need = top >= thr_ref[0, h]
            hit = need if hit is None else (hit | need)
        return jnp.max(jnp.where(hit & (lane == kt_next), 1.0, 0.0)) > 0.5

    n_last = t0 // tk
    m_scr[...] = jnp.full(m_scr.shape, NEG, F32)
    l_scr[...] = jnp.zeros(l_scr.shape, F32)
    scores(n_last, sa_scr)
    scores(jnp.maximum(n_last - 1, 0), sb_scr)
    update(n_last, sa_scr, True)

    def pair(j):
        kt = n_last - 1 - 2 * j
        scores(kt - 1, sa_scr)
        update(kt, sb_scr, False)
        scores(jnp.maximum(kt - 2, 0), sb_scr)
        update(kt - 1, sa_scr, False)
        return kt - 2

    n_pairs = n_last // 2
    if has_bias:
        def cond(state):
            j, go = state
            return (j < n_pairs) & go

        def body(state):
            j, _ = state
            return j + 1, live(pair(j))

        _, go = lax.while_loop(cond, body, (jnp.int32(0), live(n_last - 1)))
    else:
        lax.fori_loop(0, n_pairs, lambda j, c: (pair(j), c)[1], 0)
        go = True

    @pl.when((n_last % 2 == 1) & go)
    def _():
        update(0, sb_scr, False)

    ls = [l_scr[h] for h in range(N_HEADS)]
    for h in range(N_HEADS):
        rs = slice(h * HEAD_DIM, (h + 1) * HEAD_DIM)
        acc_scr[rs, :] = acc_scr[rs, :] / ls[h]
    o_ref[0] = acc_scr[...].T.astype(o_ref.dtype)


def _split3(x):
    def cut(v):
        bits = lax.bitcast_convert_type(v, jnp.uint32) & jnp.uint32(0xFFFF0000)
        return lax.bitcast_convert_type(bits, F32)

    hi = cut(x)
    r1 = x - hi
    mid = cut(r1)
    lo = r1 - mid
    return hi.astype(BF16), mid.astype(BF16), lo.astype(BF16)


def _causal_attn(q, k, v, tq, tk, sel=None, fcum=None, qk_bound=None, sel_bound=None):
    b, s, _ = q.shape
    assert tq <= tk
    nk = s // tk
    rows = N_HEADS * tq
    n_s = N_HEADS
    vt = v.reshape(b, nk, tk, W_MIX).transpose(0, 1, 3, 2)
    args = [q, k.reshape(b, nk, tk, W_MIX), vt]
    in_specs = [
        pl.BlockSpec((1, tq, W_MIX), lambda a, i: (a, i, 0)),
        pl.BlockSpec((1, nk, tk, W_MIX), lambda a, i: (a, 0, 0, 0)),
        pl.BlockSpec((1, nk, W_MIX, tk), lambda a, i: (a, 0, 0, 0)),
    ]
    if sel is not None:
        args.append(sel)
        in_specs.append(pl.BlockSpec((1, tq, sel.shape[2]), lambda a, i: (a, i, 0)))
    if fcum is not None:
        fcum = fcum * LOG2E
        qk_bound = qk_bound * LOG2E
        ones = jnp.ones(fcum.shape, BF16)
        zero = jnp.zeros(fcum.shape, BF16)
        parts = _split3(fcum)
        key_f = jnp.stack([ones, ones, ones] + [-p for p in parts] + [zero, zero], axis=-1)
        qry_f = jnp.stack(list(parts) + [ones, ones, ones, zero, zero], axis=-1)

        def to_slab(f):
            f = jnp.pad(f, ((0, 0), (0, 0), (0, 0), (0, HEAD_DIM - N_FEAT)))
            return jnp.roll(f, 1, axis=2).reshape(b, s, W_MIX)

        kf, qf = to_slab(key_f).reshape(b, nk, tk, W_MIX), to_slab(qry_f)
        f_rows = fcum.transpose(0, 2, 1)
        f_end = f_rows[:, :, tk - 1::tk]
        thr = jnp.pad(f_end - (qk_bound + FOX_STOP * LOG2E), ((0, 0), (0, 0), (0, LANES - nk)),
                      constant_values=BIG).reshape(b, N_HEADS, 1, LANES)
        args += [kf, qf, f_rows.reshape(b, N_HEADS, 1, s), thr]
        in_specs += [pl.BlockSpec((1, nk, tk, W_MIX), lambda a, i: (a, 0, 0, 0)),
                     pl.BlockSpec((1, tq, W_MIX), lambda a, i: (a, i, 0)),
                     pl.BlockSpec((1, N_HEADS, 1, tq), lambda a, i: (a, 0, 0, i)),
                     pl.BlockSpec((1, N_HEADS, 1, LANES), lambda a, i: (a, 0, 0, 0))]
    def call(call_args, call_specs, fixed_max):
        return pl.pallas_call(
            functools.partial(_causal_kernel, tq=tq, tk=tk, has_sel=sel is not None,
                              has_bias=fcum is not None, fixed_max=fixed_max),
            out_shape=jax.ShapeDtypeStruct((b, s, W_MIX), BF16),
            grid=(b, s // tq),
            in_specs=call_specs,
            out_specs=pl.BlockSpec((1, tq, W_MIX), lambda a, i: (a, i, 0)),
            scratch_shapes=[pltpu.VMEM((rows, W_MIX), BF16), pltpu.VMEM((W_MIX, tq), F32),
                            pltpu.VMEM((n_s, tk, tq), F32), pltpu.VMEM((n_s, tk, tq), F32),
                            pltpu.VMEM((N_HEADS, 1, tq), F32), pltpu.VMEM((N_HEADS, 1, tq), F32)],
            compiler_params=_params("arbitrary", "arbitrary"),
            name="causal_sel%d_bias%d_fixed%d" % (sel is not None, fcum is not None, fixed_max),
        )(*call_args)

    if sel_bound is None:
        return call(args, in_specs, False)
    shift = jnp.zeros((1, W_MIX), F32).at[0, SHIFT_LANE].set(-sel_bound)
    shift_spec = pl.BlockSpec((1, W_MIX), lambda a, i: (0, 0))
    return lax.cond(sel_bound <= FIXED_MAX_LIMIT,
                    lambda: call(args + [shift], in_specs + [shift_spec], True),
                    lambda: call(args, in_specs, False))


def _window_kernel(*refs, tq, wk, pad, window, ls, emit_lse, other_dils, kv_lanes, fixed_max):
    it = iter(refs)
    q_ref, k_ref, v_ref = next(it), next(it), next(it)
    n_other = len(other_dils)
    others_in = [(next(it), next(it)) for _ in range(n_other)]
    wexp_ref = next(it) if n_other else None
    shift_ref = next(it) if fixed_max else None
    o_ref = next(it)
    lse_ref = next(it) if emit_lse else None
    qm_scr, acc_scr = next(it), next(it)
    others = []
    if n_other:
        og_scr, lg_scr = next(it), next(it)
        for g, (dg, (og_ref, lg_ref)) in enumerate(zip(other_dils, others_in)):
            for r in range(dg):
                rows = pl.ds(r, tq // dg, stride=dg)
                for c in range(W_MIX // LANES):
                    lo = r * W_MIX + c * LANES
                    og_scr[g, c, rows, :] = og_ref[0, :, lo:lo + LANES]
                lg_scr[g, rows, :] = lg_ref[0, :, r * LANES:(r + 1) * LANES]
            others.append((og_scr.at[g], lg_scr.at[g]))
    t0 = pl.program_id(2) * tq
    start = pl.multiple_of(jnp.clip(t0 - pad, 0, ls - wk), LANES)
    if kv_lanes is None:
        _stack_masked(q_ref[0], qm_scr, tq)
    else:
        lane = lax.broadcasted_iota(jnp.int32, (1, W_MIX), 1)
        on_key = (lane >= kv_lanes[0]) & (lane < kv_lanes[0] + HEAD_DIM)
        qf = q_ref[0].astype(F32)
        for h in range(N_HEADS):
            shift = (kv_lanes[0] - h * HEAD_DIM) % W_MIX
            rot = pltpu.roll(qf, shift, 1) if shift else qf
            qm_scr[h * tq:(h + 1) * tq, :] = jnp.where(on_key, rot, 0.0).astype(BF16)
    k_w = k_ref[0, pl.ds(start, wk), :]
    vt_w = v_ref[0, pl.ds(start, wk), :].astype(F32).T.astype(BF16)
    ss = [_dot(k_w, qm_scr[h * tq:(h + 1) * tq, :], NT) for h in range(N_HEADS)]
    t_pos = t0 + lax.broadcasted_iota(jnp.int32, (1, tq), 1)
    s_pos = start + lax.broadcasted_iota(jnp.int32, (wk, 1), 0)
    mask = (s_pos <= t_pos) & (t_pos - s_pos < window)
    ps, inv_ls, lses = [], [], []
    for h in range(N_HEADS):
        s = jnp.where(mask, ss[h], NEG)
        m = shift_ref[:, 0:1] if fixed_max else jnp.max(s, axis=0, keepdims=True)
        p = jnp.exp2(s - m)
        l = jnp.sum(p, axis=0, keepdims=True)
        ps.append(p.astype(BF16))
        inv_ls.append(1.0 / l)
        lses.append(m + jnp.log2(l))
    for h in range(N_HEADS):
        rs = slice(h * HEAD_DIM, (h + 1) * HEAD_DIM)
        vs = rs if kv_lanes is None else slice(kv_lanes[1], kv_lanes[1] + HEAD_DIM)
        acc_scr[rs, :] = _dot(vt_w[vs, :], ps[h]) * inv_ls[h]
    o_self = acc_scr[...].T
    if emit_lse or n_other:
        row = lax.broadcasted_iota(jnp.int32, (LANES, 1), 0)
        stat = jnp.zeros((LANES, tq), F32)
        for h in range(N_HEADS):
            stat = jnp.where(row == h, lses[h], stat)
        lse_tile = stat.T
    if n_other:
        lg_t = [lg[...].T for (_, lg) in others]
        row = lax.broadcasted_iota(jnp.int32, (LANES, 1), 0)
        wmat = jnp.zeros((LANES, tq), F32)
        for h in range(N_HEADS):
            group_lse = [lses[h]] + [t[h:h + 1, :] for t in lg_t]
            top = functools.reduce(jnp.maximum, group_lse)
            ws = [jnp.exp2(x - top) for x in group_lse]
            inv_den = 1.0 / functools.reduce(jnp.add, ws)
            for g, w in enumerate(ws):
                wmat = jnp.where(row == g * N_HEADS + h, w * inv_den, wmat)
        wt = wmat.T
        groups = [o_self] + [jnp.concatenate([og[c] for c in range(W_MIX // LANES)], axis=1)
                             for (og, _) in others]
        out = None
        for g, o_g in enumerate(groups):
            term = _dot_split(wt, wexp_ref[g]) * o_g
            out = term if out is None else out + term
        o_ref[0] = out.astype(o_ref.dtype)
    else:
        o_ref[0] = o_self.astype(o_ref.dtype)
    if emit_lse:
        lse_ref[0] = lse_tile


def _window_attn(q, k, v, *, dil, window, tq, out_dtype, score_bound, emit_lse=False, others=(),
                 kv_lanes=None):
    b, ls, _ = q.shape
    tq = min(tq, ls)
    pad = -(-(window - 1) // LANES) * LANES
    wk = min(tq + pad, ls)
    rows = N_HEADS * tq
    args = [q, k, v]
    in_specs = [
        pl.BlockSpec((1, tq, W_MIX), lambda a, r, i: (a, i, r)),
        pl.BlockSpec((1, ls, W_MIX), lambda a, r, i: (a, 0, r)),
        pl.BlockSpec((1, ls, W_MIX), lambda a, r, i: (a, 0, r)),
    ]
    scratch = [pltpu.VMEM((rows, W_MIX), BF16), pltpu.VMEM((W_MIX, tq), F32)]
    for (o_g, lse_g, dg) in others:
        args += [o_g, lse_g]
        in_specs += [pl.BlockSpec((1, tq // dg, dg * W_MIX), lambda a, r, i: (a, i, 0)),
                     pl.BlockSpec((1, tq // dg, dg * LANES), lambda a, r, i: (a, i, 0))]
    if others:
        wexp = np.zeros((len(others) + 1, LANES, W_MIX), np.float32)
        for g in range(len(others) + 1):
            for h in range(N_HEADS):
                wexp[g, g * N_HEADS + h, h * HEAD_DIM:(h + 1) * HEAD_DIM] = 1.0
        args.append(jnp.asarray(wexp, BF16))
        in_specs.append(pl.BlockSpec(wexp.shape, lambda a, r, i: (0, 0, 0)))
        scratch += [pltpu.VMEM((len(others), W_MIX // LANES, tq, LANES), F32),
                    pltpu.VMEM((len(others), tq, LANES), F32)]
    out_shape = [jax.ShapeDtypeStruct((b, ls, dil * W_MIX), out_dtype)]
    out_specs = [pl.BlockSpec((1, tq, W_MIX), lambda a, r, i: (a, i, r))]
    if emit_lse:
        out_shape.append(jax.ShapeDtypeStruct((b, ls, dil * LANES), F32))
        out_specs.append(pl.BlockSpec((1, tq, LANES), lambda a, r, i: (a, i, r)))
    def call(call_args, call_specs, fixed_max):
        return pl.pallas_call(
            functools.partial(_window_kernel, tq=tq, wk=wk, pad=pad, window=window, ls=ls,
                              emit_lse=emit_lse, other_dils=tuple(dg for (_, _, dg) in others),
                              kv_lanes=kv_lanes, fixed_max=fixed_max),
            out_shape=out_shape,
            grid=(b, dil, ls // tq),
            in_specs=call_specs,
            out_specs=out_specs,
            scratch_shapes=scratch,
            compiler_params=_params("arbitrary", "arbitrary", "arbitrary"),
            name="window_d%d_w%d_fixed%d" % (dil, window, fixed_max),
        )(*call_args)

    shift = jnp.full((1, LANES), score_bound, F32)
    shift_spec = pl.BlockSpec((1, LANES), lambda a, r, i: (0, 0))
    res = lax.cond(score_bound <= FIXED_MAX_LIMIT,
                   lambda: call(args + [shift], in_specs + [shift_spec], True),
                   lambda: call(args, in_specs, False))
    return (res[0], res[1], dil) if emit_lse else res[0]


def _sb_kernel(q_ref, k_ref, vt_ref, tri_ref, o_ref, qm_scr, carry_scr, acc_scr, *, tq, tk):
    i = pl.program_id(1)
    t0 = i * tq
    _stack_masked(q_ref[0], qm_scr, tq)
    carry_scr[...] = jnp.zeros(carry_scr.shape, F32)
    acc_scr[...] = jnp.zeros(acc_scr.shape, F32)
    t_pos = t0 + lax.broadcasted_iota(jnp.int32, (1, tq), 1)

    def tile(kt, diag):
        k_t = k_ref[0, kt]
        vt_t = vt_ref[0, kt]
        tri = tri_ref[...]
        zs = [_dot(k_t, qm_scr[h * tq:(h + 1) * tq, :], NT) for h in range(N_HEADS)]
        if diag:
            s_pos = kt * tk + lax.broadcasted_iota(jnp.int32, (tk, 1), 0)
            strict = s_pos < t_pos
        weights, his, los = [], [], []
        for h in range(N_HEADS):
            z = zs[h]
            lg = -(jnp.maximum(z, 0.0) + jnp.log(1.0 + jnp.exp(-jnp.abs(z))))
            if diag:
                lg = jnp.where(strict, lg, 0.0)
            hi = lg.astype(BF16)
            his.append(hi)
            los.append((lg - hi.astype(F32)).astype(BF16))
        cums = [_dot(tri, his[h]) + _dot(tri, los[h]) for h in range(N_HEADS)]
        for h in range(N_HEADS):
            a = jnp.exp(zs[h] + cums[h] + carry_scr[h])
            if diag:
                a = jnp.where(strict, a, 0.0)
            weights.append(a.astype(BF16))
            carry_scr[h] += cums[h][0:1, :]
        for h in range(N_HEADS):
            rs = slice(h * HEAD_DIM, (h + 1) * HEAD_DIM)
            acc_scr[rs, :] += _dot(vt_t[rs, :], weights[h])

    n_diag = tq // tk
    for d in reversed(range(n_diag)):
        tile(i * n_diag + d, True)
    n_below = i * n_diag

    def cond(state):
        j, top = state
        return (j < n_below) & (top > SB_STOP)

    def body(state):
        j, _ = state
        tile(n_below - 1 - j, False)
        return j + 1, jnp.max(carry_scr[...])

    lax.while_loop(cond, body, (jnp.int32(0), jnp.max(carry_scr[...])))
    o_ref[0] = acc_scr[...].T.astype(o_ref.dtype)


def _stick_breaking(q, k, v, tq, tk):
    b, s, _ = q.shape
    assert tq % tk == 0
    nk = s // tk
    rows = N_HEADS * tq
    tri = jnp.asarray(np.triu(np.ones((tk, tk), np.float32)), BF16)
    vt = v.reshape(b, nk, tk, W_MIX).transpose(0, 1, 3, 2)
    return pl.pallas_call(
        functools.partial(_sb_kernel, tq=tq, tk=tk),
        out_shape=jax.ShapeDtypeStruct((b, s, W_MIX), BF16),
        grid=(b, s // tq),
        in_specs=[
            pl.BlockSpec((1, tq, W_MIX), lambda a, i: (a, i, 0)),
            pl.BlockSpec((1, nk, tk, W_MIX), lambda a, i: (a, 0, 0, 0)),
            pl.BlockSpec((1, nk, W_MIX, tk), lambda a, i: (a, 0, 0, 0)),
            pl.BlockSpec((tk, tk), lambda a, i: (0, 0)),
        ],
        out_specs=pl.BlockSpec((1, tq, W_MIX), lambda a, i: (a, i, 0)),
        scratch_shapes=[pltpu.VMEM((rows, W_MIX), BF16), pltpu.VMEM((N_HEADS, 1, tq), F32),
                        pltpu.VMEM((W_MIX, tq), F32)],
        compiler_params=_params("arbitrary", "arbitrary"),
        name="stick_breaking",
    )(q, k.reshape(b, nk, tk, W_MIX), vt, tri)


def _foxcum_kernel(x_ref, b_ref, tri_ref, o_ref, carry_scr):
    @pl.when(pl.program_id(1) == 0)
    def _():
        carry_scr[...] = jnp.zeros(carry_scr.shape, F32)

    z = x_ref[0] + b_ref[...]
    logf = jnp.minimum(z, 0.0) - jnp.log(1.0 + jnp.exp(-jnp.abs(z)))
    tri = tri_ref[...]
    hi = logf.astype(BF16)
    r1 = logf - hi.astype(F32)
    mid = r1.astype(BF16)
    lo = (r1 - mid.astype(F32)).astype(BF16)
    cum = _dot(tri, hi) + _dot(tri, mid) + _dot(tri, lo) + carry_scr[...]
    o_ref[0] = cum
    carry_scr[...] = cum[cum.shape[0] - 1:, :]


def _fox_cumsum(misc, bias_vec, tc):
    b, s, w = misc.shape
    tri = jnp.asarray(np.tril(np.ones((tc, tc), np.float32)), BF16)
    return pl.pallas_call(
        _foxcum_kernel,
        out_shape=jax.ShapeDtypeStruct((b, s, w), F32),
        grid=(b, s // tc),
        in_specs=[
            pl.BlockSpec((1, tc, w), lambda a, i: (a, i, 0)),
            pl.BlockSpec((1, w), lambda a, i: (0, 0)),
            pl.BlockSpec((tc, tc), lambda a, i: (0, 0)),
        ],
        out_specs=pl.BlockSpec((1, tc, w), lambda a, i: (a, i, 0)),
        scratch_shapes=[pltpu.VMEM((1, w), F32)],
        compiler_params=_params("arbitrary", "arbitrary"),
        name="fox_cumsum",
    )(misc, bias_vec, tri)


def _merge_kernel(x_ref, g_ref, sc_ref, sh_ref, ga_ref, wm_ref, misc_ref, pg_ref,
                  ocmp_ref, osel_ref, owin_ref, ob_ref, oc_ref, od_ref,
                  wa_ref, wb_ref, wc_ref, wd_ref, wo_ref, o_ref):
    x = x_ref[...]
    d = x.shape[1]
    h = _mod_norm(x, g_ref[...], sc_ref[0], sh_ref[0]).astype(BF16)
    gate = jax.nn.sigmoid(misc_ref[...])
    o_a = (_dot_split(gate, pg_ref[0]) * ocmp_ref[...].astype(F32)
           + _dot_split(gate, pg_ref[1]) * osel_ref[...].astype(F32)
           + _dot_split(gate, pg_ref[2]) * owin_ref[...].astype(F32)).astype(BF16)
    mixed = jnp.zeros(x.shape, F32)
    for m, (o_m, w_ref) in enumerate(((o_a, wa_ref), (ob_ref[...], wb_ref),
                                      (oc_ref[...], wc_ref), (od_ref[...], wd_ref))):
        y = _dot(o_m, w_ref[...])
        gl = _dot(h, wm_ref[:, m * d:(m + 1) * d])
        mixed = mixed + jax.nn.sigmoid(gl) * y
    o_ref[...] = x + ga_ref[0] * _dot(mixed.astype(BF16), wo_ref[...])


def _merge(x2, g, sc, sh, ga, w_merge, misc, pg, o_cmp, o_sel, o_win, o_b, o_c, o_d,
           wa, wb, wc, wd, wo, seq, tt):
    n, d = x2.shape
    tpb = seq // tt
    row = lambda w: pl.BlockSpec((tt, w), lambda i: (i, 0))
    full = lambda a: pl.BlockSpec(a.shape, lambda i: (0,) * a.ndim)
    per_b = pl.BlockSpec((1, 1, d), lambda i: (i // tpb, 0, 0))
    return pl.pallas_call(
        _merge_kernel,
        out_shape=jax.ShapeDtypeStruct((n, d), F32),
        grid=(n // tt,),
        in_specs=[row(d), full(g), per_b, per_b, per_b, full(w_merge), row(W_MIX), full(pg)]
        + [row(W_MIX)] * 6 + [full(wa), full(wb), full(wc), full(wd), full(wo)],
        out_specs=row(d),
        compiler_params=_params("arbitrary"),
        name="merge_out",
    )(x2, g, sc, sh, ga, w_merge, misc, pg, o_cmp, o_sel, o_win, o_b, o_c, o_d, wa, wb, wc, wd, wo)


def _ffn_kernel(x_ref, g_ref, sc_ref, sh_ref, gf_ref, w1_ref, w3_ref, w2_ref, o_ref, h_scr, acc_scr):
    f = pl.program_id(1)

    @pl.when(f == 0)
    def _():
        h_scr[...] = _mod_norm(x_ref[...], g_ref[...], sc_ref[0], sh_ref[0]).astype(BF16)
        acc_scr[...] = jnp.zeros(acc_scr.shape, F32)

    h = h_scr[...]
    a = _dot(h, w1_ref[...])
    b = _dot(h, w3_ref[...])
    acc_scr[...] += _dot((a * jax.nn.sigmoid(a) * b).astype(BF16), w2_ref[...])

    @pl.when(f == pl.num_programs(1) - 1)
    def _():
        o_ref[...] = x_ref[...] + gf_ref[0] * acc_scr[...]


def _ffn(x2, g, sc, sh, gf, w1, w3, w2, seq, tt, tf):
    n, d = x2.shape
    dff = w1.shape[1]
    tpb = seq // tt
    per_b = pl.BlockSpec((1, 1, d), lambda i, f: (i // tpb, 0, 0))
    return pl.pallas_call(
        _ffn_kernel,
        out_shape=jax.ShapeDtypeStruct((n, d), F32),
        grid=(n // tt, dff // tf),
        in_specs=[
            pl.BlockSpec((tt, d), lambda i, f: (i, 0)),
            pl.BlockSpec((1, d), lambda i, f: (0, 0)),
            per_b, per_b, per_b,
            pl.BlockSpec((d, tf), lambda i, f: (0, f)),
            pl.BlockSpec((d, tf), lambda i, f: (0, f)),
            pl.BlockSpec((tf, d), lambda i, f: (f, 0)),
        ],
        out_specs=pl.BlockSpec((tt, d), lambda i, f: (i, 0)),
        scratch_shapes=[pltpu.VMEM((tt, d), BF16), pltpu.VMEM((tt, d), F32)],
        compiler_params=_params("arbitrary", "arbitrary"),
        name="ffn_swiglu",
    )(x2, g, sc, sh, gf, w1, w3, w2)


def _route_kernel(x_ref, g_ref, sc_ref, sh_ref, rw_ref, up_ref, h_ref, rank_ref, gate_ref, cnt_ref):
    hf = _mod_norm(x_ref[...], g_ref[...], sc_ref[0], sh_ref[0])
    h_ref[...] = hf.astype(BF16)
    logits = _dot(rw_ref[...], hf, NT, HI)
    ne, tt = logits.shape
    e_idx = lax.broadcasted_iota(jnp.int32, (ne, 1), 0).astype(F32)
    v1 = jnp.max(logits, axis=0, keepdims=True)
    i1 = jnp.min(jnp.where(logits == v1, e_idx, float(ne)), axis=0, keepdims=True)
    m1 = e_idx == i1
    rest = jnp.where(m1, -3e38, logits)
    v2 = jnp.max(rest, axis=0, keepdims=True)
    i2 = jnp.min(jnp.where(rest == v2, e_idx, float(ne)), axis=0, keepdims=True)
    m2 = e_idx == i2
    e2 = jnp.exp(v2 - v1)
    g1 = 1.0 / (1.0 + e2)
    g2 = e2 / (1.0 + e2)
    routed = m1 | m2
    rf = jnp.where(routed, 1.0, 0.0)
    rank = _dot(rf.astype(BF16), up_ref[...])
    rank = jnp.where(routed, rank, -1.0)
    gate = jnp.where(m1, g1, 0.0) + jnp.where(m2, g2, 0.0)
    for e in range(ne):
        rank_ref[0, e] = rank[e:e + 1, :]
        gate_ref[0, e] = gate[e:e + 1, :]
    cnt = jnp.sum(rf, axis=1, keepdims=True)
    cnt_ref[0] = jnp.broadcast_to(cnt, (ne, LANES))


def _route(x2, g, sc, sh, rw_t, seq, tt):
    n, d = x2.shape
    ne = rw_t.shape[0]
    tpb = seq // tt
    nt = n // tt
    upper = jnp.asarray(np.triu(np.ones((tt, tt), np.float32), 1), BF16)
    per_b = pl.BlockSpec((1, 1, d), lambda i: (i // tpb, 0, 0))
    return pl.pallas_call(
        _route_kernel,
        out_shape=[jax.ShapeDtypeStruct((n, d), BF16),
                   jax.ShapeDtypeStruct((nt, ne, 1, tt), F32),
                   jax.ShapeDtypeStruct((nt, ne, 1, tt), F32),
                   jax.ShapeDtypeStruct((nt, ne, LANES), F32)],
        grid=(nt,),
        in_specs=[
            pl.BlockSpec((tt, d), lambda i: (i, 0)),
            pl.BlockSpec((1, d), lambda i: (0, 0)),
            per_b, per_b,
            pl.BlockSpec((ne, d), lambda i: (0, 0)),
            pl.BlockSpec((tt, tt), lambda i: (0, 0)),
        ],
        out_specs=[pl.BlockSpec((tt, d), lambda i: (i, 0)),
                   pl.BlockSpec((1, ne, 1, tt), lambda i: (i, 0, 0, 0)),
                   pl.BlockSpec((1, ne, 1, tt), lambda i: (i, 0, 0, 0)),
                   pl.BlockSpec((1, ne, LANES), lambda i: (i, 0, 0))],
        compiler_params=_params("arbitrary"),
        name="moe_route",
    )(x2, g, sc, sh, rw_t, upper)


def _moe_kernel(cnt_ref, x_ref, gf_ref, h_ref, rank_ref, gate_ref, w1_ref, w3_ref, w2_ref,
                o_ref, acc_scr, xs_scr, y_scr, *, chunk):
    i, e, f = pl.program_id(0), pl.program_id(1), pl.program_id(2)
    ne, nf = pl.num_programs(1), pl.num_programs(2)

    @pl.when((e == 0) & (f == 0))
    def _():
        acc_scr[...] = jnp.zeros(acc_scr.shape, F32)

    count = cnt_ref[i * ne + e]
    rank = rank_ref[0, 0]
    gate = gate_ref[0, 0]
    n_small = (count + chunk - 1) // chunk
    n_big = (count + 2 * chunk - 1) // (2 * chunk)

    def one_hot(c, rows):
        r = c * rows + lax.broadcasted_iota(jnp.int32, (rows, 1), 0)
        return rank == r.astype(F32)

    def rows_of(c, rows):
        return pl.ds(pl.multiple_of(c * rows, rows), rows)

    @pl.when(f == 0)
    def _():
        h = h_ref[...]

        def gather(c, carry):
            p = jnp.where(one_hot(c, chunk), 1.0, 0.0).astype(BF16)
            xs_scr[rows_of(c, chunk), :] = _dot(p, h).astype(BF16)
            return carry

        lax.fori_loop(0, n_small, gather, 0)

        def clear(c, carry):
            y_scr[rows_of(c, 2 * chunk), :] = jnp.zeros((2 * chunk, y_scr.shape[1]), F32)
            return carry

        lax.fori_loop(0, n_big, clear, 0)

    def expert(c, carry):
        xs = xs_scr[rows_of(c, chunk), :]
        a = _dot(xs, w1_ref[0])
        b = _dot(xs, w3_ref[0])
        y_scr[rows_of(c, chunk), :] += _dot((a * jax.nn.sigmoid(a) * b).astype(BF16), w2_ref[0])
        return carry

    lax.fori_loop(0, n_small, expert, 0)

    @pl.when(f == nf - 1)
    def _():
        def scatter(c, carry):
            hit = one_hot(c, 2 * chunk)
            p = jnp.where(hit, 1.0, 0.0).astype(BF16)
            gcol = jnp.sum(jnp.where(hit, gate, 0.0), axis=-1, keepdims=True)
            acc_scr[...] += _dot(p, (y_scr[rows_of(c, 2 * chunk), :] * gcol).astype(BF16), TN)
            return carry

        lax.fori_loop(0, n_big, scatter, 0)

    @pl.when((e == ne - 1) & (f == nf - 1))
    def _():
        o_ref[...] = x_ref[...] + gf_ref[0] * acc_scr[...]


def _moe(counts, x2, gf, h2, rank, gate, w1, w3, w2, seq, tt, tf, chunk):
    n, d = x2.shape
    ne, _, dff = w1.shape
    tpb = seq // tt
    grid_spec = pltpu.PrefetchScalarGridSpec(
        num_scalar_prefetch=1,
        grid=(n // tt, ne, dff // tf),
        in_specs=[
            pl.BlockSpec((tt, d), lambda i, e, f, c: (i, 0)),
            pl.BlockSpec((1, 1, d), lambda i, e, f, c: (i // tpb, 0, 0)),
            pl.BlockSpec((tt, d), lambda i, e, f, c: (i, 0)),
            pl.BlockSpec((1, 1, 1, tt), lambda i, e, f, c: (i, e, 0, 0)),
            pl.BlockSpec((1, 1, 1, tt), lambda i, e, f, c: (i, e, 0, 0)),
            pl.BlockSpec((1, d, tf), lambda i, e, f, c: (e, 0, f)),
            pl.BlockSpec((1, d, tf), lambda i, e, f, c: (e, 0, f)),
            pl.BlockSpec((1, tf, d), lambda i, e, f, c: (e, f, 0)),
        ],
        out_specs=pl.BlockSpec((tt, d), lambda i, e, f, c: (i, 0)),
        scratch_shapes=[pltpu.VMEM((tt, d), F32), pltpu.VMEM((tt, d), BF16), pltpu.VMEM((tt, d), F32)],
    )
    return pl.pallas_call(
        functools.partial(_moe_kernel, chunk=chunk),
        out_shape=jax.ShapeDtypeStruct((n, d), F32),
        grid_spec=grid_spec,
        compiler_params=_params("arbitrary", "arbitrary", "arbitrary"),
        name="moe_experts",
    )(counts, x2, gf, h2, rank, gate, w1, w3, w2)


def _overlap_matrix(ncp, nc, n_sel):
    c0 = np.arange(ncp) * CMP_STRIDE
    c1 = c0 + CMP_LEN
    s0 = np.arange(n_sel) * SEL_LEN
    s1 = s0 + SEL_LEN
    ov = ((c0[:, None] < s1[None, :]) & (c1[:, None] > s0[None, :])).astype(np.float32)
    ov[nc:] = 0.0
    return jnp.asarray(ov.T)


def _gate_expand():
    pg = np.zeros((3, W_MIX, W_MIX), np.float32)
    for br in range(3):
        for h in range(N_HEADS):
            pg[br, GATE_LANE + 3 * h + br, h * HEAD_DIM:(h + 1) * HEAD_DIM] = 1.0
    return jnp.asarray(pg, BF16)


def _mixer_layer(x2, b, s, mod, norm_g, rope, w_in, qk_gain, pe_k, pe_v, ck1, ck2, cv1, cv2,
                 fox_b, w_branch, w_out):
    n, d = x2.shape
    sh_a, sc_a, g_a = mod[0], mod[1], mod[2]
    w_slab, gain, w_merge = _pack_w_in(w_in, qk_gain)
    bd = jnp.asarray(np.kron(np.eye(N_HEADS), np.full((HEAD_DIM, HEAD_DIM), 1.0 / HEAD_DIM)), BF16)
    tiles = _tiles(s)
    outs = _proj(x2, norm_g, sc_a, sh_a, w_slab, gain, bd, rope, s, tiles.proj)
    sl = [a.reshape(b, s, W_MIX) for a in outs[:N_SLABS]]
    dil_in = {1: (sl[S_BQ], sl[S_BK], sl[S_BV])}
    for di, dil in enumerate(_DILATIONS):
        dil_in[dil] = tuple(outs[N_SLABS + si * len(_DILATIONS) + di].reshape(b, s // dil, dil * W_MIX)
                            for si in range(len(_DIL_SLABS)))
    misc = sl[S_MISC]

    nch = s // CMP_STRIDE
    nc = nch - CMP_LEN // CMP_STRIDE + 1
    kc_raw = sl[S_KC][..., :HEAD_DIM]
    vc_raw = misc[..., VC_LANE:VC_LANE + HEAD_DIM]
    chunks = jnp.stack([kc_raw, vc_raw]).reshape(2, b, nch, CMP_STRIDE * HEAD_DIM)
    chunks_next = jnp.concatenate([chunks[:, :, 1:], jnp.zeros_like(chunks[:, :, :1])], axis=2)
    pe = jnp.stack([pe_k, pe_v]).reshape(2, 1, CMP_LEN * HEAD_DIM)
    kvc = jnp.tile(_compress(chunks, chunks_next, pe, jnp.stack([ck1, cv1]), jnp.stack([ck2, cv2])),
                   (1, 1, 1, N_HEADS))
    overlap = _overlap_matrix(nch, nc, s // SEL_LEN)
    o_cmp, selmask = _cmp_topk(sl[S_QNR], kvc[0], kvc[1].transpose(0, 2, 1).astype(BF16), overlap,
                               tiles.causal_q, nc)
    tq, tk = tiles.causal_q, tiles.causal_k
    def score_bound(gq, gk):
        return 1.02 * LOG2E * HEAD_DIM ** 0.5 * jnp.max(jnp.abs(qk_gain[gq])) * jnp.max(jnp.abs(qk_gain[gk])) + 0.1

    o_sel = _causal_attn(sl[S_QR], sl[S_KV], sl[S_KV], tq, tk, sel=selmask, sel_bound=score_bound(0, 2))
    o_win = _window_attn(sl[S_QR], sl[S_KV], sl[S_KV], dil=1, window=NSA_WINDOW, tq=tiles.window_q,
                         out_dtype=BF16, score_bound=score_bound(0, 3), kv_lanes=(KW_LANE, VW_LANE))

    others = []
    for (wdw, dil) in DIL_CONFIGS[:0:-1]:
        others.append(_window_attn(*dil_in[dil], dil=dil, window=wdw // dil + 1,
                                   tq=s // dil if s // dil <= tiles.short_seq else tiles.window_q,
                                   out_dtype=F32, score_bound=score_bound(4, 5), emit_lse=True))
    wdw, dil = DIL_CONFIGS[0]
    o_b = _window_attn(*dil_in[dil], dil=dil, window=wdw // dil + 1, tq=tiles.window_q, out_dtype=BF16,
                       score_bound=score_bound(4, 5), others=others)

    o_c = _stick_breaking(sl[S_CQ], sl[S_CK], sl[S_CV], tiles.sb, tiles.sb)

    bias_vec = jnp.zeros((1, W_MIX), F32).at[0, FOX_LANE:FOX_LANE + N_HEADS].set(fox_b)
    fcum = _fox_cumsum(misc, bias_vec, tiles.cum)
    qk_bound = 1.02 * HEAD_DIM ** 0.5 * jnp.max(jnp.abs(qk_gain[6])) * jnp.max(jnp.abs(qk_gain[7])) + 0.05
    o_d = _causal_attn(sl[S_DQ], sl[S_DK], sl[S_DV], tq, tk, fcum=fcum[..., FOX_LANE:FOX_LANE + N_HEADS],
                       qk_bound=qk_bound)

    wb16 = w_branch.astype(BF16)
    flat = lambda a: a.reshape(n, W_MIX)
    return _merge(x2, norm_g, sc_a, sh_a, g_a, w_merge, flat(misc), _gate_expand(),
                  flat(o_cmp), flat(o_sel), flat(o_win), flat(o_b), flat(o_c), flat(o_d),
                  wb16[0], wb16[1], wb16[2], wb16[3], w_out.astype(BF16), s, tiles.proj)


def kernel(x, c, positions, w_ada, b_ada, norm_mix, norm_ffn, w_in, qk_gain, nsa_pe_k, nsa_pe_v,
           nsa_ck_w1, nsa_ck_w2, nsa_cv_w1, nsa_cv_w2, fox_bias, w_branch, w_out,
           ffn_w1, ffn_w3, ffn_w2, router_w, moe_w1, moe_w3, moe_w2):
    b, s, d = x.shape
    depth = w_ada.shape[0]
    rope = _rope_tables(positions)
    mods = _ada(c, w_ada, b_ada).reshape(depth, b, 6, 1, d).transpose(0, 2, 1, 3, 4)
    x2 = x.reshape(b * s, d)
    for l in range(depth):
        mod = mods[l]
        x2 = _mixer_layer(x2, b, s, mod[0:3], norm_mix[l].reshape(1, d), rope, w_in[l], qk_gain[l],
                          nsa_pe_k[l], nsa_pe_v[l], nsa_ck_w1[l], nsa_ck_w2[l], nsa_cv_w1[l],
                          nsa_cv_w2[l], fox_bias[l], w_branch[l], w_out[l])
        sh_f, sc_f, g_f = mod[3], mod[4], mod[5]
        gn = norm_ffn[l].reshape(1, d)
        e = l // 2
        tiles = _tiles(s)
        if l % 2 == 0:
            dff = ffn_w1.shape[2]
            x2 = _ffn(x2, gn, sc_f, sh_f, g_f, ffn_w1[e].astype(BF16), ffn_w3[e].astype(BF16),
                      ffn_w2[e].astype(BF16), s, tiles.ffn, dff // 2)
        else:
            dff = moe_w1.shape[3]
            h2, rank, gate, cnt = _route(x2, gn, sc_f, sh_f, router_w[e].T, s, tiles.ffn)
            counts = cnt[:, :, 0].astype(jnp.int32).reshape(-1)
            x2 = _moe(counts, x2, g_f, h2, rank, gate, moe_w1[e].astype(BF16), moe_w3[e].astype(BF16),
                      moe_w2[e].astype(BF16), s, tiles.ffn, dff // 2, tiles.moe_chunk)
    return x2.reshape(b, s, d)
```
